```python
import jax, jax.numpy as jnp
from jax import lax
import numpy as np

D_MODEL = 1024
BATCH = 8
SEQ = 4096
DEPTH = 4

HEAD_DIM = 64
HEADS_PER_GROUP = 8
DILATION_GROUPS = ((128, 1), (512, 4), (2048, 16))
N_GROUPS = len(DILATION_GROUPS)
N_ATTN_HEADS = N_GROUPS * HEADS_PER_GROUP
QKV_WIDTH = N_ATTN_HEADS * HEAD_DIM
ATTN_OUT = HEADS_PER_GROUP * HEAD_DIM
CONV_WIDTH = D_MODEL
CONV_K = 3
D_FF = ((8 * D_MODEL // 3) + 127) // 128 * 128
IN_WIDTH = 3 * QKV_WIDTH + 3 * CONV_WIDTH + 2 * D_MODEL
NUM_BUCKETS = 32
MAX_DISTANCE = 2048
BLOCK = 128
N_SUB = 3
EPS = 1e-6
NEG_INF = -1e30

kernel_name = "hybrid_macaron_conv_dilated_attn"


def _t5_bucket(dist):
    exact = NUM_BUCKETS // 2
    d = np.maximum(dist, 1).astype(np.float32)
    large = exact + (np.log(d / exact) / np.log(MAX_DISTANCE / exact) * (NUM_BUCKETS - exact)).astype(np.int32)
    large = np.minimum(large, NUM_BUCKETS - 1)
    return np.where(dist < exact, dist, large).astype(np.int32)


def _rmsnorm(x, g):
    xf = x.astype(jnp.float32)
    y = xf * lax.rsqrt(jnp.mean(xf * xf, axis=-1, keepdims=True) + EPS) * g.astype(jnp.float32)
    return y.astype(x.dtype)


def _swiglu(h, w_gate, w_up, w_down):
    return (jax.nn.silu(h @ w_gate) * (h @ w_up)) @ w_down


def _causal_dwconv(u, w):
    rhs = w.astype(u.dtype).reshape(CONV_K, 1, u.shape[-1])
    return lax.conv_general_dilated(u, rhs, window_strides=(1,), padding=[(CONV_K - 1, 0)],
                                    dimension_numbers=("NWC", "WIO", "NWC"),
                                    feature_group_count=u.shape[-1])


def _dilated_window_attention(q, k, v, bias_tab, window, dilation):
    B, S, H, E = q.shape
    span = window // dilation
    L = S // dilation
    nb = -(-L // BLOCK)
    Lp = nb * BLOCK

    def to_sub(t):
        t = t.reshape(B, L, dilation, H, E)
        return jnp.pad(t, ((0, 0), (0, Lp - L), (0, 0), (0, 0), (0, 0)))

    qs, ks, vs = to_sub(q), to_sub(k), to_sub(v)
    qb = qs.reshape(B, nb, BLOCK, dilation, H, E)

    def key_blocks(t):
        tp = jnp.pad(t, ((0, 0), (BLOCK, 0), (0, 0), (0, 0), (0, 0)))
        prev = tp[:, :Lp].reshape(B, nb, BLOCK, dilation, H, E)
        cur = t.reshape(B, nb, BLOCK, dilation, H, E)
        return jnp.concatenate([prev, cur], axis=2)

    kb, vb = key_blocks(ks), key_blocks(vs)

    i = np.arange(BLOCK)[:, None]
    j = np.arange(2 * BLOCK)[None, :]
    rel = i - j + BLOCK
    band = (rel >= 0) & (rel <= span)
    first_ok = (np.arange(nb)[:, None, None] > 0) | (j[None] >= BLOCK)
    mask = jnp.asarray(band[None] & first_ok)[None, :, None, None]
    bucket = jnp.asarray(_t5_bucket(np.maximum(rel, 0) * dilation))
    bias = jnp.transpose(bias_tab[bucket].astype(jnp.float32), (2, 0, 1))

    logits = jnp.einsum("bnqrhe,bnkrhe->bnrhqk", qb, kb).astype(jnp.float32) * (HEAD_DIM ** -0.5)
    logits = jnp.where(mask, logits + bias, NEG_INF)
    m = jnp.max(logits, axis=-1, keepdims=True)
    p = jnp.exp(logits - m)
    s = jnp.sum(p, axis=-1)
    o = jnp.einsum("bnrhqk,bnkrhe->bnqrhe", p, vb.astype(jnp.float32))
    s_q = jnp.transpose(s, (0, 1, 4, 2, 3))
    o = o / s_q[..., None]
    lse = jnp.transpose(m[..., 0] + jnp.log(s), (0, 1, 4, 2, 3))
    o = o.reshape(B, Lp, dilation, H, E)[:, :L].reshape(B, S, H, E)
    lse = lse.reshape(B, Lp, dilation, H)[:, :L].reshape(B, S, H)
    return o, lse


def _mixer(h, w_in, conv_w, w_conv_out, w_attn_out, w_o, rel_bias):
    B, S, _ = h.shape
    u = h @ w_in
    splits = np.cumsum([QKV_WIDTH, QKV_WIDTH, QKV_WIDTH, CONV_WIDTH, CONV_WIDTH, CONV_WIDTH, D_MODEL])
    q, k, v, cb, cc, ch, g_conv, g_attn = jnp.split(u, splits, axis=-1)

    y_conv = (cb * _causal_dwconv(cc * ch, conv_w)) @ w_conv_out

    q = q.reshape(B, S, N_GROUPS, HEADS_PER_GROUP, HEAD_DIM)
    k = k.reshape(B, S, N_GROUPS, HEADS_PER_GROUP, HEAD_DIM)
    v = v.reshape(B, S, N_GROUPS, HEADS_PER_GROUP, HEAD_DIM)
    outs, lses = [], []
    for g, (window, dilation) in enumerate(DILATION_GROUPS):
        tab = rel_bias[:, g * HEADS_PER_GROUP:(g + 1) * HEADS_PER_GROUP]
        o_g, lse_g = _dilated_window_attention(q[:, :, g], k[:, :, g], v[:, :, g], tab, window, dilation)
        outs.append(o_g)
        lses.append(lse_g)
    alpha = jax.nn.softmax(jnp.stack(lses, axis=0), axis=0)
    o = jnp.sum(alpha[..., None] * jnp.stack(outs, axis=0), axis=0)
    y_attn = o.reshape(B, S, ATTN_OUT).astype(h.dtype) @ w_attn_out

    merged = jax.nn.sigmoid(g_conv) * y_conv + jax.nn.sigmoid(g_attn) * y_attn
    return merged @ w_o


def _fwd_setup_inputs(seed: int = 0) -> dict:
    key = jax.random.key(seed)
    ks = jax.random.split(key, 16)
    f32 = jnp.float32

    def nrm(k, shape, fan_in):
        return jax.random.normal(k, shape, f32) * (fan_in ** -0.5)

    return {
        "x": jax.random.normal(ks[0], (BATCH, SEQ, D_MODEL), f32),
        "c": jax.random.normal(ks[1], (BATCH, D_MODEL), f32),
        "ada_w": nrm(ks[2], (DEPTH, D_MODEL, N_SUB * 3 * D_MODEL), D_MODEL),
        "ada_b": 0.02 * jax.random.normal(ks[3], (DEPTH, N_SUB * 3 * D_MODEL), f32),
        "norm_g": 1.0 + 0.05 * jax.random.normal(ks[4], (DEPTH, N_SUB, D_MODEL), f32),
        "ffn_w_gate": nrm(ks[5], (DEPTH, 2, D_MODEL, D_FF), D_MODEL),
        "ffn_w_up": nrm(ks[6], (DEPTH, 2, D_MODEL, D_FF), D_MODEL),
        "ffn_w_down": nrm(ks[7], (DEPTH, 2, D_FF, D_MODEL), D_FF),
        "w_in": nrm(ks[8], (DEPTH, D_MODEL, IN_WIDTH), D_MODEL),
        "conv_w": nrm(ks[9], (DEPTH, CONV_K, CONV_WIDTH), CONV_K),
        "w_conv_out": nrm(ks[10], (DEPTH, CONV_WIDTH, D_MODEL), CONV_WIDTH),
        "w_attn_out": nrm(ks[11], (DEPTH, ATTN_OUT, D_MODEL), ATTN_OUT),
        "w_o": nrm(ks[12], (DEPTH, D_MODEL, D_MODEL), D_MODEL),
        "rel_bias": 0.5 * jax.random.normal(ks[13], (NUM_BUCKETS, N_ATTN_HEADS), f32),
        "final_g": 1.0 + 0.05 * jax.random.normal(ks[14], (D_MODEL,), f32),
    }


def _fwd_reference(x, c, ada_w, ada_b, norm_g, ffn_w_gate, ffn_w_up, ffn_w_down, w_in, conv_w,
              w_conv_out, w_attn_out, w_o, rel_bias, final_g):
    cs = jax.nn.silu(c)
    B = c.shape[0]
    for l in range(DEPTH):
        mod = (cs @ ada_w[l] + ada_b[l]).reshape(B, N_SUB, 3, D_MODEL)[:, :, :, None, :]
        h = _rmsnorm(x, norm_g[l, 0]) * (1.0 + mod[:, 0, 1]) + mod[:, 0, 0]
        x = x + 0.5 * mod[:, 0, 2] * _swiglu(h, ffn_w_gate[l, 0], ffn_w_up[l, 0], ffn_w_down[l, 0])
        h = _rmsnorm(x, norm_g[l, 1]) * (1.0 + mod[:, 1, 1]) + mod[:, 1, 0]
        x = x + mod[:, 1, 2] * _mixer(h, w_in[l], conv_w[l], w_conv_out[l], w_attn_out[l], w_o[l], rel_bias)
        h = _rmsnorm(x, norm_g[l, 2]) * (1.0 + mod[:, 2, 1]) + mod[:, 2, 0]
        x = x + 0.5 * mod[:, 2, 2] * _swiglu(h, ffn_w_gate[l, 1], ffn_w_up[l, 1], ffn_w_down[l, 1])
    return _rmsnorm(x, final_g)


import jax as _jax
import jax.numpy as _jnp

TWIN_FORMAT = 'train_step'
FWD_PARAMS = ['x', 'c', 'ada_w', 'ada_b', 'norm_g', 'ffn_w_gate', 'ffn_w_up', 'ffn_w_down', 'w_in', 'conv_w', 'w_conv_out', 'w_attn_out', 'w_o', 'rel_bias', 'final_g']
TWIN_WEIGHTS = ['ada_w', 'ada_b', 'norm_g', 'ffn_w_gate', 'ffn_w_up', 'ffn_w_down', 'w_in', 'conv_w', 'w_conv_out', 'w_attn_out', 'w_o', 'rel_bias', 'final_g']
TWIN_DIFF_INPUT = 'x'
TWIN_INPUTS = ['x', 'c', 'ada_w', 'ada_b', 'norm_g', 'ffn_w_gate', 'ffn_w_up', 'ffn_w_down', 'w_in', 'conv_w', 'w_conv_out', 'w_attn_out', 'w_o', 'rel_bias', 'final_g', 'loss_target', 'm_ada_w', 'm_ada_b', 'm_norm_g', 'm_ffn_w_gate', 'm_ffn_w_up', 'm_ffn_w_down', 'm_w_in', 'm_conv_w', 'm_w_conv_out', 'm_w_attn_out', 'm_w_o', 'm_rel_bias', 'm_final_g', 'v_ada_w', 'v_ada_b', 'v_norm_g', 'v_ffn_w_gate', 'v_ffn_w_up', 'v_ffn_w_down', 'v_w_in', 'v_conv_w', 'v_w_conv_out', 'v_w_attn_out', 'v_w_o', 'v_rel_bias', 'v_final_g']
TWIN_OUTPUTS = ['loss', 'grad_x', 'grad_ada_w', 'grad_ada_b', 'grad_norm_g', 'grad_ffn_w_gate', 'grad_ffn_w_up', 'grad_ffn_w_down', 'grad_w_in', 'grad_conv_w', 'grad_w_conv_out', 'grad_w_attn_out', 'grad_w_o', 'grad_rel_bias', 'grad_final_g', 'delta_ada_w', 'delta_ada_b', 'delta_norm_g', 'delta_ffn_w_gate', 'delta_ffn_w_up', 'delta_ffn_w_down', 'delta_w_in', 'delta_conv_w', 'delta_w_conv_out', 'delta_w_attn_out', 'delta_w_o', 'delta_rel_bias', 'delta_final_g', 'new_m_ada_w', 'new_m_ada_b', 'new_m_norm_g', 'new_m_ffn_w_gate', 'new_m_ffn_w_up', 'new_m_ffn_w_down', 'new_m_w_in', 'new_m_conv_w', 'new_m_w_conv_out', 'new_m_w_attn_out', 'new_m_w_o', 'new_m_rel_bias', 'new_m_final_g', 'new_v_ada_w', 'new_v_ada_b', 'new_v_norm_g', 'new_v_ffn_w_gate', 'new_v_ffn_w_up', 'new_v_ffn_w_down', 'new_v_w_in', 'new_v_conv_w', 'new_v_w_conv_out', 'new_v_w_attn_out', 'new_v_w_o', 'new_v_rel_bias', 'new_v_final_g']
TWIN_LEAF_KINDS = {'loss': 'loss', 'grad_x': 'grad_x', 'grad_ada_w': 'grad_w', 'grad_ada_b': 'grad_w', 'grad_norm_g': 'grad_w', 'grad_ffn_w_gate': 'grad_w', 'grad_ffn_w_up': 'grad_w', 'grad_ffn_w_down': 'grad_w', 'grad_w_in': 'grad_w', 'grad_conv_w': 'grad_w', 'grad_w_conv_out': 'grad_w', 'grad_w_attn_out': 'grad_w', 'grad_w_o': 'grad_w', 'grad_rel_bias': 'grad_w', 'grad_final_g': 'grad_w', 'delta_ada_w': 'delta_w', 'delta_ada_b': 'delta_w', 'delta_norm_g': 'delta_w', 'delta_ffn_w_gate': 'delta_w', 'delta_ffn_w_up': 'delta_w', 'delta_ffn_w_down': 'delta_w', 'delta_w_in': 'delta_w', 'delta_conv_w': 'delta_w', 'delta_w_conv_out': 'delta_w', 'delta_w_attn_out': 'delta_w', 'delta_w_o': 'delta_w', 'delta_rel_bias': 'delta_w', 'delta_final_g': 'delta_w', 'new_m_ada_w': 'new_m', 'new_m_ada_b': 'new_m', 'new_m_norm_g': 'new_m', 'new_m_ffn_w_gate': 'new_m', 'new_m_ffn_w_up': 'new_m', 'new_m_ffn_w_down': 'new_m', 'new_m_w_in': 'new_m', 'new_m_conv_w': 'new_m', 'new_m_w_conv_out': 'new_m', 'new_m_w_attn_out': 'new_m', 'new_m_w_o': 'new_m', 'new_m_rel_bias': 'new_m', 'new_m_final_g': 'new_m', 'new_v_ada_w': 'new_v', 'new_v_ada_b': 'new_v', 'new_v_norm_g': 'new_v', 'new_v_ffn_w_gate': 'new_v', 'new_v_ffn_w_up': 'new_v', 'new_v_ffn_w_down': 'new_v', 'new_v_w_in': 'new_v', 'new_v_conv_w': 'new_v', 'new_v_w_conv_out': 'new_v', 'new_v_w_attn_out': 'new_v', 'new_v_w_o': 'new_v', 'new_v_rel_bias': 'new_v', 'new_v_final_g': 'new_v'}


def _forward(args):
    return _fwd_reference(*[args[k] for k in FWD_PARAMS])


def _output_shape():
    out = _jax.eval_shape(lambda: _forward(_fwd_setup_inputs(0)))
    return out.shape, out.dtype

N_MICROBATCH = 1
ADAM_LR = 0.001
ADAM_B1 = 0.9
ADAM_B2 = 0.999
ADAM_EPS = 1e-08
ADAM_WD = 0.01
ADAM_STEP = 10
PER_EXAMPLE_BATCH_AXIS = {'x': 0, 'c': 0, 'loss_target': 0}
SHARED_INPUTS = []
_WEIGHT_DTYPES = {'ada_w': _jnp.float32, 'ada_b': _jnp.float32, 'norm_g': _jnp.float32, 'ffn_w_gate': _jnp.float32, 'ffn_w_up': _jnp.float32, 'ffn_w_down': _jnp.float32, 'w_in': _jnp.float32, 'conv_w': _jnp.float32, 'w_conv_out': _jnp.float32, 'w_attn_out': _jnp.float32, 'w_o': _jnp.float32, 'rel_bias': _jnp.float32, 'final_g': _jnp.float32}
MOMENT_SCALE = {'ada_w': 8.726596e-02, 'ada_b': 1.467939e-01, 'norm_g': 1.311841e-01, 'ffn_w_gate': 3.853652e-02, 'ffn_w_up': 3.737422e-02, 'ffn_w_down': 6.193263e-02, 'w_in': 7.028382e-02, 'conv_w': 1.217116e-01, 'w_conv_out': 1.185086e-01, 'w_attn_out': 3.103495e-02, 'w_o': 1.209530e-01, 'rel_bias': 3.182652e-02, 'final_g': 3.276277e+01}


def _to_microbatches(a, axis):
    t = _jnp.moveaxis(a, axis, 0)
    t = t.reshape((N_MICROBATCH, t.shape[0] // N_MICROBATCH) + t.shape[1:])
    return _jnp.moveaxis(t, 1, axis + 1)


def setup_inputs(seed: int = 0) -> dict:
    inp = _fwd_setup_inputs(seed)
    key = _jax.random.fold_in(_jax.random.key(seed), 7919)
    shape, _ = _output_shape()
    out = dict(inp)
    out["loss_target"] = _jax.random.normal(_jax.random.fold_in(key, 0), shape, _jnp.float32)
    for i, name in enumerate(TWIN_WEIGHTS):
        w = inp[name].astype(_jnp.float32)
        if MOMENT_SCALE is None:
            s = _jnp.sqrt(_jnp.mean(_jnp.square(w)) + 1e-30)
        else:
            s = MOMENT_SCALE[name]
        km, kv = _jax.random.split(_jax.random.fold_in(key, i + 1))
        out[name] = w
        out["m_" + name] = s * _jax.random.normal(km, w.shape, _jnp.float32)
        out["v_" + name] = (s * s) * _jax.random.uniform(kv, w.shape, _jnp.float32, 0.5, 1.5)
    if N_MICROBATCH > 1:
        for name, axis in PER_EXAMPLE_BATCH_AXIS.items():
            out[name] = _to_microbatches(out[name], axis)
    return {'x': out['x'], 'c': out['c'], 'ada_w': out['ada_w'], 'ada_b': out['ada_b'], 'norm_g': out['norm_g'], 'ffn_w_gate': out['ffn_w_gate'], 'ffn_w_up': out['ffn_w_up'], 'ffn_w_down': out['ffn_w_down'], 'w_in': out['w_in'], 'conv_w': out['conv_w'], 'w_conv_out': out['w_conv_out'], 'w_attn_out': out['w_attn_out'], 'w_o': out['w_o'], 'rel_bias': out['rel_bias'], 'final_g': out['final_g'], 'loss_target': out['loss_target'], 'm_ada_w': out['m_ada_w'], 'm_ada_b': out['m_ada_b'], 'm_norm_g': out['m_norm_g'], 'm_ffn_w_gate': out['m_ffn_w_gate'], 'm_ffn_w_up': out['m_ffn_w_up'], 'm_ffn_w_down': out['m_ffn_w_down'], 'm_w_in': out['m_w_in'], 'm_conv_w': out['m_conv_w'], 'm_w_conv_out': out['m_w_conv_out'], 'm_w_attn_out': out['m_w_attn_out'], 'm_w_o': out['m_w_o'], 'm_rel_bias': out['m_rel_bias'], 'm_final_g': out['m_final_g'], 'v_ada_w': out['v_ada_w'], 'v_ada_b': out['v_ada_b'], 'v_norm_g': out['v_norm_g'], 'v_ffn_w_gate': out['v_ffn_w_gate'], 'v_ffn_w_up': out['v_ffn_w_up'], 'v_ffn_w_down': out['v_ffn_w_down'], 'v_w_in': out['v_w_in'], 'v_conv_w': out['v_conv_w'], 'v_w_conv_out': out['v_w_conv_out'], 'v_w_attn_out': out['v_w_attn_out'], 'v_w_o': out['v_w_o'], 'v_rel_bias': out['v_rel_bias'], 'v_final_g': out['v_final_g']}


def _loss(weights, diff, rest, loss_target):
    with _jax.named_scope("forward"):
        args = {**rest, TWIN_DIFF_INPUT: diff, **{k: w.astype(_WEIGHT_DTYPES[k]) for k, w in weights.items()}}
        y = _forward(args)
    with _jax.named_scope("loss_head"):
        err = _jnp.square(y.astype(_jnp.float32) - loss_target)
        return 0.5 * _jnp.sum(_jnp.mean(err, axis=-1)) if err.ndim else 0.5 * err


def _adamw(w, g, m, v):
    m = ADAM_B1 * m + (1.0 - ADAM_B1) * g
    v = ADAM_B2 * v + (1.0 - ADAM_B2) * _jnp.square(g)
    m_hat = m / (1.0 - ADAM_B1 ** ADAM_STEP)
    v_hat = v / (1.0 - ADAM_B2 ** ADAM_STEP)
    delta = -ADAM_LR * (m_hat / (_jnp.sqrt(v_hat) + ADAM_EPS) + ADAM_WD * w)
    return delta, m, v


def reference(x, c, ada_w, ada_b, norm_g, ffn_w_gate, ffn_w_up, ffn_w_down, w_in, conv_w, w_conv_out, w_attn_out, w_o, rel_bias, final_g, loss_target, m_ada_w, m_ada_b, m_norm_g, m_ffn_w_gate, m_ffn_w_up, m_ffn_w_down, m_w_in, m_conv_w, m_w_conv_out, m_w_attn_out, m_w_o, m_rel_bias, m_final_g, v_ada_w, v_ada_b, v_norm_g, v_ffn_w_gate, v_ffn_w_up, v_ffn_w_down, v_w_in, v_conv_w, v_w_conv_out, v_w_attn_out, v_w_o, v_rel_bias, v_final_g):
    given = dict(x=x, c=c, ada_w=ada_w, ada_b=ada_b, norm_g=norm_g, ffn_w_gate=ffn_w_gate, ffn_w_up=ffn_w_up, ffn_w_down=ffn_w_down, w_in=w_in, conv_w=conv_w, w_conv_out=w_conv_out, w_attn_out=w_attn_out, w_o=w_o, rel_bias=rel_bias, final_g=final_g, loss_target=loss_target, m_ada_w=m_ada_w, m_ada_b=m_ada_b, m_norm_g=m_norm_g, m_ffn_w_gate=m_ffn_w_gate, m_ffn_w_up=m_ffn_w_up, m_ffn_w_down=m_ffn_w_down, m_w_in=m_w_in, m_conv_w=m_conv_w, m_w_conv_out=m_w_conv_out, m_w_attn_out=m_w_attn_out, m_w_o=m_w_o, m_rel_bias=m_rel_bias, m_final_g=m_final_g, v_ada_w=v_ada_w, v_ada_b=v_ada_b, v_norm_g=v_norm_g, v_ffn_w_gate=v_ffn_w_gate, v_ffn_w_up=v_ffn_w_up, v_ffn_w_down=v_ffn_w_down, v_w_in=v_w_in, v_conv_w=v_conv_w, v_w_conv_out=v_w_conv_out, v_w_attn_out=v_w_attn_out, v_w_o=v_w_o, v_rel_bias=v_rel_bias, v_final_g=v_final_g)
    weights = {n: given[n] for n in TWIN_WEIGHTS}
    shared = {n: given[n] for n in SHARED_INPUTS}
    per_example = {n: given[n] for n in ['x', 'c']}
    grad_fn = _jax.value_and_grad(_loss, argnums=(0, 1))

    def one_microbatch(ex, loss_target):
        ex = dict(ex)
        diff = ex.pop(TWIN_DIFF_INPUT)
        return grad_fn(weights, diff, {**shared, **ex}, loss_target)

    if N_MICROBATCH == 1:
        loss, (grad_w, grad_x) = one_microbatch(per_example, given["loss_target"])
    else:
        def body(carry, xs):
            loss_sum, grad_sum = carry
            l_k, (gw_k, gx_k) = one_microbatch(xs[0], xs[1])
            with _jax.named_scope("update"):
                return (loss_sum + l_k, _jax.tree.map(_jnp.add, grad_sum, gw_k)), gx_k

        init = (_jnp.zeros((), _jnp.float32), _jax.tree.map(_jnp.zeros_like, weights))
        (loss, grad_w), grad_x = _jax.lax.scan(body, init, (per_example, given["loss_target"]))
    with _jax.named_scope("update"):
        delta_w, new_m, new_v = {}, {}, {}
        for n in TWIN_WEIGHTS:
            delta_w[n], new_m[n], new_v[n] = _adamw(weights[n], grad_w[n], given["m_" + n], given["v_" + n])
    return (loss, grad_x, *[grad_w[n] for n in TWIN_WEIGHTS], *[delta_w[n] for n in TWIN_WEIGHTS],
            *[new_m[n] for n in TWIN_WEIGHTS], *[new_v[n] for n in TWIN_WEIGHTS])
```

```python
import functools

import numpy as np
import jax
import jax.numpy as jnp
from jax import lax
from jax.experimental import pallas as pl
from jax.experimental.pallas import tpu as pltpu

F32 = jnp.float32
BF16 = jnp.bfloat16

N_DEV = 8
HEAD_DIM = 64
HEAD_SHIFT = 6
HEADS_PER_GROUP = 8
DILATION_GROUPS = ((128, 1), (512, 4), (2048, 16))
N_GROUPS = len(DILATION_GROUPS)
ATTN_OUT = HEADS_PER_GROUP * HEAD_DIM
QKV_G = 3 * ATTN_OUT
BLOCK = 128
NUM_BUCKETS = 32
MAX_DISTANCE = 2048
CONV_K = 3
EPS = 1e-6
NEG_INF = -1e30
SCALE = HEAD_DIM ** -0.5
FF_BLK = 256

ADAM_LR = 0.001
ADAM_B1 = 0.9
ADAM_B2 = 0.999
ADAM_EPS = 1e-08
ADAM_WD = 0.01
ADAM_STEP = 10

PACK_W = 1024
V7X_VMEM_LIMIT = 48 * 1024 * 1024
MESH = pl.DeviceIdType.MESH

NN = (((1,), (0,)), ((), ()))
NT = (((1,), (1,)), ((), ()))
TN = (((0,), (0,)), ((), ()))


def _pick(dim, cands):
    for c in cands:
        if dim % c == 0:
            return c
    return dim


def _pick_k(K, cap=2816):
    if K <= cap or K % 128:
        return K
    best = 128
    for m in range(1, K // 128 + 1):
        if (K // 128) % m == 0 and 128 * m <= cap:
            best = 128 * m
    return best


def _params(sem):
    return pltpu.CompilerParams(dimension_semantics=sem, vmem_limit_bytes=V7X_VMEM_LIMIT)


def _all_gather(x_shard, name, in_vmem):
    m_per, n = x_shard.shape

    def body(x_ref, out_ref, send_sems, recv_sems, local_sem):
        x, y, c = lax.axis_index("x"), lax.axis_index("y"), lax.axis_index("c")
        me, sibling = (x, y, c), (x, y, 1 - c)
        chips = [(1 - x, y), (x, 1 - y), (1 - x, 1 - y)]

        def rows(px, py, pc):
            return out_ref.at[pl.ds((4 * px + 2 * py + pc) * m_per, m_per), :]

        def copy(k, block, to, src=None):
            return pltpu.make_async_remote_copy(
                src_ref=rows(*block) if src is None else src, dst_ref=rows(*block),
                send_sem=send_sems.at[k], recv_sem=recv_sems.at[k], device_id=to, device_id_type=MESH)

        mine = pltpu.make_async_copy(x_ref, rows(*me), local_sem)
        mine.start()
        first = [copy(0, me, sibling, src=x_ref)]
        first += [copy(1 + j, me, (*chip, c), src=x_ref) for j, chip in enumerate(chips)]
        for cp in first:
            cp.start()
        passed = [copy(4 + j, (*chip, c), sibling) for j, chip in enumerate(chips)]
        for j, chip in enumerate(chips):
            copy(1 + j, (*chip, c), me).wait_recv()
            passed[j].start()
        copy(0, sibling, me).wait_recv()
        for j, chip in enumerate(chips):
            copy(4 + j, (*chip, 1 - c), me).wait_recv()
        for cp in first + passed:
            cp.wait_send()
        mine.wait()

    space = pltpu.VMEM if in_vmem else pltpu.HBM
    return pl.pallas_call(
        body, name=name,
        out_shape=jax.ShapeDtypeStruct((N_DEV * m_per, n), x_shard.dtype),
        in_specs=[pl.BlockSpec(memory_space=space)],
        out_specs=pl.BlockSpec(memory_space=space),
        scratch_shapes=[pltpu.SemaphoreType.DMA((7,)), pltpu.SemaphoreType.DMA((7,)), pltpu.SemaphoreType.DMA],
    )(x_shard)


def _all_to_all(src, name):
    def body(src_ref, dst_ref, send_sems, recv_sems, local_sem):
        x, y, c = lax.axis_index("x"), lax.axis_index("y"), lax.axis_index("c")
        me = 4 * x + 2 * y + c
        mine = pltpu.make_async_copy(src_ref.at[me], dst_ref.at[me], local_sem)
        mine.start()
        sends, peers = [], []
        for k in range(1, N_DEV):
            px = 1 - x if (k >> 2) & 1 else x
            py = 1 - y if (k >> 1) & 1 else y
            pc = 1 - c if k & 1 else c
            peer = 4 * px + 2 * py + pc
            cp = pltpu.make_async_remote_copy(
                src_ref=src_ref.at[peer], dst_ref=dst_ref.at[me],
                send_sem=send_sems.at[k - 1], recv_sem=recv_sems.at[k - 1],
                device_id=(px, py, pc), device_id_type=MESH)
            cp.start()
            sends.append(cp)
            peers.append((peer, (px, py, pc)))
        for k, (peer, pid) in enumerate(peers):
            pltpu.make_async_remote_copy(
                src_ref=src_ref.at[me], dst_ref=dst_ref.at[peer],
                send_sem=send_sems.at[k], recv_sem=recv_sems.at[k],
                device_id=pid, device_id_type=MESH).wait_recv()
        for cp in sends:
            cp.wait_send()
        mine.wait()

    return pl.pallas_call(
        body, name=name,
        out_shape=jax.ShapeDtypeStruct(src.shape, src.dtype),
        in_specs=[pl.BlockSpec(memory_space=pltpu.HBM)],
        out_specs=pl.BlockSpec(memory_space=pltpu.HBM),
        scratch_shapes=[pltpu.SemaphoreType.DMA((7,)), pltpu.SemaphoreType.DMA((7,)), pltpu.SemaphoreType.DMA],
    )(src)


def _sum_sources(parts, name):
    _, r, n = parts.shape
    tr = _pick(r, [256, 128, 64, 32, 16, 8])

    def kern(p_ref, o_ref):
        acc = p_ref[0].astype(F32)
        for k in range(1, N_DEV):
            acc = acc + p_ref[k].astype(F32)
        o_ref[...] = acc

    return pl.pallas_call(
        kern, name=name, grid=(r // tr,),
        out_shape=jax.ShapeDtypeStruct((r, n), F32),
        in_specs=[pl.BlockSpec((N_DEV, tr, n), lambda i: (0, i, 0))],
        out_specs=pl.BlockSpec((tr, n), lambda i: (i, 0)),
        compiler_params=_params(("parallel",)),
    )(parts)


def _matmul(a, b, mode, out_dtype, name, tm=None, tn=None, tk=None, resid=None):
    if mode == "nn":
        (M, K), N = a.shape, b.shape[1]
    elif mode == "nt":
        (M, K), N = a.shape, b.shape[0]
    else:
        (K, M), N = a.shape, b.shape[1]
    dims = {"nn": NN, "nt": NT, "tn": TN}[mode]
    tm = tm or _pick(M, [1024, 1408, 512, 256, 128])
    tn = tn or _pick(N, [1024, 1408, 512, 256, 128])
    tk = tk or _pick_k(K)
    nk = K // tk
    a_spec = {"nn": pl.BlockSpec((tm, tk), lambda i, j, k: (i, k)),
              "nt": pl.BlockSpec((tm, tk), lambda i, j, k: (i, k)),
              "tn": pl.BlockSpec((tk, tm), lambda i, j, k: (k, i))}[mode]
    b_spec = {"nn": pl.BlockSpec((tk, tn), lambda i, j, k: (k, j)),
              "nt": pl.BlockSpec((tn, tk), lambda i, j, k: (j, k)),
              "tn": pl.BlockSpec((tk, tn), lambda i, j, k: (k, j))}[mode]
    o_spec = pl.BlockSpec((tm, tn), lambda i, j, k: (i, j))
    n_in = 2 if resid is None else 4
    n_out = 1 if resid is None else 2

    def kern(*refs):
        a_ref, b_ref = refs[0], refs[1]
        outs = refs[n_in:n_in + n_out]
        acc_ref = refs[n_in + n_out] if nk > 1 else None

        def finish(acc):
            if resid is None:
                outs[0][...] = acc.astype(out_dtype)
            else:
                x_ref, g_ref = refs[2], refs[3]
                outs[0][...] = x_ref[...] + (resid[2] * g_ref[...]) * acc
                outs[1][...] = acc.astype(out_dtype)

        part = lax.dot_general(a_ref[...], b_ref[...], dims, preferred_element_type=F32)
        if nk == 1:
            finish(part)
        else:
            k = pl.program_id(2)

            @pl.when(k == 0)
            def _():
                acc_ref[...] = part

            @pl.when(k > 0)
            def _():
                acc_ref[...] += part

            @pl.when(k == nk - 1)
            def _():
                finish(acc_ref[...])

    in_specs = [a_spec, b_spec]
    args = [a, b]
    out_shape = [jax.ShapeDtypeStruct((M, N), out_dtype)]
    out_specs = [o_spec]
    if resid is not None:
        in_specs += [o_spec, pl.BlockSpec((1, tn), lambda i, j, k: (0, j))]
        args += [resid[0], resid[1]]
        out_shape = [jax.ShapeDtypeStruct((M, N), F32)] + out_shape
        out_specs = [o_spec, o_spec]
    res = pl.pallas_call(
        kern, name=name, grid=(M // tm, N // tn, nk),
        out_shape=out_shape, in_specs=in_specs, out_specs=out_specs,
        scratch_shapes=[pltpu.VMEM((tm, tn), F32)] if nk > 1 else [],
        compiler_params=_params(("parallel", "parallel", "arbitrary")),
    )(*args)
    return res[0] if resid is None else res


def _dot3(a, b, dims):
    ah = a.astype(BF16)
    al = (a - ah.astype(F32)).astype(BF16)
    bh = b.astype(BF16)
    bl = (b - bh.astype(F32)).astype(BF16)
    d = functools.partial(lax.dot_general, dimension_numbers=dims, preferred_element_type=F32)
    return d(ah, bh) + (d(ah, bl) + d(al, bh))


def _row_spec(tm, d):
    return pl.BlockSpec((tm, d), lambda i: (i, 0))


def _vec_spec(d, rows=1):
    return pl.BlockSpec((rows, d), lambda i: (0, 0))


def _norm_mod_fwd(x, g, s, b, name):
    S, D = x.shape
    tm = _pick(S, [512, 256, 128])

    def kern(x_ref, g_ref, s_ref, b_ref, h_ref):
        xv = x_ref[...]
        r = lax.rsqrt(jnp.mean(xv * xv, axis=1, keepdims=True) + EPS)
        h_ref[...] = (xv * r * g_ref[...] * (1.0 + s_ref[...]) + b_ref[...]).astype(BF16)

    return pl.pallas_call(
        kern, name=name, grid=(S // tm,),
        out_shape=jax.ShapeDtypeStruct((S, D), BF16),
        in_specs=[_row_spec(tm, D), _vec_spec(D), _vec_spec(D), _vec_spec(D)],
        out_specs=_row_spec(tm, D),
        compiler_params=_params(("parallel",)),
    )(x, g, s, b)


def _norm_mod_bwd(x, dh, dxo, g, s, name):
    S, D = x.shape
    tm = _pick(S, [256, 128])
    n = S // tm

    def kern(x_ref, dh_ref, dxo_ref, g_ref, s_ref, dx_ref, cs_ref):
        i = pl.program_id(0)
        xv = x_ref[...]
        r = lax.rsqrt(jnp.mean(xv * xv, axis=1, keepdims=True) + EPS)
        xn = xv * r
        dh_v = dh_ref[...].astype(F32)
        one_s = 1.0 + s_ref[...]
        dxn = dh_v * (g_ref[...] * one_s)
        dx_ref[...] = dxo_ref[...] + r * (dxn - xn * jnp.mean(xn * dxn, axis=1, keepdims=True))

        @pl.when(i == 0)
        def _():
            cs_ref[...] = jnp.zeros_like(cs_ref)

        cs_ref[0:1, :] += jnp.sum(dh_v, axis=0, keepdims=True)
        cs_ref[1:2, :] += jnp.sum(dh_v * xn, axis=0, keepdims=True)

        @pl.when(i == n - 1)
        def _():
            t = cs_ref[1:2, :]
            cs_ref[2:3, :] = g_ref[...] * t
            cs_ref[3:4, :] = one_s * t

    return pl.pallas_call(
        kern, name=name, grid=(n,),
        out_shape=[jax.ShapeDtypeStruct((S, D), F32), jax.ShapeDtypeStruct((8, D), F32)],
        in_specs=[_row_spec(tm, D), _row_spec(tm, D), _row_spec(tm, D), _vec_spec(D), _vec_spec(D)],
        out_specs=[_row_spec(tm, D), _vec_spec(D, 8)],
        compiler_params=_params(("arbitrary",)),
    )(x, dh, dxo, g, s)


def _gate_bwd(dxo, f, gate, coef, name):
    S, D = dxo.shape
    tm = _pick(S, [512, 256, 128])

    def kern(dxo_ref, f_ref, gate_ref, df_ref, cs_ref):
        i = pl.program_id(0)
        dv = dxo_ref[...]
        df_ref[...] = ((coef * gate_ref[...]) * dv).astype(BF16)

        @pl.when(i == 0)
        def _():
            cs_ref[...] = jnp.zeros_like(cs_ref)

        cs_ref[0:1, :] += coef * jnp.sum(f_ref[...].astype(F32) * dv, axis=0, keepdims=True)

    return pl.pallas_call(
        kern, name=name, grid=(S // tm,),
        out_shape=[jax.ShapeDtypeStruct((S, D), BF16), jax.ShapeDtypeStruct((8, D), F32)],
        in_specs=[_row_spec(tm, D), _row_spec(tm, D), _vec_spec(D)],
        out_specs=[_row_spec(tm, D), _vec_spec(D, 8)],
        compiler_params=_params(("arbitrary",)),
    )(dxo, f, gate)


def _loss_head(x, g, target, name):
    S, D = x.shape
    tm = _pick(S, [256, 128])
    n = S // tm

    def kern(x_ref, g_ref, t_ref, dx_ref, cs_ref):
        i = pl.program_id(0)
        xv = x_ref[...]
        r = lax.rsqrt(jnp.mean(xv * xv, axis=1, keepdims=True) + EPS)
        xn = xv * r
        e = xn * g_ref[...] - t_ref[...]
        dxn = (e * (1.0 / D)) * g_ref[...]
        dx_ref[...] = r * (dxn - xn * jnp.mean(xn * dxn, axis=1, keepdims=True))

        @pl.when(i == 0)
        def _():
            cs_ref[...] = jnp.zeros_like(cs_ref)

        cs_ref[0:1, :] += jnp.sum(xn * e, axis=0, keepdims=True) * (1.0 / D)
        cs_ref[1:2, :] += jnp.sum(e * e, axis=0, keepdims=True)

        @pl.when(i == n - 1)
        def _():
            tot = jnp.sum(cs_ref[1:2, :], axis=1, keepdims=True) * (0.5 / D)
            cs_ref[2:3, :] = jnp.broadcast_to(tot, (1, D))

    return pl.pallas_call(
        kern, name=name, grid=(n,),
        out_shape=[jax.ShapeDtypeStruct((S, D), F32), jax.ShapeDtypeStruct((8, D), F32)],
        in_specs=[_row_spec(tm, D), _vec_spec(D), _row_spec(tm, D)],
        out_specs=[_row_spec(tm, D), _vec_spec(D, 8)],
        compiler_params=_params(("arbitrary",)),
    )(x, g, target)


def _swiglu_fwd(au, name):
    S, F2 = au.shape
    F = F2 // 2
    tm = _pick(S, [512, 256, 128])

    def kern(au_ref, z_ref):
        a = au_ref[:, :FF_BLK].astype(F32)
        u = au_ref[:, FF_BLK:].astype(F32)
        z_ref[...] = (a * jax.nn.sigmoid(a) * u).astype(BF16)

    return pl.pallas_call(
        kern, name=name, grid=(S // tm, F // FF_BLK),
        out_shape=jax.ShapeDtypeStruct((S, F), BF16),
        in_specs=[pl.BlockSpec((tm, 2 * FF_BLK), lambda i, j: (i, j))],
        out_specs=pl.BlockSpec((tm, FF_BLK), lambda i, j: (i, j)),
        compiler_params=_params(("parallel", "parallel")),
    )(au)


def _swiglu_bwd(au, dz, name):
    S, F2 = au.shape
    F = F2 // 2
    tm = _pick(S, [512, 256, 128])

    def kern(au_ref, dz_ref, d_ref):
        a = au_ref[:, :FF_BLK].astype(F32)
        u = au_ref[:, FF_BLK:].astype(F32)
        dzv = dz_ref[...].astype(F32)
        sg = jax.nn.sigmoid(a)
        d_ref[:, :FF_BLK] = (dzv * u * (sg * (1.0 + a * (1.0 - sg)))).astype(BF16)
        d_ref[:, FF_BLK:] = (dzv * (a * sg)).astype(BF16)

    return pl.pallas_call(
        kern, name=name, grid=(S // tm, F // FF_BLK),
        out_shape=jax.ShapeDtypeStruct((S, F2), BF16),
        in_specs=[pl.BlockSpec((tm, 2 * FF_BLK), lambda i, j: (i, j)),
                  pl.BlockSpec((tm, FF_BLK), lambda i, j: (i, j))],
        out_specs=pl.BlockSpec((tm, 2 * FF_BLK), lambda i, j: (i, j)),
        compiler_params=_params(("parallel", "parallel")),
    )(au, dz)


def _shift_down(p, row, prev_rows):
    a, b = prev_rows
    p1 = jnp.where(row == 0, b, pltpu.roll(p, 1, 0))
    p2 = jnp.where(row == 0, a, jnp.where(row == 1, b, pltpu.roll(p, 2, 0)))
    return p1, p2


def _conv_fwd(cg, conv_w, name):
    S, D5 = cg.shape
    D = D5 // 5
    tm = _pick(S, [256, 128])
    t8 = tm // 8

    def prev(col):
        return pl.BlockSpec((8, D), lambda i: (jnp.maximum(i * t8 - 1, 0), col))

    def kern(cb_ref, cc_ref, ch_ref, ccp_ref, chp_ref, w_ref, y_ref):
        i = pl.program_id(0)
        keep = jnp.where(i > 0, 1.0, 0.0)
        p = cc_ref[...].astype(F32) * ch_ref[...].astype(F32)
        pa = ccp_ref[6:7, :].astype(F32) * chp_ref[6:7, :].astype(F32) * keep
        pb = ccp_ref[7:8, :].astype(F32) * chp_ref[7:8, :].astype(F32) * keep
        row = lax.broadcasted_iota(jnp.int32, (tm, D), 0)
        p1, p2 = _shift_down(p, row, (pa, pb))
        dw = w_ref[0:1, :] * p2 + w_ref[1:2, :] * p1 + w_ref[2:3, :] * p
        y_ref[...] = (cb_ref[...].astype(F32) * dw).astype(BF16)

    def col(cidx):
        return pl.BlockSpec((tm, D), lambda i: (i, cidx))

    return pl.pallas_call(
        kern, name=name, grid=(S // tm,),
        out_shape=jax.ShapeDtypeStruct((S, D), BF16),
        in_specs=[col(0), col(1), col(2), prev(1), prev(2), _vec_spec(D, CONV_K)],
        out_specs=_row_spec(tm, D),
        compiler_params=_params(("parallel",)),
    )(cg, cg, cg, cg, cg, conv_w)


def _conv_bwd(cg, dy, conv_w, name):
    S, D5 = cg.shape
    D = D5 // 5
    tm = _pick(S, [256, 128])
    t8 = tm // 8
    n = S // tm
    last8 = S // 8 - 1

    def prev(col):
        return pl.BlockSpec((8, D), lambda i: (jnp.maximum(i * t8 - 1, 0), col))

    def nxt(col):
        return pl.BlockSpec((8, D), lambda i: (jnp.minimum((i + 1) * t8, last8), col))

    def kern(cb_ref, cc_ref, ch_ref, dy_ref, ccp_ref, chp_ref, cbn_ref, dyn_ref, w_ref, d_ref, cs_ref):
        i = pl.program_id(0)
        keep_p = jnp.where(i > 0, 1.0, 0.0)
        keep_n = jnp.where(i < n - 1, 1.0, 0.0)
        cb = cb_ref[...].astype(F32)
        cc = cc_ref[...].astype(F32)
        ch = ch_ref[...].astype(F32)
        dyv = dy_ref[...].astype(F32)
        p = cc * ch
        pa = ccp_ref[6:7, :].astype(F32) * chp_ref[6:7, :].astype(F32) * keep_p
        pb = ccp_ref[7:8, :].astype(F32) * chp_ref[7:8, :].astype(F32) * keep_p
        row = lax.broadcasted_iota(jnp.int32, (tm, D), 0)
        p1, p2 = _shift_down(p, row, (pa, pb))
        w0, w1, w2 = w_ref[0:1, :], w_ref[1:2, :], w_ref[2:3, :]
        dw = w0 * p2 + w1 * p1 + w2 * p
        ddw = dyv * cb
        na = dyn_ref[0:1, :].astype(F32) * cbn_ref[0:1, :].astype(F32) * keep_n
        nb = dyn_ref[1:2, :].astype(F32) * cbn_ref[1:2, :].astype(F32) * keep_n
        u1 = jnp.where(row == tm - 1, na, pltpu.roll(ddw, tm - 1, 0))
        u2 = jnp.where(row == tm - 2, na, jnp.where(row == tm - 1, nb, pltpu.roll(ddw, tm - 2, 0)))
        dp = w2 * ddw + w1 * u1 + w0 * u2
        d_ref[:, 0:D] = (dyv * dw).astype(BF16)
        d_ref[:, D:2 * D] = (dp * ch).astype(BF16)
        d_ref[:, 2 * D:3 * D] = (dp * cc).astype(BF16)

        @pl.when(i == 0)
        def _():
            cs_ref[...] = jnp.zeros_like(cs_ref)

        cs_ref[0:1, :] += jnp.sum(ddw * p2, axis=0, keepdims=True)
        cs_ref[1:2, :] += jnp.sum(ddw * p1, axis=0, keepdims=True)
        cs_ref[2:3, :] += jnp.sum(ddw * p, axis=0, keepdims=True)

    def col(cidx):
        return pl.BlockSpec((tm, D), lambda i: (i, cidx))

    return pl.pallas_call(
        kern, name=name, grid=(n,),
        out_shape=[jax.ShapeDtypeStruct((S, 3 * D), BF16), jax.ShapeDtypeStruct((8, D), F32)],
        in_specs=[col(0), col(1), col(2), _row_spec(tm, D), prev(1), prev(2), nxt(0),
                  pl.BlockSpec((8, D), lambda i: (jnp.minimum((i + 1) * t8, last8), 0)), _vec_spec(D, CONV_K)],
        out_specs=[pl.BlockSpec((tm, 3 * D), lambda i: (i, 0)), _vec_spec(D, 8)],
        compiler_params=_params(("arbitrary",)),
    )(cg, cg, cg, dy, cg, cg, cg, dy, conv_w)


def _merge_fwd(cg, yc, ya, name):
    S, D = yc.shape
    tm = _pick(S, [512, 256, 128])

    def kern(gc_ref, ga_ref, yc_ref, ya_ref, m_ref):
        m_ref[...] = (jax.nn.sigmoid(gc_ref[...].astype(F32)) * yc_ref[...].astype(F32)
                      + jax.nn.sigmoid(ga_ref[...].astype(F32)) * ya_ref[...].astype(F32)).astype(BF16)

    return pl.pallas_call(
        kern, name=name, grid=(S // tm,),
        out_shape=jax.ShapeDtypeStruct((S, D), BF16),
        in_specs=[pl.BlockSpec((tm, D), lambda i: (i, 3)), pl.BlockSpec((tm, D), lambda i: (i, 4)),
                  _row_spec(tm, D), _row_spec(tm, D)],
        out_specs=_row_spec(tm, D),
        compiler_params=_params(("parallel",)),
    )(cg, cg, yc, ya)


def _merge_bwd(cg, yc, ya, dm, name):
    S, D = yc.shape
    tm = _pick(S, [256, 128])

    def kern(gc_ref, ga_ref, yc_ref, ya_ref, dm_ref, dyc_ref, dya_ref, dg_ref):
        dmv = dm_ref[...].astype(F32)
        sc = jax.nn.sigmoid(gc_ref[...].astype(F32))
        sa = jax.nn.sigmoid(ga_ref[...].astype(F32))
        dyc_ref[...] = (dmv * sc).astype(BF16)
        dya_ref[...] = (dmv * sa).astype(BF16)
        dg_ref[:, 0:D] = (dmv * yc_ref[...].astype(F32) * (sc * (1.0 - sc))).astype(BF16)
        dg_ref[:, D:2 * D] = (dmv * ya_ref[...].astype(F32) * (sa * (1.0 - sa))).astype(BF16)

    return pl.pallas_call(
        kern, name=name, grid=(S // tm,),
        out_shape=[jax.ShapeDtypeStruct((S, D), BF16), jax.ShapeDtypeStruct((S, D), BF16),
                   jax.ShapeDtypeStruct((S, 2 * D), BF16)],
        in_specs=[pl.BlockSpec((tm, D), lambda i: (i, 3)), pl.BlockSpec((tm, D), lambda i: (i, 4)),
                  _row_spec(tm, D), _row_spec(tm, D), _row_spec(tm, D)],
        out_specs=[_row_spec(tm, D), _row_spec(tm, D), pl.BlockSpec((tm, 2 * D), lambda i: (i, 0))],
        compiler_params=_params(("parallel",)),
    )(cg, cg, yc, ya, dm)


def _t5_bucket(dist):
    exact = NUM_BUCKETS // 2
    d = np.maximum(dist, 1).astype(np.float32)
    large = exact + (np.log(d / exact) / np.log(MAX_DISTANCE / exact) * (NUM_BUCKETS - exact)).astype(np.int32)
    large = np.minimum(large, NUM_BUCKETS - 1)
    return np.where(dist < exact, dist, large).astype(np.int32)


def _bucket_tables():
    i = np.arange(BLOCK)[:, None]
    j = np.arange(2 * BLOCK)[None, :]
    rel = i - j + BLOCK
    return np.stack([_t5_bucket(np.maximum(rel, 0) * d) for _, d in DILATION_GROUPS]).astype(np.int32)


def _band_masks():
    i = lax.broadcasted_iota(jnp.int32, (BLOCK, 2 * BLOCK), 0)
    j = lax.broadcasted_iota(jnp.int32, (BLOCK, 2 * BLOCK), 1)
    rel = i - j + BLOCK
    band = (rel >= 0) & (rel <= BLOCK)
    return band, band & (j >= BLOCK)


def _bias_build(rel_bias, buckets, name):
    def kern(rb_ref, bk_ref, o_ref):
        g = pl.program_id(0)
        bk = bk_ref[0]
        band, first = _band_masks()
        for h in range(HEADS_PER_GROUP):
            acc = jnp.zeros((BLOCK, 2 * BLOCK), F32)
            for b in range(NUM_BUCKETS):
                acc = jnp.where(bk == b, rb_ref[b, g * HEADS_PER_GROUP + h], acc)
            o_ref[0, 0, h] = jnp.where(first, acc, NEG_INF)
            o_ref[0, 1, h] = jnp.where(band, acc, NEG_INF)

    return pl.pallas_call(
        kern, name=name, grid=(N_GROUPS,),
        out_shape=jax.ShapeDtypeStruct((N_GROUPS, 2, HEADS_PER_GROUP, BLOCK, 2 * BLOCK), F32),
        in_specs=[pl.BlockSpec(memory_space=pltpu.SMEM),
                  pl.BlockSpec((1, BLOCK, 2 * BLOCK), lambda g: (g, 0, 0))],
        out_specs=pl.BlockSpec((1, 2, HEADS_PER_GROUP, BLOCK, 2 * BLOCK), lambda g: (g, 0, 0, 0, 0)),
        compiler_params=_params(("parallel",)),
    )(rel_bias, buckets)


def _bias_bwd(dlog, buckets, name):
    def kern(dl_ref, bk_ref, o_ref):
        g = pl.program_id(0)
        bk = bk_ref[0]
        rowi = lax.broadcasted_iota(jnp.int32, (NUM_BUCKETS, 128), 0)
        coli = lax.broadcasted_iota(jnp.int32, (NUM_BUCKETS, 128), 1)

        @pl.when(g == 0)
        def _():
            o_ref[...] = jnp.zeros_like(o_ref)

        acc = jnp.zeros((NUM_BUCKETS, 128), F32)
        for h in range(HEADS_PER_GROUP):
            dv = dl_ref[0, h]
            for b in range(NUM_BUCKETS):
                t = jnp.sum(jnp.where(bk == b, dv, 0.0), axis=0, keepdims=True)
                t = jnp.sum(t, axis=1, keepdims=True)
                acc = acc + jnp.where((rowi == b) & (coli == g * HEADS_PER_GROUP + h), t, 0.0)
        o_ref[...] += acc

    return pl.pallas_call(
        kern, name=name, grid=(N_GROUPS,),
        out_shape=jax.ShapeDtypeStruct((NUM_BUCKETS, 128), F32),
        in_specs=[pl.BlockSpec((1, HEADS_PER_GROUP, BLOCK, 2 * BLOCK), lambda g: (g, 0, 0, 0)),
                  pl.BlockSpec((1, BLOCK, 2 * BLOCK), lambda g: (g, 0, 0))],
        out_specs=pl.BlockSpec((NUM_BUCKETS, 128), lambda g: (0, 0)),
        compiler_params=_params(("arbitrary",)),
    )(dlog, buckets)


def _head_masks():
    lane = lax.broadcasted_iota(jnp.int32, (BLOCK, 128), 1)
    lo = lane < HEAD_DIM
    return lo, jnp.logical_not(lo)


def _attn_fwd(qkv, bias, d, name):
    L = qkv.shape[0]
    nb = L // BLOCK

    def kern(q_ref, kp_ref, kc_ref, vp_ref, vc_ref, b_ref, o_ref, lse_ref):
        lo, hi = _head_masks()
        for p in range(HEADS_PER_GROUP // 2):
            sl = slice(128 * p, 128 * (p + 1))
            q = q_ref[:, sl]
            k = jnp.concatenate([kp_ref[:, sl], kc_ref[:, sl]], axis=0)
            v = jnp.concatenate([vp_ref[:, sl], vc_ref[:, sl]], axis=0)
            o2, l2 = [], []
            for hh, msk in enumerate((lo, hi)):
                qm = jnp.where(msk, q, jnp.zeros_like(q))
                s = lax.dot_general(qm, k, NT, preferred_element_type=F32) * SCALE + b_ref[0, 2 * p + hh]
                m = jnp.max(s, axis=1, keepdims=True)
                e = jnp.exp(s - m)
                l = jnp.sum(e, axis=1, keepdims=True)
                o2.append(lax.dot_general(e.astype(BF16), v, NN, preferred_element_type=F32) / l)
                l2.append(jnp.broadcast_to(m + jnp.log(l), (BLOCK, 128)))
            o_ref[:, sl] = jnp.where(lo, o2[0], o2[1])
            lse_ref[:, sl] = jnp.where(lo, l2[0], l2[1])

    def blk(col, prev):
        if prev:
            return pl.BlockSpec((BLOCK, ATTN_OUT), lambda r, n: (jnp.maximum(n - 1, 0), 3 * r + col))
        return pl.BlockSpec((BLOCK, ATTN_OUT), lambda r, n: (n, 3 * r + col))

    o_spec = pl.BlockSpec((BLOCK, ATTN_OUT), lambda r, n: (n, r))
    return pl.pallas_call(
        kern, name=name, grid=(d, nb),
        out_shape=[jax.ShapeDtypeStruct((L, d * ATTN_OUT), F32)] * 2,
        in_specs=[blk(0, False), blk(1, True), blk(1, False), blk(2, True), blk(2, False),
                  pl.BlockSpec((1, HEADS_PER_GROUP, BLOCK, 2 * BLOCK), lambda r, n: (jnp.minimum(n, 1), 0, 0, 0))],
        out_specs=[o_spec, o_spec],
        compiler_params=_params(("parallel", "arbitrary")),
    )(qkv, qkv, qkv, qkv, qkv, bias)


def _attn_bwd(qkv, do, lse, delta, bias, d, name):
    L = qkv.shape[0]
    nb = L // BLOCK
    low = -3.0e38

    def kern(q_ref, kp_ref, kc_ref, vp_ref, vc_ref, do_ref, lse_ref, dl_ref, b_ref,
             dq_ref, dk_ref, dv_ref, db_ref, ck_ref, cv_ref):
        r, n = pl.program_id(0), pl.program_id(1)

        @pl.when((r == 0) & (n == 0))
        def _():
            db_ref[...] = jnp.zeros_like(db_ref)

        @pl.when(n == 0)
        def _():
            ck_ref[...] = jnp.zeros_like(ck_ref)
            cv_ref[...] = jnp.zeros_like(cv_ref)

        @pl.when(n < nb)
        def _():
            lo, hi = _head_masks()
            for p in range(HEADS_PER_GROUP // 2):
                sl = slice(128 * p, 128 * (p + 1))
                q = q_ref[:, sl]
                k = jnp.concatenate([kp_ref[:, sl], kc_ref[:, sl]], axis=0)
                v = jnp.concatenate([vp_ref[:, sl], vc_ref[:, sl]], axis=0)
                dov = do_ref[:, sl]
                lse_b = lse_ref[:, sl]
                del_b = dl_ref[:, sl]
                dq2 = []
                dk_acc = jnp.zeros((2 * BLOCK, 128), F32)
                dv_acc = jnp.zeros((2 * BLOCK, 128), F32)
                for hh, msk in enumerate((lo, hi)):
                    qm = jnp.where(msk, q, jnp.zeros_like(q))
                    dom = jnp.where(msk, dov, jnp.zeros_like(dov))
                    lse_h = jnp.max(jnp.where(msk, lse_b, low), axis=1, keepdims=True)
                    del_h = jnp.max(jnp.where(msk, del_b, low), axis=1, keepdims=True)
                    s = lax.dot_general(qm, k, NT, preferred_element_type=F32) * SCALE + b_ref[0, 2 * p + hh]
                    pr = jnp.exp(s - lse_h)
                    dp = lax.dot_general(dom, v, NT, preferred_element_type=F32)
                    ds = pr * (dp - del_h)
                    db_ref[2 * p + hh] += ds
                    dsb = (ds * SCALE).astype(BF16)
                    dq2.append(lax.dot_general(dsb, k, NN, preferred_element_type=F32))
                    dk_acc = dk_acc + lax.dot_general(dsb, qm, TN, preferred_element_type=F32)
                    dv_acc = dv_acc + lax.dot_general(pr.astype(BF16), dom, TN, preferred_element_type=F32)
                dq_ref[:, sl] = jnp.where(lo, dq2[0], dq2[1]).astype(BF16)
                dk_ref[:, sl] = (ck_ref[:, sl] + dk_acc[0:BLOCK]).astype(BF16)
                dv_ref[:, sl] = (cv_ref[:, sl] + dv_acc[0:BLOCK]).astype(BF16)
                ck_ref[:, sl] = dk_acc[BLOCK:2 * BLOCK]
                cv_ref[:, sl] = dv_acc[BLOCK:2 * BLOCK]

        @pl.when(n == nb)
        def _():
            dk_ref[...] = ck_ref[...].astype(BF16)
            dv_ref[...] = cv_ref[...].astype(BF16)

    def cur(n):
        return jnp.minimum(n, nb - 1)

    def blk(col, prev):
        if prev:
            return pl.BlockSpec((BLOCK, ATTN_OUT), lambda r, n: (jnp.maximum(cur(n) - 1, 0), 3 * r + col))
        return pl.BlockSpec((BLOCK, ATTN_OUT), lambda r, n: (cur(n), 3 * r + col))

    q_like = pl.BlockSpec((BLOCK, ATTN_OUT), lambda r, n: (cur(n), r))
    k_like = pl.BlockSpec((BLOCK, ATTN_OUT), lambda r, n: (jnp.maximum(n - 1, 0), r))
    return pl.pallas_call(
        kern, name=name, grid=(d, nb + 1),
        out_shape=[jax.ShapeDtypeStruct((L, d * ATTN_OUT), BF16)] * 3
        + [jax.ShapeDtypeStruct((HEADS_PER_GROUP, BLOCK, 2 * BLOCK), F32)],
        in_specs=[blk(0, False), blk(1, True), blk(1, False), blk(2, True), blk(2, False),
                  q_like, q_like, q_like,
                  pl.BlockSpec((1, HEADS_PER_GROUP, BLOCK, 2 * BLOCK),
                               lambda r, n: (jnp.minimum(cur(n), 1), 0, 0, 0))],
        out_specs=[q_like, k_like, k_like,
                   pl.BlockSpec((HEADS_PER_GROUP, BLOCK, 2 * BLOCK), lambda r, n: (0, 0, 0))],
        scratch_shapes=[pltpu.VMEM((BLOCK, ATTN_OUT), F32), pltpu.VMEM((BLOCK, ATTN_OUT), F32)],
        compiler_params=_params(("arbitrary", "arbitrary")),
    )(qkv, qkv, qkv, qkv, qkv, do, lse, delta, bias)


def _combine_fwd(os_, lses, name):
    S, W = os_[0].shape
    tm = _pick(S, [256, 128])

    def kern(o0, o1, o2, l0, l1, l2, of_ref, ob_ref, lse_ref):
        a0, a1, a2 = l0[...], l1[...], l2[...]
        m = jnp.maximum(jnp.maximum(a0, a1), a2)
        e0, e1, e2 = jnp.exp(a0 - m), jnp.exp(a1 - m), jnp.exp(a2 - m)
        tot = e0 + e1 + e2
        o = (e0 * o0[...] + e1 * o1[...] + e2 * o2[...]) / tot
        of_ref[...] = o
        ob_ref[...] = o.astype(BF16)
        lse_ref[...] = m + jnp.log(tot)

    spec = _row_spec(tm, W)
    return pl.pallas_call(
        kern, name=name, grid=(S // tm,),
        out_shape=[jax.ShapeDtypeStruct((S, W), F32), jax.ShapeDtypeStruct((S, W), BF16),
                   jax.ShapeDtypeStruct((S, W), F32)],
        in_specs=[spec] * 6, out_specs=[spec] * 3,
        compiler_params=_params(("parallel",)),
    )(*os_, *lses)


def _delta(do, o, name):
    S, W = o.shape
    tm = _pick(S, [512, 256, 128])

    def kern(do_ref, o_ref, d_ref):
        prod = do_ref[...].astype(F32) * o_ref[...]
        ri = jnp.right_shift(lax.broadcasted_iota(jnp.int32, (W, W), 0), HEAD_SHIFT)
        ci = jnp.right_shift(lax.broadcasted_iota(jnp.int32, (W, W), 1), HEAD_SHIFT)
        same = jnp.where(ri == ci, 1.0, 0.0).astype(BF16)
        hi_p = prod.astype(BF16)
        lo_p = (prod - hi_p.astype(F32)).astype(BF16)
        d_ref[...] = (lax.dot_general(hi_p, same, NN, preferred_element_type=F32)
                      + lax.dot_general(lo_p, same, NN, preferred_element_type=F32))

    return pl.pallas_call(
        kern, name=name, grid=(S // tm,),
        out_shape=jax.ShapeDtypeStruct((S, W), F32),
        in_specs=[_row_spec(tm, W), _row_spec(tm, W)], out_specs=_row_spec(tm, W),
        compiler_params=_params(("parallel",)),
    )(do, o)


def _ada_fwd(c16, ada_w, name):
    depth, D, n = ada_w.shape
    rows = 2 * N_DEV

    def kern(c_ref, w_ref, o_ref, cs_ref):
        cv = c_ref[...]
        cs = cv * jax.nn.sigmoid(cv)
        cs_ref[...] = cs
        o_ref[0] = _dot3(cs, w_ref[0], NN)

    return pl.pallas_call(
        kern, name=name, grid=(depth,),
        out_shape=[jax.ShapeDtypeStruct((depth, rows, n), F32), jax.ShapeDtypeStruct((rows, D), F32)],
        in_specs=[pl.BlockSpec((rows, D), lambda l: (0, 0)), pl.BlockSpec((1, D, n), lambda l: (l, 0, 0))],
        out_specs=[pl.BlockSpec((1, rows, n), lambda l: (l, 0, 0)), pl.BlockSpec((rows, D), lambda l: (0, 0))],
        compiler_params=_params(("arbitrary",)),
    )(c16, ada_w)


def _ada_bwd(cs16, dm16, name):
    depth, _, n = dm16.shape
    D = cs16.shape[1]

    def kern(cs_ref, dm_ref, o_ref):
        o_ref[0] = _dot3(cs_ref[...], dm_ref[0], TN)

    return pl.pallas_call(
        kern, name=name, grid=(depth,),
        out_shape=jax.ShapeDtypeStruct((depth, D, n), F32),
        in_specs=[pl.BlockSpec((2 * N_DEV, D), lambda l: (0, 0)), pl.BlockSpec((1, 2 * N_DEV, n), lambda l: (l, 0, 0))],
        out_specs=pl.BlockSpec((1, D, n), lambda l: (l, 0, 0)),
        compiler_params=_params(("parallel",)),
    )(cs16, dm16)


def _sum_rows8(parts, name):
    _, r, n = parts.shape

    def kern(p_ref, o_ref):
        acc = p_ref[0]
        for k in range(1, N_DEV):
            acc = acc + p_ref[k]
        o_ref[...] = acc

    return pl.pallas_call(
        kern, name=name, out_shape=jax.ShapeDtypeStruct((r, n), F32),
        in_specs=[pl.BlockSpec(memory_space=pltpu.VMEM)], out_specs=pl.BlockSpec(memory_space=pltpu.VMEM),
    )(parts)


def _adamw(w, g, m, v, name):
    shape = w.shape
    c = shape[-1]
    r = int(np.prod(shape[:-1])) if len(shape) > 1 else 1
    w2, g2, m2, v2 = (t.reshape(r, c) for t in (w, g, m, v))
    tr = r
    for cand in (2048, 1024, 512, 256, 128, 64, 32, 16, 8):
        if r % cand == 0 and cand * c * 4 <= (1 << 20):
            tr = cand
            break
    c1 = 1.0 - ADAM_B1 ** ADAM_STEP
    c2 = 1.0 - ADAM_B2 ** ADAM_STEP

    def kern(w_ref, g_ref, m_ref, v_ref, d_ref, nm_ref, nv_ref):
        gv = g_ref[...]
        nm = ADAM_B1 * m_ref[...] + (1.0 - ADAM_B1) * gv
        nv = ADAM_B2 * v_ref[...] + (1.0 - ADAM_B2) * (gv * gv)
        nm_ref[...] = nm
        nv_ref[...] = nv
        d_ref[...] = -ADAM_LR * ((nm / c1) / (jnp.sqrt(nv / c2) + ADAM_EPS) + ADAM_WD * w_ref[...])

    spec = pl.BlockSpec((tr, c), lambda i: (i, 0))
    outs = pl.pallas_call(
        kern, name=name, grid=(r // tr,),
        out_shape=[jax.ShapeDtypeStruct((r, c), F32)] * 3,
        in_specs=[spec] * 4, out_specs=[spec] * 3,
        compiler_params=_params(("parallel",)),
    )(w2, g2, m2, v2)
    return tuple(o.reshape(shape) for o in outs)


BIG = (("gate", 2), ("up", 2), ("down", 1), ("w_in", 1), ("co", 0), ("ao", 1), ("wo", 0))


def _pack_rows(shapes):
    rows, off = {}, 0
    for name, _ in BIG:
        n = int(np.prod(shapes[name])) // PACK_W
        assert n * PACK_W == int(np.prod(shapes[name]))
        rows[name] = (off, n)
        off += n
    return rows, off


def _interleave(wg, wu):
    D, F = wg.shape
    nblk = F // FF_BLK
    return jnp.stack([wg.reshape(D, nblk, FF_BLK), wu.reshape(D, nblk, FF_BLK)], axis=2).reshape(D, 2 * F)


def _deinterleave(wgu):
    D, F2 = wgu.shape
    t = wgu.reshape(D, F2 // (2 * FF_BLK), 2, FF_BLK)
    return t[:, :, 0].reshape(D, F2 // 2), t[:, :, 1].reshape(D, F2 // 2)


def _perm_w_in(w_in):
    qw = N_GROUPS * ATTN_OUT
    parts = []
    for g in range(N_GROUPS):
        for t in range(3):
            parts.append(w_in[:, t * qw + g * ATTN_OUT: t * qw + (g + 1) * ATTN_OUT])
    return jnp.concatenate(parts, axis=1), w_in[:, 3 * qw:]


def _unperm_w_in(d_qkv, d_cg):
    cols = []
    for t in range(3):
        for g in range(N_GROUPS):
            cols.append(d_qkv[:, (3 * g + t) * ATTN_OUT:(3 * g + t + 1) * ATTN_OUT])
    return jnp.concatenate(cols + [d_cg], axis=1)


def _view(a, d):
    S, w = a.shape
    return a.reshape(S // d, d * w)


def _unview(a, d):
    L, dw = a.shape
    return a.reshape(L * d, dw // d)


def kernel(x, c, ada_w, ada_b, norm_g, ffn_w_gate, ffn_w_up, ffn_w_down, w_in, conv_w, w_conv_out, w_attn_out, w_o, rel_bias, final_g, loss_target, m_ada_w, m_ada_b, m_norm_g, m_ffn_w_gate, m_ffn_w_up, m_ffn_w_down, m_w_in, m_conv_w, m_w_conv_out, m_w_attn_out, m_w_o, m_rel_bias, m_final_g, v_ada_w, v_ada_b, v_norm_g, v_ffn_w_gate, v_ffn_w_up, v_ffn_w_down, v_w_in, v_conv_w, v_w_conv_out, v_w_attn_out, v_w_o, v_rel_bias, v_final_g):
    depth = ada_w.shape[0]
    S, D = x.shape[1], x.shape[2]
    me = 4 * lax.axis_index("x") + 2 * lax.axis_index("y") + lax.axis_index("c")
    x0 = x.reshape(S, D)
    target = loss_target.reshape(S, D)

    shards = {"gate": ffn_w_gate, "up": ffn_w_up, "down": ffn_w_down, "w_in": w_in,
              "co": w_conv_out, "ao": w_attn_out, "wo": w_o}
    shard_shapes = {k: v.shape[1:] for k, v in shards.items()}
    rows, rows_per_layer = _pack_rows(shard_shapes)
    packed = jnp.concatenate(
        [shards[name].astype(BF16).reshape(depth, rows[name][1], PACK_W) for name, _ in BIG], axis=1)
    gathered = _all_gather(packed.reshape(depth * rows_per_layer, PACK_W), "weights_all_gather", in_vmem=False)
    gathered = gathered.reshape(N_DEV, depth, rows_per_layer, PACK_W)

    def full(name, l):
        off, n = rows[name]
        ax = dict(BIG)[name]
        return jnp.concatenate(
            [gathered[k, l, off:off + n].reshape(shard_shapes[name]) for k in range(N_DEV)], axis=ax)

    W = []
    for l in range(depth):
        gate, up, down = full("gate", l), full("up", l), full("down", l)
        w_qkv, w_cg = _perm_w_in(full("w_in", l))
        W.append(dict(
            gu=[_interleave(gate[i], up[i]) for i in range(2)], down=[down[i] for i in range(2)],
            qkv=w_qkv, cg=w_cg, in_all=jnp.concatenate([w_qkv, w_cg], axis=1),
            co=full("co", l), ao=full("ao", l), wo=full("wo", l)))

    c_all = _all_gather(c.reshape(D // 128, 128), "c_all_gather", in_vmem=True).reshape(N_DEV, D)
    c16 = jnp.concatenate([c_all, jnp.zeros_like(c_all)], axis=0)
    mod_part, cs16 = _ada_fwd(c16, ada_w, "ada_fwd")
    mod_part = mod_part[:, :N_DEV]
    n_ada = ada_w.shape[2]
    mod_all = _all_gather(mod_part.reshape(depth * N_DEV * n_ada // 128, 128), "mod_all_gather", in_vmem=True)
    mod_all = mod_all.reshape(N_DEV, depth, N_DEV, n_ada)
    mod_mine = lax.dynamic_index_in_dim(mod_all, me, axis=2, keepdims=False)
    mod = jnp.transpose(mod_mine, (1, 0, 2)).reshape(depth, N_DEV * n_ada) + ada_b
    mod = mod.reshape(depth, 3, 3, 1, D)

    small = jnp.concatenate([norm_g.reshape(-1), conv_w.reshape(-1)]).reshape(-1, 128)
    small_all = _all_gather(small, "small_all_gather", in_vmem=True).reshape(N_DEV, -1)
    n_ng = norm_g.size
    dsh = D // N_DEV
    norm_g_full = jnp.transpose(small_all[:, :n_ng].reshape(N_DEV, depth, 3, dsh), (1, 2, 0, 3)).reshape(depth, 3, 1, D)
    conv_w_full = jnp.transpose(small_all[:, n_ng:].reshape(N_DEV, depth, CONV_K, dsh), (1, 2, 0, 3)).reshape(depth, CONV_K, D)

    buckets = jnp.asarray(_bucket_tables())
    bias = _bias_build(rel_bias, buckets, "bias_build")

    saved = []
    xc = x0
    for l in range(depth):
        sv = {}
        for sub in (0, 1, 2):
            g, sh, sc, gt = norm_g_full[l, sub], mod[l, sub, 0], mod[l, sub, 1], mod[l, sub, 2]
            tag = f"l{l}s{sub}"
            h = _norm_mod_fwd(xc, g, sc, sh, "norm_mod_fwd")
            rec = dict(x=xc, h=h)
            if sub != 1:
                i = 0 if sub == 0 else 1
                au = _matmul(h, W[l]["gu"][i], "nn", BF16, "ffn_up")
                z = _swiglu_fwd(au, "swiglu_fwd")
                xc, f = _matmul(z, W[l]["down"][i], "nn", BF16, "ffn_down", tm=512, resid=(xc, gt, 0.5))
                rec.update(au=au, z=z, f=f)
            else:
                qkv = _matmul(h, W[l]["qkv"], "nn", BF16, "mixer_qkv")
                cg = _matmul(h, W[l]["cg"], "nn", BF16, "mixer_cg")
                os_, lses, views = [], [], []
                for gi, (_, dil) in enumerate(DILATION_GROUPS):
                    qv = _view(qkv[:, gi * QKV_G:(gi + 1) * QKV_G], dil)
                    o_g, lse_g = _attn_fwd(qv, bias[gi], dil, f"attn_fwd_g{gi}")
                    os_.append(_unview(o_g, dil))
                    lses.append(_unview(lse_g, dil))
                    views.append(qv)
                o_f, o_b, lse = _combine_fwd(os_, lses, "combine_fwd")
                yc_in = _conv_fwd(cg, conv_w_full[l], "conv_fwd")
                yc = _matmul(yc_in, W[l]["co"], "nn", BF16, "conv_out")
                ya = _matmul(o_b, W[l]["ao"], "nn", BF16, "attn_out")
                merged = _merge_fwd(cg, yc, ya, "merge_fwd")
                xc, f = _matmul(merged, W[l]["wo"], "nn", BF16, "mixer_out", resid=(xc, gt, 1.0))
                rec.update(views=views, cg=cg, o_f=o_f, o_b=o_b, lse=lse, yc_in=yc_in, yc=yc, ya=ya,
                           merged=merged, f=f)
            sv[sub] = rec
        saved.append(sv)

    dx, head = _loss_head(xc, final_g.reshape(1, D), target, "loss_head")
    d_final_g = head[0]
    loss_part = head[2, 0]

    d_mod = [[None] * 3 for _ in range(depth)]
    d_norm = [[None] * 3 for _ in range(depth)]
    d_conv = [None] * depth
    d_big = [dict() for _ in range(depth)]
    dlog = jnp.zeros((N_GROUPS, HEADS_PER_GROUP, BLOCK, 2 * BLOCK), F32)
    for l in reversed(range(depth)):
        for sub in (2, 1, 0):
            rec = saved[l][sub]
            g, sc, gt = norm_g_full[l, sub], mod[l, sub, 1], mod[l, sub, 2]
            if sub != 1:
                i = 0 if sub == 0 else 1
                df, gsum = _gate_bwd(dx, rec["f"], gt, 0.5, "gate_bwd")
                dz = _matmul(df, W[l]["down"][i], "nt", BF16, "ffn_down_dx")
                d_down = _matmul(rec["z"], df, "tn", BF16, "ffn_down_dw")
                dau = _swiglu_bwd(rec["au"], dz, "swiglu_bwd")
                dh = _matmul(dau, W[l]["gu"][i], "nt", BF16, "ffn_up_dx")
                d_gate, d_up = _deinterleave(_matmul(rec["h"], dau, "tn", BF16, "ffn_up_dw"))
                d_big[l].setdefault("gate", [None, None])[i] = d_gate
                d_big[l].setdefault("up", [None, None])[i] = d_up
                d_big[l].setdefault("down", [None, None])[i] = d_down
            else:
                dout, gsum = _gate_bwd(dx, rec["f"], gt, 1.0, "gate_bwd_mixer")
                dm = _matmul(dout, W[l]["wo"], "nt", BF16, "mixer_out_dx")
                d_big[l]["wo"] = _matmul(rec["merged"], dout, "tn", BF16, "mixer_out_dw")
                dyc, dya, dgg = _merge_bwd(rec["cg"], rec["yc"], rec["ya"], dm, "merge_bwd")
                dyc_in = _matmul(dyc, W[l]["co"], "nt", BF16, "conv_out_dx")
                d_big[l]["co"] = _matmul(rec["yc_in"], dyc, "tn", BF16, "conv_out_dw")
                do = _matmul(dya, W[l]["ao"], "nt", BF16, "attn_out_dx")
                d_big[l]["ao"] = _matmul(rec["o_b"], dya, "tn", BF16, "attn_out_dw")
                delta = _delta(do, rec["o_f"], "attn_delta")
                pieces = []
                dlog_l = []
                for gi, (_, dil) in enumerate(DILATION_GROUPS):
                    dq, dk, dv, dlg = _attn_bwd(rec["views"][gi], _view(do, dil), _view(rec["lse"], dil),
                                                _view(delta, dil), bias[gi], dil, f"attn_bwd_g{gi}")
                    pieces += [_unview(dq, dil), _unview(dk, dil), _unview(dv, dil)]
                    dlog_l.append(dlg)
                dlog = dlog + jnp.stack(dlog_l)
                dcc, conv_sum = _conv_bwd(rec["cg"], dyc_in, conv_w_full[l], "conv_bwd")
                d_conv[l] = conv_sum[0:CONV_K]
                du = jnp.concatenate(pieces + [dcc, dgg], axis=1)
                dh = _matmul(du, W[l]["in_all"], "nt", BF16, "mixer_in_dx")
                d_in = _matmul(rec["h"], du, "tn", BF16, "mixer_in_dw")
                nq = N_GROUPS * QKV_G
                d_big[l]["w_in"] = _unperm_w_in(d_in[:, :nq], d_in[:, nq:])
            dx, sums = _norm_mod_bwd(rec["x"], dh, dx, g, sc, "norm_mod_bwd")
            d_mod[l][sub] = jnp.stack([sums[0], sums[2], gsum[0]])
            d_norm[l][sub] = sums[3]
    grad_x = dx.reshape(1, S, D)
    d_rel = _bias_bwd(dlog, buckets, "bias_bwd")[:, :rel_bias.shape[1]]

    def to_chunks(name, l):
        gfull = d_big[l][name]
        if isinstance(gfull, list):
            gfull = jnp.stack(gfull)
        ax = dict(BIG)[name]
        return jnp.stack(jnp.split(gfull, N_DEV, axis=ax)).reshape(N_DEV, rows[name][1], PACK_W)

    send = jnp.concatenate(
        [jnp.concatenate([to_chunks(name, l) for name, _ in BIG], axis=1) for l in range(depth)], axis=1)
    recv = _all_to_all(send, "grads_all_to_all")
    g_sum = _sum_sources(recv, "grads_sum").reshape(depth, rows_per_layer, PACK_W)

    def grad_shard(name):
        off, n = rows[name]
        return g_sum[:, off:off + n].reshape((depth,) + tuple(shard_shapes[name]))

    d_mod_flat = jnp.stack([jnp.stack(d_mod[l]) for l in range(depth)]).reshape(-1)
    d_norm_flat = jnp.stack([jnp.stack(d_norm[l]) for l in range(depth)]).reshape(-1)
    d_conv_flat = jnp.stack(d_conv).reshape(-1)
    vec = jnp.concatenate([d_mod_flat, d_norm_flat, d_conv_flat, d_rel.reshape(-1), d_final_g,
                           jnp.broadcast_to(loss_part, (128,))])
    pad = (-vec.size) % 1024
    vec = jnp.concatenate([vec, jnp.zeros((pad,), F32)]).reshape(-1, 128)
    parts = _all_gather(vec, "small_grads_all_gather", in_vmem=True).reshape(N_DEV, vec.shape[0], 128)
    tot = _sum_rows8(parts, "small_grads_sum").reshape(-1)
    o0 = 0
    g_ada_b = tot[o0:o0 + d_mod_flat.size].reshape(ada_b.shape)
    o0 += d_mod_flat.size
    g_norm_full = tot[o0:o0 + d_norm_flat.size].reshape(depth, 3, D)
    o0 += d_norm_flat.size
    g_conv_full = tot[o0:o0 + d_conv_flat.size].reshape(depth, CONV_K, D)
    o0 += d_conv_flat.size
    g_rel = tot[o0:o0 + rel_bias.size].reshape(rel_bias.shape)
    o0 += rel_bias.size
    g_final = tot[o0:o0 + D]
    o0 += D
    loss = tot[o0]
    g_norm = lax.dynamic_slice_in_dim(g_norm_full, me * dsh, dsh, axis=2)
    g_conv = lax.dynamic_slice_in_dim(g_conv_full, me * dsh, dsh, axis=2)

    dm_all = parts.reshape(N_DEV, -1)[:, :d_mod_flat.size].reshape(N_DEV, depth, N_DEV * n_ada)
    dm_cols = lax.dynamic_slice_in_dim(dm_all, me * n_ada, n_ada, axis=2)
    dm16 = jnp.concatenate([jnp.transpose(dm_cols, (1, 0, 2)), jnp.zeros((depth, N_DEV, n_ada), F32)], axis=1)
    g_ada_w = _ada_bwd(cs16, dm16, "ada_bwd")

    grads = dict(ada_w=g_ada_w, ada_b=g_ada_b, norm_g=g_norm, ffn_w_gate=grad_shard("gate"),
                 ffn_w_up=grad_shard("up"), ffn_w_down=grad_shard("down"), w_in=grad_shard("w_in"),
                 conv_w=g_conv, w_conv_out=grad_shard("co"), w_attn_out=grad_shard("ao"), w_o=grad_shard("wo"),
                 rel_bias=g_rel, final_g=g_final)
    weights = dict(ada_w=ada_w, ada_b=ada_b, norm_g=norm_g, ffn_w_gate=ffn_w_gate, ffn_w_up=ffn_w_up,
                   ffn_w_down=ffn_w_down, w_in=w_in, conv_w=conv_w, w_conv_out=w_conv_out, w_attn_out=w_attn_out,
                   w_o=w_o, rel_bias=rel_bias, final_g=final_g)
    ms = dict(ada_w=m_ada_w, ada_b=m_ada_b, norm_g=m_norm_g, ffn_w_gate=m_ffn_w_gate, ffn_w_up=m_ffn_w_up,
              ffn_w_down=m_ffn_w_down, w_in=m_w_in, conv_w=m_conv_w, w_conv_out=m_w_conv_out,
              w_attn_out=m_w_attn_out, w_o=m_w_o, rel_bias=m_rel_bias, final_g=m_final_g)
    vs = dict(ada_w=v_ada_w, ada_b=v_ada_b, norm_g=v_norm_g, ffn_w_gate=v_ffn_w_gate, ffn_w_up=v_ffn_w_up,
              ffn_w_down=v_ffn_w_down, w_in=v_w_in, conv_w=v_conv_w, w_conv_out=v_w_conv_out,
              w_attn_out=v_w_attn_out, w_o=v_w_o, rel_bias=v_rel_bias, final_g=v_final_g)
    order = list(weights)
    deltas, new_m, new_v = [], [], []
    for name in order:
        d_, m_, v_ = _adamw(weights[name], grads[name], ms[name], vs[name], "adamw_" + name)
        deltas.append(d_)
        new_m.append(m_)
        new_v.append(v_)
    return (loss, grad_x, *[grads[n] for n in order], *deltas, *new_m, *new_v)
```

```python
import functools

import numpy as np
import jax
import jax.numpy as jnp
from jax import lax
from jax.experimental import pallas as pl
from jax.experimental.pallas import tpu as pltpu

F32 = jnp.float32
BF16 = jnp.bfloat16

N_DEV = 8
HEAD_DIM = 64
HEAD_SHIFT = 6
HEADS_PER_GROUP = 8
DILATION_GROUPS = ((128, 1), (512, 4), (2048, 16))
DILS = tuple(d for _, d in DILATION_GROUPS)
N_GROUPS = len(DILATION_GROUPS)
ATTN_OUT = HEADS_PER_GROUP * HEAD_DIM
QKV_W = N_GROUPS * ATTN_OUT
BLOCK = 128
NUM_BUCKETS = 32
MAX_DISTANCE = 2048
CONV_K = 3
EPS = 1e-6
NEG_INF = -1e30
SCALE = HEAD_DIM ** -0.5

ADAM_LR = 0.001
ADAM_B1 = 0.9
ADAM_B2 = 0.999
ADAM_EPS = 1e-08
ADAM_WD = 0.01
ADAM_STEP = 10

V7X_VMEM_LIMIT = 48 * 1024 * 1024
MESH = pl.DeviceIdType.MESH

NN = (((1,), (0,)), ((), ()))
NT = (((1,), (1,)), ((), ()))
TN = (((0,), (0,)), ((), ()))


def _pick(dim, cands):
    for c in cands:
        if dim % c == 0:
            return c
    return dim


def _pick_k(K, cap=2816):
    if K <= cap or K % 128:
        return K
    best = 128
    for m in range(1, K // 128 + 1):
        if (K // 128) % m == 0 and 128 * m <= cap:
            best = 128 * m
    return best


def _params(sem):
    return pltpu.CompilerParams(dimension_semantics=sem, vmem_limit_bytes=V7X_VMEM_LIMIT)


def _all_gather(x_shard, name):
    m_per, n = x_shard.shape

    def body(x_ref, out_ref, send_sems, recv_sems, local_sem):
        x, y, c = lax.axis_index("x"), lax.axis_index("y"), lax.axis_index("c")
        me, sibling = (x, y, c), (x, y, 1 - c)
        chips = [(1 - x, y), (x, 1 - y), (1 - x, 1 - y)]

        def rows(px, py, pc):
            return out_ref.at[pl.ds((4 * px + 2 * py + pc) * m_per, m_per), :]

        def copy(k, block, to, src=None):
            return pltpu.make_async_remote_copy(
                src_ref=rows(*block) if src is None else src, dst_ref=rows(*block),
                send_sem=send_sems.at[k], recv_sem=recv_sems.at[k], device_id=to, device_id_type=MESH)

        mine = pltpu.make_async_copy(x_ref, rows(*me), local_sem)
        mine.start()
        first = [copy(0, me, sibling, src=x_ref)]
        first += [copy(1 + j, me, (*chip, c), src=x_ref) for j, chip in enumerate(chips)]
        for cp in first:
            cp.start()
        passed = [copy(4 + j, (*chip, c), sibling) for j, chip in enumerate(chips)]
        for j, chip in enumerate(chips):
            copy(1 + j, (*chip, c), me).wait_recv()
            passed[j].start()
        copy(0, sibling, me).wait_recv()
        for j, chip in enumerate(chips):
            copy(4 + j, (*chip, 1 - c), me).wait_recv()
        for cp in first + passed:
            cp.wait_send()
        mine.wait()

    return pl.pallas_call(
        body, name=name,
        out_shape=jax.ShapeDtypeStruct((N_DEV * m_per, n), x_shard.dtype),
        in_specs=[pl.BlockSpec(memory_space=pltpu.VMEM)],
        out_specs=pl.BlockSpec(memory_space=pltpu.VMEM),
        scratch_shapes=[pltpu.SemaphoreType.DMA((7,)), pltpu.SemaphoreType.DMA((7,)), pltpu.SemaphoreType.DMA],
    )(x_shard)


def _offsets(piece_rows):
    offs, o = [], 0
    for n in piece_rows:
        offs.append(o)
        o += n
    return offs


def _all_gather_pieces(packed, piece_rows, name):
    R, w = packed.shape
    offs = _offsets(piece_rows)
    P = len(piece_rows)
    assert offs[-1] + piece_rows[-1] == R

    def body(src_ref, *rest):
        outs = rest[:P]
        send_sems, recv_sems, local_sems = rest[P:]
        x, y, c = lax.axis_index("x"), lax.axis_index("y"), lax.axis_index("c")
        me, sibling = (x, y, c), (x, y, 1 - c)
        chips = [(1 - x, y), (x, 1 - y), (1 - x, 1 - y)]

        def rows(p, px, py, pc):
            n = piece_rows[p]
            return outs[p].at[pl.ds((4 * px + 2 * py + pc) * n, n), :]

        def copies(k, block, to, from_src):
            cps = []
            for p in range(P):
                src = src_ref.at[pl.ds(offs[p], piece_rows[p]), :] if from_src else rows(p, *block)
                cps.append(pltpu.make_async_remote_copy(
                    src_ref=src, dst_ref=rows(p, *block), send_sem=send_sems.at[k], recv_sem=recv_sems.at[k],
                    device_id=to, device_id_type=MESH))
            return cps

        def whole(k):
            return pltpu.make_async_remote_copy(
                src_ref=src_ref, dst_ref=src_ref, send_sem=send_sems.at[k], recv_sem=recv_sems.at[k],
                device_id=me, device_id_type=MESH)

        mine = [pltpu.make_async_copy(src_ref.at[pl.ds(offs[p], piece_rows[p]), :], rows(p, *me), local_sems.at[p])
                for p in range(P)]
        for cp in mine:
            cp.start()
        for cp in copies(0, me, sibling, True):
            cp.start()
        for j, chip in enumerate(chips):
            for cp in copies(1 + j, me, (*chip, c), True):
                cp.start()
        for j, chip in enumerate(chips):
            whole(1 + j).wait_recv()
            for cp in copies(4 + j, (*chip, c), sibling, False):
                cp.start()
        whole(0).wait_recv()
        for j in range(3):
            whole(4 + j).wait_recv()
        for k in range(7):
            whole(k).wait_send()
        for cp in mine:
            cp.wait()

    hbm = pl.BlockSpec(memory_space=pltpu.HBM)
    return pl.pallas_call(
        body, name=name,
        out_shape=[jax.ShapeDtypeStruct((N_DEV * n, w), packed.dtype) for n in piece_rows],
        in_specs=[hbm], out_specs=[hbm] * P,
        scratch_shapes=[pltpu.SemaphoreType.DMA((7,)), pltpu.SemaphoreType.DMA((7,)), pltpu.SemaphoreType.DMA((P,))],
    )(packed)


def _all_to_all_pieces(pieces, name):
    P = len(pieces)
    w = pieces[0].shape[1]
    piece_rows = [p.shape[0] // N_DEV for p in pieces]
    offs = _offsets(piece_rows)
    R = offs[-1] + piece_rows[-1]

    def body(*refs):
        srcs = refs[:P]
        dst_ref = refs[P]
        send_sems, recv_sems, local_sems = refs[P + 1:]
        x, y, c = lax.axis_index("x"), lax.axis_index("y"), lax.axis_index("c")
        me = 4 * x + 2 * y + c

        def chunk(p, dev):
            return srcs[p].at[pl.ds(dev * piece_rows[p], piece_rows[p]), :]

        def slot(p, dev):
            return dst_ref.at[dev, pl.ds(offs[p], piece_rows[p]), :]

        mine = [pltpu.make_async_copy(chunk(p, me), slot(p, me), local_sems.at[p]) for p in range(P)]
        for cp in mine:
            cp.start()
        peers = []
        for k in range(1, N_DEV):
            px = 1 - x if (k >> 2) & 1 else x
            py = 1 - y if (k >> 1) & 1 else y
            pc = 1 - c if k & 1 else c
            peer = 4 * px + 2 * py + pc
            peers.append((peer, (px, py, pc)))
            for p in range(P):
                pltpu.make_async_remote_copy(
                    src_ref=chunk(p, peer), dst_ref=slot(p, me), send_sem=send_sems.at[k - 1],
                    recv_sem=recv_sems.at[k - 1], device_id=(px, py, pc), device_id_type=MESH).start()
        for k, (peer, pid) in enumerate(peers):
            whole = pltpu.make_async_remote_copy(
                src_ref=dst_ref.at[me], dst_ref=dst_ref.at[peer], send_sem=send_sems.at[k],
                recv_sem=recv_sems.at[k], device_id=pid, device_id_type=MESH)
            whole.wait_recv()
        for k, (peer, pid) in enumerate(peers):
            whole = pltpu.make_async_remote_copy(
                src_ref=dst_ref.at[me], dst_ref=dst_ref.at[peer], send_sem=send_sems.at[k],
                recv_sem=recv_sems.at[k], device_id=pid, device_id_type=MESH)
            whole.wait_send()
        for cp in mine:
            cp.wait()

    hbm = pl.BlockSpec(memory_space=pltpu.HBM)
    return pl.pallas_call(
        body, name=name,
        out_shape=jax.ShapeDtypeStruct((N_DEV, R, w), pieces[0].dtype),
        in_specs=[hbm] * P, out_specs=hbm,
        scratch_shapes=[pltpu.SemaphoreType.DMA((7,)), pltpu.SemaphoreType.DMA((7,)), pltpu.SemaphoreType.DMA((P,))],
    )(*pieces)


def _sum_sources(parts, name):
    _, r, n = parts.shape
    tr = _pick(r, [256, 128, 64, 32, 16, 8])

    def kern(p_ref, o_ref):
        acc = p_ref[0].astype(F32)
        for k in range(1, N_DEV):
            acc = acc + p_ref[k].astype(F32)
        o_ref[...] = acc

    return pl.pallas_call(
        kern, name=name, grid=(r // tr,),
        out_shape=jax.ShapeDtypeStruct((r, n), F32),
        in_specs=[pl.BlockSpec((N_DEV, tr, n), lambda i: (0, i, 0))],
        out_specs=pl.BlockSpec((tr, n), lambda i: (i, 0)),
        compiler_params=_params(("parallel",)),
    )(parts)


def _matmul(a, b, mode, out_dtype, name, tm=None, tn=None, tk=None, resid=None):
    if mode == "nn":
        (M, K), N = a.shape, b.shape[1]
    elif mode == "nt":
        (M, K), N = a.shape, b.shape[0]
    else:
        (K, M), N = a.shape, b.shape[1]
    dims = {"nn": NN, "nt": NT, "tn": TN}[mode]
    tm = tm or _pick(M, [1024, 1408, 512, 256, 128])
    tn = tn or _pick(N, [1024, 1408, 512, 256, 128])
    tk = tk or _pick_k(K)
    nk = K // tk
    a_spec = {"nn": pl.BlockSpec((tm, tk), lambda i, j, k: (i, k)),
              "nt": pl.BlockSpec((tm, tk), lambda i, j, k: (i, k)),
              "tn": pl.BlockSpec((tk, tm), lambda i, j, k: (k, i))}[mode]
    b_spec = {"nn": pl.BlockSpec((tk, tn), lambda i, j, k: (k, j)),
              "nt": pl.BlockSpec((tn, tk), lambda i, j, k: (j, k)),
              "tn": pl.BlockSpec((tk, tn), lambda i, j, k: (k, j))}[mode]
    o_spec = pl.BlockSpec((tm, tn), lambda i, j, k: (i, j))
    n_in = 2 if resid is None else 4
    n_out = 1 if resid is None else 2

    def kern(*refs):
        a_ref, b_ref = refs[0], refs[1]
        outs = refs[n_in:n_in + n_out]
        acc_ref = refs[n_in + n_out] if nk > 1 else None

        def finish(acc):
            if resid is None:
                outs[0][...] = acc.astype(out_dtype)
            else:
                x_ref, g_ref = refs[2], refs[3]
                outs[0][...] = x_ref[...] + (resid[2] * g_ref[...]) * acc
                outs[1][...] = acc.astype(out_dtype)

        part = lax.dot_general(a_ref[...], b_ref[...], dims, preferred_element_type=F32)
        if nk == 1:
            finish(part)
        else:
            k = pl.program_id(2)

            @pl.when(k == 0)
            def _():
                acc_ref[...] = part

            @pl.when(k > 0)
            def _():
                acc_ref[...] += part

            @pl.when(k == nk - 1)
            def _():
                finish(acc_ref[...])

    in_specs = [a_spec, b_spec]
    args = [a, b]
    out_shape = [jax.ShapeDtypeStruct((M, N), out_dtype)]
    out_specs = [o_spec]
    if resid is not None:
        in_specs += [o_spec, pl.BlockSpec((1, tn), lambda i, j, k: (0, j))]
        args += [resid[0], resid[1]]
        out_shape = [jax.ShapeDtypeStruct((M, N), F32)] + out_shape
        out_specs = [o_spec, o_spec]
    res = pl.pallas_call(
        kern, name=name, grid=(M // tm, N // tn, nk),
        out_shape=out_shape, in_specs=in_specs, out_specs=out_specs,
        scratch_shapes=[pltpu.VMEM((tm, tn), F32)] if nk > 1 else [],
        compiler_params=_params(("parallel", "parallel", "arbitrary")),
    )(*args)
    return res[0] if resid is None else res


def _dot3(a, b, dims):
    ah = a.astype(BF16)
    al = (a - ah.astype(F32)).astype(BF16)
    bh = b.astype(BF16)
    bl = (b - bh.astype(F32)).astype(BF16)
    d = functools.partial(lax.dot_general, dimension_numbers=dims, preferred_element_type=F32)
    return d(ah, bh) + (d(ah, bl) + d(al, bh))


def _silu_parts(a):
    sg = jax.nn.sigmoid(a)
    return a * sg, sg * (1.0 + a * (1.0 - sg))


def _ffn_up(h, wg_t, wu_t, name):
    S, D = h.shape
    F = wg_t.shape[0]
    tm = _pick(S, [512, 256, 128])
    tn = _pick(F, [1408, 512, 256, 128])

    def kern(h_ref, g_ref, u_ref, a_out, u_out, z_out):
        hv = h_ref[...]
        a = lax.dot_general(hv, g_ref[...], NT, preferred_element_type=F32)
        u = lax.dot_general(hv, u_ref[...], NT, preferred_element_type=F32)
        a_out[...] = a.astype(BF16)
        u_out[...] = u.astype(BF16)
        z_out[...] = (_silu_parts(a)[0] * u).astype(BF16)

    w_spec = pl.BlockSpec((tn, D), lambda j, i: (j, 0))
    o_spec = pl.BlockSpec((tm, tn), lambda j, i: (i, j))
    return pl.pallas_call(
        kern, name=name, grid=(F // tn, S // tm),
        out_shape=[jax.ShapeDtypeStruct((S, F), BF16)] * 3,
        in_specs=[pl.BlockSpec((tm, D), lambda j, i: (i, 0)), w_spec, w_spec],
        out_specs=[o_spec] * 3,
        compiler_params=_params(("parallel", "parallel")),
    )(h, wg_t, wu_t)


def _ffn_up_bwd(dz, a, u, wg_t, wu_t, name):
    S, F = dz.shape
    D = wg_t.shape[1]
    tm = _pick(S, [512, 256, 128])
    tk = _pick(F, [1408, 512, 256, 128])
    nk = F // tk

    def kern(dz_ref, a_ref, u_ref, g_ref, w_ref, da_out, du_out, dh_out, acc_ref):
        k = pl.program_id(1)
        av = a_ref[...].astype(F32)
        uv = u_ref[...].astype(F32)
        dzv = dz_ref[...].astype(F32)
        silu, dsilu = _silu_parts(av)
        da = (dzv * uv * dsilu).astype(BF16)
        du = (dzv * silu).astype(BF16)
        da_out[...] = da
        du_out[...] = du
        part = (lax.dot_general(da, g_ref[...], NN, preferred_element_type=F32)
                + lax.dot_general(du, w_ref[...], NN, preferred_element_type=F32))

        @pl.when(k == 0)
        def _():
            acc_ref[...] = part

        @pl.when(k > 0)
        def _():
            acc_ref[...] += part

        @pl.when(k == nk - 1)
        def _():
            dh_out[...] = acc_ref[...]

    t_spec = pl.BlockSpec((tm, tk), lambda i, k: (i, k))
    w_spec = pl.BlockSpec((tk, D), lambda i, k: (k, 0))
    return pl.pallas_call(
        kern, name=name, grid=(S // tm, nk),
        out_shape=[jax.ShapeDtypeStruct((S, F), BF16)] * 2 + [jax.ShapeDtypeStruct((S, D), F32)],
        in_specs=[t_spec, t_spec, t_spec, w_spec, w_spec],
        out_specs=[t_spec, t_spec, pl.BlockSpec((tm, D), lambda i, k: (i, 0))],
        scratch_shapes=[pltpu.VMEM((tm, D), F32)],
        compiler_params=_params(("parallel", "arbitrary")),
    )(dz, a, u, wg_t, wu_t)


def _attn_dh(dq, dkv, w_t, name):
    S = dq.shape[0]
    D = w_t.shape[1]
    tm = _pick(S, [1024, 512, 256, 128])

    def kern(dq_ref, dk_ref, dv_ref, wq_ref, wk_ref, wv_ref, o_ref):
        o_ref[...] = (lax.dot_general(dq_ref[...], wq_ref[...], NN, preferred_element_type=F32)
                      + lax.dot_general(dk_ref[...], wk_ref[...], NN, preferred_element_type=F32)
                      + lax.dot_general(dv_ref[...], wv_ref[...], NN, preferred_element_type=F32))

    def w_blk(j):
        return pl.BlockSpec((ATTN_OUT, D), lambda i: (j, 0))

    return pl.pallas_call(
        kern, name=name, grid=(S // tm,),
        out_shape=jax.ShapeDtypeStruct((S, D), F32),
        in_specs=[pl.BlockSpec((tm, ATTN_OUT), lambda i: (i, 0)), pl.BlockSpec((tm, ATTN_OUT), lambda i: (i, 0)),
                  pl.BlockSpec((tm, ATTN_OUT), lambda i: (i, 1)), w_blk(0), w_blk(1), w_blk(2)],
        out_specs=pl.BlockSpec((tm, D), lambda i: (i, 0)),
        compiler_params=_params(("parallel",)),
    )(dq, dkv, dkv, w_t, w_t, w_t)


def _row_spec(tm, d):
    return pl.BlockSpec((tm, d), lambda i: (i, 0))


def _vec_spec(d, rows=1):
    return pl.BlockSpec((rows, d), lambda i: (0, 0))


def _perm_spec(dil, tm, w):
    return pl.BlockSpec((dil, tm // dil, w), lambda i: (0, i, 0))


def _stage_shape(tm, w):
    return pltpu.VMEM((w // 128, tm, 128), F32)


def _stage(scr, val):
    for ci in range(scr.shape[0]):
        scr[ci] = val[:, 128 * ci:128 * (ci + 1)]


def _unstage(scr):
    return jnp.concatenate([scr[ci] for ci in range(scr.shape[0])], axis=1)


def _get_residue(scr, res, dil):
    n = scr.shape[1] // dil
    return jnp.concatenate([scr[ci, pl.ds(res, n, stride=dil), :] for ci in range(scr.shape[0])], axis=1)


def _put_residue(scr, res, dil, val):
    n = scr.shape[1] // dil
    for ci in range(scr.shape[0]):
        scr[ci, pl.ds(res, n, stride=dil), :] = val[:, 128 * ci:128 * (ci + 1)]


def _norm_mod_fwd(x, g, s, b, name, dils=()):
    S, D = x.shape
    tm = _pick(S, [256, 128])

    def kern(x_ref, g_ref, s_ref, b_ref, h_ref, *rest):
        xv = x_ref[...]
        r = lax.rsqrt(jnp.mean(xv * xv, axis=1, keepdims=True) + EPS)
        hv = xv * r * g_ref[...] * (1.0 + s_ref[...]) + b_ref[...]
        h_ref[...] = hv.astype(BF16)
        if dils:
            scr = rest[len(dils)]
            _stage(scr, hv)
            for dil, p_ref in zip(dils, rest[:len(dils)]):
                for res in range(dil):
                    p_ref[res] = _get_residue(scr, res, dil).astype(BF16)

    return pl.pallas_call(
        kern, name=name, grid=(S // tm,),
        out_shape=[jax.ShapeDtypeStruct((S, D), BF16)] + [jax.ShapeDtypeStruct((dil, S // dil, D), BF16) for dil in dils],
        in_specs=[_row_spec(tm, D), _vec_spec(D), _vec_spec(D), _vec_spec(D)],
        out_specs=[_row_spec(tm, D)] + [_perm_spec(dil, tm, D) for dil in dils],
        scratch_shapes=[_stage_shape(tm, D)] if dils else [],
        compiler_params=_params(("parallel",)),
    )(x, g, s, b)


def _norm_mod_bwd(x, dh_nat, dh_perm, dxo, g, s, name):
    S, D = x.shape
    tm = _pick(S, [256, 128])
    n = S // tm
    n_nat, n_perm = len(dh_nat), len(dh_perm)

    def kern(*refs):
        x_ref = refs[0]
        nat = refs[1:1 + n_nat]
        perm = refs[1 + n_nat:1 + n_nat + n_perm]
        dxo_ref, g_ref, s_ref, dx_ref, cs_ref = refs[1 + n_nat + n_perm:6 + n_nat + n_perm]
        scr = refs[6 + n_nat + n_perm:]
        i = pl.program_id(0)
        xv = x_ref[...]
        r = lax.rsqrt(jnp.mean(xv * xv, axis=1, keepdims=True) + EPS)
        xn = xv * r
        dh_v = nat[0][...].astype(F32)
        for t in nat[1:]:
            dh_v = dh_v + t[...].astype(F32)
        for (dil, _), p_ref, sc in zip(dh_perm, perm, scr):
            for res in range(dil):
                _put_residue(sc, res, dil, p_ref[res])
            dh_v = dh_v + _unstage(sc)
        one_s = 1.0 + s_ref[...]
        dxn = dh_v * (g_ref[...] * one_s)
        dx_ref[...] = dxo_ref[...] + r * (dxn - xn * jnp.mean(xn * dxn, axis=1, keepdims=True))

        @pl.when(i == 0)
        def _():
            cs_ref[...] = jnp.zeros_like(cs_ref)

        cs_ref[0:1, :] += jnp.sum(dh_v, axis=0, keepdims=True)
        cs_ref[1:2, :] += jnp.sum(dh_v * xn, axis=0, keepdims=True)

        @pl.when(i == n - 1)
        def _():
            t = cs_ref[1:2, :]
            cs_ref[2:3, :] = g_ref[...] * t
            cs_ref[3:4, :] = one_s * t

    return pl.pallas_call(
        kern, name=name, grid=(n,),
        out_shape=[jax.ShapeDtypeStruct((S, D), F32), jax.ShapeDtypeStruct((8, D), F32)],
        in_specs=[_row_spec(tm, D)] + [_row_spec(tm, D)] * n_nat + [_perm_spec(dil, tm, D) for dil, _ in dh_perm]
        + [_row_spec(tm, D), _vec_spec(D), _vec_spec(D)],
        out_specs=[_row_spec(tm, D), _vec_spec(D, 8)],
        scratch_shapes=[_stage_shape(tm, D) for _ in dh_perm],
        compiler_params=_params(("arbitrary",)),
    )(x, *dh_nat, *[a for _, a in dh_perm], dxo, g, s)


def _gate_bwd(dxo, f, gate, coef, name):
    S, D = dxo.shape
    tm = _pick(S, [512, 256, 128])

    def kern(dxo_ref, f_ref, gate_ref, df_ref, cs_ref):
        i = pl.program_id(0)
        dv = dxo_ref[...]
        df_ref[...] = ((coef * gate_ref[...]) * dv).astype(BF16)

        @pl.when(i == 0)
        def _():
            cs_ref[...] = jnp.zeros_like(cs_ref)

        cs_ref[0:1, :] += coef * jnp.sum(f_ref[...].astype(F32) * dv, axis=0, keepdims=True)

    return pl.pallas_call(
        kern, name=name, grid=(S // tm,),
        out_shape=[jax.ShapeDtypeStruct((S, D), BF16), jax.ShapeDtypeStruct((8, D), F32)],
        in_specs=[_row_spec(tm, D), _row_spec(tm, D), _vec_spec(D)],
        out_specs=[_row_spec(tm, D), _vec_spec(D, 8)],
        compiler_params=_params(("arbitrary",)),
    )(dxo, f, gate)


def _loss_head(x, g, target, name):
    S, D = x.shape
    tm = _pick(S, [256, 128])
    n = S // tm

    def kern(x_ref, g_ref, t_ref, dx_ref, cs_ref):
        i = pl.program_id(0)
        xv = x_ref[...]
        r = lax.rsqrt(jnp.mean(xv * xv, axis=1, keepdims=True) + EPS)
        xn = xv * r
        e = xn * g_ref[...] - t_ref[...]
        dxn = (e * (1.0 / D)) * g_ref[...]
        dx_ref[...] = r * (dxn - xn * jnp.mean(xn * dxn, axis=1, keepdims=True))

        @pl.when(i == 0)
        def _():
            cs_ref[...] = jnp.zeros_like(cs_ref)

        cs_ref[0:1, :] += jnp.sum(xn * e, axis=0, keepdims=True) * (1.0 / D)
        cs_ref[1:2, :] += jnp.sum(e * e, axis=0, keepdims=True)

        @pl.when(i == n - 1)
        def _():
            tot = jnp.sum(cs_ref[1:2, :], axis=1, keepdims=True) * (0.5 / D)
            cs_ref[2:3, :] = jnp.broadcast_to(tot, (1, D))

    return pl.pallas_call(
        kern, name=name, grid=(n,),
        out_shape=[jax.ShapeDtypeStruct((S, D), F32), jax.ShapeDtypeStruct((8, D), F32)],
        in_specs=[_row_spec(tm, D), _vec_spec(D), _row_spec(tm, D)],
        out_specs=[_row_spec(tm, D), _vec_spec(D, 8)],
        compiler_params=_params(("arbitrary",)),
    )(x, g, target)


def _shift_down(p, row, prev_rows):
    a, b = prev_rows
    p1 = jnp.where(row == 0, b, pltpu.roll(p, 1, 0))
    p2 = jnp.where(row == 0, a, jnp.where(row == 1, b, pltpu.roll(p, 2, 0)))
    return p1, p2


def _conv_fwd(cg, conv_w, name):
    S, D5 = cg.shape
    D = D5 // 5
    tm = _pick(S, [256, 128])
    t8 = tm // 8

    def prev(col):
        return pl.BlockSpec((8, D), lambda i: (jnp.maximum(i * t8 - 1, 0), col))

    def kern(cb_ref, cc_ref, ch_ref, ccp_ref, chp_ref, w_ref, y_ref):
        i = pl.program_id(0)
        keep = jnp.where(i > 0, 1.0, 0.0)
        p = cc_ref[...].astype(F32) * ch_ref[...].astype(F32)
        pa = ccp_ref[6:7, :].astype(F32) * chp_ref[6:7, :].astype(F32) * keep
        pb = ccp_ref[7:8, :].astype(F32) * chp_ref[7:8, :].astype(F32) * keep
        row = lax.broadcasted_iota(jnp.int32, (tm, D), 0)
        p1, p2 = _shift_down(p, row, (pa, pb))
        dw = w_ref[0:1, :] * p2 + w_ref[1:2, :] * p1 + w_ref[2:3, :] * p
        y_ref[...] = (cb_ref[...].astype(F32) * dw).astype(BF16)

    def col(cidx):
        return pl.BlockSpec((tm, D), lambda i: (i, cidx))

    return pl.pallas_call(
        kern, name=name, grid=(S // tm,),
        out_shape=jax.ShapeDtypeStruct((S, D), BF16),
        in_specs=[col(0), col(1), col(2), prev(1), prev(2), _vec_spec(D, CONV_K)],
        out_specs=_row_spec(tm, D),
        compiler_params=_params(("parallel",)),
    )(cg, cg, cg, cg, cg, conv_w)


def _conv_bwd(cg, dy, dgg, conv_w, name):
    S, D5 = cg.shape
    D = D5 // 5
    tm = _pick(S, [256, 128])
    t8 = tm // 8
    n = S // tm
    last8 = S // 8 - 1

    def prev(col):
        return pl.BlockSpec((8, D), lambda i: (jnp.maximum(i * t8 - 1, 0), col))

    def nxt(col):
        return pl.BlockSpec((8, D), lambda i: (jnp.minimum((i + 1) * t8, last8), col))

    def kern(cb_ref, cc_ref, ch_ref, dy_ref, dgg_ref, ccp_ref, chp_ref, cbn_ref, dyn_ref, w_ref, d_ref, cs_ref):
        i = pl.program_id(0)
        keep_p = jnp.where(i > 0, 1.0, 0.0)
        keep_n = jnp.where(i < n - 1, 1.0, 0.0)
        cb = cb_ref[...].astype(F32)
        cc = cc_ref[...].astype(F32)
        ch = ch_ref[...].astype(F32)
        dyv = dy_ref[...].astype(F32)
        p = cc * ch
        pa = ccp_ref[6:7, :].astype(F32) * chp_ref[6:7, :].astype(F32) * keep_p
        pb = ccp_ref[7:8, :].astype(F32) * chp_ref[7:8, :].astype(F32) * keep_p
        row = lax.broadcasted_iota(jnp.int32, (tm, D), 0)
        p1, p2 = _shift_down(p, row, (pa, pb))
        w0, w1, w2 = w_ref[0:1, :], w_ref[1:2, :], w_ref[2:3, :]
        dw = w0 * p2 + w1 * p1 + w2 * p
        ddw = dyv * cb
        na = dyn_ref[0:1, :].astype(F32) * cbn_ref[0:1, :].astype(F32) * keep_n
        nb = dyn_ref[1:2, :].astype(F32) * cbn_ref[1:2, :].astype(F32) * keep_n
        u1 = jnp.where(row == tm - 1, na, pltpu.roll(ddw, tm - 1, 0))
        u2 = jnp.where(row == tm - 2, na, jnp.where(row == tm - 1, nb, pltpu.roll(ddw, tm - 2, 0)))
        dp = w2 * ddw + w1 * u1 + w0 * u2
        d_ref[:, 0:D] = (dyv * dw).astype(BF16)
        d_ref[:, D:2 * D] = (dp * ch).astype(BF16)
        d_ref[:, 2 * D:3 * D] = (dp * cc).astype(BF16)
        d_ref[:, 3 * D:5 * D] = dgg_ref[...]

        @pl.when(i == 0)
        def _():
            cs_ref[...] = jnp.zeros_like(cs_ref)

        cs_ref[0:1, :] += jnp.sum(ddw * p2, axis=0, keepdims=True)
        cs_ref[1:2, :] += jnp.sum(ddw * p1, axis=0, keepdims=True)
        cs_ref[2:3, :] += jnp.sum(ddw * p, axis=0, keepdims=True)

    def col(cidx):
        return pl.BlockSpec((tm, D), lambda i: (i, cidx))

    return pl.pallas_call(
        kern, name=name, grid=(n,),
        out_shape=[jax.ShapeDtypeStruct((S, 5 * D), BF16), jax.ShapeDtypeStruct((8, D), F32)],
        in_specs=[col(0), col(1), col(2), _row_spec(tm, D), _row_spec(tm, 2 * D), prev(1), prev(2), nxt(0),
                  pl.BlockSpec((8, D), lambda i: (jnp.minimum((i + 1) * t8, last8), 0)), _vec_spec(D, CONV_K)],
        out_specs=[_row_spec(tm, 5 * D), _vec_spec(D, 8)],
        compiler_params=_params(("arbitrary",)),
    )(cg, cg, cg, dy, dgg, cg, cg, cg, dy, conv_w)


def _merge_fwd(cg, yc, ya, name):
    S, D = yc.shape
    tm = _pick(S, [512, 256, 128])

    def kern(gc_ref, ga_ref, yc_ref, ya_ref, m_ref):
        m_ref[...] = (jax.nn.sigmoid(gc_ref[...].astype(F32)) * yc_ref[...].astype(F32)
                      + jax.nn.sigmoid(ga_ref[...].astype(F32)) * ya_ref[...].astype(F32)).astype(BF16)

    return pl.pallas_call(
        kern, name=name, grid=(S // tm,),
        out_shape=jax.ShapeDtypeStruct((S, D), BF16),
        in_specs=[pl.BlockSpec((tm, D), lambda i: (i, 3)), pl.BlockSpec((tm, D), lambda i: (i, 4)),
                  _row_spec(tm, D), _row_spec(tm, D)],
        out_specs=_row_spec(tm, D),
        compiler_params=_params(("parallel",)),
    )(cg, cg, yc, ya)


def _merge_bwd(cg, yc, ya, dm, name):
    S, D = yc.shape
    tm = _pick(S, [256, 128])

    def kern(gc_ref, ga_ref, yc_ref, ya_ref, dm_ref, dyc_ref, dya_ref, dg_ref):
        dmv = dm_ref[...].astype(F32)
        sc = jax.nn.sigmoid(gc_ref[...].astype(F32))
        sa = jax.nn.sigmoid(ga_ref[...].astype(F32))
        dyc_ref[...] = (dmv * sc).astype(BF16)
        dya_ref[...] = (dmv * sa).astype(BF16)
        dg_ref[:, 0:D] = (dmv * yc_ref[...].astype(F32) * (sc * (1.0 - sc))).astype(BF16)
        dg_ref[:, D:2 * D] = (dmv * ya_ref[...].astype(F32) * (sa * (1.0 - sa))).astype(BF16)

    return pl.pallas_call(
        kern, name=name, grid=(S // tm,),
        out_shape=[jax.ShapeDtypeStruct((S, D), BF16), jax.ShapeDtypeStruct((S, D), BF16),
                   jax.ShapeDtypeStruct((S, 2 * D), BF16)],
        in_specs=[pl.BlockSpec((tm, D), lambda i: (i, 3)), pl.BlockSpec((tm, D), lambda i: (i, 4)),
                  _row_spec(tm, D), _row_spec(tm, D), _row_spec(tm, D)],
        out_specs=[_row_spec(tm, D), _row_spec(tm, D), pl.BlockSpec((tm, 2 * D), lambda i: (i, 0))],
        compiler_params=_params(("parallel",)),
    )(cg, cg, yc, ya, dm)


def _t5_bucket(dist):
    exact = NUM_BUCKETS // 2
    d = np.maximum(dist, 1).astype(np.float32)
    large = exact + (np.log(d / exact) / np.log(MAX_DISTANCE / exact) * (NUM_BUCKETS - exact)).astype(np.int32)
    large = np.minimum(large, NUM_BUCKETS - 1)
    return np.where(dist < exact, dist, large).astype(np.int32)


def _bucket_tables():
    i = np.arange(BLOCK)[:, None]
    j = np.arange(2 * BLOCK)[None, :]
    rel = i - j + BLOCK
    return np.stack([_t5_bucket(np.maximum(rel, 0) * d) for _, d in DILATION_GROUPS]).astype(np.int32)


def _band_masks():
    i = lax.broadcasted_iota(jnp.int32, (BLOCK, 2 * BLOCK), 0)
    j = lax.broadcasted_iota(jnp.int32, (BLOCK, 2 * BLOCK), 1)
    rel = i - j + BLOCK
    band = (rel >= 0) & (rel <= BLOCK)
    return band, band & (j >= BLOCK)


def _bias_build(rel_bias, buckets, name):
    def kern(rb_ref, bk_ref, o_ref):
        g = pl.program_id(0)
        bk = bk_ref[0]
        band, first = _band_masks()
        for h in range(HEADS_PER_GROUP):
            acc = jnp.zeros((BLOCK, 2 * BLOCK), F32)
            for b in range(NUM_BUCKETS):
                acc = jnp.where(bk == b, rb_ref[b, g * HEADS_PER_GROUP + h], acc)
            o_ref[0, 0, h] = jnp.where(first, acc, NEG_INF)
            o_ref[0, 1, h] = jnp.where(band, acc, NEG_INF)

    return pl.pallas_call(
        kern, name=name, grid=(N_GROUPS,),
        out_shape=jax.ShapeDtypeStruct((N_GROUPS, 2, HEADS_PER_GROUP, BLOCK, 2 * BLOCK), F32),
        in_specs=[pl.BlockSpec(memory_space=pltpu.SMEM),
                  pl.BlockSpec((1, BLOCK, 2 * BLOCK), lambda g: (g, 0, 0))],
        out_specs=pl.BlockSpec((1, 2, HEADS_PER_GROUP, BLOCK, 2 * BLOCK), lambda g: (g, 0, 0, 0, 0)),
        compiler_params=_params(("parallel",)),
    )(rel_bias, buckets)


def _bias_bwd(dlog, buckets, name):
    def kern(dl_ref, bk_ref, o_ref):
        g = pl.program_id(0)
        bk = bk_ref[0]
        rowi = lax.broadcasted_iota(jnp.int32, (NUM_BUCKETS, 128), 0)
        coli = lax.broadcasted_iota(jnp.int32, (NUM_BUCKETS, 128), 1)

        @pl.when(g == 0)
        def _():
            o_ref[...] = jnp.zeros_like(o_ref)

        acc = jnp.zeros((NUM_BUCKETS, 128), F32)
        for h in range(HEADS_PER_GROUP):
            dv = dl_ref[0, h]
            for b in range(NUM_BUCKETS):
                t = jnp.sum(jnp.where(bk == b, dv, 0.0), axis=0, keepdims=True)
                t = jnp.sum(t, axis=1, keepdims=True)
                acc = acc + jnp.where((rowi == b) & (coli == g * HEADS_PER_GROUP + h), t, 0.0)
        o_ref[...] += acc

    return pl.pallas_call(
        kern, name=name, grid=(N_GROUPS,),
        out_shape=jax.ShapeDtypeStruct((NUM_BUCKETS, 128), F32),
        in_specs=[pl.BlockSpec((1, HEADS_PER_GROUP, BLOCK, 2 * BLOCK), lambda g: (g, 0, 0, 0)),
                  pl.BlockSpec((1, BLOCK, 2 * BLOCK), lambda g: (g, 0, 0))],
        out_specs=pl.BlockSpec((NUM_BUCKETS, 128), lambda g: (0, 0)),
        compiler_params=_params(("arbitrary",)),
    )(dlog, buckets)


def _head_masks():
    lane = lax.broadcasted_iota(jnp.int32, (BLOCK, 128), 1)
    lo = lane < HEAD_DIM
    return lo, jnp.logical_not(lo)


def _attn_fwd(qkv, bias, d, name):
    S = qkv.shape[0]
    nb = S // d // BLOCK

    def kern(q_ref, kp_ref, kc_ref, vp_ref, vc_ref, b_ref, o_ref, lse_ref):
        lo, hi = _head_masks()
        for p in range(HEADS_PER_GROUP // 2):
            sl = slice(128 * p, 128 * (p + 1))
            q = q_ref[:, sl]
            k = jnp.concatenate([kp_ref[:, sl], kc_ref[:, sl]], axis=0)
            v = jnp.concatenate([vp_ref[:, sl], vc_ref[:, sl]], axis=0)
            o2, l2 = [], []
            for hh, msk in enumerate((lo, hi)):
                qm = jnp.where(msk, q, jnp.zeros_like(q))
                s = lax.dot_general(qm, k, NT, preferred_element_type=F32) * SCALE + b_ref[0, 2 * p + hh]
                m = jnp.max(s, axis=1, keepdims=True)
                e = jnp.exp(s - m)
                l = jnp.sum(e, axis=1, keepdims=True)
                o2.append(lax.dot_general(e.astype(BF16), v, NN, preferred_element_type=F32) / l)
                l2.append(jnp.broadcast_to(m + jnp.log(l), (BLOCK, 128)))
            o_ref[:, sl] = jnp.where(lo, o2[0], o2[1])
            lse_ref[:, sl] = jnp.where(lo, l2[0], l2[1])

    def blk(col, prev):
        if prev:
            return pl.BlockSpec((BLOCK, ATTN_OUT), lambda r, n: (r * nb + jnp.maximum(n - 1, 0), col))
        return pl.BlockSpec((BLOCK, ATTN_OUT), lambda r, n: (r * nb + n, col))

    o_spec = pl.BlockSpec((BLOCK, ATTN_OUT), lambda r, n: (r * nb + n, 0))
    return pl.pallas_call(
        kern, name=name, grid=(d, nb),
        out_shape=[jax.ShapeDtypeStruct((S, ATTN_OUT), F32)] * 2,
        in_specs=[blk(0, False), blk(1, True), blk(1, False), blk(2, True), blk(2, False),
                  pl.BlockSpec((1, HEADS_PER_GROUP, BLOCK, 2 * BLOCK), lambda r, n: (jnp.minimum(n, 1), 0, 0, 0))],
        out_specs=[o_spec, o_spec],
        compiler_params=_params(("parallel", "arbitrary")),
    )(qkv, qkv, qkv, qkv, qkv, bias)


def _attn_bwd(qkv, do, lse, delta, bias, d, name):
    S = qkv.shape[0]
    nb = S // d // BLOCK
    low = -3.0e38

    def kern(q_ref, kp_ref, kc_ref, vp_ref, vc_ref, do_ref, lse_ref, dl_ref, b_ref,
             dq_ref, dkv_ref, db_ref, ck_ref, cv_ref):
        r, n = pl.program_id(0), pl.program_id(1)

        @pl.when((r == 0) & (n == 0))
        def _():
            db_ref[...] = jnp.zeros_like(db_ref)

        @pl.when(n == 0)
        def _():
            ck_ref[...] = jnp.zeros_like(ck_ref)
            cv_ref[...] = jnp.zeros_like(cv_ref)

        @pl.when(n < nb)
        def _():
            lo, hi = _head_masks()
            for p in range(HEADS_PER_GROUP // 2):
                sl = slice(128 * p, 128 * (p + 1))
                sv = slice(ATTN_OUT + 128 * p, ATTN_OUT + 128 * (p + 1))
                q = q_ref[:, sl]
                k = jnp.concatenate([kp_ref[:, sl], kc_ref[:, sl]], axis=0)
                v = jnp.concatenate([vp_ref[:, sl], vc_ref[:, sl]], axis=0)
                dov = do_ref[:, sl]
                lse_b = lse_ref[:, sl]
                del_b = dl_ref[:, sl]
                dq2 = []
                dk_acc = jnp.zeros((2 * BLOCK, 128), F32)
                dv_acc = jnp.zeros((2 * BLOCK, 128), F32)
                for hh, msk in enumerate((lo, hi)):
                    qm = jnp.where(msk, q, jnp.zeros_like(q))
                    dom = jnp.where(msk, dov, jnp.zeros_like(dov))
                    lse_h = jnp.max(jnp.where(msk, lse_b, low), axis=1, keepdims=True)
                    del_h = jnp.max(jnp.where(msk, del_b, low), axis=1, keepdims=True)
                    s = lax.dot_general(qm, k, NT, preferred_element_type=F32) * SCALE + b_ref[0, 2 * p + hh]
                    pr = jnp.exp(s - lse_h)
                    dp = lax.dot_general(dom, v, NT, preferred_element_type=F32)
                    ds = pr * (dp - del_h)
                    db_ref[2 * p + hh] += ds
                    dsb = (ds * SCALE).astype(BF16)
                    dq2.append(lax.dot_general(dsb, k, NN, preferred_element_type=F32))
                    dk_acc = dk_acc + lax.dot_general(dsb, qm, TN, preferred_element_type=F32)
                    dv_acc = dv_acc + lax.dot_general(pr.astype(BF16), dom, TN, preferred_element_type=F32)
                dq_ref[:, sl] = jnp.where(lo, dq2[0], dq2[1]).astype(BF16)
                dkv_ref[:, sl] = (ck_ref[:, sl] + dk_acc[0:BLOCK]).astype(BF16)
                dkv_ref[:, sv] = (cv_ref[:, sl] + dv_acc[0:BLOCK]).astype(BF16)
                ck_ref[:, sl] = dk_acc[BLOCK:2 * BLOCK]
                cv_ref[:, sl] = dv_acc[BLOCK:2 * BLOCK]

        @pl.when(n == nb)
        def _():
            dkv_ref[:, 0:ATTN_OUT] = ck_ref[...].astype(BF16)
            dkv_ref[:, ATTN_OUT:2 * ATTN_OUT] = cv_ref[...].astype(BF16)

    def cur(n):
        return jnp.minimum(n, nb - 1)

    def blk(col, prev):
        if prev:
            return pl.BlockSpec((BLOCK, ATTN_OUT), lambda r, n: (r * nb + jnp.maximum(cur(n) - 1, 0), col))
        return pl.BlockSpec((BLOCK, ATTN_OUT), lambda r, n: (r * nb + cur(n), col))

    q_like = pl.BlockSpec((BLOCK, ATTN_OUT), lambda r, n: (r * nb + cur(n), 0))
    return pl.pallas_call(
        kern, name=name, grid=(d, nb + 1),
        out_shape=[jax.ShapeDtypeStruct((S, ATTN_OUT), BF16), jax.ShapeDtypeStruct((S, 2 * ATTN_OUT), BF16),
                   jax.ShapeDtypeStruct((HEADS_PER_GROUP, BLOCK, 2 * BLOCK), F32)],
        in_specs=[blk(0, False), blk(1, True), blk(1, False), blk(2, True), blk(2, False),
                  q_like, q_like, q_like,
                  pl.BlockSpec((1, HEADS_PER_GROUP, BLOCK, 2 * BLOCK),
                               lambda r, n: (jnp.minimum(cur(n), 1), 0, 0, 0))],
        out_specs=[q_like,
                   pl.BlockSpec((BLOCK, 2 * ATTN_OUT), lambda r, n: (r * nb + jnp.maximum(n - 1, 0), 0)),
                   pl.BlockSpec((HEADS_PER_GROUP, BLOCK, 2 * BLOCK), lambda r, n: (0, 0, 0))],
        scratch_shapes=[pltpu.VMEM((BLOCK, ATTN_OUT), F32), pltpu.VMEM((BLOCK, ATTN_OUT), F32)],
        compiler_params=_params(("arbitrary", "arbitrary")),
    )(qkv, qkv, qkv, qkv, qkv, do, lse, delta, bias)


def _by_residue(a, dil):
    return a if dil == 1 else a.reshape(dil, a.shape[0] // dil, a.shape[1])


def _flat(a):
    return a if a.ndim == 2 else a.reshape(a.shape[0] * a.shape[1], a.shape[2])


def _combine_fwd(os_, lses, name):
    S, W = os_[0].shape
    tm = _pick(S, [256, 128])
    perm = [dil for dil in DILS if dil > 1]

    def kern(*refs):
        o_in, l_in = refs[0:N_GROUPS], refs[N_GROUPS:2 * N_GROUPS]
        of_ref, ob_ref, lse_ref = refs[2 * N_GROUPS:2 * N_GROUPS + 3]
        lse_p = refs[2 * N_GROUPS + 3:2 * N_GROUPS + 3 + len(perm)]
        scr = refs[2 * N_GROUPS + 3 + len(perm):]
        ov, lv = [], []
        si = 0
        for g, dil in enumerate(DILS):
            if dil == 1:
                ov.append(o_in[g][...])
                lv.append(l_in[g][...])
            else:
                so, sl = scr[si], scr[si + 1]
                si += 2
                for res in range(dil):
                    _put_residue(so, res, dil, o_in[g][res])
                    _put_residue(sl, res, dil, l_in[g][res])
                ov.append(_unstage(so))
                lv.append(_unstage(sl))
        m = jnp.maximum(jnp.maximum(lv[0], lv[1]), lv[2])
        e = [jnp.exp(t - m) for t in lv]
        tot = e[0] + e[1] + e[2]
        o = (e[0] * ov[0] + e[1] * ov[1] + e[2] * ov[2]) / tot
        lse = m + jnp.log(tot)
        of_ref[...] = o
        ob_ref[...] = o.astype(BF16)
        lse_ref[...] = lse
        sl = scr[1]
        _stage(sl, lse)
        for dil, p_ref in zip(perm, lse_p):
            for res in range(dil):
                p_ref[res] = _get_residue(sl, res, dil)

    def in_spec(dil):
        return _row_spec(tm, W) if dil == 1 else _perm_spec(dil, tm, W)

    ins = [_by_residue(a, dil) for a, dil in zip(os_, DILS)] + [_by_residue(a, dil) for a, dil in zip(lses, DILS)]
    return pl.pallas_call(
        kern, name=name, grid=(S // tm,),
        out_shape=[jax.ShapeDtypeStruct((S, W), F32), jax.ShapeDtypeStruct((S, W), BF16),
                   jax.ShapeDtypeStruct((S, W), F32)]
        + [jax.ShapeDtypeStruct((dil, S // dil, W), F32) for dil in perm],
        in_specs=[in_spec(dil) for dil in DILS] * 2,
        out_specs=[_row_spec(tm, W)] * 3 + [_perm_spec(dil, tm, W) for dil in perm],
        scratch_shapes=[_stage_shape(tm, W) for _ in range(2 * len(perm))],
        compiler_params=_params(("parallel",)),
    )(*ins)


def _delta(do, o, name):
    S, W = o.shape
    tm = _pick(S, [256, 128])
    perm = [dil for dil in DILS if dil > 1]

    def kern(do_ref, o_ref, dob_ref, d_ref, *rest):
        scr, scr_do = rest[2 * len(perm)], rest[2 * len(perm) + 1]
        prod = do_ref[...] * o_ref[...]
        ri = jnp.right_shift(lax.broadcasted_iota(jnp.int32, (W, W), 0), HEAD_SHIFT)
        ci = jnp.right_shift(lax.broadcasted_iota(jnp.int32, (W, W), 1), HEAD_SHIFT)
        same = jnp.where(ri == ci, 1.0, 0.0).astype(BF16)
        hi_p = prod.astype(BF16)
        lo_p = (prod - hi_p.astype(F32)).astype(BF16)
        dl = (lax.dot_general(hi_p, same, NN, preferred_element_type=F32)
              + lax.dot_general(lo_p, same, NN, preferred_element_type=F32))
        d_ref[...] = dl
        dob_ref[...] = do_ref[...].astype(BF16)
        _stage(scr, dl)
        _stage(scr_do, do_ref[...])
        for j, dil in enumerate(perm):
            for res in range(dil):
                rest[2 * j][res] = _get_residue(scr_do, res, dil).astype(BF16)
                rest[2 * j + 1][res] = _get_residue(scr, res, dil)

    out_shape = [jax.ShapeDtypeStruct((S, W), BF16), jax.ShapeDtypeStruct((S, W), F32)]
    out_specs = [_row_spec(tm, W), _row_spec(tm, W)]
    for dil in perm:
        out_shape += [jax.ShapeDtypeStruct((dil, S // dil, W), BF16), jax.ShapeDtypeStruct((dil, S // dil, W), F32)]
        out_specs += [_perm_spec(dil, tm, W), _perm_spec(dil, tm, W)]
    return pl.pallas_call(
        kern, name=name, grid=(S // tm,),
        out_shape=out_shape,
        in_specs=[_row_spec(tm, W), _row_spec(tm, W)], out_specs=out_specs,
        scratch_shapes=[_stage_shape(tm, W), _stage_shape(tm, W)],
        compiler_params=_params(("parallel",)),
    )(do, o)


def _ada_fwd(c16, ada_w, name):
    depth, D, n = ada_w.shape
    rows = 2 * N_DEV

    def kern(c_ref, w_ref, o_ref, cs_ref):
        cv = c_ref[...]
        cs = cv * jax.nn.sigmoid(cv)
        cs_ref[...] = cs
        o_ref[0] = _dot3(cs, w_ref[0], NN)

    return pl.pallas_call(
        kern, name=name, grid=(depth,),
        out_shape=[jax.ShapeDtypeStruct((depth, rows, n), F32), jax.ShapeDtypeStruct((rows, D), F32)],
        in_specs=[pl.BlockSpec((rows, D), lambda l: (0, 0)), pl.BlockSpec((1, D, n), lambda l: (l, 0, 0))],
        out_specs=[pl.BlockSpec((1, rows, n), lambda l: (l, 0, 0)), pl.BlockSpec((rows, D), lambda l: (0, 0))],
        compiler_params=_params(("arbitrary",)),
    )(c16, ada_w)


def _ada_bwd(cs16, dm16, name):
    depth, _, n = dm16.shape
    D = cs16.shape[1]

    def kern(cs_ref, dm_ref, o_ref):
        o_ref[0] = _dot3(cs_ref[...], dm_ref[0], TN)

    return pl.pallas_call(
        kern, name=name, grid=(depth,),
        out_shape=jax.ShapeDtypeStruct((depth, D, n), F32),
        in_specs=[pl.BlockSpec((2 * N_DEV, D), lambda l: (0, 0)), pl.BlockSpec((1, 2 * N_DEV, n), lambda l: (l, 0, 0))],
        out_specs=pl.BlockSpec((1, D, n), lambda l: (l, 0, 0)),
        compiler_params=_params(("parallel",)),
    )(cs16, dm16)


def _sum_rows8(parts, name):
    _, r, n = parts.shape

    def kern(p_ref, o_ref):
        acc = p_ref[0]
        for k in range(1, N_DEV):
            acc = acc + p_ref[k]
        o_ref[...] = acc

    return pl.pallas_call(
        kern, name=name, out_shape=jax.ShapeDtypeStruct((r, n), F32),
        in_specs=[pl.BlockSpec(memory_space=pltpu.VMEM)], out_specs=pl.BlockSpec(memory_space=pltpu.VMEM),
    )(parts)


def _adamw(w, g, m, v, name):
    shape = w.shape
    c = shape[-1]
    r = int(np.prod(shape[:-1])) if len(shape) > 1 else 1
    w2, g2, m2, v2 = (t.reshape(r, c) for t in (w, g, m, v))
    tr = r
    for cand in (2048, 1024, 512, 256, 128, 64, 32, 16, 8):
        if r % cand == 0 and cand * c * 4 <= (1 << 20):
            tr = cand
            break
    c1 = 1.0 - ADAM_B1 ** ADAM_STEP
    c2 = 1.0 - ADAM_B2 ** ADAM_STEP

    def kern(w_ref, g_ref, m_ref, v_ref, d_ref, nm_ref, nv_ref):
        gv = g_ref[...]
        nm = ADAM_B1 * m_ref[...] + (1.0 - ADAM_B1) * gv
        nv = ADAM_B2 * v_ref[...] + (1.0 - ADAM_B2) * (gv * gv)
        nm_ref[...] = nm
        nv_ref[...] = nv
        d_ref[...] = -ADAM_LR * ((nm / c1) / (jnp.sqrt(nv / c2) + ADAM_EPS) + ADAM_WD * w_ref[...])

    spec = pl.BlockSpec((tr, c), lambda i: (i, 0))
    outs = pl.pallas_call(
        kern, name=name, grid=(r // tr,),
        out_shape=[jax.ShapeDtypeStruct((r, c), F32)] * 3,
        in_specs=[spec] * 4, out_specs=[spec] * 3,
        compiler_params=_params(("parallel",)),
    )(w2, g2, m2, v2)
    return tuple(o.reshape(shape) for o in outs)


def kernel(x, c, ada_w, ada_b, norm_g, ffn_w_gate, ffn_w_up, ffn_w_down, w_in, conv_w, w_conv_out, w_attn_out, w_o, rel_bias, final_g, loss_target, m_ada_w, m_ada_b, m_norm_g, m_ffn_w_gate, m_ffn_w_up, m_ffn_w_down, m_w_in, m_conv_w, m_w_conv_out, m_w_attn_out, m_w_o, m_rel_bias, m_final_g, v_ada_w, v_ada_b, v_norm_g, v_ffn_w_gate, v_ffn_w_up, v_ffn_w_down, v_w_in, v_conv_w, v_w_conv_out, v_w_attn_out, v_w_o, v_rel_bias, v_final_g):
    depth = ada_w.shape[0]
    S, D = x.shape[1], x.shape[2]
    me = 4 * lax.axis_index("x") + 2 * lax.axis_index("y") + lax.axis_index("c")
    x0 = x.reshape(S, D)
    target = loss_target.reshape(S, D)
    fsh = ffn_w_down.shape[2]
    insh = w_in.shape[2]
    dsh = D // N_DEV
    ao_rows = dsh * ATTN_OUT // D

    piece_rows = [fsh] * 6 + [insh, dsh, dsh, ao_rows]

    def pack(l):
        def t(a):
            return jnp.transpose(a).astype(BF16)
        ps = [t(ffn_w_gate[l, 0]), t(ffn_w_gate[l, 1]), t(ffn_w_up[l, 0]), t(ffn_w_up[l, 1]),
              ffn_w_down[l, 0].astype(BF16), ffn_w_down[l, 1].astype(BF16), t(w_in[l]),
              w_conv_out[l].astype(BF16), w_o[l].astype(BF16), t(w_attn_out[l]).reshape(ao_rows, D)]
        return jnp.concatenate(ps, axis=0)

    W = []
    for l in range(depth):
        full = _all_gather_pieces(pack(l), piece_rows, "weights_all_gather")
        in_t = full[6]
        qkv_t = [jnp.concatenate([in_t[t * QKV_W + g * ATTN_OUT: t * QKV_W + (g + 1) * ATTN_OUT] for t in range(3)])
                 for g in range(N_GROUPS)]
        ao_t = full[9].reshape(N_DEV, dsh, ATTN_OUT).reshape(D, ATTN_OUT)
        W.append(dict(g_t=full[0:2], u_t=full[2:4], down=full[4:6], qkv_t=qkv_t, cg_t=in_t[3 * QKV_W:],
                      co=full[7], wo=full[8], ao_t=ao_t))

    c_all = _all_gather(c.reshape(D // 128, 128), "c_all_gather").reshape(N_DEV, D)
    c16 = jnp.concatenate([c_all, jnp.zeros_like(c_all)], axis=0)
    mod_part, cs16 = _ada_fwd(c16, ada_w, "ada_fwd")
    mod_part = mod_part[:, :N_DEV]
    n_ada = ada_w.shape[2]
    mod_all = _all_gather(mod_part.reshape(depth * N_DEV * n_ada // 128, 128), "mod_all_gather")
    mod_all = mod_all.reshape(N_DEV, depth, N_DEV, n_ada)
    mod_mine = lax.dynamic_index_in_dim(mod_all, me, axis=2, keepdims=False)
    mod = jnp.transpose(mod_mine, (1, 0, 2)).reshape(depth, N_DEV * n_ada) + ada_b
    mod = mod.reshape(depth, 3, 3, 1, D)

    small = jnp.concatenate([norm_g.reshape(-1), conv_w.reshape(-1)]).reshape(-1, 128)
    small_all = _all_gather(small, "small_all_gather").reshape(N_DEV, -1)
    n_ng = norm_g.size
    norm_g_full = jnp.transpose(small_all[:, :n_ng].reshape(N_DEV, depth, 3, dsh), (1, 2, 0, 3)).reshape(depth, 3, 1, D)
    conv_w_full = jnp.transpose(small_all[:, n_ng:].reshape(N_DEV, depth, CONV_K, dsh), (1, 2, 0, 3)).reshape(depth, CONV_K, D)

    buckets = jnp.asarray(_bucket_tables())
    bias = _bias_build(rel_bias, buckets, "bias_build")
    perm_dils = tuple(dil for dil in DILS if dil > 1)

    saved = []
    xc = x0
    for l in range(depth):
        sv = {}
        for sub in (0, 1, 2):
            g, sh, sc, gt = norm_g_full[l, sub], mod[l, sub, 0], mod[l, sub, 1], mod[l, sub, 2]
            rec = dict(x=xc)
            if sub != 1:
                i = 0 if sub == 0 else 1
                h = _norm_mod_fwd(xc, g, sc, sh, "norm_mod_fwd")[0]
                a, u, z = _ffn_up(h, W[l]["g_t"][i], W[l]["u_t"][i], "ffn_up")
                xc, f = _matmul(z, W[l]["down"][i], "nn", BF16, "ffn_down", tm=512, resid=(xc, gt, 0.5))
                rec.update(h=h, a=a, u=u, z=z, f=f)
            else:
                hs = _norm_mod_fwd(xc, g, sc, sh, "norm_mod_fwd_mixer", dils=perm_dils)
                h = hs[0]
                h_res = [h] + [_flat(t) for t in hs[1:]]
                cg = _matmul(h, W[l]["cg_t"], "nt", BF16, "mixer_cg")
                qkvs, os_, lses = [], [], []
                for gi, dil in enumerate(DILS):
                    qkv = _matmul(h_res[gi], W[l]["qkv_t"][gi], "nt", BF16, "mixer_qkv")
                    o_g, lse_g = _attn_fwd(qkv, bias[gi], dil, f"attn_fwd_g{gi}")
                    qkvs.append(qkv)
                    os_.append(o_g)
                    lses.append(lse_g)
                comb = _combine_fwd(os_, lses, "combine_fwd")
                o_f, o_b, lse = comb[0:3]
                lse_res = [lse] + [_flat(t) for t in comb[3:]]
                yc_in = _conv_fwd(cg, conv_w_full[l], "conv_fwd")
                yc = _matmul(yc_in, W[l]["co"], "nn", BF16, "conv_out")
                ya = _matmul(o_b, W[l]["ao_t"], "nt", BF16, "attn_out")
                merged = _merge_fwd(cg, yc, ya, "merge_fwd")
                xc, f = _matmul(merged, W[l]["wo"], "nn", BF16, "mixer_out", resid=(xc, gt, 1.0))
                rec.update(h=h, h_res=h_res, qkvs=qkvs, cg=cg, o_f=o_f, o_b=o_b, lse_res=lse_res, yc_in=yc_in,
                           yc=yc, ya=ya, merged=merged, f=f)
            sv[sub] = rec
        saved.append(sv)

    dx, head = _loss_head(xc, final_g.reshape(1, D), target, "loss_head")
    d_final_g = head[0]
    loss_part = head[2, 0]

    d_mod = [[None] * 3 for _ in range(depth)]
    d_norm = [[None] * 3 for _ in range(depth)]
    d_conv = [None] * depth
    g_rows = [None] * depth
    dlog = jnp.zeros((N_GROUPS, HEADS_PER_GROUP, BLOCK, 2 * BLOCK), F32)
    for l in reversed(range(depth)):
        dW = {}
        for sub in (2, 1, 0):
            rec = saved[l][sub]
            g, sc, gt = norm_g_full[l, sub], mod[l, sub, 1], mod[l, sub, 2]
            if sub != 1:
                i = 0 if sub == 0 else 1
                df, gsum = _gate_bwd(dx, rec["f"], gt, 0.5, "gate_bwd")
                dz = _matmul(df, W[l]["down"][i], "nt", BF16, "ffn_down_dx")
                dW["down", i] = _matmul(rec["z"], df, "tn", BF16, "ffn_down_dw")
                da, du, dh = _ffn_up_bwd(dz, rec["a"], rec["u"], W[l]["g_t"][i], W[l]["u_t"][i], "ffn_up_bwd")
                dW["g_t", i] = _matmul(da, rec["h"], "tn", BF16, "ffn_gate_dw")
                dW["u_t", i] = _matmul(du, rec["h"], "tn", BF16, "ffn_up_dw")
                dx, sums = _norm_mod_bwd(rec["x"], [dh], [], dx, g, sc, "norm_mod_bwd")
            else:
                dout, gsum = _gate_bwd(dx, rec["f"], gt, 1.0, "gate_bwd_mixer")
                dm = _matmul(dout, W[l]["wo"], "nt", BF16, "mixer_out_dx")
                dW["wo"] = _matmul(rec["merged"], dout, "tn", BF16, "mixer_out_dw")
                dyc, dya, dgg = _merge_bwd(rec["cg"], rec["yc"], rec["ya"], dm, "merge_bwd")
                dyc_in = _matmul(dyc, W[l]["co"], "nt", BF16, "conv_out_dx")
                dW["co"] = _matmul(rec["yc_in"], dyc, "tn", BF16, "conv_out_dw")
                do = _matmul(dya, W[l]["ao_t"], "nn", F32, "attn_out_dx")
                dW["ao_t"] = _matmul(dya, rec["o_b"], "tn", BF16, "attn_out_dw")
                dl = _delta(do, rec["o_f"], "attn_delta")
                do_res = [dl[0]] + [_flat(t) for t in dl[2::2]]
                del_res = [dl[1]] + [_flat(t) for t in dl[3::2]]
                dh_attn, dw_q, dw_kv, dlog_l = [], [], [], []
                for gi, dil in enumerate(DILS):
                    dq, dkv, dlg = _attn_bwd(rec["qkvs"][gi], do_res[gi], rec["lse_res"][gi], del_res[gi],
                                             bias[gi], dil, f"attn_bwd_g{gi}")
                    dlog_l.append(dlg)
                    dh_attn.append(_attn_dh(dq, dkv, W[l]["qkv_t"][gi], "attn_dh"))
                    dw_q.append(_matmul(dq, rec["h_res"][gi], "tn", BF16, "mixer_q_dw"))
                    dw_kv.append(_matmul(dkv, rec["h_res"][gi], "tn", BF16, "mixer_kv_dw"))
                dlog = dlog + jnp.stack(dlog_l)
                dcg, conv_sum = _conv_bwd(rec["cg"], dyc_in, dgg, conv_w_full[l], "conv_bwd")
                d_conv[l] = conv_sum[0:CONV_K]
                dh_cg = _matmul(dcg, W[l]["cg_t"], "nn", F32, "mixer_cg_dx")
                dw_cg = _matmul(dcg, rec["h"], "tn", BF16, "mixer_cg_dw")
                dW["in_t"] = jnp.concatenate(
                    dw_q + [t[:ATTN_OUT] for t in dw_kv] + [t[ATTN_OUT:] for t in dw_kv] + [dw_cg], axis=0)
                perm_parts = [(dil, _by_residue(dh_attn[gi], dil)) for gi, dil in enumerate(DILS) if dil > 1]
                dx, sums = _norm_mod_bwd(rec["x"], [dh_cg, dh_attn[0]], perm_parts, dx, g, sc, "norm_mod_bwd_mixer")
            d_mod[l][sub] = jnp.stack([sums[0], sums[2], gsum[0]])
            d_norm[l][sub] = sums[3]
        pieces = [dW["g_t", 0], dW["g_t", 1], dW["u_t", 0], dW["u_t", 1], dW["down", 0], dW["down", 1],
                  dW["in_t"], dW["co"], dW["wo"], dW["ao_t"].reshape(N_DEV * ao_rows, D)]
        recv = _all_to_all_pieces(pieces, "grads_all_to_all")
        g_rows[l] = _sum_sources(recv, "grads_sum")
    grad_x = dx.reshape(1, S, D)
    d_rel = _bias_bwd(dlog, buckets, "bias_bwd")[:, :rel_bias.shape[1]]

    offs = _offsets(piece_rows)

    def shard_grad(p, transpose, shape=None):
        rows = [g_rows[l][offs[p]:offs[p] + piece_rows[p]] for l in range(depth)]
        if shape is not None:
            rows = [t.reshape(shape) for t in rows]
        return jnp.stack([jnp.transpose(t) if transpose else t for t in rows])

    g_gate = jnp.stack([shard_grad(0, True), shard_grad(1, True)], axis=1)
    g_up = jnp.stack([shard_grad(2, True), shard_grad(3, True)], axis=1)
    g_down = jnp.stack([shard_grad(4, False), shard_grad(5, False)], axis=1)
    g_w_in = shard_grad(6, True)
    g_co = shard_grad(7, False)
    g_wo = shard_grad(8, False)
    g_ao = shard_grad(9, True, (dsh, ATTN_OUT))

    d_mod_flat = jnp.stack([jnp.stack(d_mod[l]) for l in range(depth)]).reshape(-1)
    d_norm_flat = jnp.stack([jnp.stack(d_norm[l]) for l in range(depth)]).reshape(-1)
    d_conv_flat = jnp.stack(d_conv).reshape(-1)
    vec = jnp.concatenate([d_mod_flat, d_norm_flat, d_conv_flat, d_rel.reshape(-1), d_final_g,
                           jnp.broadcast_to(loss_part, (128,))])
    pad = (-vec.size) % 1024
    vec = jnp.concatenate([vec, jnp.zeros((pad,), F32)]).reshape(-1, 128)
    parts = _all_gather(vec, "small_grads_all_gather").reshape(N_DEV, vec.shape[0], 128)
    tot = _sum_rows8(parts, "small_grads_sum").reshape(-1)
    o0 = 0
    g_ada_b = tot[o0:o0 + d_mod_flat.size].reshape(ada_b.shape)
    o0 += d_mod_flat.size
    g_norm_full = tot[o0:o0 + d_norm_flat.size].reshape(depth, 3, D)
    o0 += d_norm_flat.size
    g_conv_full = tot[o0:o0 + d_conv_flat.size].reshape(depth, CONV_K, D)
    o0 += d_conv_flat.size
    g_rel = tot[o0:o0 + rel_bias.size].reshape(rel_bias.shape)
    o0 += rel_bias.size
    g_final = tot[o0:o0 + D]
    o0 += D
    loss = tot[o0]
    g_norm = lax.dynamic_slice_in_dim(g_norm_full, me * dsh, dsh, axis=2)
    g_conv = lax.dynamic_slice_in_dim(g_conv_full, me * dsh, dsh, axis=2)

    dm_all = parts.reshape(N_DEV, -1)[:, :d_mod_flat.size].reshape(N_DEV, depth, N_DEV * n_ada)
    dm_cols = lax.dynamic_slice_in_dim(dm_all, me * n_ada, n_ada, axis=2)
    dm16 = jnp.concatenate([jnp.transpose(dm_cols, (1, 0, 2)), jnp.zeros((depth, N_DEV, n_ada), F32)], axis=1)
    g_ada_w = _ada_bwd(cs16, dm16, "ada_bwd")

    grads = dict(ada_w=g_ada_w, ada_b=g_ada_b, norm_g=g_norm, ffn_w_gate=g_gate, ffn_w_up=g_up,
                 ffn_w_down=g_down, w_in=g_w_in, conv_w=g_conv, w_conv_out=g_co, w_attn_out=g_ao, w_o=g_wo,
                 rel_bias=g_rel, final_g=g_final)
    weights = dict(ada_w=ada_w, ada_b=ada_b, norm_g=norm_g, ffn_w_gate=ffn_w_gate, ffn_w_up=ffn_w_up,
                   ffn_w_down=ffn_w_down, w_in=w_in, conv_w=conv_w, w_conv_out=w_conv_out, w_attn_out=w_attn_out,
                   w_o=w_o, rel_bias=rel_bias, final_g=final_g)
    ms = dict(ada_w=m_ada_w, ada_b=m_ada_b, norm_g=m_norm_g, ffn_w_gate=m_ffn_w_gate, ffn_w_up=m_ffn_w_up,
              ffn_w_down=m_ffn_w_down, w_in=m_w_in, conv_w=m_conv_w, w_conv_out=m_w_conv_out,
              w_attn_out=m_w_attn_out, w_o=m_w_o, rel_bias=m_rel_bias, final_g=m_final_g)
    vs = dict(ada_w=v_ada_w, ada_b=v_ada_b, norm_g=v_norm_g, ffn_w_gate=v_ffn_w_gate, ffn_w_up=v_ffn_w_up,
              ffn_w_down=v_ffn_w_down, w_in=v_w_in, conv_w=v_conv_w, w_conv_out=v_w_conv_out,
              w_attn_out=v_w_attn_out, w_o=v_w_o, rel_bias=v_rel_bias, final_g=v_final_g)
    order = list(weights)
    deltas, new_m, new_v = [], [], []
    for name in order:
        d_, m_, v_ = _adamw(weights[name], grads[name], ms[name], vs[name], "adamw_" + name)
        deltas.append(d_)
        new_m.append(m_)
        new_v.append(v_)
    return (loss, grad_x, *[grads[n] for n in order], *deltas, *new_m, *new_v)
```

```python
import functools

import numpy as np
import jax
import jax.numpy as jnp
from jax import lax
from jax.experimental import pallas as pl
from jax.experimental.pallas import tpu as pltpu

F32 = jnp.float32
BF16 = jnp.bfloat16

N_DEV = 8
HEAD_DIM = 64
HEAD_SHIFT = 6
HEADS_PER_GROUP = 8
DILATION_GROUPS = ((128, 1), (512, 4), (2048, 16))
DILS = tuple(d for _, d in DILATION_GROUPS)
N_GROUPS = len(DILATION_GROUPS)
ATTN_OUT = HEADS_PER_GROUP * HEAD_DIM
QKV_W = N_GROUPS * ATTN_OUT
BLOCK = 128
NUM_BUCKETS = 32
MAX_DISTANCE = 2048
CONV_K = 3
EPS = 1e-6
NEG_INF = -1e30
SCALE = HEAD_DIM ** -0.5

ADAM_LR = 0.001
ADAM_B1 = 0.9
ADAM_B2 = 0.999
ADAM_EPS = 1e-08
ADAM_WD = 0.01
ADAM_STEP = 10

V7X_VMEM_LIMIT = 48 * 1024 * 1024
MESH = pl.DeviceIdType.MESH

NN = (((1,), (0,)), ((), ()))
NT = (((1,), (1,)), ((), ()))
TN = (((0,), (0,)), ((), ()))


def _pick(dim, cands):
    for c in cands:
        if dim % c == 0:
            return c
    return dim


def _pick_k(K, cap=2816):
    if K <= cap or K % 128:
        return K
    best = 128
    for m in range(1, K // 128 + 1):
        if (K // 128) % m == 0 and 128 * m <= cap:
            best = 128 * m
    return best


def _params(sem):
    return pltpu.CompilerParams(dimension_semantics=sem, vmem_limit_bytes=V7X_VMEM_LIMIT)


def _all_gather(x_shard, name):
    m_per, n = x_shard.shape

    def body(x_ref, out_ref, send_sems, recv_sems, local_sem):
        x, y, c = lax.axis_index("x"), lax.axis_index("y"), lax.axis_index("c")
        me, sibling = (x, y, c), (x, y, 1 - c)
        chips = [(1 - x, y), (x, 1 - y), (1 - x, 1 - y)]

        def rows(px, py, pc):
            return out_ref.at[pl.ds((4 * px + 2 * py + pc) * m_per, m_per), :]

        def copy(k, block, to, src=None):
            return pltpu.make_async_remote_copy(
                src_ref=rows(*block) if src is None else src, dst_ref=rows(*block),
                send_sem=send_sems.at[k], recv_sem=recv_sems.at[k], device_id=to, device_id_type=MESH)

        mine = pltpu.make_async_copy(x_ref, rows(*me), local_sem)
        mine.start()
        first = [copy(0, me, sibling, src=x_ref)]
        first += [copy(1 + j, me, (*chip, c), src=x_ref) for j, chip in enumerate(chips)]
        for cp in first:
            cp.start()
        passed = [copy(4 + j, (*chip, c), sibling) for j, chip in enumerate(chips)]
        for j, chip in enumerate(chips):
            copy(1 + j, (*chip, c), me).wait_recv()
            passed[j].start()
        copy(0, sibling, me).wait_recv()
        for j, chip in enumerate(chips):
            copy(4 + j, (*chip, 1 - c), me).wait_recv()
        for cp in first + passed:
            cp.wait_send()
        mine.wait()

    return pl.pallas_call(
        body, name=name,
        out_shape=jax.ShapeDtypeStruct((N_DEV * m_per, n), x_shard.dtype),
        in_specs=[pl.BlockSpec(memory_space=pltpu.VMEM)],
        out_specs=pl.BlockSpec(memory_space=pltpu.VMEM),
        scratch_shapes=[pltpu.SemaphoreType.DMA((7,)), pltpu.SemaphoreType.DMA((7,)), pltpu.SemaphoreType.DMA],
    )(x_shard)


def _offsets(piece_rows):
    offs, o = [], 0
    for n in piece_rows:
        offs.append(o)
        o += n
    return offs


HBM_SPEC = pl.BlockSpec(memory_space=pltpu.HBM)
SEM_SPEC = pl.BlockSpec(memory_space=pltpu.SEMAPHORE)
ANY_SPEC = pl.BlockSpec(memory_space=pl.ANY)
SPLIT_COPY_PARAMS = pltpu.CompilerParams(has_side_effects=pltpu.SideEffectType.DATAFLOW_SIDE_EFFECTING)


def _in_hbm(a):
    return pltpu.with_memory_space_constraint(a, pltpu.HBM)


def _dma_sems(n):
    return [pltpu.SemaphoreType.DMA(())] * n


def _whole(ref, send_sem, recv_sem, me):
    return pltpu.make_async_remote_copy(src_ref=ref, dst_ref=ref, send_sem=send_sem, recv_sem=recv_sem,
                                        device_id=me, device_id_type=MESH)


def _gather_start(packed, piece_rows, after, name):
    R, w = packed.shape
    offs = _offsets(piece_rows)
    P = len(piece_rows)
    assert offs[-1] + piece_rows[-1] == R

    def body(*refs):
        src_ref = refs[0]
        o = refs[P + 2:]
        send, recv = o[0:4], o[4:8]
        zones, token, local_sems = o[9:9 + P], o[9 + P], o[10 + P]
        x, y, c = lax.axis_index("x"), lax.axis_index("y"), lax.axis_index("c")
        targets = [(x, y, 1 - c), (1 - x, y, c), (x, 1 - y, c), (1 - x, 1 - y, c)]
        me = 4 * x + 2 * y + c

        def piece(p):
            return src_ref.at[pl.ds(offs[p], piece_rows[p]), :]

        def rows(p):
            return zones[p].at[pl.ds(me * piece_rows[p], piece_rows[p]), :]

        mine = [pltpu.make_async_copy(piece(p), rows(p), local_sems.at[p]) for p in range(P)]
        for cp in mine:
            cp.start()
        for k, to in enumerate(targets):
            for p in range(P):
                pltpu.make_async_remote_copy(src_ref=piece(p), dst_ref=rows(p), send_sem=send[k], recv_sem=recv[k],
                                             device_id=to, device_id_type=MESH).start()
        for cp in mine:
            cp.wait()
        token[...] = jnp.zeros_like(token)

    zones_in = [_in_hbm(lax.empty((N_DEV * n, w), packed.dtype)) for n in piece_rows]
    outs = pl.pallas_call(
        body, name=name,
        out_shape=(*_dma_sems(8), pltpu.HBM((R, w), packed.dtype),
                   *[pltpu.HBM((N_DEV * n, w), packed.dtype) for n in piece_rows],
                   jax.ShapeDtypeStruct((8, 128), F32)),
        in_specs=[HBM_SPEC] * (P + 1) + [ANY_SPEC],
        out_specs=[SEM_SPEC] * 8 + [HBM_SPEC] * (P + 1) + [pl.BlockSpec(memory_space=pltpu.VMEM)],
        input_output_aliases={0: 8, **{1 + p: 9 + p for p in range(P)}},
        scratch_shapes=[pltpu.SemaphoreType.DMA((P,))],
        compiler_params=SPLIT_COPY_PARAMS,
    )(_in_hbm(packed), *zones_in, after)
    return outs[0:8], outs[8], list(outs[9:9 + P]), outs[9 + P]


def _gather_forward(sems, packed, zones, piece_rows, after, name):
    P = len(piece_rows)

    def body(*refs):
        src_ref = refs[0]
        s = refs[1 + P:9 + P]
        o = refs[10 + P:]
        send, recv = s[0:4], s[4:8]
        send2, recv2, zones_o = o[0:3], o[3:6], o[7:7 + P]
        x, y, c = lax.axis_index("x"), lax.axis_index("y"), lax.axis_index("c")
        me = (x, y, c)
        chips = [(1 - x, y), (x, 1 - y), (1 - x, 1 - y)]
        for j, (px, py) in enumerate(chips):
            _whole(src_ref, send[1 + j], recv[1 + j], me).wait_recv()
            blk = 4 * px + 2 * py + c
            for p in range(P):
                r = zones_o[p].at[pl.ds(blk * piece_rows[p], piece_rows[p]), :]
                pltpu.make_async_remote_copy(src_ref=r, dst_ref=r, send_sem=send2[j], recv_sem=recv2[j],
                                             device_id=(x, y, 1 - c), device_id_type=MESH).start()
        _whole(src_ref, send[0], recv[0], me).wait_recv()
        for k in range(4):
            _whole(src_ref, send[k], recv[k], me).wait_send()

    outs = pl.pallas_call(
        body, name=name,
        out_shape=(*_dma_sems(6), pltpu.HBM(packed.shape, packed.dtype),
                   *[pltpu.HBM(z.shape, z.dtype) for z in zones]),
        in_specs=[HBM_SPEC] * (P + 1) + [SEM_SPEC] * 8 + [ANY_SPEC],
        out_specs=[SEM_SPEC] * 6 + [HBM_SPEC] * (P + 1),
        input_output_aliases={0: 6, **{1 + p: 7 + p for p in range(P)}},
        compiler_params=SPLIT_COPY_PARAMS,
    )(packed, *zones, *sems, after)
    return outs[0:6], outs[6], list(outs[7:7 + P])


def _gather_finish(sems2, packed, zones, after, name):
    P = len(zones)

    def body(*refs):
        src_ref = refs[0]
        s = refs[1 + P:7 + P]
        x, y, c = lax.axis_index("x"), lax.axis_index("y"), lax.axis_index("c")
        for j in range(3):
            _whole(src_ref, s[j], s[3 + j], (x, y, c)).wait_recv()
        for j in range(3):
            _whole(src_ref, s[j], s[3 + j], (x, y, c)).wait_send()

    outs = pl.pallas_call(
        body, name=name,
        out_shape=(pltpu.HBM(packed.shape, packed.dtype), *[pltpu.HBM(z.shape, z.dtype) for z in zones]),
        in_specs=[HBM_SPEC] * (P + 1) + [SEM_SPEC] * 6 + [ANY_SPEC],
        out_specs=[HBM_SPEC] * (P + 1),
        input_output_aliases={p: p for p in range(P + 1)},
        compiler_params=SPLIT_COPY_PARAMS,
    )(packed, *zones, *sems2, after)
    return list(outs[1:1 + P])


def _scatter_start(pieces, after, name):
    P = len(pieces)
    w = pieces[0].shape[1]
    piece_rows = [p.shape[0] // N_DEV for p in pieces]
    offs = _offsets(piece_rows)
    R = offs[-1] + piece_rows[-1]

    def body(*refs):
        o = refs[P + 2:]
        send, recv = o[0:7], o[7:14]
        srcs, dst_ref, token, local_sems = o[14:14 + P], o[14 + P], o[15 + P], o[16 + P]
        x, y, c = lax.axis_index("x"), lax.axis_index("y"), lax.axis_index("c")
        me = 4 * x + 2 * y + c

        def chunk(p, dev):
            return srcs[p].at[pl.ds(dev * piece_rows[p], piece_rows[p]), :]

        def slot(p, dev):
            return dst_ref.at[dev, pl.ds(offs[p], piece_rows[p]), :]

        mine = [pltpu.make_async_copy(chunk(p, me), slot(p, me), local_sems.at[p]) for p in range(P)]
        for cp in mine:
            cp.start()
        for k in range(1, N_DEV):
            px = 1 - x if (k >> 2) & 1 else x
            py = 1 - y if (k >> 1) & 1 else y
            pc = 1 - c if k & 1 else c
            peer = 4 * px + 2 * py + pc
            for p in range(P):
                pltpu.make_async_remote_copy(
                    src_ref=chunk(p, peer), dst_ref=slot(p, me), send_sem=send[k - 1], recv_sem=recv[k - 1],
                    device_id=(px, py, pc), device_id_type=MESH).start()
        for cp in mine:
            cp.wait()
        token[...] = jnp.zeros_like(token)

    dtype = pieces[0].dtype
    outs = pl.pallas_call(
        body, name=name,
        out_shape=(*_dma_sems(14), *[pltpu.HBM(p.shape, dtype) for p in pieces], pltpu.HBM((N_DEV, R, w), dtype),
                   jax.ShapeDtypeStruct((8, 128), F32)),
        in_specs=[HBM_SPEC] * (P + 1) + [ANY_SPEC],
        out_specs=[SEM_SPEC] * 14 + [HBM_SPEC] * (P + 1) + [pl.BlockSpec(memory_space=pltpu.VMEM)],
        input_output_aliases={p: 14 + p for p in range(P + 1)},
        scratch_shapes=[pltpu.SemaphoreType.DMA((P,))],
        compiler_params=SPLIT_COPY_PARAMS,
    )(*[_in_hbm(p) for p in pieces], _in_hbm(lax.empty((N_DEV, R, w), dtype)), after)
    return outs[0:14], list(outs[14:14 + P]), outs[14 + P], outs[15 + P]


def _scatter_finish(sems, pieces, recv, after, name):
    P = len(pieces)

    def body(*refs):
        dst_ref = refs[P]
        s = refs[P + 1:P + 15]
        x, y, c = lax.axis_index("x"), lax.axis_index("y"), lax.axis_index("c")
        for k in range(7):
            _whole(dst_ref.at[0], s[k], s[7 + k], (x, y, c)).wait_recv()
        for k in range(7):
            _whole(dst_ref.at[0], s[k], s[7 + k], (x, y, c)).wait_send()

    outs = pl.pallas_call(
        body, name=name,
        out_shape=(*[pltpu.HBM(p.shape, p.dtype) for p in pieces], pltpu.HBM(recv.shape, recv.dtype)),
        in_specs=[HBM_SPEC] * (P + 1) + [SEM_SPEC] * 14 + [ANY_SPEC],
        out_specs=[HBM_SPEC] * (P + 1),
        input_output_aliases={p: p for p in range(P + 1)},
        compiler_params=SPLIT_COPY_PARAMS,
    )(*pieces, recv, *sems, after)
    return outs[P]


def _sum_sources(parts, name):
    _, r, n = parts.shape
    tr = _pick(r, [256, 128, 64, 32, 16, 8])

    def kern(p_ref, o_ref):
        acc = p_ref[0].astype(F32)
        for k in range(1, N_DEV):
            acc = acc + p_ref[k].astype(F32)
        o_ref[...] = acc

    return pl.pallas_call(
        kern, name=name, grid=(r // tr,),
        out_shape=jax.ShapeDtypeStruct((r, n), F32),
        in_specs=[pl.BlockSpec((N_DEV, tr, n), lambda i: (0, i, 0))],
        out_specs=pl.BlockSpec((tr, n), lambda i: (i, 0)),
        compiler_params=_params(("parallel",)),
    )(parts)


def _matmul(a, b, mode, out_dtype, name, tm=None, tn=None, tk=None, resid=None):
    if mode == "nn":
        (M, K), N = a.shape, b.shape[1]
    elif mode == "nt":
        (M, K), N = a.shape, b.shape[0]
    else:
        (K, M), N = a.shape, b.shape[1]
    dims = {"nn": NN, "nt": NT, "tn": TN}[mode]
    tm = tm or _pick(M, [1024, 1408, 512, 256, 128])
    tn = tn or _pick(N, [1024, 1408, 512, 256, 128])
    tk = tk or _pick_k(K)
    nk = K // tk
    a_spec = {"nn": pl.BlockSpec((tm, tk), lambda i, j, k: (i, k)),
              "nt": pl.BlockSpec((tm, tk), lambda i, j, k: (i, k)),
              "tn": pl.BlockSpec((tk, tm), lambda i, j, k: (k, i))}[mode]
    b_spec = {"nn": pl.BlockSpec((tk, tn), lambda i, j, k: (k, j)),
              "nt": pl.BlockSpec((tn, tk), lambda i, j, k: (j, k)),
              "tn": pl.BlockSpec((tk, tn), lambda i, j, k: (k, j))}[mode]
    o_spec = pl.BlockSpec((tm, tn), lambda i, j, k: (i, j))
    n_in = 2 if resid is None else 4
    n_out = 1 if resid is None else 2

    def kern(*refs):
        a_ref, b_ref = refs[0], refs[1]
        outs = refs[n_in:n_in + n_out]
        acc_ref = refs[n_in + n_out] if nk > 1 else None

        def finish(acc):
            if resid is None:
                outs[0][...] = acc.astype(out_dtype)
            else:
                x_ref, g_ref = refs[2], refs[3]
                outs[0][...] = x_ref[...] + (resid[2] * g_ref[...]) * acc
                outs[1][...] = acc.astype(out_dtype)

        part = lax.dot_general(a_ref[...], b_ref[...], dims, preferred_element_type=F32)
        if nk == 1:
            finish(part)
        else:
            k = pl.program_id(2)

            @pl.when(k == 0)
            def _():
                acc_ref[...] = part

            @pl.when(k > 0)
            def _():
                acc_ref[...] += part

            @pl.when(k == nk - 1)
            def _():
                finish(acc_ref[...])

    in_specs = [a_spec, b_spec]
    args = [a, b]
    out_shape = [jax.ShapeDtypeStruct((M, N), out_dtype)]
    out_specs = [o_spec]
    if resid is not None:
        in_specs += [o_spec, pl.BlockSpec((1, tn), lambda i, j, k: (0, j))]
        args += [resid[0], resid[1]]
        out_shape = [jax.ShapeDtypeStruct((M, N), F32)] + out_shape
        out_specs = [o_spec, o_spec]
    res = pl.pallas_call(
        kern, name=name, grid=(M // tm, N // tn, nk),
        out_shape=out_shape, in_specs=in_specs, out_specs=out_specs,
        scratch_shapes=[pltpu.VMEM((tm, tn), F32)] if nk > 1 else [],
        compiler_params=_params(("parallel", "parallel", "arbitrary")),
    )(*args)
    return res[0] if resid is None else res


def _dot3(a, b, dims):
    ah = a.astype(BF16)
    al = (a - ah.astype(F32)).astype(BF16)
    bh = b.astype(BF16)
    bl = (b - bh.astype(F32)).astype(BF16)
    d = functools.partial(lax.dot_general, dimension_numbers=dims, preferred_element_type=F32)
    return d(ah, bh) + (d(ah, bl) + d(al, bh))


def _silu_parts(a):
    sg = jax.nn.sigmoid(a)
    return a * sg, sg * (1.0 + a * (1.0 - sg))


def _ffn_up(h, wg_t, wu_t, name):
    S, D = h.shape
    F = wg_t.shape[0]
    tm = _pick(S, [512, 256, 128])
    tn = _pick(F, [1408, 512, 256, 128])

    def kern(h_ref, g_ref, u_ref, a_out, u_out, z_out):
        hv = h_ref[...]
        a = lax.dot_general(hv, g_ref[...], NT, preferred_element_type=F32)
        u = lax.dot_general(hv, u_ref[...], NT, preferred_element_type=F32)
        a_out[...] = a.astype(BF16)
        u_out[...] = u.astype(BF16)
        z_out[...] = (_silu_parts(a)[0] * u).astype(BF16)

    w_spec = pl.BlockSpec((tn, D), lambda j, i: (j, 0))
    o_spec = pl.BlockSpec((tm, tn), lambda j, i: (i, j))
    return pl.pallas_call(
        kern, name=name, grid=(F // tn, S // tm),
        out_shape=[jax.ShapeDtypeStruct((S, F), BF16)] * 3,
        in_specs=[pl.BlockSpec((tm, D), lambda j, i: (i, 0)), w_spec, w_spec],
        out_specs=[o_spec] * 3,
        compiler_params=_params(("parallel", "parallel")),
    )(h, wg_t, wu_t)


def _ffn_up_bwd(dz, a, u, wg_t, wu_t, name):
    S, F = dz.shape
    D = wg_t.shape[1]
    tm = _pick(S, [512, 256, 128])
    tk = _pick(F, [1408, 512, 256, 128])
    nk = F // tk

    def kern(dz_ref, a_ref, u_ref, g_ref, w_ref, da_out, du_out, dh_out, acc_ref):
        k = pl.program_id(1)
        av = a_ref[...].astype(F32)
        uv = u_ref[...].astype(F32)
        dzv = dz_ref[...].astype(F32)
        silu, dsilu = _silu_parts(av)
        da = (dzv * uv * dsilu).astype(BF16)
        du = (dzv * silu).astype(BF16)
        da_out[...] = da
        du_out[...] = du
        part = (lax.dot_general(da, g_ref[...], NN, preferred_element_type=F32)
                + lax.dot_general(du, w_ref[...], NN, preferred_element_type=F32))

        @pl.when(k == 0)
        def _():
            acc_ref[...] = part

        @pl.when(k > 0)
        def _():
            acc_ref[...] += part

        @pl.when(k == nk - 1)
        def _():
            dh_out[...] = acc_ref[...]

    t_spec = pl.BlockSpec((tm, tk), lambda i, k: (i, k))
    w_spec = pl.BlockSpec((tk, D), lambda i, k: (k, 0))
    return pl.pallas_call(
        kern, name=name, grid=(S // tm, nk),
        out_shape=[jax.ShapeDtypeStruct((S, F), BF16)] * 2 + [jax.ShapeDtypeStruct((S, D), F32)],
        in_specs=[t_spec, t_spec, t_spec, w_spec, w_spec],
        out_specs=[t_spec, t_spec, pl.BlockSpec((tm, D), lambda i, k: (i, 0))],
        scratch_shapes=[pltpu.VMEM((tm, D), F32)],
        compiler_params=_params(("parallel", "arbitrary")),
    )(dz, a, u, wg_t, wu_t)


def _attn_dh(dq, dkv, w_t, name):
    S = dq.shape[0]
    D = w_t.shape[1]
    tm = _pick(S, [1024, 512, 256, 128])

    def kern(dq_ref, dk_ref, dv_ref, wq_ref, wk_ref, wv_ref, o_ref):
        o_ref[...] = (lax.dot_general(dq_ref[...], wq_ref[...], NN, preferred_element_type=F32)
                      + lax.dot_general(dk_ref[...], wk_ref[...], NN, preferred_element_type=F32)
                      + lax.dot_general(dv_ref[...], wv_ref[...], NN, preferred_element_type=F32))

    def w_blk(j):
        return pl.BlockSpec((ATTN_OUT, D), lambda i: (j, 0))

    return pl.pallas_call(
        kern, name=name, grid=(S // tm,),
        out_shape=jax.ShapeDtypeStruct((S, D), F32),
        in_specs=[pl.BlockSpec((tm, ATTN_OUT), lambda i: (i, 0)), pl.BlockSpec((tm, ATTN_OUT), lambda i: (i, 0)),
                  pl.BlockSpec((tm, ATTN_OUT), lambda i: (i, 1)), w_blk(0), w_blk(1), w_blk(2)],
        out_specs=pl.BlockSpec((tm, D), lambda i: (i, 0)),
        compiler_params=_params(("parallel",)),
    )(dq, dkv, dkv, w_t, w_t, w_t)


def _row_spec(tm, d):
    return pl.BlockSpec((tm, d), lambda i: (i, 0))


def _vec_spec(d, rows=1):
    return pl.BlockSpec((rows, d), lambda i: (0, 0))


def _perm_spec(dil, tm, w):
    return pl.BlockSpec((dil, tm // dil, w), lambda i: (0, i, 0))


def _stage_shape(tm, w):
    return pltpu.VMEM((w // 128, tm, 128), F32)


def _stage(scr, val):
    for ci in range(scr.shape[0]):
        scr[ci] = val[:, 128 * ci:128 * (ci + 1)]


def _unstage(scr):
    return jnp.concatenate([scr[ci] for ci in range(scr.shape[0])], axis=1)


def _get_residue(scr, res, dil):
    n = scr.shape[1] // dil
    return jnp.concatenate([scr[ci, pl.ds(res, n, stride=dil), :] for ci in range(scr.shape[0])], axis=1)


def _put_residue(scr, res, dil, val):
    n = scr.shape[1] // dil
    for ci in range(scr.shape[0]):
        scr[ci, pl.ds(res, n, stride=dil), :] = val[:, 128 * ci:128 * (ci + 1)]


def _norm_mod_fwd(x, g, s, b, name, dils=()):
    S, D = x.shape
    tm = _pick(S, [256, 128])

    def kern(x_ref, g_ref, s_ref, b_ref, h_ref, *rest):
        xv = x_ref[...]
        r = lax.rsqrt(jnp.mean(xv * xv, axis=1, keepdims=True) + EPS)
        hv = xv * r * g_ref[...] * (1.0 + s_ref[...]) + b_ref[...]
        h_ref[...] = hv.astype(BF16)
        if dils:
            scr = rest[len(dils)]
            _stage(scr, hv)
            for dil, p_ref in zip(dils, rest[:len(dils)]):
                for res in range(dil):
                    p_ref[res] = _get_residue(scr, res, dil).astype(BF16)

    return pl.pallas_call(
        kern, name=name, grid=(S // tm,),
        out_shape=[jax.ShapeDtypeStruct((S, D), BF16)] + [jax.ShapeDtypeStruct((dil, S // dil, D), BF16) for dil in dils],
        in_specs=[_row_spec(tm, D), _vec_spec(D), _vec_spec(D), _vec_spec(D)],
        out_specs=[_row_spec(tm, D)] + [_perm_spec(dil, tm, D) for dil in dils],
        scratch_shapes=[_stage_shape(tm, D)] if dils else [],
        compiler_params=_params(("parallel",)),
    )(x, g, s, b)


def _norm_mod_bwd(x, dh_nat, dh_perm, dxo, g, s, name):
    S, D = x.shape
    tm = _pick(S, [256, 128])
    n = S // tm
    n_nat, n_perm = len(dh_nat), len(dh_perm)

    def kern(*refs):
        x_ref = refs[0]
        nat = refs[1:1 + n_nat]
        perm = refs[1 + n_nat:1 + n_nat + n_perm]
        dxo_ref, g_ref, s_ref, dx_ref, cs_ref = refs[1 + n_nat + n_perm:6 + n_nat + n_perm]
        scr = refs[6 + n_nat + n_perm:]
        i = pl.program_id(0)
        xv = x_ref[...]
        r = lax.rsqrt(jnp.mean(xv * xv, axis=1, keepdims=True) + EPS)
        xn = xv * r
        dh_v = nat[0][...].astype(F32)
        for t in nat[1:]:
            dh_v = dh_v + t[...].astype(F32)
        for (dil, _), p_ref, sc in zip(dh_perm, perm, scr):
            for res in range(dil):
                _put_residue(sc, res, dil, p_ref[res])
            dh_v = dh_v + _unstage(sc)
        one_s = 1.0 + s_ref[...]
        dxn = dh_v * (g_ref[...] * one_s)
        dx_ref[...] = dxo_ref[...] + r * (dxn - xn * jnp.mean(xn * dxn, axis=1, keepdims=True))

        @pl.when(i == 0)
        def _():
            cs_ref[...] = jnp.zeros_like(cs_ref)

        cs_ref[0:1, :] += jnp.sum(dh_v, axis=0, keepdims=True)
        cs_ref[1:2, :] += jnp.sum(dh_v * xn, axis=0, keepdims=True)

        @pl.when(i == n - 1)
        def _():
            t = cs_ref[1:2, :]
            cs_ref[2:3, :] = g_ref[...] * t
            cs_ref[3:4, :] = one_s * t

    return pl.pallas_call(
        kern, name=name, grid=(n,),
        out_shape=[jax.ShapeDtypeStruct((S, D), F32), jax.ShapeDtypeStruct((8, D), F32)],
        in_specs=[_row_spec(tm, D)] + [_row_spec(tm, D)] * n_nat + [_perm_spec(dil, tm, D) for dil, _ in dh_perm]
        + [_row_spec(tm, D), _vec_spec(D), _vec_spec(D)],
        out_specs=[_row_spec(tm, D), _vec_spec(D, 8)],
        scratch_shapes=[_stage_shape(tm, D) for _ in dh_perm],
        compiler_params=_params(("arbitrary",)),
    )(x, *dh_nat, *[a for _, a in dh_perm], dxo, g, s)


def _gate_bwd(dxo, f, gate, coef, name):
    S, D = dxo.shape
    tm = _pick(S, [512, 256, 128])

    def kern(dxo_ref, f_ref, gate_ref, df_ref, cs_ref):
        i = pl.program_id(0)
        dv = dxo_ref[...]
        df_ref[...] = ((coef * gate_ref[...]) * dv).astype(BF16)

        @pl.when(i == 0)
        def _():
            cs_ref[...] = jnp.zeros_like(cs_ref)

        cs_ref[0:1, :] += coef * jnp.sum(f_ref[...].astype(F32) * dv, axis=0, keepdims=True)

    return pl.pallas_call(
        kern, name=name, grid=(S // tm,),
        out_shape=[jax.ShapeDtypeStruct((S, D), BF16), jax.ShapeDtypeStruct((8, D), F32)],
        in_specs=[_row_spec(tm, D), _row_spec(tm, D), _vec_spec(D)],
        out_specs=[_row_spec(tm, D), _vec_spec(D, 8)],
        compiler_params=_params(("arbitrary",)),
    )(dxo, f, gate)


def _loss_head(x, g, target, name):
    S, D = x.shape
    tm = _pick(S, [256, 128])
    n = S // tm

    def kern(x_ref, g_ref, t_ref, dx_ref, cs_ref):
        i = pl.program_id(0)
        xv = x_ref[...]
        r = lax.rsqrt(jnp.mean(xv * xv, axis=1, keepdims=True) + EPS)
        xn = xv * r
        e = xn * g_ref[...] - t_ref[...]
        dxn = (e * (1.0 / D)) * g_ref[...]
        dx_ref[...] = r * (dxn - xn * jnp.mean(xn * dxn, axis=1, keepdims=True))

        @pl.when(i == 0)
        def _():
            cs_ref[...] = jnp.zeros_like(cs_ref)

        cs_ref[0:1, :] += jnp.sum(xn * e, axis=0, keepdims=True) * (1.0 / D)
        cs_ref[1:2, :] += jnp.sum(e * e, axis=0, keepdims=True)

        @pl.when(i == n - 1)
        def _():
            tot = jnp.sum(cs_ref[1:2, :], axis=1, keepdims=True) * (0.5 / D)
            cs_ref[2:3, :] = jnp.broadcast_to(tot, (1, D))

    return pl.pallas_call(
        kern, name=name, grid=(n,),
        out_shape=[jax.ShapeDtypeStruct((S, D), F32), jax.ShapeDtypeStruct((8, D), F32)],
        in_specs=[_row_spec(tm, D), _vec_spec(D), _row_spec(tm, D)],
        out_specs=[_row_spec(tm, D), _vec_spec(D, 8)],
        compiler_params=_params(("arbitrary",)),
    )(x, g, target)


def _shift_down(p, row, prev_rows):
    a, b = prev_rows
    p1 = jnp.where(row == 0, b, pltpu.roll(p, 1, 0))
    p2 = jnp.where(row == 0, a, jnp.where(row == 1, b, pltpu.roll(p, 2, 0)))
    return p1, p2


def _conv_fwd(cg, conv_w, name):
    S, D5 = cg.shape
    D = D5 // 5
    tm = _pick(S, [256, 128])
    t8 = tm // 8

    def prev(col):
        return pl.BlockSpec((8, D), lambda i: (jnp.maximum(i * t8 - 1, 0), col))

    def kern(cb_ref, cc_ref, ch_ref, ccp_ref, chp_ref, w_ref, y_ref):
        i = pl.program_id(0)
        keep = jnp.where(i > 0, 1.0, 0.0)
        p = cc_ref[...].astype(F32) * ch_ref[...].astype(F32)
        pa = ccp_ref[6:7, :].astype(F32) * chp_ref[6:7, :].astype(F32) * keep
        pb = ccp_ref[7:8, :].astype(F32) * chp_ref[7:8, :].astype(F32) * keep
        row = lax.broadcasted_iota(jnp.int32, (tm, D), 0)
        p1, p2 = _shift_down(p, row, (pa, pb))
        dw = w_ref[0:1, :] * p2 + w_ref[1:2, :] * p1 + w_ref[2:3, :] * p
        y_ref[...] = (cb_ref[...].astype(F32) * dw).astype(BF16)

    def col(cidx):
        return pl.BlockSpec((tm, D), lambda i: (i, cidx))

    return pl.pallas_call(
        kern, name=name, grid=(S // tm,),
        out_shape=jax.ShapeDtypeStruct((S, D), BF16),
        in_specs=[col(0), col(1), col(2), prev(1), prev(2), _vec_spec(D, CONV_K)],
        out_specs=_row_spec(tm, D),
        compiler_params=_params(("parallel",)),
    )(cg, cg, cg, cg, cg, conv_w)


def _conv_bwd(cg, dy, dgg, conv_w, name):
    S, D5 = cg.shape
    D = D5 // 5
    tm = _pick(S, [256, 128])
    t8 = tm // 8
    n = S // tm
    last8 = S // 8 - 1

    def prev(col):
        return pl.BlockSpec((8, D), lambda i: (jnp.maximum(i * t8 - 1, 0), col))

    def nxt(col):
        return pl.BlockSpec((8, D), lambda i: (jnp.minimum((i + 1) * t8, last8), col))

    def kern(cb_ref, cc_ref, ch_ref, dy_ref, dgg_ref, ccp_ref, chp_ref, cbn_ref, dyn_ref, w_ref, d_ref, cs_ref):
        i = pl.program_id(0)
        keep_p = jnp.where(i > 0, 1.0, 0.0)
        keep_n = jnp.where(i < n - 1, 1.0, 0.0)
        cb = cb_ref[...].astype(F32)
        cc = cc_ref[...].astype(F32)
        ch = ch_ref[...].astype(F32)
        dyv = dy_ref[...].astype(F32)
        p = cc * ch
        pa = ccp_ref[6:7, :].astype(F32) * chp_ref[6:7, :].astype(F32) * keep_p
        pb = ccp_ref[7:8, :].astype(F32) * chp_ref[7:8, :].astype(F32) * keep_p
        row = lax.broadcasted_iota(jnp.int32, (tm, D), 0)
        p1, p2 = _shift_down(p, row, (pa, pb))
        w0, w1, w2 = w_ref[0:1, :], w_ref[1:2, :], w_ref[2:3, :]
        dw = w0 * p2 + w1 * p1 + w2 * p
        ddw = dyv * cb
        na = dyn_ref[0:1, :].astype(F32) * cbn_ref[0:1, :].astype(F32) * keep_n
        nb = dyn_ref[1:2, :].astype(F32) * cbn_ref[1:2, :].astype(F32) * keep_n
        u1 = jnp.where(row == tm - 1, na, pltpu.roll(ddw, tm - 1, 0))
        u2 = jnp.where(row == tm - 2, na, jnp.where(row == tm - 1, nb, pltpu.roll(ddw, tm - 2, 0)))
        dp = w2 * ddw + w1 * u1 + w0 * u2
        d_ref[:, 0:D] = (dyv * dw).astype(BF16)
        d_ref[:, D:2 * D] = (dp * ch).astype(BF16)
        d_ref[:, 2 * D:3 * D] = (dp * cc).astype(BF16)
        d_ref[:, 3 * D:5 * D] = dgg_ref[...]

        @pl.when(i == 0)
        def _():
            cs_ref[...] = jnp.zeros_like(cs_ref)

        cs_ref[0:1, :] += jnp.sum(ddw * p2, axis=0, keepdims=True)
        cs_ref[1:2, :] += jnp.sum(ddw * p1, axis=0, keepdims=True)
        cs_ref[2:3, :] += jnp.sum(ddw * p, axis=0, keepdims=True)

    def col(cidx):
        return pl.BlockSpec((tm, D), lambda i: (i, cidx))

    return pl.pallas_call(
        kern, name=name, grid=(n,),
        out_shape=[jax.ShapeDtypeStruct((S, 5 * D), BF16), jax.ShapeDtypeStruct((8, D), F32)],
        in_specs=[col(0), col(1), col(2), _row_spec(tm, D), _row_spec(tm, 2 * D), prev(1), prev(2), nxt(0),
                  pl.BlockSpec((8, D), lambda i: (jnp.minimum((i + 1) * t8, last8), 0)), _vec_spec(D, CONV_K)],
        out_specs=[_row_spec(tm, 5 * D), _vec_spec(D, 8)],
        compiler_params=_params(("arbitrary",)),
    )(cg, cg, cg, dy, dgg, cg, cg, cg, dy, conv_w)


def _merge_fwd(cg, yc, ya, name):
    S, D = yc.shape
    tm = _pick(S, [512, 256, 128])

    def kern(gc_ref, ga_ref, yc_ref, ya_ref, m_ref):
        m_ref[...] = (jax.nn.sigmoid(gc_ref[...].astype(F32)) * yc_ref[...].astype(F32)
                      + jax.nn.sigmoid(ga_ref[...].astype(F32)) * ya_ref[...].astype(F32)).astype(BF16)

    return pl.pallas_call(
        kern, name=name, grid=(S // tm,),
        out_shape=jax.ShapeDtypeStruct((S, D), BF16),
        in_specs=[pl.BlockSpec((tm, D), lambda i: (i, 3)), pl.BlockSpec((tm, D), lambda i: (i, 4)),
                  _row_spec(tm, D), _row_spec(tm, D)],
        out_specs=_row_spec(tm, D),
        compiler_params=_params(("parallel",)),
    )(cg, cg, yc, ya)


def _merge_bwd(cg, yc, ya, dm, name):
    S, D = yc.shape
    tm = _pick(S, [256, 128])

    def kern(gc_ref, ga_ref, yc_ref, ya_ref, dm_ref, dyc_ref, dya_ref, dg_ref):
        dmv = dm_ref[...].astype(F32)
        sc = jax.nn.sigmoid(gc_ref[...].astype(F32))
        sa = jax.nn.sigmoid(ga_ref[...].astype(F32))
        dyc_ref[...] = (dmv * sc).astype(BF16)
        dya_ref[...] = (dmv * sa).astype(BF16)
        dg_ref[:, 0:D] = (dmv * yc_ref[...].astype(F32) * (sc * (1.0 - sc))).astype(BF16)
        dg_ref[:, D:2 * D] = (dmv * ya_ref[...].astype(F32) * (sa * (1.0 - sa))).astype(BF16)

    return pl.pallas_call(
        kern, name=name, grid=(S // tm,),
        out_shape=[jax.ShapeDtypeStruct((S, D), BF16), jax.ShapeDtypeStruct((S, D), BF16),
                   jax.ShapeDtypeStruct((S, 2 * D), BF16)],
        in_specs=[pl.BlockSpec((tm, D), lambda i: (i, 3)), pl.BlockSpec((tm, D), lambda i: (i, 4)),
                  _row_spec(tm, D), _row_spec(tm, D), _row_spec(tm, D)],
        out_specs=[_row_spec(tm, D), _row_spec(tm, D), pl.BlockSpec((tm, 2 * D), lambda i: (i, 0))],
        compiler_params=_params(("parallel",)),
    )(cg, cg, yc, ya, dm)


def _t5_bucket(dist):
    exact = NUM_BUCKETS // 2
    d = np.maximum(dist, 1).astype(np.float32)
    large = exact + (np.log(d / exact) / np.log(MAX_DISTANCE / exact) * (NUM_BUCKETS - exact)).astype(np.int32)
    large = np.minimum(large, NUM_BUCKETS - 1)
    return np.where(dist < exact, dist, large).astype(np.int32)


def _bucket_tables():
    i = np.arange(BLOCK)[:, None]
    j = np.arange(2 * BLOCK)[None, :]
    rel = i - j + BLOCK
    return np.stack([_t5_bucket(np.maximum(rel, 0) * d) for _, d in DILATION_GROUPS]).astype(np.int32)


def _band_masks():
    i = lax.broadcasted_iota(jnp.int32, (BLOCK, 2 * BLOCK), 0)
    j = lax.broadcasted_iota(jnp.int32, (BLOCK, 2 * BLOCK), 1)
    rel = i - j + BLOCK
    band = (rel >= 0) & (rel <= BLOCK)
    return band, band & (j >= BLOCK)


def _bias_build(rel_bias, buckets, name):
    def kern(rb_ref, bk_ref, o_ref):
        g = pl.program_id(0)
        bk = bk_ref[0]
        band, first = _band_masks()
        for h in range(HEADS_PER_GROUP):
            acc = jnp.zeros((BLOCK, 2 * BLOCK), F32)
            for b in range(NUM_BUCKETS):
                acc = jnp.where(bk == b, rb_ref[b, g * HEADS_PER_GROUP + h], acc)
            o_ref[0, 0, h] = jnp.where(first, acc, NEG_INF)
            o_ref[0, 1, h] = jnp.where(band, acc, NEG_INF)

    return pl.pallas_call(
        kern, name=name, grid=(N_GROUPS,),
        out_shape=jax.ShapeDtypeStruct((N_GROUPS, 2, HEADS_PER_GROUP, BLOCK, 2 * BLOCK), F32),
        in_specs=[pl.BlockSpec(memory_space=pltpu.SMEM),
                  pl.BlockSpec((1, BLOCK, 2 * BLOCK), lambda g: (g, 0, 0))],
        out_specs=pl.BlockSpec((1, 2, HEADS_PER_GROUP, BLOCK, 2 * BLOCK), lambda g: (g, 0, 0, 0, 0)),
        compiler_params=_params(("parallel",)),
    )(rel_bias, buckets)


def _bias_bwd(dlog, buckets, name):
    def kern(dl_ref, bk_ref, o_ref):
        g = pl.program_id(0)
        bk = bk_ref[0]
        rowi = lax.broadcasted_iota(jnp.int32, (NUM_BUCKETS, 128), 0)
        coli = lax.broadcasted_iota(jnp.int32, (NUM_BUCKETS, 128), 1)

        @pl.when(g == 0)
        def _():
            o_ref[...] = jnp.zeros_like(o_ref)

        acc = jnp.zeros((NUM_BUCKETS, 128), F32)
        for h in range(HEADS_PER_GROUP):
            dv = dl_ref[0, h]
            for b in range(NUM_BUCKETS):
                t = jnp.sum(jnp.where(bk == b, dv, 0.0), axis=0, keepdims=True)
                t = jnp.sum(t, axis=1, keepdims=True)
                acc = acc + jnp.where((rowi == b) & (coli == g * HEADS_PER_GROUP + h), t, 0.0)
        o_ref[...] += acc

    return pl.pallas_call(
        kern, name=name, grid=(N_GROUPS,),
        out_shape=jax.ShapeDtypeStruct((NUM_BUCKETS, 128), F32),
        in_specs=[pl.BlockSpec((1, HEADS_PER_GROUP, BLOCK, 2 * BLOCK), lambda g: (g, 0, 0, 0)),
                  pl.BlockSpec((1, BLOCK, 2 * BLOCK), lambda g: (g, 0, 0))],
        out_specs=pl.BlockSpec((NUM_BUCKETS, 128), lambda g: (0, 0)),
        compiler_params=_params(("arbitrary",)),
    )(dlog, buckets)


def _head_masks():
    lane = lax.broadcasted_iota(jnp.int32, (BLOCK, 128), 1)
    lo = lane < HEAD_DIM
    return lo, jnp.logical_not(lo)


def _attn_fwd(qkv, bias, d, name):
    S = qkv.shape[0]
    nb = S // d // BLOCK

    def kern(q_ref, kp_ref, kc_ref, vp_ref, vc_ref, b_ref, o_ref, lse_ref):
        lo, hi = _head_masks()
        for p in range(HEADS_PER_GROUP // 2):
            sl = slice(128 * p, 128 * (p + 1))
            q = q_ref[:, sl]
            k = jnp.concatenate([kp_ref[:, sl], kc_ref[:, sl]], axis=0)
            v = jnp.concatenate([vp_ref[:, sl], vc_ref[:, sl]], axis=0)
            o2, l2 = [], []
            for hh, msk in enumerate((lo, hi)):
                qm = jnp.where(msk, q, jnp.zeros_like(q))
                s = lax.dot_general(qm, k, NT, preferred_element_type=F32) * SCALE + b_ref[0, 2 * p + hh]
                m = jnp.max(s, axis=1, keepdims=True)
                e = jnp.exp(s - m)
                l = jnp.sum(e, axis=1, keepdims=True)
                o2.append(lax.dot_general(e.astype(BF16), v, NN, preferred_element_type=F32) / l)
                l2.append(jnp.broadcast_to(m + jnp.log(l), (BLOCK, 128)))
            o_ref[:, sl] = jnp.where(lo, o2[0], o2[1])
            lse_ref[:, sl] = jnp.where(lo, l2[0], l2[1])

    def blk(col, prev):
        if prev:
            return pl.BlockSpec((BLOCK, ATTN_OUT), lambda r, n: (r * nb + jnp.maximum(n - 1, 0), col))
        return pl.BlockSpec((BLOCK, ATTN_OUT), lambda r, n: (r * nb + n, col))

    o_spec = pl.BlockSpec((BLOCK, ATTN_OUT), lambda r, n: (r * nb + n, 0))
    return pl.pallas_call(
        kern, name=name, grid=(d, nb),
        out_shape=[jax.ShapeDtypeStruct((S, ATTN_OUT), F32)] * 2,
        in_specs=[blk(0, False), blk(1, True), blk(1, False), blk(2, True), blk(2, False),
                  pl.BlockSpec((1, HEADS_PER_GROUP, BLOCK, 2 * BLOCK), lambda r, n: (jnp.minimum(n, 1), 0, 0, 0))],
        out_specs=[o_spec, o_spec],
        compiler_params=_params(("parallel", "arbitrary")),
    )(qkv, qkv, qkv, qkv, qkv, bias)


def _attn_bwd(qkv, do, lse, delta, bias, d, name):
    S = qkv.shape[0]
    nb = S // d // BLOCK
    low = -3.0e38

    def kern(q_ref, kp_ref, kc_ref, vp_ref, vc_ref, do_ref, lse_ref, dl_ref, b_ref,
             dq_ref, dkv_ref, db_ref, ck_ref, cv_ref):
        r, n = pl.program_id(0), pl.program_id(1)

        @pl.when((r == 0) & (n == 0))
        def _():
            db_ref[...] = jnp.zeros_like(db_ref)

        @pl.when(n == 0)
        def _():
            ck_ref[...] = jnp.zeros_like(ck_ref)
            cv_ref[...] = jnp.zeros_like(cv_ref)

        @pl.when(n < nb)
        def _():
            lo, hi = _head_masks()
            for p in range(HEADS_PER_GROUP // 2):
                sl = slice(128 * p, 128 * (p + 1))
                sv = slice(ATTN_OUT + 128 * p, ATTN_OUT + 128 * (p + 1))
                q = q_ref[:, sl]
                k = jnp.concatenate([kp_ref[:, sl], kc_ref[:, sl]], axis=0)
                v = jnp.concatenate([vp_ref[:, sl], vc_ref[:, sl]], axis=0)
                dov = do_ref[:, sl]
                lse_b = lse_ref[:, sl]
                del_b = dl_ref[:, sl]
                dq2 = []
                dk_acc = jnp.zeros((2 * BLOCK, 128), F32)
                dv_acc = jnp.zeros((2 * BLOCK, 128), F32)
                for hh, msk in enumerate((lo, hi)):
                    qm = jnp.where(msk, q, jnp.zeros_like(q))
                    dom = jnp.where(msk, dov, jnp.zeros_like(dov))
                    lse_h = jnp.max(jnp.where(msk, lse_b, low), axis=1, keepdims=True)
                    del_h = jnp.max(jnp.where(msk, del_b, low), axis=1, keepdims=True)
                    s = lax.dot_general(qm, k, NT, preferred_element_type=F32) * SCALE + b_ref[0, 2 * p + hh]
                    pr = jnp.exp(s - lse_h)
                    dp = lax.dot_general(dom, v, NT, preferred_element_type=F32)
                    ds = pr * (dp - del_h)
                    db_ref[2 * p + hh] += ds
                    dsb = (ds * SCALE).astype(BF16)
                    dq2.append(lax.dot_general(dsb, k, NN, preferred_element_type=F32))
                    dk_acc = dk_acc + lax.dot_general(dsb, qm, TN, preferred_element_type=F32)
                    dv_acc = dv_acc + lax.dot_general(pr.astype(BF16), dom, TN, preferred_element_type=F32)
                dq_ref[:, sl] = jnp.where(lo, dq2[0], dq2[1]).astype(BF16)
                dkv_ref[:, sl] = (ck_ref[:, sl] + dk_acc[0:BLOCK]).astype(BF16)
                dkv_ref[:, sv] = (cv_ref[:, sl] + dv_acc[0:BLOCK]).astype(BF16)
                ck_ref[:, sl] = dk_acc[BLOCK:2 * BLOCK]
                cv_ref[:, sl] = dv_acc[BLOCK:2 * BLOCK]

        @pl.when(n == nb)
        def _():
            dkv_ref[:, 0:ATTN_OUT] = ck_ref[...].astype(BF16)
            dkv_ref[:, ATTN_OUT:2 * ATTN_OUT] = cv_ref[...].astype(BF16)

    def cur(n):
        return jnp.minimum(n, nb - 1)

    def blk(col, prev):
        if prev:
            return pl.BlockSpec((BLOCK, ATTN_OUT), lambda r, n: (r * nb + jnp.maximum(cur(n) - 1, 0), col))
        return pl.BlockSpec((BLOCK, ATTN_OUT), lambda r, n: (r * nb + cur(n), col))

    q_like = pl.BlockSpec((BLOCK, ATTN_OUT), lambda r, n: (r * nb + cur(n), 0))
    return pl.pallas_call(
        kern, name=name, grid=(d, nb + 1),
        out_shape=[jax.ShapeDtypeStruct((S, ATTN_OUT), BF16), jax.ShapeDtypeStruct((S, 2 * ATTN_OUT), BF16),
                   jax.ShapeDtypeStruct((HEADS_PER_GROUP, BLOCK, 2 * BLOCK), F32)],
        in_specs=[blk(0, False), blk(1, True), blk(1, False), blk(2, True), blk(2, False),
                  q_like, q_like, q_like,
                  pl.BlockSpec((1, HEADS_PER_GROUP, BLOCK, 2 * BLOCK),
                               lambda r, n: (jnp.minimum(cur(n), 1), 0, 0, 0))],
        out_specs=[q_like,
                   pl.BlockSpec((BLOCK, 2 * ATTN_OUT), lambda r, n: (r * nb + jnp.maximum(n - 1, 0), 0)),
                   pl.BlockSpec((HEADS_PER_GROUP, BLOCK, 2 * BLOCK), lambda r, n: (0, 0, 0))],
        scratch_shapes=[pltpu.VMEM((BLOCK, ATTN_OUT), F32), pltpu.VMEM((BLOCK, ATTN_OUT), F32)],
        compiler_params=_params(("arbitrary", "arbitrary")),
    )(qkv, qkv, qkv, qkv, qkv, do, lse, delta, bias)


def _by_residue(a, dil):
    return a if dil == 1 else a.reshape(dil, a.shape[0] // dil, a.shape[1])


def _flat(a):
    return a if a.ndim == 2 else a.reshape(a.shape[0] * a.shape[1], a.shape[2])


def _combine_fwd(os_, lses, name):
    S, W = os_[0].shape
    tm = _pick(S, [256, 128])
    perm = [dil for dil in DILS if dil > 1]

    def kern(*refs):
        o_in, l_in = refs[0:N_GROUPS], refs[N_GROUPS:2 * N_GROUPS]
        of_ref, ob_ref, lse_ref = refs[2 * N_GROUPS:2 * N_GROUPS + 3]
        lse_p = refs[2 * N_GROUPS + 3:2 * N_GROUPS + 3 + len(perm)]
        scr = refs[2 * N_GROUPS + 3 + len(perm):]
        ov, lv = [], []
        si = 0
        for g, dil in enumerate(DILS):
            if dil == 1:
                ov.append(o_in[g][...])
                lv.append(l_in[g][...])
            else:
                so, sl = scr[si], scr[si + 1]
                si += 2
                for res in range(dil):
                    _put_residue(so, res, dil, o_in[g][res])
                    _put_residue(sl, res, dil, l_in[g][res])
                ov.append(_unstage(so))
                lv.append(_unstage(sl))
        m = jnp.maximum(jnp.maximum(lv[0], lv[1]), lv[2])
        e = [jnp.exp(t - m) for t in lv]
        tot = e[0] + e[1] + e[2]
        o = (e[0] * ov[0] + e[1] * ov[1] + e[2] * ov[2]) / tot
        lse = m + jnp.log(tot)
        of_ref[...] = o
        ob_ref[...] = o.astype(BF16)
        lse_ref[...] = lse
        sl = scr[1]
        _stage(sl, lse)
        for dil, p_ref in zip(perm, lse_p):
            for res in range(dil):
                p_ref[res] = _get_residue(sl, res, dil)

    def in_spec(dil):
        return _row_spec(tm, W) if dil == 1 else _perm_spec(dil, tm, W)

    ins = [_by_residue(a, dil) for a, dil in zip(os_, DILS)] + [_by_residue(a, dil) for a, dil in zip(lses, DILS)]
    return pl.pallas_call(
        kern, name=name, grid=(S // tm,),
        out_shape=[jax.ShapeDtypeStruct((S, W), F32), jax.ShapeDtypeStruct((S, W), BF16),
                   jax.ShapeDtypeStruct((S, W), F32)]
        + [jax.ShapeDtypeStruct((dil, S // dil, W), F32) for dil in perm],
        in_specs=[in_spec(dil) for dil in DILS] * 2,
        out_specs=[_row_spec(tm, W)] * 3 + [_perm_spec(dil, tm, W) for dil in perm],
        scratch_shapes=[_stage_shape(tm, W) for _ in range(2 * len(perm))],
        compiler_params=_params(("parallel",)),
    )(*ins)


def _delta(do, o, name):
    S, W = o.shape
    tm = _pick(S, [256, 128])
    perm = [dil for dil in DILS if dil > 1]

    def kern(do_ref, o_ref, dob_ref, d_ref, *rest):
        scr, scr_do = rest[2 * len(perm)], rest[2 * len(perm) + 1]
        prod = do_ref[...] * o_ref[...]
        ri = jnp.right_shift(lax.broadcasted_iota(jnp.int32, (W, W), 0), HEAD_SHIFT)
        ci = jnp.right_shift(lax.broadcasted_iota(jnp.int32, (W, W), 1), HEAD_SHIFT)
        same = jnp.where(ri == ci, 1.0, 0.0).astype(BF16)
        hi_p = prod.astype(BF16)
        lo_p = (prod - hi_p.astype(F32)).astype(BF16)
        dl = (lax.dot_general(hi_p, same, NN, preferred_element_type=F32)
              + lax.dot_general(lo_p, same, NN, preferred_element_type=F32))
        d_ref[...] = dl
        dob_ref[...] = do_ref[...].astype(BF16)
        _stage(scr, dl)
        _stage(scr_do, do_ref[...])
        for j, dil in enumerate(perm):
            for res in range(dil):
                rest[2 * j][res] = _get_residue(scr_do, res, dil).astype(BF16)
                rest[2 * j + 1][res] = _get_residue(scr, res, dil)

    out_shape = [jax.ShapeDtypeStruct((S, W), BF16), jax.ShapeDtypeStruct((S, W), F32)]
    out_specs = [_row_spec(tm, W), _row_spec(tm, W)]
    for dil in perm:
        out_shape += [jax.ShapeDtypeStruct((dil, S // dil, W), BF16), jax.ShapeDtypeStruct((dil, S // dil, W), F32)]
        out_specs += [_perm_spec(dil, tm, W), _perm_spec(dil, tm, W)]
    return pl.pallas_call(
        kern, name=name, grid=(S // tm,),
        out_shape=out_shape,
        in_specs=[_row_spec(tm, W), _row_spec(tm, W)], out_specs=out_specs,
        scratch_shapes=[_stage_shape(tm, W), _stage_shape(tm, W)],
        compiler_params=_params(("parallel",)),
    )(do, o)


def _ada_fwd(c16, ada_w, name):
    depth, D, n = ada_w.shape
    rows = 2 * N_DEV

    def kern(c_ref, w_ref, o_ref, cs_ref):
        cv = c_ref[...]
        cs = cv * jax.nn.sigmoid(cv)
        cs_ref[...] = cs
        o_ref[0] = _dot3(cs, w_ref[0], NN)

    return pl.pallas_call(
        kern, name=name, grid=(depth,),
        out_shape=[jax.ShapeDtypeStruct((depth, rows, n), F32), jax.ShapeDtypeStruct((rows, D), F32)],
        in_specs=[pl.BlockSpec((rows, D), lambda l: (0, 0)), pl.BlockSpec((1, D, n), lambda l: (l, 0, 0))],
        out_specs=[pl.BlockSpec((1, rows, n), lambda l: (l, 0, 0)), pl.BlockSpec((rows, D), lambda l: (0, 0))],
        compiler_params=_params(("arbitrary",)),
    )(c16, ada_w)


def _ada_bwd(cs16, dm16, name):
    depth, _, n = dm16.shape
    D = cs16.shape[1]

    def kern(cs_ref, dm_ref, o_ref):
        o_ref[0] = _dot3(cs_ref[...], dm_ref[0], TN)

    return pl.pallas_call(
        kern, name=name, grid=(depth,),
        out_shape=jax.ShapeDtypeStruct((depth, D, n), F32),
        in_specs=[pl.BlockSpec((2 * N_DEV, D), lambda l: (0, 0)), pl.BlockSpec((1, 2 * N_DEV, n), lambda l: (l, 0, 0))],
        out_specs=pl.BlockSpec((1, D, n), lambda l: (l, 0, 0)),
        compiler_params=_params(("parallel",)),
    )(cs16, dm16)


def _sum_rows8(parts, name):
    _, r, n = parts.shape

    def kern(p_ref, o_ref):
        acc = p_ref[0]
        for k in range(1, N_DEV):
            acc = acc + p_ref[k]
        o_ref[...] = acc

    return pl.pallas_call(
        kern, name=name, out_shape=jax.ShapeDtypeStruct((r, n), F32),
        in_specs=[pl.BlockSpec(memory_space=pltpu.VMEM)], out_specs=pl.BlockSpec(memory_space=pltpu.VMEM),
    )(parts)


def _adamw(w, g, m, v, name):
    shape = w.shape
    c = shape[-1]
    r = int(np.prod(shape[:-1])) if len(shape) > 1 else 1
    w2, g2, m2, v2 = (t.reshape(r, c) for t in (w, g, m, v))
    tr = r
    for cand in (2048, 1024, 512, 256, 128, 64, 32, 16, 8):
        if r % cand == 0 and cand * c * 4 <= (1 << 20):
            tr = cand
            break
    c1 = 1.0 - ADAM_B1 ** ADAM_STEP
    c2 = 1.0 - ADAM_B2 ** ADAM_STEP

    def kern(w_ref, g_ref, m_ref, v_ref, d_ref, nm_ref, nv_ref):
        gv = g_ref[...]
        nm = ADAM_B1 * m_ref[...] + (1.0 - ADAM_B1) * gv
        nv = ADAM_B2 * v_ref[...] + (1.0 - ADAM_B2) * (gv * gv)
        nm_ref[...] = nm
        nv_ref[...] = nv
        d_ref[...] = -ADAM_LR * ((nm / c1) / (jnp.sqrt(nv / c2) + ADAM_EPS) + ADAM_WD * w_ref[...])

    spec = pl.BlockSpec((tr, c), lambda i: (i, 0))
    outs = pl.pallas_call(
        kern, name=name, grid=(r // tr,),
        out_shape=[jax.ShapeDtypeStruct((r, c), F32)] * 3,
        in_specs=[spec] * 4, out_specs=[spec] * 3,
        compiler_params=_params(("parallel",)),
    )(w2, g2, m2, v2)
    return tuple(o.reshape(shape) for o in outs)


def kernel(x, c, ada_w, ada_b, norm_g, ffn_w_gate, ffn_w_up, ffn_w_down, w_in, conv_w, w_conv_out, w_attn_out, w_o, rel_bias, final_g, loss_target, m_ada_w, m_ada_b, m_norm_g, m_ffn_w_gate, m_ffn_w_up, m_ffn_w_down, m_w_in, m_conv_w, m_w_conv_out, m_w_attn_out, m_w_o, m_rel_bias, m_final_g, v_ada_w, v_ada_b, v_norm_g, v_ffn_w_gate, v_ffn_w_up, v_ffn_w_down, v_w_in, v_conv_w, v_w_conv_out, v_w_attn_out, v_w_o, v_rel_bias, v_final_g):
    depth = ada_w.shape[0]
    S, D = x.shape[1], x.shape[2]
    me = 4 * lax.axis_index("x") + 2 * lax.axis_index("y") + lax.axis_index("c")
    x0 = x.reshape(S, D)
    target = loss_target.reshape(S, D)
    fsh = ffn_w_down.shape[2]
    insh = w_in.shape[2]
    dsh = D // N_DEV
    ao_rows = dsh * ATTN_OUT // D

    piece_rows = [fsh] * 6 + [insh, dsh, dsh, ao_rows]

    def pack(l):
        def t(a):
            return jnp.transpose(a).astype(BF16)
        ps = [t(ffn_w_gate[l, 0]), t(ffn_w_gate[l, 1]), t(ffn_w_up[l, 0]), t(ffn_w_up[l, 1]),
              ffn_w_down[l, 0].astype(BF16), ffn_w_down[l, 1].astype(BF16), t(w_in[l]),
              w_conv_out[l].astype(BF16), w_o[l].astype(BF16), t(w_attn_out[l]).reshape(ao_rows, D)]
        return jnp.concatenate(ps, axis=0)

    def unpack(full):
        in_t = full[6]
        qkv_t = [jnp.concatenate([in_t[t * QKV_W + g * ATTN_OUT: t * QKV_W + (g + 1) * ATTN_OUT] for t in range(3)])
                 for g in range(N_GROUPS)]
        ao_t = full[9].reshape(N_DEV, dsh, ATTN_OUT).reshape(D, ATTN_OUT)
        return dict(g_t=full[0:2], u_t=full[2:4], down=full[4:6], qkv_t=qkv_t, cg_t=in_t[3 * QKV_W:],
                    co=full[7], wo=full[8], ao_t=ao_t)

    def behind(v, token):
        return v + token[0, 0]

    gather = _gather_start(pack(0), piece_rows, c, "weights_gather_start_l0")

    c_all = _all_gather(c.reshape(D // 128, 128), "c_all_gather").reshape(N_DEV, D)
    c16 = jnp.concatenate([c_all, jnp.zeros_like(c_all)], axis=0)
    mod_part, cs16 = _ada_fwd(c16, ada_w, "ada_fwd")
    mod_part = mod_part[:, :N_DEV]
    n_ada = ada_w.shape[2]
    mod_all = _all_gather(mod_part.reshape(depth * N_DEV * n_ada // 128, 128), "mod_all_gather")
    mod_all = mod_all.reshape(N_DEV, depth, N_DEV, n_ada)
    mod_mine = lax.dynamic_index_in_dim(mod_all, me, axis=2, keepdims=False)
    mod = jnp.transpose(mod_mine, (1, 0, 2)).reshape(depth, N_DEV * n_ada) + ada_b
    mod = mod.reshape(depth, 3, 3, 1, D)

    small = jnp.concatenate([norm_g.reshape(-1), conv_w.reshape(-1)]).reshape(-1, 128)
    small_all = _all_gather(small, "small_all_gather").reshape(N_DEV, -1)
    n_ng = norm_g.size
    norm_g_full = jnp.transpose(small_all[:, :n_ng].reshape(N_DEV, depth, 3, dsh), (1, 2, 0, 3)).reshape(depth, 3, 1, D)
    conv_w_full = jnp.transpose(small_all[:, n_ng:].reshape(N_DEV, depth, CONV_K, dsh), (1, 2, 0, 3)).reshape(depth, CONV_K, D)

    buckets = jnp.asarray(_bucket_tables())
    bias = _bias_build(rel_bias, buckets, "bias_build")
    perm_dils = tuple(dil for dil in DILS if dil > 1)

    fwd = _gather_forward(gather[0], gather[1], gather[2], piece_rows, bias, "weights_gather_forward_l0")
    W = [unpack(_gather_finish(fwd[0], fwd[1], fwd[2], bias, "weights_gather_finish_l0"))]

    saved = []
    xc = x0
    for l in range(depth):
        sv = {}
        gather = None
        if l + 1 < depth:
            gather = _gather_start(pack(l + 1), piece_rows, W[l]["wo"], f"weights_gather_start_l{l + 1}")
        for sub in (0, 1, 2):
            g, sh, sc, gt = norm_g_full[l, sub], mod[l, sub, 0], mod[l, sub, 1], mod[l, sub, 2]
            if sub == 0 and gather is not None:
                g = behind(g, gather[3])
            rec = dict(x=xc)
            if sub != 1:
                i = 0 if sub == 0 else 1
                h = _norm_mod_fwd(xc, g, sc, sh, "norm_mod_fwd")[0]
                a, u, z = _ffn_up(h, W[l]["g_t"][i], W[l]["u_t"][i], "ffn_up")
                xc, f = _matmul(z, W[l]["down"][i], "nn", BF16, "ffn_down", tm=512, resid=(xc, gt, 0.5))
                rec.update(h=h, a=a, u=u, z=z, f=f)
            else:
                hs = _norm_mod_fwd(xc, g, sc, sh, "norm_mod_fwd_mixer", dils=perm_dils)
                h = hs[0]
                h_res = [h] + [_flat(t) for t in hs[1:]]
                cg = _matmul(h, W[l]["cg_t"], "nt", BF16, "mixer_cg")
                qkvs, os_, lses = [], [], []
                for gi, dil in enumerate(DILS):
                    qkv = _matmul(h_res[gi], W[l]["qkv_t"][gi], "nt", BF16, "mixer_qkv")
                    o_g, lse_g = _attn_fwd(qkv, bias[gi], dil, f"attn_fwd_g{gi}")
                    qkvs.append(qkv)
                    os_.append(o_g)
                    lses.append(lse_g)
                comb = _combine_fwd(os_, lses, "combine_fwd")
                o_f, o_b, lse = comb[0:3]
                lse_res = [lse] + [_flat(t) for t in comb[3:]]
                yc_in = _conv_fwd(cg, conv_w_full[l], "conv_fwd")
                yc = _matmul(yc_in, W[l]["co"], "nn", BF16, "conv_out")
                ya = _matmul(o_b, W[l]["ao_t"], "nt", BF16, "attn_out")
                merged = _merge_fwd(cg, yc, ya, "merge_fwd")
                xc, f = _matmul(merged, W[l]["wo"], "nn", BF16, "mixer_out", resid=(xc, gt, 1.0))
                rec.update(h=h, h_res=h_res, qkvs=qkvs, cg=cg, o_f=o_f, o_b=o_b, lse_res=lse_res, yc_in=yc_in,
                           yc=yc, ya=ya, merged=merged, f=f)
                if gather is not None:
                    fwd = _gather_forward(gather[0], gather[1], gather[2], piece_rows, xc,
                                          f"weights_gather_forward_l{l + 1}")
            sv[sub] = rec
        if gather is not None:
            W.append(unpack(_gather_finish(fwd[0], fwd[1], fwd[2], xc, f"weights_gather_finish_l{l + 1}")))
        saved.append(sv)

    dx, head = _loss_head(xc, final_g.reshape(1, D), target, "loss_head")
    d_final_g = head[0]
    loss_part = head[2, 0]

    d_mod = [[None] * 3 for _ in range(depth)]
    d_norm = [[None] * 3 for _ in range(depth)]
    d_conv = [None] * depth
    g_rows = [None] * depth
    dlog = jnp.zeros((N_GROUPS, HEADS_PER_GROUP, BLOCK, 2 * BLOCK), F32)
    scatter = None
    for l in reversed(range(depth)):
        dW = {}
        for sub in (2, 1, 0):
            rec = saved[l][sub]
            g, sc, gt = norm_g_full[l, sub], mod[l, sub, 1], mod[l, sub, 2]
            if sub == 2 and scatter is not None:
                gt = behind(gt, scatter[3])
            if sub != 1:
                i = 0 if sub == 0 else 1
                df, gsum = _gate_bwd(dx, rec["f"], gt, 0.5, "gate_bwd")
                dz = _matmul(df, W[l]["down"][i], "nt", BF16, "ffn_down_dx")
                dW["down", i] = _matmul(rec["z"], df, "tn", BF16, "ffn_down_dw")
                da, du, dh = _ffn_up_bwd(dz, rec["a"], rec["u"], W[l]["g_t"][i], W[l]["u_t"][i], "ffn_up_bwd")
                dW["g_t", i] = _matmul(da, rec["h"], "tn", BF16, "ffn_gate_dw")
                dW["u_t", i] = _matmul(du, rec["h"], "tn", BF16, "ffn_up_dw")
                dx, sums = _norm_mod_bwd(rec["x"], [dh], [], dx, g, sc, "norm_mod_bwd")
            else:
                dout, gsum = _gate_bwd(dx, rec["f"], gt, 1.0, "gate_bwd_mixer")
                dm = _matmul(dout, W[l]["wo"], "nt", BF16, "mixer_out_dx")
                dW["wo"] = _matmul(rec["merged"], dout, "tn", BF16, "mixer_out_dw")
                dyc, dya, dgg = _merge_bwd(rec["cg"], rec["yc"], rec["ya"], dm, "merge_bwd")
                dyc_in = _matmul(dyc, W[l]["co"], "nt", BF16, "conv_out_dx")
                dW["co"] = _matmul(rec["yc_in"], dyc, "tn", BF16, "conv_out_dw")
                do = _matmul(dya, W[l]["ao_t"], "nn", F32, "attn_out_dx")
                dW["ao_t"] = _matmul(dya, rec["o_b"], "tn", BF16, "attn_out_dw")
                dl = _delta(do, rec["o_f"], "attn_delta")
                do_res = [dl[0]] + [_flat(t) for t in dl[2::2]]
                del_res = [dl[1]] + [_flat(t) for t in dl[3::2]]
                dh_attn, dw_q, dw_kv, dlog_l = [], [], [], []
                for gi, dil in enumerate(DILS):
                    dq, dkv, dlg = _attn_bwd(rec["qkvs"][gi], do_res[gi], rec["lse_res"][gi], del_res[gi],
                                             bias[gi], dil, f"attn_bwd_g{gi}")
                    dlog_l.append(dlg)
                    dh_attn.append(_attn_dh(dq, dkv, W[l]["qkv_t"][gi], "attn_dh"))
                    dw_q.append(_matmul(dq, rec["h_res"][gi], "tn", BF16, "mixer_q_dw"))
                    dw_kv.append(_matmul(dkv, rec["h_res"][gi], "tn", BF16, "mixer_kv_dw"))
                dlog = dlog + jnp.stack(dlog_l)
                dcg, conv_sum = _conv_bwd(rec["cg"], dyc_in, dgg, conv_w_full[l], "conv_bwd")
                d_conv[l] = conv_sum[0:CONV_K]
                dh_cg = _matmul(dcg, W[l]["cg_t"], "nn", F32, "mixer_cg_dx")
                dw_cg = _matmul(dcg, rec["h"], "tn", BF16, "mixer_cg_dw")
                dW["in_t"] = jnp.concatenate(
                    dw_q + [t[:ATTN_OUT] for t in dw_kv] + [t[ATTN_OUT:] for t in dw_kv] + [dw_cg], axis=0)
                perm_parts = [(dil, _by_residue(dh_attn[gi], dil)) for gi, dil in enumerate(DILS) if dil > 1]
                dx, sums = _norm_mod_bwd(rec["x"], [dh_cg, dh_attn[0]], perm_parts, dx, g, sc, "norm_mod_bwd_mixer")
            d_mod[l][sub] = jnp.stack([sums[0], sums[2], gsum[0]])
            d_norm[l][sub] = sums[3]
        pieces = [dW["g_t", 0], dW["g_t", 1], dW["u_t", 0], dW["u_t", 1], dW["down", 0], dW["down", 1],
                  dW["in_t"], dW["co"], dW["wo"], dW["ao_t"].reshape(N_DEV * ao_rows, D)]
        after = dx
        if scatter is not None:
            after = _scatter_finish(scatter[0], scatter[1], scatter[2], dx, f"grads_scatter_finish_l{l + 1}")
            g_rows[l + 1] = _sum_sources(after, "grads_sum")
        scatter = _scatter_start(pieces, after, f"grads_scatter_start_l{l}")
    recv = _scatter_finish(scatter[0], scatter[1], scatter[2], dx, "grads_scatter_finish_l0")
    g_rows[0] = _sum_sources(recv, "grads_sum")
    grad_x = dx.reshape(1, S, D)
    d_rel = _bias_bwd(dlog, buckets, "bias_bwd")[:, :rel_bias.shape[1]]

    offs = _offsets(piece_rows)

    def shard_grad(p, transpose, shape=None):
        rows = [g_rows[l][offs[p]:offs[p] + piece_rows[p]] for l in range(depth)]
        if shape is not None:
            rows = [t.reshape(shape) for t in rows]
        return jnp.stack([jnp.transpose(t) if transpose else t for t in rows])

    g_gate = jnp.stack([shard_grad(0, True), shard_grad(1, True)], axis=1)
    g_up = jnp.stack([shard_grad(2, True), shard_grad(3, True)], axis=1)
    g_down = jnp.stack([shard_grad(4, False), shard_grad(5, False)], axis=1)
    g_w_in = shard_grad(6, True)
    g_co = shard_grad(7, False)
    g_wo = shard_grad(8, False)
    g_ao = shard_grad(9, True, (dsh, ATTN_OUT))

    d_mod_flat = jnp.stack([jnp.stack(d_mod[l]) for l in range(depth)]).reshape(-1)
    d_norm_flat = jnp.stack([jnp.stack(d_norm[l]) for l in range(depth)]).reshape(-1)
    d_conv_flat = jnp.stack(d_conv).reshape(-1)
    vec = jnp.concatenate([d_mod_flat, d_norm_flat, d_conv_flat, d_rel.reshape(-1), d_final_g,
                           jnp.broadcast_to(loss_part, (128,))])
    pad = (-vec.size) % 1024
    vec = jnp.concatenate([vec, jnp.zeros((pad,), F32)]).reshape(-1, 128)
    parts = _all_gather(vec, "small_grads_all_gather").reshape(N_DEV, vec.shape[0], 128)
    tot = _sum_rows8(parts, "small_grads_sum").reshape(-1)
    o0 = 0
    g_ada_b = tot[o0:o0 + d_mod_flat.size].reshape(ada_b.shape)
    o0 += d_mod_flat.size
    g_norm_full = tot[o0:o0 + d_norm_flat.size].reshape(depth, 3, D)
    o0 += d_norm_flat.size
    g_conv_full = tot[o0:o0 + d_conv_flat.size].reshape(depth, CONV_K, D)
    o0 += d_conv_flat.size
    g_rel = tot[o0:o0 + rel_bias.size].reshape(rel_bias.shape)
    o0 += rel_bias.size
    g_final = tot[o0:o0 + D]
    o0 += D
    loss = tot[o0]
    g_norm = lax.dynamic_slice_in_dim(g_norm_full, me * dsh, dsh, axis=2)
    g_conv = lax.dynamic_slice_in_dim(g_conv_full, me * dsh, dsh, axis=2)

    dm_all = parts.reshape(N_DEV, -1)[:, :d_mod_flat.size].reshape(N_DEV, depth, N_DEV * n_ada)
    dm_cols = lax.dynamic_slice_in_dim(dm_all, me * n_ada, n_ada, axis=2)
    dm16 = jnp.concatenate([jnp.transpose(dm_cols, (1, 0, 2)), jnp.zeros((depth, N_DEV, n_ada), F32)], axis=1)
    g_ada_w = _ada_bwd(cs16, dm16, "ada_bwd")

    grads = dict(ada_w=g_ada_w, ada_b=g_ada_b, norm_g=g_norm, ffn_w_gate=g_gate, ffn_w_up=g_up,
                 ffn_w_down=g_down, w_in=g_w_in, conv_w=g_conv, w_conv_out=g_co, w_attn_out=g_ao, w_o=g_wo,
                 rel_bias=g_rel, final_g=g_final)
    weights = dict(ada_w=ada_w, ada_b=ada_b, norm_g=norm_g, ffn_w_gate=ffn_w_gate, ffn_w_up=ffn_w_up,
                   ffn_w_down=ffn_w_down, w_in=w_in, conv_w=conv_w, w_conv_out=w_conv_out, w_attn_out=w_attn_out,
                   w_o=w_o, rel_bias=rel_bias, final_g=final_g)
    ms = dict(ada_w=m_ada_w, ada_b=m_ada_b, norm_g=m_norm_g, ffn_w_gate=m_ffn_w_gate, ffn_w_up=m_ffn_w_up,
              ffn_w_down=m_ffn_w_down, w_in=m_w_in, conv_w=m_conv_w, w_conv_out=m_w_conv_out,
              w_attn_out=m_w_attn_out, w_o=m_w_o, rel_bias=m_rel_bias, final_g=m_final_g)
    vs = dict(ada_w=v_ada_w, ada_b=v_ada_b, norm_g=v_norm_g, ffn_w_gate=v_ffn_w_gate, ffn_w_up=v_ffn_w_up,
              ffn_w_down=v_ffn_w_down, w_in=v_w_in, conv_w=v_conv_w, w_conv_out=v_w_conv_out,
              w_attn_out=v_w_attn_out, w_o=v_w_o, rel_bias=v_rel_bias, final_g=v_final_g)
    order = list(weights)
    deltas, new_m, new_v = [], [], []
    for name in order:
        d_, m_, v_ = _adamw(weights[name], grads[name], ms[name], vs[name], "adamw_" + name)
        deltas.append(d_)
        new_m.append(m_)
        new_v.append(v_)
    return (loss, grad_x, *[grads[n] for n in order], *deltas, *new_m, *new_v)
```

```python
import functools

import numpy as np
import jax
import jax.numpy as jnp
from jax import lax
from jax.experimental import pallas as pl
from jax.experimental.pallas import tpu as pltpu

F32 = jnp.float32
BF16 = jnp.bfloat16

N_DEV = 8
HEAD_DIM = 64
HEAD_SHIFT = 6
HEADS_PER_GROUP = 8
DILATION_GROUPS = ((128, 1), (512, 4), (2048, 16))
DILS = tuple(d for _, d in DILATION_GROUPS)
N_GROUPS = len(DILATION_GROUPS)
ATTN_OUT = HEADS_PER_GROUP * HEAD_DIM
QKV_W = N_GROUPS * ATTN_OUT
BLOCK = 128
NUM_BUCKETS = 32
MAX_DISTANCE = 2048
CONV_K = 3
EPS = 1e-6
NEG_INF = -1e30
SCALE = HEAD_DIM ** -0.5

ADAM_LR = 0.001
ADAM_B1 = 0.9
ADAM_B2 = 0.999
ADAM_EPS = 1e-08
ADAM_WD = 0.01
ADAM_STEP = 10

V7X_VMEM_LIMIT = 48 * 1024 * 1024
MESH = pl.DeviceIdType.MESH

NN = (((1,), (0,)), ((), ()))
NT = (((1,), (1,)), ((), ()))
TN = (((0,), (0,)), ((), ()))


def _pick(dim, cands):
    for c in cands:
        if dim % c == 0:
            return c
    return dim


def _pick_k(K, cap=2816):
    if K <= cap or K % 128:
        return K
    best = 128
    for m in range(1, K // 128 + 1):
        if (K // 128) % m == 0 and 128 * m <= cap:
            best = 128 * m
    return best


def _params(sem):
    return pltpu.CompilerParams(dimension_semantics=sem, vmem_limit_bytes=V7X_VMEM_LIMIT)


def _all_gather(x_shard, name):
    m_per, n = x_shard.shape

    def body(x_ref, out_ref, send_sems, recv_sems, local_sem):
        x, y, c = lax.axis_index("x"), lax.axis_index("y"), lax.axis_index("c")
        me, sibling = (x, y, c), (x, y, 1 - c)
        chips = [(1 - x, y), (x, 1 - y), (1 - x, 1 - y)]

        def rows(px, py, pc):
            return out_ref.at[pl.ds((4 * px + 2 * py + pc) * m_per, m_per), :]

        def copy(k, block, to, src=None):
            return pltpu.make_async_remote_copy(
                src_ref=rows(*block) if src is None else src, dst_ref=rows(*block),
                send_sem=send_sems.at[k], recv_sem=recv_sems.at[k], device_id=to, device_id_type=MESH)

        mine = pltpu.make_async_copy(x_ref, rows(*me), local_sem)
        mine.start()
        first = [copy(0, me, sibling, src=x_ref)]
        first += [copy(1 + j, me, (*chip, c), src=x_ref) for j, chip in enumerate(chips)]
        for cp in first:
            cp.start()
        passed = [copy(4 + j, (*chip, c), sibling) for j, chip in enumerate(chips)]
        for j, chip in enumerate(chips):
            copy(1 + j, (*chip, c), me).wait_recv()
            passed[j].start()
        copy(0, sibling, me).wait_recv()
        for j, chip in enumerate(chips):
            copy(4 + j, (*chip, 1 - c), me).wait_recv()
        for cp in first + passed:
            cp.wait_send()
        mine.wait()

    return pl.pallas_call(
        body, name=name,
        out_shape=jax.ShapeDtypeStruct((N_DEV * m_per, n), x_shard.dtype),
        in_specs=[pl.BlockSpec(memory_space=pltpu.VMEM)],
        out_specs=pl.BlockSpec(memory_space=pltpu.VMEM),
        scratch_shapes=[pltpu.SemaphoreType.DMA((7,)), pltpu.SemaphoreType.DMA((7,)), pltpu.SemaphoreType.DMA],
    )(x_shard)


def _offsets(piece_rows):
    offs, o = [], 0
    for n in piece_rows:
        offs.append(o)
        o += n
    return offs


HBM_SPEC = pl.BlockSpec(memory_space=pltpu.HBM)
SEM_SPEC = pl.BlockSpec(memory_space=pltpu.SEMAPHORE)
ANY_SPEC = pl.BlockSpec(memory_space=pl.ANY)
SPLIT_COPY_PARAMS = pltpu.CompilerParams(has_side_effects=pltpu.SideEffectType.DATAFLOW_SIDE_EFFECTING)


def _in_hbm(a):
    return pltpu.with_memory_space_constraint(a, pltpu.HBM)


def _dma_sems(n):
    return [pltpu.SemaphoreType.DMA(())] * n


def _whole(ref, send_sem, recv_sem, me):
    return pltpu.make_async_remote_copy(src_ref=ref, dst_ref=ref, send_sem=send_sem, recv_sem=recv_sem,
                                        device_id=me, device_id_type=MESH)


def _gather_start(packed, piece_rows, after, name):
    R, w = packed.shape
    offs = _offsets(piece_rows)
    P = len(piece_rows)
    assert offs[-1] + piece_rows[-1] == R

    def body(*refs):
        src_ref = refs[0]
        o = refs[P + 2:]
        send, recv = o[0:4], o[4:8]
        zones, token, local_sems = o[9:9 + P], o[9 + P], o[10 + P]
        x, y, c = lax.axis_index("x"), lax.axis_index("y"), lax.axis_index("c")
        targets = [(x, y, 1 - c), (1 - x, y, c), (x, 1 - y, c), (1 - x, 1 - y, c)]
        me = 4 * x + 2 * y + c

        def piece(p):
            return src_ref.at[pl.ds(offs[p], piece_rows[p]), :]

        def rows(p):
            return zones[p].at[pl.ds(me * piece_rows[p], piece_rows[p]), :]

        mine = [pltpu.make_async_copy(piece(p), rows(p), local_sems.at[p]) for p in range(P)]
        for cp in mine:
            cp.start()
        for cp in mine:
            cp.wait()
        for k, to in enumerate(targets):
            for p in range(P):
                pltpu.make_async_remote_copy(src_ref=piece(p), dst_ref=rows(p), send_sem=send[k], recv_sem=recv[k],
                                             device_id=to, device_id_type=MESH).start()
        token[...] = jnp.zeros_like(token)

    zones_in = [_in_hbm(lax.empty((N_DEV * n, w), packed.dtype)) for n in piece_rows]
    outs = pl.pallas_call(
        body, name=name,
        out_shape=(*_dma_sems(8), pltpu.HBM((R, w), packed.dtype),
                   *[pltpu.HBM((N_DEV * n, w), packed.dtype) for n in piece_rows],
                   jax.ShapeDtypeStruct((8, 128), F32)),
        in_specs=[HBM_SPEC] * (P + 1) + [ANY_SPEC],
        out_specs=[SEM_SPEC] * 8 + [HBM_SPEC] * (P + 1) + [pl.BlockSpec(memory_space=pltpu.VMEM)],
        input_output_aliases={0: 8, **{1 + p: 9 + p for p in range(P)}},
        scratch_shapes=[pltpu.SemaphoreType.DMA((P,))],
        compiler_params=SPLIT_COPY_PARAMS,
    )(_in_hbm(packed), *zones_in, after)
    return outs[0:8], outs[8], list(outs[9:9 + P]), outs[9 + P]


def _gather_forward(sems, packed, zones, piece_rows, after, name):
    P = len(piece_rows)

    def body(*refs):
        src_ref = refs[0]
        s = refs[1 + P:9 + P]
        o = refs[10 + P:]
        send, recv = s[0:4], s[4:8]
        send2, recv2, zones_o = o[0:3], o[3:6], o[7:7 + P]
        x, y, c = lax.axis_index("x"), lax.axis_index("y"), lax.axis_index("c")
        me = (x, y, c)
        chips = [(1 - x, y), (x, 1 - y), (1 - x, 1 - y)]
        for j, (px, py) in enumerate(chips):
            _whole(src_ref, send[1 + j], recv[1 + j], me).wait_recv()
            blk = 4 * px + 2 * py + c
            for p in range(P):
                r = zones_o[p].at[pl.ds(blk * piece_rows[p], piece_rows[p]), :]
                pltpu.make_async_remote_copy(src_ref=r, dst_ref=r, send_sem=send2[j], recv_sem=recv2[j],
                                             device_id=(x, y, 1 - c), device_id_type=MESH).start()
        _whole(src_ref, send[0], recv[0], me).wait_recv()
        for k in range(4):
            _whole(src_ref, send[k], recv[k], me).wait_send()

    outs = pl.pallas_call(
        body, name=name,
        out_shape=(*_dma_sems(6), pltpu.HBM(packed.shape, packed.dtype),
                   *[pltpu.HBM(z.shape, z.dtype) for z in zones]),
        in_specs=[HBM_SPEC] * (P + 1) + [SEM_SPEC] * 8 + [ANY_SPEC],
        out_specs=[SEM_SPEC] * 6 + [HBM_SPEC] * (P + 1),
        input_output_aliases={0: 6, **{1 + p: 7 + p for p in range(P)}},
        compiler_params=SPLIT_COPY_PARAMS,
    )(packed, *zones, *sems, after)
    return outs[0:6], outs[6], list(outs[7:7 + P])


def _gather_finish(sems2, packed, zones, after, name):
    P = len(zones)

    def body(*refs):
        src_ref = refs[0]
        s = refs[1 + P:7 + P]
        x, y, c = lax.axis_index("x"), lax.axis_index("y"), lax.axis_index("c")
        for j in range(3):
            _whole(src_ref, s[j], s[3 + j], (x, y, c)).wait_recv()
        for j in range(3):
            _whole(src_ref, s[j], s[3 + j], (x, y, c)).wait_send()

    outs = pl.pallas_call(
        body, name=name,
        out_shape=(pltpu.HBM(packed.shape, packed.dtype), *[pltpu.HBM(z.shape, z.dtype) for z in zones]),
        in_specs=[HBM_SPEC] * (P + 1) + [SEM_SPEC] * 6 + [ANY_SPEC],
        out_specs=[HBM_SPEC] * (P + 1),
        input_output_aliases={p: p for p in range(P + 1)},
        compiler_params=SPLIT_COPY_PARAMS,
    )(packed, *zones, *sems2, after)
    return list(outs[1:1 + P])


def _scatter_start(pieces, after, name):
    P = len(pieces)
    w = pieces[0].shape[1]
    piece_rows = [p.shape[0] // N_DEV for p in pieces]
    offs = _offsets(piece_rows)
    R = offs[-1] + piece_rows[-1]

    def body(*refs):
        o = refs[P + 2:]
        send, recv = o[0:7], o[7:14]
        srcs, dst_ref, token, local_sems = o[14:14 + P], o[14 + P], o[15 + P], o[16 + P]
        x, y, c = lax.axis_index("x"), lax.axis_index("y"), lax.axis_index("c")
        me = 4 * x + 2 * y + c

        def chunk(p, dev):
            return srcs[p].at[pl.ds(dev * piece_rows[p], piece_rows[p]), :]

        def slot(p, dev):
            return dst_ref.at[dev, pl.ds(offs[p], piece_rows[p]), :]

        mine = [pltpu.make_async_copy(chunk(p, me), slot(p, me), local_sems.at[p]) for p in range(P)]
        for cp in mine:
            cp.start()
        for cp in mine:
            cp.wait()
        for k in range(1, N_DEV):
            px = 1 - x if (k >> 2) & 1 else x
            py = 1 - y if (k >> 1) & 1 else y
            pc = 1 - c if k & 1 else c
            peer = 4 * px + 2 * py + pc
            for p in range(P):
                pltpu.make_async_remote_copy(
                    src_ref=chunk(p, peer), dst_ref=slot(p, me), send_sem=send[k - 1], recv_sem=recv[k - 1],
                    device_id=(px, py, pc), device_id_type=MESH).start()
        token[...] = jnp.zeros_like(token)

    dtype = pieces[0].dtype
    outs = pl.pallas_call(
        body, name=name,
        out_shape=(*_dma_sems(14), *[pltpu.HBM(p.shape, dtype) for p in pieces], pltpu.HBM((N_DEV, R, w), dtype),
                   jax.ShapeDtypeStruct((8, 128), F32)),
        in_specs=[HBM_SPEC] * (P + 1) + [ANY_SPEC],
        out_specs=[SEM_SPEC] * 14 + [HBM_SPEC] * (P + 1) + [pl.BlockSpec(memory_space=pltpu.VMEM)],
        input_output_aliases={p: 14 + p for p in range(P + 1)},
        scratch_shapes=[pltpu.SemaphoreType.DMA((P,))],
        compiler_params=SPLIT_COPY_PARAMS,
    )(*[_in_hbm(p) for p in pieces], _in_hbm(lax.empty((N_DEV, R, w), dtype)), after)
    return outs[0:14], list(outs[14:14 + P]), outs[14 + P], outs[15 + P]


def _scatter_finish(sems, pieces, recv, after, name):
    P = len(pieces)

    def body(*refs):
        dst_ref = refs[P]
        s = refs[P + 1:P + 15]
        x, y, c = lax.axis_index("x"), lax.axis_index("y"), lax.axis_index("c")
        for k in range(7):
            _whole(dst_ref.at[0], s[k], s[7 + k], (x, y, c)).wait_recv()
        for k in range(7):
            _whole(dst_ref.at[0], s[k], s[7 + k], (x, y, c)).wait_send()

    outs = pl.pallas_call(
        body, name=name,
        out_shape=(*[pltpu.HBM(p.shape, p.dtype) for p in pieces], pltpu.HBM(recv.shape, recv.dtype)),
        in_specs=[HBM_SPEC] * (P + 1) + [SEM_SPEC] * 14 + [ANY_SPEC],
        out_specs=[HBM_SPEC] * (P + 1),
        input_output_aliases={p: p for p in range(P + 1)},
        compiler_params=SPLIT_COPY_PARAMS,
    )(*pieces, recv, *sems, after)
    return outs[P]


def _sum_sources(parts, name):
    _, r, n = parts.shape
    tr = _pick(r, [256, 128, 64, 32, 16, 8])

    def kern(p_ref, o_ref):
        acc = p_ref[0].astype(F32)
        for k in range(1, N_DEV):
            acc = acc + p_ref[k].astype(F32)
        o_ref[...] = acc

    return pl.pallas_call(
        kern, name=name, grid=(r // tr,),
        out_shape=jax.ShapeDtypeStruct((r, n), F32),
        in_specs=[pl.BlockSpec((N_DEV, tr, n), lambda i: (0, i, 0))],
        out_specs=pl.BlockSpec((tr, n), lambda i: (i, 0)),
        compiler_params=_params(("parallel",)),
    )(parts)


def _matmul(a, b, mode, out_dtype, name, tm=None, tn=None, tk=None, resid=None):
    if mode == "nn":
        (M, K), N = a.shape, b.shape[1]
    elif mode == "nt":
        (M, K), N = a.shape, b.shape[0]
    else:
        (K, M), N = a.shape, b.shape[1]
    dims = {"nn": NN, "nt": NT, "tn": TN}[mode]
    tm = tm or _pick(M, [1024, 1408, 512, 256, 128])
    tn = tn or _pick(N, [1024, 1408, 512, 256, 128])
    tk = tk or _pick_k(K)
    nk = K // tk
    a_spec = {"nn": pl.BlockSpec((tm, tk), lambda i, j, k: (i, k)),
              "nt": pl.BlockSpec((tm, tk), lambda i, j, k: (i, k)),
              "tn": pl.BlockSpec((tk, tm), lambda i, j, k: (k, i))}[mode]
    b_spec = {"nn": pl.BlockSpec((tk, tn), lambda i, j, k: (k, j)),
              "nt": pl.BlockSpec((tn, tk), lambda i, j, k: (j, k)),
              "tn": pl.BlockSpec((tk, tn), lambda i, j, k: (k, j))}[mode]
    o_spec = pl.BlockSpec((tm, tn), lambda i, j, k: (i, j))
    n_in = 2 if resid is None else 4
    n_out = 1 if resid is None else 2

    def kern(*refs):
        a_ref, b_ref = refs[0], refs[1]
        outs = refs[n_in:n_in + n_out]
        acc_ref = refs[n_in + n_out] if nk > 1 else None

        def finish(acc):
            if resid is None:
                outs[0][...] = acc.astype(out_dtype)
            else:
                x_ref, g_ref = refs[2], refs[3]
                outs[0][...] = x_ref[...] + (resid[2] * g_ref[...]) * acc
                outs[1][...] = acc.astype(out_dtype)

        part = lax.dot_general(a_ref[...], b_ref[...], dims, preferred_element_type=F32)
        if nk == 1:
            finish(part)
        else:
            k = pl.program_id(2)

            @pl.when(k == 0)
            def _():
                acc_ref[...] = part

            @pl.when(k > 0)
            def _():
                acc_ref[...] += part

            @pl.when(k == nk - 1)
            def _():
                finish(acc_ref[...])

    in_specs = [a_spec, b_spec]
    args = [a, b]
    out_shape = [jax.ShapeDtypeStruct((M, N), out_dtype)]
    out_specs = [o_spec]
    if resid is not None:
        in_specs += [o_spec, pl.BlockSpec((1, tn), lambda i, j, k: (0, j))]
        args += [resid[0], resid[1]]
        out_shape = [jax.ShapeDtypeStruct((M, N), F32)] + out_shape
        out_specs = [o_spec, o_spec]
    res = pl.pallas_call(
        kern, name=name, grid=(M // tm, N // tn, nk),
        out_shape=out_shape, in_specs=in_specs, out_specs=out_specs,
        scratch_shapes=[pltpu.VMEM((tm, tn), F32)] if nk > 1 else [],
        compiler_params=_params(("parallel", "parallel", "arbitrary")),
    )(*args)
    return res[0] if resid is None else res


def _dot3(a, b, dims):
    ah = a.astype(BF16)
    al = (a - ah.astype(F32)).astype(BF16)
    bh = b.astype(BF16)
    bl = (b - bh.astype(F32)).astype(BF16)
    d = functools.partial(lax.dot_general, dimension_numbers=dims, preferred_element_type=F32)
    return d(ah, bh) + (d(ah, bl) + d(al, bh))


def _silu_parts(a):
    sg = jax.nn.sigmoid(a)
    return a * sg, sg * (1.0 + a * (1.0 - sg))


def _ffn_up(h, wg_t, wu_t, name):
    S, D = h.shape
    F = wg_t.shape[0]
    tm = _pick(S, [512, 256, 128])
    tn = _pick(F, [1408, 512, 256, 128])

    def kern(h_ref, g_ref, u_ref, a_out, u_out, z_out):
        hv = h_ref[...]
        a = lax.dot_general(hv, g_ref[...], NT, preferred_element_type=F32)
        u = lax.dot_general(hv, u_ref[...], NT, preferred_element_type=F32)
        a_out[...] = a.astype(BF16)
        u_out[...] = u.astype(BF16)
        z_out[...] = (_silu_parts(a)[0] * u).astype(BF16)

    w_spec = pl.BlockSpec((tn, D), lambda j, i: (j, 0))
    o_spec = pl.BlockSpec((tm, tn), lambda j, i: (i, j))
    return pl.pallas_call(
        kern, name=name, grid=(F // tn, S // tm),
        out_shape=[jax.ShapeDtypeStruct((S, F), BF16)] * 3,
        in_specs=[pl.BlockSpec((tm, D), lambda j, i: (i, 0)), w_spec, w_spec],
        out_specs=[o_spec] * 3,
        compiler_params=_params(("parallel", "parallel")),
    )(h, wg_t, wu_t)


def _ffn_up_bwd(dz, a, u, wg_t, wu_t, name):
    S, F = dz.shape
    D = wg_t.shape[1]
    tm = _pick(S, [512, 256, 128])
    tk = _pick(F, [1408, 512, 256, 128])
    nk = F // tk

    def kern(dz_ref, a_ref, u_ref, g_ref, w_ref, da_out, du_out, dh_out, acc_ref):
        k = pl.program_id(1)
        av = a_ref[...].astype(F32)
        uv = u_ref[...].astype(F32)
        dzv = dz_ref[...].astype(F32)
        silu, dsilu = _silu_parts(av)
        da = (dzv * uv * dsilu).astype(BF16)
        du = (dzv * silu).astype(BF16)
        da_out[...] = da
        du_out[...] = du
        part = (lax.dot_general(da, g_ref[...], NN, preferred_element_type=F32)
                + lax.dot_general(du, w_ref[...], NN, preferred_element_type=F32))

        @pl.when(k == 0)
        def _():
            acc_ref[...] = part

        @pl.when(k > 0)
        def _():
            acc_ref[...] += part

        @pl.when(k == nk - 1)
        def _():
            dh_out[...] = acc_ref[...]

    t_spec = pl.BlockSpec((tm, tk), lambda i, k: (i, k))
    w_spec = pl.BlockSpec((tk, D), lambda i, k: (k, 0))
    return pl.pallas_call(
        kern, name=name, grid=(S // tm, nk),
        out_shape=[jax.ShapeDtypeStruct((S, F), BF16)] * 2 + [jax.ShapeDtypeStruct((S, D), F32)],
        in_specs=[t_spec, t_spec, t_spec, w_spec, w_spec],
        out_specs=[t_spec, t_spec, pl.BlockSpec((tm, D), lambda i, k: (i, 0))],
        scratch_shapes=[pltpu.VMEM((tm, D), F32)],
        compiler_params=_params(("parallel", "arbitrary")),
    )(dz, a, u, wg_t, wu_t)


def _attn_dh(dq, dkv, w_t, name):
    S = dq.shape[0]
    D = w_t.shape[1]
    tm = _pick(S, [1024, 512, 256, 128])

    def kern(dq_ref, dk_ref, dv_ref, wq_ref, wk_ref, wv_ref, o_ref):
        o_ref[...] = (lax.dot_general(dq_ref[...], wq_ref[...], NN, preferred_element_type=F32)
                      + lax.dot_general(dk_ref[...], wk_ref[...], NN, preferred_element_type=F32)
                      + lax.dot_general(dv_ref[...], wv_ref[...], NN, preferred_element_type=F32))

    def w_blk(j):
        return pl.BlockSpec((ATTN_OUT, D), lambda i: (j, 0))

    return pl.pallas_call(
        kern, name=name, grid=(S // tm,),
        out_shape=jax.ShapeDtypeStruct((S, D), F32),
        in_specs=[pl.BlockSpec((tm, ATTN_OUT), lambda i: (i, 0)), pl.BlockSpec((tm, ATTN_OUT), lambda i: (i, 0)),
                  pl.BlockSpec((tm, ATTN_OUT), lambda i: (i, 1)), w_blk(0), w_blk(1), w_blk(2)],
        out_specs=pl.BlockSpec((tm, D), lambda i: (i, 0)),
        compiler_params=_params(("parallel",)),
    )(dq, dkv, dkv, w_t, w_t, w_t)


def _row_spec(tm, d):
    return pl.BlockSpec((tm, d), lambda i: (i, 0))


def _vec_spec(d, rows=1):
    return pl.BlockSpec((rows, d), lambda i: (0, 0))


def _perm_spec(dil, tm, w):
    return pl.BlockSpec((dil, tm // dil, w), lambda i: (0, i, 0))


def _stage_shape(tm, w):
    return pltpu.VMEM((w // 128, tm, 128), F32)


def _stage(scr, val):
    for ci in range(scr.shape[0]):
        scr[ci] = val[:, 128 * ci:128 * (ci + 1)]


def _unstage(scr):
    return jnp.concatenate([scr[ci] for ci in range(scr.shape[0])], axis=1)


def _get_residue(scr, res, dil):
    n = scr.shape[1] // dil
    return jnp.concatenate([scr[ci, pl.ds(res, n, stride=dil), :] for ci in range(scr.shape[0])], axis=1)


def _put_residue(scr, res, dil, val):
    n = scr.shape[1] // dil
    for ci in range(scr.shape[0]):
        scr[ci, pl.ds(res, n, stride=dil), :] = val[:, 128 * ci:128 * (ci + 1)]


def _norm_mod_fwd(x, g, s, b, name, dils=()):
    S, D = x.shape
    tm = _pick(S, [256, 128])

    def kern(x_ref, g_ref, s_ref, b_ref, h_ref, *rest):
        xv = x_ref[...]
        r = lax.rsqrt(jnp.mean(xv * xv, axis=1, keepdims=True) + EPS)
        hv = xv * r * g_ref[...] * (1.0 + s_ref[...]) + b_ref[...]
        h_ref[...] = hv.astype(BF16)
        if dils:
            scr = rest[len(dils)]
            _stage(scr, hv)
            for dil, p_ref in zip(dils, rest[:len(dils)]):
                for res in range(dil):
                    p_ref[res] = _get_residue(scr, res, dil).astype(BF16)

    return pl.pallas_call(
        kern, name=name, grid=(S // tm,),
        out_shape=[jax.ShapeDtypeStruct((S, D), BF16)] + [jax.ShapeDtypeStruct((dil, S // dil, D), BF16) for dil in dils],
        in_specs=[_row_spec(tm, D), _vec_spec(D), _vec_spec(D), _vec_spec(D)],
        out_specs=[_row_spec(tm, D)] + [_perm_spec(dil, tm, D) for dil in dils],
        scratch_shapes=[_stage_shape(tm, D)] if dils else [],
        compiler_params=_params(("parallel",)),
    )(x, g, s, b)


def _norm_mod_bwd(x, dh_nat, dh_perm, dxo, g, s, name):
    S, D = x.shape
    tm = _pick(S, [256, 128])
    n = S // tm
    n_nat, n_perm = len(dh_nat), len(dh_perm)

    def kern(*refs):
        x_ref = refs[0]
        nat = refs[1:1 + n_nat]
        perm = refs[1 + n_nat:1 + n_nat + n_perm]
        dxo_ref, g_ref, s_ref, dx_ref, cs_ref = refs[1 + n_nat + n_perm:6 + n_nat + n_perm]
        scr = refs[6 + n_nat + n_perm:]
        i = pl.program_id(0)
        xv = x_ref[...]
        r = lax.rsqrt(jnp.mean(xv * xv, axis=1, keepdims=True) + EPS)
        xn = xv * r
        dh_v = nat[0][...].astype(F32)
        for t in nat[1:]:
            dh_v = dh_v + t[...].astype(F32)
        for (dil, _), p_ref, sc in zip(dh_perm, perm, scr):
            for res in range(dil):
                _put_residue(sc, res, dil, p_ref[res])
            dh_v = dh_v + _unstage(sc)
        one_s = 1.0 + s_ref[...]
        dxn = dh_v * (g_ref[...] * one_s)
        dx_ref[...] = dxo_ref[...] + r * (dxn - xn * jnp.mean(xn * dxn, axis=1, keepdims=True))

        @pl.when(i == 0)
        def _():
            cs_ref[...] = jnp.zeros_like(cs_ref)

        cs_ref[0:1, :] += jnp.sum(dh_v, axis=0, keepdims=True)
        cs_ref[1:2, :] += jnp.sum(dh_v * xn, axis=0, keepdims=True)

        @pl.when(i == n - 1)
        def _():
            t = cs_ref[1:2, :]
            cs_ref[2:3, :] = g_ref[...] * t
            cs_ref[3:4, :] = one_s * t

    return pl.pallas_call(
        kern, name=name, grid=(n,),
        out_shape=[jax.ShapeDtypeStruct((S, D), F32), jax.ShapeDtypeStruct((8, D), F32)],
        in_specs=[_row_spec(tm, D)] + [_row_spec(tm, D)] * n_nat + [_perm_spec(dil, tm, D) for dil, _ in dh_perm]
        + [_row_spec(tm, D), _vec_spec(D), _vec_spec(D)],
        out_specs=[_row_spec(tm, D), _vec_spec(D, 8)],
        scratch_shapes=[_stage_shape(tm, D) for _ in dh_perm],
        compiler_params=_params(("arbitrary",)),
    )(x, *dh_nat, *[a for _, a in dh_perm], dxo, g, s)


def _gate_bwd(dxo, f, gate, coef, name):
    S, D = dxo.shape
    tm = _pick(S, [512, 256, 128])

    def kern(dxo_ref, f_ref, gate_ref, df_ref, cs_ref):
        i = pl.program_id(0)
        dv = dxo_ref[...]
        df_ref[...] = ((coef * gate_ref[...]) * dv).astype(BF16)

        @pl.when(i == 0)
        def _():
            cs_ref[...] = jnp.zeros_like(cs_ref)

        cs_ref[0:1, :] += coef * jnp.sum(f_ref[...].astype(F32) * dv, axis=0, keepdims=True)

    return pl.pallas_call(
        kern, name=name, grid=(S // tm,),
        out_shape=[jax.ShapeDtypeStruct((S, D), BF16), jax.ShapeDtypeStruct((8, D), F32)],
        in_specs=[_row_spec(tm, D), _row_spec(tm, D), _vec_spec(D)],
        out_specs=[_row_spec(tm, D), _vec_spec(D, 8)],
        compiler_params=_params(("arbitrary",)),
    )(dxo, f, gate)


def _loss_head(x, g, target, name):
    S, D = x.shape
    tm = _pick(S, [256, 128])
    n = S // tm

    def kern(x_ref, g_ref, t_ref, dx_ref, cs_ref):
        i = pl.program_id(0)
        xv = x_ref[...]
        r = lax.rsqrt(jnp.mean(xv * xv, axis=1, keepdims=True) + EPS)
        xn = xv * r
        e = xn * g_ref[...] - t_ref[...]
        dxn = (e * (1.0 / D)) * g_ref[...]
        dx_ref[...] = r * (dxn - xn * jnp.mean(xn * dxn, axis=1, keepdims=True))

        @pl.when(i == 0)
        def _():
            cs_ref[...] = jnp.zeros_like(cs_ref)

        cs_ref[0:1, :] += jnp.sum(xn * e, axis=0, keepdims=True) * (1.0 / D)
        cs_ref[1:2, :] += jnp.sum(e * e, axis=0, keepdims=True)

        @pl.when(i == n - 1)
        def _():
            tot = jnp.sum(cs_ref[1:2, :], axis=1, keepdims=True) * (0.5 / D)
            cs_ref[2:3, :] = jnp.broadcast_to(tot, (1, D))

    return pl.pallas_call(
        kern, name=name, grid=(n,),
        out_shape=[jax.ShapeDtypeStruct((S, D), F32), jax.ShapeDtypeStruct((8, D), F32)],
        in_specs=[_row_spec(tm, D), _vec_spec(D), _row_spec(tm, D)],
        out_specs=[_row_spec(tm, D), _vec_spec(D, 8)],
        compiler_params=_params(("arbitrary",)),
    )(x, g, target)


def _shift_down(p, row, prev_rows):
    a, b = prev_rows
    p1 = jnp.where(row == 0, b, pltpu.roll(p, 1, 0))
    p2 = jnp.where(row == 0, a, jnp.where(row == 1, b, pltpu.roll(p, 2, 0)))
    return p1, p2


def _conv_fwd(cg, conv_w, name):
    S, D5 = cg.shape
    D = D5 // 5
    tm = _pick(S, [256, 128])
    t8 = tm // 8

    def prev(col):
        return pl.BlockSpec((8, D), lambda i: (jnp.maximum(i * t8 - 1, 0), col))

    def kern(cb_ref, cc_ref, ch_ref, ccp_ref, chp_ref, w_ref, y_ref):
        i = pl.program_id(0)
        keep = jnp.where(i > 0, 1.0, 0.0)
        p = cc_ref[...].astype(F32) * ch_ref[...].astype(F32)
        pa = ccp_ref[6:7, :].astype(F32) * chp_ref[6:7, :].astype(F32) * keep
        pb = ccp_ref[7:8, :].astype(F32) * chp_ref[7:8, :].astype(F32) * keep
        row = lax.broadcasted_iota(jnp.int32, (tm, D), 0)
        p1, p2 = _shift_down(p, row, (pa, pb))
        dw = w_ref[0:1, :] * p2 + w_ref[1:2, :] * p1 + w_ref[2:3, :] * p
        y_ref[...] = (cb_ref[...].astype(F32) * dw).astype(BF16)

    def col(cidx):
        return pl.BlockSpec((tm, D), lambda i: (i, cidx))

    return pl.pallas_call(
        kern, name=name, grid=(S // tm,),
        out_shape=jax.ShapeDtypeStruct((S, D), BF16),
        in_specs=[col(0), col(1), col(2), prev(1), prev(2), _vec_spec(D, CONV_K)],
        out_specs=_row_spec(tm, D),
        compiler_params=_params(("parallel",)),
    )(cg, cg, cg, cg, cg, conv_w)


def _conv_bwd(cg, dy, dgg, conv_w, name):
    S, D5 = cg.shape
    D = D5 // 5
    tm = _pick(S, [256, 128])
    t8 = tm // 8
    n = S // tm
    last8 = S // 8 - 1

    def prev(col):
        return pl.BlockSpec((8, D), lambda i: (jnp.maximum(i * t8 - 1, 0), col))

    def nxt(col):
        return pl.BlockSpec((8, D), lambda i: (jnp.minimum((i + 1) * t8, last8), col))

    def kern(cb_ref, cc_ref, ch_ref, dy_ref, dgg_ref, ccp_ref, chp_ref, cbn_ref, dyn_ref, w_ref, d_ref, cs_ref):
        i = pl.program_id(0)
        keep_p = jnp.where(i > 0, 1.0, 0.0)
        keep_n = jnp.where(i < n - 1, 1.0, 0.0)
        cb = cb_ref[...].astype(F32)
        cc = cc_ref[...].astype(F32)
        ch = ch_ref[...].astype(F32)
        dyv = dy_ref[...].astype(F32)
        p = cc * ch
        pa = ccp_ref[6:7, :].astype(F32) * chp_ref[6:7, :].astype(F32) * keep_p
        pb = ccp_ref[7:8, :].astype(F32) * chp_ref[7:8, :].astype(F32) * keep_p
        row = lax.broadcasted_iota(jnp.int32, (tm, D), 0)
        p1, p2 = _shift_down(p, row, (pa, pb))
        w0, w1, w2 = w_ref[0:1, :], w_ref[1:2, :], w_ref[2:3, :]
        dw = w0 * p2 + w1 * p1 + w2 * p
        ddw = dyv * cb
        na = dyn_ref[0:1, :].astype(F32) * cbn_ref[0:1, :].astype(F32) * keep_n
        nb = dyn_ref[1:2, :].astype(F32) * cbn_ref[1:2, :].astype(F32) * keep_n
        u1 = jnp.where(row == tm - 1, na, pltpu.roll(ddw, tm - 1, 0))
        u2 = jnp.where(row == tm - 2, na, jnp.where(row == tm - 1, nb, pltpu.roll(ddw, tm - 2, 0)))
        dp = w2 * ddw + w1 * u1 + w0 * u2
        d_ref[:, 0:D] = (dyv * dw).astype(BF16)
        d_ref[:, D:2 * D] = (dp * ch).astype(BF16)
        d_ref[:, 2 * D:3 * D] = (dp * cc).astype(BF16)
        d_ref[:, 3 * D:5 * D] = dgg_ref[...]

        @pl.when(i == 0)
        def _():
            cs_ref[...] = jnp.zeros_like(cs_ref)

        cs_ref[0:1, :] += jnp.sum(ddw * p2, axis=0, keepdims=True)
        cs_ref[1:2, :] += jnp.sum(ddw * p1, axis=0, keepdims=True)
        cs_ref[2:3, :] += jnp.sum(ddw * p, axis=0, keepdims=True)

    def col(cidx):
        return pl.BlockSpec((tm, D), lambda i: (i, cidx))

    return pl.pallas_call(
        kern, name=name, grid=(n,),
        out_shape=[jax.ShapeDtypeStruct((S, 5 * D), BF16), jax.ShapeDtypeStruct((8, D), F32)],
        in_specs=[col(0), col(1), col(2), _row_spec(tm, D), _row_spec(tm, 2 * D), prev(1), prev(2), nxt(0),
                  pl.BlockSpec((8, D), lambda i: (jnp.minimum((i + 1) * t8, last8), 0)), _vec_spec(D, CONV_K)],
        out_specs=[_row_spec(tm, 5 * D), _vec_spec(D, 8)],
        compiler_params=_params(("arbitrary",)),
    )(cg, cg, cg, dy, dgg, cg, cg, cg, dy, conv_w)


def _merge_fwd(cg, yc, ya, name):
    S, D = yc.shape
    tm = _pick(S, [512, 256, 128])

    def kern(gc_ref, ga_ref, yc_ref, ya_ref, m_ref):
        m_ref[...] = (jax.nn.sigmoid(gc_ref[...].astype(F32)) * yc_ref[...].astype(F32)
                      + jax.nn.sigmoid(ga_ref[...].astype(F32)) * ya_ref[...].astype(F32)).astype(BF16)

    return pl.pallas_call(
        kern, name=name, grid=(S // tm,),
        out_shape=jax.ShapeDtypeStruct((S, D), BF16),
        in_specs=[pl.BlockSpec((tm, D), lambda i: (i, 3)), pl.BlockSpec((tm, D), lambda i: (i, 4)),
                  _row_spec(tm, D), _row_spec(tm, D)],
        out_specs=_row_spec(tm, D),
        compiler_params=_params(("parallel",)),
    )(cg, cg, yc, ya)


def _merge_bwd(cg, yc, ya, dm, name):
    S, D = yc.shape
    tm = _pick(S, [256, 128])

    def kern(gc_ref, ga_ref, yc_ref, ya_ref, dm_ref, dyc_ref, dya_ref, dg_ref):
        dmv = dm_ref[...].astype(F32)
        sc = jax.nn.sigmoid(gc_ref[...].astype(F32))
        sa = jax.nn.sigmoid(ga_ref[...].astype(F32))
        dyc_ref[...] = (dmv * sc).astype(BF16)
        dya_ref[...] = (dmv * sa).astype(BF16)
        dg_ref[:, 0:D] = (dmv * yc_ref[...].astype(F32) * (sc * (1.0 - sc))).astype(BF16)
        dg_ref[:, D:2 * D] = (dmv * ya_ref[...].astype(F32) * (sa * (1.0 - sa))).astype(BF16)

    return pl.pallas_call(
        kern, name=name, grid=(S // tm,),
        out_shape=[jax.ShapeDtypeStruct((S, D), BF16), jax.ShapeDtypeStruct((S, D), BF16),
                   jax.ShapeDtypeStruct((S, 2 * D), BF16)],
        in_specs=[pl.BlockSpec((tm, D), lambda i: (i, 3)), pl.BlockSpec((tm, D), lambda i: (i, 4)),
                  _row_spec(tm, D), _row_spec(tm, D), _row_spec(tm, D)],
        out_specs=[_row_spec(tm, D), _row_spec(tm, D), pl.BlockSpec((tm, 2 * D), lambda i: (i, 0))],
        compiler_params=_params(("parallel",)),
    )(cg, cg, yc, ya, dm)


def _t5_bucket(dist):
    exact = NUM_BUCKETS // 2
    d = np.maximum(dist, 1).astype(np.float32)
    large = exact + (np.log(d / exact) / np.log(MAX_DISTANCE / exact) * (NUM_BUCKETS - exact)).astype(np.int32)
    large = np.minimum(large, NUM_BUCKETS - 1)
    return np.where(dist < exact, dist, large).astype(np.int32)


def _bucket_tables():
    i = np.arange(BLOCK)[:, None]
    j = np.arange(2 * BLOCK)[None, :]
    rel = i - j + BLOCK
    return np.stack([_t5_bucket(np.maximum(rel, 0) * d) for _, d in DILATION_GROUPS]).astype(np.int32)


def _band_masks():
    i = lax.broadcasted_iota(jnp.int32, (BLOCK, 2 * BLOCK), 0)
    j = lax.broadcasted_iota(jnp.int32, (BLOCK, 2 * BLOCK), 1)
    rel = i - j + BLOCK
    band = (rel >= 0) & (rel <= BLOCK)
    return band, band & (j >= BLOCK)


def _bias_build(rel_bias, buckets, name):
    def kern(rb_ref, bk_ref, o_ref):
        g = pl.program_id(0)
        bk = bk_ref[0]
        band, first = _band_masks()
        for h in range(HEADS_PER_GROUP):
            acc = jnp.zeros((BLOCK, 2 * BLOCK), F32)
            for b in range(NUM_BUCKETS):
                acc = jnp.where(bk == b, rb_ref[b, g * HEADS_PER_GROUP + h], acc)
            o_ref[0, 0, h] = jnp.where(first, acc, NEG_INF)
            o_ref[0, 1, h] = jnp.where(band, acc, NEG_INF)

    return pl.pallas_call(
        kern, name=name, grid=(N_GROUPS,),
        out_shape=jax.ShapeDtypeStruct((N_GROUPS, 2, HEADS_PER_GROUP, BLOCK, 2 * BLOCK), F32),
        in_specs=[pl.BlockSpec(memory_space=pltpu.SMEM),
                  pl.BlockSpec((1, BLOCK, 2 * BLOCK), lambda g: (g, 0, 0))],
        out_specs=pl.BlockSpec((1, 2, HEADS_PER_GROUP, BLOCK, 2 * BLOCK), lambda g: (g, 0, 0, 0, 0)),
        compiler_params=_params(("parallel",)),
    )(rel_bias, buckets)


def _bias_bwd(dlog, buckets, name):
    def kern(dl_ref, bk_ref, o_ref):
        g = pl.program_id(0)
        bk = bk_ref[0]
        rowi = lax.broadcasted_iota(jnp.int32, (NUM_BUCKETS, 128), 0)
        coli = lax.broadcasted_iota(jnp.int32, (NUM_BUCKETS, 128), 1)

        @pl.when(g == 0)
        def _():
            o_ref[...] = jnp.zeros_like(o_ref)

        acc = jnp.zeros((NUM_BUCKETS, 128), F32)
        for h in range(HEADS_PER_GROUP):
            dv = dl_ref[0, h]
            for b in range(NUM_BUCKETS):
                t = jnp.sum(jnp.where(bk == b, dv, 0.0), axis=0, keepdims=True)
                t = jnp.sum(t, axis=1, keepdims=True)
                acc = acc + jnp.where((rowi == b) & (coli == g * HEADS_PER_GROUP + h), t, 0.0)
        o_ref[...] += acc

    return pl.pallas_call(
        kern, name=name, grid=(N_GROUPS,),
        out_shape=jax.ShapeDtypeStruct((NUM_BUCKETS, 128), F32),
        in_specs=[pl.BlockSpec((1, HEADS_PER_GROUP, BLOCK, 2 * BLOCK), lambda g: (g, 0, 0, 0)),
                  pl.BlockSpec((1, BLOCK, 2 * BLOCK), lambda g: (g, 0, 0))],
        out_specs=pl.BlockSpec((NUM_BUCKETS, 128), lambda g: (0, 0)),
        compiler_params=_params(("arbitrary",)),
    )(dlog, buckets)


def _head_masks():
    lane = lax.broadcasted_iota(jnp.int32, (BLOCK, 128), 1)
    lo = lane < HEAD_DIM
    return lo, jnp.logical_not(lo)


def _attn_fwd(qkv, bias, d, name):
    S = qkv.shape[0]
    nb = S // d // BLOCK

    def kern(q_ref, kp_ref, kc_ref, vp_ref, vc_ref, b_ref, o_ref, lse_ref):
        lo, hi = _head_masks()
        for p in range(HEADS_PER_GROUP // 2):
            sl = slice(128 * p, 128 * (p + 1))
            q = q_ref[:, sl]
            k = jnp.concatenate([kp_ref[:, sl], kc_ref[:, sl]], axis=0)
            v = jnp.concatenate([vp_ref[:, sl], vc_ref[:, sl]], axis=0)
            o2, l2 = [], []
            for hh, msk in enumerate((lo, hi)):
                qm = jnp.where(msk, q, jnp.zeros_like(q))
                s = lax.dot_general(qm, k, NT, preferred_element_type=F32) * SCALE + b_ref[0, 2 * p + hh]
                m = jnp.max(s, axis=1, keepdims=True)
                e = jnp.exp(s - m)
                l = jnp.sum(e, axis=1, keepdims=True)
                o2.append(lax.dot_general(e.astype(BF16), v, NN, preferred_element_type=F32) / l)
                l2.append(jnp.broadcast_to(m + jnp.log(l), (BLOCK, 128)))
            o_ref[:, sl] = jnp.where(lo, o2[0], o2[1])
            lse_ref[:, sl] = jnp.where(lo, l2[0], l2[1])

    def blk(col, prev):
        if prev:
            return pl.BlockSpec((BLOCK, ATTN_OUT), lambda r, n: (r * nb + jnp.maximum(n - 1, 0), col))
        return pl.BlockSpec((BLOCK, ATTN_OUT), lambda r, n: (r * nb + n, col))

    o_spec = pl.BlockSpec((BLOCK, ATTN_OUT), lambda r, n: (r * nb + n, 0))
    return pl.pallas_call(
        kern, name=name, grid=(d, nb),
        out_shape=[jax.ShapeDtypeStruct((S, ATTN_OUT), F32)] * 2,
        in_specs=[blk(0, False), blk(1, True), blk(1, False), blk(2, True), blk(2, False),
                  pl.BlockSpec((1, HEADS_PER_GROUP, BLOCK, 2 * BLOCK), lambda r, n: (jnp.minimum(n, 1), 0, 0, 0))],
        out_specs=[o_spec, o_spec],
        compiler_params=_params(("parallel", "arbitrary")),
    )(qkv, qkv, qkv, qkv, qkv, bias)


def _attn_bwd(qkv, do, lse, delta, bias, d, name):
    S = qkv.shape[0]
    nb = S // d // BLOCK
    low = -3.0e38

    def kern(q_ref, kp_ref, kc_ref, vp_ref, vc_ref, do_ref, lse_ref, dl_ref, b_ref,
             dq_ref, dkv_ref, db_ref, ck_ref, cv_ref):
        r, n = pl.program_id(0), pl.program_id(1)

        @pl.when((r == 0) & (n == 0))
        def _():
            db_ref[...] = jnp.zeros_like(db_ref)

        @pl.when(n == 0)
        def _():
            ck_ref[...] = jnp.zeros_like(ck_ref)
            cv_ref[...] = jnp.zeros_like(cv_ref)

        @pl.when(n < nb)
        def _():
            lo, hi = _head_masks()
            for p in range(HEADS_PER_GROUP // 2):
                sl = slice(128 * p, 128 * (p + 1))
                sv = slice(ATTN_OUT + 128 * p, ATTN_OUT + 128 * (p + 1))
                q = q_ref[:, sl]
                k = jnp.concatenate([kp_ref[:, sl], kc_ref[:, sl]], axis=0)
                v = jnp.concatenate([vp_ref[:, sl], vc_ref[:, sl]], axis=0)
                dov = do_ref[:, sl]
                lse_b = lse_ref[:, sl]
                del_b = dl_ref[:, sl]
                dq2 = []
                dk_acc = jnp.zeros((2 * BLOCK, 128), F32)
                dv_acc = jnp.zeros((2 * BLOCK, 128), F32)
                for hh, msk in enumerate((lo, hi)):
                    qm = jnp.where(msk, q, jnp.zeros_like(q))
                    dom = jnp.where(msk, dov, jnp.zeros_like(dov))
                    lse_h = jnp.max(jnp.where(msk, lse_b, low), axis=1, keepdims=True)
                    del_h = jnp.max(jnp.where(msk, del_b, low), axis=1, keepdims=True)
                    s = lax.dot_general(qm, k, NT, preferred_element_type=F32) * SCALE + b_ref[0, 2 * p + hh]
                    pr = jnp.exp(s - lse_h)
                    dp = lax.dot_general(dom, v, NT, preferred_element_type=F32)
                    ds = pr * (dp - del_h)
                    db_ref[2 * p + hh] += ds
                    dsb = (ds * SCALE).astype(BF16)
                    dq2.append(lax.dot_general(dsb, k, NN, preferred_element_type=F32))
                    dk_acc = dk_acc + lax.dot_general(dsb, qm, TN, preferred_element_type=F32)
                    dv_acc = dv_acc + lax.dot_general(pr.astype(BF16), dom, TN, preferred_element_type=F32)
                dq_ref[:, sl] = jnp.where(lo, dq2[0], dq2[1]).astype(BF16)
                dkv_ref[:, sl] = (ck_ref[:, sl] + dk_acc[0:BLOCK]).astype(BF16)
                dkv_ref[:, sv] = (cv_ref[:, sl] + dv_acc[0:BLOCK]).astype(BF16)
                ck_ref[:, sl] = dk_acc[BLOCK:2 * BLOCK]
                cv_ref[:, sl] = dv_acc[BLOCK:2 * BLOCK]

        @pl.when(n == nb)
        def _():
            dkv_ref[:, 0:ATTN_OUT] = ck_ref[...].astype(BF16)
            dkv_ref[:, ATTN_OUT:2 * ATTN_OUT] = cv_ref[...].astype(BF16)

    def cur(n):
        return jnp.minimum(n, nb - 1)

    def blk(col, prev):
        if prev:
            return pl.BlockSpec((BLOCK, ATTN_OUT), lambda r, n: (r * nb + jnp.maximum(cur(n) - 1, 0), col))
        return pl.BlockSpec((BLOCK, ATTN_OUT), lambda r, n: (r * nb + cur(n), col))

    q_like = pl.BlockSpec((BLOCK, ATTN_OUT), lambda r, n: (r * nb + cur(n), 0))
    return pl.pallas_call(
        kern, name=name, grid=(d, nb + 1),
        out_shape=[jax.ShapeDtypeStruct((S, ATTN_OUT), BF16), jax.ShapeDtypeStruct((S, 2 * ATTN_OUT), BF16),
                   jax.ShapeDtypeStruct((HEADS_PER_GROUP, BLOCK, 2 * BLOCK), F32)],
        in_specs=[blk(0, False), blk(1, True), blk(1, False), blk(2, True), blk(2, False),
                  q_like, q_like, q_like,
                  pl.BlockSpec((1, HEADS_PER_GROUP, BLOCK, 2 * BLOCK),
                               lambda r, n: (jnp.minimum(cur(n), 1), 0, 0, 0))],
        out_specs=[q_like,
                   pl.BlockSpec((BLOCK, 2 * ATTN_OUT), lambda r, n: (r * nb + jnp.maximum(n - 1, 0), 0)),
                   pl.BlockSpec((HEADS_PER_GROUP, BLOCK, 2 * BLOCK), lambda r, n: (0, 0, 0))],
        scratch_shapes=[pltpu.VMEM((BLOCK, ATTN_OUT), F32), pltpu.VMEM((BLOCK, ATTN_OUT), F32)],
        compiler_params=_params(("arbitrary", "arbitrary")),
    )(qkv, qkv, qkv, qkv, qkv, do, lse, delta, bias)


def _by_residue(a, dil):
    return a if dil == 1 else a.reshape(dil, a.shape[0] // dil, a.shape[1])


def _flat(a):
    return a if a.ndim == 2 else a.reshape(a.shape[0] * a.shape[1], a.shape[2])


def _combine_fwd(os_, lses, name):
    S, W = os_[0].shape
    tm = _pick(S, [256, 128])
    perm = [dil for dil in DILS if dil > 1]

    def kern(*refs):
        o_in, l_in = refs[0:N_GROUPS], refs[N_GROUPS:2 * N_GROUPS]
        of_ref, ob_ref, lse_ref = refs[2 * N_GROUPS:2 * N_GROUPS + 3]
        lse_p = refs[2 * N_GROUPS + 3:2 * N_GROUPS + 3 + len(perm)]
        scr = refs[2 * N_GROUPS + 3 + len(perm):]
        ov, lv = [], []
        si = 0
        for g, dil in enumerate(DILS):
            if dil == 1:
                ov.append(o_in[g][...])
                lv.append(l_in[g][...])
            else:
                so, sl = scr[si], scr[si + 1]
                si += 2
                for res in range(dil):
                    _put_residue(so, res, dil, o_in[g][res])
                    _put_residue(sl, res, dil, l_in[g][res])
                ov.append(_unstage(so))
                lv.append(_unstage(sl))
        m = jnp.maximum(jnp.maximum(lv[0], lv[1]), lv[2])
        e = [jnp.exp(t - m) for t in lv]
        tot = e[0] + e[1] + e[2]
        o = (e[0] * ov[0] + e[1] * ov[1] + e[2] * ov[2]) / tot
        lse = m + jnp.log(tot)
        of_ref[...] = o
        ob_ref[...] = o.astype(BF16)
        lse_ref[...] = lse
        sl = scr[1]
        _stage(sl, lse)
        for dil, p_ref in zip(perm, lse_p):
            for res in range(dil):
                p_ref[res] = _get_residue(sl, res, dil)

    def in_spec(dil):
        return _row_spec(tm, W) if dil == 1 else _perm_spec(dil, tm, W)

    ins = [_by_residue(a, dil) for a, dil in zip(os_, DILS)] + [_by_residue(a, dil) for a, dil in zip(lses, DILS)]
    return pl.pallas_call(
        kern, name=name, grid=(S // tm,),
        out_shape=[jax.ShapeDtypeStruct((S, W), F32), jax.ShapeDtypeStruct((S, W), BF16),
                   jax.ShapeDtypeStruct((S, W), F32)]
        + [jax.ShapeDtypeStruct((dil, S // dil, W), F32) for dil in perm],
        in_specs=[in_spec(dil) for dil in DILS] * 2,
        out_specs=[_row_spec(tm, W)] * 3 + [_perm_spec(dil, tm, W) for dil in perm],
        scratch_shapes=[_stage_shape(tm, W) for _ in range(2 * len(perm))],
        compiler_params=_params(("parallel",)),
    )(*ins)


def _delta(do, o, name):
    S, W = o.shape
    tm = _pick(S, [256, 128])
    perm = [dil for dil in DILS if dil > 1]

    def kern(do_ref, o_ref, dob_ref, d_ref, *rest):
        scr, scr_do = rest[2 * len(perm)], rest[2 * len(perm) + 1]
        prod = do_ref[...] * o_ref[...]
        ri = jnp.right_shift(lax.broadcasted_iota(jnp.int32, (W, W), 0), HEAD_SHIFT)
        ci = jnp.right_shift(lax.broadcasted_iota(jnp.int32, (W, W), 1), HEAD_SHIFT)
        same = jnp.where(ri == ci, 1.0, 0.0).astype(BF16)
        hi_p = prod.astype(BF16)
        lo_p = (prod - hi_p.astype(F32)).astype(BF16)
        dl = (lax.dot_general(hi_p, same, NN, preferred_element_type=F32)
              + lax.dot_general(lo_p, same, NN, preferred_element_type=F32))
        d_ref[...] = dl
        dob_ref[...] = do_ref[...].astype(BF16)
        _stage(scr, dl)
        _stage(scr_do, do_ref[...])
        for j, dil in enumerate(perm):
            for res in range(dil):
                rest[2 * j][res] = _get_residue(scr_do, res, dil).astype(BF16)
                rest[2 * j + 1][res] = _get_residue(scr, res, dil)

    out_shape = [jax.ShapeDtypeStruct((S, W), BF16), jax.ShapeDtypeStruct((S, W), F32)]
    out_specs = [_row_spec(tm, W), _row_spec(tm, W)]
    for dil in perm:
        out_shape += [jax.ShapeDtypeStruct((dil, S // dil, W), BF16), jax.ShapeDtypeStruct((dil, S // dil, W), F32)]
        out_specs += [_perm_spec(dil, tm, W), _perm_spec(dil, tm, W)]
    return pl.pallas_call(
        kern, name=name, grid=(S // tm,),
        out_shape=out_shape,
        in_specs=[_row_spec(tm, W), _row_spec(tm, W)], out_specs=out_specs,
        scratch_shapes=[_stage_shape(tm, W), _stage_shape(tm, W)],
        compiler_params=_params(("parallel",)),
    )(do, o)


def _ada_fwd(c16, ada_w, name):
    depth, D, n = ada_w.shape
    rows = 2 * N_DEV

    def kern(c_ref, w_ref, o_ref, cs_ref):
        cv = c_ref[...]
        cs = cv * jax.nn.sigmoid(cv)
        cs_ref[...] = cs
        o_ref[0] = _dot3(cs, w_ref[0], NN)

    return pl.pallas_call(
        kern, name=name, grid=(depth,),
        out_shape=[jax.ShapeDtypeStruct((depth, rows, n), F32), jax.ShapeDtypeStruct((rows, D), F32)],
        in_specs=[pl.BlockSpec((rows, D), lambda l: (0, 0)), pl.BlockSpec((1, D, n), lambda l: (l, 0, 0))],
        out_specs=[pl.BlockSpec((1, rows, n), lambda l: (l, 0, 0)), pl.BlockSpec((rows, D), lambda l: (0, 0))],
        compiler_params=_params(("arbitrary",)),
    )(c16, ada_w)


def _ada_bwd(cs16, dm16, name):
    depth, _, n = dm16.shape
    D = cs16.shape[1]

    def kern(cs_ref, dm_ref, o_ref):
        o_ref[0] = _dot3(cs_ref[...], dm_ref[0], TN)

    return pl.pallas_call(
        kern, name=name, grid=(depth,),
        out_shape=jax.ShapeDtypeStruct((depth, D, n), F32),
        in_specs=[pl.BlockSpec((2 * N_DEV, D), lambda l: (0, 0)), pl.BlockSpec((1, 2 * N_DEV, n), lambda l: (l, 0, 0))],
        out_specs=pl.BlockSpec((1, D, n), lambda l: (l, 0, 0)),
        compiler_params=_params(("parallel",)),
    )(cs16, dm16)


def _sum_rows8(parts, name):
    _, r, n = parts.shape

    def kern(p_ref, o_ref):
        acc = p_ref[0]
        for k in range(1, N_DEV):
            acc = acc + p_ref[k]
        o_ref[...] = acc

    return pl.pallas_call(
        kern, name=name, out_shape=jax.ShapeDtypeStruct((r, n), F32),
        in_specs=[pl.BlockSpec(memory_space=pltpu.VMEM)], out_specs=pl.BlockSpec(memory_space=pltpu.VMEM),
    )(parts)


def _adamw(w, g, m, v, name):
    shape = w.shape
    c = shape[-1]
    r = int(np.prod(shape[:-1])) if len(shape) > 1 else 1
    w2, g2, m2, v2 = (t.reshape(r, c) for t in (w, g, m, v))
    tr = r
    for cand in (2048, 1024, 512, 256, 128, 64, 32, 16, 8):
        if r % cand == 0 and cand * c * 4 <= (1 << 20):
            tr = cand
            break
    c1 = 1.0 - ADAM_B1 ** ADAM_STEP
    c2 = 1.0 - ADAM_B2 ** ADAM_STEP

    def kern(w_ref, g_ref, m_ref, v_ref, d_ref, nm_ref, nv_ref):
        gv = g_ref[...]
        nm = ADAM_B1 * m_ref[...] + (1.0 - ADAM_B1) * gv
        nv = ADAM_B2 * v_ref[...] + (1.0 - ADAM_B2) * (gv * gv)
        nm_ref[...] = nm
        nv_ref[...] = nv
        d_ref[...] = -ADAM_LR * ((nm / c1) / (jnp.sqrt(nv / c2) + ADAM_EPS) + ADAM_WD * w_ref[...])

    spec = pl.BlockSpec((tr, c), lambda i: (i, 0))
    outs = pl.pallas_call(
        kern, name=name, grid=(r // tr,),
        out_shape=[jax.ShapeDtypeStruct((r, c), F32)] * 3,
        in_specs=[spec] * 4, out_specs=[spec] * 3,
        compiler_params=_params(("parallel",)),
    )(w2, g2, m2, v2)
    return tuple(o.reshape(shape) for o in outs)


def kernel(x, c, ada_w, ada_b, norm_g, ffn_w_gate, ffn_w_up, ffn_w_down, w_in, conv_w, w_conv_out, w_attn_out, w_o, rel_bias, final_g, loss_target, m_ada_w, m_ada_b, m_norm_g, m_ffn_w_gate, m_ffn_w_up, m_ffn_w_down, m_w_in, m_conv_w, m_w_conv_out, m_w_attn_out, m_w_o, m_rel_bias, m_final_g, v_ada_w, v_ada_b, v_norm_g, v_ffn_w_gate, v_ffn_w_up, v_ffn_w_down, v_w_in, v_conv_w, v_w_conv_out, v_w_attn_out, v_w_o, v_rel_bias, v_final_g):
    depth = ada_w.shape[0]
    S, D = x.shape[1], x.shape[2]
    me = 4 * lax.axis_index("x") + 2 * lax.axis_index("y") + lax.axis_index("c")
    x0 = x.reshape(S, D)
    target = loss_target.reshape(S, D)
    fsh = ffn_w_down.shape[2]
    insh = w_in.shape[2]
    dsh = D // N_DEV
    ao_rows = dsh * ATTN_OUT // D

    piece_rows = [fsh] * 6 + [insh, dsh, dsh, ao_rows]

    def pack(l):
        def t(a):
            return jnp.transpose(a).astype(BF16)
        ps = [t(ffn_w_gate[l, 0]), t(ffn_w_gate[l, 1]), t(ffn_w_up[l, 0]), t(ffn_w_up[l, 1]),
              ffn_w_down[l, 0].astype(BF16), ffn_w_down[l, 1].astype(BF16), t(w_in[l]),
              w_conv_out[l].astype(BF16), w_o[l].astype(BF16), t(w_attn_out[l]).reshape(ao_rows, D)]
        return jnp.concatenate(ps, axis=0)

    def unpack(full):
        in_t = full[6]
        qkv_t = [jnp.concatenate([in_t[t * QKV_W + g * ATTN_OUT: t * QKV_W + (g + 1) * ATTN_OUT] for t in range(3)])
                 for g in range(N_GROUPS)]
        ao_t = full[9].reshape(N_DEV, dsh, ATTN_OUT).reshape(D, ATTN_OUT)
        return dict(g_t=full[0:2], u_t=full[2:4], down=full[4:6], qkv_t=qkv_t, cg_t=in_t[3 * QKV_W:],
                    co=full[7], wo=full[8], ao_t=ao_t)

    def behind(v, token):
        return v + token[0, 0]

    gather = _gather_start(pack(0), piece_rows, c, "weights_gather_start_l0")

    c_all = _all_gather(c.reshape(D // 128, 128), "c_all_gather").reshape(N_DEV, D)
    c16 = jnp.concatenate([c_all, jnp.zeros_like(c_all)], axis=0)
    mod_part, cs16 = _ada_fwd(c16, ada_w, "ada_fwd")
    mod_part = mod_part[:, :N_DEV]
    n_ada = ada_w.shape[2]
    mod_all = _all_gather(mod_part.reshape(depth * N_DEV * n_ada // 128, 128), "mod_all_gather")
    mod_all = mod_all.reshape(N_DEV, depth, N_DEV, n_ada)
    mod_mine = lax.dynamic_index_in_dim(mod_all, me, axis=2, keepdims=False)
    mod = jnp.transpose(mod_mine, (1, 0, 2)).reshape(depth, N_DEV * n_ada) + ada_b
    mod = mod.reshape(depth, 3, 3, 1, D)

    small = jnp.concatenate([norm_g.reshape(-1), conv_w.reshape(-1)]).reshape(-1, 128)
    small_all = _all_gather(small, "small_all_gather").reshape(N_DEV, -1)
    n_ng = norm_g.size
    norm_g_full = jnp.transpose(small_all[:, :n_ng].reshape(N_DEV, depth, 3, dsh), (1, 2, 0, 3)).reshape(depth, 3, 1, D)
    conv_w_full = jnp.transpose(small_all[:, n_ng:].reshape(N_DEV, depth, CONV_K, dsh), (1, 2, 0, 3)).reshape(depth, CONV_K, D)

    buckets = jnp.asarray(_bucket_tables())
    bias = _bias_build(rel_bias, buckets, "bias_build")
    perm_dils = tuple(dil for dil in DILS if dil > 1)

    fwd = _gather_forward(gather[0], gather[1], gather[2], piece_rows, bias, "weights_gather_forward_l0")
    W = [unpack(_gather_finish(fwd[0], fwd[1], fwd[2], bias, "weights_gather_finish_l0"))]

    saved = []
    xc = x0
    for l in range(depth):
        sv = {}
        gather = None
        if l + 1 < depth:
            gather = _gather_start(pack(l + 1), piece_rows, W[l]["wo"], f"weights_gather_start_l{l + 1}")
        for sub in (0, 1, 2):
            g, sh, sc, gt = norm_g_full[l, sub], mod[l, sub, 0], mod[l, sub, 1], mod[l, sub, 2]
            if sub == 0 and gather is not None:
                g = behind(g, gather[3])
            rec = dict(x=xc)
            if sub != 1:
                i = 0 if sub == 0 else 1
                h = _norm_mod_fwd(xc, g, sc, sh, "norm_mod_fwd")[0]
                a, u, z = _ffn_up(h, W[l]["g_t"][i], W[l]["u_t"][i], "ffn_up")
                xc, f = _matmul(z, W[l]["down"][i], "nn", BF16, "ffn_down", tm=512, resid=(xc, gt, 0.5))
                rec.update(h=h, a=a, u=u, z=z, f=f)
            else:
                hs = _norm_mod_fwd(xc, g, sc, sh, "norm_mod_fwd_mixer", dils=perm_dils)
                h = hs[0]
                h_res = [h] + [_flat(t) for t in hs[1:]]
                cg = _matmul(h, W[l]["cg_t"], "nt", BF16, "mixer_cg")
                qkvs, os_, lses = [], [], []
                for gi, dil in enumerate(DILS):
                    qkv = _matmul(h_res[gi], W[l]["qkv_t"][gi], "nt", BF16, "mixer_qkv")
                    o_g, lse_g = _attn_fwd(qkv, bias[gi], dil, f"attn_fwd_g{gi}")
                    qkvs.append(qkv)
                    os_.append(o_g)
                    lses.append(lse_g)
                comb = _combine_fwd(os_, lses, "combine_fwd")
                o_f, o_b, lse = comb[0:3]
                lse_res = [lse] + [_flat(t) for t in comb[3:]]
                yc_in = _conv_fwd(cg, conv_w_full[l], "conv_fwd")
                yc = _matmul(yc_in, W[l]["co"], "nn", BF16, "conv_out")
                ya = _matmul(o_b, W[l]["ao_t"], "nt", BF16, "attn_out")
                merged = _merge_fwd(cg, yc, ya, "merge_fwd")
                xc, f = _matmul(merged, W[l]["wo"], "nn", BF16, "mixer_out", resid=(xc, gt, 1.0))
                rec.update(h=h, h_res=h_res, qkvs=qkvs, cg=cg, o_f=o_f, o_b=o_b, lse_res=lse_res, yc_in=yc_in,
                           yc=yc, ya=ya, merged=merged, f=f)
                if gather is not None:
                    fwd = _gather_forward(gather[0], gather[1], gather[2], piece_rows, xc,
                                          f"weights_gather_forward_l{l + 1}")
            sv[sub] = rec
        if gather is not None:
            W.append(unpack(_gather_finish(fwd[0], fwd[1], fwd[2], xc, f"weights_gather_finish_l{l + 1}")))
        saved.append(sv)

    dx, head = _loss_head(xc, final_g.reshape(1, D), target, "loss_head")
    d_final_g = head[0]
    loss_part = head[2, 0]

    d_mod = [[None] * 3 for _ in range(depth)]
    d_norm = [[None] * 3 for _ in range(depth)]
    d_conv = [None] * depth
    dlog = jnp.zeros((N_GROUPS, HEADS_PER_GROUP, BLOCK, 2 * BLOCK), F32)
    n_pieces = len(piece_rows)
    LATE = (0, 2, 4)
    EARLY = tuple(p for p in range(n_pieces) if p not in LATE)
    g_piece = [[None] * n_pieces for _ in range(depth)]

    piece_keys = (("g_t", 0), ("g_t", 1), ("u_t", 0), ("u_t", 1), ("down", 0), ("down", 1), "in_t", "co", "wo", "ao_t")

    def pieces_of(dW, idx):
        return [dW[piece_keys[p]].reshape(N_DEV * piece_rows[p], D) for p in idx]

    def finish_scatter(sc, idx, after_arr, layer, part=""):
        recv = _scatter_finish(sc[0], sc[1], sc[2], after_arr, f"grads_scatter_finish_l{layer}{part}")
        tot = _sum_sources(recv, "grads_sum")
        o = 0
        for p in idx:
            g_piece[layer][p] = tot[o:o + piece_rows[p]]
            o += piece_rows[p]
        return recv

    scatter = None
    early = None
    for l in reversed(range(depth)):
        dW = {}
        for sub in (2, 1, 0):
            rec = saved[l][sub]
            g, sc, gt = norm_g_full[l, sub], mod[l, sub, 1], mod[l, sub, 2]
            if sub == 2 and scatter is not None:
                gt = behind(gt, scatter[3])
            if sub == 0 and early is not None:
                gt = behind(gt, early[3])
            if sub != 1:
                i = 0 if sub == 0 else 1
                df, gsum = _gate_bwd(dx, rec["f"], gt, 0.5, "gate_bwd")
                dz = _matmul(df, W[l]["down"][i], "nt", BF16, "ffn_down_dx")
                dW["down", i] = _matmul(rec["z"], df, "tn", BF16, "ffn_down_dw")
                da, du, dh = _ffn_up_bwd(dz, rec["a"], rec["u"], W[l]["g_t"][i], W[l]["u_t"][i], "ffn_up_bwd")
                dW["g_t", i] = _matmul(da, rec["h"], "tn", BF16, "ffn_gate_dw")
                dW["u_t", i] = _matmul(du, rec["h"], "tn", BF16, "ffn_up_dw")
                dx, sums = _norm_mod_bwd(rec["x"], [dh], [], dx, g, sc, "norm_mod_bwd")
            else:
                dout, gsum = _gate_bwd(dx, rec["f"], gt, 1.0, "gate_bwd_mixer")
                dm = _matmul(dout, W[l]["wo"], "nt", BF16, "mixer_out_dx")
                dW["wo"] = _matmul(rec["merged"], dout, "tn", BF16, "mixer_out_dw")
                dyc, dya, dgg = _merge_bwd(rec["cg"], rec["yc"], rec["ya"], dm, "merge_bwd")
                dyc_in = _matmul(dyc, W[l]["co"], "nt", BF16, "conv_out_dx")
                dW["co"] = _matmul(rec["yc_in"], dyc, "tn", BF16, "conv_out_dw")
                do = _matmul(dya, W[l]["ao_t"], "nn", F32, "attn_out_dx")
                dW["ao_t"] = _matmul(dya, rec["o_b"], "tn", BF16, "attn_out_dw")
                dl = _delta(do, rec["o_f"], "attn_delta")
                do_res = [dl[0]] + [_flat(t) for t in dl[2::2]]
                del_res = [dl[1]] + [_flat(t) for t in dl[3::2]]
                dh_attn, dw_q, dw_kv, dlog_l = [], [], [], []
                for gi, dil in enumerate(DILS):
                    dq, dkv, dlg = _attn_bwd(rec["qkvs"][gi], do_res[gi], rec["lse_res"][gi], del_res[gi],
                                             bias[gi], dil, f"attn_bwd_g{gi}")
                    dlog_l.append(dlg)
                    dh_attn.append(_attn_dh(dq, dkv, W[l]["qkv_t"][gi], "attn_dh"))
                    dw_q.append(_matmul(dq, rec["h_res"][gi], "tn", BF16, "mixer_q_dw"))
                    dw_kv.append(_matmul(dkv, rec["h_res"][gi], "tn", BF16, "mixer_kv_dw"))
                dlog = dlog + jnp.stack(dlog_l)
                dcg, conv_sum = _conv_bwd(rec["cg"], dyc_in, dgg, conv_w_full[l], "conv_bwd")
                d_conv[l] = conv_sum[0:CONV_K]
                dh_cg = _matmul(dcg, W[l]["cg_t"], "nn", F32, "mixer_cg_dx")
                dw_cg = _matmul(dcg, rec["h"], "tn", BF16, "mixer_cg_dw")
                dW["in_t"] = jnp.concatenate(
                    dw_q + [t[:ATTN_OUT] for t in dw_kv] + [t[ATTN_OUT:] for t in dw_kv] + [dw_cg], axis=0)
                perm_parts = [(dil, _by_residue(dh_attn[gi], dil)) for gi, dil in enumerate(DILS) if dil > 1]
                dx, sums = _norm_mod_bwd(rec["x"], [dh_cg, dh_attn[0]], perm_parts, dx, g, sc, "norm_mod_bwd_mixer")
            d_mod[l][sub] = jnp.stack([sums[0], sums[2], gsum[0]])
            d_norm[l][sub] = sums[3]
            if l == 0 and sub == 1:
                after = dx
                if scatter is not None:
                    after = finish_scatter(scatter, range(n_pieces), dx, l + 1)
                    scatter = None
                early = _scatter_start(pieces_of(dW, EARLY), after, "grads_scatter_start_l0_early")
        if l > 0:
            after = dx
            if scatter is not None:
                after = finish_scatter(scatter, range(n_pieces), dx, l + 1)
            scatter = _scatter_start(pieces_of(dW, range(n_pieces)), after, f"grads_scatter_start_l{l}")
        else:
            late = _scatter_start(pieces_of(dW, LATE), dx, "grads_scatter_start_l0_late")
            finish_scatter(early, EARLY, late[3], 0, "_early")
            finish_scatter(late, LATE, dx, 0, "_late")
    grad_x = dx.reshape(1, S, D)
    d_rel = _bias_bwd(dlog, buckets, "bias_bwd")[:, :rel_bias.shape[1]]

    def shard_grad(p, transpose, shape=None):
        rows = [g_piece[l][p] for l in range(depth)]
        if shape is not None:
            rows = [t.reshape(shape) for t in rows]
        return jnp.stack([jnp.transpose(t) if transpose else t for t in rows])

    g_gate = jnp.stack([shard_grad(0, True), shard_grad(1, True)], axis=1)
    g_up = jnp.stack([shard_grad(2, True), shard_grad(3, True)], axis=1)
    g_down = jnp.stack([shard_grad(4, False), shard_grad(5, False)], axis=1)
    g_w_in = shard_grad(6, True)
    g_co = shard_grad(7, False)
    g_wo = shard_grad(8, False)
    g_ao = shard_grad(9, True, (dsh, ATTN_OUT))

    d_mod_flat = jnp.stack([jnp.stack(d_mod[l]) for l in range(depth)]).reshape(-1)
    d_norm_flat = jnp.stack([jnp.stack(d_norm[l]) for l in range(depth)]).reshape(-1)
    d_conv_flat = jnp.stack(d_conv).reshape(-1)
    vec = jnp.concatenate([d_mod_flat, d_norm_flat, d_conv_flat, d_rel.reshape(-1), d_final_g,
                           jnp.broadcast_to(loss_part, (128,))])
    pad = (-vec.size) % 1024
    vec = jnp.concatenate([vec, jnp.zeros((pad,), F32)]).reshape(-1, 128)
    parts = _all_gather(vec, "small_grads_all_gather").reshape(N_DEV, vec.shape[0], 128)
    tot = _sum_rows8(parts, "small_grads_sum").reshape(-1)
    o0 = 0
    g_ada_b = tot[o0:o0 + d_mod_flat.size].reshape(ada_b.shape)
    o0 += d_mod_flat.size
    g_norm_full = tot[o0:o0 + d_norm_flat.size].reshape(depth, 3, D)
    o0 += d_norm_flat.size
    g_conv_full = tot[o0:o0 + d_conv_flat.size].reshape(depth, CONV_K, D)
    o0 += d_conv_flat.size
    g_rel = tot[o0:o0 + rel_bias.size].reshape(rel_bias.shape)
    o0 += rel_bias.size
    g_final = tot[o0:o0 + D]
    o0 += D
    loss = tot[o0]
    g_norm = lax.dynamic_slice_in_dim(g_norm_full, me * dsh, dsh, axis=2)
    g_conv = lax.dynamic_slice_in_dim(g_conv_full, me * dsh, dsh, axis=2)

    dm_all = parts.reshape(N_DEV, -1)[:, :d_mod_flat.size].reshape(N_DEV, depth, N_DEV * n_ada)
    dm_cols = lax.dynamic_slice_in_dim(dm_all, me * n_ada, n_ada, axis=2)
    dm16 = jnp.concatenate([jnp.transpose(dm_cols, (1, 0, 2)), jnp.zeros((depth, N_DEV, n_ada), F32)], axis=1)
    g_ada_w = _ada_bwd(cs16, dm16, "ada_bwd")

    grads = dict(ada_w=g_ada_w, ada_b=g_ada_b, norm_g=g_norm, ffn_w_gate=g_gate, ffn_w_up=g_up,
                 ffn_w_down=g_down, w_in=g_w_in, conv_w=g_conv, w_conv_out=g_co, w_attn_out=g_ao, w_o=g_wo,
                 rel_bias=g_rel, final_g=g_final)
    weights = dict(ada_w=ada_w, ada_b=ada_b, norm_g=norm_g, ffn_w_gate=ffn_w_gate, ffn_w_up=ffn_w_up,
                   ffn_w_down=ffn_w_down, w_in=w_in, conv_w=conv_w, w_conv_out=w_conv_out, w_attn_out=w_attn_out,
                   w_o=w_o, rel_bias=rel_bias, final_g=final_g)
    ms = dict(ada_w=m_ada_w, ada_b=m_ada_b, norm_g=m_norm_g, ffn_w_gate=m_ffn_w_gate, ffn_w_up=m_ffn_w_up,
              ffn_w_down=m_ffn_w_down, w_in=m_w_in, conv_w=m_conv_w, w_conv_out=m_w_conv_out,
              w_attn_out=m_w_attn_out, w_o=m_w_o, rel_bias=m_rel_bias, final_g=m_final_g)
    vs = dict(ada_w=v_ada_w, ada_b=v_ada_b, norm_g=v_norm_g, ffn_w_gate=v_ffn_w_gate, ffn_w_up=v_ffn_w_up,
              ffn_w_down=v_ffn_w_down, w_in=v_w_in, conv_w=v_conv_w, w_conv_out=v_w_conv_out,
              w_attn_out=v_w_attn_out, w_o=v_w_o, rel_bias=v_rel_bias, final_g=v_final_g)
    order = list(weights)
    deltas, new_m, new_v = [], [], []
    for name in order:
        d_, m_, v_ = _adamw(weights[name], grads[name], ms[name], vs[name], "adamw_" + name)
        deltas.append(d_)
        new_m.append(m_)
        new_v.append(v_)
    return (loss, grad_x, *[grads[n] for n in order], *deltas, *new_m, *new_v)
```

```python
import functools

import numpy as np
import jax
import jax.numpy as jnp
from jax import lax
from jax.experimental import pallas as pl
from jax.experimental.pallas import tpu as pltpu

F32 = jnp.float32
BF16 = jnp.bfloat16

N_DEV = 8
HEAD_DIM = 64
HEAD_SHIFT = 6
HEADS_PER_GROUP = 8
DILATION_GROUPS = ((128, 1), (512, 4), (2048, 16))
DILS = tuple(d for _, d in DILATION_GROUPS)
N_GROUPS = len(DILATION_GROUPS)
ATTN_OUT = HEADS_PER_GROUP * HEAD_DIM
QKV_W = N_GROUPS * ATTN_OUT
BLOCK = 128
NUM_BUCKETS = 32
MAX_DISTANCE = 2048
CONV_K = 3
EPS = 1e-6
NEG_INF = -1e30
SCALE = HEAD_DIM ** -0.5

ADAM_LR = 0.001
ADAM_B1 = 0.9
ADAM_B2 = 0.999
ADAM_EPS = 1e-08
ADAM_WD = 0.01
ADAM_STEP = 10

V7X_VMEM_LIMIT = 48 * 1024 * 1024
MESH = pl.DeviceIdType.MESH

NN = (((1,), (0,)), ((), ()))
NT = (((1,), (1,)), ((), ()))
TN = (((0,), (0,)), ((), ()))


def _pick(dim, cands):
    for c in cands:
        if dim % c == 0:
            return c
    return dim


def _pick_k(K, cap=2816):
    if K <= cap or K % 128:
        return K
    best = 128
    for m in range(1, K // 128 + 1):
        if (K // 128) % m == 0 and 128 * m <= cap:
            best = 128 * m
    return best


def _params(sem):
    return pltpu.CompilerParams(dimension_semantics=sem, vmem_limit_bytes=V7X_VMEM_LIMIT)


def _all_gather(x_shard, name):
    m_per, n = x_shard.shape

    def body(x_ref, out_ref, send_sems, recv_sems, local_sem):
        x, y, c = lax.axis_index("x"), lax.axis_index("y"), lax.axis_index("c")
        me, sibling = (x, y, c), (x, y, 1 - c)
        chips = [(1 - x, y), (x, 1 - y), (1 - x, 1 - y)]

        def rows(px, py, pc):
            return out_ref.at[pl.ds((4 * px + 2 * py + pc) * m_per, m_per), :]

        def copy(k, block, to, src=None):
            return pltpu.make_async_remote_copy(
                src_ref=rows(*block) if src is None else src, dst_ref=rows(*block),
                send_sem=send_sems.at[k], recv_sem=recv_sems.at[k], device_id=to, device_id_type=MESH)

        mine = pltpu.make_async_copy(x_ref, rows(*me), local_sem)
        mine.start()
        first = [copy(0, me, sibling, src=x_ref)]
        first += [copy(1 + j, me, (*chip, c), src=x_ref) for j, chip in enumerate(chips)]
        for cp in first:
            cp.start()
        passed = [copy(4 + j, (*chip, c), sibling) for j, chip in enumerate(chips)]
        for j, chip in enumerate(chips):
            copy(1 + j, (*chip, c), me).wait_recv()
            passed[j].start()
        copy(0, sibling, me).wait_recv()
        for j, chip in enumerate(chips):
            copy(4 + j, (*chip, 1 - c), me).wait_recv()
        for cp in first + passed:
            cp.wait_send()
        mine.wait()

    return pl.pallas_call(
        body, name=name,
        out_shape=jax.ShapeDtypeStruct((N_DEV * m_per, n), x_shard.dtype),
        in_specs=[pl.BlockSpec(memory_space=pltpu.VMEM)],
        out_specs=pl.BlockSpec(memory_space=pltpu.VMEM),
        scratch_shapes=[pltpu.SemaphoreType.DMA((7,)), pltpu.SemaphoreType.DMA((7,)), pltpu.SemaphoreType.DMA],
    )(x_shard)


def _offsets(piece_rows):
    offs, o = [], 0
    for n in piece_rows:
        offs.append(o)
        o += n
    return offs


HBM_SPEC = pl.BlockSpec(memory_space=pltpu.HBM)
SEM_SPEC = pl.BlockSpec(memory_space=pltpu.SEMAPHORE)
ANY_SPEC = pl.BlockSpec(memory_space=pl.ANY)
SPLIT_COPY_PARAMS = pltpu.CompilerParams(has_side_effects=pltpu.SideEffectType.DATAFLOW_SIDE_EFFECTING)


def _in_hbm(a):
    return pltpu.with_memory_space_constraint(a, pltpu.HBM)


def _dma_sems(n):
    return [pltpu.SemaphoreType.DMA(())] * n


def _whole(ref, send_sem, recv_sem, me):
    return pltpu.make_async_remote_copy(src_ref=ref, dst_ref=ref, send_sem=send_sem, recv_sem=recv_sem,
                                        device_id=me, device_id_type=MESH)


def _gather_start(packed, piece_rows, after, name):
    R, w = packed.shape
    offs = _offsets(piece_rows)
    P = len(piece_rows)
    assert offs[-1] + piece_rows[-1] == R

    def body(*refs):
        src_ref = refs[0]
        o = refs[P + 2:]
        send, recv = o[0:4], o[4:8]
        zones, token, stage, local_sems = o[9:9 + P], o[9 + P], o[10 + P], o[11 + P]
        x, y, c = lax.axis_index("x"), lax.axis_index("y"), lax.axis_index("c")
        targets = [(x, y, 1 - c), (1 - x, y, c), (x, 1 - y, c), (1 - x, 1 - y, c)]
        me = 4 * x + 2 * y + c

        def piece(p, ref):
            return ref.at[pl.ds(offs[p], piece_rows[p]), :]

        def rows(p):
            return zones[p].at[pl.ds(me * piece_rows[p], piece_rows[p]), :]

        for k, to in enumerate(targets):
            for p in range(P):
                pltpu.make_async_remote_copy(src_ref=piece(p, src_ref), dst_ref=rows(p), send_sem=send[k],
                                             recv_sem=recv[k], device_id=to, device_id_type=MESH).start()
        load = pltpu.make_async_copy(src_ref, stage, local_sems.at[P])
        load.start()
        load.wait()
        mine = [pltpu.make_async_copy(piece(p, stage), rows(p), local_sems.at[p]) for p in range(P)]
        for cp in mine:
            cp.start()
        for cp in mine:
            cp.wait()
        token[...] = jnp.zeros_like(token)

    zones_in = [_in_hbm(lax.empty((N_DEV * n, w), packed.dtype)) for n in piece_rows]
    outs = pl.pallas_call(
        body, name=name,
        out_shape=(*_dma_sems(8), pltpu.HBM((R, w), packed.dtype),
                   *[pltpu.HBM((N_DEV * n, w), packed.dtype) for n in piece_rows],
                   jax.ShapeDtypeStruct((8, 128), F32)),
        in_specs=[HBM_SPEC] * (P + 1) + [ANY_SPEC],
        out_specs=[SEM_SPEC] * 8 + [HBM_SPEC] * (P + 1) + [pl.BlockSpec(memory_space=pltpu.VMEM)],
        input_output_aliases={0: 8, **{1 + p: 9 + p for p in range(P)}},
        scratch_shapes=[pltpu.VMEM((R, w), packed.dtype), pltpu.SemaphoreType.DMA((P + 1,))],
        compiler_params=SPLIT_COPY_PARAMS,
    )(_in_hbm(packed), *zones_in, after)
    return outs[0:8], outs[8], list(outs[9:9 + P]), outs[9 + P]


def _gather_forward(sems, packed, zones, piece_rows, after, name):
    P = len(piece_rows)

    def body(*refs):
        src_ref = refs[0]
        s = refs[1 + P:9 + P]
        o = refs[10 + P:]
        send, recv = s[0:4], s[4:8]
        send2, recv2, zones_o = o[0:3], o[3:6], o[7:7 + P]
        x, y, c = lax.axis_index("x"), lax.axis_index("y"), lax.axis_index("c")
        me = (x, y, c)
        chips = [(1 - x, y), (x, 1 - y), (1 - x, 1 - y)]
        for j, (px, py) in enumerate(chips):
            _whole(src_ref, send[1 + j], recv[1 + j], me).wait_recv()
            blk = 4 * px + 2 * py + c
            for p in range(P):
                r = zones_o[p].at[pl.ds(blk * piece_rows[p], piece_rows[p]), :]
                pltpu.make_async_remote_copy(src_ref=r, dst_ref=r, send_sem=send2[j], recv_sem=recv2[j],
                                             device_id=(x, y, 1 - c), device_id_type=MESH).start()
        _whole(src_ref, send[0], recv[0], me).wait_recv()
        for k in range(4):
            _whole(src_ref, send[k], recv[k], me).wait_send()

    outs = pl.pallas_call(
        body, name=name,
        out_shape=(*_dma_sems(6), pltpu.HBM(packed.shape, packed.dtype),
                   *[pltpu.HBM(z.shape, z.dtype) for z in zones]),
        in_specs=[HBM_SPEC] * (P + 1) + [SEM_SPEC] * 8 + [ANY_SPEC],
        out_specs=[SEM_SPEC] * 6 + [HBM_SPEC] * (P + 1),
        input_output_aliases={0: 6, **{1 + p: 7 + p for p in range(P)}},
        compiler_params=SPLIT_COPY_PARAMS,
    )(packed, *zones, *sems, after)
    return outs[0:6], outs[6], list(outs[7:7 + P])


def _gather_finish(sems2, packed, zones, after, name):
    P = len(zones)

    def body(*refs):
        src_ref = refs[0]
        s = refs[1 + P:7 + P]
        x, y, c = lax.axis_index("x"), lax.axis_index("y"), lax.axis_index("c")
        for j in range(3):
            _whole(src_ref, s[j], s[3 + j], (x, y, c)).wait_recv()
        for j in range(3):
            _whole(src_ref, s[j], s[3 + j], (x, y, c)).wait_send()

    outs = pl.pallas_call(
        body, name=name,
        out_shape=(pltpu.HBM(packed.shape, packed.dtype), *[pltpu.HBM(z.shape, z.dtype) for z in zones]),
        in_specs=[HBM_SPEC] * (P + 1) + [SEM_SPEC] * 6 + [ANY_SPEC],
        out_specs=[HBM_SPEC] * (P + 1),
        input_output_aliases={p: p for p in range(P + 1)},
        compiler_params=SPLIT_COPY_PARAMS,
    )(packed, *zones, *sems2, after)
    return list(outs[1:1 + P])


def _scatter_start(pieces, after, name):
    P = len(pieces)
    w = pieces[0].shape[1]
    piece_rows = [p.shape[0] // N_DEV for p in pieces]
    offs = _offsets(piece_rows)
    R = offs[-1] + piece_rows[-1]

    def body(*refs):
        o = refs[P + 2:]
        send, recv = o[0:7], o[7:14]
        srcs, dst_ref, token, stage, local_sems = o[14:14 + P], o[14 + P], o[15 + P], o[16 + P], o[17 + P]
        x, y, c = lax.axis_index("x"), lax.axis_index("y"), lax.axis_index("c")
        me = 4 * x + 2 * y + c

        def chunk(p, dev):
            return srcs[p].at[pl.ds(dev * piece_rows[p], piece_rows[p]), :]

        def slot(p, dev):
            return dst_ref.at[dev, pl.ds(offs[p], piece_rows[p]), :]

        for k in range(1, N_DEV):
            px = 1 - x if (k >> 2) & 1 else x
            py = 1 - y if (k >> 1) & 1 else y
            pc = 1 - c if k & 1 else c
            peer = 4 * px + 2 * py + pc
            for p in range(P):
                pltpu.make_async_remote_copy(
                    src_ref=chunk(p, peer), dst_ref=slot(p, me), send_sem=send[k - 1], recv_sem=recv[k - 1],
                    device_id=(px, py, pc), device_id_type=MESH).start()
        mine = [pltpu.make_async_copy(chunk(p, me), stage.at[pl.ds(offs[p], piece_rows[p]), :], local_sems.at[p])
                for p in range(P)]
        for cp in mine:
            cp.start()
        for cp in mine:
            cp.wait()
        store = pltpu.make_async_copy(stage, dst_ref.at[me], local_sems.at[P])
        store.start()
        store.wait()
        token[...] = jnp.zeros_like(token)

    dtype = pieces[0].dtype
    outs = pl.pallas_call(
        body, name=name,
        out_shape=(*_dma_sems(14), *[pltpu.HBM(p.shape, dtype) for p in pieces], pltpu.HBM((N_DEV, R, w), dtype),
                   jax.ShapeDtypeStruct((8, 128), F32)),
        in_specs=[HBM_SPEC] * (P + 1) + [ANY_SPEC],
        out_specs=[SEM_SPEC] * 14 + [HBM_SPEC] * (P + 1) + [pl.BlockSpec(memory_space=pltpu.VMEM)],
        input_output_aliases={p: 14 + p for p in range(P + 1)},
        scratch_shapes=[pltpu.VMEM((R, w), dtype), pltpu.SemaphoreType.DMA((P + 1,))],
        compiler_params=SPLIT_COPY_PARAMS,
    )(*[_in_hbm(p) for p in pieces], _in_hbm(lax.empty((N_DEV, R, w), dtype)), after)
    return outs[0:14], list(outs[14:14 + P]), outs[14 + P], outs[15 + P]


def _scatter_finish(sems, pieces, recv, after, name):
    P = len(pieces)

    def body(*refs):
        dst_ref = refs[P]
        s = refs[P + 1:P + 15]
        x, y, c = lax.axis_index("x"), lax.axis_index("y"), lax.axis_index("c")
        for k in range(7):
            _whole(dst_ref.at[0], s[k], s[7 + k], (x, y, c)).wait_recv()
        for k in range(7):
            _whole(dst_ref.at[0], s[k], s[7 + k], (x, y, c)).wait_send()

    outs = pl.pallas_call(
        body, name=name,
        out_shape=(*[pltpu.HBM(p.shape, p.dtype) for p in pieces], pltpu.HBM(recv.shape, recv.dtype)),
        in_specs=[HBM_SPEC] * (P + 1) + [SEM_SPEC] * 14 + [ANY_SPEC],
        out_specs=[HBM_SPEC] * (P + 1),
        input_output_aliases={p: p for p in range(P + 1)},
        compiler_params=SPLIT_COPY_PARAMS,
    )(*pieces, recv, *sems, after)
    return outs[P]


def _sum_sources(parts, name):
    _, r, n = parts.shape
    tr = _pick(r, [256, 128, 64, 32, 16, 8])

    def kern(p_ref, o_ref):
        acc = p_ref[0].astype(F32)
        for k in range(1, N_DEV):
            acc = acc + p_ref[k].astype(F32)
        o_ref[...] = acc

    return pl.pallas_call(
        kern, name=name, grid=(r // tr,),
        out_shape=jax.ShapeDtypeStruct((r, n), F32),
        in_specs=[pl.BlockSpec((N_DEV, tr, n), lambda i: (0, i, 0))],
        out_specs=pl.BlockSpec((tr, n), lambda i: (i, 0)),
        compiler_params=_params(("parallel",)),
    )(parts)


def _matmul(a, b, mode, out_dtype, name, tm=None, tn=None, tk=None, resid=None):
    if mode == "nn":
        (M, K), N = a.shape, b.shape[1]
    elif mode == "nt":
        (M, K), N = a.shape, b.shape[0]
    else:
        (K, M), N = a.shape, b.shape[1]
    dims = {"nn": NN, "nt": NT, "tn": TN}[mode]
    tm = tm or _pick(M, [1024, 1408, 512, 256, 128])
    tn = tn or _pick(N, [1024, 1408, 512, 256, 128])
    tk = tk or _pick_k(K)
    nk = K // tk
    a_spec = {"nn": pl.BlockSpec((tm, tk), lambda i, j, k: (i, k)),
              "nt": pl.BlockSpec((tm, tk), lambda i, j, k: (i, k)),
              "tn": pl.BlockSpec((tk, tm), lambda i, j, k: (k, i))}[mode]
    b_spec = {"nn": pl.BlockSpec((tk, tn), lambda i, j, k: (k, j)),
              "nt": pl.BlockSpec((tn, tk), lambda i, j, k: (j, k)),
              "tn": pl.BlockSpec((tk, tn), lambda i, j, k: (k, j))}[mode]
    o_spec = pl.BlockSpec((tm, tn), lambda i, j, k: (i, j))
    n_in = 2 if resid is None else 4
    n_out = 1 if resid is None else 2

    def kern(*refs):
        a_ref, b_ref = refs[0], refs[1]
        outs = refs[n_in:n_in + n_out]
        acc_ref = refs[n_in + n_out] if nk > 1 else None

        def finish(acc):
            if resid is None:
                outs[0][...] = acc.astype(out_dtype)
            else:
                x_ref, g_ref = refs[2], refs[3]
                outs[0][...] = x_ref[...] + (resid[2] * g_ref[...]) * acc
                outs[1][...] = acc.astype(out_dtype)

        part = lax.dot_general(a_ref[...], b_ref[...], dims, preferred_element_type=F32)
        if nk == 1:
            finish(part)
        else:
            k = pl.program_id(2)

            @pl.when(k == 0)
            def _():
                acc_ref[...] = part

            @pl.when(k > 0)
            def _():
                acc_ref[...] += part

            @pl.when(k == nk - 1)
            def _():
                finish(acc_ref[...])

    in_specs = [a_spec, b_spec]
    args = [a, b]
    out_shape = [jax.ShapeDtypeStruct((M, N), out_dtype)]
    out_specs = [o_spec]
    if resid is not None:
        in_specs += [o_spec, pl.BlockSpec((1, tn), lambda i, j, k: (0, j))]
        args += [resid[0], resid[1]]
        out_shape = [jax.ShapeDtypeStruct((M, N), F32)] + out_shape
        out_specs = [o_spec, o_spec]
    res = pl.pallas_call(
        kern, name=name, grid=(M // tm, N // tn, nk),
        out_shape=out_shape, in_specs=in_specs, out_specs=out_specs,
        scratch_shapes=[pltpu.VMEM((tm, tn), F32)] if nk > 1 else [],
        compiler_params=_params(("parallel", "parallel", "arbitrary")),
    )(*args)
    return res[0] if resid is None else res


def _dot3(a, b, dims):
    ah = a.astype(BF16)
    al = (a - ah.astype(F32)).astype(BF16)
    bh = b.astype(BF16)
    bl = (b - bh.astype(F32)).astype(BF16)
    d = functools.partial(lax.dot_general, dimension_numbers=dims, preferred_element_type=F32)
    return d(ah, bh) + (d(ah, bl) + d(al, bh))


def _silu_parts(a):
    sg = jax.nn.sigmoid(a)
    return a * sg, sg * (1.0 + a * (1.0 - sg))


def _ffn_up(h, wg_t, wu_t, name):
    S, D = h.shape
    F = wg_t.shape[0]
    tm = _pick(S, [512, 256, 128])
    tn = _pick(F, [1408, 512, 256, 128])

    def kern(h_ref, g_ref, u_ref, a_out, u_out, z_out):
        hv = h_ref[...]
        a = lax.dot_general(hv, g_ref[...], NT, preferred_element_type=F32)
        u = lax.dot_general(hv, u_ref[...], NT, preferred_element_type=F32)
        a_out[...] = a.astype(BF16)
        u_out[...] = u.astype(BF16)
        z_out[...] = (_silu_parts(a)[0] * u).astype(BF16)

    w_spec = pl.BlockSpec((tn, D), lambda j, i: (j, 0))
    o_spec = pl.BlockSpec((tm, tn), lambda j, i: (i, j))
    return pl.pallas_call(
        kern, name=name, grid=(F // tn, S // tm),
        out_shape=[jax.ShapeDtypeStruct((S, F), BF16)] * 3,
        in_specs=[pl.BlockSpec((tm, D), lambda j, i: (i, 0)), w_spec, w_spec],
        out_specs=[o_spec] * 3,
        compiler_params=_params(("parallel", "parallel")),
    )(h, wg_t, wu_t)


def _ffn_up_bwd(dz, a, u, wg_t, wu_t, name):
    S, F = dz.shape
    D = wg_t.shape[1]
    tm = _pick(S, [512, 256, 128])
    tk = _pick(F, [1408, 512, 256, 128])
    nk = F // tk

    def kern(dz_ref, a_ref, u_ref, g_ref, w_ref, da_out, du_out, dh_out, acc_ref):
        k = pl.program_id(1)
        av = a_ref[...].astype(F32)
        uv = u_ref[...].astype(F32)
        dzv = dz_ref[...].astype(F32)
        silu, dsilu = _silu_parts(av)
        da = (dzv * uv * dsilu).astype(BF16)
        du = (dzv * silu).astype(BF16)
        da_out[...] = da
        du_out[...] = du
        part = (lax.dot_general(da, g_ref[...], NN, preferred_element_type=F32)
                + lax.dot_general(du, w_ref[...], NN, preferred_element_type=F32))

        @pl.when(k == 0)
        def _():
            acc_ref[...] = part

        @pl.when(k > 0)
        def _():
            acc_ref[...] += part

        @pl.when(k == nk - 1)
        def _():
            dh_out[...] = acc_ref[...]

    t_spec = pl.BlockSpec((tm, tk), lambda i, k: (i, k))
    w_spec = pl.BlockSpec((tk, D), lambda i, k: (k, 0))
    return pl.pallas_call(
        kern, name=name, grid=(S // tm, nk),
        out_shape=[jax.ShapeDtypeStruct((S, F), BF16)] * 2 + [jax.ShapeDtypeStruct((S, D), F32)],
        in_specs=[t_spec, t_spec, t_spec, w_spec, w_spec],
        out_specs=[t_spec, t_spec, pl.BlockSpec((tm, D), lambda i, k: (i, 0))],
        scratch_shapes=[pltpu.VMEM((tm, D), F32)],
        compiler_params=_params(("parallel", "arbitrary")),
    )(dz, a, u, wg_t, wu_t)


def _attn_dh(dq, dkv, w_t, name):
    S = dq.shape[0]
    D = w_t.shape[1]
    tm = _pick(S, [1024, 512, 256, 128])

    def kern(dq_ref, dk_ref, dv_ref, wq_ref, wk_ref, wv_ref, o_ref):
        o_ref[...] = (lax.dot_general(dq_ref[...], wq_ref[...], NN, preferred_element_type=F32)
                      + lax.dot_general(dk_ref[...], wk_ref[...], NN, preferred_element_type=F32)
                      + lax.dot_general(dv_ref[...], wv_ref[...], NN, preferred_element_type=F32))

    def w_blk(j):
        return pl.BlockSpec((ATTN_OUT, D), lambda i: (j, 0))

    return pl.pallas_call(
        kern, name=name, grid=(S // tm,),
        out_shape=jax.ShapeDtypeStruct((S, D), F32),
        in_specs=[pl.BlockSpec((tm, ATTN_OUT), lambda i: (i, 0)), pl.BlockSpec((tm, ATTN_OUT), lambda i: (i, 0)),
                  pl.BlockSpec((tm, ATTN_OUT), lambda i: (i, 1)), w_blk(0), w_blk(1), w_blk(2)],
        out_specs=pl.BlockSpec((tm, D), lambda i: (i, 0)),
        compiler_params=_params(("parallel",)),
    )(dq, dkv, dkv, w_t, w_t, w_t)


def _row_spec(tm, d):
    return pl.BlockSpec((tm, d), lambda i: (i, 0))


def _vec_spec(d, rows=1):
    return pl.BlockSpec((rows, d), lambda i: (0, 0))


def _perm_spec(dil, tm, w):
    return pl.BlockSpec((dil, tm // dil, w), lambda i: (0, i, 0))


def _stage_shape(tm, w):
    return pltpu.VMEM((w // 128, tm, 128), F32)


def _stage(scr, val):
    for ci in range(scr.shape[0]):
        scr[ci] = val[:, 128 * ci:128 * (ci + 1)]


def _unstage(scr):
    return jnp.concatenate([scr[ci] for ci in range(scr.shape[0])], axis=1)


def _get_residue(scr, res, dil):
    n = scr.shape[1] // dil
    return jnp.concatenate([scr[ci, pl.ds(res, n, stride=dil), :] for ci in range(scr.shape[0])], axis=1)


def _put_residue(scr, res, dil, val):
    n = scr.shape[1] // dil
    for ci in range(scr.shape[0]):
        scr[ci, pl.ds(res, n, stride=dil), :] = val[:, 128 * ci:128 * (ci + 1)]


def _norm_mod_fwd(x, g, s, b, name, dils=()):
    S, D = x.shape
    tm = _pick(S, [256, 128])

    def kern(x_ref, g_ref, s_ref, b_ref, h_ref, *rest):
        xv = x_ref[...]
        r = lax.rsqrt(jnp.mean(xv * xv, axis=1, keepdims=True) + EPS)
        hv = xv * r * g_ref[...] * (1.0 + s_ref[...]) + b_ref[...]
        h_ref[...] = hv.astype(BF16)
        if dils:
            scr = rest[len(dils)]
            _stage(scr, hv)
            for dil, p_ref in zip(dils, rest[:len(dils)]):
                for res in range(dil):
                    p_ref[res] = _get_residue(scr, res, dil).astype(BF16)

    return pl.pallas_call(
        kern, name=name, grid=(S // tm,),
        out_shape=[jax.ShapeDtypeStruct((S, D), BF16)] + [jax.ShapeDtypeStruct((dil, S // dil, D), BF16) for dil in dils],
        in_specs=[_row_spec(tm, D), _vec_spec(D), _vec_spec(D), _vec_spec(D)],
        out_specs=[_row_spec(tm, D)] + [_perm_spec(dil, tm, D) for dil in dils],
        scratch_shapes=[_stage_shape(tm, D)] if dils else [],
        compiler_params=_params(("parallel",)),
    )(x, g, s, b)


def _norm_mod_bwd(x, dh_nat, dh_perm, dxo, g, s, name):
    S, D = x.shape
    tm = _pick(S, [256, 128])
    n = S // tm
    n_nat, n_perm = len(dh_nat), len(dh_perm)

    def kern(*refs):
        x_ref = refs[0]
        nat = refs[1:1 + n_nat]
        perm = refs[1 + n_nat:1 + n_nat + n_perm]
        dxo_ref, g_ref, s_ref, dx_ref, cs_ref = refs[1 + n_nat + n_perm:6 + n_nat + n_perm]
        scr = refs[6 + n_nat + n_perm:]
        i = pl.program_id(0)
        xv = x_ref[...]
        r = lax.rsqrt(jnp.mean(xv * xv, axis=1, keepdims=True) + EPS)
        xn = xv * r
        dh_v = nat[0][...].astype(F32)
        for t in nat[1:]:
            dh_v = dh_v + t[...].astype(F32)
        for (dil, _), p_ref, sc in zip(dh_perm, perm, scr):
            for res in range(dil):
                _put_residue(sc, res, dil, p_ref[res])
            dh_v = dh_v + _unstage(sc)
        one_s = 1.0 + s_ref[...]
        dxn = dh_v * (g_ref[...] * one_s)
        dx_ref[...] = dxo_ref[...] + r * (dxn - xn * jnp.mean(xn * dxn, axis=1, keepdims=True))

        @pl.when(i == 0)
        def _():
            cs_ref[...] = jnp.zeros_like(cs_ref)

        cs_ref[0:1, :] += jnp.sum(dh_v, axis=0, keepdims=True)
        cs_ref[1:2, :] += jnp.sum(dh_v * xn, axis=0, keepdims=True)

        @pl.when(i == n - 1)
        def _():
            t = cs_ref[1:2, :]
            cs_ref[2:3, :] = g_ref[...] * t
            cs_ref[3:4, :] = one_s * t

    return pl.pallas_call(
        kern, name=name, grid=(n,),
        out_shape=[jax.ShapeDtypeStruct((S, D), F32), jax.ShapeDtypeStruct((8, D), F32)],
        in_specs=[_row_spec(tm, D)] + [_row_spec(tm, D)] * n_nat + [_perm_spec(dil, tm, D) for dil, _ in dh_perm]
        + [_row_spec(tm, D), _vec_spec(D), _vec_spec(D)],
        out_specs=[_row_spec(tm, D), _vec_spec(D, 8)],
        scratch_shapes=[_stage_shape(tm, D) for _ in dh_perm],
        compiler_params=_params(("arbitrary",)),
    )(x, *dh_nat, *[a for _, a in dh_perm], dxo, g, s)


def _gate_bwd(dxo, f, gate, coef, name):
    S, D = dxo.shape
    tm = _pick(S, [512, 256, 128])

    def kern(dxo_ref, f_ref, gate_ref, df_ref, cs_ref):
        i = pl.program_id(0)
        dv = dxo_ref[...]
        df_ref[...] = ((coef * gate_ref[...]) * dv).astype(BF16)

        @pl.when(i == 0)
        def _():
            cs_ref[...] = jnp.zeros_like(cs_ref)

        cs_ref[0:1, :] += coef * jnp.sum(f_ref[...].astype(F32) * dv, axis=0, keepdims=True)

    return pl.pallas_call(
        kern, name=name, grid=(S // tm,),
        out_shape=[jax.ShapeDtypeStruct((S, D), BF16), jax.ShapeDtypeStruct((8, D), F32)],
        in_specs=[_row_spec(tm, D), _row_spec(tm, D), _vec_spec(D)],
        out_specs=[_row_spec(tm, D), _vec_spec(D, 8)],
        compiler_params=_params(("arbitrary",)),
    )(dxo, f, gate)


def _loss_head(x, g, target, name):
    S, D = x.shape
    tm = _pick(S, [256, 128])
    n = S // tm

    def kern(x_ref, g_ref, t_ref, dx_ref, cs_ref):
        i = pl.program_id(0)
        xv = x_ref[...]
        r = lax.rsqrt(jnp.mean(xv * xv, axis=1, keepdims=True) + EPS)
        xn = xv * r
        e = xn * g_ref[...] - t_ref[...]
        dxn = (e * (1.0 / D)) * g_ref[...]
        dx_ref[...] = r * (dxn - xn * jnp.mean(xn * dxn, axis=1, keepdims=True))

        @pl.when(i == 0)
        def _():
            cs_ref[...] = jnp.zeros_like(cs_ref)

        cs_ref[0:1, :] += jnp.sum(xn * e, axis=0, keepdims=True) * (1.0 / D)
        cs_ref[1:2, :] += jnp.sum(e * e, axis=0, keepdims=True)

        @pl.when(i == n - 1)
        def _():
            tot = jnp.sum(cs_ref[1:2, :], axis=1, keepdims=True) * (0.5 / D)
            cs_ref[2:3, :] = jnp.broadcast_to(tot, (1, D))

    return pl.pallas_call(
        kern, name=name, grid=(n,),
        out_shape=[jax.ShapeDtypeStruct((S, D), F32), jax.ShapeDtypeStruct((8, D), F32)],
        in_specs=[_row_spec(tm, D), _vec_spec(D), _row_spec(tm, D)],
        out_specs=[_row_spec(tm, D), _vec_spec(D, 8)],
        compiler_params=_params(("arbitrary",)),
    )(x, g, target)


def _shift_down(p, row, prev_rows):
    a, b = prev_rows
    p1 = jnp.where(row == 0, b, pltpu.roll(p, 1, 0))
    p2 = jnp.where(row == 0, a, jnp.where(row == 1, b, pltpu.roll(p, 2, 0)))
    return p1, p2


def _conv_fwd(cg, conv_w, name):
    S, D5 = cg.shape
    D = D5 // 5
    tm = _pick(S, [256, 128])
    t8 = tm // 8

    def prev(col):
        return pl.BlockSpec((8, D), lambda i: (jnp.maximum(i * t8 - 1, 0), col))

    def kern(cb_ref, cc_ref, ch_ref, ccp_ref, chp_ref, w_ref, y_ref):
        i = pl.program_id(0)
        keep = jnp.where(i > 0, 1.0, 0.0)
        p = cc_ref[...].astype(F32) * ch_ref[...].astype(F32)
        pa = ccp_ref[6:7, :].astype(F32) * chp_ref[6:7, :].astype(F32) * keep
        pb = ccp_ref[7:8, :].astype(F32) * chp_ref[7:8, :].astype(F32) * keep
        row = lax.broadcasted_iota(jnp.int32, (tm, D), 0)
        p1, p2 = _shift_down(p, row, (pa, pb))
        dw = w_ref[0:1, :] * p2 + w_ref[1:2, :] * p1 + w_ref[2:3, :] * p
        y_ref[...] = (cb_ref[...].astype(F32) * dw).astype(BF16)

    def col(cidx):
        return pl.BlockSpec((tm, D), lambda i: (i, cidx))

    return pl.pallas_call(
        kern, name=name, grid=(S // tm,),
        out_shape=jax.ShapeDtypeStruct((S, D), BF16),
        in_specs=[col(0), col(1), col(2), prev(1), prev(2), _vec_spec(D, CONV_K)],
        out_specs=_row_spec(tm, D),
        compiler_params=_params(("parallel",)),
    )(cg, cg, cg, cg, cg, conv_w)


def _conv_bwd(cg, dy, dgg, conv_w, name):
    S, D5 = cg.shape
    D = D5 // 5
    tm = _pick(S, [256, 128])
    t8 = tm // 8
    n = S // tm
    last8 = S // 8 - 1

    def prev(col):
        return pl.BlockSpec((8, D), lambda i: (jnp.maximum(i * t8 - 1, 0), col))

    def nxt(col):
        return pl.BlockSpec((8, D), lambda i: (jnp.minimum((i + 1) * t8, last8), col))

    def kern(cb_ref, cc_ref, ch_ref, dy_ref, dgg_ref, ccp_ref, chp_ref, cbn_ref, dyn_ref, w_ref, d_ref, cs_ref):
        i = pl.program_id(0)
        keep_p = jnp.where(i > 0, 1.0, 0.0)
        keep_n = jnp.where(i < n - 1, 1.0, 0.0)
        cb = cb_ref[...].astype(F32)
        cc = cc_ref[...].astype(F32)
        ch = ch_ref[...].astype(F32)
        dyv = dy_ref[...].astype(F32)
        p = cc * ch
        pa = ccp_ref[6:7, :].astype(F32) * chp_ref[6:7, :].astype(F32) * keep_p
        pb = ccp_ref[7:8, :].astype(F32) * chp_ref[7:8, :].astype(F32) * keep_p
        row = lax.broadcasted_iota(jnp.int32, (tm, D), 0)
        p1, p2 = _shift_down(p, row, (pa, pb))
        w0, w1, w2 = w_ref[0:1, :], w_ref[1:2, :], w_ref[2:3, :]
        dw = w0 * p2 + w1 * p1 + w2 * p
        ddw = dyv * cb
        na = dyn_ref[0:1, :].astype(F32) * cbn_ref[0:1, :].astype(F32) * keep_n
        nb = dyn_ref[1:2, :].astype(F32) * cbn_ref[1:2, :].astype(F32) * keep_n
        u1 = jnp.where(row == tm - 1, na, pltpu.roll(ddw, tm - 1, 0))
        u2 = jnp.where(row == tm - 2, na, jnp.where(row == tm - 1, nb, pltpu.roll(ddw, tm - 2, 0)))
        dp = w2 * ddw + w1 * u1 + w0 * u2
        d_ref[:, 0:D] = (dyv * dw).astype(BF16)
        d_ref[:, D:2 * D] = (dp * ch).astype(BF16)
        d_ref[:, 2 * D:3 * D] = (dp * cc).astype(BF16)
        d_ref[:, 3 * D:5 * D] = dgg_ref[...]

        @pl.when(i == 0)
        def _():
            cs_ref[...] = jnp.zeros_like(cs_ref)

        cs_ref[0:1, :] += jnp.sum(ddw * p2, axis=0, keepdims=True)
        cs_ref[1:2, :] += jnp.sum(ddw * p1, axis=0, keepdims=True)
        cs_ref[2:3, :] += jnp.sum(ddw * p, axis=0, keepdims=True)

    def col(cidx):
        return pl.BlockSpec((tm, D), lambda i: (i, cidx))

    return pl.pallas_call(
        kern, name=name, grid=(n,),
        out_shape=[jax.ShapeDtypeStruct((S, 5 * D), BF16), jax.ShapeDtypeStruct((8, D), F32)],
        in_specs=[col(0), col(1), col(2), _row_spec(tm, D), _row_spec(tm, 2 * D), prev(1), prev(2), nxt(0),
                  pl.BlockSpec((8, D), lambda i: (jnp.minimum((i + 1) * t8, last8), 0)), _vec_spec(D, CONV_K)],
        out_specs=[_row_spec(tm, 5 * D), _vec_spec(D, 8)],
        compiler_params=_params(("arbitrary",)),
    )(cg, cg, cg, dy, dgg, cg, cg, cg, dy, conv_w)


def _merge_fwd(cg, yc, ya, name):
    S, D = yc.shape
    tm = _pick(S, [512, 256, 128])

    def kern(gc_ref, ga_ref, yc_ref, ya_ref, m_ref):
        m_ref[...] = (jax.nn.sigmoid(gc_ref[...].astype(F32)) * yc_ref[...].astype(F32)
                      + jax.nn.sigmoid(ga_ref[...].astype(F32)) * ya_ref[...].astype(F32)).astype(BF16)

    return pl.pallas_call(
        kern, name=name, grid=(S // tm,),
        out_shape=jax.ShapeDtypeStruct((S, D), BF16),
        in_specs=[pl.BlockSpec((tm, D), lambda i: (i, 3)), pl.BlockSpec((tm, D), lambda i: (i, 4)),
                  _row_spec(tm, D), _row_spec(tm, D)],
        out_specs=_row_spec(tm, D),
        compiler_params=_params(("parallel",)),
    )(cg, cg, yc, ya)


def _merge_bwd(cg, yc, ya, dm, name):
    S, D = yc.shape
    tm = _pick(S, [256, 128])

    def kern(gc_ref, ga_ref, yc_ref, ya_ref, dm_ref, dyc_ref, dya_ref, dg_ref):
        dmv = dm_ref[...].astype(F32)
        sc = jax.nn.sigmoid(gc_ref[...].astype(F32))
        sa = jax.nn.sigmoid(ga_ref[...].astype(F32))
        dyc_ref[...] = (dmv * sc).astype(BF16)
        dya_ref[...] = (dmv * sa).astype(BF16)
        dg_ref[:, 0:D] = (dmv * yc_ref[...].astype(F32) * (sc * (1.0 - sc))).astype(BF16)
        dg_ref[:, D:2 * D] = (dmv * ya_ref[...].astype(F32) * (sa * (1.0 - sa))).astype(BF16)

    return pl.pallas_call(
        kern, name=name, grid=(S // tm,),
        out_shape=[jax.ShapeDtypeStruct((S, D), BF16), jax.ShapeDtypeStruct((S, D), BF16),
                   jax.ShapeDtypeStruct((S, 2 * D), BF16)],
        in_specs=[pl.BlockSpec((tm, D), lambda i: (i, 3)), pl.BlockSpec((tm, D), lambda i: (i, 4)),
                  _row_spec(tm, D), _row_spec(tm, D), _row_spec(tm, D)],
        out_specs=[_row_spec(tm, D), _row_spec(tm, D), pl.BlockSpec((tm, 2 * D), lambda i: (i, 0))],
        compiler_params=_params(("parallel",)),
    )(cg, cg, yc, ya, dm)


def _t5_bucket(dist):
    exact = NUM_BUCKETS // 2
    d = np.maximum(dist, 1).astype(np.float32)
    large = exact + (np.log(d / exact) / np.log(MAX_DISTANCE / exact) * (NUM_BUCKETS - exact)).astype(np.int32)
    large = np.minimum(large, NUM_BUCKETS - 1)
    return np.where(dist < exact, dist, large).astype(np.int32)


def _bucket_tables():
    i = np.arange(BLOCK)[:, None]
    j = np.arange(2 * BLOCK)[None, :]
    rel = i - j + BLOCK
    return np.stack([_t5_bucket(np.maximum(rel, 0) * d) for _, d in DILATION_GROUPS]).astype(np.int32)


def _band_masks():
    i = lax.broadcasted_iota(jnp.int32, (BLOCK, 2 * BLOCK), 0)
    j = lax.broadcasted_iota(jnp.int32, (BLOCK, 2 * BLOCK), 1)
    rel = i - j + BLOCK
    band = (rel >= 0) & (rel <= BLOCK)
    return band, band & (j >= BLOCK)


def _bias_build(rel_bias, buckets, name):
    def kern(rb_ref, bk_ref, o_ref):
        g = pl.program_id(0)
        bk = bk_ref[0]
        band, first = _band_masks()
        for h in range(HEADS_PER_GROUP):
            acc = jnp.zeros((BLOCK, 2 * BLOCK), F32)
            for b in range(NUM_BUCKETS):
                acc = jnp.where(bk == b, rb_ref[b, g * HEADS_PER_GROUP + h], acc)
            o_ref[0, 0, h] = jnp.where(first, acc, NEG_INF)
            o_ref[0, 1, h] = jnp.where(band, acc, NEG_INF)

    return pl.pallas_call(
        kern, name=name, grid=(N_GROUPS,),
        out_shape=jax.ShapeDtypeStruct((N_GROUPS, 2, HEADS_PER_GROUP, BLOCK, 2 * BLOCK), F32),
        in_specs=[pl.BlockSpec(memory_space=pltpu.SMEM),
                  pl.BlockSpec((1, BLOCK, 2 * BLOCK), lambda g: (g, 0, 0))],
        out_specs=pl.BlockSpec((1, 2, HEADS_PER_GROUP, BLOCK, 2 * BLOCK), lambda g: (g, 0, 0, 0, 0)),
        compiler_params=_params(("parallel",)),
    )(rel_bias, buckets)


def _bias_bwd(dlog, buckets, name):
    def kern(dl_ref, bk_ref, o_ref):
        g = pl.program_id(0)
        bk = bk_ref[0]
        rowi = lax.broadcasted_iota(jnp.int32, (NUM_BUCKETS, 128), 0)
        coli = lax.broadcasted_iota(jnp.int32, (NUM_BUCKETS, 128), 1)

        @pl.when(g == 0)
        def _():
            o_ref[...] = jnp.zeros_like(o_ref)

        acc = jnp.zeros((NUM_BUCKETS, 128), F32)
        for h in range(HEADS_PER_GROUP):
            dv = dl_ref[0, h]
            for b in range(NUM_BUCKETS):
                t = jnp.sum(jnp.where(bk == b, dv, 0.0), axis=0, keepdims=True)
                t = jnp.sum(t, axis=1, keepdims=True)
                acc = acc + jnp.where((rowi == b) & (coli == g * HEADS_PER_GROUP + h), t, 0.0)
        o_ref[...] += acc

    return pl.pallas_call(
        kern, name=name, grid=(N_GROUPS,),
        out_shape=jax.ShapeDtypeStruct((NUM_BUCKETS, 128), F32),
        in_specs=[pl.BlockSpec((1, HEADS_PER_GROUP, BLOCK, 2 * BLOCK), lambda g: (g, 0, 0, 0)),
                  pl.BlockSpec((1, BLOCK, 2 * BLOCK), lambda g: (g, 0, 0))],
        out_specs=pl.BlockSpec((NUM_BUCKETS, 128), lambda g: (0, 0)),
        compiler_params=_params(("arbitrary",)),
    )(dlog, buckets)


def _head_masks():
    lane = lax.broadcasted_iota(jnp.int32, (BLOCK, 128), 1)
    lo = lane < HEAD_DIM
    return lo, jnp.logical_not(lo)


def _attn_fwd(qkv, bias, d, name):
    S = qkv.shape[0]
    nb = S // d // BLOCK

    def kern(q_ref, kp_ref, kc_ref, vp_ref, vc_ref, b_ref, o_ref, lse_ref):
        lo, hi = _head_masks()
        for p in range(HEADS_PER_GROUP // 2):
            sl = slice(128 * p, 128 * (p + 1))
            q = q_ref[:, sl]
            k = jnp.concatenate([kp_ref[:, sl], kc_ref[:, sl]], axis=0)
            v = jnp.concatenate([vp_ref[:, sl], vc_ref[:, sl]], axis=0)
            o2, l2 = [], []
            for hh, msk in enumerate((lo, hi)):
                qm = jnp.where(msk, q, jnp.zeros_like(q))
                s = lax.dot_general(qm, k, NT, preferred_element_type=F32) * SCALE + b_ref[0, 2 * p + hh]
                m = jnp.max(s, axis=1, keepdims=True)
                e = jnp.exp(s - m)
                l = jnp.sum(e, axis=1, keepdims=True)
                o2.append(lax.dot_general(e.astype(BF16), v, NN, preferred_element_type=F32) / l)
                l2.append(jnp.broadcast_to(m + jnp.log(l), (BLOCK, 128)))
            o_ref[:, sl] = jnp.where(lo, o2[0], o2[1])
            lse_ref[:, sl] = jnp.where(lo, l2[0], l2[1])

    def blk(col, prev):
        if prev:
            return pl.BlockSpec((BLOCK, ATTN_OUT), lambda r, n: (r * nb + jnp.maximum(n - 1, 0), col))
        return pl.BlockSpec((BLOCK, ATTN_OUT), lambda r, n: (r * nb + n, col))

    o_spec = pl.BlockSpec((BLOCK, ATTN_OUT), lambda r, n: (r * nb + n, 0))
    return pl.pallas_call(
        kern, name=name, grid=(d, nb),
        out_shape=[jax.ShapeDtypeStruct((S, ATTN_OUT), F32)] * 2,
        in_specs=[blk(0, False), blk(1, True), blk(1, False), blk(2, True), blk(2, False),
                  pl.BlockSpec((1, HEADS_PER_GROUP, BLOCK, 2 * BLOCK), lambda r, n: (jnp.minimum(n, 1), 0, 0, 0))],
        out_specs=[o_spec, o_spec],
        compiler_params=_params(("parallel", "arbitrary")),
    )(qkv, qkv, qkv, qkv, qkv, bias)


def _attn_bwd(qkv, do, lse, delta, bias, d, name):
    S = qkv.shape[0]
    nb = S // d // BLOCK
    low = -3.0e38

    def kern(q_ref, kp_ref, kc_ref, vp_ref, vc_ref, do_ref, lse_ref, dl_ref, b_ref,
             dq_ref, dkv_ref, db_ref, ck_ref, cv_ref):
        r, n = pl.program_id(0), pl.program_id(1)

        @pl.when((r == 0) & (n == 0))
        def _():
            db_ref[...] = jnp.zeros_like(db_ref)

        @pl.when(n == 0)
        def _():
            ck_ref[...] = jnp.zeros_like(ck_ref)
            cv_ref[...] = jnp.zeros_like(cv_ref)

        @pl.when(n < nb)
        def _():
            lo, hi = _head_masks()
            for p in range(HEADS_PER_GROUP // 2):
                sl = slice(128 * p, 128 * (p + 1))
                sv = slice(ATTN_OUT + 128 * p, ATTN_OUT + 128 * (p + 1))
                q = q_ref[:, sl]
                k = jnp.concatenate([kp_ref[:, sl], kc_ref[:, sl]], axis=0)
                v = jnp.concatenate([vp_ref[:, sl], vc_ref[:, sl]], axis=0)
                dov = do_ref[:, sl]
                lse_b = lse_ref[:, sl]
                del_b = dl_ref[:, sl]
                dq2 = []
                dk_acc = jnp.zeros((2 * BLOCK, 128), F32)
                dv_acc = jnp.zeros((2 * BLOCK, 128), F32)
                for hh, msk in enumerate((lo, hi)):
                    qm = jnp.where(msk, q, jnp.zeros_like(q))
                    dom = jnp.where(msk, dov, jnp.zeros_like(dov))
                    lse_h = jnp.max(jnp.where(msk, lse_b, low), axis=1, keepdims=True)
                    del_h = jnp.max(jnp.where(msk, del_b, low), axis=1, keepdims=True)
                    s = lax.dot_general(qm, k, NT, preferred_element_type=F32) * SCALE + b_ref[0, 2 * p + hh]
                    pr = jnp.exp(s - lse_h)
                    dp = lax.dot_general(dom, v, NT, preferred_element_type=F32)
                    ds = pr * (dp - del_h)
                    db_ref[2 * p + hh] += ds
                    dsb = (ds * SCALE).astype(BF16)
                    dq2.append(lax.dot_general(dsb, k, NN, preferred_element_type=F32))
                    dk_acc = dk_acc + lax.dot_general(dsb, qm, TN, preferred_element_type=F32)
                    dv_acc = dv_acc + lax.dot_general(pr.astype(BF16), dom, TN, preferred_element_type=F32)
                dq_ref[:, sl] = jnp.where(lo, dq2[0], dq2[1]).astype(BF16)
                dkv_ref[:, sl] = (ck_ref[:, sl] + dk_acc[0:BLOCK]).astype(BF16)
                dkv_ref[:, sv] = (cv_ref[:, sl] + dv_acc[0:BLOCK]).astype(BF16)
                ck_ref[:, sl] = dk_acc[BLOCK:2 * BLOCK]
                cv_ref[:, sl] = dv_acc[BLOCK:2 * BLOCK]

        @pl.when(n == nb)
        def _():
            dkv_ref[:, 0:ATTN_OUT] = ck_ref[...].astype(BF16)
            dkv_ref[:, ATTN_OUT:2 * ATTN_OUT] = cv_ref[...].astype(BF16)

    def cur(n):
        return jnp.minimum(n, nb - 1)

    def blk(col, prev):
        if prev:
            return pl.BlockSpec((BLOCK, ATTN_OUT), lambda r, n: (r * nb + jnp.maximum(cur(n) - 1, 0), col))
        return pl.BlockSpec((BLOCK, ATTN_OUT), lambda r, n: (r * nb + cur(n), col))

    q_like = pl.BlockSpec((BLOCK, ATTN_OUT), lambda r, n: (r * nb + cur(n), 0))
    return pl.pallas_call(
        kern, name=name, grid=(d, nb + 1),
        out_shape=[jax.ShapeDtypeStruct((S, ATTN_OUT), BF16), jax.ShapeDtypeStruct((S, 2 * ATTN_OUT), BF16),
                   jax.ShapeDtypeStruct((HEADS_PER_GROUP, BLOCK, 2 * BLOCK), F32)],
        in_specs=[blk(0, False), blk(1, True), blk(1, False), blk(2, True), blk(2, False),
                  q_like, q_like, q_like,
                  pl.BlockSpec((1, HEADS_PER_GROUP, BLOCK, 2 * BLOCK),
                               lambda r, n: (jnp.minimum(cur(n), 1), 0, 0, 0))],
        out_specs=[q_like,
                   pl.BlockSpec((BLOCK, 2 * ATTN_OUT), lambda r, n: (r * nb + jnp.maximum(n - 1, 0), 0)),
                   pl.BlockSpec((HEADS_PER_GROUP, BLOCK, 2 * BLOCK), lambda r, n: (0, 0, 0))],
        scratch_shapes=[pltpu.VMEM((BLOCK, ATTN_OUT), F32), pltpu.VMEM((BLOCK, ATTN_OUT), F32)],
        compiler_params=_params(("arbitrary", "arbitrary")),
    )(qkv, qkv, qkv, qkv, qkv, do, lse, delta, bias)


def _by_residue(a, dil):
    return a if dil == 1 else a.reshape(dil, a.shape[0] // dil, a.shape[1])


def _flat(a):
    return a if a.ndim == 2 else a.reshape(a.shape[0] * a.shape[1], a.shape[2])


def _combine_fwd(os_, lses, name):
    S, W = os_[0].shape
    tm = _pick(S, [256, 128])
    perm = [dil for dil in DILS if dil > 1]

    def kern(*refs):
        o_in, l_in = refs[0:N_GROUPS], refs[N_GROUPS:2 * N_GROUPS]
        of_ref, ob_ref, lse_ref = refs[2 * N_GROUPS:2 * N_GROUPS + 3]
        lse_p = refs[2 * N_GROUPS + 3:2 * N_GROUPS + 3 + len(perm)]
        scr = refs[2 * N_GROUPS + 3 + len(perm):]
        ov, lv = [], []
        si = 0
        for g, dil in enumerate(DILS):
            if dil == 1:
                ov.append(o_in[g][...])
                lv.append(l_in[g][...])
            else:
                so, sl = scr[si], scr[si + 1]
                si += 2
                for res in range(dil):
                    _put_residue(so, res, dil, o_in[g][res])
                    _put_residue(sl, res, dil, l_in[g][res])
                ov.append(_unstage(so))
                lv.append(_unstage(sl))
        m = jnp.maximum(jnp.maximum(lv[0], lv[1]), lv[2])
        e = [jnp.exp(t - m) for t in lv]
        tot = e[0] + e[1] + e[2]
        o = (e[0] * ov[0] + e[1] * ov[1] + e[2] * ov[2]) / tot
        lse = m + jnp.log(tot)
        of_ref[...] = o
        ob_ref[...] = o.astype(BF16)
        lse_ref[...] = lse
        sl = scr[1]
        _stage(sl, lse)
        for dil, p_ref in zip(perm, lse_p):
            for res in range(dil):
                p_ref[res] = _get_residue(sl, res, dil)

    def in_spec(dil):
        return _row_spec(tm, W) if dil == 1 else _perm_spec(dil, tm, W)

    ins = [_by_residue(a, dil) for a, dil in zip(os_, DILS)] + [_by_residue(a, dil) for a, dil in zip(lses, DILS)]
    return pl.pallas_call(
        kern, name=name, grid=(S // tm,),
        out_shape=[jax.ShapeDtypeStruct((S, W), F32), jax.ShapeDtypeStruct((S, W), BF16),
                   jax.ShapeDtypeStruct((S, W), F32)]
        + [jax.ShapeDtypeStruct((dil, S // dil, W), F32) for dil in perm],
        in_specs=[in_spec(dil) for dil in DILS] * 2,
        out_specs=[_row_spec(tm, W)] * 3 + [_perm_spec(dil, tm, W) for dil in perm],
        scratch_shapes=[_stage_shape(tm, W) for _ in range(2 * len(perm))],
        compiler_params=_params(("parallel",)),
    )(*ins)


def _delta(do, o, name):
    S, W = o.shape
    tm = _pick(S, [256, 128])
    perm = [dil for dil in DILS if dil > 1]

    def kern(do_ref, o_ref, dob_ref, d_ref, *rest):
        scr, scr_do = rest[2 * len(perm)], rest[2 * len(perm) + 1]
        prod = do_ref[...] * o_ref[...]
        ri = jnp.right_shift(lax.broadcasted_iota(jnp.int32, (W, W), 0), HEAD_SHIFT)
        ci = jnp.right_shift(lax.broadcasted_iota(jnp.int32, (W, W), 1), HEAD_SHIFT)
        same = jnp.where(ri == ci, 1.0, 0.0).astype(BF16)
        hi_p = prod.astype(BF16)
        lo_p = (prod - hi_p.astype(F32)).astype(BF16)
        dl = (lax.dot_general(hi_p, same, NN, preferred_element_type=F32)
              + lax.dot_general(lo_p, same, NN, preferred_element_type=F32))
        d_ref[...] = dl
        dob_ref[...] = do_ref[...].astype(BF16)
        _stage(scr, dl)
        _stage(scr_do, do_ref[...])
        for j, dil in enumerate(perm):
            for res in range(dil):
                rest[2 * j][res] = _get_residue(scr_do, res, dil).astype(BF16)
                rest[2 * j + 1][res] = _get_residue(scr, res, dil)

    out_shape = [jax.ShapeDtypeStruct((S, W), BF16), jax.ShapeDtypeStruct((S, W), F32)]
    out_specs = [_row_spec(tm, W), _row_spec(tm, W)]
    for dil in perm:
        out_shape += [jax.ShapeDtypeStruct((dil, S // dil, W), BF16), jax.ShapeDtypeStruct((dil, S // dil, W), F32)]
        out_specs += [_perm_spec(dil, tm, W), _perm_spec(dil, tm, W)]
    return pl.pallas_call(
        kern, name=name, grid=(S // tm,),
        out_shape=out_shape,
        in_specs=[_row_spec(tm, W), _row_spec(tm, W)], out_specs=out_specs,
        scratch_shapes=[_stage_shape(tm, W), _stage_shape(tm, W)],
        compiler_params=_params(("parallel",)),
    )(do, o)


def _ada_fwd(c16, ada_w, name):
    depth, D, n = ada_w.shape
    rows = 2 * N_DEV

    def kern(c_ref, w_ref, o_ref, cs_ref):
        cv = c_ref[...]
        cs = cv * jax.nn.sigmoid(cv)
        cs_ref[...] = cs
        o_ref[0] = _dot3(cs, w_ref[0], NN)

    return pl.pallas_call(
        kern, name=name, grid=(depth,),
        out_shape=[jax.ShapeDtypeStruct((depth, rows, n), F32), jax.ShapeDtypeStruct((rows, D), F32)],
        in_specs=[pl.BlockSpec((rows, D), lambda l: (0, 0)), pl.BlockSpec((1, D, n), lambda l: (l, 0, 0))],
        out_specs=[pl.BlockSpec((1, rows, n), lambda l: (l, 0, 0)), pl.BlockSpec((rows, D), lambda l: (0, 0))],
        compiler_params=_params(("arbitrary",)),
    )(c16, ada_w)


def _ada_bwd(cs16, dm16, name):
    depth, _, n = dm16.shape
    D = cs16.shape[1]

    def kern(cs_ref, dm_ref, o_ref):
        o_ref[0] = _dot3(cs_ref[...], dm_ref[0], TN)

    return pl.pallas_call(
        kern, name=name, grid=(depth,),
        out_shape=jax.ShapeDtypeStruct((depth, D, n), F32),
        in_specs=[pl.BlockSpec((2 * N_DEV, D), lambda l: (0, 0)), pl.BlockSpec((1, 2 * N_DEV, n), lambda l: (l, 0, 0))],
        out_specs=pl.BlockSpec((1, D, n), lambda l: (l, 0, 0)),
        compiler_params=_params(("parallel",)),
    )(cs16, dm16)


def _sum_rows8(parts, name):
    _, r, n = parts.shape

    def kern(p_ref, o_ref):
        acc = p_ref[0]
        for k in range(1, N_DEV):
            acc = acc + p_ref[k]
        o_ref[...] = acc

    return pl.pallas_call(
        kern, name=name, out_shape=jax.ShapeDtypeStruct((r, n), F32),
        in_specs=[pl.BlockSpec(memory_space=pltpu.VMEM)], out_specs=pl.BlockSpec(memory_space=pltpu.VMEM),
    )(parts)


def _adamw(w, g, m, v, name):
    shape = w.shape
    c = shape[-1]
    r = int(np.prod(shape[:-1])) if len(shape) > 1 else 1
    w2, g2, m2, v2 = (t.reshape(r, c) for t in (w, g, m, v))
    tr = r
    for cand in (2048, 1024, 512, 256, 128, 64, 32, 16, 8):
        if r % cand == 0 and cand * c * 4 <= (1 << 20):
            tr = cand
            break
    c1 = 1.0 - ADAM_B1 ** ADAM_STEP
    c2 = 1.0 - ADAM_B2 ** ADAM_STEP

    def kern(w_ref, g_ref, m_ref, v_ref, d_ref, nm_ref, nv_ref):
        gv = g_ref[...]
        nm = ADAM_B1 * m_ref[...] + (1.0 - ADAM_B1) * gv
        nv = ADAM_B2 * v_ref[...] + (1.0 - ADAM_B2) * (gv * gv)
        nm_ref[...] = nm
        nv_ref[...] = nv
        d_ref[...] = -ADAM_LR * ((nm / c1) / (jnp.sqrt(nv / c2) + ADAM_EPS) + ADAM_WD * w_ref[...])

    spec = pl.BlockSpec((tr, c), lambda i: (i, 0))
    outs = pl.pallas_call(
        kern, name=name, grid=(r // tr,),
        out_shape=[jax.ShapeDtypeStruct((r, c), F32)] * 3,
        in_specs=[spec] * 4, out_specs=[spec] * 3,
        compiler_params=_params(("parallel",)),
    )(w2, g2, m2, v2)
    return tuple(o.reshape(shape) for o in outs)


def kernel(x, c, ada_w, ada_b, norm_g, ffn_w_gate, ffn_w_up, ffn_w_down, w_in, conv_w, w_conv_out, w_attn_out, w_o, rel_bias, final_g, loss_target, m_ada_w, m_ada_b, m_norm_g, m_ffn_w_gate, m_ffn_w_up, m_ffn_w_down, m_w_in, m_conv_w, m_w_conv_out, m_w_attn_out, m_w_o, m_rel_bias, m_final_g, v_ada_w, v_ada_b, v_norm_g, v_ffn_w_gate, v_ffn_w_up, v_ffn_w_down, v_w_in, v_conv_w, v_w_conv_out, v_w_attn_out, v_w_o, v_rel_bias, v_final_g):
    depth = ada_w.shape[0]
    S, D = x.shape[1], x.shape[2]
    me = 4 * lax.axis_index("x") + 2 * lax.axis_index("y") + lax.axis_index("c")
    x0 = x.reshape(S, D)
    target = loss_target.reshape(S, D)
    fsh = ffn_w_down.shape[2]
    insh = w_in.shape[2]
    dsh = D // N_DEV
    ao_rows = dsh * ATTN_OUT // D

    piece_rows = [fsh] * 6 + [insh, dsh, dsh, ao_rows]

    FIRST = [0, 2, 4]
    REST = [p for p in range(len(piece_rows)) if p not in FIRST]

    def rows_of(idx):
        return [piece_rows[p] for p in idx]

    def pack(l, idx=None):
        def t(a):
            return jnp.transpose(a).astype(BF16)
        ps = [t(ffn_w_gate[l, 0]), t(ffn_w_gate[l, 1]), t(ffn_w_up[l, 0]), t(ffn_w_up[l, 1]),
              ffn_w_down[l, 0].astype(BF16), ffn_w_down[l, 1].astype(BF16), t(w_in[l]),
              w_conv_out[l].astype(BF16), w_o[l].astype(BF16), t(w_attn_out[l]).reshape(ao_rows, D)]
        return jnp.concatenate(ps if idx is None else [ps[p] for p in idx], axis=0)

    def unpack(full):
        in_t = full[6]
        qkv_t = [jnp.concatenate([in_t[t * QKV_W + g * ATTN_OUT: t * QKV_W + (g + 1) * ATTN_OUT] for t in range(3)])
                 for g in range(N_GROUPS)]
        ao_t = full[9].reshape(N_DEV, dsh, ATTN_OUT).reshape(D, ATTN_OUT)
        return dict(g_t=full[0:2], u_t=full[2:4], down=full[4:6], qkv_t=qkv_t, cg_t=in_t[3 * QKV_W:],
                    co=full[7], wo=full[8], ao_t=ao_t)

    def behind(v, token):
        return v + token[0, 0]

    first = _gather_start(pack(0, FIRST), rows_of(FIRST), c, "weights_gather_start_l0_first")
    rest = _gather_start(pack(0, REST), rows_of(REST), first[3], "weights_gather_start_l0_rest")

    c_all = _all_gather(behind(c, rest[3]).reshape(D // 128, 128), "c_all_gather").reshape(N_DEV, D)
    c16 = jnp.concatenate([c_all, jnp.zeros_like(c_all)], axis=0)
    mod_part, cs16 = _ada_fwd(c16, ada_w, "ada_fwd")
    mod_part = mod_part[:, :N_DEV]
    n_ada = ada_w.shape[2]
    mod_all = _all_gather(mod_part.reshape(depth * N_DEV * n_ada // 128, 128), "mod_all_gather")
    mod_all = mod_all.reshape(N_DEV, depth, N_DEV, n_ada)
    mod_mine = lax.dynamic_index_in_dim(mod_all, me, axis=2, keepdims=False)
    mod = jnp.transpose(mod_mine, (1, 0, 2)).reshape(depth, N_DEV * n_ada) + ada_b
    mod = mod.reshape(depth, 3, 3, 1, D)

    small = jnp.concatenate([norm_g.reshape(-1), conv_w.reshape(-1)]).reshape(-1, 128)
    small_all = _all_gather(small, "small_all_gather").reshape(N_DEV, -1)
    n_ng = norm_g.size
    norm_g_full = jnp.transpose(small_all[:, :n_ng].reshape(N_DEV, depth, 3, dsh), (1, 2, 0, 3)).reshape(depth, 3, 1, D)
    conv_w_full = jnp.transpose(small_all[:, n_ng:].reshape(N_DEV, depth, CONV_K, dsh), (1, 2, 0, 3)).reshape(depth, CONV_K, D)

    buckets = jnp.asarray(_bucket_tables())
    bias = _bias_build(rel_bias, buckets, "bias_build")
    perm_dils = tuple(dil for dil in DILS if dil > 1)

    fwd = _gather_forward(first[0], first[1], first[2], rows_of(FIRST), bias, "weights_gather_forward_l0_first")
    zones_first = _gather_finish(fwd[0], fwd[1], fwd[2], bias, "weights_gather_finish_l0_first")
    W = [dict(g_t=[zones_first[0]], u_t=[zones_first[1]], down=[zones_first[2]])] + [None] * (depth - 1)

    saved = []
    xc = x0
    for l in range(depth):
        sv = {}
        gather, tie_sub = None, 0
        if 0 < l < depth - 1:
            gather = _gather_start(pack(l + 1), piece_rows, W[l]["wo"], f"weights_gather_start_l{l + 1}")
        for sub in (0, 1, 2):
            if l == 0 and sub == 1:
                fwd = _gather_forward(rest[0], rest[1], rest[2], rows_of(REST), xc, "weights_gather_forward_l0_rest")
                zones_rest = _gather_finish(fwd[0], fwd[1], fwd[2], xc, "weights_gather_finish_l0_rest")
                full = [None] * len(piece_rows)
                for p, z in zip(FIRST + REST, zones_first + zones_rest):
                    full[p] = z
                W[0] = unpack(full)
                if depth > 1:
                    gather, tie_sub = _gather_start(pack(1), piece_rows, W[0]["wo"], "weights_gather_start_l1"), 1
            g, sh, sc, gt = norm_g_full[l, sub], mod[l, sub, 0], mod[l, sub, 1], mod[l, sub, 2]
            if sub == tie_sub and gather is not None:
                g = behind(g, gather[3])
            rec = dict(x=xc)
            if sub != 1:
                i = 0 if sub == 0 else 1
                h = _norm_mod_fwd(xc, g, sc, sh, "norm_mod_fwd")[0]
                a, u, z = _ffn_up(h, W[l]["g_t"][i], W[l]["u_t"][i], "ffn_up")
                xc, f = _matmul(z, W[l]["down"][i], "nn", BF16, "ffn_down", tm=512, resid=(xc, gt, 0.5))
                rec.update(h=h, a=a, u=u, z=z, f=f)
            else:
                hs = _norm_mod_fwd(xc, g, sc, sh, "norm_mod_fwd_mixer", dils=perm_dils)
                h = hs[0]
                h_res = [h] + [_flat(t) for t in hs[1:]]
                cg = _matmul(h, W[l]["cg_t"], "nt", BF16, "mixer_cg")
                qkvs, os_, lses = [], [], []
                for gi, dil in enumerate(DILS):
                    qkv = _matmul(h_res[gi], W[l]["qkv_t"][gi], "nt", BF16, "mixer_qkv")
                    o_g, lse_g = _attn_fwd(qkv, bias[gi], dil, f"attn_fwd_g{gi}")
                    qkvs.append(qkv)
                    os_.append(o_g)
                    lses.append(lse_g)
                comb = _combine_fwd(os_, lses, "combine_fwd")
                o_f, o_b, lse = comb[0:3]
                lse_res = [lse] + [_flat(t) for t in comb[3:]]
                yc_in = _conv_fwd(cg, conv_w_full[l], "conv_fwd")
                yc = _matmul(yc_in, W[l]["co"], "nn", BF16, "conv_out")
                ya = _matmul(o_b, W[l]["ao_t"], "nt", BF16, "attn_out")
                merged = _merge_fwd(cg, yc, ya, "merge_fwd")
                xc, f = _matmul(merged, W[l]["wo"], "nn", BF16, "mixer_out", resid=(xc, gt, 1.0))
                rec.update(h=h, h_res=h_res, qkvs=qkvs, cg=cg, o_f=o_f, o_b=o_b, lse_res=lse_res, yc_in=yc_in,
                           yc=yc, ya=ya, merged=merged, f=f)
                if gather is not None:
                    fwd = _gather_forward(gather[0], gather[1], gather[2], piece_rows, xc,
                                          f"weights_gather_forward_l{l + 1}")
            sv[sub] = rec
        if gather is not None:
            W[l + 1] = unpack(_gather_finish(fwd[0], fwd[1], fwd[2], xc, f"weights_gather_finish_l{l + 1}"))
        saved.append(sv)

    dx, head = _loss_head(xc, final_g.reshape(1, D), target, "loss_head")
    d_final_g = head[0]
    loss_part = head[2, 0]

    d_mod = [[None] * 3 for _ in range(depth)]
    d_norm = [[None] * 3 for _ in range(depth)]
    d_conv = [None] * depth
    dlog = jnp.zeros((N_GROUPS, HEADS_PER_GROUP, BLOCK, 2 * BLOCK), F32)
    n_pieces = len(piece_rows)
    LATE = (0, 2, 4)
    EARLY = tuple(p for p in range(n_pieces) if p not in LATE)
    g_piece = [[None] * n_pieces for _ in range(depth)]

    piece_keys = (("g_t", 0), ("g_t", 1), ("u_t", 0), ("u_t", 1), ("down", 0), ("down", 1), "in_t", "co", "wo", "ao_t")

    def pieces_of(dW, idx):
        return [dW[piece_keys[p]].reshape(N_DEV * piece_rows[p], D) for p in idx]

    def finish_scatter(sc, idx, after_arr, layer, part=""):
        recv = _scatter_finish(sc[0], sc[1], sc[2], after_arr, f"grads_scatter_finish_l{layer}{part}")
        tot = _sum_sources(recv, "grads_sum")
        o = 0
        for p in idx:
            g_piece[layer][p] = tot[o:o + piece_rows[p]]
            o += piece_rows[p]
        return recv

    scatter = None
    early = None
    for l in reversed(range(depth)):
        dW = {}
        for sub in (2, 1, 0):
            rec = saved[l][sub]
            g, sc, gt = norm_g_full[l, sub], mod[l, sub, 1], mod[l, sub, 2]
            if sub == 2 and scatter is not None:
                gt = behind(gt, scatter[3])
            if sub == 0 and early is not None:
                gt = behind(gt, early[3])
            if sub != 1:
                i = 0 if sub == 0 else 1
                df, gsum = _gate_bwd(dx, rec["f"], gt, 0.5, "gate_bwd")
                dz = _matmul(df, W[l]["down"][i], "nt", BF16, "ffn_down_dx")
                dW["down", i] = _matmul(rec["z"], df, "tn", BF16, "ffn_down_dw")
                da, du, dh = _ffn_up_bwd(dz, rec["a"], rec["u"], W[l]["g_t"][i], W[l]["u_t"][i], "ffn_up_bwd")
                dW["g_t", i] = _matmul(da, rec["h"], "tn", BF16, "ffn_gate_dw")
                dW["u_t", i] = _matmul(du, rec["h"], "tn", BF16, "ffn_up_dw")
                dx, sums = _norm_mod_bwd(rec["x"], [dh], [], dx, g, sc, "norm_mod_bwd")
            else:
                dout, gsum = _gate_bwd(dx, rec["f"], gt, 1.0, "gate_bwd_mixer")
                dm = _matmul(dout, W[l]["wo"], "nt", BF16, "mixer_out_dx")
                dW["wo"] = _matmul(rec["merged"], dout, "tn", BF16, "mixer_out_dw")
                dyc, dya, dgg = _merge_bwd(rec["cg"], rec["yc"], rec["ya"], dm, "merge_bwd")
                dyc_in = _matmul(dyc, W[l]["co"], "nt", BF16, "conv_out_dx")
                dW["co"] = _matmul(rec["yc_in"], dyc, "tn", BF16, "conv_out_dw")
                do = _matmul(dya, W[l]["ao_t"], "nn", F32, "attn_out_dx")
                dW["ao_t"] = _matmul(dya, rec["o_b"], "tn", BF16, "attn_out_dw")
                dl = _delta(do, rec["o_f"], "attn_delta")
                do_res = [dl[0]] + [_flat(t) for t in dl[2::2]]
                del_res = [dl[1]] + [_flat(t) for t in dl[3::2]]
                dh_attn, dw_q, dw_kv, dlog_l = [], [], [], []
                for gi, dil in enumerate(DILS):
                    dq, dkv, dlg = _attn_bwd(rec["qkvs"][gi], do_res[gi], rec["lse_res"][gi], del_res[gi],
                                             bias[gi], dil, f"attn_bwd_g{gi}")
                    dlog_l.append(dlg)
                    dh_attn.append(_attn_dh(dq, dkv, W[l]["qkv_t"][gi], "attn_dh"))
                    dw_q.append(_matmul(dq, rec["h_res"][gi], "tn", BF16, "mixer_q_dw"))
                    dw_kv.append(_matmul(dkv, rec["h_res"][gi], "tn", BF16, "mixer_kv_dw"))
                dlog = dlog + jnp.stack(dlog_l)
                dcg, conv_sum = _conv_bwd(rec["cg"], dyc_in, dgg, conv_w_full[l], "conv_bwd")
                d_conv[l] = conv_sum[0:CONV_K]
                dh_cg = _matmul(dcg, W[l]["cg_t"], "nn", F32, "mixer_cg_dx")
                dw_cg = _matmul(dcg, rec["h"], "tn", BF16, "mixer_cg_dw")
                dW["in_t"] = jnp.concatenate(
                    dw_q + [t[:ATTN_OUT] for t in dw_kv] + [t[ATTN_OUT:] for t in dw_kv] + [dw_cg], axis=0)
                perm_parts = [(dil, _by_residue(dh_attn[gi], dil)) for gi, dil in enumerate(DILS) if dil > 1]
                dx, sums = _norm_mod_bwd(rec["x"], [dh_cg, dh_attn[0]], perm_parts, dx, g, sc, "norm_mod_bwd_mixer")
            d_mod[l][sub] = jnp.stack([sums[0], sums[2], gsum[0]])
            d_norm[l][sub] = sums[3]
            if l == 0 and sub == 1:
                after = dx
                if scatter is not None:
                    after = finish_scatter(scatter, range(n_pieces), dx, l + 1)
                    scatter = None
                early = _scatter_start(pieces_of(dW, EARLY), after, "grads_scatter_start_l0_early")
        if l > 0:
            after = dx
            if scatter is not None:
                after = finish_scatter(scatter, range(n_pieces), dx, l + 1)
            scatter = _scatter_start(pieces_of(dW, range(n_pieces)), after, f"grads_scatter_start_l{l}")
        else:
            late = _scatter_start(pieces_of(dW, LATE), dx, "grads_scatter_start_l0_late")
            finish_scatter(early, EARLY, late[3], 0, "_early")
            finish_scatter(late, LATE, dx, 0, "_late")
    grad_x = dx.reshape(1, S, D)
    d_rel = _bias_bwd(dlog, buckets, "bias_bwd")[:, :rel_bias.shape[1]]

    def shard_grad(p, transpose, shape=None):
        rows = [g_piece[l][p] for l in range(depth)]
        if shape is not None:
            rows = [t.reshape(shape) for t in rows]
        return jnp.stack([jnp.transpose(t) if transpose else t for t in rows])

    g_gate = jnp.stack([shard_grad(0, True), shard_grad(1, True)], axis=1)
    g_up = jnp.stack([shard_grad(2, True), shard_grad(3, True)], axis=1)
    g_down = jnp.stack([shard_grad(4, False), shard_grad(5, False)], axis=1)
    g_w_in = shard_grad(6, True)
    g_co = shard_grad(7, False)
    g_wo = shard_grad(8, False)
    g_ao = shard_grad(9, True, (dsh, ATTN_OUT))

    d_mod_flat = jnp.stack([jnp.stack(d_mod[l]) for l in range(depth)]).reshape(-1)
    d_norm_flat = jnp.stack([jnp.stack(d_norm[l]) for l in range(depth)]).reshape(-1)
    d_conv_flat = jnp.stack(d_conv).reshape(-1)
    vec = jnp.concatenate([d_mod_flat, d_norm_flat, d_conv_flat, d_rel.reshape(-1), d_final_g,
                           jnp.broadcast_to(loss_part, (128,))])
    pad = (-vec.size) % 1024
    vec = jnp.concatenate([vec, jnp.zeros((pad,), F32)]).reshape(-1, 128)
    parts = _all_gather(vec, "small_grads_all_gather").reshape(N_DEV, vec.shape[0], 128)
    tot = _sum_rows8(parts, "small_grads_sum").reshape(-1)
    o0 = 0
    g_ada_b = tot[o0:o0 + d_mod_flat.size].reshape(ada_b.shape)
    o0 += d_mod_flat.size
    g_norm_full = tot[o0:o0 + d_norm_flat.size].reshape(depth, 3, D)
    o0 += d_norm_flat.size
    g_conv_full = tot[o0:o0 + d_conv_flat.size].reshape(depth, CONV_K, D)
    o0 += d_conv_flat.size
    g_rel = tot[o0:o0 + rel_bias.size].reshape(rel_bias.shape)
    o0 += rel_bias.size
    g_final = tot[o0:o0 + D]
    o0 += D
    loss = tot[o0]
    g_norm = lax.dynamic_slice_in_dim(g_norm_full, me * dsh, dsh, axis=2)
    g_conv = lax.dynamic_slice_in_dim(g_conv_full, me * dsh, dsh, axis=2)

    dm_all = parts.reshape(N_DEV, -1)[:, :d_mod_flat.size].reshape(N_DEV, depth, N_DEV * n_ada)
    dm_cols = lax.dynamic_slice_in_dim(dm_all, me * n_ada, n_ada, axis=2)
    dm16 = jnp.concatenate([jnp.transpose(dm_cols, (1, 0, 2)), jnp.zeros((depth, N_DEV, n_ada), F32)], axis=1)
    g_ada_w = _ada_bwd(cs16, dm16, "ada_bwd")

    grads = dict(ada_w=g_ada_w, ada_b=g_ada_b, norm_g=g_norm, ffn_w_gate=g_gate, ffn_w_up=g_up,
                 ffn_w_down=g_down, w_in=g_w_in, conv_w=g_conv, w_conv_out=g_co, w_attn_out=g_ao, w_o=g_wo,
                 rel_bias=g_rel, final_g=g_final)
    weights = dict(ada_w=ada_w, ada_b=ada_b, norm_g=norm_g, ffn_w_gate=ffn_w_gate, ffn_w_up=ffn_w_up,
                   ffn_w_down=ffn_w_down, w_in=w_in, conv_w=conv_w, w_conv_out=w_conv_out, w_attn_out=w_attn_out,
                   w_o=w_o, rel_bias=rel_bias, final_g=final_g)
    ms = dict(ada_w=m_ada_w, ada_b=m_ada_b, norm_g=m_norm_g, ffn_w_gate=m_ffn_w_gate, ffn_w_up=m_ffn_w_up,
              ffn_w_down=m_ffn_w_down, w_in=m_w_in, conv_w=m_conv_w, w_conv_out=m_w_conv_out,
              w_attn_out=m_w_attn_out, w_o=m_w_o, rel_bias=m_rel_bias, final_g=m_final_g)
    vs = dict(ada_w=v_ada_w, ada_b=v_ada_b, norm_g=v_norm_g, ffn_w_gate=v_ffn_w_gate, ffn_w_up=v_ffn_w_up,
              ffn_w_down=v_ffn_w_down, w_in=v_w_in, conv_w=v_conv_w, w_conv_out=v_w_conv_out,
              w_attn_out=v_w_attn_out, w_o=v_w_o, rel_bias=v_rel_bias, final_g=v_final_g)
    order = list(weights)
    deltas, new_m, new_v = [], [], []
    for name in order:
        d_, m_, v_ = _adamw(weights[name], grads[name], ms[name], vs[name], "adamw_" + name)
        deltas.append(d_)
        new_m.append(m_)
        new_v.append(v_)
    return (loss, grad_x, *[grads[n] for n in order], *deltas, *new_m, *new_v)
```

```python
import functools

import numpy as np
import jax
import jax.numpy as jnp
from jax import lax
from jax.experimental import pallas as pl
from jax.experimental.pallas import tpu as pltpu

F32 = jnp.float32
BF16 = jnp.bfloat16

N_DEV = 8
HEAD_DIM = 64
HEAD_SHIFT = 6
HEADS_PER_GROUP = 8
DILATION_GROUPS = ((128, 1), (512, 4), (2048, 16))
DILS = tuple(d for _, d in DILATION_GROUPS)
N_GROUPS = len(DILATION_GROUPS)
ATTN_OUT = HEADS_PER_GROUP * HEAD_DIM
QKV_W = N_GROUPS * ATTN_OUT
BLOCK = 128
NUM_BUCKETS = 32
MAX_DISTANCE = 2048
CONV_K = 3
EPS = 1e-6
NEG_INF = -1e30
SCALE = HEAD_DIM ** -0.5

ADAM_LR = 0.001
ADAM_B1 = 0.9
ADAM_B2 = 0.999
ADAM_EPS = 1e-08
ADAM_WD = 0.01
ADAM_STEP = 10

V7X_VMEM_LIMIT = 48 * 1024 * 1024
MESH = pl.DeviceIdType.MESH

NN = (((1,), (0,)), ((), ()))
NT = (((1,), (1,)), ((), ()))
TN = (((0,), (0,)), ((), ()))


def _pick(dim, cands):
    for c in cands:
        if dim % c == 0:
            return c
    return dim


def _pick_k(K, cap=2816):
    if K <= cap or K % 128:
        return K
    best = 128
    for m in range(1, K // 128 + 1):
        if (K // 128) % m == 0 and 128 * m <= cap:
            best = 128 * m
    return best


def _params(sem):
    return pltpu.CompilerParams(dimension_semantics=sem, vmem_limit_bytes=V7X_VMEM_LIMIT)


def _all_gather(x_shard, name):
    m_per, n = x_shard.shape

    def body(x_ref, out_ref, send_sems, recv_sems, local_sem):
        x, y, c = lax.axis_index("x"), lax.axis_index("y"), lax.axis_index("c")
        me, sibling = (x, y, c), (x, y, 1 - c)
        chips = [(1 - x, y), (x, 1 - y), (1 - x, 1 - y)]

        def rows(px, py, pc):
            return out_ref.at[pl.ds((4 * px + 2 * py + pc) * m_per, m_per), :]

        def copy(k, block, to, src=None):
            return pltpu.make_async_remote_copy(
                src_ref=rows(*block) if src is None else src, dst_ref=rows(*block),
                send_sem=send_sems.at[k], recv_sem=recv_sems.at[k], device_id=to, device_id_type=MESH)

        mine = pltpu.make_async_copy(x_ref, rows(*me), local_sem)
        mine.start()
        first = [copy(0, me, sibling, src=x_ref)]
        first += [copy(1 + j, me, (*chip, c), src=x_ref) for j, chip in enumerate(chips)]
        for cp in first:
            cp.start()
        passed = [copy(4 + j, (*chip, c), sibling) for j, chip in enumerate(chips)]
        for j, chip in enumerate(chips):
            copy(1 + j, (*chip, c), me).wait_recv()
            passed[j].start()
        copy(0, sibling, me).wait_recv()
        for j, chip in enumerate(chips):
            copy(4 + j, (*chip, 1 - c), me).wait_recv()
        for cp in first + passed:
            cp.wait_send()
        mine.wait()

    return pl.pallas_call(
        body, name=name,
        out_shape=jax.ShapeDtypeStruct((N_DEV * m_per, n), x_shard.dtype),
        in_specs=[pl.BlockSpec(memory_space=pltpu.VMEM)],
        out_specs=pl.BlockSpec(memory_space=pltpu.VMEM),
        scratch_shapes=[pltpu.SemaphoreType.DMA((7,)), pltpu.SemaphoreType.DMA((7,)), pltpu.SemaphoreType.DMA],
    )(x_shard)


def _offsets(piece_rows):
    offs, o = [], 0
    for n in piece_rows:
        offs.append(o)
        o += n
    return offs


HBM_SPEC = pl.BlockSpec(memory_space=pltpu.HBM)
SEM_SPEC = pl.BlockSpec(memory_space=pltpu.SEMAPHORE)
ANY_SPEC = pl.BlockSpec(memory_space=pl.ANY)
SPLIT_COPY_PARAMS = pltpu.CompilerParams(has_side_effects=pltpu.SideEffectType.DATAFLOW_SIDE_EFFECTING)


def _in_hbm(a):
    return pltpu.with_memory_space_constraint(a, pltpu.HBM)


def _dma_sems(n):
    return [pltpu.SemaphoreType.DMA(())] * n


def _whole(ref, send_sem, recv_sem, me):
    return pltpu.make_async_remote_copy(src_ref=ref, dst_ref=ref, send_sem=send_sem, recv_sem=recv_sem,
                                        device_id=me, device_id_type=MESH)


def _gather_start(packed, piece_rows, after, name):
    R, w = packed.shape
    offs = _offsets(piece_rows)
    P = len(piece_rows)
    assert offs[-1] + piece_rows[-1] == R

    def body(*refs):
        src_ref = refs[0]
        o = refs[P + 2:]
        send, recv = o[0:4], o[4:8]
        zones, token, stage, local_sems = o[9:9 + P], o[9 + P], o[10 + P], o[11 + P]
        x, y, c = lax.axis_index("x"), lax.axis_index("y"), lax.axis_index("c")
        targets = [(x, y, 1 - c), (1 - x, y, c), (x, 1 - y, c), (1 - x, 1 - y, c)]
        me = 4 * x + 2 * y + c

        def piece(p, ref):
            return ref.at[pl.ds(offs[p], piece_rows[p]), :]

        def rows(p):
            return zones[p].at[pl.ds(me * piece_rows[p], piece_rows[p]), :]

        for k, to in enumerate(targets):
            for p in range(P):
                pltpu.make_async_remote_copy(src_ref=piece(p, src_ref), dst_ref=rows(p), send_sem=send[k],
                                             recv_sem=recv[k], device_id=to, device_id_type=MESH).start()
        load = pltpu.make_async_copy(src_ref, stage, local_sems.at[P])
        load.start()
        load.wait()
        mine = [pltpu.make_async_copy(piece(p, stage), rows(p), local_sems.at[p]) for p in range(P)]
        for cp in mine:
            cp.start()
        for cp in mine:
            cp.wait()
        token[...] = jnp.zeros_like(token)

    zones_in = [_in_hbm(lax.empty((N_DEV * n, w), packed.dtype)) for n in piece_rows]
    outs = pl.pallas_call(
        body, name=name,
        out_shape=(*_dma_sems(8), pltpu.HBM((R, w), packed.dtype),
                   *[pltpu.HBM((N_DEV * n, w), packed.dtype) for n in piece_rows],
                   jax.ShapeDtypeStruct((8, 128), F32)),
        in_specs=[HBM_SPEC] * (P + 1) + [ANY_SPEC],
        out_specs=[SEM_SPEC] * 8 + [HBM_SPEC] * (P + 1) + [pl.BlockSpec(memory_space=pltpu.VMEM)],
        input_output_aliases={0: 8, **{1 + p: 9 + p for p in range(P)}},
        scratch_shapes=[pltpu.VMEM((R, w), packed.dtype), pltpu.SemaphoreType.DMA((P + 1,))],
        compiler_params=SPLIT_COPY_PARAMS,
    )(_in_hbm(packed), *zones_in, after)
    return outs[0:8], outs[8], list(outs[9:9 + P]), outs[9 + P]


def _gather_forward(sems, packed, zones, piece_rows, after, name):
    P = len(piece_rows)

    def body(*refs):
        src_ref = refs[0]
        s = refs[1 + P:9 + P]
        o = refs[10 + P:]
        send, recv = s[0:4], s[4:8]
        send2, recv2, zones_o = o[0:3], o[3:6], o[7:7 + P]
        x, y, c = lax.axis_index("x"), lax.axis_index("y"), lax.axis_index("c")
        me = (x, y, c)
        chips = [(1 - x, y), (x, 1 - y), (1 - x, 1 - y)]
        for j, (px, py) in enumerate(chips):
            _whole(src_ref, send[1 + j], recv[1 + j], me).wait_recv()
            blk = 4 * px + 2 * py + c
            for p in range(P):
                r = zones_o[p].at[pl.ds(blk * piece_rows[p], piece_rows[p]), :]
                pltpu.make_async_remote_copy(src_ref=r, dst_ref=r, send_sem=send2[j], recv_sem=recv2[j],
                                             device_id=(x, y, 1 - c), device_id_type=MESH).start()
        _whole(src_ref, send[0], recv[0], me).wait_recv()
        for k in range(4):
            _whole(src_ref, send[k], recv[k], me).wait_send()

    outs = pl.pallas_call(
        body, name=name,
        out_shape=(*_dma_sems(6), pltpu.HBM(packed.shape, packed.dtype),
                   *[pltpu.HBM(z.shape, z.dtype) for z in zones]),
        in_specs=[HBM_SPEC] * (P + 1) + [SEM_SPEC] * 8 + [ANY_SPEC],
        out_specs=[SEM_SPEC] * 6 + [HBM_SPEC] * (P + 1),
        input_output_aliases={0: 6, **{1 + p: 7 + p for p in range(P)}},
        compiler_params=SPLIT_COPY_PARAMS,
    )(packed, *zones, *sems, after)
    return outs[0:6], outs[6], list(outs[7:7 + P])


def _gather_finish(sems2, packed, zones, after, name):
    P = len(zones)

    def body(*refs):
        src_ref = refs[0]
        s = refs[1 + P:7 + P]
        x, y, c = lax.axis_index("x"), lax.axis_index("y"), lax.axis_index("c")
        for j in range(3):
            _whole(src_ref, s[j], s[3 + j], (x, y, c)).wait_recv()
        for j in range(3):
            _whole(src_ref, s[j], s[3 + j], (x, y, c)).wait_send()

    outs = pl.pallas_call(
        body, name=name,
        out_shape=(pltpu.HBM(packed.shape, packed.dtype), *[pltpu.HBM(z.shape, z.dtype) for z in zones]),
        in_specs=[HBM_SPEC] * (P + 1) + [SEM_SPEC] * 6 + [ANY_SPEC],
        out_specs=[HBM_SPEC] * (P + 1),
        input_output_aliases={p: p for p in range(P + 1)},
        compiler_params=SPLIT_COPY_PARAMS,
    )(packed, *zones, *sems2, after)
    return list(outs[1:1 + P])


def _scatter_start(pieces, after, name):
    P = len(pieces)
    w = pieces[0].shape[1]
    piece_rows = [p.shape[0] // N_DEV for p in pieces]
    offs = _offsets(piece_rows)
    R = offs[-1] + piece_rows[-1]

    def body(*refs):
        o = refs[P + 2:]
        send, recv = o[0:7], o[7:14]
        srcs, dst_ref, token, stage, local_sems = o[14:14 + P], o[14 + P], o[15 + P], o[16 + P], o[17 + P]
        x, y, c = lax.axis_index("x"), lax.axis_index("y"), lax.axis_index("c")
        me = 4 * x + 2 * y + c

        def chunk(p, dev):
            return srcs[p].at[pl.ds(dev * piece_rows[p], piece_rows[p]), :]

        def slot(p, dev):
            return dst_ref.at[dev, pl.ds(offs[p], piece_rows[p]), :]

        for k in range(1, N_DEV):
            px = 1 - x if (k >> 2) & 1 else x
            py = 1 - y if (k >> 1) & 1 else y
            pc = 1 - c if k & 1 else c
            peer = 4 * px + 2 * py + pc
            for p in range(P):
                pltpu.make_async_remote_copy(
                    src_ref=chunk(p, peer), dst_ref=slot(p, me), send_sem=send[k - 1], recv_sem=recv[k - 1],
                    device_id=(px, py, pc), device_id_type=MESH).start()
        mine = [pltpu.make_async_copy(chunk(p, me), stage.at[pl.ds(offs[p], piece_rows[p]), :], local_sems.at[p])
                for p in range(P)]
        for cp in mine:
            cp.start()
        for cp in mine:
            cp.wait()
        store = pltpu.make_async_copy(stage, dst_ref.at[me], local_sems.at[P])
        store.start()
        store.wait()
        token[...] = jnp.zeros_like(token)

    dtype = pieces[0].dtype
    outs = pl.pallas_call(
        body, name=name,
        out_shape=(*_dma_sems(14), *[pltpu.HBM(p.shape, dtype) for p in pieces], pltpu.HBM((N_DEV, R, w), dtype),
                   jax.ShapeDtypeStruct((8, 128), F32)),
        in_specs=[HBM_SPEC] * (P + 1) + [ANY_SPEC],
        out_specs=[SEM_SPEC] * 14 + [HBM_SPEC] * (P + 1) + [pl.BlockSpec(memory_space=pltpu.VMEM)],
        input_output_aliases={p: 14 + p for p in range(P + 1)},
        scratch_shapes=[pltpu.VMEM((R, w), dtype), pltpu.SemaphoreType.DMA((P + 1,))],
        compiler_params=SPLIT_COPY_PARAMS,
    )(*[_in_hbm(p) for p in pieces], _in_hbm(lax.empty((N_DEV, R, w), dtype)), after)
    return outs[0:14], list(outs[14:14 + P]), outs[14 + P], outs[15 + P]


def _scatter_finish(sems, pieces, recv, after, name):
    P = len(pieces)
    after = list(after) if isinstance(after, (list, tuple)) else [after]

    def body(*refs):
        dst_ref = refs[P]
        s = refs[P + 1:P + 15]
        x, y, c = lax.axis_index("x"), lax.axis_index("y"), lax.axis_index("c")
        for k in range(7):
            _whole(dst_ref.at[0], s[k], s[7 + k], (x, y, c)).wait_recv()
        for k in range(7):
            _whole(dst_ref.at[0], s[k], s[7 + k], (x, y, c)).wait_send()

    outs = pl.pallas_call(
        body, name=name,
        out_shape=(*[pltpu.HBM(p.shape, p.dtype) for p in pieces], pltpu.HBM(recv.shape, recv.dtype)),
        in_specs=[HBM_SPEC] * (P + 1) + [SEM_SPEC] * 14 + [ANY_SPEC] * len(after),
        out_specs=[HBM_SPEC] * (P + 1),
        input_output_aliases={p: p for p in range(P + 1)},
        compiler_params=SPLIT_COPY_PARAMS,
    )(*pieces, recv, *sems, *after)
    return outs[P]


def _sum_sources(parts, name):
    _, r, n = parts.shape
    tr = _pick(r, [256, 128, 64, 32, 16, 8])

    def kern(p_ref, o_ref):
        acc = p_ref[0].astype(F32)
        for k in range(1, N_DEV):
            acc = acc + p_ref[k].astype(F32)
        o_ref[...] = acc

    return pl.pallas_call(
        kern, name=name, grid=(r // tr,),
        out_shape=jax.ShapeDtypeStruct((r, n), F32),
        in_specs=[pl.BlockSpec((N_DEV, tr, n), lambda i: (0, i, 0))],
        out_specs=pl.BlockSpec((tr, n), lambda i: (i, 0)),
        compiler_params=_params(("parallel",)),
    )(parts)


def _matmul(a, b, mode, out_dtype, name, tm=None, tn=None, tk=None, resid=None):
    if mode == "nn":
        (M, K), N = a.shape, b.shape[1]
    elif mode == "nt":
        (M, K), N = a.shape, b.shape[0]
    else:
        (K, M), N = a.shape, b.shape[1]
    dims = {"nn": NN, "nt": NT, "tn": TN}[mode]
    tm = tm or _pick(M, [1024, 1408, 512, 256, 128])
    tn = tn or _pick(N, [1024, 1408, 512, 256, 128])
    tk = tk or _pick_k(K)
    nk = K // tk
    a_spec = {"nn": pl.BlockSpec((tm, tk), lambda i, j, k: (i, k)),
              "nt": pl.BlockSpec((tm, tk), lambda i, j, k: (i, k)),
              "tn": pl.BlockSpec((tk, tm), lambda i, j, k: (k, i))}[mode]
    b_spec = {"nn": pl.BlockSpec((tk, tn), lambda i, j, k: (k, j)),
              "nt": pl.BlockSpec((tn, tk), lambda i, j, k: (j, k)),
              "tn": pl.BlockSpec((tk, tn), lambda i, j, k: (k, j))}[mode]
    o_spec = pl.BlockSpec((tm, tn), lambda i, j, k: (i, j))
    n_in = 2 if resid is None else 4
    n_out = 1 if resid is None else 2

    def kern(*refs):
        a_ref, b_ref = refs[0], refs[1]
        outs = refs[n_in:n_in + n_out]
        acc_ref = refs[n_in + n_out] if nk > 1 else None

        def finish(acc):
            if resid is None:
                outs[0][...] = acc.astype(out_dtype)
            else:
                x_ref, g_ref = refs[2], refs[3]
                outs[0][...] = x_ref[...] + (resid[2] * g_ref[...]) * acc
                outs[1][...] = acc.astype(out_dtype)

        part = lax.dot_general(a_ref[...], b_ref[...], dims, preferred_element_type=F32)
        if nk == 1:
            finish(part)
        else:
            k = pl.program_id(2)

            @pl.when(k == 0)
            def _():
                acc_ref[...] = part

            @pl.when(k > 0)
            def _():
                acc_ref[...] += part

            @pl.when(k == nk - 1)
            def _():
                finish(acc_ref[...])

    in_specs = [a_spec, b_spec]
    args = [a, b]
    out_shape = [jax.ShapeDtypeStruct((M, N), out_dtype)]
    out_specs = [o_spec]
    if resid is not None:
        in_specs += [o_spec, pl.BlockSpec((1, tn), lambda i, j, k: (0, j))]
        args += [resid[0], resid[1]]
        out_shape = [jax.ShapeDtypeStruct((M, N), F32)] + out_shape
        out_specs = [o_spec, o_spec]
    res = pl.pallas_call(
        kern, name=name, grid=(M // tm, N // tn, nk),
        out_shape=out_shape, in_specs=in_specs, out_specs=out_specs,
        scratch_shapes=[pltpu.VMEM((tm, tn), F32)] if nk > 1 else [],
        compiler_params=_params(("parallel", "parallel", "arbitrary")),
    )(*args)
    return res[0] if resid is None else res


def _dot3(a, b, dims):
    ah = a.astype(BF16)
    al = (a - ah.astype(F32)).astype(BF16)
    bh = b.astype(BF16)
    bl = (b - bh.astype(F32)).astype(BF16)
    d = functools.partial(lax.dot_general, dimension_numbers=dims, preferred_element_type=F32)
    return d(ah, bh) + (d(ah, bl) + d(al, bh))


def _silu_parts(a):
    sg = jax.nn.sigmoid(a)
    return a * sg, sg * (1.0 + a * (1.0 - sg))


def _ffn_up(h, wg_t, wu_t, name):
    S, D = h.shape
    F = wg_t.shape[0]
    tm = _pick(S, [512, 256, 128])
    tn = _pick(F, [1408, 512, 256, 128])

    def kern(h_ref, g_ref, u_ref, a_out, u_out, z_out):
        hv = h_ref[...]
        a = lax.dot_general(hv, g_ref[...], NT, preferred_element_type=F32)
        u = lax.dot_general(hv, u_ref[...], NT, preferred_element_type=F32)
        a_out[...] = a.astype(BF16)
        u_out[...] = u.astype(BF16)
        z_out[...] = (_silu_parts(a)[0] * u).astype(BF16)

    w_spec = pl.BlockSpec((tn, D), lambda j, i: (j, 0))
    o_spec = pl.BlockSpec((tm, tn), lambda j, i: (i, j))
    return pl.pallas_call(
        kern, name=name, grid=(F // tn, S // tm),
        out_shape=[jax.ShapeDtypeStruct((S, F), BF16)] * 3,
        in_specs=[pl.BlockSpec((tm, D), lambda j, i: (i, 0)), w_spec, w_spec],
        out_specs=[o_spec] * 3,
        compiler_params=_params(("parallel", "parallel")),
    )(h, wg_t, wu_t)


def _ffn_up_bwd(dz, a, u, wg_t, wu_t, name):
    S, F = dz.shape
    D = wg_t.shape[1]
    tm = _pick(S, [512, 256, 128])
    tk = _pick(F, [1408, 512, 256, 128])
    nk = F // tk

    def kern(dz_ref, a_ref, u_ref, g_ref, w_ref, da_out, du_out, dh_out, acc_ref):
        k = pl.program_id(1)
        av = a_ref[...].astype(F32)
        uv = u_ref[...].astype(F32)
        dzv = dz_ref[...].astype(F32)
        silu, dsilu = _silu_parts(av)
        da = (dzv * uv * dsilu).astype(BF16)
        du = (dzv * silu).astype(BF16)
        da_out[...] = da
        du_out[...] = du
        part = (lax.dot_general(da, g_ref[...], NN, preferred_element_type=F32)
                + lax.dot_general(du, w_ref[...], NN, preferred_element_type=F32))

        @pl.when(k == 0)
        def _():
            acc_ref[...] = part

        @pl.when(k > 0)
        def _():
            acc_ref[...] += part

        @pl.when(k == nk - 1)
        def _():
            dh_out[...] = acc_ref[...]

    t_spec = pl.BlockSpec((tm, tk), lambda i, k: (i, k))
    w_spec = pl.BlockSpec((tk, D), lambda i, k: (k, 0))
    return pl.pallas_call(
        kern, name=name, grid=(S // tm, nk),
        out_shape=[jax.ShapeDtypeStruct((S, F), BF16)] * 2 + [jax.ShapeDtypeStruct((S, D), F32)],
        in_specs=[t_spec, t_spec, t_spec, w_spec, w_spec],
        out_specs=[t_spec, t_spec, pl.BlockSpec((tm, D), lambda i, k: (i, 0))],
        scratch_shapes=[pltpu.VMEM((tm, D), F32)],
        compiler_params=_params(("parallel", "arbitrary")),
    )(dz, a, u, wg_t, wu_t)


def _attn_dh(dq, dkv, w_t, name):
    S = dq.shape[0]
    D = w_t.shape[1]
    tm = _pick(S, [1024, 512, 256, 128])

    def kern(dq_ref, dk_ref, dv_ref, wq_ref, wk_ref, wv_ref, o_ref):
        o_ref[...] = (lax.dot_general(dq_ref[...], wq_ref[...], NN, preferred_element_type=F32)
                      + lax.dot_general(dk_ref[...], wk_ref[...], NN, preferred_element_type=F32)
                      + lax.dot_general(dv_ref[...], wv_ref[...], NN, preferred_element_type=F32))

    def w_blk(j):
        return pl.BlockSpec((ATTN_OUT, D), lambda i: (j, 0))

    return pl.pallas_call(
        kern, name=name, grid=(S // tm,),
        out_shape=jax.ShapeDtypeStruct((S, D), F32),
        in_specs=[pl.BlockSpec((tm, ATTN_OUT), lambda i: (i, 0)), pl.BlockSpec((tm, ATTN_OUT), lambda i: (i, 0)),
                  pl.BlockSpec((tm, ATTN_OUT), lambda i: (i, 1)), w_blk(0), w_blk(1), w_blk(2)],
        out_specs=pl.BlockSpec((tm, D), lambda i: (i, 0)),
        compiler_params=_params(("parallel",)),
    )(dq, dkv, dkv, w_t, w_t, w_t)


def _row_spec(tm, d):
    return pl.BlockSpec((tm, d), lambda i: (i, 0))


def _vec_spec(d, rows=1):
    return pl.BlockSpec((rows, d), lambda i: (0, 0))


def _perm_spec(dil, tm, w):
    return pl.BlockSpec((dil, tm // dil, w), lambda i: (0, i, 0))


def _stage_shape(tm, w):
    return pltpu.VMEM((w // 128, tm, 128), F32)


def _stage(scr, val):
    for ci in range(scr.shape[0]):
        scr[ci] = val[:, 128 * ci:128 * (ci + 1)]


def _unstage(scr):
    return jnp.concatenate([scr[ci] for ci in range(scr.shape[0])], axis=1)


def _get_residue(scr, res, dil):
    n = scr.shape[1] // dil
    return jnp.concatenate([scr[ci, pl.ds(res, n, stride=dil), :] for ci in range(scr.shape[0])], axis=1)


def _put_residue(scr, res, dil, val):
    n = scr.shape[1] // dil
    for ci in range(scr.shape[0]):
        scr[ci, pl.ds(res, n, stride=dil), :] = val[:, 128 * ci:128 * (ci + 1)]


def _norm_mod_fwd(x, g, s, b, name, dils=()):
    S, D = x.shape
    tm = _pick(S, [512, 256, 128])

    def kern(x_ref, g_ref, s_ref, b_ref, h_ref, *rest):
        xv = x_ref[...]
        r = lax.rsqrt(jnp.mean(xv * xv, axis=1, keepdims=True) + EPS)
        hv = xv * r * g_ref[...] * (1.0 + s_ref[...]) + b_ref[...]
        h_ref[...] = hv.astype(BF16)
        if dils:
            scr = rest[len(dils)]
            _stage(scr, hv)
            for dil, p_ref in zip(dils, rest[:len(dils)]):
                for res in range(dil):
                    p_ref[res] = _get_residue(scr, res, dil).astype(BF16)

    return pl.pallas_call(
        kern, name=name, grid=(S // tm,),
        out_shape=[jax.ShapeDtypeStruct((S, D), BF16)] + [jax.ShapeDtypeStruct((dil, S // dil, D), BF16) for dil in dils],
        in_specs=[_row_spec(tm, D), _vec_spec(D), _vec_spec(D), _vec_spec(D)],
        out_specs=[_row_spec(tm, D)] + [_perm_spec(dil, tm, D) for dil in dils],
        scratch_shapes=[_stage_shape(tm, D)] if dils else [],
        compiler_params=_params(("parallel",)),
    )(x, g, s, b)


def _norm_mod_bwd(x, dh_nat, dh_perm, dxo, g, s, name):
    S, D = x.shape
    tm = _pick(S, [512, 256, 128])
    n = S // tm
    n_nat, n_perm = len(dh_nat), len(dh_perm)

    def kern(*refs):
        x_ref = refs[0]
        nat = refs[1:1 + n_nat]
        perm = refs[1 + n_nat:1 + n_nat + n_perm]
        dxo_ref, g_ref, s_ref, dx_ref, cs_ref = refs[1 + n_nat + n_perm:6 + n_nat + n_perm]
        scr = refs[6 + n_nat + n_perm:]
        i = pl.program_id(0)
        xv = x_ref[...]
        r = lax.rsqrt(jnp.mean(xv * xv, axis=1, keepdims=True) + EPS)
        xn = xv * r
        dh_v = nat[0][...].astype(F32)
        for t in nat[1:]:
            dh_v = dh_v + t[...].astype(F32)
        for (dil, _), p_ref, sc in zip(dh_perm, perm, scr):
            for res in range(dil):
                _put_residue(sc, res, dil, p_ref[res])
            dh_v = dh_v + _unstage(sc)
        one_s = 1.0 + s_ref[...]
        dxn = dh_v * (g_ref[...] * one_s)
        dx_ref[...] = dxo_ref[...] + r * (dxn - xn * jnp.mean(xn * dxn, axis=1, keepdims=True))

        @pl.when(i == 0)
        def _():
            cs_ref[...] = jnp.zeros_like(cs_ref)

        cs_ref[0:1, :] += jnp.sum(dh_v, axis=0, keepdims=True)
        cs_ref[1:2, :] += jnp.sum(dh_v * xn, axis=0, keepdims=True)

        @pl.when(i == n - 1)
        def _():
            t = cs_ref[1:2, :]
            cs_ref[2:3, :] = g_ref[...] * t
            cs_ref[3:4, :] = one_s * t

    return pl.pallas_call(
        kern, name=name, grid=(n,),
        out_shape=[jax.ShapeDtypeStruct((S, D), F32), jax.ShapeDtypeStruct((8, D), F32)],
        in_specs=[_row_spec(tm, D)] + [_row_spec(tm, D)] * n_nat + [_perm_spec(dil, tm, D) for dil, _ in dh_perm]
        + [_row_spec(tm, D), _vec_spec(D), _vec_spec(D)],
        out_specs=[_row_spec(tm, D), _vec_spec(D, 8)],
        scratch_shapes=[_stage_shape(tm, D) for _ in dh_perm],
        compiler_params=_params(("arbitrary",)),
    )(x, *dh_nat, *[a for _, a in dh_perm], dxo, g, s)


def _gate_bwd(dxo, f, gate, coef, name):
    S, D = dxo.shape
    tm = _pick(S, [512, 256, 128])

    def kern(dxo_ref, f_ref, gate_ref, df_ref, cs_ref):
        i = pl.program_id(0)
        dv = dxo_ref[...]
        df_ref[...] = ((coef * gate_ref[...]) * dv).astype(BF16)

        @pl.when(i == 0)
        def _():
            cs_ref[...] = jnp.zeros_like(cs_ref)

        cs_ref[0:1, :] += coef * jnp.sum(f_ref[...].astype(F32) * dv, axis=0, keepdims=True)

    return pl.pallas_call(
        kern, name=name, grid=(S // tm,),
        out_shape=[jax.ShapeDtypeStruct((S, D), BF16), jax.ShapeDtypeStruct((8, D), F32)],
        in_specs=[_row_spec(tm, D), _row_spec(tm, D), _vec_spec(D)],
        out_specs=[_row_spec(tm, D), _vec_spec(D, 8)],
        compiler_params=_params(("arbitrary",)),
    )(dxo, f, gate)


def _loss_head(x, g, target, name):
    S, D = x.shape
    tm = _pick(S, [512, 256, 128])
    n = S // tm

    def kern(x_ref, g_ref, t_ref, dx_ref, cs_ref):
        i = pl.program_id(0)
        xv = x_ref[...]
        r = lax.rsqrt(jnp.mean(xv * xv, axis=1, keepdims=True) + EPS)
        xn = xv * r
        e = xn * g_ref[...] - t_ref[...]
        dxn = (e * (1.0 / D)) * g_ref[...]
        dx_ref[...] = r * (dxn - xn * jnp.mean(xn * dxn, axis=1, keepdims=True))

        @pl.when(i == 0)
        def _():
            cs_ref[...] = jnp.zeros_like(cs_ref)

        cs_ref[0:1, :] += jnp.sum(xn * e, axis=0, keepdims=True) * (1.0 / D)
        cs_ref[1:2, :] += jnp.sum(e * e, axis=0, keepdims=True)

        @pl.when(i == n - 1)
        def _():
            tot = jnp.sum(cs_ref[1:2, :], axis=1, keepdims=True) * (0.5 / D)
            cs_ref[2:3, :] = jnp.broadcast_to(tot, (1, D))

    return pl.pallas_call(
        kern, name=name, grid=(n,),
        out_shape=[jax.ShapeDtypeStruct((S, D), F32), jax.ShapeDtypeStruct((8, D), F32)],
        in_specs=[_row_spec(tm, D), _vec_spec(D), _row_spec(tm, D)],
        out_specs=[_row_spec(tm, D), _vec_spec(D, 8)],
        compiler_params=_params(("arbitrary",)),
    )(x, g, target)


def _shift_down(p, row, prev_rows):
    a, b = prev_rows
    p1 = jnp.where(row == 0, b, pltpu.roll(p, 1, 0))
    p2 = jnp.where(row == 0, a, jnp.where(row == 1, b, pltpu.roll(p, 2, 0)))
    return p1, p2


def _conv_fwd(cg, conv_w, name):
    S, D5 = cg.shape
    D = D5 // 5
    tm = _pick(S, [512, 256, 128])
    t8 = tm // 8

    def prev(col):
        return pl.BlockSpec((8, D), lambda i: (jnp.maximum(i * t8 - 1, 0), col))

    def kern(cb_ref, cc_ref, ch_ref, ccp_ref, chp_ref, w_ref, y_ref):
        i = pl.program_id(0)
        keep = jnp.where(i > 0, 1.0, 0.0)
        p = cc_ref[...].astype(F32) * ch_ref[...].astype(F32)
        pa = ccp_ref[6:7, :].astype(F32) * chp_ref[6:7, :].astype(F32) * keep
        pb = ccp_ref[7:8, :].astype(F32) * chp_ref[7:8, :].astype(F32) * keep
        row = lax.broadcasted_iota(jnp.int32, (tm, D), 0)
        p1, p2 = _shift_down(p, row, (pa, pb))
        dw = w_ref[0:1, :] * p2 + w_ref[1:2, :] * p1 + w_ref[2:3, :] * p
        y_ref[...] = (cb_ref[...].astype(F32) * dw).astype(BF16)

    def col(cidx):
        return pl.BlockSpec((tm, D), lambda i: (i, cidx))

    return pl.pallas_call(
        kern, name=name, grid=(S // tm,),
        out_shape=jax.ShapeDtypeStruct((S, D), BF16),
        in_specs=[col(0), col(1), col(2), prev(1), prev(2), _vec_spec(D, CONV_K)],
        out_specs=_row_spec(tm, D),
        compiler_params=_params(("parallel",)),
    )(cg, cg, cg, cg, cg, conv_w)


def _conv_bwd(cg, dy, dgg, conv_w, name):
    S, D5 = cg.shape
    D = D5 // 5
    tm = _pick(S, [512, 256, 128])
    t8 = tm // 8
    n = S // tm
    last8 = S // 8 - 1

    def prev(col):
        return pl.BlockSpec((8, D), lambda i: (jnp.maximum(i * t8 - 1, 0), col))

    def nxt(col):
        return pl.BlockSpec((8, D), lambda i: (jnp.minimum((i + 1) * t8, last8), col))

    def kern(cb_ref, cc_ref, ch_ref, dy_ref, dgg_ref, ccp_ref, chp_ref, cbn_ref, dyn_ref, w_ref, d_ref, cs_ref):
        i = pl.program_id(0)
        keep_p = jnp.where(i > 0, 1.0, 0.0)
        keep_n = jnp.where(i < n - 1, 1.0, 0.0)
        cb = cb_ref[...].astype(F32)
        cc = cc_ref[...].astype(F32)
        ch = ch_ref[...].astype(F32)
        dyv = dy_ref[...].astype(F32)
        p = cc * ch
        pa = ccp_ref[6:7, :].astype(F32) * chp_ref[6:7, :].astype(F32) * keep_p
        pb = ccp_ref[7:8, :].astype(F32) * chp_ref[7:8, :].astype(F32) * keep_p
        row = lax.broadcasted_iota(jnp.int32, (tm, D), 0)
        p1, p2 = _shift_down(p, row, (pa, pb))
        w0, w1, w2 = w_ref[0:1, :], w_ref[1:2, :], w_ref[2:3, :]
        dw = w0 * p2 + w1 * p1 + w2 * p
        ddw = dyv * cb
        na = dyn_ref[0:1, :].astype(F32) * cbn_ref[0:1, :].astype(F32) * keep_n
        nb = dyn_ref[1:2, :].astype(F32) * cbn_ref[1:2, :].astype(F32) * keep_n
        u1 = jnp.where(row == tm - 1, na, pltpu.roll(ddw, tm - 1, 0))
        u2 = jnp.where(row == tm - 2, na, jnp.where(row == tm - 1, nb, pltpu.roll(ddw, tm - 2, 0)))
        dp = w2 * ddw + w1 * u1 + w0 * u2
        d_ref[:, 0:D] = (dyv * dw).astype(BF16)
        d_ref[:, D:2 * D] = (dp * ch).astype(BF16)
        d_ref[:, 2 * D:3 * D] = (dp * cc).astype(BF16)
        d_ref[:, 3 * D:5 * D] = dgg_ref[...]

        @pl.when(i == 0)
        def _():
            cs_ref[...] = jnp.zeros_like(cs_ref)

        cs_ref[0:1, :] += jnp.sum(ddw * p2, axis=0, keepdims=True)
        cs_ref[1:2, :] += jnp.sum(ddw * p1, axis=0, keepdims=True)
        cs_ref[2:3, :] += jnp.sum(ddw * p, axis=0, keepdims=True)

    def col(cidx):
        return pl.BlockSpec((tm, D), lambda i: (i, cidx))

    return pl.pallas_call(
        kern, name=name, grid=(n,),
        out_shape=[jax.ShapeDtypeStruct((S, 5 * D), BF16), jax.ShapeDtypeStruct((8, D), F32)],
        in_specs=[col(0), col(1), col(2), _row_spec(tm, D), _row_spec(tm, 2 * D), prev(1), prev(2), nxt(0),
                  pl.BlockSpec((8, D), lambda i: (jnp.minimum((i + 1) * t8, last8), 0)), _vec_spec(D, CONV_K)],
        out_specs=[_row_spec(tm, 5 * D), _vec_spec(D, 8)],
        compiler_params=_params(("arbitrary",)),
    )(cg, cg, cg, dy, dgg, cg, cg, cg, dy, conv_w)


def _merge_fwd(cg, yc, ya, name):
    S, D = yc.shape
    tm = _pick(S, [512, 256, 128])

    def kern(gc_ref, ga_ref, yc_ref, ya_ref, m_ref):
        m_ref[...] = (jax.nn.sigmoid(gc_ref[...].astype(F32)) * yc_ref[...].astype(F32)
                      + jax.nn.sigmoid(ga_ref[...].astype(F32)) * ya_ref[...].astype(F32)).astype(BF16)

    return pl.pallas_call(
        kern, name=name, grid=(S // tm,),
        out_shape=jax.ShapeDtypeStruct((S, D), BF16),
        in_specs=[pl.BlockSpec((tm, D), lambda i: (i, 3)), pl.BlockSpec((tm, D), lambda i: (i, 4)),
                  _row_spec(tm, D), _row_spec(tm, D)],
        out_specs=_row_spec(tm, D),
        compiler_params=_params(("parallel",)),
    )(cg, cg, yc, ya)


def _merge_bwd(cg, yc, ya, dm, name):
    S, D = yc.shape
    tm = _pick(S, [512, 256, 128])

    def kern(gc_ref, ga_ref, yc_ref, ya_ref, dm_ref, dyc_ref, dya_ref, dg_ref):
        dmv = dm_ref[...].astype(F32)
        sc = jax.nn.sigmoid(gc_ref[...].astype(F32))
        sa = jax.nn.sigmoid(ga_ref[...].astype(F32))
        dyc_ref[...] = (dmv * sc).astype(BF16)
        dya_ref[...] = (dmv * sa).astype(BF16)
        dg_ref[:, 0:D] = (dmv * yc_ref[...].astype(F32) * (sc * (1.0 - sc))).astype(BF16)
        dg_ref[:, D:2 * D] = (dmv * ya_ref[...].astype(F32) * (sa * (1.0 - sa))).astype(BF16)

    return pl.pallas_call(
        kern, name=name, grid=(S // tm,),
        out_shape=[jax.ShapeDtypeStruct((S, D), BF16), jax.ShapeDtypeStruct((S, D), BF16),
                   jax.ShapeDtypeStruct((S, 2 * D), BF16)],
        in_specs=[pl.BlockSpec((tm, D), lambda i: (i, 3)), pl.BlockSpec((tm, D), lambda i: (i, 4)),
                  _row_spec(tm, D), _row_spec(tm, D), _row_spec(tm, D)],
        out_specs=[_row_spec(tm, D), _row_spec(tm, D), pl.BlockSpec((tm, 2 * D), lambda i: (i, 0))],
        compiler_params=_params(("parallel",)),
    )(cg, cg, yc, ya, dm)


def _t5_bucket(dist):
    exact = NUM_BUCKETS // 2
    d = np.maximum(dist, 1).astype(np.float32)
    large = exact + (np.log(d / exact) / np.log(MAX_DISTANCE / exact) * (NUM_BUCKETS - exact)).astype(np.int32)
    large = np.minimum(large, NUM_BUCKETS - 1)
    return np.where(dist < exact, dist, large).astype(np.int32)


def _bucket_tables():
    i = np.arange(BLOCK)[:, None]
    j = np.arange(2 * BLOCK)[None, :]
    rel = i - j + BLOCK
    return np.stack([_t5_bucket(np.maximum(rel, 0) * d) for _, d in DILATION_GROUPS]).astype(np.int32)


def _band_masks():
    i = lax.broadcasted_iota(jnp.int32, (BLOCK, 2 * BLOCK), 0)
    j = lax.broadcasted_iota(jnp.int32, (BLOCK, 2 * BLOCK), 1)
    rel = i - j + BLOCK
    band = (rel >= 0) & (rel <= BLOCK)
    return band, band & (j >= BLOCK)


def _bias_build(rel_bias, buckets, name):
    def kern(rb_ref, bk_ref, o_ref):
        g = pl.program_id(0)
        bk = bk_ref[0]
        band, first = _band_masks()
        for h in range(HEADS_PER_GROUP):
            acc = jnp.zeros((BLOCK, 2 * BLOCK), F32)
            for b in range(NUM_BUCKETS):
                acc = jnp.where(bk == b, rb_ref[b, g * HEADS_PER_GROUP + h], acc)
            o_ref[0, 0, h] = jnp.where(first, acc, NEG_INF)
            o_ref[0, 1, h] = jnp.where(band, acc, NEG_INF)

    return pl.pallas_call(
        kern, name=name, grid=(N_GROUPS,),
        out_shape=jax.ShapeDtypeStruct((N_GROUPS, 2, HEADS_PER_GROUP, BLOCK, 2 * BLOCK), F32),
        in_specs=[pl.BlockSpec(memory_space=pltpu.SMEM),
                  pl.BlockSpec((1, BLOCK, 2 * BLOCK), lambda g: (g, 0, 0))],
        out_specs=pl.BlockSpec((1, 2, HEADS_PER_GROUP, BLOCK, 2 * BLOCK), lambda g: (g, 0, 0, 0, 0)),
        compiler_params=_params(("parallel",)),
    )(rel_bias, buckets)


def _bias_bwd(dlog, buckets, name):
    def kern(dl_ref, bk_ref, o_ref):
        g = pl.program_id(0)
        bk = bk_ref[0]
        rowi = lax.broadcasted_iota(jnp.int32, (NUM_BUCKETS, 128), 0)
        coli = lax.broadcasted_iota(jnp.int32, (NUM_BUCKETS, 128), 1)

        @pl.when(g == 0)
        def _():
            o_ref[...] = jnp.zeros_like(o_ref)

        acc = jnp.zeros((NUM_BUCKETS, 128), F32)
        for h in range(HEADS_PER_GROUP):
            dv = dl_ref[0, h]
            for b in range(NUM_BUCKETS):
                t = jnp.sum(jnp.where(bk == b, dv, 0.0), axis=0, keepdims=True)
                t = jnp.sum(t, axis=1, keepdims=True)
                acc = acc + jnp.where((rowi == b) & (coli == g * HEADS_PER_GROUP + h), t, 0.0)
        o_ref[...] += acc

    return pl.pallas_call(
        kern, name=name, grid=(N_GROUPS,),
        out_shape=jax.ShapeDtypeStruct((NUM_BUCKETS, 128), F32),
        in_specs=[pl.BlockSpec((1, HEADS_PER_GROUP, BLOCK, 2 * BLOCK), lambda g: (g, 0, 0, 0)),
                  pl.BlockSpec((1, BLOCK, 2 * BLOCK), lambda g: (g, 0, 0))],
        out_specs=pl.BlockSpec((NUM_BUCKETS, 128), lambda g: (0, 0)),
        compiler_params=_params(("arbitrary",)),
    )(dlog, buckets)


def _head_masks():
    lane = lax.broadcasted_iota(jnp.int32, (BLOCK, 128), 1)
    lo = lane < HEAD_DIM
    return lo, jnp.logical_not(lo)


def _attn_fwd(qkv, bias, d, name):
    S = qkv.shape[0]
    nb = S // d // BLOCK

    def kern(q_ref, kp_ref, kc_ref, vp_ref, vc_ref, b_ref, o_ref, lse_ref):
        lo, hi = _head_masks()
        for p in range(HEADS_PER_GROUP // 2):
            sl = slice(128 * p, 128 * (p + 1))
            q = q_ref[:, sl]
            k = jnp.concatenate([kp_ref[:, sl], kc_ref[:, sl]], axis=0)
            v = jnp.concatenate([vp_ref[:, sl], vc_ref[:, sl]], axis=0)
            o2, l2 = [], []
            for hh, msk in enumerate((lo, hi)):
                qm = jnp.where(msk, q, jnp.zeros_like(q))
                s = lax.dot_general(qm, k, NT, preferred_element_type=F32) * SCALE + b_ref[0, 2 * p + hh]
                m = jnp.max(s, axis=1, keepdims=True)
                e = jnp.exp(s - m)
                l = jnp.sum(e, axis=1, keepdims=True)
                o2.append(lax.dot_general(e.astype(BF16), v, NN, preferred_element_type=F32) / l)
                l2.append(jnp.broadcast_to(m + jnp.log(l), (BLOCK, 128)))
            o_ref[:, sl] = jnp.where(lo, o2[0], o2[1])
            lse_ref[:, sl] = jnp.where(lo, l2[0], l2[1])

    def blk(col, prev):
        if prev:
            return pl.BlockSpec((BLOCK, ATTN_OUT), lambda r, n: (r * nb + jnp.maximum(n - 1, 0), col))
        return pl.BlockSpec((BLOCK, ATTN_OUT), lambda r, n: (r * nb + n, col))

    o_spec = pl.BlockSpec((BLOCK, ATTN_OUT), lambda r, n: (r * nb + n, 0))
    return pl.pallas_call(
        kern, name=name, grid=(d, nb),
        out_shape=[jax.ShapeDtypeStruct((S, ATTN_OUT), F32)] * 2,
        in_specs=[blk(0, False), blk(1, True), blk(1, False), blk(2, True), blk(2, False),
                  pl.BlockSpec((1, HEADS_PER_GROUP, BLOCK, 2 * BLOCK), lambda r, n: (jnp.minimum(n, 1), 0, 0, 0))],
        out_specs=[o_spec, o_spec],
        compiler_params=_params(("parallel", "arbitrary")),
    )(qkv, qkv, qkv, qkv, qkv, bias)


def _attn_bwd(qkv, do, lse, delta, bias, d, name):
    S = qkv.shape[0]
    nb = S // d // BLOCK
    low = -3.0e38

    def kern(q_ref, kp_ref, kc_ref, vp_ref, vc_ref, do_ref, lse_ref, dl_ref, b_ref,
             dq_ref, dkv_ref, db_ref, ck_ref, cv_ref):
        r, n = pl.program_id(0), pl.program_id(1)

        @pl.when((r == 0) & (n == 0))
        def _():
            db_ref[...] = jnp.zeros_like(db_ref)

        @pl.when(n == 0)
        def _():
            ck_ref[...] = jnp.zeros_like(ck_ref)
            cv_ref[...] = jnp.zeros_like(cv_ref)

        @pl.when(n < nb)
        def _():
            lo, hi = _head_masks()
            for p in range(HEADS_PER_GROUP // 2):
                sl = slice(128 * p, 128 * (p + 1))
                sv = slice(ATTN_OUT + 128 * p, ATTN_OUT + 128 * (p + 1))
                q = q_ref[:, sl]
                k = jnp.concatenate([kp_ref[:, sl], kc_ref[:, sl]], axis=0)
                v = jnp.concatenate([vp_ref[:, sl], vc_ref[:, sl]], axis=0)
                dov = do_ref[:, sl]
                lse_b = lse_ref[:, sl]
                del_b = dl_ref[:, sl]
                dq2 = []
                dk_acc = jnp.zeros((2 * BLOCK, 128), F32)
                dv_acc = jnp.zeros((2 * BLOCK, 128), F32)
                for hh, msk in enumerate((lo, hi)):
                    qm = jnp.where(msk, q, jnp.zeros_like(q))
                    dom = jnp.where(msk, dov, jnp.zeros_like(dov))
                    lse_h = jnp.max(jnp.where(msk, lse_b, low), axis=1, keepdims=True)
                    del_h = jnp.max(jnp.where(msk, del_b, low), axis=1, keepdims=True)
                    s = lax.dot_general(qm, k, NT, preferred_element_type=F32) * SCALE + b_ref[0, 2 * p + hh]
                    pr = jnp.exp(s - lse_h)
                    dp = lax.dot_general(dom, v, NT, preferred_element_type=F32)
                    ds = pr * (dp - del_h)
                    db_ref[2 * p + hh] += ds
                    dsb = (ds * SCALE).astype(BF16)
                    dq2.append(lax.dot_general(dsb, k, NN, preferred_element_type=F32))
                    dk_acc = dk_acc + lax.dot_general(dsb, qm, TN, preferred_element_type=F32)
                    dv_acc = dv_acc + lax.dot_general(pr.astype(BF16), dom, TN, preferred_element_type=F32)
                dq_ref[:, sl] = jnp.where(lo, dq2[0], dq2[1]).astype(BF16)
                dkv_ref[:, sl] = (ck_ref[:, sl] + dk_acc[0:BLOCK]).astype(BF16)
                dkv_ref[:, sv] = (cv_ref[:, sl] + dv_acc[0:BLOCK]).astype(BF16)
                ck_ref[:, sl] = dk_acc[BLOCK:2 * BLOCK]
                cv_ref[:, sl] = dv_acc[BLOCK:2 * BLOCK]

        @pl.when(n == nb)
        def _():
            dkv_ref[:, 0:ATTN_OUT] = ck_ref[...].astype(BF16)
            dkv_ref[:, ATTN_OUT:2 * ATTN_OUT] = cv_ref[...].astype(BF16)

    def cur(n):
        return jnp.minimum(n, nb - 1)

    def blk(col, prev):
        if prev:
            return pl.BlockSpec((BLOCK, ATTN_OUT), lambda r, n: (r * nb + jnp.maximum(cur(n) - 1, 0), col))
        return pl.BlockSpec((BLOCK, ATTN_OUT), lambda r, n: (r * nb + cur(n), col))

    q_like = pl.BlockSpec((BLOCK, ATTN_OUT), lambda r, n: (r * nb + cur(n), 0))
    return pl.pallas_call(
        kern, name=name, grid=(d, nb + 1),
        out_shape=[jax.ShapeDtypeStruct((S, ATTN_OUT), BF16), jax.ShapeDtypeStruct((S, 2 * ATTN_OUT), BF16),
                   jax.ShapeDtypeStruct((HEADS_PER_GROUP, BLOCK, 2 * BLOCK), F32)],
        in_specs=[blk(0, False), blk(1, True), blk(1, False), blk(2, True), blk(2, False),
                  q_like, q_like, q_like,
                  pl.BlockSpec((1, HEADS_PER_GROUP, BLOCK, 2 * BLOCK),
                               lambda r, n: (jnp.minimum(cur(n), 1), 0, 0, 0))],
        out_specs=[q_like,
                   pl.BlockSpec((BLOCK, 2 * ATTN_OUT), lambda r, n: (r * nb + jnp.maximum(n - 1, 0), 0)),
                   pl.BlockSpec((HEADS_PER_GROUP, BLOCK, 2 * BLOCK), lambda r, n: (0, 0, 0))],
        scratch_shapes=[pltpu.VMEM((BLOCK, ATTN_OUT), F32), pltpu.VMEM((BLOCK, ATTN_OUT), F32)],
        compiler_params=_params(("arbitrary", "arbitrary")),
    )(qkv, qkv, qkv, qkv, qkv, do, lse, delta, bias)


def _by_residue(a, dil):
    return a if dil == 1 else a.reshape(dil, a.shape[0] // dil, a.shape[1])


def _flat(a):
    return a if a.ndim == 2 else a.reshape(a.shape[0] * a.shape[1], a.shape[2])


def _combine_fwd(os_, lses, name):
    S, W = os_[0].shape
    tm = _pick(S, [512, 256, 128])
    perm = [dil for dil in DILS if dil > 1]

    def kern(*refs):
        o_in, l_in = refs[0:N_GROUPS], refs[N_GROUPS:2 * N_GROUPS]
        of_ref, ob_ref, lse_ref = refs[2 * N_GROUPS:2 * N_GROUPS + 3]
        lse_p = refs[2 * N_GROUPS + 3:2 * N_GROUPS + 3 + len(perm)]
        scr = refs[2 * N_GROUPS + 3 + len(perm):]
        ov, lv = [], []
        si = 0
        for g, dil in enumerate(DILS):
            if dil == 1:
                ov.append(o_in[g][...])
                lv.append(l_in[g][...])
            else:
                so, sl = scr[si], scr[si + 1]
                si += 2
                for res in range(dil):
                    _put_residue(so, res, dil, o_in[g][res])
                    _put_residue(sl, res, dil, l_in[g][res])
                ov.append(_unstage(so))
                lv.append(_unstage(sl))
        m = jnp.maximum(jnp.maximum(lv[0], lv[1]), lv[2])
        e = [jnp.exp(t - m) for t in lv]
        tot = e[0] + e[1] + e[2]
        o = (e[0] * ov[0] + e[1] * ov[1] + e[2] * ov[2]) / tot
        lse = m + jnp.log(tot)
        of_ref[...] = o
        ob_ref[...] = o.astype(BF16)
        lse_ref[...] = lse
        sl = scr[1]
        _stage(sl, lse)
        for dil, p_ref in zip(perm, lse_p):
            for res in range(dil):
                p_ref[res] = _get_residue(sl, res, dil)

    def in_spec(dil):
        return _row_spec(tm, W) if dil == 1 else _perm_spec(dil, tm, W)

    ins = [_by_residue(a, dil) for a, dil in zip(os_, DILS)] + [_by_residue(a, dil) for a, dil in zip(lses, DILS)]
    return pl.pallas_call(
        kern, name=name, grid=(S // tm,),
        out_shape=[jax.ShapeDtypeStruct((S, W), F32), jax.ShapeDtypeStruct((S, W), BF16),
                   jax.ShapeDtypeStruct((S, W), F32)]
        + [jax.ShapeDtypeStruct((dil, S // dil, W), F32) for dil in perm],
        in_specs=[in_spec(dil) for dil in DILS] * 2,
        out_specs=[_row_spec(tm, W)] * 3 + [_perm_spec(dil, tm, W) for dil in perm],
        scratch_shapes=[_stage_shape(tm, W) for _ in range(2 * len(perm))],
        compiler_params=_params(("parallel",)),
    )(*ins)


def _delta(do, o, name):
    S, W = o.shape
    tm = _pick(S, [512, 256, 128])
    perm = [dil for dil in DILS if dil > 1]

    def kern(do_ref, o_ref, dob_ref, d_ref, *rest):
        scr, scr_do = rest[2 * len(perm)], rest[2 * len(perm) + 1]
        prod = do_ref[...] * o_ref[...]
        ri = jnp.right_shift(lax.broadcasted_iota(jnp.int32, (W, W), 0), HEAD_SHIFT)
        ci = jnp.right_shift(lax.broadcasted_iota(jnp.int32, (W, W), 1), HEAD_SHIFT)
        same = jnp.where(ri == ci, 1.0, 0.0).astype(BF16)
        hi_p = prod.astype(BF16)
        lo_p = (prod - hi_p.astype(F32)).astype(BF16)
        dl = (lax.dot_general(hi_p, same, NN, preferred_element_type=F32)
              + lax.dot_general(lo_p, same, NN, preferred_element_type=F32))
        d_ref[...] = dl
        dob_ref[...] = do_ref[...].astype(BF16)
        _stage(scr, dl)
        _stage(scr_do, do_ref[...])
        for j, dil in enumerate(perm):
            for res in range(dil):
                rest[2 * j][res] = _get_residue(scr_do, res, dil).astype(BF16)
                rest[2 * j + 1][res] = _get_residue(scr, res, dil)

    out_shape = [jax.ShapeDtypeStruct((S, W), BF16), jax.ShapeDtypeStruct((S, W), F32)]
    out_specs = [_row_spec(tm, W), _row_spec(tm, W)]
    for dil in perm:
        out_shape += [jax.ShapeDtypeStruct((dil, S // dil, W), BF16), jax.ShapeDtypeStruct((dil, S // dil, W), F32)]
        out_specs += [_perm_spec(dil, tm, W), _perm_spec(dil, tm, W)]
    return pl.pallas_call(
        kern, name=name, grid=(S // tm,),
        out_shape=out_shape,
        in_specs=[_row_spec(tm, W), _row_spec(tm, W)], out_specs=out_specs,
        scratch_shapes=[_stage_shape(tm, W), _stage_shape(tm, W)],
        compiler_params=_params(("parallel",)),
    )(do, o)


def _ada_fwd(c16, ada_w, name):
    depth, D, n = ada_w.shape
    rows = 2 * N_DEV

    def kern(c_ref, w_ref, o_ref, cs_ref):
        cv = c_ref[...]
        cs = cv * jax.nn.sigmoid(cv)
        cs_ref[...] = cs
        o_ref[0] = _dot3(cs, w_ref[0], NN)

    return pl.pallas_call(
        kern, name=name, grid=(depth,),
        out_shape=[jax.ShapeDtypeStruct((depth, rows, n), F32), jax.ShapeDtypeStruct((rows, D), F32)],
        in_specs=[pl.BlockSpec((rows, D), lambda l: (0, 0)), pl.BlockSpec((1, D, n), lambda l: (l, 0, 0))],
        out_specs=[pl.BlockSpec((1, rows, n), lambda l: (l, 0, 0)), pl.BlockSpec((rows, D), lambda l: (0, 0))],
        compiler_params=_params(("arbitrary",)),
    )(c16, ada_w)


def _ada_bwd(cs16, dm16, name):
    depth, _, n = dm16.shape
    D = cs16.shape[1]

    def kern(cs_ref, dm_ref, o_ref):
        o_ref[0] = _dot3(cs_ref[...], dm_ref[0], TN)

    return pl.pallas_call(
        kern, name=name, grid=(depth,),
        out_shape=jax.ShapeDtypeStruct((depth, D, n), F32),
        in_specs=[pl.BlockSpec((2 * N_DEV, D), lambda l: (0, 0)), pl.BlockSpec((1, 2 * N_DEV, n), lambda l: (l, 0, 0))],
        out_specs=pl.BlockSpec((1, D, n), lambda l: (l, 0, 0)),
        compiler_params=_params(("parallel",)),
    )(cs16, dm16)


def _sum_rows8(parts, name):
    _, r, n = parts.shape

    def kern(p_ref, o_ref):
        acc = p_ref[0]
        for k in range(1, N_DEV):
            acc = acc + p_ref[k]
        o_ref[...] = acc

    return pl.pallas_call(
        kern, name=name, out_shape=jax.ShapeDtypeStruct((r, n), F32),
        in_specs=[pl.BlockSpec(memory_space=pltpu.VMEM)], out_specs=pl.BlockSpec(memory_space=pltpu.VMEM),
    )(parts)


def _adamw(w, g, m, v, name):
    shape = w.shape
    c = shape[-1]
    r = int(np.prod(shape[:-1])) if len(shape) > 1 else 1
    w2, g2, m2, v2 = (t.reshape(r, c) for t in (w, g, m, v))
    tr = r
    for cand in (2048, 1024, 512, 256, 128, 64, 32, 16, 8):
        if r % cand == 0 and cand * c * 4 <= (1 << 20):
            tr = cand
            break
    c1 = 1.0 - ADAM_B1 ** ADAM_STEP
    c2 = 1.0 - ADAM_B2 ** ADAM_STEP

    def kern(w_ref, g_ref, m_ref, v_ref, d_ref, nm_ref, nv_ref):
        gv = g_ref[...]
        nm = ADAM_B1 * m_ref[...] + (1.0 - ADAM_B1) * gv
        nv = ADAM_B2 * v_ref[...] + (1.0 - ADAM_B2) * (gv * gv)
        nm_ref[...] = nm
        nv_ref[...] = nv
        d_ref[...] = -ADAM_LR * ((nm / c1) / (jnp.sqrt(nv / c2) + ADAM_EPS) + ADAM_WD * w_ref[...])

    spec = pl.BlockSpec((tr, c), lambda i: (i, 0))
    outs = pl.pallas_call(
        kern, name=name, grid=(r // tr,),
        out_shape=[jax.ShapeDtypeStruct((r, c), F32)] * 3,
        in_specs=[spec] * 4, out_specs=[spec] * 3,
        compiler_params=_params(("parallel",)),
    )(w2, g2, m2, v2)
    return tuple(o.reshape(shape) for o in outs)


def _adamw_slab(w3, g, m3, v3, idx, prev, name):
    ns, r, c = w3.shape
    tr = r
    for cand in (2048, 1024, 512, 256, 128, 64, 32, 16, 8):
        if r % cand == 0 and cand * c * 4 <= (1 << 20):
            tr = cand
            break
    c1 = 1.0 - ADAM_B1 ** ADAM_STEP
    c2 = 1.0 - ADAM_B2 ** ADAM_STEP

    def kern(w_ref, g_ref, m_ref, v_ref, p0, p1, p2, p3, go_ref, d_ref, nm_ref, nv_ref):
        gv = g_ref[...]
        nm = ADAM_B1 * m_ref[0] + (1.0 - ADAM_B1) * gv
        nv = ADAM_B2 * v_ref[0] + (1.0 - ADAM_B2) * (gv * gv)
        go_ref[0] = gv
        nm_ref[0] = nm
        nv_ref[0] = nv
        d_ref[0] = -ADAM_LR * ((nm / c1) / (jnp.sqrt(nv / c2) + ADAM_EPS) + ADAM_WD * w_ref[0])

    if prev is None:
        prev = [lax.empty((ns, r, c), F32) for _ in range(4)]
    slab = pl.BlockSpec((1, tr, c), lambda i: (idx, i, 0))
    return pl.pallas_call(
        kern, name=name, grid=(r // tr,),
        out_shape=[jax.ShapeDtypeStruct((ns, r, c), F32)] * 4,
        in_specs=[slab, pl.BlockSpec((tr, c), lambda i: (i, 0)), slab, slab] + [pl.BlockSpec(memory_space=pl.ANY)] * 4,
        out_specs=[slab] * 4,
        input_output_aliases={4: 0, 5: 1, 6: 2, 7: 3},
        compiler_params=_params(("parallel",)),
    )(w3, g, m3, v3, *prev)


def kernel(x, c, ada_w, ada_b, norm_g, ffn_w_gate, ffn_w_up, ffn_w_down, w_in, conv_w, w_conv_out, w_attn_out, w_o, rel_bias, final_g, loss_target, m_ada_w, m_ada_b, m_norm_g, m_ffn_w_gate, m_ffn_w_up, m_ffn_w_down, m_w_in, m_conv_w, m_w_conv_out, m_w_attn_out, m_w_o, m_rel_bias, m_final_g, v_ada_w, v_ada_b, v_norm_g, v_ffn_w_gate, v_ffn_w_up, v_ffn_w_down, v_w_in, v_conv_w, v_w_conv_out, v_w_attn_out, v_w_o, v_rel_bias, v_final_g):
    depth = ada_w.shape[0]
    S, D = x.shape[1], x.shape[2]
    me = 4 * lax.axis_index("x") + 2 * lax.axis_index("y") + lax.axis_index("c")
    x0 = x.reshape(S, D)
    target = loss_target.reshape(S, D)
    fsh = ffn_w_down.shape[2]
    insh = w_in.shape[2]
    dsh = D // N_DEV
    ao_rows = dsh * ATTN_OUT // D

    piece_rows = [fsh] * 6 + [insh, dsh, dsh, ao_rows]

    FIRST = [0, 2, 4]
    REST = [p for p in range(len(piece_rows)) if p not in FIRST]

    def rows_of(idx):
        return [piece_rows[p] for p in idx]

    def pack(l, idx=None):
        def t(a):
            return jnp.transpose(a).astype(BF16)
        ps = [t(ffn_w_gate[l, 0]), t(ffn_w_gate[l, 1]), t(ffn_w_up[l, 0]), t(ffn_w_up[l, 1]),
              ffn_w_down[l, 0].astype(BF16), ffn_w_down[l, 1].astype(BF16), t(w_in[l]),
              w_conv_out[l].astype(BF16), w_o[l].astype(BF16), t(w_attn_out[l]).reshape(ao_rows, D)]
        return jnp.concatenate(ps if idx is None else [ps[p] for p in idx], axis=0)

    def unpack(full):
        in_t = full[6]
        qkv_t = [jnp.concatenate([in_t[t * QKV_W + g * ATTN_OUT: t * QKV_W + (g + 1) * ATTN_OUT] for t in range(3)])
                 for g in range(N_GROUPS)]
        ao_t = full[9].reshape(N_DEV, dsh, ATTN_OUT).reshape(D, ATTN_OUT)
        return dict(g_t=full[0:2], u_t=full[2:4], down=full[4:6], qkv_t=qkv_t, cg_t=in_t[3 * QKV_W:],
                    co=full[7], wo=full[8], ao_t=ao_t)

    def behind(v, token):
        return v + token[0, 0]

    c_all = _all_gather(c.reshape(D // 128, 128), "c_all_gather").reshape(N_DEV, D)
    c16 = jnp.concatenate([c_all, jnp.zeros_like(c_all)], axis=0)
    mod_part, cs16 = _ada_fwd(c16, ada_w, "ada_fwd")
    mod_part = mod_part[:, :N_DEV]
    n_ada = ada_w.shape[2]
    mod_all = _all_gather(mod_part.reshape(depth * N_DEV * n_ada // 128, 128), "mod_all_gather")
    mod_all = mod_all.reshape(N_DEV, depth, N_DEV, n_ada)
    mod_mine = lax.dynamic_index_in_dim(mod_all, me, axis=2, keepdims=False)
    mod = jnp.transpose(mod_mine, (1, 0, 2)).reshape(depth, N_DEV * n_ada) + ada_b
    mod = mod.reshape(depth, 3, 3, 1, D)

    small = jnp.concatenate([norm_g.reshape(-1), conv_w.reshape(-1)]).reshape(-1, 128)
    small_all = _all_gather(small, "small_all_gather").reshape(N_DEV, -1)
    n_ng = norm_g.size
    norm_g_full = jnp.transpose(small_all[:, :n_ng].reshape(N_DEV, depth, 3, dsh), (1, 2, 0, 3)).reshape(depth, 3, 1, D)
    conv_w_full = jnp.transpose(small_all[:, n_ng:].reshape(N_DEV, depth, CONV_K, dsh), (1, 2, 0, 3)).reshape(depth, CONV_K, D)

    buckets = jnp.asarray(_bucket_tables())
    bias = _bias_build(rel_bias, buckets, "bias_build")
    perm_dils = tuple(dil for dil in DILS if dil > 1)

    chain_done = mod.reshape(-1)[:128] + small_all.reshape(-1)[:128]
    first = _gather_start(pack(0, FIRST), rows_of(FIRST), chain_done, "weights_gather_start_l0_first")
    rest = _gather_start(pack(0, REST), rows_of(REST), first[3], "weights_gather_start_l0_rest")
    fwd = _gather_forward(first[0], first[1], first[2], rows_of(FIRST), bias, "weights_gather_forward_l0_first")
    zones_first = _gather_finish(fwd[0], fwd[1], fwd[2], bias, "weights_gather_finish_l0_first")
    W = [dict(g_t=[zones_first[0]], u_t=[zones_first[1]], down=[zones_first[2]])] + [None] * (depth - 1)

    saved = []
    xc = x0
    for l in range(depth):
        sv = {}
        gather, tie_sub = None, 0
        if 0 < l < depth - 1:
            gather = _gather_start(pack(l + 1), piece_rows, W[l]["wo"], f"weights_gather_start_l{l + 1}")
        for sub in (0, 1, 2):
            if l == 0 and sub == 1:
                fwd = _gather_forward(rest[0], rest[1], rest[2], rows_of(REST), xc, "weights_gather_forward_l0_rest")
                zones_rest = _gather_finish(fwd[0], fwd[1], fwd[2], xc, "weights_gather_finish_l0_rest")
                full = [None] * len(piece_rows)
                for p, z in zip(FIRST + REST, zones_first + zones_rest):
                    full[p] = z
                W[0] = unpack(full)
                if depth > 1:
                    gather, tie_sub = _gather_start(pack(1), piece_rows, W[0]["wo"], "weights_gather_start_l1"), 1
            g, sh, sc, gt = norm_g_full[l, sub], mod[l, sub, 0], mod[l, sub, 1], mod[l, sub, 2]
            if sub == tie_sub and gather is not None:
                g = behind(g, gather[3])
            rec = dict(x=xc)
            if sub != 1:
                i = 0 if sub == 0 else 1
                h = _norm_mod_fwd(xc, g, sc, sh, "norm_mod_fwd")[0]
                a, u, z = _ffn_up(h, W[l]["g_t"][i], W[l]["u_t"][i], "ffn_up")
                xc, f = _matmul(z, W[l]["down"][i], "nn", BF16, "ffn_down", tm=512, resid=(xc, gt, 0.5))
                rec.update(h=h, a=a, u=u, z=z, f=f)
            else:
                hs = _norm_mod_fwd(xc, g, sc, sh, "norm_mod_fwd_mixer", dils=perm_dils)
                h = hs[0]
                h_res = [h] + [_flat(t) for t in hs[1:]]
                cg = _matmul(h, W[l]["cg_t"], "nt", BF16, "mixer_cg")
                qkvs, os_, lses = [], [], []
                for gi, dil in enumerate(DILS):
                    qkv = _matmul(h_res[gi], W[l]["qkv_t"][gi], "nt", BF16, "mixer_qkv")
                    o_g, lse_g = _attn_fwd(qkv, bias[gi], dil, f"attn_fwd_g{gi}")
                    qkvs.append(qkv)
                    os_.append(o_g)
                    lses.append(lse_g)
                comb = _combine_fwd(os_, lses, "combine_fwd")
                o_f, o_b, lse = comb[0:3]
                lse_res = [lse] + [_flat(t) for t in comb[3:]]
                yc_in = _conv_fwd(cg, conv_w_full[l], "conv_fwd")
                yc = _matmul(yc_in, W[l]["co"], "nn", BF16, "conv_out")
                ya = _matmul(o_b, W[l]["ao_t"], "nt", BF16, "attn_out")
                merged = _merge_fwd(cg, yc, ya, "merge_fwd")
                xc, f = _matmul(merged, W[l]["wo"], "nn", BF16, "mixer_out", resid=(xc, gt, 1.0))
                rec.update(h=h, h_res=h_res, qkvs=qkvs, cg=cg, o_f=o_f, o_b=o_b, lse_res=lse_res, yc_in=yc_in,
                           yc=yc, ya=ya, merged=merged, f=f)
                if gather is not None:
                    fwd = _gather_forward(gather[0], gather[1], gather[2], piece_rows, xc,
                                          f"weights_gather_forward_l{l + 1}")
            sv[sub] = rec
        if gather is not None:
            W[l + 1] = unpack(_gather_finish(fwd[0], fwd[1], fwd[2], xc, f"weights_gather_finish_l{l + 1}"))
        saved.append(sv)

    dx, head = _loss_head(xc, final_g.reshape(1, D), target, "loss_head")
    d_final_g = head[0]
    loss_part = head[2, 0]

    d_mod = [[None] * 3 for _ in range(depth)]
    d_norm = [[None] * 3 for _ in range(depth)]
    d_conv = [None] * depth
    dlog = jnp.zeros((N_GROUPS, HEADS_PER_GROUP, BLOCK, 2 * BLOCK), F32)
    n_pieces = len(piece_rows)
    LATE = (0, 2, 4)
    EARLY = tuple(p for p in range(n_pieces) if p not in LATE)
    g_piece = [[None] * n_pieces for _ in range(depth)]

    piece_keys = (("g_t", 0), ("g_t", 1), ("u_t", 0), ("u_t", 1), ("down", 0), ("down", 1), "in_t", "co", "wo", "ao_t")

    def pieces_of(dW, idx):
        return [dW[piece_keys[p]].reshape(N_DEV * piece_rows[p], D) for p in idx]

    def finish_scatter(sc, idx, after_arr, layer, part=""):
        recv = _scatter_finish(sc[0], sc[1], sc[2], after_arr, f"grads_scatter_finish_l{layer}{part}")
        tot = _sum_sources(recv, "grads_sum")
        o = 0
        for p in idx:
            g_piece[layer][p] = tot[o:o + piece_rows[p]]
            o += piece_rows[p]
        return recv

    scatter = None
    early = None
    for l in reversed(range(depth)):
        dW = {}
        for sub in (2, 1, 0):
            rec = saved[l][sub]
            g, sc, gt = norm_g_full[l, sub], mod[l, sub, 1], mod[l, sub, 2]
            if sub == 2 and scatter is not None:
                gt = behind(gt, scatter[3])
            if sub == 0 and early is not None:
                gt = behind(gt, early[3])
            if sub != 1:
                i = 0 if sub == 0 else 1
                df, gsum = _gate_bwd(dx, rec["f"], gt, 0.5, "gate_bwd")
                dz = _matmul(df, W[l]["down"][i], "nt", BF16, "ffn_down_dx")
                dW["down", i] = _matmul(rec["z"], df, "tn", BF16, "ffn_down_dw")
                da, du, dh = _ffn_up_bwd(dz, rec["a"], rec["u"], W[l]["g_t"][i], W[l]["u_t"][i], "ffn_up_bwd")
                dW["g_t", i] = _matmul(da, rec["h"], "tn", BF16, "ffn_gate_dw")
                dW["u_t", i] = _matmul(du, rec["h"], "tn", BF16, "ffn_up_dw")
                dx, sums = _norm_mod_bwd(rec["x"], [dh], [], dx, g, sc, "norm_mod_bwd")
            else:
                dout, gsum = _gate_bwd(dx, rec["f"], gt, 1.0, "gate_bwd_mixer")
                dm = _matmul(dout, W[l]["wo"], "nt", BF16, "mixer_out_dx")
                dW["wo"] = _matmul(rec["merged"], dout, "tn", BF16, "mixer_out_dw")
                dyc, dya, dgg = _merge_bwd(rec["cg"], rec["yc"], rec["ya"], dm, "merge_bwd")
                dyc_in = _matmul(dyc, W[l]["co"], "nt", BF16, "conv_out_dx")
                dW["co"] = _matmul(rec["yc_in"], dyc, "tn", BF16, "conv_out_dw")
                do = _matmul(dya, W[l]["ao_t"], "nn", F32, "attn_out_dx")
                dW["ao_t"] = _matmul(dya, rec["o_b"], "tn", BF16, "attn_out_dw")
                dl = _delta(do, rec["o_f"], "attn_delta")
                do_res = [dl[0]] + [_flat(t) for t in dl[2::2]]
                del_res = [dl[1]] + [_flat(t) for t in dl[3::2]]
                dh_attn, dw_q, dw_kv, dlog_l = [], [], [], []
                for gi, dil in enumerate(DILS):
                    dq, dkv, dlg = _attn_bwd(rec["qkvs"][gi], do_res[gi], rec["lse_res"][gi], del_res[gi],
                                             bias[gi], dil, f"attn_bwd_g{gi}")
                    dlog_l.append(dlg)
                    dh_attn.append(_attn_dh(dq, dkv, W[l]["qkv_t"][gi], "attn_dh"))
                    dw_q.append(_matmul(dq, rec["h_res"][gi], "tn", BF16, "mixer_q_dw"))
                    dw_kv.append(_matmul(dkv, rec["h_res"][gi], "tn", BF16, "mixer_kv_dw"))
                dlog = dlog + jnp.stack(dlog_l)
                dcg, conv_sum = _conv_bwd(rec["cg"], dyc_in, dgg, conv_w_full[l], "conv_bwd")
                d_conv[l] = conv_sum[0:CONV_K]
                dh_cg = _matmul(dcg, W[l]["cg_t"], "nn", F32, "mixer_cg_dx")
                dw_cg = _matmul(dcg, rec["h"], "tn", BF16, "mixer_cg_dw")
                dW["in_t"] = jnp.concatenate(
                    dw_q + [t[:ATTN_OUT] for t in dw_kv] + [t[ATTN_OUT:] for t in dw_kv] + [dw_cg], axis=0)
                perm_parts = [(dil, _by_residue(dh_attn[gi], dil)) for gi, dil in enumerate(DILS) if dil > 1]
                dx, sums = _norm_mod_bwd(rec["x"], [dh_cg, dh_attn[0]], perm_parts, dx, g, sc, "norm_mod_bwd_mixer")
            d_mod[l][sub] = jnp.stack([sums[0], sums[2], gsum[0]])
            d_norm[l][sub] = sums[3]
            if l == 0 and sub == 1:
                after = dx
                if scatter is not None:
                    after = finish_scatter(scatter, range(n_pieces), dx, l + 1)
                    scatter = None
                early = _scatter_start(pieces_of(dW, EARLY), after, "grads_scatter_start_l0_early")
        if l > 0:
            after = dx
            if scatter is not None:
                after = finish_scatter(scatter, range(n_pieces), dx, l + 1)
            scatter = _scatter_start(pieces_of(dW, range(n_pieces)), after, f"grads_scatter_start_l{l}")
        else:
            late = _scatter_start(pieces_of(dW, LATE), dx, "grads_scatter_start_l0_late")
    grad_x = dx.reshape(1, S, D)
    d_rel = _bias_bwd(dlog, buckets, "bias_bwd")[:, :rel_bias.shape[1]]

    big = dict(
        ffn_w_gate=(ffn_w_gate, m_ffn_w_gate, v_ffn_w_gate, (0, 1), True, None),
        ffn_w_up=(ffn_w_up, m_ffn_w_up, v_ffn_w_up, (2, 3), True, None),
        ffn_w_down=(ffn_w_down, m_ffn_w_down, v_ffn_w_down, (4, 5), False, None),
        w_in=(w_in, m_w_in, v_w_in, (6,), True, None),
        w_conv_out=(w_conv_out, m_w_conv_out, v_w_conv_out, (7,), False, None),
        w_o=(w_o, m_w_o, v_w_o, (8,), False, None),
        w_attn_out=(w_attn_out, m_w_attn_out, v_w_attn_out, (9,), True, (dsh, ATTN_OUT)))
    big_out = {name: None for name in big}

    def adam_layer(l, token=None):
        for name, (w_, m_, v_, plist, transposed, shard_shape) in big.items():
            ns, r, cc = depth * len(plist), w_.shape[-2], w_.shape[-1]
            w3, m3, v3 = (t.reshape(ns, r, cc) for t in (w_, m_, v_))
            for j, p in enumerate(plist):
                gl = g_piece[l][p]
                if shard_shape is not None:
                    gl = gl.reshape(shard_shape)
                if transposed:
                    gl = jnp.transpose(gl)
                if token is not None:
                    gl = behind(gl, token)
                big_out[name] = _adamw_slab(w3, gl, m3, v3, l * len(plist) + j, big_out[name], "adamw_" + name)

    for l in range(depth - 1, 0, -1):
        adam_layer(l, late[3])
    done = [late[3]] + [st[1] for st in big_out.values() if st is not None]
    finish_scatter(early, EARLY, done, 0, "_early")
    finish_scatter(late, LATE, dx, 0, "_late")
    adam_layer(0)

    d_mod_flat = jnp.stack([jnp.stack(d_mod[l]) for l in range(depth)]).reshape(-1)
    d_norm_flat = jnp.stack([jnp.stack(d_norm[l]) for l in range(depth)]).reshape(-1)
    d_conv_flat = jnp.stack(d_conv).reshape(-1)
    vec = jnp.concatenate([d_mod_flat, d_norm_flat, d_conv_flat, d_rel.reshape(-1), d_final_g,
                           jnp.broadcast_to(loss_part, (128,))])
    pad = (-vec.size) % 1024
    vec = jnp.concatenate([vec, jnp.zeros((pad,), F32)]).reshape(-1, 128)
    parts = _all_gather(vec, "small_grads_all_gather").reshape(N_DEV, vec.shape[0], 128)
    tot = _sum_rows8(parts, "small_grads_sum").reshape(-1)
    o0 = 0
    g_ada_b = tot[o0:o0 + d_mod_flat.size].reshape(ada_b.shape)
    o0 += d_mod_flat.size
    g_norm_full = tot[o0:o0 + d_norm_flat.size].reshape(depth, 3, D)
    o0 += d_norm_flat.size
    g_conv_full = tot[o0:o0 + d_conv_flat.size].reshape(depth, CONV_K, D)
    o0 += d_conv_flat.size
    g_rel = tot[o0:o0 + rel_bias.size].reshape(rel_bias.shape)
    o0 += rel_bias.size
    g_final = tot[o0:o0 + D]
    o0 += D
    loss = tot[o0]
    g_norm = lax.dynamic_slice_in_dim(g_norm_full, me * dsh, dsh, axis=2)
    g_conv = lax.dynamic_slice_in_dim(g_conv_full, me * dsh, dsh, axis=2)

    dm_all = parts.reshape(N_DEV, -1)[:, :d_mod_flat.size].reshape(N_DEV, depth, N_DEV * n_ada)
    dm_cols = lax.dynamic_slice_in_dim(dm_all, me * n_ada, n_ada, axis=2)
    dm16 = jnp.concatenate([jnp.transpose(dm_cols, (1, 0, 2)), jnp.zeros((depth, N_DEV, n_ada), F32)], axis=1)
    g_ada_w = _ada_bwd(cs16, dm16, "ada_bwd")

    small = dict(ada_w=(ada_w, g_ada_w, m_ada_w, v_ada_w), ada_b=(ada_b, g_ada_b, m_ada_b, v_ada_b),
                 norm_g=(norm_g, g_norm, m_norm_g, v_norm_g), conv_w=(conv_w, g_conv, m_conv_w, v_conv_w),
                 rel_bias=(rel_bias, g_rel, m_rel_bias, v_rel_bias), final_g=(final_g, g_final, m_final_g, v_final_g))
    order = ("ada_w", "ada_b", "norm_g", "ffn_w_gate", "ffn_w_up", "ffn_w_down", "w_in", "conv_w", "w_conv_out",
             "w_attn_out", "w_o", "rel_bias", "final_g")
    res = {}
    for name in order:
        if name in big:
            res[name] = tuple(t.reshape(big[name][0].shape) for t in big_out[name])
        else:
            w_, g_, m_, v_ = small[name]
            res[name] = (g_,) + _adamw(w_, g_, m_, v_, "adamw_" + name)
    return (loss, grad_x, *[res[n][0] for n in order], *[res[n][1] for n in order],
            *[res[n][2] for n in order], *[res[n][3] for n in order])
```

```python
import functools

import numpy as np
import jax
import jax.numpy as jnp
from jax import lax
from jax.experimental import pallas as pl
from jax.experimental.pallas import tpu as pltpu

F32 = jnp.float32
BF16 = jnp.bfloat16

N_DEV = 8
HEAD_DIM = 64
HEAD_SHIFT = 6
HEADS_PER_GROUP = 8
DILATION_GROUPS = ((128, 1), (512, 4), (2048, 16))
DILS = tuple(d for _, d in DILATION_GROUPS)
N_GROUPS = len(DILATION_GROUPS)
ATTN_OUT = HEADS_PER_GROUP * HEAD_DIM
QKV_W = N_GROUPS * ATTN_OUT
BLOCK = 128
NUM_BUCKETS = 32
MAX_DISTANCE = 2048
CONV_K = 3
EPS = 1e-6
NEG_INF = -1e30
SCALE = HEAD_DIM ** -0.5

ADAM_LR = 0.001
ADAM_B1 = 0.9
ADAM_B2 = 0.999
ADAM_EPS = 1e-08
ADAM_WD = 0.01
ADAM_STEP = 10

V7X_VMEM_LIMIT = 48 * 1024 * 1024
MESH = pl.DeviceIdType.MESH

NN = (((1,), (0,)), ((), ()))
NT = (((1,), (1,)), ((), ()))
TN = (((0,), (0,)), ((), ()))


def _pick(dim, cands):
    for c in cands:
        if dim % c == 0:
            return c
    return dim


def _pick_k(K, cap=2816):
    if K <= cap or K % 128:
        return K
    best = 128
    for m in range(1, K // 128 + 1):
        if (K // 128) % m == 0 and 128 * m <= cap:
            best = 128 * m
    return best


def _params(sem):
    return pltpu.CompilerParams(dimension_semantics=sem, vmem_limit_bytes=V7X_VMEM_LIMIT)


def _all_gather(x_shard, name):
    m_per, n = x_shard.shape

    def body(x_ref, out_ref, send_sems, recv_sems, local_sem):
        x, y, c = lax.axis_index("x"), lax.axis_index("y"), lax.axis_index("c")
        me, sibling = (x, y, c), (x, y, 1 - c)
        chips = [(1 - x, y), (x, 1 - y), (1 - x, 1 - y)]

        def rows(px, py, pc):
            return out_ref.at[pl.ds((4 * px + 2 * py + pc) * m_per, m_per), :]

        def copy(k, block, to, src=None):
            return pltpu.make_async_remote_copy(
                src_ref=rows(*block) if src is None else src, dst_ref=rows(*block),
                send_sem=send_sems.at[k], recv_sem=recv_sems.at[k], device_id=to, device_id_type=MESH)

        mine = pltpu.make_async_copy(x_ref, rows(*me), local_sem)
        mine.start()
        first = [copy(0, me, sibling, src=x_ref)]
        first += [copy(1 + j, me, (*chip, c), src=x_ref) for j, chip in enumerate(chips)]
        for cp in first:
            cp.start()
        passed = [copy(4 + j, (*chip, c), sibling) for j, chip in enumerate(chips)]
        for j, chip in enumerate(chips):
            copy(1 + j, (*chip, c), me).wait_recv()
            passed[j].start()
        copy(0, sibling, me).wait_recv()
        for j, chip in enumerate(chips):
            copy(4 + j, (*chip, 1 - c), me).wait_recv()
        for cp in first + passed:
            cp.wait_send()
        mine.wait()

    return pl.pallas_call(
        body, name=name,
        out_shape=jax.ShapeDtypeStruct((N_DEV * m_per, n), x_shard.dtype),
        in_specs=[pl.BlockSpec(memory_space=pltpu.VMEM)],
        out_specs=pl.BlockSpec(memory_space=pltpu.VMEM),
        scratch_shapes=[pltpu.SemaphoreType.DMA((7,)), pltpu.SemaphoreType.DMA((7,)), pltpu.SemaphoreType.DMA],
    )(x_shard)


def _offsets(piece_rows):
    offs, o = [], 0
    for n in piece_rows:
        offs.append(o)
        o += n
    return offs


HBM_SPEC = pl.BlockSpec(memory_space=pltpu.HBM)
SEM_SPEC = pl.BlockSpec(memory_space=pltpu.SEMAPHORE)
ANY_SPEC = pl.BlockSpec(memory_space=pl.ANY)
SPLIT_COPY_PARAMS = pltpu.CompilerParams(has_side_effects=pltpu.SideEffectType.DATAFLOW_SIDE_EFFECTING)


def _in_hbm(a):
    return pltpu.with_memory_space_constraint(a, pltpu.HBM)


def _dma_sems(n):
    return [pltpu.SemaphoreType.DMA(())] * n


def _whole(ref, send_sem, recv_sem, me):
    return pltpu.make_async_remote_copy(src_ref=ref, dst_ref=ref, send_sem=send_sem, recv_sem=recv_sem,
                                        device_id=me, device_id_type=MESH)


def _gather_start(packed, piece_rows, after, name):
    R, w = packed.shape
    offs = _offsets(piece_rows)
    P = len(piece_rows)
    assert offs[-1] + piece_rows[-1] == R

    def body(*refs):
        src_ref = refs[0]
        o = refs[P + 2:]
        send, recv = o[0:4], o[4:8]
        zones, token, stage, local_sems = o[9:9 + P], o[9 + P], o[10 + P], o[11 + P]
        x, y, c = lax.axis_index("x"), lax.axis_index("y"), lax.axis_index("c")
        targets = [(x, y, 1 - c), (1 - x, y, c), (x, 1 - y, c), (1 - x, 1 - y, c)]
        me = 4 * x + 2 * y + c

        def piece(p, ref):
            return ref.at[pl.ds(offs[p], piece_rows[p]), :]

        def rows(p):
            return zones[p].at[pl.ds(me * piece_rows[p], piece_rows[p]), :]

        for k, to in enumerate(targets):
            for p in range(P):
                pltpu.make_async_remote_copy(src_ref=piece(p, src_ref), dst_ref=rows(p), send_sem=send[k],
                                             recv_sem=recv[k], device_id=to, device_id_type=MESH).start()
        load = pltpu.make_async_copy(src_ref, stage, local_sems.at[P])
        load.start()
        load.wait()
        mine = [pltpu.make_async_copy(piece(p, stage), rows(p), local_sems.at[p]) for p in range(P)]
        for cp in mine:
            cp.start()
        for cp in mine:
            cp.wait()
        token[...] = jnp.zeros_like(token)

    zones_in = [_in_hbm(lax.empty((N_DEV * n, w), packed.dtype)) for n in piece_rows]
    outs = pl.pallas_call(
        body, name=name,
        out_shape=(*_dma_sems(8), pltpu.HBM((R, w), packed.dtype),
                   *[pltpu.HBM((N_DEV * n, w), packed.dtype) for n in piece_rows],
                   jax.ShapeDtypeStruct((8, 128), F32)),
        in_specs=[HBM_SPEC] * (P + 1) + [ANY_SPEC],
        out_specs=[SEM_SPEC] * 8 + [HBM_SPEC] * (P + 1) + [pl.BlockSpec(memory_space=pltpu.VMEM)],
        input_output_aliases={0: 8, **{1 + p: 9 + p for p in range(P)}},
        scratch_shapes=[pltpu.VMEM((R, w), packed.dtype), pltpu.SemaphoreType.DMA((P + 1,))],
        compiler_params=SPLIT_COPY_PARAMS,
    )(_in_hbm(packed), *zones_in, after)
    return outs[0:8], outs[8], list(outs[9:9 + P]), outs[9 + P]


def _gather_forward(sems, packed, zones, piece_rows, after, name):
    P = len(piece_rows)

    def body(*refs):
        src_ref = refs[0]
        s = refs[1 + P:9 + P]
        o = refs[10 + P:]
        send, recv = s[0:4], s[4:8]
        send2, recv2, zones_o = o[0:3], o[3:6], o[7:7 + P]
        x, y, c = lax.axis_index("x"), lax.axis_index("y"), lax.axis_index("c")
        me = (x, y, c)
        chips = [(1 - x, y), (x, 1 - y), (1 - x, 1 - y)]
        for j, (px, py) in enumerate(chips):
            _whole(src_ref, send[1 + j], recv[1 + j], me).wait_recv()
            blk = 4 * px + 2 * py + c
            for p in range(P):
                r = zones_o[p].at[pl.ds(blk * piece_rows[p], piece_rows[p]), :]
                pltpu.make_async_remote_copy(src_ref=r, dst_ref=r, send_sem=send2[j], recv_sem=recv2[j],
                                             device_id=(x, y, 1 - c), device_id_type=MESH).start()
        _whole(src_ref, send[0], recv[0], me).wait_recv()
        for k in range(4):
            _whole(src_ref, send[k], recv[k], me).wait_send()
        o[7 + P][...] = jnp.zeros_like(o[7 + P])

    outs = pl.pallas_call(
        body, name=name,
        out_shape=(*_dma_sems(6), pltpu.HBM(packed.shape, packed.dtype),
                   *[pltpu.HBM(z.shape, z.dtype) for z in zones], jax.ShapeDtypeStruct((8, 128), F32)),
        in_specs=[HBM_SPEC] * (P + 1) + [SEM_SPEC] * 8 + [ANY_SPEC],
        out_specs=[SEM_SPEC] * 6 + [HBM_SPEC] * (P + 1) + [pl.BlockSpec(memory_space=pltpu.VMEM)],
        input_output_aliases={0: 6, **{1 + p: 7 + p for p in range(P)}},
        compiler_params=SPLIT_COPY_PARAMS,
    )(packed, *zones, *sems, after)
    return outs[0:6], outs[6], list(outs[7:7 + P]), outs[7 + P]


def _gather_finish(sems2, packed, zones, after, name):
    P = len(zones)

    def body(*refs):
        src_ref = refs[0]
        s = refs[1 + P:7 + P]
        x, y, c = lax.axis_index("x"), lax.axis_index("y"), lax.axis_index("c")
        for j in range(3):
            _whole(src_ref, s[j], s[3 + j], (x, y, c)).wait_recv()
        for j in range(3):
            _whole(src_ref, s[j], s[3 + j], (x, y, c)).wait_send()

    outs = pl.pallas_call(
        body, name=name,
        out_shape=(pltpu.HBM(packed.shape, packed.dtype), *[pltpu.HBM(z.shape, z.dtype) for z in zones]),
        in_specs=[HBM_SPEC] * (P + 1) + [SEM_SPEC] * 6 + [ANY_SPEC],
        out_specs=[HBM_SPEC] * (P + 1),
        input_output_aliases={p: p for p in range(P + 1)},
        compiler_params=SPLIT_COPY_PARAMS,
    )(packed, *zones, *sems2, after)
    return list(outs[1:1 + P])


def _scatter_start(pieces, after, name):
    P = len(pieces)
    w = pieces[0].shape[1]
    piece_rows = [p.shape[0] // N_DEV for p in pieces]
    offs = _offsets(piece_rows)
    R = offs[-1] + piece_rows[-1]

    def body(*refs):
        o = refs[P + 2:]
        send, recv = o[0:7], o[7:14]
        srcs, dst_ref, token, stage, local_sems = o[14:14 + P], o[14 + P], o[15 + P], o[16 + P], o[17 + P]
        x, y, c = lax.axis_index("x"), lax.axis_index("y"), lax.axis_index("c")
        me = 4 * x + 2 * y + c

        def chunk(p, dev):
            return srcs[p].at[pl.ds(dev * piece_rows[p], piece_rows[p]), :]

        def slot(p, dev):
            return dst_ref.at[dev, pl.ds(offs[p], piece_rows[p]), :]

        for k in range(1, N_DEV):
            px = 1 - x if (k >> 2) & 1 else x
            py = 1 - y if (k >> 1) & 1 else y
            pc = 1 - c if k & 1 else c
            peer = 4 * px + 2 * py + pc
            for p in range(P):
                pltpu.make_async_remote_copy(
                    src_ref=chunk(p, peer), dst_ref=slot(p, me), send_sem=send[k - 1], recv_sem=recv[k - 1],
                    device_id=(px, py, pc), device_id_type=MESH).start()
        mine = [pltpu.make_async_copy(chunk(p, me), stage.at[pl.ds(offs[p], piece_rows[p]), :], local_sems.at[p])
                for p in range(P)]
        for cp in mine:
            cp.start()
        for cp in mine:
            cp.wait()
        store = pltpu.make_async_copy(stage, dst_ref.at[me], local_sems.at[P])
        store.start()
        store.wait()
        token[...] = jnp.zeros_like(token)

    dtype = pieces[0].dtype
    outs = pl.pallas_call(
        body, name=name,
        out_shape=(*_dma_sems(14), *[pltpu.HBM(p.shape, dtype) for p in pieces], pltpu.HBM((N_DEV, R, w), dtype),
                   jax.ShapeDtypeStruct((8, 128), F32)),
        in_specs=[HBM_SPEC] * (P + 1) + [ANY_SPEC],
        out_specs=[SEM_SPEC] * 14 + [HBM_SPEC] * (P + 1) + [pl.BlockSpec(memory_space=pltpu.VMEM)],
        input_output_aliases={p: 14 + p for p in range(P + 1)},
        scratch_shapes=[pltpu.VMEM((R, w), dtype), pltpu.SemaphoreType.DMA((P + 1,))],
        compiler_params=SPLIT_COPY_PARAMS,
    )(*[_in_hbm(p) for p in pieces], _in_hbm(lax.empty((N_DEV, R, w), dtype)), after)
    return outs[0:14], list(outs[14:14 + P]), outs[14 + P], outs[15 + P]


def _scatter_finish(sems, pieces, recv, after, name):
    P = len(pieces)
    after = list(after) if isinstance(after, (list, tuple)) else [after]

    def body(*refs):
        dst_ref = refs[P]
        s = refs[P + 1:P + 15]
        x, y, c = lax.axis_index("x"), lax.axis_index("y"), lax.axis_index("c")
        for k in range(7):
            _whole(dst_ref.at[0], s[k], s[7 + k], (x, y, c)).wait_recv()
        for k in range(7):
            _whole(dst_ref.at[0], s[k], s[7 + k], (x, y, c)).wait_send()

    outs = pl.pallas_call(
        body, name=name,
        out_shape=(*[pltpu.HBM(p.shape, p.dtype) for p in pieces], pltpu.HBM(recv.shape, recv.dtype)),
        in_specs=[HBM_SPEC] * (P + 1) + [SEM_SPEC] * 14 + [ANY_SPEC] * len(after),
        out_specs=[HBM_SPEC] * (P + 1),
        input_output_aliases={p: p for p in range(P + 1)},
        compiler_params=SPLIT_COPY_PARAMS,
    )(*pieces, recv, *sems, *after)
    return outs[P]


def _sum_sources(parts, name):
    _, r, n = parts.shape
    tr = _pick(r, [256, 128, 64, 32, 16, 8])

    def kern(p_ref, o_ref):
        acc = p_ref[0].astype(F32)
        for k in range(1, N_DEV):
            acc = acc + p_ref[k].astype(F32)
        o_ref[...] = acc

    return pl.pallas_call(
        kern, name=name, grid=(r // tr,),
        out_shape=jax.ShapeDtypeStruct((r, n), F32),
        in_specs=[pl.BlockSpec((N_DEV, tr, n), lambda i: (0, i, 0))],
        out_specs=pl.BlockSpec((tr, n), lambda i: (i, 0)),
        compiler_params=_params(("parallel",)),
    )(parts)


def _matmul(a, b, mode, out_dtype, name, tm=None, tn=None, tk=None, resid=None):
    if mode == "nn":
        (M, K), N = a.shape, b.shape[1]
    elif mode == "nt":
        (M, K), N = a.shape, b.shape[0]
    else:
        (K, M), N = a.shape, b.shape[1]
    dims = {"nn": NN, "nt": NT, "tn": TN}[mode]
    tm = tm or _pick(M, [1024, 1408, 512, 256, 128])
    tn = tn or _pick(N, [1024, 1408, 512, 256, 128])
    tk = tk or _pick_k(K)
    nk = K // tk
    a_spec = {"nn": pl.BlockSpec((tm, tk), lambda i, j, k: (i, k)),
              "nt": pl.BlockSpec((tm, tk), lambda i, j, k: (i, k)),
              "tn": pl.BlockSpec((tk, tm), lambda i, j, k: (k, i))}[mode]
    b_spec = {"nn": pl.BlockSpec((tk, tn), lambda i, j, k: (k, j)),
              "nt": pl.BlockSpec((tn, tk), lambda i, j, k: (j, k)),
              "tn": pl.BlockSpec((tk, tn), lambda i, j, k: (k, j))}[mode]
    o_spec = pl.BlockSpec((tm, tn), lambda i, j, k: (i, j))
    n_in = 2 if resid is None else 4
    n_out = 1 if resid is None else 2

    def kern(*refs):
        a_ref, b_ref = refs[0], refs[1]
        outs = refs[n_in:n_in + n_out]
        acc_ref = refs[n_in + n_out] if nk > 1 else None

        def finish(acc):
            if resid is None:
                outs[0][...] = acc.astype(out_dtype)
            else:
                x_ref, g_ref = refs[2], refs[3]
                outs[0][...] = x_ref[...] + (resid[2] * g_ref[...]) * acc
                outs[1][...] = acc.astype(out_dtype)

        part = lax.dot_general(a_ref[...], b_ref[...], dims, preferred_element_type=F32)
        if nk == 1:
            finish(part)
        else:
            k = pl.program_id(2)

            @pl.when(k == 0)
            def _():
                acc_ref[...] = part

            @pl.when(k > 0)
            def _():
                acc_ref[...] += part

            @pl.when(k == nk - 1)
            def _():
                finish(acc_ref[...])

    in_specs = [a_spec, b_spec]
    args = [a, b]
    out_shape = [jax.ShapeDtypeStruct((M, N), out_dtype)]
    out_specs = [o_spec]
    if resid is not None:
        in_specs += [o_spec, pl.BlockSpec((1, tn), lambda i, j, k: (0, j))]
        args += [resid[0], resid[1]]
        out_shape = [jax.ShapeDtypeStruct((M, N), F32)] + out_shape
        out_specs = [o_spec, o_spec]
    res = pl.pallas_call(
        kern, name=name, grid=(M // tm, N // tn, nk),
        out_shape=out_shape, in_specs=in_specs, out_specs=out_specs,
        scratch_shapes=[pltpu.VMEM((tm, tn), F32)] if nk > 1 else [],
        compiler_params=_params(("parallel", "parallel", "arbitrary")),
    )(*args)
    return res[0] if resid is None else res


def _dot3(a, b, dims):
    ah = a.astype(BF16)
    al = (a - ah.astype(F32)).astype(BF16)
    bh = b.astype(BF16)
    bl = (b - bh.astype(F32)).astype(BF16)
    d = functools.partial(lax.dot_general, dimension_numbers=dims, preferred_element_type=F32)
    return d(ah, bh) + (d(ah, bl) + d(al, bh))


def _silu_parts(a):
    sg = jax.nn.sigmoid(a)
    return a * sg, sg * (1.0 + a * (1.0 - sg))


def _ffn_up(h, wg_t, wu_t, name):
    S, D = h.shape
    F = wg_t.shape[0]
    tm = _pick(S, [512, 256, 128])
    tn = _pick(F, [1408, 512, 256, 128])

    def kern(h_ref, g_ref, u_ref, a_out, u_out, z_out):
        hv = h_ref[...]
        a = lax.dot_general(hv, g_ref[...], NT, preferred_element_type=F32)
        u = lax.dot_general(hv, u_ref[...], NT, preferred_element_type=F32)
        a_out[...] = a.astype(BF16)
        u_out[...] = u.astype(BF16)
        z_out[...] = (_silu_parts(a)[0] * u).astype(BF16)

    w_spec = pl.BlockSpec((tn, D), lambda j, i: (j, 0))
    o_spec = pl.BlockSpec((tm, tn), lambda j, i: (i, j))
    return pl.pallas_call(
        kern, name=name, grid=(F // tn, S // tm),
        out_shape=[jax.ShapeDtypeStruct((S, F), BF16)] * 3,
        in_specs=[pl.BlockSpec((tm, D), lambda j, i: (i, 0)), w_spec, w_spec],
        out_specs=[o_spec] * 3,
        compiler_params=_params(("parallel", "parallel")),
    )(h, wg_t, wu_t)


def _ffn_up_bwd(dz, a, u, wg_t, wu_t, name):
    S, F = dz.shape
    D = wg_t.shape[1]
    tm = _pick(S, [512, 256, 128])
    tk = _pick(F, [1408, 512, 256, 128])
    nk = F // tk

    def kern(dz_ref, a_ref, u_ref, g_ref, w_ref, da_out, du_out, dh_out, acc_ref):
        k = pl.program_id(1)
        av = a_ref[...].astype(F32)
        uv = u_ref[...].astype(F32)
        dzv = dz_ref[...].astype(F32)
        silu, dsilu = _silu_parts(av)
        da = (dzv * uv * dsilu).astype(BF16)
        du = (dzv * silu).astype(BF16)
        da_out[...] = da
        du_out[...] = du
        part = (lax.dot_general(da, g_ref[...], NN, preferred_element_type=F32)
                + lax.dot_general(du, w_ref[...], NN, preferred_element_type=F32))

        @pl.when(k == 0)
        def _():
            acc_ref[...] = part

        @pl.when(k > 0)
        def _():
            acc_ref[...] += part

        @pl.when(k == nk - 1)
        def _():
            dh_out[...] = acc_ref[...]

    t_spec = pl.BlockSpec((tm, tk), lambda i, k: (i, k))
    w_spec = pl.BlockSpec((tk, D), lambda i, k: (k, 0))
    return pl.pallas_call(
        kern, name=name, grid=(S // tm, nk),
        out_shape=[jax.ShapeDtypeStruct((S, F), BF16)] * 2 + [jax.ShapeDtypeStruct((S, D), F32)],
        in_specs=[t_spec, t_spec, t_spec, w_spec, w_spec],
        out_specs=[t_spec, t_spec, pl.BlockSpec((tm, D), lambda i, k: (i, 0))],
        scratch_shapes=[pltpu.VMEM((tm, D), F32)],
        compiler_params=_params(("parallel", "arbitrary")),
    )(dz, a, u, wg_t, wu_t)


def _attn_dh(dq, dkv, w_t, name):
    S = dq.shape[0]
    D = w_t.shape[1]
    tm = _pick(S, [1024, 512, 256, 128])

    def kern(dq_ref, dk_ref, dv_ref, wq_ref, wk_ref, wv_ref, o_ref):
        o_ref[...] = (lax.dot_general(dq_ref[...], wq_ref[...], NN, preferred_element_type=F32)
                      + lax.dot_general(dk_ref[...], wk_ref[...], NN, preferred_element_type=F32)
                      + lax.dot_general(dv_ref[...], wv_ref[...], NN, preferred_element_type=F32))

    def w_blk(j):
        return pl.BlockSpec((ATTN_OUT, D), lambda i: (j, 0))

    return pl.pallas_call(
        kern, name=name, grid=(S // tm,),
        out_shape=jax.ShapeDtypeStruct((S, D), F32),
        in_specs=[pl.BlockSpec((tm, ATTN_OUT), lambda i: (i, 0)), pl.BlockSpec((tm, ATTN_OUT), lambda i: (i, 0)),
                  pl.BlockSpec((tm, ATTN_OUT), lambda i: (i, 1)), w_blk(0), w_blk(1), w_blk(2)],
        out_specs=pl.BlockSpec((tm, D), lambda i: (i, 0)),
        compiler_params=_params(("parallel",)),
    )(dq, dkv, dkv, w_t, w_t, w_t)


def _row_spec(tm, d):
    return pl.BlockSpec((tm, d), lambda i: (i, 0))


def _vec_spec(d, rows=1):
    return pl.BlockSpec((rows, d), lambda i: (0, 0))


def _perm_spec(dil, tm, w):
    return pl.BlockSpec((dil, tm // dil, w), lambda i: (0, i, 0))


def _stage_shape(tm, w):
    return pltpu.VMEM((w // 128, tm, 128), F32)


def _stage(scr, val):
    for ci in range(scr.shape[0]):
        scr[ci] = val[:, 128 * ci:128 * (ci + 1)]


def _unstage(scr):
    return jnp.concatenate([scr[ci] for ci in range(scr.shape[0])], axis=1)


def _get_residue(scr, res, dil):
    n = scr.shape[1] // dil
    return jnp.concatenate([scr[ci, pl.ds(res, n, stride=dil), :] for ci in range(scr.shape[0])], axis=1)


def _put_residue(scr, res, dil, val):
    n = scr.shape[1] // dil
    for ci in range(scr.shape[0]):
        scr[ci, pl.ds(res, n, stride=dil), :] = val[:, 128 * ci:128 * (ci + 1)]


def _norm_mod_fwd(x, g, s, b, name, dils=()):
    S, D = x.shape
    tm = _pick(S, [512, 256, 128])

    def kern(x_ref, g_ref, s_ref, b_ref, h_ref, *rest):
        xv = x_ref[...]
        r = lax.rsqrt(jnp.mean(xv * xv, axis=1, keepdims=True) + EPS)
        hv = xv * r * g_ref[...] * (1.0 + s_ref[...]) + b_ref[...]
        h_ref[...] = hv.astype(BF16)
        if dils:
            scr = rest[len(dils)]
            _stage(scr, hv)
            for dil, p_ref in zip(dils, rest[:len(dils)]):
                for res in range(dil):
                    p_ref[res] = _get_residue(scr, res, dil).astype(BF16)

    return pl.pallas_call(
        kern, name=name, grid=(S // tm,),
        out_shape=[jax.ShapeDtypeStruct((S, D), BF16)] + [jax.ShapeDtypeStruct((dil, S // dil, D), BF16) for dil in dils],
        in_specs=[_row_spec(tm, D), _vec_spec(D), _vec_spec(D), _vec_spec(D)],
        out_specs=[_row_spec(tm, D)] + [_perm_spec(dil, tm, D) for dil in dils],
        scratch_shapes=[_stage_shape(tm, D)] if dils else [],
        compiler_params=_params(("parallel",)),
    )(x, g, s, b)


def _gate_part(dx, nxt_refs, coef, df_ref, cs_ref):
    f_ref, gate_ref = nxt_refs
    df_ref[...] = ((coef * gate_ref[...]) * dx).astype(BF16)
    cs_ref[4:5, :] += coef * jnp.sum(f_ref[...].astype(F32) * dx, axis=0, keepdims=True)


def _norm_mod_bwd(x, dh_nat, dh_perm, dxo, g, s, name, nxt=None):
    S, D = x.shape
    tm = _pick(S, [512, 256, 128])
    n = S // tm
    n_nat, n_perm = len(dh_nat), len(dh_perm)
    n_nxt = 0 if nxt is None else 2

    def kern(*refs):
        x_ref = refs[0]
        nat = refs[1:1 + n_nat]
        perm = refs[1 + n_nat:1 + n_nat + n_perm]
        base = 1 + n_nat + n_perm
        dxo_ref, g_ref, s_ref = refs[base:base + 3]
        nxt_refs = refs[base + 3:base + 3 + n_nxt]
        dx_ref, cs_ref = refs[base + 3 + n_nxt:base + 5 + n_nxt]
        rest = refs[base + 5 + n_nxt:]
        df_ref = rest[0] if nxt is not None else None
        scr = rest[1:] if nxt is not None else rest
        i = pl.program_id(0)
        xv = x_ref[...]
        r = lax.rsqrt(jnp.mean(xv * xv, axis=1, keepdims=True) + EPS)
        xn = xv * r
        dh_v = nat[0][...].astype(F32)
        for t in nat[1:]:
            dh_v = dh_v + t[...].astype(F32)
        for (dil, _), p_ref, sc in zip(dh_perm, perm, scr):
            for res in range(dil):
                _put_residue(sc, res, dil, p_ref[res])
            dh_v = dh_v + _unstage(sc)
        one_s = 1.0 + s_ref[...]
        dxn = dh_v * (g_ref[...] * one_s)
        dx = dxo_ref[...] + r * (dxn - xn * jnp.mean(xn * dxn, axis=1, keepdims=True))
        dx_ref[...] = dx

        @pl.when(i == 0)
        def _():
            cs_ref[...] = jnp.zeros_like(cs_ref)

        cs_ref[0:1, :] += jnp.sum(dh_v, axis=0, keepdims=True)
        cs_ref[1:2, :] += jnp.sum(dh_v * xn, axis=0, keepdims=True)
        if nxt is not None:
            _gate_part(dx, nxt_refs, nxt[2], df_ref, cs_ref)

        @pl.when(i == n - 1)
        def _():
            t = cs_ref[1:2, :]
            cs_ref[2:3, :] = g_ref[...] * t
            cs_ref[3:4, :] = one_s * t

    nxt_specs = [] if nxt is None else [_row_spec(tm, D), _vec_spec(D)]
    nxt_args = [] if nxt is None else [nxt[0], nxt[1]]
    return pl.pallas_call(
        kern, name=name, grid=(n,),
        out_shape=[jax.ShapeDtypeStruct((S, D), F32), jax.ShapeDtypeStruct((8, D), F32)]
        + ([] if nxt is None else [jax.ShapeDtypeStruct((S, D), BF16)]),
        in_specs=[_row_spec(tm, D)] + [_row_spec(tm, D)] * n_nat + [_perm_spec(dil, tm, D) for dil, _ in dh_perm]
        + [_row_spec(tm, D), _vec_spec(D), _vec_spec(D)] + nxt_specs,
        out_specs=[_row_spec(tm, D), _vec_spec(D, 8)] + ([] if nxt is None else [_row_spec(tm, D)]),
        scratch_shapes=[_stage_shape(tm, D) for _ in dh_perm],
        compiler_params=_params(("arbitrary",)),
    )(x, *dh_nat, *[a for _, a in dh_perm], dxo, g, s, *nxt_args)


def _loss_head(x, g, target, nxt, name):
    S, D = x.shape
    tm = _pick(S, [512, 256, 128])
    n = S // tm

    def kern(x_ref, g_ref, t_ref, f_ref, gate_ref, dx_ref, cs_ref, df_ref):
        i = pl.program_id(0)
        xv = x_ref[...]
        r = lax.rsqrt(jnp.mean(xv * xv, axis=1, keepdims=True) + EPS)
        xn = xv * r
        e = xn * g_ref[...] - t_ref[...]
        dxn = (e * (1.0 / D)) * g_ref[...]
        dx = r * (dxn - xn * jnp.mean(xn * dxn, axis=1, keepdims=True))
        dx_ref[...] = dx

        @pl.when(i == 0)
        def _():
            cs_ref[...] = jnp.zeros_like(cs_ref)

        cs_ref[0:1, :] += jnp.sum(xn * e, axis=0, keepdims=True) * (1.0 / D)
        cs_ref[1:2, :] += jnp.sum(e * e, axis=0, keepdims=True)
        _gate_part(dx, (f_ref, gate_ref), nxt[2], df_ref, cs_ref)

        @pl.when(i == n - 1)
        def _():
            tot = jnp.sum(cs_ref[1:2, :], axis=1, keepdims=True) * (0.5 / D)
            cs_ref[2:3, :] = jnp.broadcast_to(tot, (1, D))

    return pl.pallas_call(
        kern, name=name, grid=(n,),
        out_shape=[jax.ShapeDtypeStruct((S, D), F32), jax.ShapeDtypeStruct((8, D), F32),
                   jax.ShapeDtypeStruct((S, D), BF16)],
        in_specs=[_row_spec(tm, D), _vec_spec(D), _row_spec(tm, D), _row_spec(tm, D), _vec_spec(D)],
        out_specs=[_row_spec(tm, D), _vec_spec(D, 8), _row_spec(tm, D)],
        compiler_params=_params(("arbitrary",)),
    )(x, g, target, nxt[0], nxt[1])


def _shift_down(p, row, prev_rows):
    a, b = prev_rows
    p1 = jnp.where(row == 0, b, pltpu.roll(p, 1, 0))
    p2 = jnp.where(row == 0, a, jnp.where(row == 1, b, pltpu.roll(p, 2, 0)))
    return p1, p2


def _conv_fwd(cg, conv_w, name):
    S, D5 = cg.shape
    D = D5 // 5
    tm = _pick(S, [512, 256, 128])
    t8 = tm // 8

    def prev(col):
        return pl.BlockSpec((8, D), lambda i: (jnp.maximum(i * t8 - 1, 0), col))

    def kern(cb_ref, cc_ref, ch_ref, ccp_ref, chp_ref, w_ref, y_ref):
        i = pl.program_id(0)
        keep = jnp.where(i > 0, 1.0, 0.0)
        p = cc_ref[...].astype(F32) * ch_ref[...].astype(F32)
        pa = ccp_ref[6:7, :].astype(F32) * chp_ref[6:7, :].astype(F32) * keep
        pb = ccp_ref[7:8, :].astype(F32) * chp_ref[7:8, :].astype(F32) * keep
        row = lax.broadcasted_iota(jnp.int32, (tm, D), 0)
        p1, p2 = _shift_down(p, row, (pa, pb))
        dw = w_ref[0:1, :] * p2 + w_ref[1:2, :] * p1 + w_ref[2:3, :] * p
        y_ref[...] = (cb_ref[...].astype(F32) * dw).astype(BF16)

    def col(cidx):
        return pl.BlockSpec((tm, D), lambda i: (i, cidx))

    return pl.pallas_call(
        kern, name=name, grid=(S // tm,),
        out_shape=jax.ShapeDtypeStruct((S, D), BF16),
        in_specs=[col(0), col(1), col(2), prev(1), prev(2), _vec_spec(D, CONV_K)],
        out_specs=_row_spec(tm, D),
        compiler_params=_params(("parallel",)),
    )(cg, cg, cg, cg, cg, conv_w)


def _conv_bwd(cg, dy, dgg, conv_w, name):
    S, D5 = cg.shape
    D = D5 // 5
    tm = _pick(S, [512, 256, 128])
    t8 = tm // 8
    n = S // tm
    last8 = S // 8 - 1

    def prev(col):
        return pl.BlockSpec((8, D), lambda i: (jnp.maximum(i * t8 - 1, 0), col))

    def nxt(col):
        return pl.BlockSpec((8, D), lambda i: (jnp.minimum((i + 1) * t8, last8), col))

    def kern(cb_ref, cc_ref, ch_ref, dy_ref, dgg_ref, ccp_ref, chp_ref, cbn_ref, dyn_ref, w_ref, d_ref, cs_ref):
        i = pl.program_id(0)
        keep_p = jnp.where(i > 0, 1.0, 0.0)
        keep_n = jnp.where(i < n - 1, 1.0, 0.0)
        cb = cb_ref[...].astype(F32)
        cc = cc_ref[...].astype(F32)
        ch = ch_ref[...].astype(F32)
        dyv = dy_ref[...].astype(F32)
        p = cc * ch
        pa = ccp_ref[6:7, :].astype(F32) * chp_ref[6:7, :].astype(F32) * keep_p
        pb = ccp_ref[7:8, :].astype(F32) * chp_ref[7:8, :].astype(F32) * keep_p
        row = lax.broadcasted_iota(jnp.int32, (tm, D), 0)
        p1, p2 = _shift_down(p, row, (pa, pb))
        w0, w1, w2 = w_ref[0:1, :], w_ref[1:2, :], w_ref[2:3, :]
        dw = w0 * p2 + w1 * p1 + w2 * p
        ddw = dyv * cb
        na = dyn_ref[0:1, :].astype(F32) * cbn_ref[0:1, :].astype(F32) * keep_n
        nb = dyn_ref[1:2, :].astype(F32) * cbn_ref[1:2, :].astype(F32) * keep_n
        u1 = jnp.where(row == tm - 1, na, pltpu.roll(ddw, tm - 1, 0))
        u2 = jnp.where(row == tm - 2, na, jnp.where(row == tm - 1, nb, pltpu.roll(ddw, tm - 2, 0)))
        dp = w2 * ddw + w1 * u1 + w0 * u2
        d_ref[:, 0:D] = (dyv * dw).astype(BF16)
        d_ref[:, D:2 * D] = (dp * ch).astype(BF16)
        d_ref[:, 2 * D:3 * D] = (dp * cc).astype(BF16)
        d_ref[:, 3 * D:5 * D] = dgg_ref[...]

        @pl.when(i == 0)
        def _():
            cs_ref[...] = jnp.zeros_like(cs_ref)

        cs_ref[0:1, :] += jnp.sum(ddw * p2, axis=0, keepdims=True)
        cs_ref[1:2, :] += jnp.sum(ddw * p1, axis=0, keepdims=True)
        cs_ref[2:3, :] += jnp.sum(ddw * p, axis=0, keepdims=True)

    def col(cidx):
        return pl.BlockSpec((tm, D), lambda i: (i, cidx))

    return pl.pallas_call(
        kern, name=name, grid=(n,),
        out_shape=[jax.ShapeDtypeStruct((S, 5 * D), BF16), jax.ShapeDtypeStruct((8, D), F32)],
        in_specs=[col(0), col(1), col(2), _row_spec(tm, D), _row_spec(tm, 2 * D), prev(1), prev(2), nxt(0),
                  pl.BlockSpec((8, D), lambda i: (jnp.minimum((i + 1) * t8, last8), 0)), _vec_spec(D, CONV_K)],
        out_specs=[_row_spec(tm, 5 * D), _vec_spec(D, 8)],
        compiler_params=_params(("arbitrary",)),
    )(cg, cg, cg, dy, dgg, cg, cg, cg, dy, conv_w)


def _merge_fwd(cg, yc, ya, name):
    S, D = yc.shape
    tm = _pick(S, [512, 256, 128])

    def kern(gc_ref, ga_ref, yc_ref, ya_ref, m_ref):
        m_ref[...] = (jax.nn.sigmoid(gc_ref[...].astype(F32)) * yc_ref[...].astype(F32)
                      + jax.nn.sigmoid(ga_ref[...].astype(F32)) * ya_ref[...].astype(F32)).astype(BF16)

    return pl.pallas_call(
        kern, name=name, grid=(S // tm,),
        out_shape=jax.ShapeDtypeStruct((S, D), BF16),
        in_specs=[pl.BlockSpec((tm, D), lambda i: (i, 3)), pl.BlockSpec((tm, D), lambda i: (i, 4)),
                  _row_spec(tm, D), _row_spec(tm, D)],
        out_specs=_row_spec(tm, D),
        compiler_params=_params(("parallel",)),
    )(cg, cg, yc, ya)


def _merge_bwd(cg, yc, ya, dm, name):
    S, D = yc.shape
    tm = _pick(S, [512, 256, 128])

    def kern(gc_ref, ga_ref, yc_ref, ya_ref, dm_ref, dyc_ref, dya_ref, dg_ref):
        dmv = dm_ref[...].astype(F32)
        sc = jax.nn.sigmoid(gc_ref[...].astype(F32))
        sa = jax.nn.sigmoid(ga_ref[...].astype(F32))
        dyc_ref[...] = (dmv * sc).astype(BF16)
        dya_ref[...] = (dmv * sa).astype(BF16)
        dg_ref[:, 0:D] = (dmv * yc_ref[...].astype(F32) * (sc * (1.0 - sc))).astype(BF16)
        dg_ref[:, D:2 * D] = (dmv * ya_ref[...].astype(F32) * (sa * (1.0 - sa))).astype(BF16)

    return pl.pallas_call(
        kern, name=name, grid=(S // tm,),
        out_shape=[jax.ShapeDtypeStruct((S, D), BF16), jax.ShapeDtypeStruct((S, D), BF16),
                   jax.ShapeDtypeStruct((S, 2 * D), BF16)],
        in_specs=[pl.BlockSpec((tm, D), lambda i: (i, 3)), pl.BlockSpec((tm, D), lambda i: (i, 4)),
                  _row_spec(tm, D), _row_spec(tm, D), _row_spec(tm, D)],
        out_specs=[_row_spec(tm, D), _row_spec(tm, D), pl.BlockSpec((tm, 2 * D), lambda i: (i, 0))],
        compiler_params=_params(("parallel",)),
    )(cg, cg, yc, ya, dm)


def _t5_bucket(dist):
    exact = NUM_BUCKETS // 2
    d = np.maximum(dist, 1).astype(np.float32)
    large = exact + (np.log(d / exact) / np.log(MAX_DISTANCE / exact) * (NUM_BUCKETS - exact)).astype(np.int32)
    large = np.minimum(large, NUM_BUCKETS - 1)
    return np.where(dist < exact, dist, large).astype(np.int32)


def _bucket_tables():
    i = np.arange(BLOCK)[:, None]
    j = np.arange(2 * BLOCK)[None, :]
    rel = i - j + BLOCK
    return np.stack([_t5_bucket(np.maximum(rel, 0) * d) for _, d in DILATION_GROUPS]).astype(np.int32)


def _band_masks():
    i = lax.broadcasted_iota(jnp.int32, (BLOCK, 2 * BLOCK), 0)
    j = lax.broadcasted_iota(jnp.int32, (BLOCK, 2 * BLOCK), 1)
    rel = i - j + BLOCK
    band = (rel >= 0) & (rel <= BLOCK)
    return band, band & (j >= BLOCK)


def _bias_build(rel_bias, buckets, name):
    def kern(rb_ref, bk_ref, o_ref):
        g = pl.program_id(0)
        bk = bk_ref[0]
        band, first = _band_masks()
        for h in range(HEADS_PER_GROUP):
            acc = jnp.zeros((BLOCK, 2 * BLOCK), F32)
            for b in range(NUM_BUCKETS):
                acc = jnp.where(bk == b, rb_ref[b, g * HEADS_PER_GROUP + h], acc)
            o_ref[0, 0, h] = jnp.where(first, acc, NEG_INF)
            o_ref[0, 1, h] = jnp.where(band, acc, NEG_INF)

    return pl.pallas_call(
        kern, name=name, grid=(N_GROUPS,),
        out_shape=jax.ShapeDtypeStruct((N_GROUPS, 2, HEADS_PER_GROUP, BLOCK, 2 * BLOCK), F32),
        in_specs=[pl.BlockSpec(memory_space=pltpu.SMEM),
                  pl.BlockSpec((1, BLOCK, 2 * BLOCK), lambda g: (g, 0, 0))],
        out_specs=pl.BlockSpec((1, 2, HEADS_PER_GROUP, BLOCK, 2 * BLOCK), lambda g: (g, 0, 0, 0, 0)),
        compiler_params=_params(("parallel",)),
    )(rel_bias, buckets)


def _bias_bwd(dlog, buckets, name):
    def kern(dl_ref, bk_ref, o_ref):
        g = pl.program_id(0)
        bk = bk_ref[0]
        rowi = lax.broadcasted_iota(jnp.int32, (NUM_BUCKETS, 128), 0)
        coli = lax.broadcasted_iota(jnp.int32, (NUM_BUCKETS, 128), 1)

        @pl.when(g == 0)
        def _():
            o_ref[...] = jnp.zeros_like(o_ref)

        acc = jnp.zeros((NUM_BUCKETS, 128), F32)
        for h in range(HEADS_PER_GROUP):
            dv = dl_ref[0, h]
            for b in range(NUM_BUCKETS):
                t = jnp.sum(jnp.where(bk == b, dv, 0.0), axis=0, keepdims=True)
                t = jnp.sum(t, axis=1, keepdims=True)
                acc = acc + jnp.where((rowi == b) & (coli == g * HEADS_PER_GROUP + h), t, 0.0)
        o_ref[...] += acc

    return pl.pallas_call(
        kern, name=name, grid=(N_GROUPS,),
        out_shape=jax.ShapeDtypeStruct((NUM_BUCKETS, 128), F32),
        in_specs=[pl.BlockSpec((1, HEADS_PER_GROUP, BLOCK, 2 * BLOCK), lambda g: (g, 0, 0, 0)),
                  pl.BlockSpec((1, BLOCK, 2 * BLOCK), lambda g: (g, 0, 0))],
        out_specs=pl.BlockSpec((NUM_BUCKETS, 128), lambda g: (0, 0)),
        compiler_params=_params(("arbitrary",)),
    )(dlog, buckets)


def _head_masks():
    lane = lax.broadcasted_iota(jnp.int32, (BLOCK, 128), 1)
    lo = lane < HEAD_DIM
    return lo, jnp.logical_not(lo)


def _attn_fwd(qkv, bias, d, name):
    S = qkv.shape[0]
    nb = S // d // BLOCK

    def kern(q_ref, kp_ref, kc_ref, vp_ref, vc_ref, b_ref, o_ref, lse_ref):
        lo, hi = _head_masks()
        for p in range(HEADS_PER_GROUP // 2):
            sl = slice(128 * p, 128 * (p + 1))
            q = q_ref[:, sl]
            k = jnp.concatenate([kp_ref[:, sl], kc_ref[:, sl]], axis=0)
            v = jnp.concatenate([vp_ref[:, sl], vc_ref[:, sl]], axis=0)
            o2, l2 = [], []
            for hh, msk in enumerate((lo, hi)):
                qm = jnp.where(msk, q, jnp.zeros_like(q))
                s = lax.dot_general(qm, k, NT, preferred_element_type=F32) * SCALE + b_ref[0, 2 * p + hh]
                m = jnp.max(s, axis=1, keepdims=True)
                e = jnp.exp(s - m)
                l = jnp.sum(e, axis=1, keepdims=True)
                o2.append(lax.dot_general(e.astype(BF16), v, NN, preferred_element_type=F32) / l)
                l2.append(jnp.broadcast_to(m + jnp.log(l), (BLOCK, 128)))
            o_ref[:, sl] = jnp.where(lo, o2[0], o2[1])
            lse_ref[:, sl] = jnp.where(lo, l2[0], l2[1])

    def blk(col, prev):
        if prev:
            return pl.BlockSpec((BLOCK, ATTN_OUT), lambda r, n: (r * nb + jnp.maximum(n - 1, 0), col))
        return pl.BlockSpec((BLOCK, ATTN_OUT), lambda r, n: (r * nb + n, col))

    o_spec = pl.BlockSpec((BLOCK, ATTN_OUT), lambda r, n: (r * nb + n, 0))
    return pl.pallas_call(
        kern, name=name, grid=(d, nb),
        out_shape=[jax.ShapeDtypeStruct((S, ATTN_OUT), F32)] * 2,
        in_specs=[blk(0, False), blk(1, True), blk(1, False), blk(2, True), blk(2, False),
                  pl.BlockSpec((1, HEADS_PER_GROUP, BLOCK, 2 * BLOCK), lambda r, n: (jnp.minimum(n, 1), 0, 0, 0))],
        out_specs=[o_spec, o_spec],
        compiler_params=_params(("parallel", "arbitrary")),
    )(qkv, qkv, qkv, qkv, qkv, bias)


def _attn_bwd(qkv, do, lse, delta, bias, d, name):
    S = qkv.shape[0]
    nb = S // d // BLOCK
    low = -3.0e38

    def kern(q_ref, kp_ref, kc_ref, vp_ref, vc_ref, do_ref, lse_ref, dl_ref, b_ref,
             dq_ref, dkv_ref, db_ref, ck_ref, cv_ref):
        r, n = pl.program_id(0), pl.program_id(1)

        @pl.when((r == 0) & (n == 0))
        def _():
            db_ref[...] = jnp.zeros_like(db_ref)

        @pl.when(n == 0)
        def _():
            ck_ref[...] = jnp.zeros_like(ck_ref)
            cv_ref[...] = jnp.zeros_like(cv_ref)

        @pl.when(n < nb)
        def _():
            lo, hi = _head_masks()
            for p in range(HEADS_PER_GROUP // 2):
                sl = slice(128 * p, 128 * (p + 1))
                sv = slice(ATTN_OUT + 128 * p, ATTN_OUT + 128 * (p + 1))
                q = q_ref[:, sl]
                k = jnp.concatenate([kp_ref[:, sl], kc_ref[:, sl]], axis=0)
                v = jnp.concatenate([vp_ref[:, sl], vc_ref[:, sl]], axis=0)
                dov = do_ref[:, sl]
                lse_b = lse_ref[:, sl]
                del_b = dl_ref[:, sl]
                dq2 = []
                dk_acc = jnp.zeros((2 * BLOCK, 128), F32)
                dv_acc = jnp.zeros((2 * BLOCK, 128), F32)
                for hh, msk in enumerate((lo, hi)):
                    qm = jnp.where(msk, q, jnp.zeros_like(q))
                    dom = jnp.where(msk, dov, jnp.zeros_like(dov))
                    lse_h = jnp.max(jnp.where(msk, lse_b, low), axis=1, keepdims=True)
                    del_h = jnp.max(jnp.where(msk, del_b, low), axis=1, keepdims=True)
                    s = lax.dot_general(qm, k, NT, preferred_element_type=F32) * SCALE + b_ref[0, 2 * p + hh]
                    pr = jnp.exp(s - lse_h)
                    dp = lax.dot_general(dom, v, NT, preferred_element_type=F32)
                    ds = pr * (dp - del_h)
                    db_ref[2 * p + hh] += ds
                    dsb = (ds * SCALE).astype(BF16)
                    dq2.append(lax.dot_general(dsb, k, NN, preferred_element_type=F32))
                    dk_acc = dk_acc + lax.dot_general(dsb, qm, TN, preferred_element_type=F32)
                    dv_acc = dv_acc + lax.dot_general(pr.astype(BF16), dom, TN, preferred_element_type=F32)
                dq_ref[:, sl] = jnp.where(lo, dq2[0], dq2[1]).astype(BF16)
                dkv_ref[:, sl] = (ck_ref[:, sl] + dk_acc[0:BLOCK]).astype(BF16)
                dkv_ref[:, sv] = (cv_ref[:, sl] + dv_acc[0:BLOCK]).astype(BF16)
                ck_ref[:, sl] = dk_acc[BLOCK:2 * BLOCK]
                cv_ref[:, sl] = dv_acc[BLOCK:2 * BLOCK]

        @pl.when(n == nb)
        def _():
            dkv_ref[:, 0:ATTN_OUT] = ck_ref[...].astype(BF16)
            dkv_ref[:, ATTN_OUT:2 * ATTN_OUT] = cv_ref[...].astype(BF16)

    def cur(n):
        return jnp.minimum(n, nb - 1)

    def blk(col, prev):
        if prev:
            return pl.BlockSpec((BLOCK, ATTN_OUT), lambda r, n: (r * nb + jnp.maximum(cur(n) - 1, 0), col))
        return pl.BlockSpec((BLOCK, ATTN_OUT), lambda r, n: (r * nb + cur(n), col))

    q_like = pl.BlockSpec((BLOCK, ATTN_OUT), lambda r, n: (r * nb + cur(n), 0))
    return pl.pallas_call(
        kern, name=name, grid=(d, nb + 1),
        out_shape=[jax.ShapeDtypeStruct((S, ATTN_OUT), BF16), jax.ShapeDtypeStruct((S, 2 * ATTN_OUT), BF16),
                   jax.ShapeDtypeStruct((HEADS_PER_GROUP, BLOCK, 2 * BLOCK), F32)],
        in_specs=[blk(0, False), blk(1, True), blk(1, False), blk(2, True), blk(2, False),
                  q_like, q_like, q_like,
                  pl.BlockSpec((1, HEADS_PER_GROUP, BLOCK, 2 * BLOCK),
                               lambda r, n: (jnp.minimum(cur(n), 1), 0, 0, 0))],
        out_specs=[q_like,
                   pl.BlockSpec((BLOCK, 2 * ATTN_OUT), lambda r, n: (r * nb + jnp.maximum(n - 1, 0), 0)),
                   pl.BlockSpec((HEADS_PER_GROUP, BLOCK, 2 * BLOCK), lambda r, n: (0, 0, 0))],
        scratch_shapes=[pltpu.VMEM((BLOCK, ATTN_OUT), F32), pltpu.VMEM((BLOCK, ATTN_OUT), F32)],
        compiler_params=_params(("arbitrary", "arbitrary")),
    )(qkv, qkv, qkv, qkv, qkv, do, lse, delta, bias)


def _by_residue(a, dil):
    return a if dil == 1 else a.reshape(dil, a.shape[0] // dil, a.shape[1])


def _flat(a):
    return a if a.ndim == 2 else a.reshape(a.shape[0] * a.shape[1], a.shape[2])


def _combine_fwd(os_, lses, name):
    S, W = os_[0].shape
    tm = _pick(S, [512, 256, 128])
    perm = [dil for dil in DILS if dil > 1]

    def kern(*refs):
        o_in, l_in = refs[0:N_GROUPS], refs[N_GROUPS:2 * N_GROUPS]
        of_ref, ob_ref, lse_ref = refs[2 * N_GROUPS:2 * N_GROUPS + 3]
        lse_p = refs[2 * N_GROUPS + 3:2 * N_GROUPS + 3 + len(perm)]
        scr = refs[2 * N_GROUPS + 3 + len(perm):]
        ov, lv = [], []
        si = 0
        for g, dil in enumerate(DILS):
            if dil == 1:
                ov.append(o_in[g][...])
                lv.append(l_in[g][...])
            else:
                so, sl = scr[si], scr[si + 1]
                si += 2
                for res in range(dil):
                    _put_residue(so, res, dil, o_in[g][res])
                    _put_residue(sl, res, dil, l_in[g][res])
                ov.append(_unstage(so))
                lv.append(_unstage(sl))
        m = jnp.maximum(jnp.maximum(lv[0], lv[1]), lv[2])
        e = [jnp.exp(t - m) for t in lv]
        tot = e[0] + e[1] + e[2]
        o = (e[0] * ov[0] + e[1] * ov[1] + e[2] * ov[2]) / tot
        lse = m + jnp.log(tot)
        of_ref[...] = o
        ob_ref[...] = o.astype(BF16)
        lse_ref[...] = lse
        sl = scr[1]
        _stage(sl, lse)
        for dil, p_ref in zip(perm, lse_p):
            for res in range(dil):
                p_ref[res] = _get_residue(sl, res, dil)

    def in_spec(dil):
        return _row_spec(tm, W) if dil == 1 else _perm_spec(dil, tm, W)

    ins = [_by_residue(a, dil) for a, dil in zip(os_, DILS)] + [_by_residue(a, dil) for a, dil in zip(lses, DILS)]
    return pl.pallas_call(
        kern, name=name, grid=(S // tm,),
        out_shape=[jax.ShapeDtypeStruct((S, W), F32), jax.ShapeDtypeStruct((S, W), BF16),
                   jax.ShapeDtypeStruct((S, W), F32)]
        + [jax.ShapeDtypeStruct((dil, S // dil, W), F32) for dil in perm],
        in_specs=[in_spec(dil) for dil in DILS] * 2,
        out_specs=[_row_spec(tm, W)] * 3 + [_perm_spec(dil, tm, W) for dil in perm],
        scratch_shapes=[_stage_shape(tm, W) for _ in range(2 * len(perm))],
        compiler_params=_params(("parallel",)),
    )(*ins)


def _delta(do, o, name):
    S, W = o.shape
    tm = _pick(S, [512, 256, 128])
    perm = [dil for dil in DILS if dil > 1]

    def kern(do_ref, o_ref, dob_ref, d_ref, *rest):
        scr, scr_do = rest[2 * len(perm)], rest[2 * len(perm) + 1]
        prod = do_ref[...] * o_ref[...]
        ri = jnp.right_shift(lax.broadcasted_iota(jnp.int32, (W, W), 0), HEAD_SHIFT)
        ci = jnp.right_shift(lax.broadcasted_iota(jnp.int32, (W, W), 1), HEAD_SHIFT)
        same = jnp.where(ri == ci, 1.0, 0.0).astype(BF16)
        hi_p = prod.astype(BF16)
        lo_p = (prod - hi_p.astype(F32)).astype(BF16)
        dl = (lax.dot_general(hi_p, same, NN, preferred_element_type=F32)
              + lax.dot_general(lo_p, same, NN, preferred_element_type=F32))
        d_ref[...] = dl
        dob_ref[...] = do_ref[...].astype(BF16)
        _stage(scr, dl)
        _stage(scr_do, do_ref[...])
        for j, dil in enumerate(perm):
            for res in range(dil):
                rest[2 * j][res] = _get_residue(scr_do, res, dil).astype(BF16)
                rest[2 * j + 1][res] = _get_residue(scr, res, dil)

    out_shape = [jax.ShapeDtypeStruct((S, W), BF16), jax.ShapeDtypeStruct((S, W), F32)]
    out_specs = [_row_spec(tm, W), _row_spec(tm, W)]
    for dil in perm:
        out_shape += [jax.ShapeDtypeStruct((dil, S // dil, W), BF16), jax.ShapeDtypeStruct((dil, S // dil, W), F32)]
        out_specs += [_perm_spec(dil, tm, W), _perm_spec(dil, tm, W)]
    return pl.pallas_call(
        kern, name=name, grid=(S // tm,),
        out_shape=out_shape,
        in_specs=[_row_spec(tm, W), _row_spec(tm, W)], out_specs=out_specs,
        scratch_shapes=[_stage_shape(tm, W), _stage_shape(tm, W)],
        compiler_params=_params(("parallel",)),
    )(do, o)


def _ada_fwd(c16, ada_w, name):
    depth, D, n = ada_w.shape
    rows = 2 * N_DEV

    def kern(c_ref, w_ref, o_ref, cs_ref):
        cv = c_ref[...]
        cs = cv * jax.nn.sigmoid(cv)
        cs_ref[...] = cs
        o_ref[0] = _dot3(cs, w_ref[0], NN)

    return pl.pallas_call(
        kern, name=name, grid=(depth,),
        out_shape=[jax.ShapeDtypeStruct((depth, rows, n), F32), jax.ShapeDtypeStruct((rows, D), F32)],
        in_specs=[pl.BlockSpec((rows, D), lambda l: (0, 0)), pl.BlockSpec((1, D, n), lambda l: (l, 0, 0))],
        out_specs=[pl.BlockSpec((1, rows, n), lambda l: (l, 0, 0)), pl.BlockSpec((rows, D), lambda l: (0, 0))],
        compiler_params=_params(("arbitrary",)),
    )(c16, ada_w)


def _ada_bwd(cs16, dm16, name):
    depth, _, n = dm16.shape
    D = cs16.shape[1]

    def kern(cs_ref, dm_ref, o_ref):
        o_ref[0] = _dot3(cs_ref[...], dm_ref[0], TN)

    return pl.pallas_call(
        kern, name=name, grid=(depth,),
        out_shape=jax.ShapeDtypeStruct((depth, D, n), F32),
        in_specs=[pl.BlockSpec((2 * N_DEV, D), lambda l: (0, 0)), pl.BlockSpec((1, 2 * N_DEV, n), lambda l: (l, 0, 0))],
        out_specs=pl.BlockSpec((1, D, n), lambda l: (l, 0, 0)),
        compiler_params=_params(("parallel",)),
    )(cs16, dm16)


def _sum_rows8(parts, name):
    _, r, n = parts.shape

    def kern(p_ref, o_ref):
        acc = p_ref[0]
        for k in range(1, N_DEV):
            acc = acc + p_ref[k]
        o_ref[...] = acc

    return pl.pallas_call(
        kern, name=name, out_shape=jax.ShapeDtypeStruct((r, n), F32),
        in_specs=[pl.BlockSpec(memory_space=pltpu.VMEM)], out_specs=pl.BlockSpec(memory_space=pltpu.VMEM),
    )(parts)


def _adamw(w, g, m, v, name):
    shape = w.shape
    c = shape[-1]
    r = int(np.prod(shape[:-1])) if len(shape) > 1 else 1
    w2, g2, m2, v2 = (t.reshape(r, c) for t in (w, g, m, v))
    tr = r
    for cand in (2048, 1024, 512, 256, 128, 64, 32, 16, 8):
        if r % cand == 0 and cand * c * 4 <= (1 << 20):
            tr = cand
            break
    c1 = 1.0 - ADAM_B1 ** ADAM_STEP
    c2 = 1.0 - ADAM_B2 ** ADAM_STEP

    def kern(w_ref, g_ref, m_ref, v_ref, d_ref, nm_ref, nv_ref):
        gv = g_ref[...]
        nm = ADAM_B1 * m_ref[...] + (1.0 - ADAM_B1) * gv
        nv = ADAM_B2 * v_ref[...] + (1.0 - ADAM_B2) * (gv * gv)
        nm_ref[...] = nm
        nv_ref[...] = nv
        d_ref[...] = -ADAM_LR * ((nm / c1) / (jnp.sqrt(nv / c2) + ADAM_EPS) + ADAM_WD * w_ref[...])

    spec = pl.BlockSpec((tr, c), lambda i: (i, 0))
    outs = pl.pallas_call(
        kern, name=name, grid=(r // tr,),
        out_shape=[jax.ShapeDtypeStruct((r, c), F32)] * 3,
        in_specs=[spec] * 4, out_specs=[spec] * 3,
        compiler_params=_params(("parallel",)),
    )(w2, g2, m2, v2)
    return tuple(o.reshape(shape) for o in outs)


def _adamw_slab(w3, g, m3, v3, idx, prev, name):
    ns, r, c = w3.shape
    tr = r
    for cand in (2048, 1024, 512, 256, 128, 64, 32, 16, 8):
        if r % cand == 0 and cand * c * 4 <= (1 << 20):
            tr = cand
            break
    c1 = 1.0 - ADAM_B1 ** ADAM_STEP
    c2 = 1.0 - ADAM_B2 ** ADAM_STEP

    def kern(w_ref, g_ref, m_ref, v_ref, p0, p1, p2, p3, go_ref, d_ref, nm_ref, nv_ref):
        gv = g_ref[...]
        nm = ADAM_B1 * m_ref[0] + (1.0 - ADAM_B1) * gv
        nv = ADAM_B2 * v_ref[0] + (1.0 - ADAM_B2) * (gv * gv)
        go_ref[0] = gv
        nm_ref[0] = nm
        nv_ref[0] = nv
        d_ref[0] = -ADAM_LR * ((nm / c1) / (jnp.sqrt(nv / c2) + ADAM_EPS) + ADAM_WD * w_ref[0])

    if prev is None:
        prev = [lax.empty((ns, r, c), F32) for _ in range(4)]
    slab = pl.BlockSpec((1, tr, c), lambda i: (idx, i, 0))
    return pl.pallas_call(
        kern, name=name, grid=(r // tr,),
        out_shape=[jax.ShapeDtypeStruct((ns, r, c), F32)] * 4,
        in_specs=[slab, pl.BlockSpec((tr, c), lambda i: (i, 0)), slab, slab] + [pl.BlockSpec(memory_space=pl.ANY)] * 4,
        out_specs=[slab] * 4,
        input_output_aliases={4: 0, 5: 1, 6: 2, 7: 3},
        compiler_params=_params(("parallel",)),
    )(w3, g, m3, v3, *prev)


def kernel(x, c, ada_w, ada_b, norm_g, ffn_w_gate, ffn_w_up, ffn_w_down, w_in, conv_w, w_conv_out, w_attn_out, w_o, rel_bias, final_g, loss_target, m_ada_w, m_ada_b, m_norm_g, m_ffn_w_gate, m_ffn_w_up, m_ffn_w_down, m_w_in, m_conv_w, m_w_conv_out, m_w_attn_out, m_w_o, m_rel_bias, m_final_g, v_ada_w, v_ada_b, v_norm_g, v_ffn_w_gate, v_ffn_w_up, v_ffn_w_down, v_w_in, v_conv_w, v_w_conv_out, v_w_attn_out, v_w_o, v_rel_bias, v_final_g):
    depth = ada_w.shape[0]
    S, D = x.shape[1], x.shape[2]
    me = 4 * lax.axis_index("x") + 2 * lax.axis_index("y") + lax.axis_index("c")
    x0 = x.reshape(S, D)
    target = loss_target.reshape(S, D)
    fsh = ffn_w_down.shape[2]
    insh = w_in.shape[2]
    dsh = D // N_DEV
    ao_rows = dsh * ATTN_OUT // D

    piece_rows = [fsh] * 6 + [insh, dsh, dsh, ao_rows]

    FIRST, MIXER, SECOND = [0, 2, 4], [6, 7, 8, 9], [1, 3, 5]

    def rows_of(idx):
        return [piece_rows[p] for p in idx]

    def pack(l, idx=None):
        def t(a):
            return jnp.transpose(a).astype(BF16)
        ps = [t(ffn_w_gate[l, 0]), t(ffn_w_gate[l, 1]), t(ffn_w_up[l, 0]), t(ffn_w_up[l, 1]),
              ffn_w_down[l, 0].astype(BF16), ffn_w_down[l, 1].astype(BF16), t(w_in[l]),
              w_conv_out[l].astype(BF16), w_o[l].astype(BF16), t(w_attn_out[l]).reshape(ao_rows, D)]
        return jnp.concatenate(ps if idx is None else [ps[p] for p in idx], axis=0)

    def mixer_weights(full):
        in_t = full[6]
        qkv_t = [jnp.concatenate([in_t[t * QKV_W + g * ATTN_OUT: t * QKV_W + (g + 1) * ATTN_OUT] for t in range(3)])
                 for g in range(N_GROUPS)]
        ao_t = full[9].reshape(N_DEV, dsh, ATTN_OUT).reshape(D, ATTN_OUT)
        return dict(qkv_t=qkv_t, cg_t=in_t[3 * QKV_W:], co=full[7], wo=full[8], ao_t=ao_t)

    def behind(v, token):
        return v + token[0, 0].astype(v.dtype)

    c_all = _all_gather(c.reshape(D // 128, 128), "c_all_gather").reshape(N_DEV, D)
    c16 = jnp.concatenate([c_all, jnp.zeros_like(c_all)], axis=0)
    mod_part, cs16 = _ada_fwd(c16, ada_w, "ada_fwd")
    mod_part = mod_part[:, :N_DEV]
    n_ada = ada_w.shape[2]
    mod_all = _all_gather(mod_part.reshape(depth * N_DEV * n_ada // 128, 128), "mod_all_gather")
    mod_all = mod_all.reshape(N_DEV, depth, N_DEV, n_ada)
    mod_mine = lax.dynamic_index_in_dim(mod_all, me, axis=2, keepdims=False)
    mod = jnp.transpose(mod_mine, (1, 0, 2)).reshape(depth, N_DEV * n_ada) + ada_b
    mod = mod.reshape(depth, 3, 3, 1, D)

    small = jnp.concatenate([norm_g.reshape(-1), conv_w.reshape(-1)]).reshape(-1, 128)
    small_all = _all_gather(small, "small_all_gather").reshape(N_DEV, -1)
    n_ng = norm_g.size
    norm_g_full = jnp.transpose(small_all[:, :n_ng].reshape(N_DEV, depth, 3, dsh), (1, 2, 0, 3)).reshape(depth, 3, 1, D)
    conv_w_full = jnp.transpose(small_all[:, n_ng:].reshape(N_DEV, depth, CONV_K, dsh), (1, 2, 0, 3)).reshape(depth, CONV_K, D)

    buckets = jnp.asarray(_bucket_tables())
    bias = _bias_build(rel_bias, buckets, "bias_build")
    perm_dils = tuple(dil for dil in DILS if dil > 1)

    chain_done = mod.reshape(-1)[:128] + small_all.reshape(-1)[:128]
    part0 = [(FIRST, "first"), (MIXER, "mixer"), (SECOND, "second")]
    started, after0 = [], chain_done
    for idx, tag in part0:
        started.append(_gather_start(pack(0, idx), rows_of(idx), after0, f"weights_gather_start_l0_{tag}"))
        after0 = started[-1][3]
    full0 = [None] * len(piece_rows)

    def arrive0(k, after_arr):
        idx, tag = part0[k]
        st = started[k]
        fw = _gather_forward(st[0], st[1], st[2], rows_of(idx), after_arr, f"weights_gather_forward_l0_{tag}")
        for p, z in zip(idx, _gather_finish(fw[0], fw[1], fw[2], after_arr, f"weights_gather_finish_l0_{tag}")):
            full0[p] = z
        return fw[3]

    arrive0(0, bias)
    W = [dict(g_t=[full0[0]], u_t=[full0[2]], down=[full0[4]])] + [None] * (depth - 1)

    saved = []
    xc = x0
    for l in range(depth):
        sv = {}
        gather, tie_sub, pin = None, 0, None
        if 0 < l < depth - 1:
            gather = _gather_start(pack(l + 1), piece_rows, W[l]["wo"], f"weights_gather_start_l{l + 1}")
        for sub in (0, 1, 2):
            if l == 0 and sub == 1:
                arrive0(1, xc)
                W[0].update(mixer_weights(full0))
                if depth > 1:
                    gather, tie_sub = _gather_start(pack(1), piece_rows, W[0]["wo"], "weights_gather_start_l1"), 1
            if l == 0 and sub == 2:
                arrive0(2, xc)
                W[0].update(g_t=full0[0:2], u_t=full0[2:4], down=full0[4:6])
            g, sh, sc, gt = norm_g_full[l, sub], mod[l, sub, 0], mod[l, sub, 1], mod[l, sub, 2]
            if sub == tie_sub and gather is not None:
                g = behind(g, gather[3])
            if sub == 2 and pin is not None:
                g = behind(g, pin)
            rec = dict(x=xc)
            if sub != 1:
                i = 0 if sub == 0 else 1
                h = _norm_mod_fwd(xc, g, sc, sh, "norm_mod_fwd")[0]
                a, u, z = _ffn_up(h, W[l]["g_t"][i], W[l]["u_t"][i], "ffn_up")
                xc, f = _matmul(z, W[l]["down"][i], "nn", BF16, "ffn_down", tm=512, resid=(xc, gt, 0.5))
                rec.update(h=h, a=a, u=u, z=z, f=f)
            else:
                hs = _norm_mod_fwd(xc, g, sc, sh, "norm_mod_fwd_mixer", dils=perm_dils)
                h = hs[0]
                h_res = [h] + [_flat(t) for t in hs[1:]]
                cg = _matmul(h, W[l]["cg_t"], "nt", BF16, "mixer_cg")
                qkvs, os_, lses = [], [], []
                for gi, dil in enumerate(DILS):
                    qkv = _matmul(h_res[gi], W[l]["qkv_t"][gi], "nt", BF16, "mixer_qkv", tn=3 * ATTN_OUT)
                    o_g, lse_g = _attn_fwd(qkv, bias[gi], dil, f"attn_fwd_g{gi}")
                    qkvs.append(qkv)
                    os_.append(o_g)
                    lses.append(lse_g)
                comb = _combine_fwd(os_, lses, "combine_fwd")
                o_f, o_b, lse = comb[0:3]
                lse_res = [lse] + [_flat(t) for t in comb[3:]]
                yc_in = _conv_fwd(cg, conv_w_full[l], "conv_fwd")
                yc = _matmul(yc_in, W[l]["co"], "nn", BF16, "conv_out")
                ya = _matmul(o_b, W[l]["ao_t"], "nt", BF16, "attn_out")
                merged = _merge_fwd(cg, yc, ya, "merge_fwd")
                xc, f = _matmul(merged, W[l]["wo"], "nn", BF16, "mixer_out", resid=(xc, gt, 1.0))
                rec.update(h=h, h_res=h_res, qkvs=qkvs, cg=cg, o_f=o_f, o_b=o_b, lse_res=lse_res, yc_in=yc_in,
                           yc=yc, ya=ya, merged=merged, f=f)
                if gather is not None:
                    fwd = _gather_forward(gather[0], gather[1], gather[2], piece_rows, xc,
                                          f"weights_gather_forward_l{l + 1}")
                    pin = fwd[3]
            sv[sub] = rec
        if gather is not None:
            full = _gather_finish(fwd[0], fwd[1], fwd[2], xc, f"weights_gather_finish_l{l + 1}")
            W[l + 1] = dict(g_t=full[0:2], u_t=full[2:4], down=full[4:6], **mixer_weights(full))
        saved.append(sv)

    def gate_of(l, sub):
        return saved[l][sub]["f"], mod[l, sub, 2], (1.0 if sub == 1 else 0.5)

    def below(l, sub):
        if sub > 0:
            return gate_of(l, sub - 1)
        return gate_of(l - 1, 2) if l > 0 else None

    dx, head, df = _loss_head(xc, final_g.reshape(1, D), target, gate_of(depth - 1, 2), "loss_head")
    d_final_g = head[0]
    loss_part = head[2, 0]
    d_gate = head[4]

    d_mod = [[None] * 3 for _ in range(depth)]
    d_norm = [[None] * 3 for _ in range(depth)]
    d_conv = [None] * depth
    dlog = jnp.zeros((N_GROUPS, HEADS_PER_GROUP, BLOCK, 2 * BLOCK), F32)
    n_pieces = len(piece_rows)
    LATE = (0, 2, 4)
    EARLY = tuple(p for p in range(n_pieces) if p not in LATE)
    g_piece = [[None] * n_pieces for _ in range(depth)]

    piece_keys = (("g_t", 0), ("g_t", 1), ("u_t", 0), ("u_t", 1), ("down", 0), ("down", 1), "in_t", "co", "wo", "ao_t")

    def pieces_of(dW, idx):
        return [dW[piece_keys[p]].reshape(N_DEV * piece_rows[p], D) for p in idx]

    def finish_scatter(sc, idx, after_arr, layer, part=""):
        recv = _scatter_finish(sc[0], sc[1], sc[2], after_arr, f"grads_scatter_finish_l{layer}{part}")
        tot = _sum_sources(recv, "grads_sum")
        o = 0
        for p in idx:
            g_piece[layer][p] = tot[o:o + piece_rows[p]]
            o += piece_rows[p]
        return recv

    scatter = None
    early = None
    for l in reversed(range(depth)):
        dW = {}
        for sub in (2, 1, 0):
            rec = saved[l][sub]
            g, sc = norm_g_full[l, sub], mod[l, sub, 1]
            if sub == 2 and scatter is not None:
                df = behind(df, scatter[3])
            if sub == 0 and early is not None:
                df = behind(df, early[3])
            nxt = below(l, sub)
            if sub != 1:
                i = 0 if sub == 0 else 1
                dz = _matmul(df, W[l]["down"][i], "nt", BF16, "ffn_down_dx")
                dW["down", i] = _matmul(rec["z"], df, "tn", BF16, "ffn_down_dw")
                da, du, dh = _ffn_up_bwd(dz, rec["a"], rec["u"], W[l]["g_t"][i], W[l]["u_t"][i], "ffn_up_bwd")
                dW["g_t", i] = _matmul(da, rec["h"], "tn", BF16, "ffn_gate_dw")
                dW["u_t", i] = _matmul(du, rec["h"], "tn", BF16, "ffn_up_dw")
                res = _norm_mod_bwd(rec["x"], [dh], [], dx, g, sc, "norm_mod_bwd", nxt=nxt)
            else:
                dout = df
                dm = _matmul(dout, W[l]["wo"], "nt", BF16, "mixer_out_dx")
                dW["wo"] = _matmul(rec["merged"], dout, "tn", BF16, "mixer_out_dw")
                dyc, dya, dgg = _merge_bwd(rec["cg"], rec["yc"], rec["ya"], dm, "merge_bwd")
                dyc_in = _matmul(dyc, W[l]["co"], "nt", BF16, "conv_out_dx")
                dW["co"] = _matmul(rec["yc_in"], dyc, "tn", BF16, "conv_out_dw")
                do = _matmul(dya, W[l]["ao_t"], "nn", F32, "attn_out_dx")
                dW["ao_t"] = _matmul(dya, rec["o_b"], "tn", BF16, "attn_out_dw")
                dl = _delta(do, rec["o_f"], "attn_delta")
                do_res = [dl[0]] + [_flat(t) for t in dl[2::2]]
                del_res = [dl[1]] + [_flat(t) for t in dl[3::2]]
                dh_attn, dw_q, dw_kv, dlog_l = [], [], [], []
                for gi, dil in enumerate(DILS):
                    dq, dkv, dlg = _attn_bwd(rec["qkvs"][gi], do_res[gi], rec["lse_res"][gi], del_res[gi],
                                             bias[gi], dil, f"attn_bwd_g{gi}")
                    dlog_l.append(dlg)
                    dh_attn.append(_attn_dh(dq, dkv, W[l]["qkv_t"][gi], "attn_dh"))
                    dw_q.append(_matmul(dq, rec["h_res"][gi], "tn", BF16, "mixer_q_dw"))
                    dw_kv.append(_matmul(dkv, rec["h_res"][gi], "tn", BF16, "mixer_kv_dw"))
                dlog = dlog + jnp.stack(dlog_l)
                dcg, conv_sum = _conv_bwd(rec["cg"], dyc_in, dgg, conv_w_full[l], "conv_bwd")
                d_conv[l] = conv_sum[0:CONV_K]
                dh_cg = _matmul(dcg, W[l]["cg_t"], "nn", F32, "mixer_cg_dx")
                dw_cg = _matmul(dcg, rec["h"], "tn", BF16, "mixer_cg_dw")
                dW["in_t"] = jnp.concatenate(
                    dw_q + [t[:ATTN_OUT] for t in dw_kv] + [t[ATTN_OUT:] for t in dw_kv] + [dw_cg], axis=0)
                perm_parts = [(dil, _by_residue(dh_attn[gi], dil)) for gi, dil in enumerate(DILS) if dil > 1]
                res = _norm_mod_bwd(rec["x"], [dh_cg, dh_attn[0]], perm_parts, dx, g, sc, "norm_mod_bwd_mixer", nxt=nxt)
            dx, sums = res[0], res[1]
            d_mod[l][sub] = jnp.stack([sums[0], sums[2], d_gate])
            d_norm[l][sub] = sums[3]
            if nxt is not None:
                df, d_gate = res[2], sums[4]
            if l == 0 and sub == 1:
                after = dx
                if scatter is not None:
                    after = finish_scatter(scatter, range(n_pieces), dx, l + 1)
                    scatter = None
                early = _scatter_start(pieces_of(dW, EARLY), after, "grads_scatter_start_l0_early")
        if l > 0:
            after = dx
            if scatter is not None:
                after = finish_scatter(scatter, range(n_pieces), dx, l + 1)
            scatter = _scatter_start(pieces_of(dW, range(n_pieces)), after, f"grads_scatter_start_l{l}")
        else:
            late = _scatter_start(pieces_of(dW, LATE), dx, "grads_scatter_start_l0_late")
    grad_x = dx.reshape(1, S, D)
    d_rel = _bias_bwd(dlog, buckets, "bias_bwd")[:, :rel_bias.shape[1]]

    big = dict(
        ffn_w_gate=(ffn_w_gate, m_ffn_w_gate, v_ffn_w_gate, (0, 1), True, None),
        ffn_w_up=(ffn_w_up, m_ffn_w_up, v_ffn_w_up, (2, 3), True, None),
        ffn_w_down=(ffn_w_down, m_ffn_w_down, v_ffn_w_down, (4, 5), False, None),
        w_in=(w_in, m_w_in, v_w_in, (6,), True, None),
        w_conv_out=(w_conv_out, m_w_conv_out, v_w_conv_out, (7,), False, None),
        w_o=(w_o, m_w_o, v_w_o, (8,), False, None),
        w_attn_out=(w_attn_out, m_w_attn_out, v_w_attn_out, (9,), True, (dsh, ATTN_OUT)))
    big_out = {name: None for name in big}

    def adam_layer(l, token=None):
        for name, (w_, m_, v_, plist, transposed, shard_shape) in big.items():
            ns, r, cc = depth * len(plist), w_.shape[-2], w_.shape[-1]
            w3, m3, v3 = (t.reshape(ns, r, cc) for t in (w_, m_, v_))
            for j, p in enumerate(plist):
                gl = g_piece[l][p]
                if shard_shape is not None:
                    gl = gl.reshape(shard_shape)
                if transposed:
                    gl = jnp.transpose(gl)
                if token is not None:
                    gl = behind(gl, token)
                big_out[name] = _adamw_slab(w3, gl, m3, v3, l * len(plist) + j, big_out[name], "adamw_" + name)

    for l in range(depth - 1, 0, -1):
        adam_layer(l, late[3])
    done = [late[3]] + [st[1] for st in big_out.values() if st is not None]
    finish_scatter(early, EARLY, done, 0, "_early")
    finish_scatter(late, LATE, dx, 0, "_late")
    adam_layer(0)

    d_mod_flat = jnp.stack([jnp.stack(d_mod[l]) for l in range(depth)]).reshape(-1)
    d_norm_flat = jnp.stack([jnp.stack(d_norm[l]) for l in range(depth)]).reshape(-1)
    d_conv_flat = jnp.stack(d_conv).reshape(-1)
    vec = jnp.concatenate([d_mod_flat, d_norm_flat, d_conv_flat, d_rel.reshape(-1), d_final_g,
                           jnp.broadcast_to(loss_part, (128,))])
    pad = (-vec.size) % 1024
    vec = jnp.concatenate([vec, jnp.zeros((pad,), F32)]).reshape(-1, 128)
    parts = _all_gather(vec, "small_grads_all_gather").reshape(N_DEV, vec.shape[0], 128)
    tot = _sum_rows8(parts, "small_grads_sum").reshape(-1)
    o0 = 0
    g_ada_b = tot[o0:o0 + d_mod_flat.size].reshape(ada_b.shape)
    o0 += d_mod_flat.size
    g_norm_full = tot[o0:o0 + d_norm_flat.size].reshape(depth, 3, D)
    o0 += d_norm_flat.size
    g_conv_full = tot[o0:o0 + d_conv_flat.size].reshape(depth, CONV_K, D)
    o0 += d_conv_flat.size
    g_rel = tot[o0:o0 + rel_bias.size].reshape(rel_bias.shape)
    o0 += rel_bias.size
    g_final = tot[o0:o0 + D]
    o0 += D
    loss = tot[o0]
    g_norm = lax.dynamic_slice_in_dim(g_norm_full, me * dsh, dsh, axis=2)
    g_conv = lax.dynamic_slice_in_dim(g_conv_full, me * dsh, dsh, axis=2)

    dm_all = parts.reshape(N_DEV, -1)[:, :d_mod_flat.size].reshape(N_DEV, depth, N_DEV * n_ada)
    dm_cols = lax.dynamic_slice_in_dim(dm_all, me * n_ada, n_ada, axis=2)
    dm16 = jnp.concatenate([jnp.transpose(dm_cols, (1, 0, 2)), jnp.zeros((depth, N_DEV, n_ada), F32)], axis=1)
    g_ada_w = _ada_bwd(cs16, dm16, "ada_bwd")

    small = dict(ada_w=(ada_w, g_ada_w, m_ada_w, v_ada_w), ada_b=(ada_b, g_ada_b, m_ada_b, v_ada_b),
                 norm_g=(norm_g, g_norm, m_norm_g, v_norm_g), conv_w=(conv_w, g_conv, m_conv_w, v_conv_w),
                 rel_bias=(rel_bias, g_rel, m_rel_bias, v_rel_bias), final_g=(final_g, g_final, m_final_g, v_final_g))
    order = ("ada_w", "ada_b", "norm_g", "ffn_w_gate", "ffn_w_up", "ffn_w_down", "w_in", "conv_w", "w_conv_out",
             "w_attn_out", "w_o", "rel_bias", "final_g")
    res = {}
    for name in order:
        if name in big:
            res[name] = tuple(t.reshape(big[name][0].shape) for t in big_out[name])
        else:
            w_, g_, m_, v_ = small[name]
            res[name] = (g_,) + _adamw(w_, g_, m_, v_, "adamw_" + name)
    return (loss, grad_x, *[res[n][0] for n in order], *[res[n][1] for n in order],
            *[res[n][2] for n in order], *[res[n][3] for n in order])
```

```python
import functools

import numpy as np
import jax
import jax.numpy as jnp
from jax import lax
from jax.experimental import pallas as pl
from jax.experimental.pallas import tpu as pltpu

F32 = jnp.float32
BF16 = jnp.bfloat16

N_DEV = 8
HEAD_DIM = 64
HEAD_SHIFT = 6
HEADS_PER_GROUP = 8
DILATION_GROUPS = ((128, 1), (512, 4), (2048, 16))
DILS = tuple(d for _, d in DILATION_GROUPS)
N_GROUPS = len(DILATION_GROUPS)
ATTN_OUT = HEADS_PER_GROUP * HEAD_DIM
QKV_W = N_GROUPS * ATTN_OUT
BLOCK = 128
NUM_BUCKETS = 32
MAX_DISTANCE = 2048
CONV_K = 3
EPS = 1e-6
NEG_INF = -1e30
SCALE = HEAD_DIM ** -0.5

ADAM_LR = 0.001
ADAM_B1 = 0.9
ADAM_B2 = 0.999
ADAM_EPS = 1e-08
ADAM_WD = 0.01
ADAM_STEP = 10

V7X_VMEM_LIMIT = 48 * 1024 * 1024
MESH = pl.DeviceIdType.MESH

NN = (((1,), (0,)), ((), ()))
NT = (((1,), (1,)), ((), ()))
TN = (((0,), (0,)), ((), ()))


def _pick(dim, cands):
    for c in cands:
        if dim % c == 0:
            return c
    return dim


def _pick_k(K, cap=2816):
    if K <= cap or K % 128:
        return K
    best = 128
    for m in range(1, K // 128 + 1):
        if (K // 128) % m == 0 and 128 * m <= cap:
            best = 128 * m
    return best


def _params(sem):
    return pltpu.CompilerParams(dimension_semantics=sem, vmem_limit_bytes=V7X_VMEM_LIMIT)


def _all_gather(x_shard, name):
    m_per, n = x_shard.shape

    def body(x_ref, out_ref, send_sems, recv_sems, local_sem):
        x, y, c = lax.axis_index("x"), lax.axis_index("y"), lax.axis_index("c")
        me, sibling = (x, y, c), (x, y, 1 - c)
        chips = [(1 - x, y), (x, 1 - y), (1 - x, 1 - y)]

        def rows(px, py, pc):
            return out_ref.at[pl.ds((4 * px + 2 * py + pc) * m_per, m_per), :]

        def copy(k, block, to, src=None):
            return pltpu.make_async_remote_copy(
                src_ref=rows(*block) if src is None else src, dst_ref=rows(*block),
                send_sem=send_sems.at[k], recv_sem=recv_sems.at[k], device_id=to, device_id_type=MESH)

        mine = pltpu.make_async_copy(x_ref, rows(*me), local_sem)
        mine.start()
        first = [copy(0, me, sibling, src=x_ref)]
        first += [copy(1 + j, me, (*chip, c), src=x_ref) for j, chip in enumerate(chips)]
        for cp in first:
            cp.start()
        passed = [copy(4 + j, (*chip, c), sibling) for j, chip in enumerate(chips)]
        for j, chip in enumerate(chips):
            copy(1 + j, (*chip, c), me).wait_recv()
            passed[j].start()
        copy(0, sibling, me).wait_recv()
        for j, chip in enumerate(chips):
            copy(4 + j, (*chip, 1 - c), me).wait_recv()
        for cp in first + passed:
            cp.wait_send()
        mine.wait()

    return pl.pallas_call(
        body, name=name,
        out_shape=jax.ShapeDtypeStruct((N_DEV * m_per, n), x_shard.dtype),
        in_specs=[pl.BlockSpec(memory_space=pltpu.VMEM)],
        out_specs=pl.BlockSpec(memory_space=pltpu.VMEM),
        scratch_shapes=[pltpu.SemaphoreType.DMA((7,)), pltpu.SemaphoreType.DMA((7,)), pltpu.SemaphoreType.DMA],
    )(x_shard)


def _offsets(piece_rows):
    offs, o = [], 0
    for n in piece_rows:
        offs.append(o)
        o += n
    return offs


HBM_SPEC = pl.BlockSpec(memory_space=pltpu.HBM)
SEM_SPEC = pl.BlockSpec(memory_space=pltpu.SEMAPHORE)
ANY_SPEC = pl.BlockSpec(memory_space=pl.ANY)
SPLIT_COPY_PARAMS = pltpu.CompilerParams(has_side_effects=pltpu.SideEffectType.DATAFLOW_SIDE_EFFECTING)


def _in_hbm(a):
    return pltpu.with_memory_space_constraint(a, pltpu.HBM)


def _dma_sems(n):
    return [pltpu.SemaphoreType.DMA(())] * n


def _whole(ref, send_sem, recv_sem, me):
    return pltpu.make_async_remote_copy(src_ref=ref, dst_ref=ref, send_sem=send_sem, recv_sem=recv_sem,
                                        device_id=me, device_id_type=MESH)


def _gather_start(packed, piece_rows, after, name):
    R, w = packed.shape
    offs = _offsets(piece_rows)
    P = len(piece_rows)
    assert offs[-1] + piece_rows[-1] == R

    def body(*refs):
        src_ref = refs[0]
        o = refs[P + 2:]
        send, recv = o[0:4], o[4:8]
        zones, token, stage, local_sems = o[9:9 + P], o[9 + P], o[10 + P], o[11 + P]
        x, y, c = lax.axis_index("x"), lax.axis_index("y"), lax.axis_index("c")
        targets = [(x, y, 1 - c), (1 - x, y, c), (x, 1 - y, c), (1 - x, 1 - y, c)]
        me = 4 * x + 2 * y + c

        def piece(p, ref):
            return ref.at[pl.ds(offs[p], piece_rows[p]), :]

        def rows(p):
            return zones[p].at[pl.ds(me * piece_rows[p], piece_rows[p]), :]

        for k, to in enumerate(targets):
            for p in range(P):
                pltpu.make_async_remote_copy(src_ref=piece(p, src_ref), dst_ref=rows(p), send_sem=send[k],
                                             recv_sem=recv[k], device_id=to, device_id_type=MESH).start()
        load = pltpu.make_async_copy(src_ref, stage, local_sems.at[P])
        load.start()
        load.wait()
        mine = [pltpu.make_async_copy(piece(p, stage), rows(p), local_sems.at[p]) for p in range(P)]
        for cp in mine:
            cp.start()
        for cp in mine:
            cp.wait()
        token[...] = jnp.zeros_like(token)

    zones_in = [_in_hbm(lax.empty((N_DEV * n, w), packed.dtype)) for n in piece_rows]
    outs = pl.pallas_call(
        body, name=name,
        out_shape=(*_dma_sems(8), pltpu.HBM((R, w), packed.dtype),
                   *[pltpu.HBM((N_DEV * n, w), packed.dtype) for n in piece_rows],
                   jax.ShapeDtypeStruct((8, 128), F32)),
        in_specs=[HBM_SPEC] * (P + 1) + [ANY_SPEC],
        out_specs=[SEM_SPEC] * 8 + [HBM_SPEC] * (P + 1) + [pl.BlockSpec(memory_space=pltpu.VMEM)],
        input_output_aliases={0: 8, **{1 + p: 9 + p for p in range(P)}},
        scratch_shapes=[pltpu.VMEM((R, w), packed.dtype), pltpu.SemaphoreType.DMA((P + 1,))],
        compiler_params=SPLIT_COPY_PARAMS,
    )(_in_hbm(packed), *zones_in, after)
    return outs[0:8], outs[8], list(outs[9:9 + P]), outs[9 + P]


def _gather_forward(sems, packed, zones, piece_rows, after, name):
    P = len(piece_rows)

    def body(*refs):
        src_ref = refs[0]
        s = refs[1 + P:9 + P]
        o = refs[10 + P:]
        send, recv = s[0:4], s[4:8]
        send2, recv2, zones_o = o[0:3], o[3:6], o[7:7 + P]
        x, y, c = lax.axis_index("x"), lax.axis_index("y"), lax.axis_index("c")
        me = (x, y, c)
        chips = [(1 - x, y), (x, 1 - y), (1 - x, 1 - y)]
        for j, (px, py) in enumerate(chips):
            _whole(src_ref, send[1 + j], recv[1 + j], me).wait_recv()
            blk = 4 * px + 2 * py + c
            for p in range(P):
                r = zones_o[p].at[pl.ds(blk * piece_rows[p], piece_rows[p]), :]
                pltpu.make_async_remote_copy(src_ref=r, dst_ref=r, send_sem=send2[j], recv_sem=recv2[j],
                                             device_id=(x, y, 1 - c), device_id_type=MESH).start()
        _whole(src_ref, send[0], recv[0], me).wait_recv()
        for k in range(4):
            _whole(src_ref, send[k], recv[k], me).wait_send()
        o[7 + P][...] = jnp.zeros_like(o[7 + P])

    outs = pl.pallas_call(
        body, name=name,
        out_shape=(*_dma_sems(6), pltpu.HBM(packed.shape, packed.dtype),
                   *[pltpu.HBM(z.shape, z.dtype) for z in zones], jax.ShapeDtypeStruct((8, 128), F32)),
        in_specs=[HBM_SPEC] * (P + 1) + [SEM_SPEC] * 8 + [ANY_SPEC],
        out_specs=[SEM_SPEC] * 6 + [HBM_SPEC] * (P + 1) + [pl.BlockSpec(memory_space=pltpu.VMEM)],
        input_output_aliases={0: 6, **{1 + p: 7 + p for p in range(P)}},
        compiler_params=SPLIT_COPY_PARAMS,
    )(packed, *zones, *sems, after)
    return outs[0:6], outs[6], list(outs[7:7 + P]), outs[7 + P]


def _gather_finish(sems2, packed, zones, after, name):
    P = len(zones)

    def body(*refs):
        src_ref = refs[0]
        s = refs[1 + P:7 + P]
        x, y, c = lax.axis_index("x"), lax.axis_index("y"), lax.axis_index("c")
        for j in range(3):
            _whole(src_ref, s[j], s[3 + j], (x, y, c)).wait_recv()
        for j in range(3):
            _whole(src_ref, s[j], s[3 + j], (x, y, c)).wait_send()

    outs = pl.pallas_call(
        body, name=name,
        out_shape=(pltpu.HBM(packed.shape, packed.dtype), *[pltpu.HBM(z.shape, z.dtype) for z in zones]),
        in_specs=[HBM_SPEC] * (P + 1) + [SEM_SPEC] * 6 + [ANY_SPEC],
        out_specs=[HBM_SPEC] * (P + 1),
        input_output_aliases={p: p for p in range(P + 1)},
        compiler_params=SPLIT_COPY_PARAMS,
    )(packed, *zones, *sems2, after)
    return list(outs[1:1 + P])


def _scatter_start(pieces, after, name):
    P = len(pieces)
    w = pieces[0].shape[1]
    piece_rows = [p.shape[0] // N_DEV for p in pieces]
    offs = _offsets(piece_rows)
    R = offs[-1] + piece_rows[-1]

    def body(*refs):
        o = refs[P + 2:]
        send, recv = o[0:7], o[7:14]
        srcs, dst_ref, token, stage, local_sems = o[14:14 + P], o[14 + P], o[15 + P], o[16 + P], o[17 + P]
        x, y, c = lax.axis_index("x"), lax.axis_index("y"), lax.axis_index("c")
        me = 4 * x + 2 * y + c

        def chunk(p, dev):
            return srcs[p].at[pl.ds(dev * piece_rows[p], piece_rows[p]), :]

        def slot(p, dev):
            return dst_ref.at[dev, pl.ds(offs[p], piece_rows[p]), :]

        for k in range(1, N_DEV):
            px = 1 - x if (k >> 2) & 1 else x
            py = 1 - y if (k >> 1) & 1 else y
            pc = 1 - c if k & 1 else c
            peer = 4 * px + 2 * py + pc
            for p in range(P):
                pltpu.make_async_remote_copy(
                    src_ref=chunk(p, peer), dst_ref=slot(p, me), send_sem=send[k - 1], recv_sem=recv[k - 1],
                    device_id=(px, py, pc), device_id_type=MESH).start()
        mine = [pltpu.make_async_copy(chunk(p, me), stage.at[pl.ds(offs[p], piece_rows[p]), :], local_sems.at[p])
                for p in range(P)]
        for cp in mine:
            cp.start()
        for cp in mine:
            cp.wait()
        store = pltpu.make_async_copy(stage, dst_ref.at[me], local_sems.at[P])
        store.start()
        store.wait()
        token[...] = jnp.zeros_like(token)

    dtype = pieces[0].dtype
    outs = pl.pallas_call(
        body, name=name,
        out_shape=(*_dma_sems(14), *[pltpu.HBM(p.shape, dtype) for p in pieces], pltpu.HBM((N_DEV, R, w), dtype),
                   jax.ShapeDtypeStruct((8, 128), F32)),
        in_specs=[HBM_SPEC] * (P + 1) + [ANY_SPEC],
        out_specs=[SEM_SPEC] * 14 + [HBM_SPEC] * (P + 1) + [pl.BlockSpec(memory_space=pltpu.VMEM)],
        input_output_aliases={p: 14 + p for p in range(P + 1)},
        scratch_shapes=[pltpu.VMEM((R, w), dtype), pltpu.SemaphoreType.DMA((P + 1,))],
        compiler_params=SPLIT_COPY_PARAMS,
    )(*[_in_hbm(p) for p in pieces], _in_hbm(lax.empty((N_DEV, R, w), dtype)), after)
    return outs[0:14], list(outs[14:14 + P]), outs[14 + P], outs[15 + P]


def _scatter_finish(sems, pieces, recv, after, name):
    P = len(pieces)
    after = list(after) if isinstance(after, (list, tuple)) else [after]

    def body(*refs):
        dst_ref = refs[P]
        s = refs[P + 1:P + 15]
        x, y, c = lax.axis_index("x"), lax.axis_index("y"), lax.axis_index("c")
        for k in range(7):
            _whole(dst_ref.at[0], s[k], s[7 + k], (x, y, c)).wait_recv()
        for k in range(7):
            _whole(dst_ref.at[0], s[k], s[7 + k], (x, y, c)).wait_send()

    outs = pl.pallas_call(
        body, name=name,
        out_shape=(*[pltpu.HBM(p.shape, p.dtype) for p in pieces], pltpu.HBM(recv.shape, recv.dtype)),
        in_specs=[HBM_SPEC] * (P + 1) + [SEM_SPEC] * 14 + [ANY_SPEC] * len(after),
        out_specs=[HBM_SPEC] * (P + 1),
        input_output_aliases={p: p for p in range(P + 1)},
        compiler_params=SPLIT_COPY_PARAMS,
    )(*pieces, recv, *sems, *after)
    return outs[P]


def _sum_sources(parts, name):
    _, r, n = parts.shape
    tr = _pick(r, [256, 128, 64, 32, 16, 8])

    def kern(p_ref, o_ref):
        acc = p_ref[0].astype(F32)
        for k in range(1, N_DEV):
            acc = acc + p_ref[k].astype(F32)
        o_ref[...] = acc

    return pl.pallas_call(
        kern, name=name, grid=(r // tr,),
        out_shape=jax.ShapeDtypeStruct((r, n), F32),
        in_specs=[pl.BlockSpec((N_DEV, tr, n), lambda i: (0, i, 0))],
        out_specs=pl.BlockSpec((tr, n), lambda i: (i, 0)),
        compiler_params=_params(("parallel",)),
    )(parts)


def _matmul(a, b, mode, out_dtype, name, tm=None, tn=None, tk=None, resid=None):
    if mode == "nn":
        (M, K), N = a.shape, b.shape[1]
    elif mode == "nt":
        (M, K), N = a.shape, b.shape[0]
    else:
        (K, M), N = a.shape, b.shape[1]
    dims = {"nn": NN, "nt": NT, "tn": TN}[mode]
    tm = tm or _pick(M, [1024, 1408, 512, 256, 128])
    tn = tn or _pick(N, [1024, 1408, 512, 256, 128])
    tk = tk or _pick_k(K)
    nk = K // tk
    a_spec = {"nn": pl.BlockSpec((tm, tk), lambda i, j, k: (i, k)),
              "nt": pl.BlockSpec((tm, tk), lambda i, j, k: (i, k)),
              "tn": pl.BlockSpec((tk, tm), lambda i, j, k: (k, i))}[mode]
    b_spec = {"nn": pl.BlockSpec((tk, tn), lambda i, j, k: (k, j)),
              "nt": pl.BlockSpec((tn, tk), lambda i, j, k: (j, k)),
              "tn": pl.BlockSpec((tk, tn), lambda i, j, k: (k, j))}[mode]
    o_spec = pl.BlockSpec((tm, tn), lambda i, j, k: (i, j))
    n_in = 2 if resid is None else 4
    n_out = 1 if resid is None else 2

    def kern(*refs):
        a_ref, b_ref = refs[0], refs[1]
        outs = refs[n_in:n_in + n_out]
        acc_ref = refs[n_in + n_out] if nk > 1 else None

        def finish(acc):
            if resid is None:
                outs[0][...] = acc.astype(out_dtype)
            else:
                x_ref, g_ref = refs[2], refs[3]
                outs[0][...] = x_ref[...] + (resid[2] * g_ref[...]) * acc
                outs[1][...] = acc.astype(out_dtype)

        part = lax.dot_general(a_ref[...], b_ref[...], dims, preferred_element_type=F32)
        if nk == 1:
            finish(part)
        else:
            k = pl.program_id(2)

            @pl.when(k == 0)
            def _():
                acc_ref[...] = part

            @pl.when(k > 0)
            def _():
                acc_ref[...] += part

            @pl.when(k == nk - 1)
            def _():
                finish(acc_ref[...])

    in_specs = [a_spec, b_spec]
    args = [a, b]
    out_shape = [jax.ShapeDtypeStruct((M, N), out_dtype)]
    out_specs = [o_spec]
    if resid is not None:
        in_specs += [o_spec, pl.BlockSpec((1, tn), lambda i, j, k: (0, j))]
        args += [resid[0], resid[1]]
        out_shape = [jax.ShapeDtypeStruct((M, N), F32)] + out_shape
        out_specs = [o_spec, o_spec]
    res = pl.pallas_call(
        kern, name=name, grid=(M // tm, N // tn, nk),
        out_shape=out_shape, in_specs=in_specs, out_specs=out_specs,
        scratch_shapes=[pltpu.VMEM((tm, tn), F32)] if nk > 1 else [],
        compiler_params=_params(("parallel", "parallel", "arbitrary")),
    )(*args)
    return res[0] if resid is None else res


def _dot3(a, b, dims):
    ah = a.astype(BF16)
    al = (a - ah.astype(F32)).astype(BF16)
    bh = b.astype(BF16)
    bl = (b - bh.astype(F32)).astype(BF16)
    d = functools.partial(lax.dot_general, dimension_numbers=dims, preferred_element_type=F32)
    return d(ah, bh) + (d(ah, bl) + d(al, bh))


def _silu_parts(a):
    sg = jax.nn.sigmoid(a)
    return a * sg, sg * (1.0 + a * (1.0 - sg))


def _ffn_up(h, wg_t, wu_t, name):
    S, D = h.shape
    F = wg_t.shape[0]
    tm = _pick(S, [512, 256, 128])
    tn = _pick(F, [1408, 512, 256, 128])

    def kern(h_ref, g_ref, u_ref, a_out, u_out, z_out):
        hv = h_ref[...]
        a = lax.dot_general(hv, g_ref[...], NT, preferred_element_type=F32)
        u = lax.dot_general(hv, u_ref[...], NT, preferred_element_type=F32)
        a_out[...] = a.astype(BF16)
        u_out[...] = u.astype(BF16)
        z_out[...] = (_silu_parts(a)[0] * u).astype(BF16)

    w_spec = pl.BlockSpec((tn, D), lambda j, i: (j, 0))
    o_spec = pl.BlockSpec((tm, tn), lambda j, i: (i, j))
    return pl.pallas_call(
        kern, name=name, grid=(F // tn, S // tm),
        out_shape=[jax.ShapeDtypeStruct((S, F), BF16)] * 3,
        in_specs=[pl.BlockSpec((tm, D), lambda j, i: (i, 0)), w_spec, w_spec],
        out_specs=[o_spec] * 3,
        compiler_params=_params(("parallel", "parallel")),
    )(h, wg_t, wu_t)


def _ffn_up_bwd(dz, a, u, wg_t, wu_t, name):
    S, F = dz.shape
    D = wg_t.shape[1]
    tm = _pick(S, [512, 256, 128])
    tk = _pick(F, [1408, 512, 256, 128])
    nk = F // tk

    def kern(dz_ref, a_ref, u_ref, g_ref, w_ref, da_out, du_out, dh_out, acc_ref):
        k = pl.program_id(1)
        av = a_ref[...].astype(F32)
        uv = u_ref[...].astype(F32)
        dzv = dz_ref[...].astype(F32)
        silu, dsilu = _silu_parts(av)
        da = (dzv * uv * dsilu).astype(BF16)
        du = (dzv * silu).astype(BF16)
        da_out[...] = da
        du_out[...] = du
        part = (lax.dot_general(da, g_ref[...], NN, preferred_element_type=F32)
                + lax.dot_general(du, w_ref[...], NN, preferred_element_type=F32))

        @pl.when(k == 0)
        def _():
            acc_ref[...] = part

        @pl.when(k > 0)
        def _():
            acc_ref[...] += part

        @pl.when(k == nk - 1)
        def _():
            dh_out[...] = acc_ref[...]

    t_spec = pl.BlockSpec((tm, tk), lambda i, k: (i, k))
    w_spec = pl.BlockSpec((tk, D), lambda i, k: (k, 0))
    return pl.pallas_call(
        kern, name=name, grid=(S // tm, nk),
        out_shape=[jax.ShapeDtypeStruct((S, F), BF16)] * 2 + [jax.ShapeDtypeStruct((S, D), F32)],
        in_specs=[t_spec, t_spec, t_spec, w_spec, w_spec],
        out_specs=[t_spec, t_spec, pl.BlockSpec((tm, D), lambda i, k: (i, 0))],
        scratch_shapes=[pltpu.VMEM((tm, D), F32)],
        compiler_params=_params(("parallel", "arbitrary")),
    )(dz, a, u, wg_t, wu_t)


def _attn_dh(dq, dkv, w_t, name):
    S = dq.shape[0]
    D = w_t.shape[1]
    tm = _pick(S, [1024, 512, 256, 128])

    def kern(dq_ref, dk_ref, dv_ref, wq_ref, wk_ref, wv_ref, o_ref):
        o_ref[...] = (lax.dot_general(dq_ref[...], wq_ref[...], NN, preferred_element_type=F32)
                      + lax.dot_general(dk_ref[...], wk_ref[...], NN, preferred_element_type=F32)
                      + lax.dot_general(dv_ref[...], wv_ref[...], NN, preferred_element_type=F32))

    def w_blk(j):
        return pl.BlockSpec((ATTN_OUT, D), lambda i: (j, 0))

    return pl.pallas_call(
        kern, name=name, grid=(S // tm,),
        out_shape=jax.ShapeDtypeStruct((S, D), F32),
        in_specs=[pl.BlockSpec((tm, ATTN_OUT), lambda i: (i, 0)), pl.BlockSpec((tm, ATTN_OUT), lambda i: (i, 0)),
                  pl.BlockSpec((tm, ATTN_OUT), lambda i: (i, 1)), w_blk(0), w_blk(1), w_blk(2)],
        out_specs=pl.BlockSpec((tm, D), lambda i: (i, 0)),
        compiler_params=_params(("parallel",)),
    )(dq, dkv, dkv, w_t, w_t, w_t)


def _row_spec(tm, d):
    return pl.BlockSpec((tm, d), lambda i: (i, 0))


def _vec_spec(d, rows=1):
    return pl.BlockSpec((rows, d), lambda i: (0, 0))


def _perm_spec(dil, tm, w):
    return pl.BlockSpec((dil, tm // dil, w), lambda i: (0, i, 0))


def _stage_shape(tm, w):
    return pltpu.VMEM((w // 128, tm, 128), F32)


def _stage(scr, val):
    for ci in range(scr.shape[0]):
        scr[ci] = val[:, 128 * ci:128 * (ci + 1)]


def _unstage(scr):
    return jnp.concatenate([scr[ci] for ci in range(scr.shape[0])], axis=1)


def _get_residue(scr, res, dil):
    n = scr.shape[1] // dil
    return jnp.concatenate([scr[ci, pl.ds(res, n, stride=dil), :] for ci in range(scr.shape[0])], axis=1)


def _put_residue(scr, res, dil, val):
    n = scr.shape[1] // dil
    for ci in range(scr.shape[0]):
        scr[ci, pl.ds(res, n, stride=dil), :] = val[:, 128 * ci:128 * (ci + 1)]


def _norm_mod_fwd(x, g, s, b, name, dils=()):
    S, D = x.shape
    tm = _pick(S, [512, 256, 128])

    def kern(x_ref, g_ref, s_ref, b_ref, h_ref, *rest):
        xv = x_ref[...]
        r = lax.rsqrt(jnp.mean(xv * xv, axis=1, keepdims=True) + EPS)
        hv = xv * r * g_ref[...] * (1.0 + s_ref[...]) + b_ref[...]
        h_ref[...] = hv.astype(BF16)
        if dils:
            scr = rest[len(dils)]
            _stage(scr, hv)
            for dil, p_ref in zip(dils, rest[:len(dils)]):
                for res in range(dil):
                    p_ref[res] = _get_residue(scr, res, dil).astype(BF16)

    return pl.pallas_call(
        kern, name=name, grid=(S // tm,),
        out_shape=[jax.ShapeDtypeStruct((S, D), BF16)] + [jax.ShapeDtypeStruct((dil, S // dil, D), BF16) for dil in dils],
        in_specs=[_row_spec(tm, D), _vec_spec(D), _vec_spec(D), _vec_spec(D)],
        out_specs=[_row_spec(tm, D)] + [_perm_spec(dil, tm, D) for dil in dils],
        scratch_shapes=[_stage_shape(tm, D)] if dils else [],
        compiler_params=_params(("parallel",)),
    )(x, g, s, b)


def _gate_part(dx, nxt_refs, coef, df_ref, cs_ref):
    f_ref, gate_ref = nxt_refs
    df_ref[...] = ((coef * gate_ref[...]) * dx).astype(BF16)
    cs_ref[4:5, :] += coef * jnp.sum(f_ref[...].astype(F32) * dx, axis=0, keepdims=True)


def _norm_mod_bwd(x, dh_nat, dh_perm, dxo, g, s, name, nxt=None):
    S, D = x.shape
    tm = _pick(S, [512, 256, 128])
    n = S // tm
    n_nat, n_perm = len(dh_nat), len(dh_perm)
    n_nxt = 0 if nxt is None else 2

    def kern(*refs):
        x_ref = refs[0]
        nat = refs[1:1 + n_nat]
        perm = refs[1 + n_nat:1 + n_nat + n_perm]
        base = 1 + n_nat + n_perm
        dxo_ref, g_ref, s_ref = refs[base:base + 3]
        nxt_refs = refs[base + 3:base + 3 + n_nxt]
        dx_ref, cs_ref = refs[base + 3 + n_nxt:base + 5 + n_nxt]
        rest = refs[base + 5 + n_nxt:]
        df_ref = rest[0] if nxt is not None else None
        scr = rest[1:] if nxt is not None else rest
        i = pl.program_id(0)
        xv = x_ref[...]
        r = lax.rsqrt(jnp.mean(xv * xv, axis=1, keepdims=True) + EPS)
        xn = xv * r
        dh_v = nat[0][...].astype(F32)
        for t in nat[1:]:
            dh_v = dh_v + t[...].astype(F32)
        for (dil, _), p_ref, sc in zip(dh_perm, perm, scr):
            for res in range(dil):
                _put_residue(sc, res, dil, p_ref[res])
            dh_v = dh_v + _unstage(sc)
        one_s = 1.0 + s_ref[...]
        dxn = dh_v * (g_ref[...] * one_s)
        dx = dxo_ref[...] + r * (dxn - xn * jnp.mean(xn * dxn, axis=1, keepdims=True))
        dx_ref[...] = dx

        @pl.when(i == 0)
        def _():
            cs_ref[...] = jnp.zeros_like(cs_ref)

        cs_ref[0:1, :] += jnp.sum(dh_v, axis=0, keepdims=True)
        cs_ref[1:2, :] += jnp.sum(dh_v * xn, axis=0, keepdims=True)
        if nxt is not None:
            _gate_part(dx, nxt_refs, nxt[2], df_ref, cs_ref)

        @pl.when(i == n - 1)
        def _():
            t = cs_ref[1:2, :]
            cs_ref[2:3, :] = g_ref[...] * t
            cs_ref[3:4, :] = one_s * t

    nxt_specs = [] if nxt is None else [_row_spec(tm, D), _vec_spec(D)]
    nxt_args = [] if nxt is None else [nxt[0], nxt[1]]
    return pl.pallas_call(
        kern, name=name, grid=(n,),
        out_shape=[jax.ShapeDtypeStruct((S, D), F32), jax.ShapeDtypeStruct((8, D), F32)]
        + ([] if nxt is None else [jax.ShapeDtypeStruct((S, D), BF16)]),
        in_specs=[_row_spec(tm, D)] + [_row_spec(tm, D)] * n_nat + [_perm_spec(dil, tm, D) for dil, _ in dh_perm]
        + [_row_spec(tm, D), _vec_spec(D), _vec_spec(D)] + nxt_specs,
        out_specs=[_row_spec(tm, D), _vec_spec(D, 8)] + ([] if nxt is None else [_row_spec(tm, D)]),
        scratch_shapes=[_stage_shape(tm, D) for _ in dh_perm],
        compiler_params=_params(("arbitrary",)),
    )(x, *dh_nat, *[a for _, a in dh_perm], dxo, g, s, *nxt_args)


def _loss_head(x, g, target, nxt, name):
    S, D = x.shape
    tm = _pick(S, [512, 256, 128])
    n = S // tm

    def kern(x_ref, g_ref, t_ref, f_ref, gate_ref, dx_ref, cs_ref, df_ref):
        i = pl.program_id(0)
        xv = x_ref[...]
        r = lax.rsqrt(jnp.mean(xv * xv, axis=1, keepdims=True) + EPS)
        xn = xv * r
        e = xn * g_ref[...] - t_ref[...]
        dxn = (e * (1.0 / D)) * g_ref[...]
        dx = r * (dxn - xn * jnp.mean(xn * dxn, axis=1, keepdims=True))
        dx_ref[...] = dx

        @pl.when(i == 0)
        def _():
            cs_ref[...] = jnp.zeros_like(cs_ref)

        cs_ref[0:1, :] += jnp.sum(xn * e, axis=0, keepdims=True) * (1.0 / D)
        cs_ref[1:2, :] += jnp.sum(e * e, axis=0, keepdims=True)
        _gate_part(dx, (f_ref, gate_ref), nxt[2], df_ref, cs_ref)

        @pl.when(i == n - 1)
        def _():
            tot = jnp.sum(cs_ref[1:2, :], axis=1, keepdims=True) * (0.5 / D)
            cs_ref[2:3, :] = jnp.broadcast_to(tot, (1, D))

    return pl.pallas_call(
        kern, name=name, grid=(n,),
        out_shape=[jax.ShapeDtypeStruct((S, D), F32), jax.ShapeDtypeStruct((8, D), F32),
                   jax.ShapeDtypeStruct((S, D), BF16)],
        in_specs=[_row_spec(tm, D), _vec_spec(D), _row_spec(tm, D), _row_spec(tm, D), _vec_spec(D)],
        out_specs=[_row_spec(tm, D), _vec_spec(D, 8), _row_spec(tm, D)],
        compiler_params=_params(("arbitrary",)),
    )(x, g, target, nxt[0], nxt[1])


def _shift_down(p, row, prev_rows):
    a, b = prev_rows
    p1 = jnp.where(row == 0, b, pltpu.roll(p, 1, 0))
    p2 = jnp.where(row == 0, a, jnp.where(row == 1, b, pltpu.roll(p, 2, 0)))
    return p1, p2


def _conv_fwd(cg, conv_w, name):
    S, D5 = cg.shape
    D = D5 // 5
    tm = _pick(S, [512, 256, 128])
    t8 = tm // 8

    def prev(col):
        return pl.BlockSpec((8, D), lambda i: (jnp.maximum(i * t8 - 1, 0), col))

    def kern(cb_ref, cc_ref, ch_ref, ccp_ref, chp_ref, w_ref, y_ref):
        i = pl.program_id(0)
        keep = jnp.where(i > 0, 1.0, 0.0)
        p = cc_ref[...].astype(F32) * ch_ref[...].astype(F32)
        pa = ccp_ref[6:7, :].astype(F32) * chp_ref[6:7, :].astype(F32) * keep
        pb = ccp_ref[7:8, :].astype(F32) * chp_ref[7:8, :].astype(F32) * keep
        row = lax.broadcasted_iota(jnp.int32, (tm, D), 0)
        p1, p2 = _shift_down(p, row, (pa, pb))
        dw = w_ref[0:1, :] * p2 + w_ref[1:2, :] * p1 + w_ref[2:3, :] * p
        y_ref[...] = (cb_ref[...].astype(F32) * dw).astype(BF16)

    def col(cidx):
        return pl.BlockSpec((tm, D), lambda i: (i, cidx))

    return pl.pallas_call(
        kern, name=name, grid=(S // tm,),
        out_shape=jax.ShapeDtypeStruct((S, D), BF16),
        in_specs=[col(0), col(1), col(2), prev(1), prev(2), _vec_spec(D, CONV_K)],
        out_specs=_row_spec(tm, D),
        compiler_params=_params(("parallel",)),
    )(cg, cg, cg, cg, cg, conv_w)


def _conv_bwd(cg, dy, dgg, conv_w, name):
    S, D5 = cg.shape
    D = D5 // 5
    tm = _pick(S, [512, 256, 128])
    t8 = tm // 8
    n = S // tm
    last8 = S // 8 - 1

    def prev(col):
        return pl.BlockSpec((8, D), lambda i: (jnp.maximum(i * t8 - 1, 0), col))

    def nxt(col):
        return pl.BlockSpec((8, D), lambda i: (jnp.minimum((i + 1) * t8, last8), col))

    def kern(cb_ref, cc_ref, ch_ref, dy_ref, dgg_ref, ccp_ref, chp_ref, cbn_ref, dyn_ref, w_ref, d_ref, cs_ref):
        i = pl.program_id(0)
        keep_p = jnp.where(i > 0, 1.0, 0.0)
        keep_n = jnp.where(i < n - 1, 1.0, 0.0)
        cb = cb_ref[...].astype(F32)
        cc = cc_ref[...].astype(F32)
        ch = ch_ref[...].astype(F32)
        dyv = dy_ref[...].astype(F32)
        p = cc * ch
        pa = ccp_ref[6:7, :].astype(F32) * chp_ref[6:7, :].astype(F32) * keep_p
        pb = ccp_ref[7:8, :].astype(F32) * chp_ref[7:8, :].astype(F32) * keep_p
        row = lax.broadcasted_iota(jnp.int32, (tm, D), 0)
        p1, p2 = _shift_down(p, row, (pa, pb))
        w0, w1, w2 = w_ref[0:1, :], w_ref[1:2, :], w_ref[2:3, :]
        dw = w0 * p2 + w1 * p1 + w2 * p
        ddw = dyv * cb
        na = dyn_ref[0:1, :].astype(F32) * cbn_ref[0:1, :].astype(F32) * keep_n
        nb = dyn_ref[1:2, :].astype(F32) * cbn_ref[1:2, :].astype(F32) * keep_n
        u1 = jnp.where(row == tm - 1, na, pltpu.roll(ddw, tm - 1, 0))
        u2 = jnp.where(row == tm - 2, na, jnp.where(row == tm - 1, nb, pltpu.roll(ddw, tm - 2, 0)))
        dp = w2 * ddw + w1 * u1 + w0 * u2
        d_ref[:, 0:D] = (dyv * dw).astype(BF16)
        d_ref[:, D:2 * D] = (dp * ch).astype(BF16)
        d_ref[:, 2 * D:3 * D] = (dp * cc).astype(BF16)
        d_ref[:, 3 * D:5 * D] = dgg_ref[...]

        @pl.when(i == 0)
        def _():
            cs_ref[...] = jnp.zeros_like(cs_ref)

        cs_ref[0:1, :] += jnp.sum(ddw * p2, axis=0, keepdims=True)
        cs_ref[1:2, :] += jnp.sum(ddw * p1, axis=0, keepdims=True)
        cs_ref[2:3, :] += jnp.sum(ddw * p, axis=0, keepdims=True)

    def col(cidx):
        return pl.BlockSpec((tm, D), lambda i: (i, cidx))

    return pl.pallas_call(
        kern, name=name, grid=(n,),
        out_shape=[jax.ShapeDtypeStruct((S, 5 * D), BF16), jax.ShapeDtypeStruct((8, D), F32)],
        in_specs=[col(0), col(1), col(2), _row_spec(tm, D), _row_spec(tm, 2 * D), prev(1), prev(2), nxt(0),
                  pl.BlockSpec((8, D), lambda i: (jnp.minimum((i + 1) * t8, last8), 0)), _vec_spec(D, CONV_K)],
        out_specs=[_row_spec(tm, 5 * D), _vec_spec(D, 8)],
        compiler_params=_params(("arbitrary",)),
    )(cg, cg, cg, dy, dgg, cg, cg, cg, dy, conv_w)


def _merge_fwd(cg, yc, ya, name):
    S, D = yc.shape
    tm = _pick(S, [512, 256, 128])

    def kern(gc_ref, ga_ref, yc_ref, ya_ref, m_ref):
        m_ref[...] = (jax.nn.sigmoid(gc_ref[...].astype(F32)) * yc_ref[...].astype(F32)
                      + jax.nn.sigmoid(ga_ref[...].astype(F32)) * ya_ref[...].astype(F32)).astype(BF16)

    return pl.pallas_call(
        kern, name=name, grid=(S // tm,),
        out_shape=jax.ShapeDtypeStruct((S, D), BF16),
        in_specs=[pl.BlockSpec((tm, D), lambda i: (i, 3)), pl.BlockSpec((tm, D), lambda i: (i, 4)),
                  _row_spec(tm, D), _row_spec(tm, D)],
        out_specs=_row_spec(tm, D),
        compiler_params=_params(("parallel",)),
    )(cg, cg, yc, ya)


def _merge_bwd(cg, yc, ya, dm, name):
    S, D = yc.shape
    tm = _pick(S, [512, 256, 128])

    def kern(gc_ref, ga_ref, yc_ref, ya_ref, dm_ref, dyc_ref, dya_ref, dg_ref):
        dmv = dm_ref[...].astype(F32)
        sc = jax.nn.sigmoid(gc_ref[...].astype(F32))
        sa = jax.nn.sigmoid(ga_ref[...].astype(F32))
        dyc_ref[...] = (dmv * sc).astype(BF16)
        dya_ref[...] = (dmv * sa).astype(BF16)
        dg_ref[:, 0:D] = (dmv * yc_ref[...].astype(F32) * (sc * (1.0 - sc))).astype(BF16)
        dg_ref[:, D:2 * D] = (dmv * ya_ref[...].astype(F32) * (sa * (1.0 - sa))).astype(BF16)

    return pl.pallas_call(
        kern, name=name, grid=(S // tm,),
        out_shape=[jax.ShapeDtypeStruct((S, D), BF16), jax.ShapeDtypeStruct((S, D), BF16),
                   jax.ShapeDtypeStruct((S, 2 * D), BF16)],
        in_specs=[pl.BlockSpec((tm, D), lambda i: (i, 3)), pl.BlockSpec((tm, D), lambda i: (i, 4)),
                  _row_spec(tm, D), _row_spec(tm, D), _row_spec(tm, D)],
        out_specs=[_row_spec(tm, D), _row_spec(tm, D), pl.BlockSpec((tm, 2 * D), lambda i: (i, 0))],
        compiler_params=_params(("parallel",)),
    )(cg, cg, yc, ya, dm)


def _t5_bucket(dist):
    exact = NUM_BUCKETS // 2
    d = np.maximum(dist, 1).astype(np.float32)
    large = exact + (np.log(d / exact) / np.log(MAX_DISTANCE / exact) * (NUM_BUCKETS - exact)).astype(np.int32)
    large = np.minimum(large, NUM_BUCKETS - 1)
    return np.where(dist < exact, dist, large).astype(np.int32)


def _bucket_tables():
    i = np.arange(BLOCK)[:, None]
    j = np.arange(2 * BLOCK)[None, :]
    rel = i - j + BLOCK
    return np.stack([_t5_bucket(np.maximum(rel, 0) * d) for _, d in DILATION_GROUPS]).astype(np.int32)


def _band_masks():
    i = lax.broadcasted_iota(jnp.int32, (BLOCK, 2 * BLOCK), 0)
    j = lax.broadcasted_iota(jnp.int32, (BLOCK, 2 * BLOCK), 1)
    rel = i - j + BLOCK
    band = (rel >= 0) & (rel <= BLOCK)
    return band, band & (j >= BLOCK)


def _bias_build(rel_bias, buckets, name):
    def kern(rb_ref, bk_ref, o_ref):
        g = pl.program_id(0)
        bk = bk_ref[0]
        band, first = _band_masks()
        for h in range(HEADS_PER_GROUP):
            acc = jnp.zeros((BLOCK, 2 * BLOCK), F32)
            for b in range(NUM_BUCKETS):
                acc = jnp.where(bk == b, rb_ref[b, g * HEADS_PER_GROUP + h], acc)
            o_ref[0, 0, h] = jnp.where(first, acc, NEG_INF)
            o_ref[0, 1, h] = jnp.where(band, acc, NEG_INF)

    return pl.pallas_call(
        kern, name=name, grid=(N_GROUPS,),
        out_shape=jax.ShapeDtypeStruct((N_GROUPS, 2, HEADS_PER_GROUP, BLOCK, 2 * BLOCK), F32),
        in_specs=[pl.BlockSpec(memory_space=pltpu.SMEM),
                  pl.BlockSpec((1, BLOCK, 2 * BLOCK), lambda g: (g, 0, 0))],
        out_specs=pl.BlockSpec((1, 2, HEADS_PER_GROUP, BLOCK, 2 * BLOCK), lambda g: (g, 0, 0, 0, 0)),
        compiler_params=_params(("parallel",)),
    )(rel_bias, buckets)


def _bias_bwd(dlog, buckets, name):
    def kern(dl_ref, bk_ref, o_ref):
        g = pl.program_id(0)
        bk = bk_ref[0]
        rowi = lax.broadcasted_iota(jnp.int32, (NUM_BUCKETS, 128), 0)
        coli = lax.broadcasted_iota(jnp.int32, (NUM_BUCKETS, 128), 1)

        @pl.when(g == 0)
        def _():
            o_ref[...] = jnp.zeros_like(o_ref)

        acc = jnp.zeros((NUM_BUCKETS, 128), F32)
        for h in range(HEADS_PER_GROUP):
            dv = dl_ref[0, h]
            for b in range(NUM_BUCKETS):
                t = jnp.sum(jnp.where(bk == b, dv, 0.0), axis=0, keepdims=True)
                t = jnp.sum(t, axis=1, keepdims=True)
                acc = acc + jnp.where((rowi == b) & (coli == g * HEADS_PER_GROUP + h), t, 0.0)
        o_ref[...] += acc

    return pl.pallas_call(
        kern, name=name, grid=(N_GROUPS,),
        out_shape=jax.ShapeDtypeStruct((NUM_BUCKETS, 128), F32),
        in_specs=[pl.BlockSpec((1, HEADS_PER_GROUP, BLOCK, 2 * BLOCK), lambda g: (g, 0, 0, 0)),
                  pl.BlockSpec((1, BLOCK, 2 * BLOCK), lambda g: (g, 0, 0))],
        out_specs=pl.BlockSpec((NUM_BUCKETS, 128), lambda g: (0, 0)),
        compiler_params=_params(("arbitrary",)),
    )(dlog, buckets)


def _head_masks():
    lane = lax.broadcasted_iota(jnp.int32, (BLOCK, 128), 1)
    lo = lane < HEAD_DIM
    return lo, jnp.logical_not(lo)


def _attn_fwd(qkv, bias, d, name):
    S = qkv.shape[0]
    nb = S // d // BLOCK

    def kern(q_ref, kp_ref, kc_ref, vp_ref, vc_ref, b_ref, o_ref, lse_ref):
        lo, hi = _head_masks()
        for p in range(HEADS_PER_GROUP // 2):
            sl = slice(128 * p, 128 * (p + 1))
            q = q_ref[:, sl]
            k = jnp.concatenate([kp_ref[:, sl], kc_ref[:, sl]], axis=0)
            v = jnp.concatenate([vp_ref[:, sl], vc_ref[:, sl]], axis=0)
            zero = jnp.zeros_like(q)
            q2 = jnp.concatenate([jnp.where(lo, q, zero), jnp.where(hi, q, zero)], axis=0)
            b2 = jnp.concatenate([b_ref[0, 2 * p], b_ref[0, 2 * p + 1]], axis=0)
            s = lax.dot_general(q2, k, NT, preferred_element_type=F32) * SCALE + b2
            m = jnp.max(s, axis=1, keepdims=True)
            e = jnp.exp(s - m)
            l = jnp.sum(e, axis=1, keepdims=True)
            o2 = lax.dot_general(e.astype(BF16), v, NN, preferred_element_type=F32) / l
            l2 = jnp.broadcast_to(m + jnp.log(l), (2 * BLOCK, 128))
            o_ref[:, sl] = jnp.where(lo, o2[0:BLOCK], o2[BLOCK:2 * BLOCK])
            lse_ref[:, sl] = jnp.where(lo, l2[0:BLOCK], l2[BLOCK:2 * BLOCK])

    def blk(col, prev):
        if prev:
            return pl.BlockSpec((BLOCK, ATTN_OUT), lambda r, n: (r * nb + jnp.maximum(n - 1, 0), col))
        return pl.BlockSpec((BLOCK, ATTN_OUT), lambda r, n: (r * nb + n, col))

    o_spec = pl.BlockSpec((BLOCK, ATTN_OUT), lambda r, n: (r * nb + n, 0))
    return pl.pallas_call(
        kern, name=name, grid=(d, nb),
        out_shape=[jax.ShapeDtypeStruct((S, ATTN_OUT), F32)] * 2,
        in_specs=[blk(0, False), blk(1, True), blk(1, False), blk(2, True), blk(2, False),
                  pl.BlockSpec((1, HEADS_PER_GROUP, BLOCK, 2 * BLOCK), lambda r, n: (jnp.minimum(n, 1), 0, 0, 0))],
        out_specs=[o_spec, o_spec],
        compiler_params=_params(("parallel", "arbitrary")),
    )(qkv, qkv, qkv, qkv, qkv, bias)


def _attn_bwd(qkv, do, lse, delta, bias, d, name):
    S = qkv.shape[0]
    nb = S // d // BLOCK
    low = -3.0e38

    def kern(q_ref, kp_ref, kc_ref, vp_ref, vc_ref, do_ref, lse_ref, dl_ref, b_ref,
             dq_ref, dkv_ref, db_ref, ck_ref, cv_ref):
        r, n = pl.program_id(0), pl.program_id(1)

        @pl.when((r == 0) & (n == 0))
        def _():
            db_ref[...] = jnp.zeros_like(db_ref)

        @pl.when(n == 0)
        def _():
            ck_ref[...] = jnp.zeros_like(ck_ref)
            cv_ref[...] = jnp.zeros_like(cv_ref)

        @pl.when(n < nb)
        def _():
            lo, hi = _head_masks()
            for p in range(HEADS_PER_GROUP // 2):
                sl = slice(128 * p, 128 * (p + 1))
                sv = slice(ATTN_OUT + 128 * p, ATTN_OUT + 128 * (p + 1))
                q = q_ref[:, sl]
                k = jnp.concatenate([kp_ref[:, sl], kc_ref[:, sl]], axis=0)
                v = jnp.concatenate([vp_ref[:, sl], vc_ref[:, sl]], axis=0)
                dov = do_ref[:, sl]
                lse_b = lse_ref[:, sl]
                del_b = dl_ref[:, sl]
                zero = jnp.zeros_like(q)
                q2 = jnp.concatenate([jnp.where(lo, q, zero), jnp.where(hi, q, zero)], axis=0)
                do2 = jnp.concatenate([jnp.where(lo, dov, zero), jnp.where(hi, dov, zero)], axis=0)
                lse2 = jnp.concatenate([jnp.max(jnp.where(msk, lse_b, low), axis=1, keepdims=True) for msk in (lo, hi)], axis=0)
                del2 = jnp.concatenate([jnp.max(jnp.where(msk, del_b, low), axis=1, keepdims=True) for msk in (lo, hi)], axis=0)
                b2 = jnp.concatenate([b_ref[0, 2 * p], b_ref[0, 2 * p + 1]], axis=0)
                s = lax.dot_general(q2, k, NT, preferred_element_type=F32) * SCALE + b2
                pr = jnp.exp(s - lse2)
                dp = lax.dot_general(do2, v, NT, preferred_element_type=F32)
                ds = pr * (dp - del2)
                db_ref[2 * p] += ds[0:BLOCK]
                db_ref[2 * p + 1] += ds[BLOCK:2 * BLOCK]
                dsb = (ds * SCALE).astype(BF16)
                dq2 = lax.dot_general(dsb, k, NN, preferred_element_type=F32)
                dk_acc = lax.dot_general(dsb, q2, TN, preferred_element_type=F32)
                dv_acc = lax.dot_general(pr.astype(BF16), do2, TN, preferred_element_type=F32)
                dq_ref[:, sl] = jnp.where(lo, dq2[0:BLOCK], dq2[BLOCK:2 * BLOCK]).astype(BF16)
                dkv_ref[:, sl] = (ck_ref[:, sl] + dk_acc[0:BLOCK]).astype(BF16)
                dkv_ref[:, sv] = (cv_ref[:, sl] + dv_acc[0:BLOCK]).astype(BF16)
                ck_ref[:, sl] = dk_acc[BLOCK:2 * BLOCK]
                cv_ref[:, sl] = dv_acc[BLOCK:2 * BLOCK]

        @pl.when(n == nb)
        def _():
            dkv_ref[:, 0:ATTN_OUT] = ck_ref[...].astype(BF16)
            dkv_ref[:, ATTN_OUT:2 * ATTN_OUT] = cv_ref[...].astype(BF16)

    def cur(n):
        return jnp.minimum(n, nb - 1)

    def blk(col, prev):
        if prev:
            return pl.BlockSpec((BLOCK, ATTN_OUT), lambda r, n: (r * nb + jnp.maximum(cur(n) - 1, 0), col))
        return pl.BlockSpec((BLOCK, ATTN_OUT), lambda r, n: (r * nb + cur(n), col))

    q_like = pl.BlockSpec((BLOCK, ATTN_OUT), lambda r, n: (r * nb + cur(n), 0))
    return pl.pallas_call(
        kern, name=name, grid=(d, nb + 1),
        out_shape=[jax.ShapeDtypeStruct((S, ATTN_OUT), BF16), jax.ShapeDtypeStruct((S, 2 * ATTN_OUT), BF16),
                   jax.ShapeDtypeStruct((HEADS_PER_GROUP, BLOCK, 2 * BLOCK), F32)],
        in_specs=[blk(0, False), blk(1, True), blk(1, False), blk(2, True), blk(2, False),
                  q_like, q_like, q_like,
                  pl.BlockSpec((1, HEADS_PER_GROUP, BLOCK, 2 * BLOCK),
                               lambda r, n: (jnp.minimum(cur(n), 1), 0, 0, 0))],
        out_specs=[q_like,
                   pl.BlockSpec((BLOCK, 2 * ATTN_OUT), lambda r, n: (r * nb + jnp.maximum(n - 1, 0), 0)),
                   pl.BlockSpec((HEADS_PER_GROUP, BLOCK, 2 * BLOCK), lambda r, n: (0, 0, 0))],
        scratch_shapes=[pltpu.VMEM((BLOCK, ATTN_OUT), F32), pltpu.VMEM((BLOCK, ATTN_OUT), F32)],
        compiler_params=_params(("arbitrary", "arbitrary")),
    )(qkv, qkv, qkv, qkv, qkv, do, lse, delta, bias)


def _by_residue(a, dil):
    return a if dil == 1 else a.reshape(dil, a.shape[0] // dil, a.shape[1])


def _flat(a):
    return a if a.ndim == 2 else a.reshape(a.shape[0] * a.shape[1], a.shape[2])


def _combine_fwd(os_, lses, name):
    S, W = os_[0].shape
    tm = _pick(S, [512, 256, 128])
    perm = [dil for dil in DILS if dil > 1]

    def kern(*refs):
        o_in, l_in = refs[0:N_GROUPS], refs[N_GROUPS:2 * N_GROUPS]
        of_ref, ob_ref, lse_ref = refs[2 * N_GROUPS:2 * N_GROUPS + 3]
        lse_p = refs[2 * N_GROUPS + 3:2 * N_GROUPS + 3 + len(perm)]
        scr = refs[2 * N_GROUPS + 3 + len(perm):]
        ov, lv = [], []
        si = 0
        for g, dil in enumerate(DILS):
            if dil == 1:
                ov.append(o_in[g][...])
                lv.append(l_in[g][...])
            else:
                so, sl = scr[si], scr[si + 1]
                si += 2
                for res in range(dil):
                    _put_residue(so, res, dil, o_in[g][res])
                    _put_residue(sl, res, dil, l_in[g][res])
                ov.append(_unstage(so))
                lv.append(_unstage(sl))
        m = jnp.maximum(jnp.maximum(lv[0], lv[1]), lv[2])
        e = [jnp.exp(t - m) for t in lv]
        tot = e[0] + e[1] + e[2]
        o = (e[0] * ov[0] + e[1] * ov[1] + e[2] * ov[2]) / tot
        lse = m + jnp.log(tot)
        of_ref[...] = o
        ob_ref[...] = o.astype(BF16)
        lse_ref[...] = lse
        sl = scr[1]
        _stage(sl, lse)
        for dil, p_ref in zip(perm, lse_p):
            for res in range(dil):
                p_ref[res] = _get_residue(sl, res, dil)

    def in_spec(dil):
        return _row_spec(tm, W) if dil == 1 else _perm_spec(dil, tm, W)

    ins = [_by_residue(a, dil) for a, dil in zip(os_, DILS)] + [_by_residue(a, dil) for a, dil in zip(lses, DILS)]
    return pl.pallas_call(
        kern, name=name, grid=(S // tm,),
        out_shape=[jax.ShapeDtypeStruct((S, W), F32), jax.ShapeDtypeStruct((S, W), BF16),
                   jax.ShapeDtypeStruct((S, W), F32)]
        + [jax.ShapeDtypeStruct((dil, S // dil, W), F32) for dil in perm],
        in_specs=[in_spec(dil) for dil in DILS] * 2,
        out_specs=[_row_spec(tm, W)] * 3 + [_perm_spec(dil, tm, W) for dil in perm],
        scratch_shapes=[_stage_shape(tm, W) for _ in range(2 * len(perm))],
        compiler_params=_params(("parallel",)),
    )(*ins)


def _delta(do, o, name):
    S, W = o.shape
    tm = _pick(S, [512, 256, 128])
    perm = [dil for dil in DILS if dil > 1]

    def kern(do_ref, o_ref, dob_ref, d_ref, *rest):
        scr, scr_do = rest[2 * len(perm)], rest[2 * len(perm) + 1]
        prod = do_ref[...] * o_ref[...]
        ri = jnp.right_shift(lax.broadcasted_iota(jnp.int32, (W, W), 0), HEAD_SHIFT)
        ci = jnp.right_shift(lax.broadcasted_iota(jnp.int32, (W, W), 1), HEAD_SHIFT)
        same = jnp.where(ri == ci, 1.0, 0.0).astype(BF16)
        hi_p = prod.astype(BF16)
        lo_p = (prod - hi_p.astype(F32)).astype(BF16)
        dl = (lax.dot_general(hi_p, same, NN, preferred_element_type=F32)
              + lax.dot_general(lo_p, same, NN, preferred_element_type=F32))
        d_ref[...] = dl
        dob_ref[...] = do_ref[...].astype(BF16)
        _stage(scr, dl)
        _stage(scr_do, do_ref[...])
        for j, dil in enumerate(perm):
            for res in range(dil):
                rest[2 * j][res] = _get_residue(scr_do, res, dil).astype(BF16)
                rest[2 * j + 1][res] = _get_residue(scr, res, dil)

    out_shape = [jax.ShapeDtypeStruct((S, W), BF16), jax.ShapeDtypeStruct((S, W), F32)]
    out_specs = [_row_spec(tm, W), _row_spec(tm, W)]
    for dil in perm:
        out_shape += [jax.ShapeDtypeStruct((dil, S // dil, W), BF16), jax.ShapeDtypeStruct((dil, S // dil, W), F32)]
        out_specs += [_perm_spec(dil, tm, W), _perm_spec(dil, tm, W)]
    return pl.pallas_call(
        kern, name=name, grid=(S // tm,),
        out_shape=out_shape,
        in_specs=[_row_spec(tm, W), _row_spec(tm, W)], out_specs=out_specs,
        scratch_shapes=[_stage_shape(tm, W), _stage_shape(tm, W)],
        compiler_params=_params(("parallel",)),
    )(do, o)


def _ada_fwd(c16, ada_w, name):
    depth, D, n = ada_w.shape
    rows = 2 * N_DEV

    def kern(c_ref, w_ref, o_ref, cs_ref):
        cv = c_ref[...]
        cs = cv * jax.nn.sigmoid(cv)
        cs_ref[...] = cs
        o_ref[0] = _dot3(cs, w_ref[0], NN)

    return pl.pallas_call(
        kern, name=name, grid=(depth,),
        out_shape=[jax.ShapeDtypeStruct((depth, rows, n), F32), jax.ShapeDtypeStruct((rows, D), F32)],
        in_specs=[pl.BlockSpec((rows, D), lambda l: (0, 0)), pl.BlockSpec((1, D, n), lambda l: (l, 0, 0))],
        out_specs=[pl.BlockSpec((1, rows, n), lambda l: (l, 0, 0)), pl.BlockSpec((rows, D), lambda l: (0, 0))],
        compiler_params=_params(("arbitrary",)),
    )(c16, ada_w)


def _ada_bwd(cs16, dm16, name):
    depth, _, n = dm16.shape
    D = cs16.shape[1]

    def kern(cs_ref, dm_ref, o_ref):
        o_ref[0] = _dot3(cs_ref[...], dm_ref[0], TN)

    return pl.pallas_call(
        kern, name=name, grid=(depth,),
        out_shape=jax.ShapeDtypeStruct((depth, D, n), F32),
        in_specs=[pl.BlockSpec((2 * N_DEV, D), lambda l: (0, 0)), pl.BlockSpec((1, 2 * N_DEV, n), lambda l: (l, 0, 0))],
        out_specs=pl.BlockSpec((1, D, n), lambda l: (l, 0, 0)),
        compiler_params=_params(("parallel",)),
    )(cs16, dm16)


def _sum_rows8(parts, name):
    _, r, n = parts.shape

    def kern(p_ref, o_ref):
        acc = p_ref[0]
        for k in range(1, N_DEV):
            acc = acc + p_ref[k]
        o_ref[...] = acc

    return pl.pallas_call(
        kern, name=name, out_shape=jax.ShapeDtypeStruct((r, n), F32),
        in_specs=[pl.BlockSpec(memory_space=pltpu.VMEM)], out_specs=pl.BlockSpec(memory_space=pltpu.VMEM),
    )(parts)


def _adamw(w, g, m, v, name):
    shape = w.shape
    c = shape[-1]
    r = int(np.prod(shape[:-1])) if len(shape) > 1 else 1
    w2, g2, m2, v2 = (t.reshape(r, c) for t in (w, g, m, v))
    tr = r
    for cand in (2048, 1024, 512, 256, 128, 64, 32, 16, 8):
        if r % cand == 0 and cand * c * 4 <= (1 << 20):
            tr = cand
            break
    c1 = 1.0 - ADAM_B1 ** ADAM_STEP
    c2 = 1.0 - ADAM_B2 ** ADAM_STEP

    def kern(w_ref, g_ref, m_ref, v_ref, d_ref, nm_ref, nv_ref):
        gv = g_ref[...]
        nm = ADAM_B1 * m_ref[...] + (1.0 - ADAM_B1) * gv
        nv = ADAM_B2 * v_ref[...] + (1.0 - ADAM_B2) * (gv * gv)
        nm_ref[...] = nm
        nv_ref[...] = nv
        d_ref[...] = -ADAM_LR * ((nm / c1) / (jnp.sqrt(nv / c2) + ADAM_EPS) + ADAM_WD * w_ref[...])

    spec = pl.BlockSpec((tr, c), lambda i: (i, 0))
    outs = pl.pallas_call(
        kern, name=name, grid=(r // tr,),
        out_shape=[jax.ShapeDtypeStruct((r, c), F32)] * 3,
        in_specs=[spec] * 4, out_specs=[spec] * 3,
        compiler_params=_params(("parallel",)),
    )(w2, g2, m2, v2)
    return tuple(o.reshape(shape) for o in outs)


def _adamw_slab(w3, g, m3, v3, idx, prev, name):
    ns, r, c = w3.shape
    tr = r
    for cand in (2048, 1024, 512, 256, 128, 64, 32, 16, 8):
        if r % cand == 0 and cand * c * 4 <= (1 << 20):
            tr = cand
            break
    c1 = 1.0 - ADAM_B1 ** ADAM_STEP
    c2 = 1.0 - ADAM_B2 ** ADAM_STEP

    def kern(w_ref, g_ref, m_ref, v_ref, p0, p1, p2, p3, go_ref, d_ref, nm_ref, nv_ref):
        gv = g_ref[...]
        nm = ADAM_B1 * m_ref[0] + (1.0 - ADAM_B1) * gv
        nv = ADAM_B2 * v_ref[0] + (1.0 - ADAM_B2) * (gv * gv)
        go_ref[0] = gv
        nm_ref[0] = nm
        nv_ref[0] = nv
        d_ref[0] = -ADAM_LR * ((nm / c1) / (jnp.sqrt(nv / c2) + ADAM_EPS) + ADAM_WD * w_ref[0])

    if prev is None:
        prev = [lax.empty((ns, r, c), F32) for _ in range(4)]
    slab = pl.BlockSpec((1, tr, c), lambda i: (idx, i, 0))
    return pl.pallas_call(
        kern, name=name, grid=(r // tr,),
        out_shape=[jax.ShapeDtypeStruct((ns, r, c), F32)] * 4,
        in_specs=[slab, pl.BlockSpec((tr, c), lambda i: (i, 0)), slab, slab] + [pl.BlockSpec(memory_space=pl.ANY)] * 4,
        out_specs=[slab] * 4,
        input_output_aliases={4: 0, 5: 1, 6: 2, 7: 3},
        compiler_params=_params(("parallel",)),
    )(w3, g, m3, v3, *prev)


def kernel(x, c, ada_w, ada_b, norm_g, ffn_w_gate, ffn_w_up, ffn_w_down, w_in, conv_w, w_conv_out, w_attn_out, w_o, rel_bias, final_g, loss_target, m_ada_w, m_ada_b, m_norm_g, m_ffn_w_gate, m_ffn_w_up, m_ffn_w_down, m_w_in, m_conv_w, m_w_conv_out, m_w_attn_out, m_w_o, m_rel_bias, m_final_g, v_ada_w, v_ada_b, v_norm_g, v_ffn_w_gate, v_ffn_w_up, v_ffn_w_down, v_w_in, v_conv_w, v_w_conv_out, v_w_attn_out, v_w_o, v_rel_bias, v_final_g):
    depth = ada_w.shape[0]
    S, D = x.shape[1], x.shape[2]
    me = 4 * lax.axis_index("x") + 2 * lax.axis_index("y") + lax.axis_index("c")
    x0 = x.reshape(S, D)
    target = loss_target.reshape(S, D)
    fsh = ffn_w_down.shape[2]
    insh = w_in.shape[2]
    dsh = D // N_DEV
    ao_rows = dsh * ATTN_OUT // D

    piece_rows = [fsh] * 6 + [insh, dsh, dsh, ao_rows]

    FIRST, MIXER, SECOND = [0, 2, 4], [6, 7, 8, 9], [1, 3, 5]

    def rows_of(idx):
        return [piece_rows[p] for p in idx]

    def pack(l, idx=None):
        def t(a):
            return jnp.transpose(a).astype(BF16)
        ps = [t(ffn_w_gate[l, 0]), t(ffn_w_gate[l, 1]), t(ffn_w_up[l, 0]), t(ffn_w_up[l, 1]),
              ffn_w_down[l, 0].astype(BF16), ffn_w_down[l, 1].astype(BF16), t(w_in[l]),
              w_conv_out[l].astype(BF16), w_o[l].astype(BF16), t(w_attn_out[l]).reshape(ao_rows, D)]
        return jnp.concatenate(ps if idx is None else [ps[p] for p in idx], axis=0)

    def mixer_weights(full):
        in_t = full[6]
        qkv_t = [jnp.concatenate([in_t[t * QKV_W + g * ATTN_OUT: t * QKV_W + (g + 1) * ATTN_OUT] for t in range(3)])
                 for g in range(N_GROUPS)]
        ao_t = full[9].reshape(N_DEV, dsh, ATTN_OUT).reshape(D, ATTN_OUT)
        return dict(qkv_t=qkv_t, cg_t=in_t[3 * QKV_W:], co=full[7], wo=full[8], ao_t=ao_t)

    def behind(v, token):
        return v + token[0, 0].astype(v.dtype)

    c_all = _all_gather(c.reshape(D // 128, 128), "c_all_gather").reshape(N_DEV, D)
    c16 = jnp.concatenate([c_all, jnp.zeros_like(c_all)], axis=0)
    mod_part, cs16 = _ada_fwd(c16, ada_w, "ada_fwd")
    mod_part = mod_part[:, :N_DEV]
    n_ada = ada_w.shape[2]
    mod_all = _all_gather(mod_part.reshape(depth * N_DEV * n_ada // 128, 128), "mod_all_gather")
    mod_all = mod_all.reshape(N_DEV, depth, N_DEV, n_ada)
    mod_mine = lax.dynamic_index_in_dim(mod_all, me, axis=2, keepdims=False)
    mod = jnp.transpose(mod_mine, (1, 0, 2)).reshape(depth, N_DEV * n_ada) + ada_b
    mod = mod.reshape(depth, 3, 3, 1, D)

    small = jnp.concatenate([norm_g.reshape(-1), conv_w.reshape(-1)]).reshape(-1, 128)
    small_all = _all_gather(small, "small_all_gather").reshape(N_DEV, -1)
    n_ng = norm_g.size
    norm_g_full = jnp.transpose(small_all[:, :n_ng].reshape(N_DEV, depth, 3, dsh), (1, 2, 0, 3)).reshape(depth, 3, 1, D)
    conv_w_full = jnp.transpose(small_all[:, n_ng:].reshape(N_DEV, depth, CONV_K, dsh), (1, 2, 0, 3)).reshape(depth, CONV_K, D)

    buckets = jnp.asarray(_bucket_tables())
    bias = _bias_build(rel_bias, buckets, "bias_build")
    perm_dils = tuple(dil for dil in DILS if dil > 1)

    chain_done = mod.reshape(-1)[:128] + small_all.reshape(-1)[:128]
    part0 = [(FIRST, "first"), (MIXER, "mixer"), (SECOND, "second")]
    started, after0 = [], chain_done
    for idx, tag in part0:
        started.append(_gather_start(pack(0, idx), rows_of(idx), after0, f"weights_gather_start_l0_{tag}"))
        after0 = started[-1][3]
    full0 = [None] * len(piece_rows)

    def arrive0(k, after_arr):
        idx, tag = part0[k]
        st = started[k]
        fw = _gather_forward(st[0], st[1], st[2], rows_of(idx), after_arr, f"weights_gather_forward_l0_{tag}")
        for p, z in zip(idx, _gather_finish(fw[0], fw[1], fw[2], after_arr, f"weights_gather_finish_l0_{tag}")):
            full0[p] = z
        return fw[3]

    arrive0(0, bias)
    W = [dict(g_t=[full0[0]], u_t=[full0[2]], down=[full0[4]])] + [None] * (depth - 1)

    saved = []
    xc = x0
    for l in range(depth):
        sv = {}
        gather, tie_sub, pin = None, 0, None
        if 0 < l < depth - 1:
            gather = _gather_start(pack(l + 1), piece_rows, W[l]["wo"], f"weights_gather_start_l{l + 1}")
        for sub in (0, 1, 2):
            if l == 0 and sub == 1:
                arrive0(1, xc)
                W[0].update(mixer_weights(full0))
                if depth > 1:
                    gather, tie_sub = _gather_start(pack(1), piece_rows, W[0]["wo"], "weights_gather_start_l1"), 1
            if l == 0 and sub == 2:
                arrive0(2, xc)
                W[0].update(g_t=full0[0:2], u_t=full0[2:4], down=full0[4:6])
            g, sh, sc, gt = norm_g_full[l, sub], mod[l, sub, 0], mod[l, sub, 1], mod[l, sub, 2]
            if sub == tie_sub and gather is not None:
                g = behind(g, gather[3])
            if l == 0 and sub == 0:
                g = behind(g, started[-1][3])
            if sub == 2 and pin is not None:
                g = behind(g, pin)
            rec = dict(x=xc)
            if sub != 1:
                i = 0 if sub == 0 else 1
                h = _norm_mod_fwd(xc, g, sc, sh, "norm_mod_fwd")[0]
                a, u, z = _ffn_up(h, W[l]["g_t"][i], W[l]["u_t"][i], "ffn_up")
                xc, f = _matmul(z, W[l]["down"][i], "nn", BF16, "ffn_down", tm=512, resid=(xc, gt, 0.5))
                rec.update(h=h, a=a, u=u, z=z, f=f)
            else:
                hs = _norm_mod_fwd(xc, g, sc, sh, "norm_mod_fwd_mixer", dils=perm_dils)
                h = hs[0]
                h_res = [h] + [_flat(t) for t in hs[1:]]
                cg = _matmul(h, W[l]["cg_t"], "nt", BF16, "mixer_cg")
                qkvs, os_, lses = [], [], []
                for gi, dil in enumerate(DILS):
                    qkv = _matmul(h_res[gi], W[l]["qkv_t"][gi], "nt", BF16, "mixer_qkv", tn=3 * ATTN_OUT)
                    o_g, lse_g = _attn_fwd(qkv, bias[gi], dil, f"attn_fwd_g{gi}")
                    qkvs.append(qkv)
                    os_.append(o_g)
                    lses.append(lse_g)
                comb = _combine_fwd(os_, lses, "combine_fwd")
                o_f, o_b, lse = comb[0:3]
                lse_res = [lse] + [_flat(t) for t in comb[3:]]
                yc_in = _conv_fwd(cg, conv_w_full[l], "conv_fwd")
                yc = _matmul(yc_in, W[l]["co"], "nn", BF16, "conv_out")
                ya = _matmul(o_b, W[l]["ao_t"], "nt", BF16, "attn_out")
                merged = _merge_fwd(cg, yc, ya, "merge_fwd")
                xc, f = _matmul(merged, W[l]["wo"], "nn", BF16, "mixer_out", resid=(xc, gt, 1.0))
                rec.update(h=h, h_res=h_res, qkvs=qkvs, cg=cg, o_f=o_f, o_b=o_b, lse_res=lse_res, yc_in=yc_in,
                           yc=yc, ya=ya, merged=merged, f=f)
                if gather is not None:
                    fwd = _gather_forward(gather[0], gather[1], gather[2], piece_rows, xc,
                                          f"weights_gather_forward_l{l + 1}")
                    pin = fwd[3]
            sv[sub] = rec
        if gather is not None:
            full = _gather_finish(fwd[0], fwd[1], fwd[2], xc, f"weights_gather_finish_l{l + 1}")
            W[l + 1] = dict(g_t=full[0:2], u_t=full[2:4], down=full[4:6], **mixer_weights(full))
        saved.append(sv)

    def gate_of(l, sub):
        return saved[l][sub]["f"], mod[l, sub, 2], (1.0 if sub == 1 else 0.5)

    def below(l, sub):
        if sub > 0:
            return gate_of(l, sub - 1)
        return gate_of(l - 1, 2) if l > 0 else None

    dx, head, df = _loss_head(xc, final_g.reshape(1, D), target, gate_of(depth - 1, 2), "loss_head")
    d_final_g = head[0]
    loss_part = head[2, 0]
    d_gate = head[4]

    d_mod = [[None] * 3 for _ in range(depth)]
    d_norm = [[None] * 3 for _ in range(depth)]
    d_conv = [None] * depth
    dlog = jnp.zeros((N_GROUPS, HEADS_PER_GROUP, BLOCK, 2 * BLOCK), F32)
    n_pieces = len(piece_rows)
    LATE = (0, 2, 4)
    EARLY = tuple(p for p in range(n_pieces) if p not in LATE)
    g_piece = [[None] * n_pieces for _ in range(depth)]

    piece_keys = (("g_t", 0), ("g_t", 1), ("u_t", 0), ("u_t", 1), ("down", 0), ("down", 1), "in_t", "co", "wo", "ao_t")

    def pieces_of(dW, idx):
        return [dW[piece_keys[p]].reshape(N_DEV * piece_rows[p], D) for p in idx]

    def finish_scatter(sc, idx, after_arr, layer, part=""):
        recv = _scatter_finish(sc[0], sc[1], sc[2], after_arr, f"grads_scatter_finish_l{layer}{part}")
        tot = _sum_sources(recv, "grads_sum")
        o = 0
        for p in idx:
            g_piece[layer][p] = tot[o:o + piece_rows[p]]
            o += piece_rows[p]
        return recv

    scatter = None
    early = None
    for l in reversed(range(depth)):
        dW = {}
        for sub in (2, 1, 0):
            rec = saved[l][sub]
            g, sc = norm_g_full[l, sub], mod[l, sub, 1]
            if sub == 2 and scatter is not None:
                df = behind(df, scatter[3])
            if sub == 0 and early is not None:
                df = behind(df, early[3])
            nxt = below(l, sub)
            if sub != 1:
                i = 0 if sub == 0 else 1
                dz = _matmul(df, W[l]["down"][i], "nt", BF16, "ffn_down_dx")
                dW["down", i] = _matmul(rec["z"], df, "tn", BF16, "ffn_down_dw")
                da, du, dh = _ffn_up_bwd(dz, rec["a"], rec["u"], W[l]["g_t"][i], W[l]["u_t"][i], "ffn_up_bwd")
                dW["g_t", i] = _matmul(da, rec["h"], "tn", BF16, "ffn_gate_dw")
                dW["u_t", i] = _matmul(du, rec["h"], "tn", BF16, "ffn_up_dw")
                res = _norm_mod_bwd(rec["x"], [dh], [], dx, g, sc, "norm_mod_bwd", nxt=nxt)
            else:
                dout = df
                dm = _matmul(dout, W[l]["wo"], "nt", BF16, "mixer_out_dx")
                dW["wo"] = _matmul(rec["merged"], dout, "tn", BF16, "mixer_out_dw")
                dyc, dya, dgg = _merge_bwd(rec["cg"], rec["yc"], rec["ya"], dm, "merge_bwd")
                dyc_in = _matmul(dyc, W[l]["co"], "nt", BF16, "conv_out_dx")
                dW["co"] = _matmul(rec["yc_in"], dyc, "tn", BF16, "conv_out_dw")
                do = _matmul(dya, W[l]["ao_t"], "nn", F32, "attn_out_dx")
                dW["ao_t"] = _matmul(dya, rec["o_b"], "tn", BF16, "attn_out_dw")
                dl = _delta(do, rec["o_f"], "attn_delta")
                do_res = [dl[0]] + [_flat(t) for t in dl[2::2]]
                del_res = [dl[1]] + [_flat(t) for t in dl[3::2]]
                dh_attn, dw_q, dw_kv, dlog_l = [], [], [], []
                for gi, dil in enumerate(DILS):
                    dq, dkv, dlg = _attn_bwd(rec["qkvs"][gi], do_res[gi], rec["lse_res"][gi], del_res[gi],
                                             bias[gi], dil, f"attn_bwd_g{gi}")
                    dlog_l.append(dlg)
                    dh_attn.append(_attn_dh(dq, dkv, W[l]["qkv_t"][gi], "attn_dh"))
                    dw_q.append(_matmul(dq, rec["h_res"][gi], "tn", BF16, "mixer_q_dw"))
                    dw_kv.append(_matmul(dkv, rec["h_res"][gi], "tn", BF16, "mixer_kv_dw"))
                dlog = dlog + jnp.stack(dlog_l)
                dcg, conv_sum = _conv_bwd(rec["cg"], dyc_in, dgg, conv_w_full[l], "conv_bwd")
                d_conv[l] = conv_sum[0:CONV_K]
                dh_cg = _matmul(dcg, W[l]["cg_t"], "nn", F32, "mixer_cg_dx")
                dw_cg = _matmul(dcg, rec["h"], "tn", BF16, "mixer_cg_dw")
                dW["in_t"] = jnp.concatenate(
                    dw_q + [t[:ATTN_OUT] for t in dw_kv] + [t[ATTN_OUT:] for t in dw_kv] + [dw_cg], axis=0)
                perm_parts = [(dil, _by_residue(dh_attn[gi], dil)) for gi, dil in enumerate(DILS) if dil > 1]
                res = _norm_mod_bwd(rec["x"], [dh_cg, dh_attn[0]], perm_parts, dx, g, sc, "norm_mod_bwd_mixer", nxt=nxt)
            dx, sums = res[0], res[1]
            d_mod[l][sub] = jnp.stack([sums[0], sums[2], d_gate])
            d_norm[l][sub] = sums[3]
            if nxt is not None:
                df, d_gate = res[2], sums[4]
            if l == 0 and sub == 1:
                after = dx
                if scatter is not None:
                    after = finish_scatter(scatter, range(n_pieces), dx, l + 1)
                    scatter = None
                early = _scatter_start(pieces_of(dW, EARLY), after, "grads_scatter_start_l0_early")
        if l > 0:
            after = dx
            if scatter is not None:
                after = finish_scatter(scatter, range(n_pieces), dx, l + 1)
            scatter = _scatter_start(pieces_of(dW, range(n_pieces)), after, f"grads_scatter_start_l{l}")
        else:
            late = _scatter_start(pieces_of(dW, LATE), dx, "grads_scatter_start_l0_late")
    grad_x = dx.reshape(1, S, D)
    d_rel = _bias_bwd(dlog, buckets, "bias_bwd")[:, :rel_bias.shape[1]]

    big = dict(
        ffn_w_gate=(ffn_w_gate, m_ffn_w_gate, v_ffn_w_gate, (0, 1), True, None),
        ffn_w_up=(ffn_w_up, m_ffn_w_up, v_ffn_w_up, (2, 3), True, None),
        ffn_w_down=(ffn_w_down, m_ffn_w_down, v_ffn_w_down, (4, 5), False, None),
        w_in=(w_in, m_w_in, v_w_in, (6,), True, None),
        w_conv_out=(w_conv_out, m_w_conv_out, v_w_conv_out, (7,), False, None),
        w_o=(w_o, m_w_o, v_w_o, (8,), False, None),
        w_attn_out=(w_attn_out, m_w_attn_out, v_w_attn_out, (9,), True, (dsh, ATTN_OUT)))
    big_out = {name: None for name in big}

    def adam_layer(l, token=None):
        for name, (w_, m_, v_, plist, transposed, shard_shape) in big.items():
            ns, r, cc = depth * len(plist), w_.shape[-2], w_.shape[-1]
            w3, m3, v3 = (t.reshape(ns, r, cc) for t in (w_, m_, v_))
            for j, p in enumerate(plist):
                gl = g_piece[l][p]
                if shard_shape is not None:
                    gl = gl.reshape(shard_shape)
                if transposed:
                    gl = jnp.transpose(gl)
                if token is not None:
                    gl = behind(gl, token)
                big_out[name] = _adamw_slab(w3, gl, m3, v3, l * len(plist) + j, big_out[name], "adamw_" + name)

    for l in range(depth - 1, 0, -1):
        adam_layer(l, late[3])
    done = [late[3]] + [st[1] for st in big_out.values() if st is not None]
    finish_scatter(early, EARLY, done, 0, "_early")
    finish_scatter(late, LATE, dx, 0, "_late")
    adam_layer(0)

    d_mod_flat = jnp.stack([jnp.stack(d_mod[l]) for l in range(depth)]).reshape(-1)
    d_norm_flat = jnp.stack([jnp.stack(d_norm[l]) for l in range(depth)]).reshape(-1)
    d_conv_flat = jnp.stack(d_conv).reshape(-1)
    vec = jnp.concatenate([d_mod_flat, d_norm_flat, d_conv_flat, d_rel.reshape(-1), d_final_g,
                           jnp.broadcast_to(loss_part, (128,))])
    pad = (-vec.size) % 1024
    vec = jnp.concatenate([vec, jnp.zeros((pad,), F32)]).reshape(-1, 128)
    parts = _all_gather(vec, "small_grads_all_gather").reshape(N_DEV, vec.shape[0], 128)
    tot = _sum_rows8(parts, "small_grads_sum").reshape(-1)
    o0 = 0
    g_ada_b = tot[o0:o0 + d_mod_flat.size].reshape(ada_b.shape)
    o0 += d_mod_flat.size
    g_norm_full = tot[o0:o0 + d_norm_flat.size].reshape(depth, 3, D)
    o0 += d_norm_flat.size
    g_conv_full = tot[o0:o0 + d_conv_flat.size].reshape(depth, CONV_K, D)
    o0 += d_conv_flat.size
    g_rel = tot[o0:o0 + rel_bias.size].reshape(rel_bias.shape)
    o0 += rel_bias.size
    g_final = tot[o0:o0 + D]
    o0 += D
    loss = tot[o0]
    g_norm = lax.dynamic_slice_in_dim(g_norm_full, me * dsh, dsh, axis=2)
    g_conv = lax.dynamic_slice_in_dim(g_conv_full, me * dsh, dsh, axis=2)

    dm_all = parts.reshape(N_DEV, -1)[:, :d_mod_flat.size].reshape(N_DEV, depth, N_DEV * n_ada)
    dm_cols = lax.dynamic_slice_in_dim(dm_all, me * n_ada, n_ada, axis=2)
    dm16 = jnp.concatenate([jnp.transpose(dm_cols, (1, 0, 2)), jnp.zeros((depth, N_DEV, n_ada), F32)], axis=1)
    g_ada_w = _ada_bwd(cs16, dm16, "ada_bwd")

    small = dict(ada_w=(ada_w, g_ada_w, m_ada_w, v_ada_w), ada_b=(ada_b, g_ada_b, m_ada_b, v_ada_b),
                 norm_g=(norm_g, g_norm, m_norm_g, v_norm_g), conv_w=(conv_w, g_conv, m_conv_w, v_conv_w),
                 rel_bias=(rel_bias, g_rel, m_rel_bias, v_rel_bias), final_g=(final_g, g_final, m_final_g, v_final_g))
    order = ("ada_w", "ada_b", "norm_g", "ffn_w_gate", "ffn_w_up", "ffn_w_down", "w_in", "conv_w", "w_conv_out",
             "w_attn_out", "w_o", "rel_bias", "final_g")
    res = {}
    for name in order:
        if name in big:
            res[name] = tuple(t.reshape(big[name][0].shape) for t in big_out[name])
        else:
            w_, g_, m_, v_ = small[name]
            res[name] = (g_,) + _adamw(w_, g_, m_, v_, "adamw_" + name)
    return (loss, grad_x, *[res[n][0] for n in order], *[res[n][1] for n in order],
            *[res[n][2] for n in order], *[res[n][3] for n in order])
```

```python
import functools

import numpy as np
import jax
import jax.numpy as jnp
from jax import lax
from jax.experimental import pallas as pl
from jax.experimental.pallas import tpu as pltpu

F32 = jnp.float32
BF16 = jnp.bfloat16

N_DEV = 8
HEAD_DIM = 64
HEAD_SHIFT = 6
HEADS_PER_GROUP = 8
DILATION_GROUPS = ((128, 1), (512, 4), (2048, 16))
DILS = tuple(d for _, d in DILATION_GROUPS)
N_GROUPS = len(DILATION_GROUPS)
ATTN_OUT = HEADS_PER_GROUP * HEAD_DIM
QKV_W = N_GROUPS * ATTN_OUT
BLOCK = 128
NUM_BUCKETS = 32
MAX_DISTANCE = 2048
CONV_K = 3
EPS = 1e-6
NEG_INF = -1e30
SCALE = HEAD_DIM ** -0.5

ADAM_LR = 0.001
ADAM_B1 = 0.9
ADAM_B2 = 0.999
ADAM_EPS = 1e-08
ADAM_WD = 0.01
ADAM_STEP = 10

V7X_VMEM_LIMIT = 48 * 1024 * 1024
MESH = pl.DeviceIdType.MESH

NN = (((1,), (0,)), ((), ()))
NT = (((1,), (1,)), ((), ()))
TN = (((0,), (0,)), ((), ()))


def _pick(dim, cands):
    for c in cands:
        if dim % c == 0:
            return c
    return dim


def _pick_k(K, cap=2816):
    if K <= cap or K % 128:
        return K
    best = 128
    for m in range(1, K // 128 + 1):
        if (K // 128) % m == 0 and 128 * m <= cap:
            best = 128 * m
    return best


def _params(sem):
    return pltpu.CompilerParams(dimension_semantics=sem, vmem_limit_bytes=V7X_VMEM_LIMIT)


def _all_gather(x_shard, name):
    m_per, n = x_shard.shape

    def body(x_ref, out_ref, send_sems, recv_sems, local_sem):
        x, y, c = lax.axis_index("x"), lax.axis_index("y"), lax.axis_index("c")
        me, sibling = (x, y, c), (x, y, 1 - c)
        chips = [(1 - x, y), (x, 1 - y), (1 - x, 1 - y)]

        def rows(px, py, pc):
            return out_ref.at[pl.ds((4 * px + 2 * py + pc) * m_per, m_per), :]

        def copy(k, block, to, src=None):
            return pltpu.make_async_remote_copy(
                src_ref=rows(*block) if src is None else src, dst_ref=rows(*block),
                send_sem=send_sems.at[k], recv_sem=recv_sems.at[k], device_id=to, device_id_type=MESH)

        mine = pltpu.make_async_copy(x_ref, rows(*me), local_sem)
        mine.start()
        first = [copy(0, me, sibling, src=x_ref)]
        first += [copy(1 + j, me, (*chip, c), src=x_ref) for j, chip in enumerate(chips)]
        for cp in first:
            cp.start()
        passed = [copy(4 + j, (*chip, c), sibling) for j, chip in enumerate(chips)]
        for j, chip in enumerate(chips):
            copy(1 + j, (*chip, c), me).wait_recv()
            passed[j].start()
        copy(0, sibling, me).wait_recv()
        for j, chip in enumerate(chips):
            copy(4 + j, (*chip, 1 - c), me).wait_recv()
        for cp in first + passed:
            cp.wait_send()
        mine.wait()

    return pl.pallas_call(
        body, name=name,
        out_shape=jax.ShapeDtypeStruct((N_DEV * m_per, n), x_shard.dtype),
        in_specs=[pl.BlockSpec(memory_space=pltpu.VMEM)],
        out_specs=pl.BlockSpec(memory_space=pltpu.VMEM),
        scratch_shapes=[pltpu.SemaphoreType.DMA((7,)), pltpu.SemaphoreType.DMA((7,)), pltpu.SemaphoreType.DMA],
    )(x_shard)


def _offsets(piece_rows):
    offs, o = [], 0
    for n in piece_rows:
        offs.append(o)
        o += n
    return offs


HBM_SPEC = pl.BlockSpec(memory_space=pltpu.HBM)
SEM_SPEC = pl.BlockSpec(memory_space=pltpu.SEMAPHORE)
ANY_SPEC = pl.BlockSpec(memory_space=pl.ANY)
SPLIT_COPY_PARAMS = pltpu.CompilerParams(has_side_effects=pltpu.SideEffectType.DATAFLOW_SIDE_EFFECTING)


def _in_hbm(a):
    return pltpu.with_memory_space_constraint(a, pltpu.HBM)


def _dma_sems(n):
    return [pltpu.SemaphoreType.DMA(())] * n


def _whole(ref, send_sem, recv_sem, me):
    return pltpu.make_async_remote_copy(src_ref=ref, dst_ref=ref, send_sem=send_sem, recv_sem=recv_sem,
                                        device_id=me, device_id_type=MESH)


def _gather_start(packed, piece_rows, after, name):
    R, w = packed.shape
    offs = _offsets(piece_rows)
    P = len(piece_rows)
    assert offs[-1] + piece_rows[-1] == R

    def body(*refs):
        src_ref = refs[0]
        o = refs[P + 2:]
        send, recv = o[0:4], o[4:8]
        zones, token, stage, local_sems = o[9:9 + P], o[9 + P], o[10 + P], o[11 + P]
        x, y, c = lax.axis_index("x"), lax.axis_index("y"), lax.axis_index("c")
        targets = [(x, y, 1 - c), (1 - x, y, c), (x, 1 - y, c), (1 - x, 1 - y, c)]
        me = 4 * x + 2 * y + c

        def piece(p, ref):
            return ref.at[pl.ds(offs[p], piece_rows[p]), :]

        def rows(p):
            return zones[p].at[pl.ds(me * piece_rows[p], piece_rows[p]), :]

        for k, to in enumerate(targets):
            for p in range(P):
                pltpu.make_async_remote_copy(src_ref=piece(p, src_ref), dst_ref=rows(p), send_sem=send[k],
                                             recv_sem=recv[k], device_id=to, device_id_type=MESH).start()
        load = pltpu.make_async_copy(src_ref, stage, local_sems.at[P])
        load.start()
        load.wait()
        mine = [pltpu.make_async_copy(piece(p, stage), rows(p), local_sems.at[p]) for p in range(P)]
        for cp in mine:
            cp.start()
        for cp in mine:
            cp.wait()
        token[...] = jnp.zeros_like(token)

    zones_in = [_in_hbm(lax.empty((N_DEV * n, w), packed.dtype)) for n in piece_rows]
    outs = pl.pallas_call(
        body, name=name,
        out_shape=(*_dma_sems(8), pltpu.HBM((R, w), packed.dtype),
                   *[pltpu.HBM((N_DEV * n, w), packed.dtype) for n in piece_rows],
                   jax.ShapeDtypeStruct((8, 128), F32)),
        in_specs=[HBM_SPEC] * (P + 1) + [ANY_SPEC],
        out_specs=[SEM_SPEC] * 8 + [HBM_SPEC] * (P + 1) + [pl.BlockSpec(memory_space=pltpu.VMEM)],
        input_output_aliases={0: 8, **{1 + p: 9 + p for p in range(P)}},
        scratch_shapes=[pltpu.VMEM((R, w), packed.dtype), pltpu.SemaphoreType.DMA((P + 1,))],
        compiler_params=SPLIT_COPY_PARAMS,
    )(_in_hbm(packed), *zones_in, after)
    return outs[0:8], outs[8], list(outs[9:9 + P]), outs[9 + P]


def _gather_forward(sems, packed, zones, piece_rows, after, name):
    P = len(piece_rows)

    def body(*refs):
        src_ref = refs[0]
        s = refs[1 + P:9 + P]
        o = refs[10 + P:]
        send, recv = s[0:4], s[4:8]
        send2, recv2, zones_o = o[0:3], o[3:6], o[7:7 + P]
        x, y, c = lax.axis_index("x"), lax.axis_index("y"), lax.axis_index("c")
        me = (x, y, c)
        chips = [(1 - x, y), (x, 1 - y), (1 - x, 1 - y)]
        for j, (px, py) in enumerate(chips):
            _whole(src_ref, send[1 + j], recv[1 + j], me).wait_recv()
            blk = 4 * px + 2 * py + c
            for p in range(P):
                r = zones_o[p].at[pl.ds(blk * piece_rows[p], piece_rows[p]), :]
                pltpu.make_async_remote_copy(src_ref=r, dst_ref=r, send_sem=send2[j], recv_sem=recv2[j],
                                             device_id=(x, y, 1 - c), device_id_type=MESH).start()
        _whole(src_ref, send[0], recv[0], me).wait_recv()
        for k in range(4):
            _whole(src_ref, send[k], recv[k], me).wait_send()
        o[7 + P][...] = jnp.zeros_like(o[7 + P])

    outs = pl.pallas_call(
        body, name=name,
        out_shape=(*_dma_sems(6), pltpu.HBM(packed.shape, packed.dtype),
                   *[pltpu.HBM(z.shape, z.dtype) for z in zones], jax.ShapeDtypeStruct((8, 128), F32)),
        in_specs=[HBM_SPEC] * (P + 1) + [SEM_SPEC] * 8 + [ANY_SPEC],
        out_specs=[SEM_SPEC] * 6 + [HBM_SPEC] * (P + 1) + [pl.BlockSpec(memory_space=pltpu.VMEM)],
        input_output_aliases={0: 6, **{1 + p: 7 + p for p in range(P)}},
        compiler_params=SPLIT_COPY_PARAMS,
    )(packed, *zones, *sems, after)
    return outs[0:6], outs[6], list(outs[7:7 + P]), outs[7 + P]


def _gather_finish(sems2, packed, zones, after, name):
    P = len(zones)

    def body(*refs):
        src_ref = refs[0]
        s = refs[1 + P:7 + P]
        x, y, c = lax.axis_index("x"), lax.axis_index("y"), lax.axis_index("c")
        for j in range(3):
            _whole(src_ref, s[j], s[3 + j], (x, y, c)).wait_recv()
        for j in range(3):
            _whole(src_ref, s[j], s[3 + j], (x, y, c)).wait_send()

    outs = pl.pallas_call(
        body, name=name,
        out_shape=(pltpu.HBM(packed.shape, packed.dtype), *[pltpu.HBM(z.shape, z.dtype) for z in zones]),
        in_specs=[HBM_SPEC] * (P + 1) + [SEM_SPEC] * 6 + [ANY_SPEC],
        out_specs=[HBM_SPEC] * (P + 1),
        input_output_aliases={p: p for p in range(P + 1)},
        compiler_params=SPLIT_COPY_PARAMS,
    )(packed, *zones, *sems2, after)
    return list(outs[1:1 + P])


def _scatter_start(pieces, after, name):
    P = len(pieces)
    w = pieces[0].shape[1]
    piece_rows = [p.shape[0] // N_DEV for p in pieces]
    offs = _offsets(piece_rows)
    R = offs[-1] + piece_rows[-1]

    def body(*refs):
        o = refs[P + 2:]
        send, recv = o[0:7], o[7:14]
        srcs, dst_ref, token, stage, local_sems = o[14:14 + P], o[14 + P], o[15 + P], o[16 + P], o[17 + P]
        x, y, c = lax.axis_index("x"), lax.axis_index("y"), lax.axis_index("c")
        me = 4 * x + 2 * y + c

        def chunk(p, dev):
            return srcs[p].at[pl.ds(dev * piece_rows[p], piece_rows[p]), :]

        def slot(p, dev):
            return dst_ref.at[dev, pl.ds(offs[p], piece_rows[p]), :]

        for k in range(1, N_DEV):
            px = 1 - x if (k >> 2) & 1 else x
            py = 1 - y if (k >> 1) & 1 else y
            pc = 1 - c if k & 1 else c
            peer = 4 * px + 2 * py + pc
            for p in range(P):
                pltpu.make_async_remote_copy(
                    src_ref=chunk(p, peer), dst_ref=slot(p, me), send_sem=send[k - 1], recv_sem=recv[k - 1],
                    device_id=(px, py, pc), device_id_type=MESH).start()
        mine = [pltpu.make_async_copy(chunk(p, me), stage.at[pl.ds(offs[p], piece_rows[p]), :], local_sems.at[p])
                for p in range(P)]
        for cp in mine:
            cp.start()
        for cp in mine:
            cp.wait()
        store = pltpu.make_async_copy(stage, dst_ref.at[me], local_sems.at[P])
        store.start()
        store.wait()
        token[...] = jnp.zeros_like(token)

    dtype = pieces[0].dtype
    outs = pl.pallas_call(
        body, name=name,
        out_shape=(*_dma_sems(14), *[pltpu.HBM(p.shape, dtype) for p in pieces], pltpu.HBM((N_DEV, R, w), dtype),
                   jax.ShapeDtypeStruct((8, 128), F32)),
        in_specs=[HBM_SPEC] * (P + 1) + [ANY_SPEC],
        out_specs=[SEM_SPEC] * 14 + [HBM_SPEC] * (P + 1) + [pl.BlockSpec(memory_space=pltpu.VMEM)],
        input_output_aliases={p: 14 + p for p in range(P + 1)},
        scratch_shapes=[pltpu.VMEM((R, w), dtype), pltpu.SemaphoreType.DMA((P + 1,))],
        compiler_params=SPLIT_COPY_PARAMS,
    )(*[_in_hbm(p) for p in pieces], _in_hbm(lax.empty((N_DEV, R, w), dtype)), after)
    return outs[0:14], list(outs[14:14 + P]), outs[14 + P], outs[15 + P]


def _scatter_finish(sems, pieces, recv, after, name):
    P = len(pieces)
    after = list(after) if isinstance(after, (list, tuple)) else [after]

    def body(*refs):
        dst_ref = refs[P]
        s = refs[P + 1:P + 15]
        x, y, c = lax.axis_index("x"), lax.axis_index("y"), lax.axis_index("c")
        for k in range(7):
            _whole(dst_ref.at[0], s[k], s[7 + k], (x, y, c)).wait_recv()
        for k in range(7):
            _whole(dst_ref.at[0], s[k], s[7 + k], (x, y, c)).wait_send()

    outs = pl.pallas_call(
        body, name=name,
        out_shape=(*[pltpu.HBM(p.shape, p.dtype) for p in pieces], pltpu.HBM(recv.shape, recv.dtype)),
        in_specs=[HBM_SPEC] * (P + 1) + [SEM_SPEC] * 14 + [ANY_SPEC] * len(after),
        out_specs=[HBM_SPEC] * (P + 1),
        input_output_aliases={p: p for p in range(P + 1)},
        compiler_params=SPLIT_COPY_PARAMS,
    )(*pieces, recv, *sems, *after)
    return outs[P]


def _sum_sources(parts, name):
    _, r, n = parts.shape
    tr = _pick(r, [256, 128, 64, 32, 16, 8])

    def kern(p_ref, o_ref):
        acc = p_ref[0].astype(F32)
        for k in range(1, N_DEV):
            acc = acc + p_ref[k].astype(F32)
        o_ref[...] = acc

    return pl.pallas_call(
        kern, name=name, grid=(r // tr,),
        out_shape=jax.ShapeDtypeStruct((r, n), F32),
        in_specs=[pl.BlockSpec((N_DEV, tr, n), lambda i: (0, i, 0))],
        out_specs=pl.BlockSpec((tr, n), lambda i: (i, 0)),
        compiler_params=_params(("parallel",)),
    )(parts)


def _matmul(a, b, mode, out_dtype, name, tm=None, tn=None, tk=None, resid=None):
    if mode == "nn":
        (M, K), N = a.shape, b.shape[1]
    elif mode == "nt":
        (M, K), N = a.shape, b.shape[0]
    else:
        (K, M), N = a.shape, b.shape[1]
    dims = {"nn": NN, "nt": NT, "tn": TN}[mode]
    tm = tm or _pick(M, [1024, 1408, 512, 256, 128])
    tn = tn or _pick(N, [1024, 1408, 512, 256, 128])
    tk = tk or _pick_k(K)
    nk = K // tk
    a_spec = {"nn": pl.BlockSpec((tm, tk), lambda i, j, k: (i, k)),
              "nt": pl.BlockSpec((tm, tk), lambda i, j, k: (i, k)),
              "tn": pl.BlockSpec((tk, tm), lambda i, j, k: (k, i))}[mode]
    b_spec = {"nn": pl.BlockSpec((tk, tn), lambda i, j, k: (k, j)),
              "nt": pl.BlockSpec((tn, tk), lambda i, j, k: (j, k)),
              "tn": pl.BlockSpec((tk, tn), lambda i, j, k: (k, j))}[mode]
    o_spec = pl.BlockSpec((tm, tn), lambda i, j, k: (i, j))
    n_in = 2 if resid is None else 4
    n_out = 1 if resid is None else 2

    def kern(*refs):
        a_ref, b_ref = refs[0], refs[1]
        outs = refs[n_in:n_in + n_out]
        acc_ref = refs[n_in + n_out] if nk > 1 else None

        def finish(acc):
            if resid is None:
                outs[0][...] = acc.astype(out_dtype)
            else:
                x_ref, g_ref = refs[2], refs[3]
                outs[0][...] = x_ref[...] + (resid[2] * g_ref[...]) * acc
                outs[1][...] = acc.astype(out_dtype)

        part = lax.dot_general(a_ref[...], b_ref[...], dims, preferred_element_type=F32)
        if nk == 1:
            finish(part)
        else:
            k = pl.program_id(2)

            @pl.when(k == 0)
            def _():
                acc_ref[...] = part

            @pl.when(k > 0)
            def _():
                acc_ref[...] += part

            @pl.when(k == nk - 1)
            def _():
                finish(acc_ref[...])

    in_specs = [a_spec, b_spec]
    args = [a, b]
    out_shape = [jax.ShapeDtypeStruct((M, N), out_dtype)]
    out_specs = [o_spec]
    if resid is not None:
        in_specs += [o_spec, pl.BlockSpec((1, tn), lambda i, j, k: (0, j))]
        args += [resid[0], resid[1]]
        out_shape = [jax.ShapeDtypeStruct((M, N), F32)] + out_shape
        out_specs = [o_spec, o_spec]
    res = pl.pallas_call(
        kern, name=name, grid=(M // tm, N // tn, nk),
        out_shape=out_shape, in_specs=in_specs, out_specs=out_specs,
        scratch_shapes=[pltpu.VMEM((tm, tn), F32)] if nk > 1 else [],
        compiler_params=_params(("parallel", "parallel", "arbitrary")),
    )(*args)
    return res[0] if resid is None else res


def _dot3(a, b, dims):
    ah = a.astype(BF16)
    al = (a - ah.astype(F32)).astype(BF16)
    bh = b.astype(BF16)
    bl = (b - bh.astype(F32)).astype(BF16)
    d = functools.partial(lax.dot_general, dimension_numbers=dims, preferred_element_type=F32)
    return d(ah, bh) + (d(ah, bl) + d(al, bh))


def _silu_parts(a):
    sg = jax.nn.sigmoid(a)
    return a * sg, sg * (1.0 + a * (1.0 - sg))


def _ffn_up(h, wg_t, wu_t, name):
    S, D = h.shape
    F = wg_t.shape[0]
    tm = _pick(S, [512, 256, 128])
    tn = _pick(F, [1408, 512, 256, 128])

    def kern(h_ref, g_ref, u_ref, a_out, u_out, z_out):
        hv = h_ref[...]
        a = lax.dot_general(hv, g_ref[...], NT, preferred_element_type=F32)
        u = lax.dot_general(hv, u_ref[...], NT, preferred_element_type=F32)
        a_out[...] = a.astype(BF16)
        u_out[...] = u.astype(BF16)
        z_out[...] = (_silu_parts(a)[0] * u).astype(BF16)

    w_spec = pl.BlockSpec((tn, D), lambda j, i: (j, 0))
    o_spec = pl.BlockSpec((tm, tn), lambda j, i: (i, j))
    return pl.pallas_call(
        kern, name=name, grid=(F // tn, S // tm),
        out_shape=[jax.ShapeDtypeStruct((S, F), BF16)] * 3,
        in_specs=[pl.BlockSpec((tm, D), lambda j, i: (i, 0)), w_spec, w_spec],
        out_specs=[o_spec] * 3,
        compiler_params=_params(("parallel", "parallel")),
    )(h, wg_t, wu_t)


def _ffn_up_bwd(dz, a, u, wg_t, wu_t, name):
    S, F = dz.shape
    D = wg_t.shape[1]
    tm = _pick(S, [512, 256, 128])
    tk = _pick(F, [1408, 512, 256, 128])
    nk = F // tk

    def kern(dz_ref, a_ref, u_ref, g_ref, w_ref, da_out, du_out, dh_out, acc_ref):
        k = pl.program_id(1)
        av = a_ref[...].astype(F32)
        uv = u_ref[...].astype(F32)
        dzv = dz_ref[...].astype(F32)
        silu, dsilu = _silu_parts(av)
        da = (dzv * uv * dsilu).astype(BF16)
        du = (dzv * silu).astype(BF16)
        da_out[...] = da
        du_out[...] = du
        part = (lax.dot_general(da, g_ref[...], NN, preferred_element_type=F32)
                + lax.dot_general(du, w_ref[...], NN, preferred_element_type=F32))

        @pl.when(k == 0)
        def _():
            acc_ref[...] = part

        @pl.when(k > 0)
        def _():
            acc_ref[...] += part

        @pl.when(k == nk - 1)
        def _():
            dh_out[...] = acc_ref[...]

    t_spec = pl.BlockSpec((tm, tk), lambda i, k: (i, k))
    w_spec = pl.BlockSpec((tk, D), lambda i, k: (k, 0))
    return pl.pallas_call(
        kern, name=name, grid=(S // tm, nk),
        out_shape=[jax.ShapeDtypeStruct((S, F), BF16)] * 2 + [jax.ShapeDtypeStruct((S, D), F32)],
        in_specs=[t_spec, t_spec, t_spec, w_spec, w_spec],
        out_specs=[t_spec, t_spec, pl.BlockSpec((tm, D), lambda i, k: (i, 0))],
        scratch_shapes=[pltpu.VMEM((tm, D), F32)],
        compiler_params=_params(("parallel", "arbitrary")),
    )(dz, a, u, wg_t, wu_t)


def _attn_dh(dq, dkv, w_t, name):
    S = dq.shape[0]
    D = w_t.shape[1]
    tm = _pick(S, [1024, 512, 256, 128])

    def kern(dq_ref, dk_ref, dv_ref, wq_ref, wk_ref, wv_ref, o_ref):
        o_ref[...] = (lax.dot_general(dq_ref[...], wq_ref[...], NN, preferred_element_type=F32)
                      + lax.dot_general(dk_ref[...], wk_ref[...], NN, preferred_element_type=F32)
                      + lax.dot_general(dv_ref[...], wv_ref[...], NN, preferred_element_type=F32))

    def w_blk(j):
        return pl.BlockSpec((ATTN_OUT, D), lambda i: (j, 0))

    return pl.pallas_call(
        kern, name=name, grid=(S // tm,),
        out_shape=jax.ShapeDtypeStruct((S, D), F32),
        in_specs=[pl.BlockSpec((tm, ATTN_OUT), lambda i: (i, 0)), pl.BlockSpec((tm, ATTN_OUT), lambda i: (i, 0)),
                  pl.BlockSpec((tm, ATTN_OUT), lambda i: (i, 1)), w_blk(0), w_blk(1), w_blk(2)],
        out_specs=pl.BlockSpec((tm, D), lambda i: (i, 0)),
        compiler_params=_params(("parallel",)),
    )(dq, dkv, dkv, w_t, w_t, w_t)


def _row_spec(tm, d):
    return pl.BlockSpec((tm, d), lambda i: (i, 0))


def _vec_spec(d, rows=1):
    return pl.BlockSpec((rows, d), lambda i: (0, 0))


def _perm_spec(dil, tm, w):
    return pl.BlockSpec((dil, tm // dil, w), lambda i: (0, i, 0))


def _stage_shape(tm, w):
    return pltpu.VMEM((w // 128, tm, 128), F32)


def _stage(scr, val):
    for ci in range(scr.shape[0]):
        scr[ci] = val[:, 128 * ci:128 * (ci + 1)]


def _unstage(scr):
    return jnp.concatenate([scr[ci] for ci in range(scr.shape[0])], axis=1)


def _get_residue(scr, res, dil):
    n = scr.shape[1] // dil
    return jnp.concatenate([scr[ci, pl.ds(res, n, stride=dil), :] for ci in range(scr.shape[0])], axis=1)


def _put_residue(scr, res, dil, val):
    n = scr.shape[1] // dil
    for ci in range(scr.shape[0]):
        scr[ci, pl.ds(res, n, stride=dil), :] = val[:, 128 * ci:128 * (ci + 1)]


def _norm_mod_fwd(x, g, s, b, name, dils=()):
    S, D = x.shape
    tm = _pick(S, [512, 256, 128])

    def kern(x_ref, g_ref, s_ref, b_ref, h_ref, *rest):
        xv = x_ref[...]
        r = lax.rsqrt(jnp.mean(xv * xv, axis=1, keepdims=True) + EPS)
        hv = xv * r * g_ref[...] * (1.0 + s_ref[...]) + b_ref[...]
        h_ref[...] = hv.astype(BF16)
        if dils:
            scr = rest[len(dils)]
            _stage(scr, hv)
            for dil, p_ref in zip(dils, rest[:len(dils)]):
                for res in range(dil):
                    p_ref[res] = _get_residue(scr, res, dil).astype(BF16)

    return pl.pallas_call(
        kern, name=name, grid=(S // tm,),
        out_shape=[jax.ShapeDtypeStruct((S, D), BF16)] + [jax.ShapeDtypeStruct((dil, S // dil, D), BF16) for dil in dils],
        in_specs=[_row_spec(tm, D), _vec_spec(D), _vec_spec(D), _vec_spec(D)],
        out_specs=[_row_spec(tm, D)] + [_perm_spec(dil, tm, D) for dil in dils],
        scratch_shapes=[_stage_shape(tm, D)] if dils else [],
        compiler_params=_params(("parallel",)),
    )(x, g, s, b)


def _gate_part(dx, nxt_refs, coef, df_ref, cs_ref):
    f_ref, gate_ref = nxt_refs
    df_ref[...] = ((coef * gate_ref[...]) * dx).astype(BF16)
    cs_ref[4:5, :] += coef * jnp.sum(f_ref[...].astype(F32) * dx, axis=0, keepdims=True)


def _norm_mod_bwd(x, dh_nat, dh_perm, dxo, g, s, name, nxt=None):
    S, D = x.shape
    tm = _pick(S, [512, 256, 128])
    n = S // tm
    n_nat, n_perm = len(dh_nat), len(dh_perm)
    n_nxt = 0 if nxt is None else 2

    def kern(*refs):
        x_ref = refs[0]
        nat = refs[1:1 + n_nat]
        perm = refs[1 + n_nat:1 + n_nat + n_perm]
        base = 1 + n_nat + n_perm
        dxo_ref, g_ref, s_ref = refs[base:base + 3]
        nxt_refs = refs[base + 3:base + 3 + n_nxt]
        dx_ref, cs_ref = refs[base + 3 + n_nxt:base + 5 + n_nxt]
        rest = refs[base + 5 + n_nxt:]
        df_ref = rest[0] if nxt is not None else None
        scr = rest[1:] if nxt is not None else rest
        i = pl.program_id(0)
        xv = x_ref[...]
        r = lax.rsqrt(jnp.mean(xv * xv, axis=1, keepdims=True) + EPS)
        xn = xv * r
        dh_v = nat[0][...].astype(F32)
        for t in nat[1:]:
            dh_v = dh_v + t[...].astype(F32)
        for (dil, _), p_ref, sc in zip(dh_perm, perm, scr):
            for res in range(dil):
                _put_residue(sc, res, dil, p_ref[res])
            dh_v = dh_v + _unstage(sc)
        one_s = 1.0 + s_ref[...]
        dxn = dh_v * (g_ref[...] * one_s)
        dx = dxo_ref[...] + r * (dxn - xn * jnp.mean(xn * dxn, axis=1, keepdims=True))
        dx_ref[...] = dx

        @pl.when(i == 0)
        def _():
            cs_ref[...] = jnp.zeros_like(cs_ref)

        cs_ref[0:1, :] += jnp.sum(dh_v, axis=0, keepdims=True)
        cs_ref[1:2, :] += jnp.sum(dh_v * xn, axis=0, keepdims=True)
        if nxt is not None:
            _gate_part(dx, nxt_refs, nxt[2], df_ref, cs_ref)

        @pl.when(i == n - 1)
        def _():
            t = cs_ref[1:2, :]
            cs_ref[2:3, :] = g_ref[...] * t
            cs_ref[3:4, :] = one_s * t

    nxt_specs = [] if nxt is None else [_row_spec(tm, D), _vec_spec(D)]
    nxt_args = [] if nxt is None else [nxt[0], nxt[1]]
    return pl.pallas_call(
        kern, name=name, grid=(n,),
        out_shape=[jax.ShapeDtypeStruct((S, D), F32), jax.ShapeDtypeStruct((8, D), F32)]
        + ([] if nxt is None else [jax.ShapeDtypeStruct((S, D), BF16)]),
        in_specs=[_row_spec(tm, D)] + [_row_spec(tm, D)] * n_nat + [_perm_spec(dil, tm, D) for dil, _ in dh_perm]
        + [_row_spec(tm, D), _vec_spec(D), _vec_spec(D)] + nxt_specs,
        out_specs=[_row_spec(tm, D), _vec_spec(D, 8)] + ([] if nxt is None else [_row_spec(tm, D)]),
        scratch_shapes=[_stage_shape(tm, D) for _ in dh_perm],
        compiler_params=_params(("arbitrary",)),
    )(x, *dh_nat, *[a for _, a in dh_perm], dxo, g, s, *nxt_args)


def _loss_head(x, g, target, nxt, name):
    S, D = x.shape
    tm = _pick(S, [512, 256, 128])
    n = S // tm

    def kern(x_ref, g_ref, t_ref, f_ref, gate_ref, dx_ref, cs_ref, df_ref):
        i = pl.program_id(0)
        xv = x_ref[...]
        r = lax.rsqrt(jnp.mean(xv * xv, axis=1, keepdims=True) + EPS)
        xn = xv * r
        e = xn * g_ref[...] - t_ref[...]
        dxn = (e * (1.0 / D)) * g_ref[...]
        dx = r * (dxn - xn * jnp.mean(xn * dxn, axis=1, keepdims=True))
        dx_ref[...] = dx

        @pl.when(i == 0)
        def _():
            cs_ref[...] = jnp.zeros_like(cs_ref)

        cs_ref[0:1, :] += jnp.sum(xn * e, axis=0, keepdims=True) * (1.0 / D)
        cs_ref[1:2, :] += jnp.sum(e * e, axis=0, keepdims=True)
        _gate_part(dx, (f_ref, gate_ref), nxt[2], df_ref, cs_ref)

        @pl.when(i == n - 1)
        def _():
            tot = jnp.sum(cs_ref[1:2, :], axis=1, keepdims=True) * (0.5 / D)
            cs_ref[2:3, :] = jnp.broadcast_to(tot, (1, D))

    return pl.pallas_call(
        kern, name=name, grid=(n,),
        out_shape=[jax.ShapeDtypeStruct((S, D), F32), jax.ShapeDtypeStruct((8, D), F32),
                   jax.ShapeDtypeStruct((S, D), BF16)],
        in_specs=[_row_spec(tm, D), _vec_spec(D), _row_spec(tm, D), _row_spec(tm, D), _vec_spec(D)],
        out_specs=[_row_spec(tm, D), _vec_spec(D, 8), _row_spec(tm, D)],
        compiler_params=_params(("arbitrary",)),
    )(x, g, target, nxt[0], nxt[1])


def _shift_down(p, row, prev_rows):
    a, b = prev_rows
    p1 = jnp.where(row == 0, b, pltpu.roll(p, 1, 0))
    p2 = jnp.where(row == 0, a, jnp.where(row == 1, b, pltpu.roll(p, 2, 0)))
    return p1, p2


def _conv_fwd(cg, conv_w, name):
    S, D5 = cg.shape
    D = D5 // 5
    tm = _pick(S, [512, 256, 128])
    t8 = tm // 8

    def prev(col):
        return pl.BlockSpec((8, D), lambda i: (jnp.maximum(i * t8 - 1, 0), col))

    def kern(cb_ref, cc_ref, ch_ref, ccp_ref, chp_ref, w_ref, y_ref):
        i = pl.program_id(0)
        keep = jnp.where(i > 0, 1.0, 0.0)
        p = cc_ref[...].astype(F32) * ch_ref[...].astype(F32)
        pa = ccp_ref[6:7, :].astype(F32) * chp_ref[6:7, :].astype(F32) * keep
        pb = ccp_ref[7:8, :].astype(F32) * chp_ref[7:8, :].astype(F32) * keep
        row = lax.broadcasted_iota(jnp.int32, (tm, D), 0)
        p1, p2 = _shift_down(p, row, (pa, pb))
        dw = w_ref[0:1, :] * p2 + w_ref[1:2, :] * p1 + w_ref[2:3, :] * p
        y_ref[...] = (cb_ref[...].astype(F32) * dw).astype(BF16)

    def col(cidx):
        return pl.BlockSpec((tm, D), lambda i: (i, cidx))

    return pl.pallas_call(
        kern, name=name, grid=(S // tm,),
        out_shape=jax.ShapeDtypeStruct((S, D), BF16),
        in_specs=[col(0), col(1), col(2), prev(1), prev(2), _vec_spec(D, CONV_K)],
        out_specs=_row_spec(tm, D),
        compiler_params=_params(("parallel",)),
    )(cg, cg, cg, cg, cg, conv_w)


def _conv_bwd(cg, dy, dgg, conv_w, name):
    S, D5 = cg.shape
    D = D5 // 5
    tm = _pick(S, [512, 256, 128])
    t8 = tm // 8
    n = S // tm
    last8 = S // 8 - 1

    def prev(col):
        return pl.BlockSpec((8, D), lambda i: (jnp.maximum(i * t8 - 1, 0), col))

    def nxt(col):
        return pl.BlockSpec((8, D), lambda i: (jnp.minimum((i + 1) * t8, last8), col))

    def kern(cb_ref, cc_ref, ch_ref, dy_ref, dgg_ref, ccp_ref, chp_ref, cbn_ref, dyn_ref, w_ref, d_ref, cs_ref):
        i = pl.program_id(0)
        keep_p = jnp.where(i > 0, 1.0, 0.0)
        keep_n = jnp.where(i < n - 1, 1.0, 0.0)
        cb = cb_ref[...].astype(F32)
        cc = cc_ref[...].astype(F32)
        ch = ch_ref[...].astype(F32)
        dyv = dy_ref[...].astype(F32)
        p = cc * ch
        pa = ccp_ref[6:7, :].astype(F32) * chp_ref[6:7, :].astype(F32) * keep_p
        pb = ccp_ref[7:8, :].astype(F32) * chp_ref[7:8, :].astype(F32) * keep_p
        row = lax.broadcasted_iota(jnp.int32, (tm, D), 0)
        p1, p2 = _shift_down(p, row, (pa, pb))
        w0, w1, w2 = w_ref[0:1, :], w_ref[1:2, :], w_ref[2:3, :]
        dw = w0 * p2 + w1 * p1 + w2 * p
        ddw = dyv * cb
        na = dyn_ref[0:1, :].astype(F32) * cbn_ref[0:1, :].astype(F32) * keep_n
        nb = dyn_ref[1:2, :].astype(F32) * cbn_ref[1:2, :].astype(F32) * keep_n
        u1 = jnp.where(row == tm - 1, na, pltpu.roll(ddw, tm - 1, 0))
        u2 = jnp.where(row == tm - 2, na, jnp.where(row == tm - 1, nb, pltpu.roll(ddw, tm - 2, 0)))
        dp = w2 * ddw + w1 * u1 + w0 * u2
        d_ref[:, 0:D] = (dyv * dw).astype(BF16)
        d_ref[:, D:2 * D] = (dp * ch).astype(BF16)
        d_ref[:, 2 * D:3 * D] = (dp * cc).astype(BF16)
        d_ref[:, 3 * D:5 * D] = dgg_ref[...]

        @pl.when(i == 0)
        def _():
            cs_ref[...] = jnp.zeros_like(cs_ref)

        cs_ref[0:1, :] += jnp.sum(ddw * p2, axis=0, keepdims=True)
        cs_ref[1:2, :] += jnp.sum(ddw * p1, axis=0, keepdims=True)
        cs_ref[2:3, :] += jnp.sum(ddw * p, axis=0, keepdims=True)

    def col(cidx):
        return pl.BlockSpec((tm, D), lambda i: (i, cidx))

    return pl.pallas_call(
        kern, name=name, grid=(n,),
        out_shape=[jax.ShapeDtypeStruct((S, 5 * D), BF16), jax.ShapeDtypeStruct((8, D), F32)],
        in_specs=[col(0), col(1), col(2), _row_spec(tm, D), _row_spec(tm, 2 * D), prev(1), prev(2), nxt(0),
                  pl.BlockSpec((8, D), lambda i: (jnp.minimum((i + 1) * t8, last8), 0)), _vec_spec(D, CONV_K)],
        out_specs=[_row_spec(tm, 5 * D), _vec_spec(D, 8)],
        compiler_params=_params(("arbitrary",)),
    )(cg, cg, cg, dy, dgg, cg, cg, cg, dy, conv_w)


def _merge_fwd(cg, yc, ya, name):
    S, D = yc.shape
    tm = _pick(S, [512, 256, 128])

    def kern(gc_ref, ga_ref, yc_ref, ya_ref, m_ref):
        m_ref[...] = (jax.nn.sigmoid(gc_ref[...].astype(F32)) * yc_ref[...].astype(F32)
                      + jax.nn.sigmoid(ga_ref[...].astype(F32)) * ya_ref[...].astype(F32)).astype(BF16)

    return pl.pallas_call(
        kern, name=name, grid=(S // tm,),
        out_shape=jax.ShapeDtypeStruct((S, D), BF16),
        in_specs=[pl.BlockSpec((tm, D), lambda i: (i, 3)), pl.BlockSpec((tm, D), lambda i: (i, 4)),
                  _row_spec(tm, D), _row_spec(tm, D)],
        out_specs=_row_spec(tm, D),
        compiler_params=_params(("parallel",)),
    )(cg, cg, yc, ya)


def _merge_bwd(cg, yc, ya, dm, name):
    S, D = yc.shape
    tm = _pick(S, [512, 256, 128])

    def kern(gc_ref, ga_ref, yc_ref, ya_ref, dm_ref, dyc_ref, dya_ref, dg_ref):
        dmv = dm_ref[...].astype(F32)
        sc = jax.nn.sigmoid(gc_ref[...].astype(F32))
        sa = jax.nn.sigmoid(ga_ref[...].astype(F32))
        dyc_ref[...] = (dmv * sc).astype(BF16)
        dya_ref[...] = (dmv * sa).astype(BF16)
        dg_ref[:, 0:D] = (dmv * yc_ref[...].astype(F32) * (sc * (1.0 - sc))).astype(BF16)
        dg_ref[:, D:2 * D] = (dmv * ya_ref[...].astype(F32) * (sa * (1.0 - sa))).astype(BF16)

    return pl.pallas_call(
        kern, name=name, grid=(S // tm,),
        out_shape=[jax.ShapeDtypeStruct((S, D), BF16), jax.ShapeDtypeStruct((S, D), BF16),
                   jax.ShapeDtypeStruct((S, 2 * D), BF16)],
        in_specs=[pl.BlockSpec((tm, D), lambda i: (i, 3)), pl.BlockSpec((tm, D), lambda i: (i, 4)),
                  _row_spec(tm, D), _row_spec(tm, D), _row_spec(tm, D)],
        out_specs=[_row_spec(tm, D), _row_spec(tm, D), pl.BlockSpec((tm, 2 * D), lambda i: (i, 0))],
        compiler_params=_params(("parallel",)),
    )(cg, cg, yc, ya, dm)


def _t5_bucket(dist):
    exact = NUM_BUCKETS // 2
    d = np.maximum(dist, 1).astype(np.float32)
    large = exact + (np.log(d / exact) / np.log(MAX_DISTANCE / exact) * (NUM_BUCKETS - exact)).astype(np.int32)
    large = np.minimum(large, NUM_BUCKETS - 1)
    return np.where(dist < exact, dist, large).astype(np.int32)


def _bucket_tables():
    i = np.arange(BLOCK)[:, None]
    j = np.arange(2 * BLOCK)[None, :]
    rel = i - j + BLOCK
    return np.stack([_t5_bucket(np.maximum(rel, 0) * d) for _, d in DILATION_GROUPS]).astype(np.int32)


def _band_masks():
    i = lax.broadcasted_iota(jnp.int32, (BLOCK, 2 * BLOCK), 0)
    j = lax.broadcasted_iota(jnp.int32, (BLOCK, 2 * BLOCK), 1)
    rel = i - j + BLOCK
    band = (rel >= 0) & (rel <= BLOCK)
    return band, band & (j >= BLOCK)


def _bias_build(rel_bias, buckets, name):
    def kern(rb_ref, bk_ref, o_ref):
        g = pl.program_id(0)
        bk = bk_ref[0]
        band, first = _band_masks()
        for h in range(HEADS_PER_GROUP):
            acc = jnp.zeros((BLOCK, 2 * BLOCK), F32)
            for b in range(NUM_BUCKETS):
                acc = jnp.where(bk == b, rb_ref[b, g * HEADS_PER_GROUP + h], acc)
            o_ref[0, 0, h] = jnp.where(first, acc, NEG_INF)
            o_ref[0, 1, h] = jnp.where(band, acc, NEG_INF)

    return pl.pallas_call(
        kern, name=name, grid=(N_GROUPS,),
        out_shape=jax.ShapeDtypeStruct((N_GROUPS, 2, HEADS_PER_GROUP, BLOCK, 2 * BLOCK), F32),
        in_specs=[pl.BlockSpec(memory_space=pltpu.SMEM),
                  pl.BlockSpec((1, BLOCK, 2 * BLOCK), lambda g: (g, 0, 0))],
        out_specs=pl.BlockSpec((1, 2, HEADS_PER_GROUP, BLOCK, 2 * BLOCK), lambda g: (g, 0, 0, 0, 0)),
        compiler_params=_params(("parallel",)),
    )(rel_bias, buckets)


def _bias_bwd(dlog, buckets, name):
    def kern(dl_ref, bk_ref, o_ref):
        g = pl.program_id(0)
        bk = bk_ref[0]
        rowi = lax.broadcasted_iota(jnp.int32, (NUM_BUCKETS, 128), 0)
        coli = lax.broadcasted_iota(jnp.int32, (NUM_BUCKETS, 128), 1)

        @pl.when(g == 0)
        def _():
            o_ref[...] = jnp.zeros_like(o_ref)

        acc = jnp.zeros((NUM_BUCKETS, 128), F32)
        for h in range(HEADS_PER_GROUP):
            dv = dl_ref[0, h]
            for b in range(NUM_BUCKETS):
                t = jnp.sum(jnp.where(bk == b, dv, 0.0), axis=0, keepdims=True)
                t = jnp.sum(t, axis=1, keepdims=True)
                acc = acc + jnp.where((rowi == b) & (coli == g * HEADS_PER_GROUP + h), t, 0.0)
        o_ref[...] += acc

    return pl.pallas_call(
        kern, name=name, grid=(N_GROUPS,),
        out_shape=jax.ShapeDtypeStruct((NUM_BUCKETS, 128), F32),
        in_specs=[pl.BlockSpec((1, HEADS_PER_GROUP, BLOCK, 2 * BLOCK), lambda g: (g, 0, 0, 0)),
                  pl.BlockSpec((1, BLOCK, 2 * BLOCK), lambda g: (g, 0, 0))],
        out_specs=pl.BlockSpec((NUM_BUCKETS, 128), lambda g: (0, 0)),
        compiler_params=_params(("arbitrary",)),
    )(dlog, buckets)


def _head_masks():
    lane = lax.broadcasted_iota(jnp.int32, (BLOCK, 128), 1)
    lo = lane < HEAD_DIM
    return lo, jnp.logical_not(lo)


def _attn_fwd(qkv, bias, d, name):
    S = qkv.shape[0]
    nb = S // d // BLOCK

    def kern(q_ref, kp_ref, kc_ref, vp_ref, vc_ref, b_ref, o_ref, lse_ref):
        lo, hi = _head_masks()
        for p in range(HEADS_PER_GROUP // 2):
            sl = slice(128 * p, 128 * (p + 1))
            q = q_ref[:, sl]
            k = jnp.concatenate([kp_ref[:, sl], kc_ref[:, sl]], axis=0)
            v = jnp.concatenate([vp_ref[:, sl], vc_ref[:, sl]], axis=0)
            zero = jnp.zeros_like(q)
            q2 = jnp.concatenate([jnp.where(lo, q, zero), jnp.where(hi, q, zero)], axis=0)
            b2 = jnp.concatenate([b_ref[0, 2 * p], b_ref[0, 2 * p + 1]], axis=0)
            s = lax.dot_general(q2, k, NT, preferred_element_type=F32) * SCALE + b2
            m = jnp.max(s, axis=1, keepdims=True)
            e = jnp.exp(s - m)
            l = jnp.sum(e, axis=1, keepdims=True)
            o2 = lax.dot_general(e.astype(BF16), v, NN, preferred_element_type=F32) / l
            l2 = jnp.broadcast_to(m + jnp.log(l), (2 * BLOCK, 128))
            o_ref[:, sl] = jnp.where(lo, o2[0:BLOCK], o2[BLOCK:2 * BLOCK])
            lse_ref[:, sl] = jnp.where(lo, l2[0:BLOCK], l2[BLOCK:2 * BLOCK])

    def blk(col, prev):
        if prev:
            return pl.BlockSpec((BLOCK, ATTN_OUT), lambda r, n: (r * nb + jnp.maximum(n - 1, 0), col))
        return pl.BlockSpec((BLOCK, ATTN_OUT), lambda r, n: (r * nb + n, col))

    o_spec = pl.BlockSpec((BLOCK, ATTN_OUT), lambda r, n: (r * nb + n, 0))
    return pl.pallas_call(
        kern, name=name, grid=(d, nb),
        out_shape=[jax.ShapeDtypeStruct((S, ATTN_OUT), F32)] * 2,
        in_specs=[blk(0, False), blk(1, True), blk(1, False), blk(2, True), blk(2, False),
                  pl.BlockSpec((1, HEADS_PER_GROUP, BLOCK, 2 * BLOCK), lambda r, n: (jnp.minimum(n, 1), 0, 0, 0))],
        out_specs=[o_spec, o_spec],
        compiler_params=_params(("parallel", "arbitrary")),
    )(qkv, qkv, qkv, qkv, qkv, bias)


def _attn_bwd(qkv, do, lse, delta, bias, d, name):
    S = qkv.shape[0]
    nb = S // d // BLOCK
    low = -3.0e38

    def kern(q_ref, kp_ref, kc_ref, vp_ref, vc_ref, do_ref, lse_ref, dl_ref, b_ref,
             dq_ref, dkv_ref, db_ref, ck_ref, cv_ref):
        r, n = pl.program_id(0), pl.program_id(1)

        @pl.when((r == 0) & (n == 0))
        def _():
            db_ref[...] = jnp.zeros_like(db_ref)

        @pl.when(n == 0)
        def _():
            ck_ref[...] = jnp.zeros_like(ck_ref)
            cv_ref[...] = jnp.zeros_like(cv_ref)

        @pl.when(n < nb)
        def _():
            lo, hi = _head_masks()
            for p in range(HEADS_PER_GROUP // 2):
                sl = slice(128 * p, 128 * (p + 1))
                sv = slice(ATTN_OUT + 128 * p, ATTN_OUT + 128 * (p + 1))
                q = q_ref[:, sl]
                k = jnp.concatenate([kp_ref[:, sl], kc_ref[:, sl]], axis=0)
                v = jnp.concatenate([vp_ref[:, sl], vc_ref[:, sl]], axis=0)
                dov = do_ref[:, sl]
                lse_b = lse_ref[:, sl]
                del_b = dl_ref[:, sl]
                zero = jnp.zeros_like(q)
                q2 = jnp.concatenate([jnp.where(lo, q, zero), jnp.where(hi, q, zero)], axis=0)
                do2 = jnp.concatenate([jnp.where(lo, dov, zero), jnp.where(hi, dov, zero)], axis=0)
                lse2 = jnp.concatenate([jnp.max(jnp.where(msk, lse_b, low), axis=1, keepdims=True) for msk in (lo, hi)], axis=0)
                del2 = jnp.concatenate([jnp.max(jnp.where(msk, del_b, low), axis=1, keepdims=True) for msk in (lo, hi)], axis=0)
                b2 = jnp.concatenate([b_ref[0, 2 * p], b_ref[0, 2 * p + 1]], axis=0)
                s = lax.dot_general(q2, k, NT, preferred_element_type=F32) * SCALE + b2
                pr = jnp.exp(s - lse2)
                dp = lax.dot_general(do2, v, NT, preferred_element_type=F32)
                ds = pr * (dp - del2)
                db_ref[2 * p] += ds[0:BLOCK]
                db_ref[2 * p + 1] += ds[BLOCK:2 * BLOCK]
                dsb = (ds * SCALE).astype(BF16)
                dq2 = lax.dot_general(dsb, k, NN, preferred_element_type=F32)
                dk_acc = lax.dot_general(dsb, q2, TN, preferred_element_type=F32)
                dv_acc = lax.dot_general(pr.astype(BF16), do2, TN, preferred_element_type=F32)
                dq_ref[:, sl] = jnp.where(lo, dq2[0:BLOCK], dq2[BLOCK:2 * BLOCK]).astype(BF16)
                dkv_ref[:, sl] = (ck_ref[:, sl] + dk_acc[0:BLOCK]).astype(BF16)
                dkv_ref[:, sv] = (cv_ref[:, sl] + dv_acc[0:BLOCK]).astype(BF16)
                ck_ref[:, sl] = dk_acc[BLOCK:2 * BLOCK]
                cv_ref[:, sl] = dv_acc[BLOCK:2 * BLOCK]

        @pl.when(n == nb)
        def _():
            dkv_ref[:, 0:ATTN_OUT] = ck_ref[...].astype(BF16)
            dkv_ref[:, ATTN_OUT:2 * ATTN_OUT] = cv_ref[...].astype(BF16)

    def cur(n):
        return jnp.minimum(n, nb - 1)

    def blk(col, prev):
        if prev:
            return pl.BlockSpec((BLOCK, ATTN_OUT), lambda r, n: (r * nb + jnp.maximum(cur(n) - 1, 0), col))
        return pl.BlockSpec((BLOCK, ATTN_OUT), lambda r, n: (r * nb + cur(n), col))

    q_like = pl.BlockSpec((BLOCK, ATTN_OUT), lambda r, n: (r * nb + cur(n), 0))
    return pl.pallas_call(
        kern, name=name, grid=(d, nb + 1),
        out_shape=[jax.ShapeDtypeStruct((S, ATTN_OUT), BF16), jax.ShapeDtypeStruct((S, 2 * ATTN_OUT), BF16),
                   jax.ShapeDtypeStruct((HEADS_PER_GROUP, BLOCK, 2 * BLOCK), F32)],
        in_specs=[blk(0, False), blk(1, True), blk(1, False), blk(2, True), blk(2, False),
                  q_like, q_like, q_like,
                  pl.BlockSpec((1, HEADS_PER_GROUP, BLOCK, 2 * BLOCK),
                               lambda r, n: (jnp.minimum(cur(n), 1), 0, 0, 0))],
        out_specs=[q_like,
                   pl.BlockSpec((BLOCK, 2 * ATTN_OUT), lambda r, n: (r * nb + jnp.maximum(n - 1, 0), 0)),
                   pl.BlockSpec((HEADS_PER_GROUP, BLOCK, 2 * BLOCK), lambda r, n: (0, 0, 0))],
        scratch_shapes=[pltpu.VMEM((BLOCK, ATTN_OUT), F32), pltpu.VMEM((BLOCK, ATTN_OUT), F32)],
        compiler_params=_params(("arbitrary", "arbitrary")),
    )(qkv, qkv, qkv, qkv, qkv, do, lse, delta, bias)


def _by_residue(a, dil):
    return a if dil == 1 else a.reshape(dil, a.shape[0] // dil, a.shape[1])


def _flat(a):
    return a if a.ndim == 2 else a.reshape(a.shape[0] * a.shape[1], a.shape[2])


def _combine_fwd(os_, lses, name):
    S, W = os_[0].shape
    tm = _pick(S, [512, 256, 128])
    perm = [dil for dil in DILS if dil > 1]

    def kern(*refs):
        o_in, l_in = refs[0:N_GROUPS], refs[N_GROUPS:2 * N_GROUPS]
        of_ref, ob_ref, lse_ref = refs[2 * N_GROUPS:2 * N_GROUPS + 3]
        lse_p = refs[2 * N_GROUPS + 3:2 * N_GROUPS + 3 + len(perm)]
        scr = refs[2 * N_GROUPS + 3 + len(perm):]
        ov, lv = [], []
        si = 0
        for g, dil in enumerate(DILS):
            if dil == 1:
                ov.append(o_in[g][...])
                lv.append(l_in[g][...])
            else:
                so, sl = scr[si], scr[si + 1]
                si += 2
                for res in range(dil):
                    _put_residue(so, res, dil, o_in[g][res])
                    _put_residue(sl, res, dil, l_in[g][res])
                ov.append(_unstage(so))
                lv.append(_unstage(sl))
        m = jnp.maximum(jnp.maximum(lv[0], lv[1]), lv[2])
        e = [jnp.exp(t - m) for t in lv]
        tot = e[0] + e[1] + e[2]
        o = (e[0] * ov[0] + e[1] * ov[1] + e[2] * ov[2]) / tot
        lse = m + jnp.log(tot)
        of_ref[...] = o
        ob_ref[...] = o.astype(BF16)
        lse_ref[...] = lse
        sl = scr[1]
        _stage(sl, lse)
        for dil, p_ref in zip(perm, lse_p):
            for res in range(dil):
                p_ref[res] = _get_residue(sl, res, dil)

    def in_spec(dil):
        return _row_spec(tm, W) if dil == 1 else _perm_spec(dil, tm, W)

    ins = [_by_residue(a, dil) for a, dil in zip(os_, DILS)] + [_by_residue(a, dil) for a, dil in zip(lses, DILS)]
    return pl.pallas_call(
        kern, name=name, grid=(S // tm,),
        out_shape=[jax.ShapeDtypeStruct((S, W), F32), jax.ShapeDtypeStruct((S, W), BF16),
                   jax.ShapeDtypeStruct((S, W), F32)]
        + [jax.ShapeDtypeStruct((dil, S // dil, W), F32) for dil in perm],
        in_specs=[in_spec(dil) for dil in DILS] * 2,
        out_specs=[_row_spec(tm, W)] * 3 + [_perm_spec(dil, tm, W) for dil in perm],
        scratch_shapes=[_stage_shape(tm, W) for _ in range(2 * len(perm))],
        compiler_params=_params(("parallel",)),
    )(*ins)


def _delta(do, o, name):
    S, W = o.shape
    tm = _pick(S, [512, 256, 128])
    perm = [dil for dil in DILS if dil > 1]

    def kern(do_ref, o_ref, dob_ref, d_ref, *rest):
        scr, scr_do = rest[2 * len(perm)], rest[2 * len(perm) + 1]
        prod = do_ref[...] * o_ref[...]
        ri = jnp.right_shift(lax.broadcasted_iota(jnp.int32, (W, W), 0), HEAD_SHIFT)
        ci = jnp.right_shift(lax.broadcasted_iota(jnp.int32, (W, W), 1), HEAD_SHIFT)
        same = jnp.where(ri == ci, 1.0, 0.0).astype(BF16)
        hi_p = prod.astype(BF16)
        lo_p = (prod - hi_p.astype(F32)).astype(BF16)
        dl = (lax.dot_general(hi_p, same, NN, preferred_element_type=F32)
              + lax.dot_general(lo_p, same, NN, preferred_element_type=F32))
        d_ref[...] = dl
        dob_ref[...] = do_ref[...].astype(BF16)
        _stage(scr, dl)
        _stage(scr_do, do_ref[...])
        for j, dil in enumerate(perm):
            for res in range(dil):
                rest[2 * j][res] = _get_residue(scr_do, res, dil).astype(BF16)
                rest[2 * j + 1][res] = _get_residue(scr, res, dil)

    out_shape = [jax.ShapeDtypeStruct((S, W), BF16), jax.ShapeDtypeStruct((S, W), F32)]
    out_specs = [_row_spec(tm, W), _row_spec(tm, W)]
    for dil in perm:
        out_shape += [jax.ShapeDtypeStruct((dil, S // dil, W), BF16), jax.ShapeDtypeStruct((dil, S // dil, W), F32)]
        out_specs += [_perm_spec(dil, tm, W), _perm_spec(dil, tm, W)]
    return pl.pallas_call(
        kern, name=name, grid=(S // tm,),
        out_shape=out_shape,
        in_specs=[_row_spec(tm, W), _row_spec(tm, W)], out_specs=out_specs,
        scratch_shapes=[_stage_shape(tm, W), _stage_shape(tm, W)],
        compiler_params=_params(("parallel",)),
    )(do, o)


def _ada_fwd(c16, ada_w, name):
    depth, D, n = ada_w.shape
    rows = 2 * N_DEV

    def kern(c_ref, w_ref, o_ref, cs_ref):
        cv = c_ref[...]
        cs = cv * jax.nn.sigmoid(cv)
        cs_ref[...] = cs
        o_ref[0] = _dot3(cs, w_ref[0], NN)

    return pl.pallas_call(
        kern, name=name, grid=(depth,),
        out_shape=[jax.ShapeDtypeStruct((depth, rows, n), F32), jax.ShapeDtypeStruct((rows, D), F32)],
        in_specs=[pl.BlockSpec((rows, D), lambda l: (0, 0)), pl.BlockSpec((1, D, n), lambda l: (l, 0, 0))],
        out_specs=[pl.BlockSpec((1, rows, n), lambda l: (l, 0, 0)), pl.BlockSpec((rows, D), lambda l: (0, 0))],
        compiler_params=_params(("arbitrary",)),
    )(c16, ada_w)


def _ada_bwd(cs16, dm16, name):
    depth, _, n = dm16.shape
    D = cs16.shape[1]

    def kern(cs_ref, dm_ref, o_ref):
        o_ref[0] = _dot3(cs_ref[...], dm_ref[0], TN)

    return pl.pallas_call(
        kern, name=name, grid=(depth,),
        out_shape=jax.ShapeDtypeStruct((depth, D, n), F32),
        in_specs=[pl.BlockSpec((2 * N_DEV, D), lambda l: (0, 0)), pl.BlockSpec((1, 2 * N_DEV, n), lambda l: (l, 0, 0))],
        out_specs=pl.BlockSpec((1, D, n), lambda l: (l, 0, 0)),
        compiler_params=_params(("parallel",)),
    )(cs16, dm16)


def _sum_rows8(parts, name):
    _, r, n = parts.shape

    def kern(p_ref, o_ref):
        acc = p_ref[0]
        for k in range(1, N_DEV):
            acc = acc + p_ref[k]
        o_ref[...] = acc

    return pl.pallas_call(
        kern, name=name, out_shape=jax.ShapeDtypeStruct((r, n), F32),
        in_specs=[pl.BlockSpec(memory_space=pltpu.VMEM)], out_specs=pl.BlockSpec(memory_space=pltpu.VMEM),
    )(parts)


def _adamw(w, g, m, v, name):
    shape = w.shape
    c = shape[-1]
    r = int(np.prod(shape[:-1])) if len(shape) > 1 else 1
    w2, g2, m2, v2 = (t.reshape(r, c) for t in (w, g, m, v))
    tr = r
    for cand in (2048, 1024, 512, 256, 128, 64, 32, 16, 8):
        if r % cand == 0 and cand * c * 4 <= (1 << 20):
            tr = cand
            break
    c1 = 1.0 - ADAM_B1 ** ADAM_STEP
    c2 = 1.0 - ADAM_B2 ** ADAM_STEP

    def kern(w_ref, g_ref, m_ref, v_ref, d_ref, nm_ref, nv_ref):
        gv = g_ref[...]
        nm = ADAM_B1 * m_ref[...] + (1.0 - ADAM_B1) * gv
        nv = ADAM_B2 * v_ref[...] + (1.0 - ADAM_B2) * (gv * gv)
        nm_ref[...] = nm
        nv_ref[...] = nv
        d_ref[...] = -ADAM_LR * ((nm / c1) / (jnp.sqrt(nv / c2) + ADAM_EPS) + ADAM_WD * w_ref[...])

    spec = pl.BlockSpec((tr, c), lambda i: (i, 0))
    outs = pl.pallas_call(
        kern, name=name, grid=(r // tr,),
        out_shape=[jax.ShapeDtypeStruct((r, c), F32)] * 3,
        in_specs=[spec] * 4, out_specs=[spec] * 3,
        compiler_params=_params(("parallel",)),
    )(w2, g2, m2, v2)
    return tuple(o.reshape(shape) for o in outs)


def _adamw_slab(w3, g, m3, v3, idx, prev, name):
    ns, r, c = w3.shape
    tr = r
    for cand in (2048, 1024, 512, 256, 128, 64, 32, 16, 8):
        if r % cand == 0 and cand * c * 4 <= (1 << 20):
            tr = cand
            break
    c1 = 1.0 - ADAM_B1 ** ADAM_STEP
    c2 = 1.0 - ADAM_B2 ** ADAM_STEP

    def kern(w_ref, g_ref, m_ref, v_ref, p0, p1, p2, p3, go_ref, d_ref, nm_ref, nv_ref):
        gv = g_ref[...]
        nm = ADAM_B1 * m_ref[0] + (1.0 - ADAM_B1) * gv
        nv = ADAM_B2 * v_ref[0] + (1.0 - ADAM_B2) * (gv * gv)
        go_ref[0] = gv
        nm_ref[0] = nm
        nv_ref[0] = nv
        d_ref[0] = -ADAM_LR * ((nm / c1) / (jnp.sqrt(nv / c2) + ADAM_EPS) + ADAM_WD * w_ref[0])

    if prev is None:
        prev = [lax.empty((ns, r, c), F32) for _ in range(4)]
    slab = pl.BlockSpec((1, tr, c), lambda i: (idx, i, 0))
    return pl.pallas_call(
        kern, name=name, grid=(r // tr,),
        out_shape=[jax.ShapeDtypeStruct((ns, r, c), F32)] * 4,
        in_specs=[slab, pl.BlockSpec((tr, c), lambda i: (i, 0)), slab, slab] + [pl.BlockSpec(memory_space=pl.ANY)] * 4,
        out_specs=[slab] * 4,
        input_output_aliases={4: 0, 5: 1, 6: 2, 7: 3},
        compiler_params=_params(("parallel",)),
    )(w3, g, m3, v3, *prev)


def kernel(x, c, ada_w, ada_b, norm_g, ffn_w_gate, ffn_w_up, ffn_w_down, w_in, conv_w, w_conv_out, w_attn_out, w_o, rel_bias, final_g, loss_target, m_ada_w, m_ada_b, m_norm_g, m_ffn_w_gate, m_ffn_w_up, m_ffn_w_down, m_w_in, m_conv_w, m_w_conv_out, m_w_attn_out, m_w_o, m_rel_bias, m_final_g, v_ada_w, v_ada_b, v_norm_g, v_ffn_w_gate, v_ffn_w_up, v_ffn_w_down, v_w_in, v_conv_w, v_w_conv_out, v_w_attn_out, v_w_o, v_rel_bias, v_final_g):
    depth = ada_w.shape[0]
    S, D = x.shape[1], x.shape[2]
    me = 4 * lax.axis_index("x") + 2 * lax.axis_index("y") + lax.axis_index("c")
    x0 = x.reshape(S, D)
    target = loss_target.reshape(S, D)
    fsh = ffn_w_down.shape[2]
    insh = w_in.shape[2]
    dsh = D // N_DEV
    ao_rows = dsh * ATTN_OUT // D

    piece_rows = [fsh] * 6 + [insh, dsh, dsh, ao_rows]

    FIRST, MIXER, SECOND = [0, 2, 4], [6, 7, 8, 9], [1, 3, 5]

    def rows_of(idx):
        return [piece_rows[p] for p in idx]

    def pack(l, idx=None):
        def t(a):
            return jnp.transpose(a).astype(BF16)
        ps = [t(ffn_w_gate[l, 0]), t(ffn_w_gate[l, 1]), t(ffn_w_up[l, 0]), t(ffn_w_up[l, 1]),
              ffn_w_down[l, 0].astype(BF16), ffn_w_down[l, 1].astype(BF16), t(w_in[l]),
              w_conv_out[l].astype(BF16), w_o[l].astype(BF16), t(w_attn_out[l]).reshape(ao_rows, D)]
        return jnp.concatenate(ps if idx is None else [ps[p] for p in idx], axis=0)

    def mixer_weights(full):
        in_t = full[6]
        qkv_t = [jnp.concatenate([in_t[t * QKV_W + g * ATTN_OUT: t * QKV_W + (g + 1) * ATTN_OUT] for t in range(3)])
                 for g in range(N_GROUPS)]
        ao_t = full[9].reshape(N_DEV, dsh, ATTN_OUT).reshape(D, ATTN_OUT)
        return dict(qkv_t=qkv_t, cg_t=in_t[3 * QKV_W:], co=full[7], wo=full[8], ao_t=ao_t)

    def behind(v, token):
        return v + token[0, 0].astype(v.dtype)

    c_all = _all_gather(c.reshape(D // 128, 128), "c_all_gather").reshape(N_DEV, D)
    c16 = jnp.concatenate([c_all, jnp.zeros_like(c_all)], axis=0)
    mod_part, cs16 = _ada_fwd(c16, ada_w, "ada_fwd")
    mod_part = mod_part[:, :N_DEV]
    n_ada = ada_w.shape[2]
    mod_all = _all_gather(mod_part.reshape(depth * N_DEV * n_ada // 128, 128), "mod_all_gather")
    mod_all = mod_all.reshape(N_DEV, depth, N_DEV, n_ada)
    mod_mine = lax.dynamic_index_in_dim(mod_all, me, axis=2, keepdims=False)
    mod = jnp.transpose(mod_mine, (1, 0, 2)).reshape(depth, N_DEV * n_ada) + ada_b
    mod = mod.reshape(depth, 3, 3, 1, D)

    small = jnp.concatenate([norm_g.reshape(-1), conv_w.reshape(-1)]).reshape(-1, 128)
    small_all = _all_gather(small, "small_all_gather").reshape(N_DEV, -1)
    n_ng = norm_g.size
    norm_g_full = jnp.transpose(small_all[:, :n_ng].reshape(N_DEV, depth, 3, dsh), (1, 2, 0, 3)).reshape(depth, 3, 1, D)
    conv_w_full = jnp.transpose(small_all[:, n_ng:].reshape(N_DEV, depth, CONV_K, dsh), (1, 2, 0, 3)).reshape(depth, CONV_K, D)

    buckets = jnp.asarray(_bucket_tables())
    bias = _bias_build(rel_bias, buckets, "bias_build")
    perm_dils = tuple(dil for dil in DILS if dil > 1)

    chain_done = mod.reshape(-1)[:128] + small_all.reshape(-1)[:128]
    part0 = [(FIRST, "first"), (MIXER, "mixer"), (SECOND, "second")]
    started, after0 = [], chain_done
    for idx, tag in part0:
        started.append(_gather_start(pack(0, idx), rows_of(idx), after0, f"weights_gather_start_l0_{tag}"))
        after0 = started[-1][3]
    full0 = [None] * len(piece_rows)

    def arrive0(k, after_arr):
        idx, tag = part0[k]
        st = started[k]
        fw = _gather_forward(st[0], st[1], st[2], rows_of(idx), after_arr, f"weights_gather_forward_l0_{tag}")
        for p, z in zip(idx, _gather_finish(fw[0], fw[1], fw[2], after_arr, f"weights_gather_finish_l0_{tag}")):
            full0[p] = z
        return fw[3]

    arrive0(0, bias)
    W = [dict(g_t=[full0[0]], u_t=[full0[2]], down=[full0[4]])] + [None] * (depth - 1)

    saved = []
    xc = x0
    for l in range(depth):
        sv = {}
        gather, tie_sub, pin = None, 0, None
        if 0 < l < depth - 1:
            gather = _gather_start(pack(l + 1), piece_rows, W[l]["wo"], f"weights_gather_start_l{l + 1}")
        for sub in (0, 1, 2):
            if l == 0 and sub == 1:
                arrive0(1, xc)
                W[0].update(mixer_weights(full0))
                if depth > 1:
                    gather, tie_sub = _gather_start(pack(1), piece_rows, W[0]["wo"], "weights_gather_start_l1"), 1
            if l == 0 and sub == 2:
                arrive0(2, xc)
                W[0].update(g_t=full0[0:2], u_t=full0[2:4], down=full0[4:6])
            g, sh, sc, gt = norm_g_full[l, sub], mod[l, sub, 0], mod[l, sub, 1], mod[l, sub, 2]
            if sub == tie_sub and gather is not None:
                g = behind(g, gather[3])
            if l == 0 and sub == 0:
                g = behind(g, started[-1][3])
            if sub == 2 and pin is not None:
                g = behind(g, pin)
            rec = dict(x=xc)
            if sub != 1:
                i = 0 if sub == 0 else 1
                h = _norm_mod_fwd(xc, g, sc, sh, "norm_mod_fwd")[0]
                a, u, z = _ffn_up(h, W[l]["g_t"][i], W[l]["u_t"][i], "ffn_up")
                xc, f = _matmul(z, W[l]["down"][i], "nn", BF16, "ffn_down", tm=512, resid=(xc, gt, 0.5))
                rec.update(h=h, a=a, u=u, z=z, f=f)
            else:
                hs = _norm_mod_fwd(xc, g, sc, sh, "norm_mod_fwd_mixer", dils=perm_dils)
                h = hs[0]
                h_res = [h] + [_flat(t) for t in hs[1:]]
                cg = _matmul(h, W[l]["cg_t"], "nt", BF16, "mixer_cg")
                qkvs, os_, lses = [], [], []
                for gi, dil in enumerate(DILS):
                    qkv = _matmul(h_res[gi], W[l]["qkv_t"][gi], "nt", BF16, "mixer_qkv", tn=3 * ATTN_OUT)
                    o_g, lse_g = _attn_fwd(qkv, bias[gi], dil, f"attn_fwd_g{gi}")
                    qkvs.append(qkv)
                    os_.append(o_g)
                    lses.append(lse_g)
                comb = _combine_fwd(os_, lses, "combine_fwd")
                o_f, o_b, lse = comb[0:3]
                lse_res = [lse] + [_flat(t) for t in comb[3:]]
                yc_in = _conv_fwd(cg, conv_w_full[l], "conv_fwd")
                yc = _matmul(yc_in, W[l]["co"], "nn", BF16, "conv_out")
                ya = _matmul(o_b, W[l]["ao_t"], "nt", BF16, "attn_out")
                merged = _merge_fwd(cg, yc, ya, "merge_fwd")
                xc, f = _matmul(merged, W[l]["wo"], "nn", BF16, "mixer_out", resid=(xc, gt, 1.0))
                rec.update(h=h, h_res=h_res, qkvs=qkvs, cg=cg, o_f=o_f, o_b=o_b, lse_res=lse_res, yc_in=yc_in,
                           yc=yc, ya=ya, merged=merged, f=f)
                if gather is not None:
                    fwd = _gather_forward(gather[0], gather[1], gather[2], piece_rows, xc,
                                          f"weights_gather_forward_l{l + 1}")
                    pin = fwd[3]
            sv[sub] = rec
        if gather is not None:
            full = _gather_finish(fwd[0], fwd[1], fwd[2], xc, f"weights_gather_finish_l{l + 1}")
            W[l + 1] = dict(g_t=full[0:2], u_t=full[2:4], down=full[4:6], **mixer_weights(full))
        saved.append(sv)

    def gate_of(l, sub):
        return saved[l][sub]["f"], mod[l, sub, 2], (1.0 if sub == 1 else 0.5)

    def below(l, sub):
        if sub > 0:
            return gate_of(l, sub - 1)
        return gate_of(l - 1, 2) if l > 0 else None

    dx, head, df = _loss_head(xc, final_g.reshape(1, D), target, gate_of(depth - 1, 2), "loss_head")
    d_final_g = head[0]
    loss_part = head[2, 0]
    d_gate = head[4]

    d_mod = [[None] * 3 for _ in range(depth)]
    d_norm = [[None] * 3 for _ in range(depth)]
    d_conv = [None] * depth
    dlog = jnp.zeros((N_GROUPS, HEADS_PER_GROUP, BLOCK, 2 * BLOCK), F32)
    n_pieces = len(piece_rows)
    LATE = (0, 2, 4)
    EARLY = tuple(p for p in range(n_pieces) if p not in LATE)
    g_piece = [[None] * n_pieces for _ in range(depth)]

    piece_keys = (("g_t", 0), ("g_t", 1), ("u_t", 0), ("u_t", 1), ("down", 0), ("down", 1), "in_t", "co", "wo", "ao_t")

    def pieces_of(dW, idx):
        return [dW[piece_keys[p]].reshape(N_DEV * piece_rows[p], D) for p in idx]

    def finish_scatter(sc, idx, after_arr, layer, part=""):
        recv = _scatter_finish(sc[0], sc[1], sc[2], after_arr, f"grads_scatter_finish_l{layer}{part}")
        tot = _sum_sources(recv, "grads_sum")
        o = 0
        for p in idx:
            g_piece[layer][p] = tot[o:o + piece_rows[p]]
            o += piece_rows[p]
        return recv

    scatter = None
    early = None
    for l in reversed(range(depth)):
        dW = {}
        for sub in (2, 1, 0):
            rec = saved[l][sub]
            g, sc = norm_g_full[l, sub], mod[l, sub, 1]
            if sub == 2 and scatter is not None:
                df = behind(df, scatter[3])
            if sub == 0 and early is not None:
                df = behind(df, early[3])
            nxt = below(l, sub)
            if sub != 1:
                i = 0 if sub == 0 else 1
                dz = _matmul(df, W[l]["down"][i], "nt", BF16, "ffn_down_dx")
                dW["down", i] = _matmul(rec["z"], df, "tn", BF16, "ffn_down_dw")
                da, du, dh = _ffn_up_bwd(dz, rec["a"], rec["u"], W[l]["g_t"][i], W[l]["u_t"][i], "ffn_up_bwd")
                dW["g_t", i] = _matmul(da, rec["h"], "tn", BF16, "ffn_gate_dw")
                dW["u_t", i] = _matmul(du, rec["h"], "tn", BF16, "ffn_up_dw")
                res = _norm_mod_bwd(rec["x"], [dh], [], dx, g, sc, "norm_mod_bwd", nxt=nxt)
            else:
                dout = df
                dm = _matmul(dout, W[l]["wo"], "nt", BF16, "mixer_out_dx")
                dW["wo"] = _matmul(rec["merged"], dout, "tn", BF16, "mixer_out_dw")
                dyc, dya, dgg = _merge_bwd(rec["cg"], rec["yc"], rec["ya"], dm, "merge_bwd")
                dyc_in = _matmul(dyc, W[l]["co"], "nt", BF16, "conv_out_dx")
                dW["co"] = _matmul(rec["yc_in"], dyc, "tn", BF16, "conv_out_dw")
                do = _matmul(dya, W[l]["ao_t"], "nn", F32, "attn_out_dx")
                dW["ao_t"] = _matmul(dya, rec["o_b"], "tn", BF16, "attn_out_dw")
                dl = _delta(do, rec["o_f"], "attn_delta")
                do_res = [dl[0]] + [_flat(t) for t in dl[2::2]]
                del_res = [dl[1]] + [_flat(t) for t in dl[3::2]]
                dh_attn, dw_q, dw_kv, dlog_l = [], [], [], []
                for gi, dil in enumerate(DILS):
                    dq, dkv, dlg = _attn_bwd(rec["qkvs"][gi], do_res[gi], rec["lse_res"][gi], del_res[gi],
                                             bias[gi], dil, f"attn_bwd_g{gi}")
                    dlog_l.append(dlg)
                    dh_attn.append(_attn_dh(dq, dkv, W[l]["qkv_t"][gi], "attn_dh"))
                    dw_q.append(_matmul(dq, rec["h_res"][gi], "tn", BF16, "mixer_q_dw"))
                    dw_kv.append(_matmul(dkv, rec["h_res"][gi], "tn", BF16, "mixer_kv_dw"))
                dlog = dlog + jnp.stack(dlog_l)
                dcg, conv_sum = _conv_bwd(rec["cg"], dyc_in, dgg, conv_w_full[l], "conv_bwd")
                d_conv[l] = conv_sum[0:CONV_K]
                dh_cg = _matmul(dcg, W[l]["cg_t"], "nn", F32, "mixer_cg_dx")
                dw_cg = _matmul(dcg, rec["h"], "tn", BF16, "mixer_cg_dw")
                dW["in_t"] = jnp.concatenate(
                    dw_q + [t[:ATTN_OUT] for t in dw_kv] + [t[ATTN_OUT:] for t in dw_kv] + [dw_cg], axis=0)
                perm_parts = [(dil, _by_residue(dh_attn[gi], dil)) for gi, dil in enumerate(DILS) if dil > 1]
                res = _norm_mod_bwd(rec["x"], [dh_cg, dh_attn[0]], perm_parts, dx, g, sc, "norm_mod_bwd_mixer", nxt=nxt)
            dx, sums = res[0], res[1]
            d_mod[l][sub] = jnp.stack([sums[0], sums[2], d_gate])
            d_norm[l][sub] = sums[3]
            if nxt is not None:
                df, d_gate = res[2], sums[4]
            if l == 0 and sub == 1:
                after = dx
                if scatter is not None:
                    after = finish_scatter(scatter, range(n_pieces), dx, l + 1)
                    scatter = None
                early = _scatter_start(pieces_of(dW, EARLY), after, "grads_scatter_start_l0_early")
        if l > 0:
            after = dx
            if scatter is not None:
                after = finish_scatter(scatter, range(n_pieces), dx, l + 1)
            scatter = _scatter_start(pieces_of(dW, range(n_pieces)), after, f"grads_scatter_start_l{l}")
        else:
            late = _scatter_start(pieces_of(dW, LATE), dx, "grads_scatter_start_l0_late")
    grad_x = dx.reshape(1, S, D)
    d_rel = _bias_bwd(dlog, buckets, "bias_bwd")[:, :rel_bias.shape[1]]

    big = dict(
        ffn_w_gate=(ffn_w_gate, m_ffn_w_gate, v_ffn_w_gate, (0, 1), "t_shard"),
        ffn_w_up=(ffn_w_up, m_ffn_w_up, v_ffn_w_up, (2, 3), "t_shard"),
        ffn_w_down=(ffn_w_down, m_ffn_w_down, v_ffn_w_down, (4, 5), "rows"),
        w_in=(w_in, m_w_in, v_w_in, (6,), "t_shard"),
        w_conv_out=(w_conv_out, m_w_conv_out, v_w_conv_out, (7,), "rows"),
        w_o=(w_o, m_w_o, v_w_o, (8,), "rows"),
        w_attn_out=(w_attn_out, m_w_attn_out, v_w_attn_out, (9,), "t_grad"))
    big_out = {name: None for name in big}

    def adam_layer(l, token=None):
        for name, (w_, m_, v_, plist, how) in big.items():
            if how == "t_shard":
                w_, m_, v_ = (jnp.swapaxes(t, -1, -2) for t in (w_, m_, v_))
            ns, r, cc = depth * len(plist), w_.shape[-2], w_.shape[-1]
            w3, m3, v3 = (t.reshape(ns, r, cc) for t in (w_, m_, v_))
            for j, p in enumerate(plist):
                gl = g_piece[l][p]
                if how == "t_grad":
                    gl = jnp.transpose(gl.reshape(cc, r))
                if token is not None:
                    gl = behind(gl, token)
                big_out[name] = _adamw_slab(w3, gl, m3, v3, l * len(plist) + j, big_out[name], "adamw_" + name)

    for l in range(depth - 1, 0, -1):
        adam_layer(l, late[3])
    done = [late[3]] + [st[1] for st in big_out.values() if st is not None]
    finish_scatter(early, EARLY, done, 0, "_early")
    finish_scatter(late, LATE, dx, 0, "_late")
    adam_layer(0)

    d_mod_flat = jnp.stack([jnp.stack(d_mod[l]) for l in range(depth)]).reshape(-1)
    d_norm_flat = jnp.stack([jnp.stack(d_norm[l]) for l in range(depth)]).reshape(-1)
    d_conv_flat = jnp.stack(d_conv).reshape(-1)
    vec = jnp.concatenate([d_mod_flat, d_norm_flat, d_conv_flat, d_rel.reshape(-1), d_final_g,
                           jnp.broadcast_to(loss_part, (128,))])
    pad = (-vec.size) % 1024
    vec = jnp.concatenate([vec, jnp.zeros((pad,), F32)]).reshape(-1, 128)
    parts = _all_gather(vec, "small_grads_all_gather").reshape(N_DEV, vec.shape[0], 128)
    tot = _sum_rows8(parts, "small_grads_sum").reshape(-1)
    o0 = 0
    g_ada_b = tot[o0:o0 + d_mod_flat.size].reshape(ada_b.shape)
    o0 += d_mod_flat.size
    g_norm_full = tot[o0:o0 + d_norm_flat.size].reshape(depth, 3, D)
    o0 += d_norm_flat.size
    g_conv_full = tot[o0:o0 + d_conv_flat.size].reshape(depth, CONV_K, D)
    o0 += d_conv_flat.size
    g_rel = tot[o0:o0 + rel_bias.size].reshape(rel_bias.shape)
    o0 += rel_bias.size
    g_final = tot[o0:o0 + D]
    o0 += D
    loss = tot[o0]
    g_norm = lax.dynamic_slice_in_dim(g_norm_full, me * dsh, dsh, axis=2)
    g_conv = lax.dynamic_slice_in_dim(g_conv_full, me * dsh, dsh, axis=2)

    dm_all = parts.reshape(N_DEV, -1)[:, :d_mod_flat.size].reshape(N_DEV, depth, N_DEV * n_ada)
    dm_cols = lax.dynamic_slice_in_dim(dm_all, me * n_ada, n_ada, axis=2)
    dm16 = jnp.concatenate([jnp.transpose(dm_cols, (1, 0, 2)), jnp.zeros((depth, N_DEV, n_ada), F32)], axis=1)
    g_ada_w = _ada_bwd(cs16, dm16, "ada_bwd")

    small = dict(ada_w=(ada_w, g_ada_w, m_ada_w, v_ada_w), ada_b=(ada_b, g_ada_b, m_ada_b, v_ada_b),
                 norm_g=(norm_g, g_norm, m_norm_g, v_norm_g), conv_w=(conv_w, g_conv, m_conv_w, v_conv_w),
                 rel_bias=(rel_bias, g_rel, m_rel_bias, v_rel_bias), final_g=(final_g, g_final, m_final_g, v_final_g))
    order = ("ada_w", "ada_b", "norm_g", "ffn_w_gate", "ffn_w_up", "ffn_w_down", "w_in", "conv_w", "w_conv_out",
             "w_attn_out", "w_o", "rel_bias", "final_g")
    res = {}
    for name in order:
        if name in big:
            shape = big[name][0].shape
            if big[name][4] == "t_shard":
                t_shape = shape[:-2] + (shape[-1], shape[-2])
                res[name] = tuple(jnp.swapaxes(t.reshape(t_shape), -1, -2) for t in big_out[name])
            else:
                res[name] = tuple(t.reshape(shape) for t in big_out[name])
        else:
            w_, g_, m_, v_ = small[name]
            res[name] = (g_,) + _adamw(w_, g_, m_, v_, "adamw_" + name)
    return (loss, grad_x, *[res[n][0] for n in order], *[res[n][1] for n in order],
            *[res[n][2] for n in order], *[res[n][3] for n in order])
```

```python
import functools

import numpy as np
import jax
import jax.numpy as jnp
from jax import lax
from jax.experimental import pallas as pl
from jax.experimental.pallas import tpu as pltpu

F32 = jnp.float32
BF16 = jnp.bfloat16

N_DEV = 8
HEAD_DIM = 64
HEAD_SHIFT = 6
HEADS_PER_GROUP = 8
DILATION_GROUPS = ((128, 1), (512, 4), (2048, 16))
DILS = tuple(d for _, d in DILATION_GROUPS)
N_GROUPS = len(DILATION_GROUPS)
ATTN_OUT = HEADS_PER_GROUP * HEAD_DIM
QKV_W = N_GROUPS * ATTN_OUT
BLOCK = 128
NUM_BUCKETS = 32
MAX_DISTANCE = 2048
CONV_K = 3
EPS = 1e-6
NEG_INF = -1e30
SCALE = HEAD_DIM ** -0.5

ADAM_LR = 0.001
ADAM_B1 = 0.9
ADAM_B2 = 0.999
ADAM_EPS = 1e-08
ADAM_WD = 0.01
ADAM_STEP = 10

V7X_VMEM_LIMIT = 48 * 1024 * 1024
MESH = pl.DeviceIdType.MESH

NN = (((1,), (0,)), ((), ()))
NT = (((1,), (1,)), ((), ()))
TN = (((0,), (0,)), ((), ()))


def _pick(dim, cands):
    for c in cands:
        if dim % c == 0:
            return c
    return dim


def _pick_k(K, cap=2816):
    if K <= cap or K % 128:
        return K
    best = 128
    for m in range(1, K // 128 + 1):
        if (K // 128) % m == 0 and 128 * m <= cap:
            best = 128 * m
    return best


def _params(sem):
    return pltpu.CompilerParams(dimension_semantics=sem, vmem_limit_bytes=V7X_VMEM_LIMIT)


def _all_gather(x_shard, name):
    m_per, n = x_shard.shape

    def body(x_ref, out_ref, send_sems, recv_sems, local_sem):
        x, y, c = lax.axis_index("x"), lax.axis_index("y"), lax.axis_index("c")
        me, sibling = (x, y, c), (x, y, 1 - c)
        chips = [(1 - x, y), (x, 1 - y), (1 - x, 1 - y)]

        def rows(px, py, pc):
            return out_ref.at[pl.ds((4 * px + 2 * py + pc) * m_per, m_per), :]

        def copy(k, block, to, src=None):
            return pltpu.make_async_remote_copy(
                src_ref=rows(*block) if src is None else src, dst_ref=rows(*block),
                send_sem=send_sems.at[k], recv_sem=recv_sems.at[k], device_id=to, device_id_type=MESH)

        mine = pltpu.make_async_copy(x_ref, rows(*me), local_sem)
        mine.start()
        first = [copy(0, me, sibling, src=x_ref)]
        first += [copy(1 + j, me, (*chip, c), src=x_ref) for j, chip in enumerate(chips)]
        for cp in first:
            cp.start()
        passed = [copy(4 + j, (*chip, c), sibling) for j, chip in enumerate(chips)]
        for j, chip in enumerate(chips):
            copy(1 + j, (*chip, c), me).wait_recv()
            passed[j].start()
        copy(0, sibling, me).wait_recv()
        for j, chip in enumerate(chips):
            copy(4 + j, (*chip, 1 - c), me).wait_recv()
        for cp in first + passed:
            cp.wait_send()
        mine.wait()

    return pl.pallas_call(
        body, name=name,
        out_shape=jax.ShapeDtypeStruct((N_DEV * m_per, n), x_shard.dtype),
        in_specs=[pl.BlockSpec(memory_space=pltpu.VMEM)],
        out_specs=pl.BlockSpec(memory_space=pltpu.VMEM),
        scratch_shapes=[pltpu.SemaphoreType.DMA((7,)), pltpu.SemaphoreType.DMA((7,)), pltpu.SemaphoreType.DMA],
    )(x_shard)


def _offsets(piece_rows):
    offs, o = [], 0
    for n in piece_rows:
        offs.append(o)
        o += n
    return offs


HBM_SPEC = pl.BlockSpec(memory_space=pltpu.HBM)
SEM_SPEC = pl.BlockSpec(memory_space=pltpu.SEMAPHORE)
ANY_SPEC = pl.BlockSpec(memory_space=pl.ANY)
SPLIT_COPY_PARAMS = pltpu.CompilerParams(has_side_effects=pltpu.SideEffectType.DATAFLOW_SIDE_EFFECTING)


def _in_hbm(a):
    return pltpu.with_memory_space_constraint(a, pltpu.HBM)


def _dma_sems(n):
    return [pltpu.SemaphoreType.DMA(())] * n


def _whole(ref, send_sem, recv_sem, me):
    return pltpu.make_async_remote_copy(src_ref=ref, dst_ref=ref, send_sem=send_sem, recv_sem=recv_sem,
                                        device_id=me, device_id_type=MESH)


def _gather_start(packed, piece_rows, after, name):
    R, w = packed.shape
    offs = _offsets(piece_rows)
    P = len(piece_rows)
    assert offs[-1] + piece_rows[-1] == R

    def body(*refs):
        src_ref = refs[0]
        o = refs[P + 2:]
        send, recv = o[0:4], o[4:8]
        zones, token, stage, local_sems = o[9:9 + P], o[9 + P], o[10 + P], o[11 + P]
        x, y, c = lax.axis_index("x"), lax.axis_index("y"), lax.axis_index("c")
        targets = [(x, y, 1 - c), (1 - x, y, c), (x, 1 - y, c), (1 - x, 1 - y, c)]
        me = 4 * x + 2 * y + c

        def piece(p, ref):
            return ref.at[pl.ds(offs[p], piece_rows[p]), :]

        def rows(p):
            return zones[p].at[pl.ds(me * piece_rows[p], piece_rows[p]), :]

        for k, to in enumerate(targets):
            for p in range(P):
                pltpu.make_async_remote_copy(src_ref=piece(p, src_ref), dst_ref=rows(p), send_sem=send[k],
                                             recv_sem=recv[k], device_id=to, device_id_type=MESH).start()
        load = pltpu.make_async_copy(src_ref, stage, local_sems.at[P])
        load.start()
        load.wait()
        mine = [pltpu.make_async_copy(piece(p, stage), rows(p), local_sems.at[p]) for p in range(P)]
        for cp in mine:
            cp.start()
        for cp in mine:
            cp.wait()
        token[...] = jnp.zeros_like(token)

    zones_in = [_in_hbm(lax.empty((N_DEV * n, w), packed.dtype)) for n in piece_rows]
    outs = pl.pallas_call(
        body, name=name,
        out_shape=(*_dma_sems(8), pltpu.HBM((R, w), packed.dtype),
                   *[pltpu.HBM((N_DEV * n, w), packed.dtype) for n in piece_rows],
                   jax.ShapeDtypeStruct((8, 128), F32)),
        in_specs=[HBM_SPEC] * (P + 1) + [ANY_SPEC],
        out_specs=[SEM_SPEC] * 8 + [HBM_SPEC] * (P + 1) + [pl.BlockSpec(memory_space=pltpu.VMEM)],
        input_output_aliases={0: 8, **{1 + p: 9 + p for p in range(P)}},
        scratch_shapes=[pltpu.VMEM((R, w), packed.dtype), pltpu.SemaphoreType.DMA((P + 1,))],
        compiler_params=SPLIT_COPY_PARAMS,
    )(_in_hbm(packed), *zones_in, after)
    return outs[0:8], outs[8], list(outs[9:9 + P]), outs[9 + P]


def _gather_forward(sems, packed, zones, piece_rows, after, name):
    P = len(piece_rows)

    def body(*refs):
        src_ref = refs[0]
        s = refs[1 + P:9 + P]
        o = refs[10 + P:]
        send, recv = s[0:4], s[4:8]
        send2, recv2, zones_o = o[0:3], o[3:6], o[7:7 + P]
        x, y, c = lax.axis_index("x"), lax.axis_index("y"), lax.axis_index("c")
        me = (x, y, c)
        chips = [(1 - x, y), (x, 1 - y), (1 - x, 1 - y)]
        for j, (px, py) in enumerate(chips):
            _whole(src_ref, send[1 + j], recv[1 + j], me).wait_recv()
            blk = 4 * px + 2 * py + c
            for p in range(P):
                r = zones_o[p].at[pl.ds(blk * piece_rows[p], piece_rows[p]), :]
                pltpu.make_async_remote_copy(src_ref=r, dst_ref=r, send_sem=send2[j], recv_sem=recv2[j],
                                             device_id=(x, y, 1 - c), device_id_type=MESH).start()
        _whole(src_ref, send[0], recv[0], me).wait_recv()
        for k in range(4):
            _whole(src_ref, send[k], recv[k], me).wait_send()
        o[7 + P][...] = jnp.zeros_like(o[7 + P])

    outs = pl.pallas_call(
        body, name=name,
        out_shape=(*_dma_sems(6), pltpu.HBM(packed.shape, packed.dtype),
                   *[pltpu.HBM(z.shape, z.dtype) for z in zones], jax.ShapeDtypeStruct((8, 128), F32)),
        in_specs=[HBM_SPEC] * (P + 1) + [SEM_SPEC] * 8 + [ANY_SPEC],
        out_specs=[SEM_SPEC] * 6 + [HBM_SPEC] * (P + 1) + [pl.BlockSpec(memory_space=pltpu.VMEM)],
        input_output_aliases={0: 6, **{1 + p: 7 + p for p in range(P)}},
        compiler_params=SPLIT_COPY_PARAMS,
    )(packed, *zones, *sems, after)
    return outs[0:6], outs[6], list(outs[7:7 + P]), outs[7 + P]


def _gather_finish(sems2, packed, zones, after, name):
    P = len(zones)

    def body(*refs):
        src_ref = refs[0]
        s = refs[1 + P:7 + P]
        x, y, c = lax.axis_index("x"), lax.axis_index("y"), lax.axis_index("c")
        for j in range(3):
            _whole(src_ref, s[j], s[3 + j], (x, y, c)).wait_recv()
        for j in range(3):
            _whole(src_ref, s[j], s[3 + j], (x, y, c)).wait_send()

    outs = pl.pallas_call(
        body, name=name,
        out_shape=(pltpu.HBM(packed.shape, packed.dtype), *[pltpu.HBM(z.shape, z.dtype) for z in zones]),
        in_specs=[HBM_SPEC] * (P + 1) + [SEM_SPEC] * 6 + [ANY_SPEC],
        out_specs=[HBM_SPEC] * (P + 1),
        input_output_aliases={p: p for p in range(P + 1)},
        compiler_params=SPLIT_COPY_PARAMS,
    )(packed, *zones, *sems2, after)
    return list(outs[1:1 + P])


def _scatter_start(pieces, after, name):
    P = len(pieces)
    w = pieces[0].shape[1]
    piece_rows = [p.shape[0] // N_DEV for p in pieces]
    offs = _offsets(piece_rows)
    R = offs[-1] + piece_rows[-1]

    def body(*refs):
        o = refs[P + 2:]
        send, recv = o[0:7], o[7:14]
        srcs, dst_ref, token, stage, local_sems = o[14:14 + P], o[14 + P], o[15 + P], o[16 + P], o[17 + P]
        x, y, c = lax.axis_index("x"), lax.axis_index("y"), lax.axis_index("c")
        me = 4 * x + 2 * y + c

        def chunk(p, dev):
            return srcs[p].at[pl.ds(dev * piece_rows[p], piece_rows[p]), :]

        def slot(p, dev):
            return dst_ref.at[dev, pl.ds(offs[p], piece_rows[p]), :]

        for k in range(1, N_DEV):
            px = 1 - x if (k >> 2) & 1 else x
            py = 1 - y if (k >> 1) & 1 else y
            pc = 1 - c if k & 1 else c
            peer = 4 * px + 2 * py + pc
            for p in range(P):
                pltpu.make_async_remote_copy(
                    src_ref=chunk(p, peer), dst_ref=slot(p, me), send_sem=send[k - 1], recv_sem=recv[k - 1],
                    device_id=(px, py, pc), device_id_type=MESH).start()
        mine = [pltpu.make_async_copy(chunk(p, me), stage.at[pl.ds(offs[p], piece_rows[p]), :], local_sems.at[p])
                for p in range(P)]
        for cp in mine:
            cp.start()
        for cp in mine:
            cp.wait()
        store = pltpu.make_async_copy(stage, dst_ref.at[me], local_sems.at[P])
        store.start()
        store.wait()
        token[...] = jnp.zeros_like(token)

    dtype = pieces[0].dtype
    outs = pl.pallas_call(
        body, name=name,
        out_shape=(*_dma_sems(14), *[pltpu.HBM(p.shape, dtype) for p in pieces], pltpu.HBM((N_DEV, R, w), dtype),
                   jax.ShapeDtypeStruct((8, 128), F32)),
        in_specs=[HBM_SPEC] * (P + 1) + [ANY_SPEC],
        out_specs=[SEM_SPEC] * 14 + [HBM_SPEC] * (P + 1) + [pl.BlockSpec(memory_space=pltpu.VMEM)],
        input_output_aliases={p: 14 + p for p in range(P + 1)},
        scratch_shapes=[pltpu.VMEM((R, w), dtype), pltpu.SemaphoreType.DMA((P + 1,))],
        compiler_params=SPLIT_COPY_PARAMS,
    )(*[_in_hbm(p) for p in pieces], _in_hbm(lax.empty((N_DEV, R, w), dtype)), after)
    return outs[0:14], list(outs[14:14 + P]), outs[14 + P], outs[15 + P]


def _scatter_finish(sems, pieces, recv, after, name):
    P = len(pieces)
    after = list(after) if isinstance(after, (list, tuple)) else [after]

    def body(*refs):
        dst_ref = refs[P]
        s = refs[P + 1:P + 15]
        x, y, c = lax.axis_index("x"), lax.axis_index("y"), lax.axis_index("c")
        for k in range(7):
            _whole(dst_ref.at[0], s[k], s[7 + k], (x, y, c)).wait_recv()
        for k in range(7):
            _whole(dst_ref.at[0], s[k], s[7 + k], (x, y, c)).wait_send()

    outs = pl.pallas_call(
        body, name=name,
        out_shape=(*[pltpu.HBM(p.shape, p.dtype) for p in pieces], pltpu.HBM(recv.shape, recv.dtype)),
        in_specs=[HBM_SPEC] * (P + 1) + [SEM_SPEC] * 14 + [ANY_SPEC] * len(after),
        out_specs=[HBM_SPEC] * (P + 1),
        input_output_aliases={p: p for p in range(P + 1)},
        compiler_params=SPLIT_COPY_PARAMS,
    )(*pieces, recv, *sems, *after)
    return outs[P]


def _sum_sources(parts, name):
    _, r, n = parts.shape
    tr = _pick(r, [256, 192, 128, 96, 64, 32, 16, 8])

    def kern(p_ref, o_ref):
        acc = p_ref[0].astype(F32)
        for k in range(1, N_DEV):
            acc = acc + p_ref[k].astype(F32)
        o_ref[...] = acc

    return pl.pallas_call(
        kern, name=name, grid=(r // tr,),
        out_shape=jax.ShapeDtypeStruct((r, n), F32),
        in_specs=[pl.BlockSpec((N_DEV, tr, n), lambda i: (0, i, 0))],
        out_specs=pl.BlockSpec((tr, n), lambda i: (i, 0)),
        compiler_params=_params(("parallel",)),
    )(parts)


def _matmul(a, b, mode, out_dtype, name, tm=None, tn=None, tk=None, resid=None):
    if mode == "nn":
        (M, K), N = a.shape, b.shape[1]
    elif mode == "nt":
        (M, K), N = a.shape, b.shape[0]
    else:
        (K, M), N = a.shape, b.shape[1]
    dims = {"nn": NN, "nt": NT, "tn": TN}[mode]
    tm = tm or _pick(M, [1024, 1408, 512, 256, 128])
    tn = tn or _pick(N, [1024, 1408, 512, 256, 128])
    tk = tk or _pick_k(K)
    nk = K // tk
    a_spec = {"nn": pl.BlockSpec((tm, tk), lambda i, j, k: (i, k)),
              "nt": pl.BlockSpec((tm, tk), lambda i, j, k: (i, k)),
              "tn": pl.BlockSpec((tk, tm), lambda i, j, k: (k, i))}[mode]
    b_spec = {"nn": pl.BlockSpec((tk, tn), lambda i, j, k: (k, j)),
              "nt": pl.BlockSpec((tn, tk), lambda i, j, k: (j, k)),
              "tn": pl.BlockSpec((tk, tn), lambda i, j, k: (k, j))}[mode]
    o_spec = pl.BlockSpec((tm, tn), lambda i, j, k: (i, j))
    n_in = 2 if resid is None else 4
    n_out = 1 if resid is None else 2

    def kern(*refs):
        a_ref, b_ref = refs[0], refs[1]
        outs = refs[n_in:n_in + n_out]
        acc_ref = refs[n_in + n_out] if nk > 1 else None

        def finish(acc):
            if resid is None:
                outs[0][...] = acc.astype(out_dtype)
            else:
                x_ref, g_ref = refs[2], refs[3]
                outs[0][...] = x_ref[...] + (resid[2] * g_ref[...]) * acc
                outs[1][...] = acc.astype(out_dtype)

        part = lax.dot_general(a_ref[...], b_ref[...], dims, preferred_element_type=F32)
        if nk == 1:
            finish(part)
        else:
            k = pl.program_id(2)

            @pl.when(k == 0)
            def _():
                acc_ref[...] = part

            @pl.when(k > 0)
            def _():
                acc_ref[...] += part

            @pl.when(k == nk - 1)
            def _():
                finish(acc_ref[...])

    in_specs = [a_spec, b_spec]
    args = [a, b]
    out_shape = [jax.ShapeDtypeStruct((M, N), out_dtype)]
    out_specs = [o_spec]
    if resid is not None:
        in_specs += [o_spec, pl.BlockSpec((1, tn), lambda i, j, k: (0, j))]
        args += [resid[0], resid[1]]
        out_shape = [jax.ShapeDtypeStruct((M, N), F32)] + out_shape
        out_specs = [o_spec, o_spec]
    res = pl.pallas_call(
        kern, name=name, grid=(M // tm, N // tn, nk),
        out_shape=out_shape, in_specs=in_specs, out_specs=out_specs,
        scratch_shapes=[pltpu.VMEM((tm, tn), F32)] if nk > 1 else [],
        compiler_params=_params(("parallel", "parallel", "arbitrary")),
    )(*args)
    return res[0] if resid is None else res


def _dot3(a, b, dims):
    ah = a.astype(BF16)
    al = (a - ah.astype(F32)).astype(BF16)
    bh = b.astype(BF16)
    bl = (b - bh.astype(F32)).astype(BF16)
    d = functools.partial(lax.dot_general, dimension_numbers=dims, preferred_element_type=F32)
    return d(ah, bh) + (d(ah, bl) + d(al, bh))


def _silu_parts(a):
    sg = jax.nn.sigmoid(a)
    return a * sg, sg * (1.0 + a * (1.0 - sg))


def _ffn_up(h, wg_t, wu_t, name):
    S, D = h.shape
    F = wg_t.shape[0]
    tm = _pick(S, [512, 256, 128])
    tn = _pick(F, [1408, 512, 256, 128])

    def kern(h_ref, g_ref, u_ref, a_out, u_out, z_out):
        hv = h_ref[...]
        a = lax.dot_general(hv, g_ref[...], NT, preferred_element_type=F32)
        u = lax.dot_general(hv, u_ref[...], NT, preferred_element_type=F32)
        a_out[...] = a.astype(BF16)
        u_out[...] = u.astype(BF16)
        z_out[...] = (_silu_parts(a)[0] * u).astype(BF16)

    w_spec = pl.BlockSpec((tn, D), lambda j, i: (j, 0))
    o_spec = pl.BlockSpec((tm, tn), lambda j, i: (i, j))
    return pl.pallas_call(
        kern, name=name, grid=(F // tn, S // tm),
        out_shape=[jax.ShapeDtypeStruct((S, F), BF16)] * 3,
        in_specs=[pl.BlockSpec((tm, D), lambda j, i: (i, 0)), w_spec, w_spec],
        out_specs=[o_spec] * 3,
        compiler_params=_params(("parallel", "parallel")),
    )(h, wg_t, wu_t)


def _ffn_up_bwd(dz, a, u, wg_t, wu_t, name):
    S, F = dz.shape
    D = wg_t.shape[1]
    tm = _pick(S, [512, 256, 128])
    tk = _pick(F, [1408, 512, 256, 128])
    nk = F // tk

    def kern(dz_ref, a_ref, u_ref, g_ref, w_ref, da_out, du_out, dh_out, acc_ref):
        k = pl.program_id(1)
        av = a_ref[...].astype(F32)
        uv = u_ref[...].astype(F32)
        dzv = dz_ref[...].astype(F32)
        silu, dsilu = _silu_parts(av)
        da = (dzv * uv * dsilu).astype(BF16)
        du = (dzv * silu).astype(BF16)
        da_out[...] = da
        du_out[...] = du
        part = (lax.dot_general(da, g_ref[...], NN, preferred_element_type=F32)
                + lax.dot_general(du, w_ref[...], NN, preferred_element_type=F32))

        @pl.when(k == 0)
        def _():
            acc_ref[...] = part

        @pl.when(k > 0)
        def _():
            acc_ref[...] += part

        @pl.when(k == nk - 1)
        def _():
            dh_out[...] = acc_ref[...]

    t_spec = pl.BlockSpec((tm, tk), lambda i, k: (i, k))
    w_spec = pl.BlockSpec((tk, D), lambda i, k: (k, 0))
    return pl.pallas_call(
        kern, name=name, grid=(S // tm, nk),
        out_shape=[jax.ShapeDtypeStruct((S, F), BF16)] * 2 + [jax.ShapeDtypeStruct((S, D), F32)],
        in_specs=[t_spec, t_spec, t_spec, w_spec, w_spec],
        out_specs=[t_spec, t_spec, pl.BlockSpec((tm, D), lambda i, k: (i, 0))],
        scratch_shapes=[pltpu.VMEM((tm, D), F32)],
        compiler_params=_params(("parallel", "arbitrary")),
    )(dz, a, u, wg_t, wu_t)


def _attn_dh(dq, dkv, w_t, name):
    S = dq.shape[0]
    D = w_t.shape[1]
    tm = _pick(S, [1024, 512, 256, 128])

    def kern(dq_ref, dk_ref, dv_ref, wq_ref, wk_ref, wv_ref, o_ref):
        o_ref[...] = (lax.dot_general(dq_ref[...], wq_ref[...], NN, preferred_element_type=F32)
                      + lax.dot_general(dk_ref[...], wk_ref[...], NN, preferred_element_type=F32)
                      + lax.dot_general(dv_ref[...], wv_ref[...], NN, preferred_element_type=F32))

    def w_blk(j):
        return pl.BlockSpec((ATTN_OUT, D), lambda i: (j, 0))

    return pl.pallas_call(
        kern, name=name, grid=(S // tm,),
        out_shape=jax.ShapeDtypeStruct((S, D), F32),
        in_specs=[pl.BlockSpec((tm, ATTN_OUT), lambda i: (i, 0)), pl.BlockSpec((tm, ATTN_OUT), lambda i: (i, 0)),
                  pl.BlockSpec((tm, ATTN_OUT), lambda i: (i, 1)), w_blk(0), w_blk(1), w_blk(2)],
        out_specs=pl.BlockSpec((tm, D), lambda i: (i, 0)),
        compiler_params=_params(("parallel",)),
    )(dq, dkv, dkv, w_t, w_t, w_t)


def _row_spec(tm, d):
    return pl.BlockSpec((tm, d), lambda i: (i, 0))


def _vec_spec(d, rows=1):
    return pl.BlockSpec((rows, d), lambda i: (0, 0))


def _perm_spec(dil, tm, w):
    return pl.BlockSpec((dil, tm // dil, w), lambda i: (0, i, 0))


def _stage_shape(tm, w):
    return pltpu.VMEM((w // 128, tm, 128), F32)


def _stage(scr, val):
    for ci in range(scr.shape[0]):
        scr[ci] = val[:, 128 * ci:128 * (ci + 1)]


def _unstage(scr):
    return jnp.concatenate([scr[ci] for ci in range(scr.shape[0])], axis=1)


def _get_residue(scr, res, dil):
    n = scr.shape[1] // dil
    return jnp.concatenate([scr[ci, pl.ds(res, n, stride=dil), :] for ci in range(scr.shape[0])], axis=1)


def _put_residue(scr, res, dil, val):
    n = scr.shape[1] // dil
    for ci in range(scr.shape[0]):
        scr[ci, pl.ds(res, n, stride=dil), :] = val[:, 128 * ci:128 * (ci + 1)]


def _norm_mod_fwd(x, g, s, b, name, dils=()):
    S, D = x.shape
    tm = _pick(S, [512, 256, 128])

    def kern(x_ref, g_ref, s_ref, b_ref, h_ref, *rest):
        xv = x_ref[...]
        r = lax.rsqrt(jnp.mean(xv * xv, axis=1, keepdims=True) + EPS)
        hv = xv * r * g_ref[...] * (1.0 + s_ref[...]) + b_ref[...]
        h_ref[...] = hv.astype(BF16)
        if dils:
            scr = rest[len(dils)]
            _stage(scr, hv)
            for dil, p_ref in zip(dils, rest[:len(dils)]):
                for res in range(dil):
                    p_ref[res] = _get_residue(scr, res, dil).astype(BF16)

    return pl.pallas_call(
        kern, name=name, grid=(S // tm,),
        out_shape=[jax.ShapeDtypeStruct((S, D), BF16)] + [jax.ShapeDtypeStruct((dil, S // dil, D), BF16) for dil in dils],
        in_specs=[_row_spec(tm, D), _vec_spec(D), _vec_spec(D), _vec_spec(D)],
        out_specs=[_row_spec(tm, D)] + [_perm_spec(dil, tm, D) for dil in dils],
        scratch_shapes=[_stage_shape(tm, D)] if dils else [],
        compiler_params=_params(("parallel",)),
    )(x, g, s, b)


def _gate_part(dx, nxt_refs, coef, df_ref, cs_ref):
    f_ref, gate_ref = nxt_refs
    df_ref[...] = ((coef * gate_ref[...]) * dx).astype(BF16)
    cs_ref[4:5, :] += coef * jnp.sum(f_ref[...].astype(F32) * dx, axis=0, keepdims=True)


def _norm_mod_bwd(x, dh_nat, dh_perm, dxo, g, s, name, nxt=None):
    S, D = x.shape
    tm = _pick(S, [512, 256, 128])
    n = S // tm
    n_nat, n_perm = len(dh_nat), len(dh_perm)
    n_nxt = 0 if nxt is None else 2

    def kern(*refs):
        x_ref = refs[0]
        nat = refs[1:1 + n_nat]
        perm = refs[1 + n_nat:1 + n_nat + n_perm]
        base = 1 + n_nat + n_perm
        dxo_ref, g_ref, s_ref = refs[base:base + 3]
        nxt_refs = refs[base + 3:base + 3 + n_nxt]
        dx_ref, cs_ref = refs[base + 3 + n_nxt:base + 5 + n_nxt]
        rest = refs[base + 5 + n_nxt:]
        df_ref = rest[0] if nxt is not None else None
        scr = rest[1:] if nxt is not None else rest
        i = pl.program_id(0)
        xv = x_ref[...]
        r = lax.rsqrt(jnp.mean(xv * xv, axis=1, keepdims=True) + EPS)
        xn = xv * r
        dh_v = nat[0][...].astype(F32)
        for t in nat[1:]:
            dh_v = dh_v + t[...].astype(F32)
        for (dil, _), p_ref, sc in zip(dh_perm, perm, scr):
            for res in range(dil):
                _put_residue(sc, res, dil, p_ref[res])
            dh_v = dh_v + _unstage(sc)
        one_s = 1.0 + s_ref[...]
        dxn = dh_v * (g_ref[...] * one_s)
        dx = dxo_ref[...] + r * (dxn - xn * jnp.mean(xn * dxn, axis=1, keepdims=True))
        dx_ref[...] = dx

        @pl.when(i == 0)
        def _():
            cs_ref[...] = jnp.zeros_like(cs_ref)

        cs_ref[0:1, :] += jnp.sum(dh_v, axis=0, keepdims=True)
        cs_ref[1:2, :] += jnp.sum(dh_v * xn, axis=0, keepdims=True)
        if nxt is not None:
            _gate_part(dx, nxt_refs, nxt[2], df_ref, cs_ref)

        @pl.when(i == n - 1)
        def _():
            t = cs_ref[1:2, :]
            cs_ref[2:3, :] = g_ref[...] * t
            cs_ref[3:4, :] = one_s * t

    nxt_specs = [] if nxt is None else [_row_spec(tm, D), _vec_spec(D)]
    nxt_args = [] if nxt is None else [nxt[0], nxt[1]]
    return pl.pallas_call(
        kern, name=name, grid=(n,),
        out_shape=[jax.ShapeDtypeStruct((S, D), F32), jax.ShapeDtypeStruct((8, D), F32)]
        + ([] if nxt is None else [jax.ShapeDtypeStruct((S, D), BF16)]),
        in_specs=[_row_spec(tm, D)] + [_row_spec(tm, D)] * n_nat + [_perm_spec(dil, tm, D) for dil, _ in dh_perm]
        + [_row_spec(tm, D), _vec_spec(D), _vec_spec(D)] + nxt_specs,
        out_specs=[_row_spec(tm, D), _vec_spec(D, 8)] + ([] if nxt is None else [_row_spec(tm, D)]),
        scratch_shapes=[_stage_shape(tm, D) for _ in dh_perm],
        compiler_params=_params(("arbitrary",)),
    )(x, *dh_nat, *[a for _, a in dh_perm], dxo, g, s, *nxt_args)


def _loss_head(x, g, target, nxt, name):
    S, D = x.shape
    tm = _pick(S, [512, 256, 128])
    n = S // tm

    def kern(x_ref, g_ref, t_ref, f_ref, gate_ref, dx_ref, cs_ref, df_ref):
        i = pl.program_id(0)
        xv = x_ref[...]
        r = lax.rsqrt(jnp.mean(xv * xv, axis=1, keepdims=True) + EPS)
        xn = xv * r
        e = xn * g_ref[...] - t_ref[...]
        dxn = (e * (1.0 / D)) * g_ref[...]
        dx = r * (dxn - xn * jnp.mean(xn * dxn, axis=1, keepdims=True))
        dx_ref[...] = dx

        @pl.when(i == 0)
        def _():
            cs_ref[...] = jnp.zeros_like(cs_ref)

        cs_ref[0:1, :] += jnp.sum(xn * e, axis=0, keepdims=True) * (1.0 / D)
        cs_ref[1:2, :] += jnp.sum(e * e, axis=0, keepdims=True)
        _gate_part(dx, (f_ref, gate_ref), nxt[2], df_ref, cs_ref)

        @pl.when(i == n - 1)
        def _():
            tot = jnp.sum(cs_ref[1:2, :], axis=1, keepdims=True) * (0.5 / D)
            cs_ref[2:3, :] = jnp.broadcast_to(tot, (1, D))

    return pl.pallas_call(
        kern, name=name, grid=(n,),
        out_shape=[jax.ShapeDtypeStruct((S, D), F32), jax.ShapeDtypeStruct((8, D), F32),
                   jax.ShapeDtypeStruct((S, D), BF16)],
        in_specs=[_row_spec(tm, D), _vec_spec(D), _row_spec(tm, D), _row_spec(tm, D), _vec_spec(D)],
        out_specs=[_row_spec(tm, D), _vec_spec(D, 8), _row_spec(tm, D)],
        compiler_params=_params(("arbitrary",)),
    )(x, g, target, nxt[0], nxt[1])


def _shift_down(p, row, prev_rows):
    a, b = prev_rows
    p1 = jnp.where(row == 0, b, pltpu.roll(p, 1, 0))
    p2 = jnp.where(row == 0, a, jnp.where(row == 1, b, pltpu.roll(p, 2, 0)))
    return p1, p2


def _conv_fwd(cg, conv_w, name):
    S, D5 = cg.shape
    D = D5 // 5
    tm = _pick(S, [512, 256, 128])
    t8 = tm // 8

    def prev(col):
        return pl.BlockSpec((8, D), lambda i: (jnp.maximum(i * t8 - 1, 0), col))

    def kern(cb_ref, cc_ref, ch_ref, ccp_ref, chp_ref, w_ref, y_ref):
        i = pl.program_id(0)
        keep = jnp.where(i > 0, 1.0, 0.0)
        p = cc_ref[...].astype(F32) * ch_ref[...].astype(F32)
        pa = ccp_ref[6:7, :].astype(F32) * chp_ref[6:7, :].astype(F32) * keep
        pb = ccp_ref[7:8, :].astype(F32) * chp_ref[7:8, :].astype(F32) * keep
        row = lax.broadcasted_iota(jnp.int32, (tm, D), 0)
        p1, p2 = _shift_down(p, row, (pa, pb))
        dw = w_ref[0:1, :] * p2 + w_ref[1:2, :] * p1 + w_ref[2:3, :] * p
        y_ref[...] = (cb_ref[...].astype(F32) * dw).astype(BF16)

    def col(cidx):
        return pl.BlockSpec((tm, D), lambda i: (i, cidx))

    return pl.pallas_call(
        kern, name=name, grid=(S // tm,),
        out_shape=jax.ShapeDtypeStruct((S, D), BF16),
        in_specs=[col(0), col(1), col(2), prev(1), prev(2), _vec_spec(D, CONV_K)],
        out_specs=_row_spec(tm, D),
        compiler_params=_params(("parallel",)),
    )(cg, cg, cg, cg, cg, conv_w)


def _conv_bwd(cg, dy, dgg, conv_w, name):
    S, D5 = cg.shape
    D = D5 // 5
    tm = _pick(S, [512, 256, 128])
    t8 = tm // 8
    n = S // tm
    last8 = S // 8 - 1

    def prev(col):
        return pl.BlockSpec((8, D), lambda i: (jnp.maximum(i * t8 - 1, 0), col))

    def nxt(col):
        return pl.BlockSpec((8, D), lambda i: (jnp.minimum((i + 1) * t8, last8), col))

    def kern(cb_ref, cc_ref, ch_ref, dy_ref, dgg_ref, ccp_ref, chp_ref, cbn_ref, dyn_ref, w_ref, d_ref, cs_ref):
        i = pl.program_id(0)
        keep_p = jnp.where(i > 0, 1.0, 0.0)
        keep_n = jnp.where(i < n - 1, 1.0, 0.0)
        cb = cb_ref[...].astype(F32)
        cc = cc_ref[...].astype(F32)
        ch = ch_ref[...].astype(F32)
        dyv = dy_ref[...].astype(F32)
        p = cc * ch
        pa = ccp_ref[6:7, :].astype(F32) * chp_ref[6:7, :].astype(F32) * keep_p
        pb = ccp_ref[7:8, :].astype(F32) * chp_ref[7:8, :].astype(F32) * keep_p
        row = lax.broadcasted_iota(jnp.int32, (tm, D), 0)
        p1, p2 = _shift_down(p, row, (pa, pb))
        w0, w1, w2 = w_ref[0:1, :], w_ref[1:2, :], w_ref[2:3, :]
        dw = w0 * p2 + w1 * p1 + w2 * p
        ddw = dyv * cb
        na = dyn_ref[0:1, :].astype(F32) * cbn_ref[0:1, :].astype(F32) * keep_n
        nb = dyn_ref[1:2, :].astype(F32) * cbn_ref[1:2, :].astype(F32) * keep_n
        u1 = jnp.where(row == tm - 1, na, pltpu.roll(ddw, tm - 1, 0))
        u2 = jnp.where(row == tm - 2, na, jnp.where(row == tm - 1, nb, pltpu.roll(ddw, tm - 2, 0)))
        dp = w2 * ddw + w1 * u1 + w0 * u2
        d_ref[:, 0:D] = (dyv * dw).astype(BF16)
        d_ref[:, D:2 * D] = (dp * ch).astype(BF16)
        d_ref[:, 2 * D:3 * D] = (dp * cc).astype(BF16)
        d_ref[:, 3 * D:5 * D] = dgg_ref[...]

        @pl.when(i == 0)
        def _():
            cs_ref[...] = jnp.zeros_like(cs_ref)

        cs_ref[0:1, :] += jnp.sum(ddw * p2, axis=0, keepdims=True)
        cs_ref[1:2, :] += jnp.sum(ddw * p1, axis=0, keepdims=True)
        cs_ref[2:3, :] += jnp.sum(ddw * p, axis=0, keepdims=True)

    def col(cidx):
        return pl.BlockSpec((tm, D), lambda i: (i, cidx))

    return pl.pallas_call(
        kern, name=name, grid=(n,),
        out_shape=[jax.ShapeDtypeStruct((S, 5 * D), BF16), jax.ShapeDtypeStruct((8, D), F32)],
        in_specs=[col(0), col(1), col(2), _row_spec(tm, D), _row_spec(tm, 2 * D), prev(1), prev(2), nxt(0),
                  pl.BlockSpec((8, D), lambda i: (jnp.minimum((i + 1) * t8, last8), 0)), _vec_spec(D, CONV_K)],
        out_specs=[_row_spec(tm, 5 * D), _vec_spec(D, 8)],
        compiler_params=_params(("arbitrary",)),
    )(cg, cg, cg, dy, dgg, cg, cg, cg, dy, conv_w)


def _merge_fwd(cg, yc, ya, name):
    S, D = yc.shape
    tm = _pick(S, [512, 256, 128])

    def kern(gc_ref, ga_ref, yc_ref, ya_ref, m_ref):
        m_ref[...] = (jax.nn.sigmoid(gc_ref[...].astype(F32)) * yc_ref[...].astype(F32)
                      + jax.nn.sigmoid(ga_ref[...].astype(F32)) * ya_ref[...].astype(F32)).astype(BF16)

    return pl.pallas_call(
        kern, name=name, grid=(S // tm,),
        out_shape=jax.ShapeDtypeStruct((S, D), BF16),
        in_specs=[pl.BlockSpec((tm, D), lambda i: (i, 3)), pl.BlockSpec((tm, D), lambda i: (i, 4)),
                  _row_spec(tm, D), _row_spec(tm, D)],
        out_specs=_row_spec(tm, D),
        compiler_params=_params(("parallel",)),
    )(cg, cg, yc, ya)


def _merge_bwd(cg, yc, ya, dm, name):
    S, D = yc.shape
    tm = _pick(S, [512, 256, 128])

    def kern(gc_ref, ga_ref, yc_ref, ya_ref, dm_ref, dyc_ref, dya_ref, dg_ref):
        dmv = dm_ref[...].astype(F32)
        sc = jax.nn.sigmoid(gc_ref[...].astype(F32))
        sa = jax.nn.sigmoid(ga_ref[...].astype(F32))
        dyc_ref[...] = (dmv * sc).astype(BF16)
        dya_ref[...] = (dmv * sa).astype(BF16)
        dg_ref[:, 0:D] = (dmv * yc_ref[...].astype(F32) * (sc * (1.0 - sc))).astype(BF16)
        dg_ref[:, D:2 * D] = (dmv * ya_ref[...].astype(F32) * (sa * (1.0 - sa))).astype(BF16)

    return pl.pallas_call(
        kern, name=name, grid=(S // tm,),
        out_shape=[jax.ShapeDtypeStruct((S, D), BF16), jax.ShapeDtypeStruct((S, D), BF16),
                   jax.ShapeDtypeStruct((S, 2 * D), BF16)],
        in_specs=[pl.BlockSpec((tm, D), lambda i: (i, 3)), pl.BlockSpec((tm, D), lambda i: (i, 4)),
                  _row_spec(tm, D), _row_spec(tm, D), _row_spec(tm, D)],
        out_specs=[_row_spec(tm, D), _row_spec(tm, D), pl.BlockSpec((tm, 2 * D), lambda i: (i, 0))],
        compiler_params=_params(("parallel",)),
    )(cg, cg, yc, ya, dm)


def _t5_bucket(dist):
    exact = NUM_BUCKETS // 2
    d = np.maximum(dist, 1).astype(np.float32)
    large = exact + (np.log(d / exact) / np.log(MAX_DISTANCE / exact) * (NUM_BUCKETS - exact)).astype(np.int32)
    large = np.minimum(large, NUM_BUCKETS - 1)
    return np.where(dist < exact, dist, large).astype(np.int32)


def _bucket_tables():
    i = np.arange(BLOCK)[:, None]
    j = np.arange(2 * BLOCK)[None, :]
    rel = i - j + BLOCK
    return np.stack([_t5_bucket(np.maximum(rel, 0) * d) for _, d in DILATION_GROUPS]).astype(np.int32)


def _band_masks():
    i = lax.broadcasted_iota(jnp.int32, (BLOCK, 2 * BLOCK), 0)
    j = lax.broadcasted_iota(jnp.int32, (BLOCK, 2 * BLOCK), 1)
    rel = i - j + BLOCK
    band = (rel >= 0) & (rel <= BLOCK)
    return band, band & (j >= BLOCK)


def _bias_build(rel_bias, buckets, name):
    def kern(rb_ref, bk_ref, o_ref):
        g = pl.program_id(0)
        bk = bk_ref[0]
        band, first = _band_masks()
        for h in range(HEADS_PER_GROUP):
            acc = jnp.zeros((BLOCK, 2 * BLOCK), F32)
            for b in range(NUM_BUCKETS):
                acc = jnp.where(bk == b, rb_ref[b, g * HEADS_PER_GROUP + h], acc)
            o_ref[0, 0, h] = jnp.where(first, acc, NEG_INF)
            o_ref[0, 1, h] = jnp.where(band, acc, NEG_INF)

    return pl.pallas_call(
        kern, name=name, grid=(N_GROUPS,),
        out_shape=jax.ShapeDtypeStruct((N_GROUPS, 2, HEADS_PER_GROUP, BLOCK, 2 * BLOCK), F32),
        in_specs=[pl.BlockSpec(memory_space=pltpu.SMEM),
                  pl.BlockSpec((1, BLOCK, 2 * BLOCK), lambda g: (g, 0, 0))],
        out_specs=pl.BlockSpec((1, 2, HEADS_PER_GROUP, BLOCK, 2 * BLOCK), lambda g: (g, 0, 0, 0, 0)),
        compiler_params=_params(("parallel",)),
    )(rel_bias, buckets)


def _bias_bwd(dlog, buckets, name):
    def kern(dl_ref, bk_ref, o_ref):
        g = pl.program_id(0)
        bk = bk_ref[0]
        rowi = lax.broadcasted_iota(jnp.int32, (NUM_BUCKETS, 128), 0)
        coli = lax.broadcasted_iota(jnp.int32, (NUM_BUCKETS, 128), 1)

        @pl.when(g == 0)
        def _():
            o_ref[...] = jnp.zeros_like(o_ref)

        acc = jnp.zeros((NUM_BUCKETS, 128), F32)
        for h in range(HEADS_PER_GROUP):
            dv = dl_ref[0, h]
            for b in range(NUM_BUCKETS):
                t = jnp.sum(jnp.where(bk == b, dv, 0.0), axis=0, keepdims=True)
                t = jnp.sum(t, axis=1, keepdims=True)
                acc = acc + jnp.where((rowi == b) & (coli == g * HEADS_PER_GROUP + h), t, 0.0)
        o_ref[...] += acc

    return pl.pallas_call(
        kern, name=name, grid=(N_GROUPS,),
        out_shape=jax.ShapeDtypeStruct((NUM_BUCKETS, 128), F32),
        in_specs=[pl.BlockSpec((1, HEADS_PER_GROUP, BLOCK, 2 * BLOCK), lambda g: (g, 0, 0, 0)),
                  pl.BlockSpec((1, BLOCK, 2 * BLOCK), lambda g: (g, 0, 0))],
        out_specs=pl.BlockSpec((NUM_BUCKETS, 128), lambda g: (0, 0)),
        compiler_params=_params(("arbitrary",)),
    )(dlog, buckets)


def _head_masks():
    lane = lax.broadcasted_iota(jnp.int32, (BLOCK, 128), 1)
    lo = lane < HEAD_DIM
    return lo, jnp.logical_not(lo)


def _attn_fwd(qkv, bias, d, name):
    S = qkv.shape[0]
    nb = S // d // BLOCK

    def kern(q_ref, kp_ref, kc_ref, vp_ref, vc_ref, b_ref, o_ref, lse_ref):
        lo, hi = _head_masks()
        for p in range(HEADS_PER_GROUP // 2):
            sl = slice(128 * p, 128 * (p + 1))
            q = q_ref[:, sl]
            k = jnp.concatenate([kp_ref[:, sl], kc_ref[:, sl]], axis=0)
            v = jnp.concatenate([vp_ref[:, sl], vc_ref[:, sl]], axis=0)
            zero = jnp.zeros_like(q)
            q2 = jnp.concatenate([jnp.where(lo, q, zero), jnp.where(hi, q, zero)], axis=0)
            b2 = jnp.concatenate([b_ref[0, 2 * p], b_ref[0, 2 * p + 1]], axis=0)
            s = lax.dot_general(q2, k, NT, preferred_element_type=F32) * SCALE + b2
            m = jnp.max(s, axis=1, keepdims=True)
            e = jnp.exp(s - m)
            l = jnp.sum(e, axis=1, keepdims=True)
            o2 = lax.dot_general(e.astype(BF16), v, NN, preferred_element_type=F32) / l
            l2 = jnp.broadcast_to(m + jnp.log(l), (2 * BLOCK, 128))
            o_ref[:, sl] = jnp.where(lo, o2[0:BLOCK], o2[BLOCK:2 * BLOCK])
            lse_ref[:, sl] = jnp.where(lo, l2[0:BLOCK], l2[BLOCK:2 * BLOCK])

    def blk(col, prev):
        if prev:
            return pl.BlockSpec((BLOCK, ATTN_OUT), lambda r, n: (r * nb + jnp.maximum(n - 1, 0), col))
        return pl.BlockSpec((BLOCK, ATTN_OUT), lambda r, n: (r * nb + n, col))

    o_spec = pl.BlockSpec((BLOCK, ATTN_OUT), lambda r, n: (r * nb + n, 0))
    return pl.pallas_call(
        kern, name=name, grid=(d, nb),
        out_shape=[jax.ShapeDtypeStruct((S, ATTN_OUT), F32)] * 2,
        in_specs=[blk(0, False), blk(1, True), blk(1, False), blk(2, True), blk(2, False),
                  pl.BlockSpec((1, HEADS_PER_GROUP, BLOCK, 2 * BLOCK), lambda r, n: (jnp.minimum(n, 1), 0, 0, 0))],
        out_specs=[o_spec, o_spec],
        compiler_params=_params(("parallel", "arbitrary")),
    )(qkv, qkv, qkv, qkv, qkv, bias)


def _attn_bwd(qkv, do, lse, delta, bias, d, name):
    S = qkv.shape[0]
    nb = S // d // BLOCK
    low = -3.0e38

    def kern(q_ref, kp_ref, kc_ref, vp_ref, vc_ref, do_ref, lse_ref, dl_ref, b_ref,
             dq_ref, dkv_ref, db_ref, ck_ref, cv_ref):
        r, n = pl.program_id(0), pl.program_id(1)

        @pl.when((r == 0) & (n == 0))
        def _():
            db_ref[...] = jnp.zeros_like(db_ref)

        @pl.when(n == 0)
        def _():
            ck_ref[...] = jnp.zeros_like(ck_ref)
            cv_ref[...] = jnp.zeros_like(cv_ref)

        @pl.when(n < nb)
        def _():
            lo, hi = _head_masks()
            for p in range(HEADS_PER_GROUP // 2):
                sl = slice(128 * p, 128 * (p + 1))
                sv = slice(ATTN_OUT + 128 * p, ATTN_OUT + 128 * (p + 1))
                q = q_ref[:, sl]
                k = jnp.concatenate([kp_ref[:, sl], kc_ref[:, sl]], axis=0)
                v = jnp.concatenate([vp_ref[:, sl], vc_ref[:, sl]], axis=0)
                dov = do_ref[:, sl]
                lse_b = lse_ref[:, sl]
                del_b = dl_ref[:, sl]
                zero = jnp.zeros_like(q)
                q2 = jnp.concatenate([jnp.where(lo, q, zero), jnp.where(hi, q, zero)], axis=0)
                do2 = jnp.concatenate([jnp.where(lo, dov, zero), jnp.where(hi, dov, zero)], axis=0)
                lse2 = jnp.concatenate([jnp.max(jnp.where(msk, lse_b, low), axis=1, keepdims=True) for msk in (lo, hi)], axis=0)
                del2 = jnp.concatenate([jnp.max(jnp.where(msk, del_b, low), axis=1, keepdims=True) for msk in (lo, hi)], axis=0)
                b2 = jnp.concatenate([b_ref[0, 2 * p], b_ref[0, 2 * p + 1]], axis=0)
                s = lax.dot_general(q2, k, NT, preferred_element_type=F32) * SCALE + b2
                pr = jnp.exp(s - lse2)
                dp = lax.dot_general(do2, v, NT, preferred_element_type=F32)
                ds = pr * (dp - del2)
                db_ref[2 * p] += ds[0:BLOCK]
                db_ref[2 * p + 1] += ds[BLOCK:2 * BLOCK]
                dsb = (ds * SCALE).astype(BF16)
                dq2 = lax.dot_general(dsb, k, NN, preferred_element_type=F32)
                dk_acc = lax.dot_general(dsb, q2, TN, preferred_element_type=F32)
                dv_acc = lax.dot_general(pr.astype(BF16), do2, TN, preferred_element_type=F32)
                dq_ref[:, sl] = jnp.where(lo, dq2[0:BLOCK], dq2[BLOCK:2 * BLOCK]).astype(BF16)
                dkv_ref[:, sl] = (ck_ref[:, sl] + dk_acc[0:BLOCK]).astype(BF16)
                dkv_ref[:, sv] = (cv_ref[:, sl] + dv_acc[0:BLOCK]).astype(BF16)
                ck_ref[:, sl] = dk_acc[BLOCK:2 * BLOCK]
                cv_ref[:, sl] = dv_acc[BLOCK:2 * BLOCK]

        @pl.when(n == nb)
        def _():
            dkv_ref[:, 0:ATTN_OUT] = ck_ref[...].astype(BF16)
            dkv_ref[:, ATTN_OUT:2 * ATTN_OUT] = cv_ref[...].astype(BF16)

    def cur(n):
        return jnp.minimum(n, nb - 1)

    def blk(col, prev):
        if prev:
            return pl.BlockSpec((BLOCK, ATTN_OUT), lambda r, n: (r * nb + jnp.maximum(cur(n) - 1, 0), col))
        return pl.BlockSpec((BLOCK, ATTN_OUT), lambda r, n: (r * nb + cur(n), col))

    q_like = pl.BlockSpec((BLOCK, ATTN_OUT), lambda r, n: (r * nb + cur(n), 0))
    return pl.pallas_call(
        kern, name=name, grid=(d, nb + 1),
        out_shape=[jax.ShapeDtypeStruct((S, ATTN_OUT), BF16), jax.ShapeDtypeStruct((S, 2 * ATTN_OUT), BF16),
                   jax.ShapeDtypeStruct((HEADS_PER_GROUP, BLOCK, 2 * BLOCK), F32)],
        in_specs=[blk(0, False), blk(1, True), blk(1, False), blk(2, True), blk(2, False),
                  q_like, q_like, q_like,
                  pl.BlockSpec((1, HEADS_PER_GROUP, BLOCK, 2 * BLOCK),
                               lambda r, n: (jnp.minimum(cur(n), 1), 0, 0, 0))],
        out_specs=[q_like,
                   pl.BlockSpec((BLOCK, 2 * ATTN_OUT), lambda r, n: (r * nb + jnp.maximum(n - 1, 0), 0)),
                   pl.BlockSpec((HEADS_PER_GROUP, BLOCK, 2 * BLOCK), lambda r, n: (0, 0, 0))],
        scratch_shapes=[pltpu.VMEM((BLOCK, ATTN_OUT), F32), pltpu.VMEM((BLOCK, ATTN_OUT), F32)],
        compiler_params=_params(("arbitrary", "arbitrary")),
    )(qkv, qkv, qkv, qkv, qkv, do, lse, delta, bias)


def _by_residue(a, dil):
    return a if dil == 1 else a.reshape(dil, a.shape[0] // dil, a.shape[1])


def _flat(a):
    return a if a.ndim == 2 else a.reshape(a.shape[0] * a.shape[1], a.shape[2])


def _combine_fwd(os_, lses, name):
    S, W = os_[0].shape
    tm = _pick(S, [512, 256, 128])
    perm = [dil for dil in DILS if dil > 1]

    def kern(*refs):
        o_in, l_in = refs[0:N_GROUPS], refs[N_GROUPS:2 * N_GROUPS]
        of_ref, ob_ref, lse_ref = refs[2 * N_GROUPS:2 * N_GROUPS + 3]
        lse_p = refs[2 * N_GROUPS + 3:2 * N_GROUPS + 3 + len(perm)]
        scr = refs[2 * N_GROUPS + 3 + len(perm):]
        ov, lv = [], []
        si = 0
        for g, dil in enumerate(DILS):
            if dil == 1:
                ov.append(o_in[g][...])
                lv.append(l_in[g][...])
            else:
                so, sl = scr[si], scr[si + 1]
                si += 2
                for res in range(dil):
                    _put_residue(so, res, dil, o_in[g][res])
                    _put_residue(sl, res, dil, l_in[g][res])
                ov.append(_unstage(so))
                lv.append(_unstage(sl))
        m = jnp.maximum(jnp.maximum(lv[0], lv[1]), lv[2])
        e = [jnp.exp(t - m) for t in lv]
        tot = e[0] + e[1] + e[2]
        o = (e[0] * ov[0] + e[1] * ov[1] + e[2] * ov[2]) / tot
        lse = m + jnp.log(tot)
        of_ref[...] = o
        ob_ref[...] = o.astype(BF16)
        lse_ref[...] = lse
        sl = scr[1]
        _stage(sl, lse)
        for dil, p_ref in zip(perm, lse_p):
            for res in range(dil):
                p_ref[res] = _get_residue(sl, res, dil)

    def in_spec(dil):
        return _row_spec(tm, W) if dil == 1 else _perm_spec(dil, tm, W)

    ins = [_by_residue(a, dil) for a, dil in zip(os_, DILS)] + [_by_residue(a, dil) for a, dil in zip(lses, DILS)]
    return pl.pallas_call(
        kern, name=name, grid=(S // tm,),
        out_shape=[jax.ShapeDtypeStruct((S, W), F32), jax.ShapeDtypeStruct((S, W), BF16),
                   jax.ShapeDtypeStruct((S, W), F32)]
        + [jax.ShapeDtypeStruct((dil, S // dil, W), F32) for dil in perm],
        in_specs=[in_spec(dil) for dil in DILS] * 2,
        out_specs=[_row_spec(tm, W)] * 3 + [_perm_spec(dil, tm, W) for dil in perm],
        scratch_shapes=[_stage_shape(tm, W) for _ in range(2 * len(perm))],
        compiler_params=_params(("parallel",)),
    )(*ins)


def _delta(do, o, name):
    S, W = o.shape
    tm = _pick(S, [512, 256, 128])
    perm = [dil for dil in DILS if dil > 1]

    def kern(do_ref, o_ref, dob_ref, d_ref, *rest):
        scr, scr_do = rest[2 * len(perm)], rest[2 * len(perm) + 1]
        prod = do_ref[...] * o_ref[...]
        ri = jnp.right_shift(lax.broadcasted_iota(jnp.int32, (W, W), 0), HEAD_SHIFT)
        ci = jnp.right_shift(lax.broadcasted_iota(jnp.int32, (W, W), 1), HEAD_SHIFT)
        same = jnp.where(ri == ci, 1.0, 0.0).astype(BF16)
        hi_p = prod.astype(BF16)
        lo_p = (prod - hi_p.astype(F32)).astype(BF16)
        dl = (lax.dot_general(hi_p, same, NN, preferred_element_type=F32)
              + lax.dot_general(lo_p, same, NN, preferred_element_type=F32))
        d_ref[...] = dl
        dob_ref[...] = do_ref[...].astype(BF16)
        _stage(scr, dl)
        _stage(scr_do, do_ref[...])
        for j, dil in enumerate(perm):
            for res in range(dil):
                rest[2 * j][res] = _get_residue(scr_do, res, dil).astype(BF16)
                rest[2 * j + 1][res] = _get_residue(scr, res, dil)

    out_shape = [jax.ShapeDtypeStruct((S, W), BF16), jax.ShapeDtypeStruct((S, W), F32)]
    out_specs = [_row_spec(tm, W), _row_spec(tm, W)]
    for dil in perm:
        out_shape += [jax.ShapeDtypeStruct((dil, S // dil, W), BF16), jax.ShapeDtypeStruct((dil, S // dil, W), F32)]
        out_specs += [_perm_spec(dil, tm, W), _perm_spec(dil, tm, W)]
    return pl.pallas_call(
        kern, name=name, grid=(S // tm,),
        out_shape=out_shape,
        in_specs=[_row_spec(tm, W), _row_spec(tm, W)], out_specs=out_specs,
        scratch_shapes=[_stage_shape(tm, W), _stage_shape(tm, W)],
        compiler_params=_params(("parallel",)),
    )(do, o)


def _ada_fwd(c16, ada_w, name):
    depth, D, n = ada_w.shape
    rows = 2 * N_DEV

    def kern(c_ref, w_ref, o_ref, cs_ref):
        cv = c_ref[...]
        cs = cv * jax.nn.sigmoid(cv)
        cs_ref[...] = cs
        o_ref[0] = _dot3(cs, w_ref[0], NN)

    return pl.pallas_call(
        kern, name=name, grid=(depth,),
        out_shape=[jax.ShapeDtypeStruct((depth, rows, n), F32), jax.ShapeDtypeStruct((rows, D), F32)],
        in_specs=[pl.BlockSpec((rows, D), lambda l: (0, 0)), pl.BlockSpec((1, D, n), lambda l: (l, 0, 0))],
        out_specs=[pl.BlockSpec((1, rows, n), lambda l: (l, 0, 0)), pl.BlockSpec((rows, D), lambda l: (0, 0))],
        compiler_params=_params(("arbitrary",)),
    )(c16, ada_w)


def _ada_bwd(cs16, dm16, name):
    depth, _, n = dm16.shape
    D = cs16.shape[1]

    def kern(cs_ref, dm_ref, o_ref):
        o_ref[0] = _dot3(cs_ref[...], dm_ref[0], TN)

    return pl.pallas_call(
        kern, name=name, grid=(depth,),
        out_shape=jax.ShapeDtypeStruct((depth, D, n), F32),
        in_specs=[pl.BlockSpec((2 * N_DEV, D), lambda l: (0, 0)), pl.BlockSpec((1, 2 * N_DEV, n), lambda l: (l, 0, 0))],
        out_specs=pl.BlockSpec((1, D, n), lambda l: (l, 0, 0)),
        compiler_params=_params(("parallel",)),
    )(cs16, dm16)


def _sum_rows8(parts, name):
    _, r, n = parts.shape

    def kern(p_ref, o_ref):
        acc = p_ref[0]
        for k in range(1, N_DEV):
            acc = acc + p_ref[k]
        o_ref[...] = acc

    return pl.pallas_call(
        kern, name=name, out_shape=jax.ShapeDtypeStruct((r, n), F32),
        in_specs=[pl.BlockSpec(memory_space=pltpu.VMEM)], out_specs=pl.BlockSpec(memory_space=pltpu.VMEM),
    )(parts)


def _adamw(w, g, m, v, name):
    shape = w.shape
    c = shape[-1]
    r = int(np.prod(shape[:-1])) if len(shape) > 1 else 1
    w2, g2, m2, v2 = (t.reshape(r, c) for t in (w, g, m, v))
    tr = r
    for cand in (2048, 1024, 512, 256, 128, 64, 32, 16, 8):
        if r % cand == 0 and cand * c * 4 <= (1 << 20):
            tr = cand
            break
    c1 = 1.0 - ADAM_B1 ** ADAM_STEP
    c2 = 1.0 - ADAM_B2 ** ADAM_STEP

    def kern(w_ref, g_ref, m_ref, v_ref, d_ref, nm_ref, nv_ref):
        gv = g_ref[...]
        nm = ADAM_B1 * m_ref[...] + (1.0 - ADAM_B1) * gv
        nv = ADAM_B2 * v_ref[...] + (1.0 - ADAM_B2) * (gv * gv)
        nm_ref[...] = nm
        nv_ref[...] = nv
        d_ref[...] = -ADAM_LR * ((nm / c1) / (jnp.sqrt(nv / c2) + ADAM_EPS) + ADAM_WD * w_ref[...])

    spec = pl.BlockSpec((tr, c), lambda i: (i, 0))
    outs = pl.pallas_call(
        kern, name=name, grid=(r // tr,),
        out_shape=[jax.ShapeDtypeStruct((r, c), F32)] * 3,
        in_specs=[spec] * 4, out_specs=[spec] * 3,
        compiler_params=_params(("parallel",)),
    )(w2, g2, m2, v2)
    return tuple(o.reshape(shape) for o in outs)


def _adamw_slab(w3, g, m3, v3, idx, prev, name):
    ns, r, c = w3.shape
    tr = r
    for k in range(1, r + 1):
        if r % k == 0 and (r // k) % 8 == 0 and (r // k) * c * 4 <= 3 * (1 << 19):
            tr = r // k
            break
    c1 = 1.0 - ADAM_B1 ** ADAM_STEP
    c2 = 1.0 - ADAM_B2 ** ADAM_STEP

    def kern(w_ref, g_ref, m_ref, v_ref, p0, p1, p2, p3, go_ref, d_ref, nm_ref, nv_ref):
        gv = g_ref[...]
        nm = ADAM_B1 * m_ref[0] + (1.0 - ADAM_B1) * gv
        nv = ADAM_B2 * v_ref[0] + (1.0 - ADAM_B2) * (gv * gv)
        go_ref[0] = gv
        nm_ref[0] = nm
        nv_ref[0] = nv
        d_ref[0] = -ADAM_LR * ((nm / c1) / (jnp.sqrt(nv / c2) + ADAM_EPS) + ADAM_WD * w_ref[0])

    if prev is None:
        prev = [lax.empty((ns, r, c), F32) for _ in range(4)]
    slab = pl.BlockSpec((1, tr, c), lambda i: (idx, i, 0))
    return pl.pallas_call(
        kern, name=name, grid=(r // tr,),
        out_shape=[jax.ShapeDtypeStruct((ns, r, c), F32)] * 4,
        in_specs=[slab, pl.BlockSpec((tr, c), lambda i: (i, 0)), slab, slab] + [pl.BlockSpec(memory_space=pl.ANY)] * 4,
        out_specs=[slab] * 4,
        input_output_aliases={4: 0, 5: 1, 6: 2, 7: 3},
        compiler_params=_params(("parallel",)),
    )(w3, g, m3, v3, *prev)


def kernel(x, c, ada_w, ada_b, norm_g, ffn_w_gate, ffn_w_up, ffn_w_down, w_in, conv_w, w_conv_out, w_attn_out, w_o, rel_bias, final_g, loss_target, m_ada_w, m_ada_b, m_norm_g, m_ffn_w_gate, m_ffn_w_up, m_ffn_w_down, m_w_in, m_conv_w, m_w_conv_out, m_w_attn_out, m_w_o, m_rel_bias, m_final_g, v_ada_w, v_ada_b, v_norm_g, v_ffn_w_gate, v_ffn_w_up, v_ffn_w_down, v_w_in, v_conv_w, v_w_conv_out, v_w_attn_out, v_w_o, v_rel_bias, v_final_g):
    depth = ada_w.shape[0]
    S, D = x.shape[1], x.shape[2]
    me = 4 * lax.axis_index("x") + 2 * lax.axis_index("y") + lax.axis_index("c")
    x0 = x.reshape(S, D)
    target = loss_target.reshape(S, D)
    fsh = ffn_w_down.shape[2]
    insh = w_in.shape[2]
    dsh = D // N_DEV
    ao_rows = dsh * ATTN_OUT // D

    piece_rows = [fsh] * 6 + [insh, dsh, dsh, ao_rows]

    FIRST, MIXER, SECOND = [0, 2, 4], [6, 7, 8, 9], [1, 3, 5]

    def rows_of(idx):
        return [piece_rows[p] for p in idx]

    def pack(l, idx=None):
        def t(a):
            return jnp.transpose(a).astype(BF16)
        ps = [t(ffn_w_gate[l, 0]), t(ffn_w_gate[l, 1]), t(ffn_w_up[l, 0]), t(ffn_w_up[l, 1]),
              ffn_w_down[l, 0].astype(BF16), ffn_w_down[l, 1].astype(BF16), t(w_in[l]),
              w_conv_out[l].astype(BF16), w_o[l].astype(BF16), t(w_attn_out[l]).reshape(ao_rows, D)]
        return jnp.concatenate(ps if idx is None else [ps[p] for p in idx], axis=0)

    def mixer_weights(full):
        in_t = full[6]
        qkv_t = [jnp.concatenate([in_t[t * QKV_W + g * ATTN_OUT: t * QKV_W + (g + 1) * ATTN_OUT] for t in range(3)])
                 for g in range(N_GROUPS)]
        ao_t = full[9].reshape(N_DEV, dsh, ATTN_OUT).reshape(D, ATTN_OUT)
        return dict(qkv_t=qkv_t, cg_t=in_t[3 * QKV_W:], co=full[7], wo=full[8], ao_t=ao_t)

    def behind(v, token):
        return v + token[0, 0].astype(v.dtype)

    c_all = _all_gather(c.reshape(D // 128, 128), "c_all_gather").reshape(N_DEV, D)
    c16 = jnp.concatenate([c_all, jnp.zeros_like(c_all)], axis=0)
    mod_part, cs16 = _ada_fwd(c16, ada_w, "ada_fwd")
    mod_part = mod_part[:, :N_DEV]
    n_ada = ada_w.shape[2]
    mod_all = _all_gather(mod_part.reshape(depth * N_DEV * n_ada // 128, 128), "mod_all_gather")
    mod_all = mod_all.reshape(N_DEV, depth, N_DEV, n_ada)
    mod_mine = lax.dynamic_index_in_dim(mod_all, me, axis=2, keepdims=False)
    mod = jnp.transpose(mod_mine, (1, 0, 2)).reshape(depth, N_DEV * n_ada) + ada_b
    mod = mod.reshape(depth, 3, 3, 1, D)

    small = jnp.concatenate([norm_g.reshape(-1), conv_w.reshape(-1)]).reshape(-1, 128)
    small_all = _all_gather(small, "small_all_gather").reshape(N_DEV, -1)
    n_ng = norm_g.size
    norm_g_full = jnp.transpose(small_all[:, :n_ng].reshape(N_DEV, depth, 3, dsh), (1, 2, 0, 3)).reshape(depth, 3, 1, D)
    conv_w_full = jnp.transpose(small_all[:, n_ng:].reshape(N_DEV, depth, CONV_K, dsh), (1, 2, 0, 3)).reshape(depth, CONV_K, D)

    buckets = jnp.asarray(_bucket_tables())
    bias = _bias_build(rel_bias, buckets, "bias_build")
    perm_dils = tuple(dil for dil in DILS if dil > 1)

    chain_done = mod.reshape(-1)[:128] + small_all.reshape(-1)[:128]
    part0 = [(FIRST, "first"), (MIXER, "mixer"), (SECOND, "second")]
    started, after0 = [], chain_done
    for idx, tag in part0:
        started.append(_gather_start(pack(0, idx), rows_of(idx), after0, f"weights_gather_start_l0_{tag}"))
        after0 = started[-1][3]
    full0 = [None] * len(piece_rows)

    def arrive0(k, after_arr):
        idx, tag = part0[k]
        st = started[k]
        fw = _gather_forward(st[0], st[1], st[2], rows_of(idx), after_arr, f"weights_gather_forward_l0_{tag}")
        for p, z in zip(idx, _gather_finish(fw[0], fw[1], fw[2], after_arr, f"weights_gather_finish_l0_{tag}")):
            full0[p] = z
        return fw[3]

    arrive0(0, bias)
    W = [dict(g_t=[full0[0]], u_t=[full0[2]], down=[full0[4]])] + [None] * (depth - 1)

    saved = []
    xc = x0
    for l in range(depth):
        sv = {}
        gather, tie_sub, pin = None, 0, None
        if 0 < l < depth - 1:
            gather = _gather_start(pack(l + 1), piece_rows, W[l]["wo"], f"weights_gather_start_l{l + 1}")
        for sub in (0, 1, 2):
            if l == 0 and sub == 1:
                arrive0(1, xc)
                W[0].update(mixer_weights(full0))
                if depth > 1:
                    gather, tie_sub = _gather_start(pack(1), piece_rows, W[0]["wo"], "weights_gather_start_l1"), 1
            if l == 0 and sub == 2:
                arrive0(2, xc)
                W[0].update(g_t=full0[0:2], u_t=full0[2:4], down=full0[4:6])
            g, sh, sc, gt = norm_g_full[l, sub], mod[l, sub, 0], mod[l, sub, 1], mod[l, sub, 2]
            if sub == tie_sub and gather is not None:
                g = behind(g, gather[3])
            if l == 0 and sub == 0:
                g = behind(g, started[-1][3])
            if sub == 2 and pin is not None:
                g = behind(g, pin)
            rec = dict(x=xc)
            if sub != 1:
                i = 0 if sub == 0 else 1
                h = _norm_mod_fwd(xc, g, sc, sh, "norm_mod_fwd")[0]
                a, u, z = _ffn_up(h, W[l]["g_t"][i], W[l]["u_t"][i], "ffn_up")
                xc, f = _matmul(z, W[l]["down"][i], "nn", BF16, "ffn_down", tm=512, resid=(xc, gt, 0.5))
                rec.update(h=h, a=a, u=u, z=z, f=f)
            else:
                hs = _norm_mod_fwd(xc, g, sc, sh, "norm_mod_fwd_mixer", dils=perm_dils)
                h = hs[0]
                h_res = [h] + [_flat(t) for t in hs[1:]]
                cg = _matmul(h, W[l]["cg_t"], "nt", BF16, "mixer_cg")
                qkvs, os_, lses = [], [], []
                for gi, dil in enumerate(DILS):
                    qkv = _matmul(h_res[gi], W[l]["qkv_t"][gi], "nt", BF16, "mixer_qkv", tn=3 * ATTN_OUT)
                    o_g, lse_g = _attn_fwd(qkv, bias[gi], dil, f"attn_fwd_g{gi}")
                    qkvs.append(qkv)
                    os_.append(o_g)
                    lses.append(lse_g)
                comb = _combine_fwd(os_, lses, "combine_fwd")
                o_f, o_b, lse = comb[0:3]
                lse_res = [lse] + [_flat(t) for t in comb[3:]]
                yc_in = _conv_fwd(cg, conv_w_full[l], "conv_fwd")
                yc = _matmul(yc_in, W[l]["co"], "nn", BF16, "conv_out")
                ya = _matmul(o_b, W[l]["ao_t"], "nt", BF16, "attn_out")
                merged = _merge_fwd(cg, yc, ya, "merge_fwd")
                xc, f = _matmul(merged, W[l]["wo"], "nn", BF16, "mixer_out", resid=(xc, gt, 1.0))
                rec.update(h=h, h_res=h_res, qkvs=qkvs, cg=cg, o_f=o_f, o_b=o_b, lse_res=lse_res, yc_in=yc_in,
                           yc=yc, ya=ya, merged=merged, f=f)
                if gather is not None:
                    fwd = _gather_forward(gather[0], gather[1], gather[2], piece_rows, xc,
                                          f"weights_gather_forward_l{l + 1}")
                    pin = fwd[3]
            sv[sub] = rec
        if gather is not None:
            full = _gather_finish(fwd[0], fwd[1], fwd[2], xc, f"weights_gather_finish_l{l + 1}")
            W[l + 1] = dict(g_t=full[0:2], u_t=full[2:4], down=full[4:6], **mixer_weights(full))
        saved.append(sv)

    def gate_of(l, sub):
        return saved[l][sub]["f"], mod[l, sub, 2], (1.0 if sub == 1 else 0.5)

    def below(l, sub):
        if sub > 0:
            return gate_of(l, sub - 1)
        return gate_of(l - 1, 2) if l > 0 else None

    dx, head, df = _loss_head(xc, final_g.reshape(1, D), target, gate_of(depth - 1, 2), "loss_head")
    d_final_g = head[0]
    loss_part = head[2, 0]
    d_gate = head[4]

    d_mod = [[None] * 3 for _ in range(depth)]
    d_norm = [[None] * 3 for _ in range(depth)]
    d_conv = [None] * depth
    dlog = jnp.zeros((N_GROUPS, HEADS_PER_GROUP, BLOCK, 2 * BLOCK), F32)
    n_pieces = len(piece_rows)
    LATE = (0, 2, 4)
    EARLY = tuple(p for p in range(n_pieces) if p not in LATE)
    g_piece = [[None] * n_pieces for _ in range(depth)]

    piece_keys = (("g_t", 0), ("g_t", 1), ("u_t", 0), ("u_t", 1), ("down", 0), ("down", 1), "in_t", "co", "wo", "ao_t")

    def pieces_of(dW, idx):
        return [dW[piece_keys[p]].reshape(N_DEV * piece_rows[p], D) for p in idx]

    def finish_scatter(sc, idx, after_arr, layer, part=""):
        recv = _scatter_finish(sc[0], sc[1], sc[2], after_arr, f"grads_scatter_finish_l{layer}{part}")
        tot = _sum_sources(recv, "grads_sum")
        o = 0
        for p in idx:
            g_piece[layer][p] = tot[o:o + piece_rows[p]]
            o += piece_rows[p]
        return recv

    scatter = None
    early = None
    for l in reversed(range(depth)):
        dW = {}
        for sub in (2, 1, 0):
            rec = saved[l][sub]
            g, sc = norm_g_full[l, sub], mod[l, sub, 1]
            if sub == 2 and scatter is not None:
                df = behind(df, scatter[3])
            if sub == 0 and early is not None:
                df = behind(df, early[3])
            nxt = below(l, sub)
            if sub != 1:
                i = 0 if sub == 0 else 1
                dz = _matmul(df, W[l]["down"][i], "nt", BF16, "ffn_down_dx")
                dW["down", i] = _matmul(rec["z"], df, "tn", BF16, "ffn_down_dw")
                da, du, dh = _ffn_up_bwd(dz, rec["a"], rec["u"], W[l]["g_t"][i], W[l]["u_t"][i], "ffn_up_bwd")
                dW["g_t", i] = _matmul(da, rec["h"], "tn", BF16, "ffn_gate_dw")
                dW["u_t", i] = _matmul(du, rec["h"], "tn", BF16, "ffn_up_dw")
                res = _norm_mod_bwd(rec["x"], [dh], [], dx, g, sc, "norm_mod_bwd", nxt=nxt)
            else:
                dout = df
                dm = _matmul(dout, W[l]["wo"], "nt", BF16, "mixer_out_dx")
                dW["wo"] = _matmul(rec["merged"], dout, "tn", BF16, "mixer_out_dw")
                dyc, dya, dgg = _merge_bwd(rec["cg"], rec["yc"], rec["ya"], dm, "merge_bwd")
                dyc_in = _matmul(dyc, W[l]["co"], "nt", BF16, "conv_out_dx")
                dW["co"] = _matmul(rec["yc_in"], dyc, "tn", BF16, "conv_out_dw")
                do = _matmul(dya, W[l]["ao_t"], "nn", F32, "attn_out_dx")
                dW["ao_t"] = _matmul(dya, rec["o_b"], "tn", BF16, "attn_out_dw")
                dl = _delta(do, rec["o_f"], "attn_delta")
                do_res = [dl[0]] + [_flat(t) for t in dl[2::2]]
                del_res = [dl[1]] + [_flat(t) for t in dl[3::2]]
                dh_attn, dw_q, dw_kv, dlog_l = [], [], [], []
                for gi, dil in enumerate(DILS):
                    dq, dkv, dlg = _attn_bwd(rec["qkvs"][gi], do_res[gi], rec["lse_res"][gi], del_res[gi],
                                             bias[gi], dil, f"attn_bwd_g{gi}")
                    dlog_l.append(dlg)
                    dh_attn.append(_attn_dh(dq, dkv, W[l]["qkv_t"][gi], "attn_dh"))
                    dw_q.append(_matmul(dq, rec["h_res"][gi], "tn", BF16, "mixer_q_dw"))
                    dw_kv.append(_matmul(dkv, rec["h_res"][gi], "tn", BF16, "mixer_kv_dw"))
                dlog = dlog + jnp.stack(dlog_l)
                dcg, conv_sum = _conv_bwd(rec["cg"], dyc_in, dgg, conv_w_full[l], "conv_bwd")
                d_conv[l] = conv_sum[0:CONV_K]
                dh_cg = _matmul(dcg, W[l]["cg_t"], "nn", F32, "mixer_cg_dx")
                dw_cg = _matmul(dcg, rec["h"], "tn", BF16, "mixer_cg_dw")
                dW["in_t"] = jnp.concatenate(
                    dw_q + [t[:ATTN_OUT] for t in dw_kv] + [t[ATTN_OUT:] for t in dw_kv] + [dw_cg], axis=0)
                perm_parts = [(dil, _by_residue(dh_attn[gi], dil)) for gi, dil in enumerate(DILS) if dil > 1]
                res = _norm_mod_bwd(rec["x"], [dh_cg, dh_attn[0]], perm_parts, dx, g, sc, "norm_mod_bwd_mixer", nxt=nxt)
            dx, sums = res[0], res[1]
            d_mod[l][sub] = jnp.stack([sums[0], sums[2], d_gate])
            d_norm[l][sub] = sums[3]
            if nxt is not None:
                df, d_gate = res[2], sums[4]
            if l == 0 and sub == 1:
                after = dx
                if scatter is not None:
                    after = finish_scatter(scatter, range(n_pieces), dx, l + 1)
                    scatter = None
                early = _scatter_start(pieces_of(dW, EARLY), after, "grads_scatter_start_l0_early")
        if l > 0:
            after = dx
            if scatter is not None:
                after = finish_scatter(scatter, range(n_pieces), dx, l + 1)
            scatter = _scatter_start(pieces_of(dW, range(n_pieces)), after, f"grads_scatter_start_l{l}")
        else:
            late = _scatter_start(pieces_of(dW, LATE), dx, "grads_scatter_start_l0_late")
    grad_x = dx.reshape(1, S, D)
    d_rel = _bias_bwd(dlog, buckets, "bias_bwd")[:, :rel_bias.shape[1]]

    big = dict(
        ffn_w_gate=(ffn_w_gate, m_ffn_w_gate, v_ffn_w_gate, (0, 1), "t_shard"),
        ffn_w_up=(ffn_w_up, m_ffn_w_up, v_ffn_w_up, (2, 3), "t_shard"),
        ffn_w_down=(ffn_w_down, m_ffn_w_down, v_ffn_w_down, (4, 5), "rows"),
        w_in=(w_in, m_w_in, v_w_in, (6,), "t_shard"),
        w_conv_out=(w_conv_out, m_w_conv_out, v_w_conv_out, (7,), "rows"),
        w_o=(w_o, m_w_o, v_w_o, (8,), "rows"),
        w_attn_out=(w_attn_out, m_w_attn_out, v_w_attn_out, (9,), "t_grad"))
    big_out = {name: None for name in big}

    def adam_layer(l, token=None):
        for name, (w_, m_, v_, plist, how) in big.items():
            if how == "t_shard":
                w_, m_, v_ = (jnp.swapaxes(t, -1, -2) for t in (w_, m_, v_))
            ns, r, cc = depth * len(plist), w_.shape[-2], w_.shape[-1]
            w3, m3, v3 = (t.reshape(ns, r, cc) for t in (w_, m_, v_))
            for j, p in enumerate(plist):
                gl = g_piece[l][p]
                if how == "t_grad":
                    gl = jnp.transpose(gl.reshape(cc, r))
                if token is not None:
                    gl = behind(gl, token)
                big_out[name] = _adamw_slab(w3, gl, m3, v3, l * len(plist) + j, big_out[name], "adamw_" + name)

    for l in range(depth - 1, 0, -1):
        adam_layer(l, late[3])
    done = [late[3]] + [st[1] for st in big_out.values() if st is not None]
    finish_scatter(early, EARLY, done, 0, "_early")
    finish_scatter(late, LATE, dx, 0, "_late")
    adam_layer(0)

    d_mod_flat = jnp.stack([jnp.stack(d_mod[l]) for l in range(depth)]).reshape(-1)
    d_norm_flat = jnp.stack([jnp.stack(d_norm[l]) for l in range(depth)]).reshape(-1)
    d_conv_flat = jnp.stack(d_conv).reshape(-1)
    vec = jnp.concatenate([d_mod_flat, d_norm_flat, d_conv_flat, d_rel.reshape(-1), d_final_g,
                           jnp.broadcast_to(loss_part, (128,))])
    pad = (-vec.size) % 1024
    vec = jnp.concatenate([vec, jnp.zeros((pad,), F32)]).reshape(-1, 128)
    parts = _all_gather(vec, "small_grads_all_gather").reshape(N_DEV, vec.shape[0], 128)
    tot = _sum_rows8(parts, "small_grads_sum").reshape(-1)
    o0 = 0
    g_ada_b = tot[o0:o0 + d_mod_flat.size].reshape(ada_b.shape)
    o0 += d_mod_flat.size
    g_norm_full = tot[o0:o0 + d_norm_flat.size].reshape(depth, 3, D)
    o0 += d_norm_flat.size
    g_conv_full = tot[o0:o0 + d_conv_flat.size].reshape(depth, CONV_K, D)
    o0 += d_conv_flat.size
    g_rel = tot[o0:o0 + rel_bias.size].reshape(rel_bias.shape)
    o0 += rel_bias.size
    g_final = tot[o0:o0 + D]
    o0 += D
    loss = tot[o0]
    g_norm = lax.dynamic_slice_in_dim(g_norm_full, me * dsh, dsh, axis=2)
    g_conv = lax.dynamic_slice_in_dim(g_conv_full, me * dsh, dsh, axis=2)

    dm_all = parts.reshape(N_DEV, -1)[:, :d_mod_flat.size].reshape(N_DEV, depth, N_DEV * n_ada)
    dm_cols = lax.dynamic_slice_in_dim(dm_all, me * n_ada, n_ada, axis=2)
    dm16 = jnp.concatenate([jnp.transpose(dm_cols, (1, 0, 2)), jnp.zeros((depth, N_DEV, n_ada), F32)], axis=1)
    g_ada_w = _ada_bwd(cs16, dm16, "ada_bwd")

    small = dict(ada_w=(ada_w, g_ada_w, m_ada_w, v_ada_w), ada_b=(ada_b, g_ada_b, m_ada_b, v_ada_b),
                 norm_g=(norm_g, g_norm, m_norm_g, v_norm_g), conv_w=(conv_w, g_conv, m_conv_w, v_conv_w),
                 rel_bias=(rel_bias, g_rel, m_rel_bias, v_rel_bias), final_g=(final_g, g_final, m_final_g, v_final_g))
    order = ("ada_w", "ada_b", "norm_g", "ffn_w_gate", "ffn_w_up", "ffn_w_down", "w_in", "conv_w", "w_conv_out",
             "w_attn_out", "w_o", "rel_bias", "final_g")
    res = {}
    for name in order:
        if name in big:
            shape = big[name][0].shape
            if big[name][4] == "t_shard":
                t_shape = shape[:-2] + (shape[-1], shape[-2])
                res[name] = tuple(jnp.swapaxes(t.reshape(t_shape), -1, -2) for t in big_out[name])
            else:
                res[name] = tuple(t.reshape(shape) for t in big_out[name])
        else:
            w_, g_, m_, v_ = small[name]
            res[name] = (g_,) + _adamw(w_, g_, m_, v_, "adamw_" + name)
    return (loss, grad_x, *[res[n][0] for n in order], *[res[n][1] for n in order],
            *[res[n][2] for n in order], *[res[n][3] for n in order])
```

```python
import functools

import numpy as np
import jax
import jax.numpy as jnp
from jax import lax
from jax.experimental import pallas as pl
from jax.experimental.pallas import tpu as pltpu

F32 = jnp.float32
BF16 = jnp.bfloat16

N_DEV = 8
HEAD_DIM = 64
HEAD_SHIFT = 6
HEADS_PER_GROUP = 8
DILATION_GROUPS = ((128, 1), (512, 4), (2048, 16))
DILS = tuple(d for _, d in DILATION_GROUPS)
N_GROUPS = len(DILATION_GROUPS)
ATTN_OUT = HEADS_PER_GROUP * HEAD_DIM
QKV_W = N_GROUPS * ATTN_OUT
BLOCK = 128
NUM_BUCKETS = 32
MAX_DISTANCE = 2048
CONV_K = 3
EPS = 1e-6
NEG_INF = -1e30
SCALE = HEAD_DIM ** -0.5

ADAM_LR = 0.001
ADAM_B1 = 0.9
ADAM_B2 = 0.999
ADAM_EPS = 1e-08
ADAM_WD = 0.01
ADAM_STEP = 10

V7X_VMEM_LIMIT = 48 * 1024 * 1024
MESH = pl.DeviceIdType.MESH

NN = (((1,), (0,)), ((), ()))
NT = (((1,), (1,)), ((), ()))
TN = (((0,), (0,)), ((), ()))


def _pick(dim, cands):
    for c in cands:
        if dim % c == 0:
            return c
    return dim


def _pick_k(K, cap=2816):
    if K <= cap or K % 128:
        return K
    best = 128
    for m in range(1, K // 128 + 1):
        if (K // 128) % m == 0 and 128 * m <= cap:
            best = 128 * m
    return best


def _params(sem):
    return pltpu.CompilerParams(dimension_semantics=sem, vmem_limit_bytes=V7X_VMEM_LIMIT)


def _all_gather(x_shard, name):
    m_per, n = x_shard.shape

    def body(x_ref, out_ref, send_sems, recv_sems, local_sem):
        x, y, c = lax.axis_index("x"), lax.axis_index("y"), lax.axis_index("c")
        me, sibling = (x, y, c), (x, y, 1 - c)
        chips = [(1 - x, y), (x, 1 - y), (1 - x, 1 - y)]

        def rows(px, py, pc):
            return out_ref.at[pl.ds((4 * px + 2 * py + pc) * m_per, m_per), :]

        def copy(k, block, to, src=None):
            return pltpu.make_async_remote_copy(
                src_ref=rows(*block) if src is None else src, dst_ref=rows(*block),
                send_sem=send_sems.at[k], recv_sem=recv_sems.at[k], device_id=to, device_id_type=MESH)

        mine = pltpu.make_async_copy(x_ref, rows(*me), local_sem)
        mine.start()
        first = [copy(0, me, sibling, src=x_ref)]
        first += [copy(1 + j, me, (*chip, c), src=x_ref) for j, chip in enumerate(chips)]
        for cp in first:
            cp.start()
        passed = [copy(4 + j, (*chip, c), sibling) for j, chip in enumerate(chips)]
        for j, chip in enumerate(chips):
            copy(1 + j, (*chip, c), me).wait_recv()
            passed[j].start()
        copy(0, sibling, me).wait_recv()
        for j, chip in enumerate(chips):
            copy(4 + j, (*chip, 1 - c), me).wait_recv()
        for cp in first + passed:
            cp.wait_send()
        mine.wait()

    return pl.pallas_call(
        body, name=name,
        out_shape=jax.ShapeDtypeStruct((N_DEV * m_per, n), x_shard.dtype),
        in_specs=[pl.BlockSpec(memory_space=pltpu.VMEM)],
        out_specs=pl.BlockSpec(memory_space=pltpu.VMEM),
        scratch_shapes=[pltpu.SemaphoreType.DMA((7,)), pltpu.SemaphoreType.DMA((7,)), pltpu.SemaphoreType.DMA],
    )(x_shard)


def _offsets(piece_rows):
    offs, o = [], 0
    for n in piece_rows:
        offs.append(o)
        o += n
    return offs


HBM_SPEC = pl.BlockSpec(memory_space=pltpu.HBM)
SEM_SPEC = pl.BlockSpec(memory_space=pltpu.SEMAPHORE)
ANY_SPEC = pl.BlockSpec(memory_space=pl.ANY)
SPLIT_COPY_PARAMS = pltpu.CompilerParams(has_side_effects=pltpu.SideEffectType.DATAFLOW_SIDE_EFFECTING)


def _in_hbm(a):
    return pltpu.with_memory_space_constraint(a, pltpu.HBM)


def _dma_sems(n):
    return [pltpu.SemaphoreType.DMA(())] * n


def _whole(ref, send_sem, recv_sem, me):
    return pltpu.make_async_remote_copy(src_ref=ref, dst_ref=ref, send_sem=send_sem, recv_sem=recv_sem,
                                        device_id=me, device_id_type=MESH)


def _gather_start(packed, piece_rows, after, name):
    R, w = packed.shape
    offs = _offsets(piece_rows)
    P = len(piece_rows)
    assert offs[-1] + piece_rows[-1] == R

    def body(*refs):
        src_ref = refs[0]
        o = refs[P + 2:]
        send, recv = o[0:4], o[4:8]
        zones, token, stage, local_sems = o[9:9 + P], o[9 + P], o[10 + P], o[11 + P]
        x, y, c = lax.axis_index("x"), lax.axis_index("y"), lax.axis_index("c")
        targets = [(x, y, 1 - c), (1 - x, y, c), (x, 1 - y, c), (1 - x, 1 - y, c)]
        me = 4 * x + 2 * y + c

        def piece(p, ref):
            return ref.at[pl.ds(offs[p], piece_rows[p]), :]

        def rows(p):
            return zones[p].at[pl.ds(me * piece_rows[p], piece_rows[p]), :]

        for k, to in enumerate(targets):
            for p in range(P):
                pltpu.make_async_remote_copy(src_ref=piece(p, src_ref), dst_ref=rows(p), send_sem=send[k],
                                             recv_sem=recv[k], device_id=to, device_id_type=MESH).start()
        load = pltpu.make_async_copy(src_ref, stage, local_sems.at[P])
        load.start()
        load.wait()
        mine = [pltpu.make_async_copy(piece(p, stage), rows(p), local_sems.at[p]) for p in range(P)]
        for cp in mine:
            cp.start()
        for cp in mine:
            cp.wait()
        token[...] = jnp.zeros_like(token)

    zones_in = [_in_hbm(lax.empty((N_DEV * n, w), packed.dtype)) for n in piece_rows]
    outs = pl.pallas_call(
        body, name=name,
        out_shape=(*_dma_sems(8), pltpu.HBM((R, w), packed.dtype),
                   *[pltpu.HBM((N_DEV * n, w), packed.dtype) for n in piece_rows],
                   jax.ShapeDtypeStruct((8, 128), F32)),
        in_specs=[HBM_SPEC] * (P + 1) + [ANY_SPEC],
        out_specs=[SEM_SPEC] * 8 + [HBM_SPEC] * (P + 1) + [pl.BlockSpec(memory_space=pltpu.VMEM)],
        input_output_aliases={0: 8, **{1 + p: 9 + p for p in range(P)}},
        scratch_shapes=[pltpu.VMEM((R, w), packed.dtype), pltpu.SemaphoreType.DMA((P + 1,))],
        compiler_params=SPLIT_COPY_PARAMS,
    )(_in_hbm(packed), *zones_in, after)
    return outs[0:8], outs[8], list(outs[9:9 + P]), outs[9 + P]


def _gather_forward(sems, packed, zones, piece_rows, after, name):
    P = len(piece_rows)

    def body(*refs):
        src_ref = refs[0]
        s = refs[1 + P:9 + P]
        o = refs[10 + P:]
        send, recv = s[0:4], s[4:8]
        send2, recv2, zones_o = o[0:3], o[3:6], o[7:7 + P]
        x, y, c = lax.axis_index("x"), lax.axis_index("y"), lax.axis_index("c")
        me = (x, y, c)
        chips = [(1 - x, y), (x, 1 - y), (1 - x, 1 - y)]
        for j, (px, py) in enumerate(chips):
            _whole(src_ref, send[1 + j], recv[1 + j], me).wait_recv()
            blk = 4 * px + 2 * py + c
            for p in range(P):
                r = zones_o[p].at[pl.ds(blk * piece_rows[p], piece_rows[p]), :]
                pltpu.make_async_remote_copy(src_ref=r, dst_ref=r, send_sem=send2[j], recv_sem=recv2[j],
                                             device_id=(x, y, 1 - c), device_id_type=MESH).start()
        _whole(src_ref, send[0], recv[0], me).wait_recv()
        for k in range(4):
            _whole(src_ref, send[k], recv[k], me).wait_send()
        o[7 + P][...] = jnp.zeros_like(o[7 + P])

    outs = pl.pallas_call(
        body, name=name,
        out_shape=(*_dma_sems(6), pltpu.HBM(packed.shape, packed.dtype),
                   *[pltpu.HBM(z.shape, z.dtype) for z in zones], jax.ShapeDtypeStruct((8, 128), F32)),
        in_specs=[HBM_SPEC] * (P + 1) + [SEM_SPEC] * 8 + [ANY_SPEC],
        out_specs=[SEM_SPEC] * 6 + [HBM_SPEC] * (P + 1) + [pl.BlockSpec(memory_space=pltpu.VMEM)],
        input_output_aliases={0: 6, **{1 + p: 7 + p for p in range(P)}},
        compiler_params=SPLIT_COPY_PARAMS,
    )(packed, *zones, *sems, after)
    return outs[0:6], outs[6], list(outs[7:7 + P]), outs[7 + P]


def _gather_finish(sems2, packed, zones, after, name):
    P = len(zones)

    def body(*refs):
        src_ref = refs[0]
        s = refs[1 + P:7 + P]
        x, y, c = lax.axis_index("x"), lax.axis_index("y"), lax.axis_index("c")
        for j in range(3):
            _whole(src_ref, s[j], s[3 + j], (x, y, c)).wait_recv()
        for j in range(3):
            _whole(src_ref, s[j], s[3 + j], (x, y, c)).wait_send()

    outs = pl.pallas_call(
        body, name=name,
        out_shape=(pltpu.HBM(packed.shape, packed.dtype), *[pltpu.HBM(z.shape, z.dtype) for z in zones]),
        in_specs=[HBM_SPEC] * (P + 1) + [SEM_SPEC] * 6 + [ANY_SPEC],
        out_specs=[HBM_SPEC] * (P + 1),
        input_output_aliases={p: p for p in range(P + 1)},
        compiler_params=SPLIT_COPY_PARAMS,
    )(packed, *zones, *sems2, after)
    return list(outs[1:1 + P])


def _scatter_start(pieces, after, name):
    P = len(pieces)
    w = pieces[0].shape[1]
    piece_rows = [p.shape[0] // N_DEV for p in pieces]
    offs = _offsets(piece_rows)
    R = offs[-1] + piece_rows[-1]

    def body(*refs):
        o = refs[P + 2:]
        send, recv = o[0:7], o[7:14]
        srcs, dst_ref, token, stage, local_sems = o[14:14 + P], o[14 + P], o[15 + P], o[16 + P], o[17 + P]
        x, y, c = lax.axis_index("x"), lax.axis_index("y"), lax.axis_index("c")
        me = 4 * x + 2 * y + c

        def chunk(p, dev):
            return srcs[p].at[pl.ds(dev * piece_rows[p], piece_rows[p]), :]

        def slot(p, dev):
            return dst_ref.at[dev, pl.ds(offs[p], piece_rows[p]), :]

        for k in range(1, N_DEV):
            px = 1 - x if (k >> 2) & 1 else x
            py = 1 - y if (k >> 1) & 1 else y
            pc = 1 - c if k & 1 else c
            peer = 4 * px + 2 * py + pc
            for p in range(P):
                pltpu.make_async_remote_copy(
                    src_ref=chunk(p, peer), dst_ref=slot(p, me), send_sem=send[k - 1], recv_sem=recv[k - 1],
                    device_id=(px, py, pc), device_id_type=MESH).start()
        mine = [pltpu.make_async_copy(chunk(p, me), stage.at[pl.ds(offs[p], piece_rows[p]), :], local_sems.at[p])
                for p in range(P)]
        for cp in mine:
            cp.start()
        for cp in mine:
            cp.wait()
        store = pltpu.make_async_copy(stage, dst_ref.at[me], local_sems.at[P])
        store.start()
        store.wait()
        token[...] = jnp.zeros_like(token)

    dtype = pieces[0].dtype
    outs = pl.pallas_call(
        body, name=name,
        out_shape=(*_dma_sems(14), *[pltpu.HBM(p.shape, dtype) for p in pieces], pltpu.HBM((N_DEV, R, w), dtype),
                   jax.ShapeDtypeStruct((8, 128), F32)),
        in_specs=[HBM_SPEC] * (P + 1) + [ANY_SPEC],
        out_specs=[SEM_SPEC] * 14 + [HBM_SPEC] * (P + 1) + [pl.BlockSpec(memory_space=pltpu.VMEM)],
        input_output_aliases={p: 14 + p for p in range(P + 1)},
        scratch_shapes=[pltpu.VMEM((R, w), dtype), pltpu.SemaphoreType.DMA((P + 1,))],
        compiler_params=SPLIT_COPY_PARAMS,
    )(*[_in_hbm(p) for p in pieces], _in_hbm(lax.empty((N_DEV, R, w), dtype)), after)
    return outs[0:14], list(outs[14:14 + P]), outs[14 + P], outs[15 + P]


def _scatter_finish(sems, pieces, recv, after, name):
    P = len(pieces)
    after = list(after) if isinstance(after, (list, tuple)) else [after]

    def body(*refs):
        dst_ref = refs[P]
        s = refs[P + 1:P + 15]
        x, y, c = lax.axis_index("x"), lax.axis_index("y"), lax.axis_index("c")
        for k in range(7):
            _whole(dst_ref.at[0], s[k], s[7 + k], (x, y, c)).wait_recv()
        for k in range(7):
            _whole(dst_ref.at[0], s[k], s[7 + k], (x, y, c)).wait_send()

    outs = pl.pallas_call(
        body, name=name,
        out_shape=(*[pltpu.HBM(p.shape, p.dtype) for p in pieces], pltpu.HBM(recv.shape, recv.dtype)),
        in_specs=[HBM_SPEC] * (P + 1) + [SEM_SPEC] * 14 + [ANY_SPEC] * len(after),
        out_specs=[HBM_SPEC] * (P + 1),
        input_output_aliases={p: p for p in range(P + 1)},
        compiler_params=SPLIT_COPY_PARAMS,
    )(*pieces, recv, *sems, *after)
    return outs[P]


def _sum_sources(parts, name):
    _, r, n = parts.shape
    tr = _pick(r, [256, 192, 128, 96, 64, 32, 16, 8])

    def kern(p_ref, o_ref):
        acc = p_ref[0].astype(F32)
        for k in range(1, N_DEV):
            acc = acc + p_ref[k].astype(F32)
        o_ref[...] = acc

    return pl.pallas_call(
        kern, name=name, grid=(r // tr,),
        out_shape=jax.ShapeDtypeStruct((r, n), F32),
        in_specs=[pl.BlockSpec((N_DEV, tr, n), lambda i: (0, i, 0))],
        out_specs=pl.BlockSpec((tr, n), lambda i: (i, 0)),
        compiler_params=_params(("parallel",)),
    )(parts)


def _matmul(a, b, mode, out_dtype, name, tm=None, tn=None, tk=None, resid=None):
    if mode == "nn":
        (M, K), N = a.shape, b.shape[1]
    elif mode == "nt":
        (M, K), N = a.shape, b.shape[0]
    else:
        (K, M), N = a.shape, b.shape[1]
    dims = {"nn": NN, "nt": NT, "tn": TN}[mode]
    tm = tm or _pick(M, [1024, 1408, 512, 256, 128])
    tn = tn or _pick(N, [1024, 1408, 512, 256, 128])
    tk = tk or _pick_k(K)
    nk = K // tk
    a_spec = {"nn": pl.BlockSpec((tm, tk), lambda i, j, k: (i, k)),
              "nt": pl.BlockSpec((tm, tk), lambda i, j, k: (i, k)),
              "tn": pl.BlockSpec((tk, tm), lambda i, j, k: (k, i))}[mode]
    b_spec = {"nn": pl.BlockSpec((tk, tn), lambda i, j, k: (k, j)),
              "nt": pl.BlockSpec((tn, tk), lambda i, j, k: (j, k)),
              "tn": pl.BlockSpec((tk, tn), lambda i, j, k: (k, j))}[mode]
    o_spec = pl.BlockSpec((tm, tn), lambda i, j, k: (i, j))
    n_in = 2 if resid is None else 4
    n_out = 1 if resid is None else 2

    def kern(*refs):
        a_ref, b_ref = refs[0], refs[1]
        outs = refs[n_in:n_in + n_out]
        acc_ref = refs[n_in + n_out] if nk > 1 else None

        def finish(acc):
            if resid is None:
                outs[0][...] = acc.astype(out_dtype)
            else:
                x_ref, g_ref = refs[2], refs[3]
                outs[0][...] = x_ref[...] + (resid[2] * g_ref[...]) * acc
                outs[1][...] = acc.astype(out_dtype)

        part = lax.dot_general(a_ref[...], b_ref[...], dims, preferred_element_type=F32)
        if nk == 1:
            finish(part)
        else:
            k = pl.program_id(2)

            @pl.when(k == 0)
            def _():
                acc_ref[...] = part

            @pl.when(k > 0)
            def _():
                acc_ref[...] += part

            @pl.when(k == nk - 1)
            def _():
                finish(acc_ref[...])

    in_specs = [a_spec, b_spec]
    args = [a, b]
    out_shape = [jax.ShapeDtypeStruct((M, N), out_dtype)]
    out_specs = [o_spec]
    if resid is not None:
        in_specs += [o_spec, pl.BlockSpec((1, tn), lambda i, j, k: (0, j))]
        args += [resid[0], resid[1]]
        out_shape = [jax.ShapeDtypeStruct((M, N), F32)] + out_shape
        out_specs = [o_spec, o_spec]
    res = pl.pallas_call(
        kern, name=name, grid=(M // tm, N // tn, nk),
        out_shape=out_shape, in_specs=in_specs, out_specs=out_specs,
        scratch_shapes=[pltpu.VMEM((tm, tn), F32)] if nk > 1 else [],
        compiler_params=_params(("parallel", "parallel", "arbitrary")),
    )(*args)
    return res[0] if resid is None else res


def _dot3(a, b, dims):
    ah = a.astype(BF16)
    al = (a - ah.astype(F32)).astype(BF16)
    bh = b.astype(BF16)
    bl = (b - bh.astype(F32)).astype(BF16)
    d = functools.partial(lax.dot_general, dimension_numbers=dims, preferred_element_type=F32)
    return d(ah, bh) + (d(ah, bl) + d(al, bh))


def _silu_parts(a):
    sg = jax.nn.sigmoid(a)
    return a * sg, sg * (1.0 + a * (1.0 - sg))


def _ffn_up(h, wg_t, wu_t, name):
    S, D = h.shape
    F = wg_t.shape[0]
    tm = _pick(S, [1024, 512, 256, 128])
    tn = _pick(F, [1408, 512, 256, 128])

    def kern(h_ref, g_ref, u_ref, a_out, u_out, z_out):
        hv = h_ref[...]
        a = lax.dot_general(hv, g_ref[...], NT, preferred_element_type=F32)
        u = lax.dot_general(hv, u_ref[...], NT, preferred_element_type=F32)
        a_out[...] = a.astype(BF16)
        u_out[...] = u.astype(BF16)
        z_out[...] = (_silu_parts(a)[0] * u).astype(BF16)

    w_spec = pl.BlockSpec((tn, D), lambda j, i: (j, 0))
    o_spec = pl.BlockSpec((tm, tn), lambda j, i: (i, j))
    return pl.pallas_call(
        kern, name=name, grid=(F // tn, S // tm),
        out_shape=[jax.ShapeDtypeStruct((S, F), BF16)] * 3,
        in_specs=[pl.BlockSpec((tm, D), lambda j, i: (i, 0)), w_spec, w_spec],
        out_specs=[o_spec] * 3,
        compiler_params=_params(("parallel", "parallel")),
    )(h, wg_t, wu_t)


def _ffn_up_bwd(dz, a, u, wg_t, wu_t, name):
    S, F = dz.shape
    D = wg_t.shape[1]
    tm = _pick(S, [512, 256, 128])
    tk = _pick(F, [1408, 512, 256, 128])
    nk = F // tk

    def kern(dz_ref, a_ref, u_ref, g_ref, w_ref, da_out, du_out, dh_out, acc_ref):
        k = pl.program_id(1)
        av = a_ref[...].astype(F32)
        uv = u_ref[...].astype(F32)
        dzv = dz_ref[...].astype(F32)
        silu, dsilu = _silu_parts(av)
        da = (dzv * uv * dsilu).astype(BF16)
        du = (dzv * silu).astype(BF16)
        da_out[...] = da
        du_out[...] = du
        part = (lax.dot_general(da, g_ref[...], NN, preferred_element_type=F32)
                + lax.dot_general(du, w_ref[...], NN, preferred_element_type=F32))

        @pl.when(k == 0)
        def _():
            acc_ref[...] = part

        @pl.when(k > 0)
        def _():
            acc_ref[...] += part

        @pl.when(k == nk - 1)
        def _():
            dh_out[...] = acc_ref[...].astype(BF16)

    t_spec = pl.BlockSpec((tm, tk), lambda i, k: (i, k))
    w_spec = pl.BlockSpec((tk, D), lambda i, k: (k, 0))
    return pl.pallas_call(
        kern, name=name, grid=(S // tm, nk),
        out_shape=[jax.ShapeDtypeStruct((S, F), BF16)] * 2 + [jax.ShapeDtypeStruct((S, D), BF16)],
        in_specs=[t_spec, t_spec, t_spec, w_spec, w_spec],
        out_specs=[t_spec, t_spec, pl.BlockSpec((tm, D), lambda i, k: (i, 0))],
        scratch_shapes=[pltpu.VMEM((tm, D), F32)],
        compiler_params=_params(("parallel", "arbitrary")),
    )(dz, a, u, wg_t, wu_t)


def _attn_dh(dq, dkv, w_t, name):
    S = dq.shape[0]
    D = w_t.shape[1]
    tm = _pick(S, [1024, 512, 256, 128])

    def kern(dq_ref, dk_ref, dv_ref, wq_ref, wk_ref, wv_ref, o_ref):
        o_ref[...] = (lax.dot_general(dq_ref[...], wq_ref[...], NN, preferred_element_type=F32)
                      + lax.dot_general(dk_ref[...], wk_ref[...], NN, preferred_element_type=F32)
                      + lax.dot_general(dv_ref[...], wv_ref[...], NN, preferred_element_type=F32))

    def w_blk(j):
        return pl.BlockSpec((ATTN_OUT, D), lambda i: (j, 0))

    return pl.pallas_call(
        kern, name=name, grid=(S // tm,),
        out_shape=jax.ShapeDtypeStruct((S, D), F32),
        in_specs=[pl.BlockSpec((tm, ATTN_OUT), lambda i: (i, 0)), pl.BlockSpec((tm, ATTN_OUT), lambda i: (i, 0)),
                  pl.BlockSpec((tm, ATTN_OUT), lambda i: (i, 1)), w_blk(0), w_blk(1), w_blk(2)],
        out_specs=pl.BlockSpec((tm, D), lambda i: (i, 0)),
        compiler_params=_params(("parallel",)),
    )(dq, dkv, dkv, w_t, w_t, w_t)


def _row_spec(tm, d):
    return pl.BlockSpec((tm, d), lambda i: (i, 0))


def _vec_spec(d, rows=1):
    return pl.BlockSpec((rows, d), lambda i: (0, 0))


def _perm_spec(dil, tm, w):
    return pl.BlockSpec((dil, tm // dil, w), lambda i: (0, i, 0))


def _stage_shape(tm, w):
    return pltpu.VMEM((w // 128, tm, 128), F32)


def _stage(scr, val):
    for ci in range(scr.shape[0]):
        scr[ci] = val[:, 128 * ci:128 * (ci + 1)]


def _unstage(scr):
    return jnp.concatenate([scr[ci] for ci in range(scr.shape[0])], axis=1)


def _get_residue(scr, res, dil):
    n = scr.shape[1] // dil
    return jnp.concatenate([scr[ci, pl.ds(res, n, stride=dil), :] for ci in range(scr.shape[0])], axis=1)


def _put_residue(scr, res, dil, val):
    n = scr.shape[1] // dil
    for ci in range(scr.shape[0]):
        scr[ci, pl.ds(res, n, stride=dil), :] = val[:, 128 * ci:128 * (ci + 1)]


def _norm_mod_fwd(x, g, s, b, name, dils=()):
    S, D = x.shape
    tm = _pick(S, [512, 256, 128])

    def kern(x_ref, g_ref, s_ref, b_ref, h_ref, *rest):
        xv = x_ref[...]
        r = lax.rsqrt(jnp.mean(xv * xv, axis=1, keepdims=True) + EPS)
        hv = xv * r * g_ref[...] * (1.0 + s_ref[...]) + b_ref[...]
        h_ref[...] = hv.astype(BF16)
        if dils:
            scr = rest[len(dils)]
            _stage(scr, hv)
            for dil, p_ref in zip(dils, rest[:len(dils)]):
                for res in range(dil):
                    p_ref[res] = _get_residue(scr, res, dil).astype(BF16)

    return pl.pallas_call(
        kern, name=name, grid=(S // tm,),
        out_shape=[jax.ShapeDtypeStruct((S, D), BF16)] + [jax.ShapeDtypeStruct((dil, S // dil, D), BF16) for dil in dils],
        in_specs=[_row_spec(tm, D), _vec_spec(D), _vec_spec(D), _vec_spec(D)],
        out_specs=[_row_spec(tm, D)] + [_perm_spec(dil, tm, D) for dil in dils],
        scratch_shapes=[_stage_shape(tm, D)] if dils else [],
        compiler_params=_params(("parallel",)),
    )(x, g, s, b)


def _gate_part(dx, nxt_refs, coef, df_ref, cs_ref):
    f_ref, gate_ref = nxt_refs
    df_ref[...] = ((coef * gate_ref[...]) * dx).astype(BF16)
    cs_ref[4:5, :] += coef * jnp.sum(f_ref[...].astype(F32) * dx, axis=0, keepdims=True)


def _norm_mod_bwd(x, dh_nat, dh_perm, dxo, g, s, name, nxt=None):
    S, D = x.shape
    tm = _pick(S, [512, 256, 128])
    n = S // tm
    n_nat, n_perm = len(dh_nat), len(dh_perm)
    n_nxt = 0 if nxt is None else 2

    def kern(*refs):
        x_ref = refs[0]
        nat = refs[1:1 + n_nat]
        perm = refs[1 + n_nat:1 + n_nat + n_perm]
        base = 1 + n_nat + n_perm
        dxo_ref, g_ref, s_ref = refs[base:base + 3]
        nxt_refs = refs[base + 3:base + 3 + n_nxt]
        dx_ref, cs_ref = refs[base + 3 + n_nxt:base + 5 + n_nxt]
        rest = refs[base + 5 + n_nxt:]
        df_ref = rest[0] if nxt is not None else None
        scr = rest[1:] if nxt is not None else rest
        i = pl.program_id(0)
        xv = x_ref[...]
        r = lax.rsqrt(jnp.mean(xv * xv, axis=1, keepdims=True) + EPS)
        xn = xv * r
        dh_v = nat[0][...].astype(F32)
        for t in nat[1:]:
            dh_v = dh_v + t[...].astype(F32)
        for (dil, _), p_ref, sc in zip(dh_perm, perm, scr):
            for res in range(dil):
                _put_residue(sc, res, dil, p_ref[res])
            dh_v = dh_v + _unstage(sc)
        one_s = 1.0 + s_ref[...]
        dxn = dh_v * (g_ref[...] * one_s)
        dx = dxo_ref[...] + r * (dxn - xn * jnp.mean(xn * dxn, axis=1, keepdims=True))
        dx_ref[...] = dx

        @pl.when(i == 0)
        def _():
            cs_ref[...] = jnp.zeros_like(cs_ref)

        cs_ref[0:1, :] += jnp.sum(dh_v, axis=0, keepdims=True)
        cs_ref[1:2, :] += jnp.sum(dh_v * xn, axis=0, keepdims=True)
        if nxt is not None:
            _gate_part(dx, nxt_refs, nxt[2], df_ref, cs_ref)

        @pl.when(i == n - 1)
        def _():
            t = cs_ref[1:2, :]
            cs_ref[2:3, :] = g_ref[...] * t
            cs_ref[3:4, :] = one_s * t

    nxt_specs = [] if nxt is None else [_row_spec(tm, D), _vec_spec(D)]
    nxt_args = [] if nxt is None else [nxt[0], nxt[1]]
    return pl.pallas_call(
        kern, name=name, grid=(n,),
        out_shape=[jax.ShapeDtypeStruct((S, D), F32), jax.ShapeDtypeStruct((8, D), F32)]
        + ([] if nxt is None else [jax.ShapeDtypeStruct((S, D), BF16)]),
        in_specs=[_row_spec(tm, D)] + [_row_spec(tm, D)] * n_nat + [_perm_spec(dil, tm, D) for dil, _ in dh_perm]
        + [_row_spec(tm, D), _vec_spec(D), _vec_spec(D)] + nxt_specs,
        out_specs=[_row_spec(tm, D), _vec_spec(D, 8)] + ([] if nxt is None else [_row_spec(tm, D)]),
        scratch_shapes=[_stage_shape(tm, D) for _ in dh_perm],
        compiler_params=_params(("arbitrary",)),
    )(x, *dh_nat, *[a for _, a in dh_perm], dxo, g, s, *nxt_args)


def _loss_head(x, g, target, nxt, name):
    S, D = x.shape
    tm = _pick(S, [512, 256, 128])
    n = S // tm

    def kern(x_ref, g_ref, t_ref, f_ref, gate_ref, dx_ref, cs_ref, df_ref):
        i = pl.program_id(0)
        xv = x_ref[...]
        r = lax.rsqrt(jnp.mean(xv * xv, axis=1, keepdims=True) + EPS)
        xn = xv * r
        e = xn * g_ref[...] - t_ref[...]
        dxn = (e * (1.0 / D)) * g_ref[...]
        dx = r * (dxn - xn * jnp.mean(xn * dxn, axis=1, keepdims=True))
        dx_ref[...] = dx

        @pl.when(i == 0)
        def _():
            cs_ref[...] = jnp.zeros_like(cs_ref)

        cs_ref[0:1, :] += jnp.sum(xn * e, axis=0, keepdims=True) * (1.0 / D)
        cs_ref[1:2, :] += jnp.sum(e * e, axis=0, keepdims=True)
        _gate_part(dx, (f_ref, gate_ref), nxt[2], df_ref, cs_ref)

        @pl.when(i == n - 1)
        def _():
            tot = jnp.sum(cs_ref[1:2, :], axis=1, keepdims=True) * (0.5 / D)
            cs_ref[2:3, :] = jnp.broadcast_to(tot, (1, D))

    return pl.pallas_call(
        kern, name=name, grid=(n,),
        out_shape=[jax.ShapeDtypeStruct((S, D), F32), jax.ShapeDtypeStruct((8, D), F32),
                   jax.ShapeDtypeStruct((S, D), BF16)],
        in_specs=[_row_spec(tm, D), _vec_spec(D), _row_spec(tm, D), _row_spec(tm, D), _vec_spec(D)],
        out_specs=[_row_spec(tm, D), _vec_spec(D, 8), _row_spec(tm, D)],
        compiler_params=_params(("arbitrary",)),
    )(x, g, target, nxt[0], nxt[1])


def _shift_down(p, row, prev_rows):
    a, b = prev_rows
    p1 = jnp.where(row == 0, b, pltpu.roll(p, 1, 0))
    p2 = jnp.where(row == 0, a, jnp.where(row == 1, b, pltpu.roll(p, 2, 0)))
    return p1, p2


def _conv_fwd(cg, conv_w, name):
    S, D5 = cg.shape
    D = D5 // 5
    tm = _pick(S, [512, 256, 128])
    t8 = tm // 8

    def prev(col):
        return pl.BlockSpec((8, D), lambda i: (jnp.maximum(i * t8 - 1, 0), col))

    def kern(cb_ref, cc_ref, ch_ref, ccp_ref, chp_ref, w_ref, y_ref):
        i = pl.program_id(0)
        keep = jnp.where(i > 0, 1.0, 0.0)
        p = cc_ref[...].astype(F32) * ch_ref[...].astype(F32)
        pa = ccp_ref[6:7, :].astype(F32) * chp_ref[6:7, :].astype(F32) * keep
        pb = ccp_ref[7:8, :].astype(F32) * chp_ref[7:8, :].astype(F32) * keep
        row = lax.broadcasted_iota(jnp.int32, (tm, D), 0)
        p1, p2 = _shift_down(p, row, (pa, pb))
        dw = w_ref[0:1, :] * p2 + w_ref[1:2, :] * p1 + w_ref[2:3, :] * p
        y_ref[...] = (cb_ref[...].astype(F32) * dw).astype(BF16)

    def col(cidx):
        return pl.BlockSpec((tm, D), lambda i: (i, cidx))

    return pl.pallas_call(
        kern, name=name, grid=(S // tm,),
        out_shape=jax.ShapeDtypeStruct((S, D), BF16),
        in_specs=[col(0), col(1), col(2), prev(1), prev(2), _vec_spec(D, CONV_K)],
        out_specs=_row_spec(tm, D),
        compiler_params=_params(("parallel",)),
    )(cg, cg, cg, cg, cg, conv_w)


def _conv_bwd(cg, dy, dgg, conv_w, name):
    S, D5 = cg.shape
    D = D5 // 5
    tm = _pick(S, [512, 256, 128])
    t8 = tm // 8
    n = S // tm
    last8 = S // 8 - 1

    def prev(col):
        return pl.BlockSpec((8, D), lambda i: (jnp.maximum(i * t8 - 1, 0), col))

    def nxt(col):
        return pl.BlockSpec((8, D), lambda i: (jnp.minimum((i + 1) * t8, last8), col))

    def kern(cb_ref, cc_ref, ch_ref, dy_ref, dgg_ref, ccp_ref, chp_ref, cbn_ref, dyn_ref, w_ref, d_ref, cs_ref):
        i = pl.program_id(0)
        keep_p = jnp.where(i > 0, 1.0, 0.0)
        keep_n = jnp.where(i < n - 1, 1.0, 0.0)
        cb = cb_ref[...].astype(F32)
        cc = cc_ref[...].astype(F32)
        ch = ch_ref[...].astype(F32)
        dyv = dy_ref[...].astype(F32)
        p = cc * ch
        pa = ccp_ref[6:7, :].astype(F32) * chp_ref[6:7, :].astype(F32) * keep_p
        pb = ccp_ref[7:8, :].astype(F32) * chp_ref[7:8, :].astype(F32) * keep_p
        row = lax.broadcasted_iota(jnp.int32, (tm, D), 0)
        p1, p2 = _shift_down(p, row, (pa, pb))
        w0, w1, w2 = w_ref[0:1, :], w_ref[1:2, :], w_ref[2:3, :]
        dw = w0 * p2 + w1 * p1 + w2 * p
        ddw = dyv * cb
        na = dyn_ref[0:1, :].astype(F32) * cbn_ref[0:1, :].astype(F32) * keep_n
        nb = dyn_ref[1:2, :].astype(F32) * cbn_ref[1:2, :].astype(F32) * keep_n
        u1 = jnp.where(row == tm - 1, na, pltpu.roll(ddw, tm - 1, 0))
        u2 = jnp.where(row == tm - 2, na, jnp.where(row == tm - 1, nb, pltpu.roll(ddw, tm - 2, 0)))
        dp = w2 * ddw + w1 * u1 + w0 * u2
        d_ref[:, 0:D] = (dyv * dw).astype(BF16)
        d_ref[:, D:2 * D] = (dp * ch).astype(BF16)
        d_ref[:, 2 * D:3 * D] = (dp * cc).astype(BF16)
        d_ref[:, 3 * D:5 * D] = dgg_ref[...]

        @pl.when(i == 0)
        def _():
            cs_ref[...] = jnp.zeros_like(cs_ref)

        cs_ref[0:1, :] += jnp.sum(ddw * p2, axis=0, keepdims=True)
        cs_ref[1:2, :] += jnp.sum(ddw * p1, axis=0, keepdims=True)
        cs_ref[2:3, :] += jnp.sum(ddw * p, axis=0, keepdims=True)

    def col(cidx):
        return pl.BlockSpec((tm, D), lambda i: (i, cidx))

    return pl.pallas_call(
        kern, name=name, grid=(n,),
        out_shape=[jax.ShapeDtypeStruct((S, 5 * D), BF16), jax.ShapeDtypeStruct((8, D), F32)],
        in_specs=[col(0), col(1), col(2), _row_spec(tm, D), _row_spec(tm, 2 * D), prev(1), prev(2), nxt(0),
                  pl.BlockSpec((8, D), lambda i: (jnp.minimum((i + 1) * t8, last8), 0)), _vec_spec(D, CONV_K)],
        out_specs=[_row_spec(tm, 5 * D), _vec_spec(D, 8)],
        compiler_params=_params(("arbitrary",)),
    )(cg, cg, cg, dy, dgg, cg, cg, cg, dy, conv_w)


def _merge_fwd(cg, yc, ya, name):
    S, D = yc.shape
    tm = _pick(S, [512, 256, 128])

    def kern(gc_ref, ga_ref, yc_ref, ya_ref, m_ref):
        m_ref[...] = (jax.nn.sigmoid(gc_ref[...].astype(F32)) * yc_ref[...].astype(F32)
                      + jax.nn.sigmoid(ga_ref[...].astype(F32)) * ya_ref[...].astype(F32)).astype(BF16)

    return pl.pallas_call(
        kern, name=name, grid=(S // tm,),
        out_shape=jax.ShapeDtypeStruct((S, D), BF16),
        in_specs=[pl.BlockSpec((tm, D), lambda i: (i, 3)), pl.BlockSpec((tm, D), lambda i: (i, 4)),
                  _row_spec(tm, D), _row_spec(tm, D)],
        out_specs=_row_spec(tm, D),
        compiler_params=_params(("parallel",)),
    )(cg, cg, yc, ya)


def _merge_bwd(cg, yc, ya, dm, name):
    S, D = yc.shape
    tm = _pick(S, [512, 256, 128])

    def kern(gc_ref, ga_ref, yc_ref, ya_ref, dm_ref, dyc_ref, dya_ref, dg_ref):
        dmv = dm_ref[...].astype(F32)
        sc = jax.nn.sigmoid(gc_ref[...].astype(F32))
        sa = jax.nn.sigmoid(ga_ref[...].astype(F32))
        dyc_ref[...] = (dmv * sc).astype(BF16)
        dya_ref[...] = (dmv * sa).astype(BF16)
        dg_ref[:, 0:D] = (dmv * yc_ref[...].astype(F32) * (sc * (1.0 - sc))).astype(BF16)
        dg_ref[:, D:2 * D] = (dmv * ya_ref[...].astype(F32) * (sa * (1.0 - sa))).astype(BF16)

    return pl.pallas_call(
        kern, name=name, grid=(S // tm,),
        out_shape=[jax.ShapeDtypeStruct((S, D), BF16), jax.ShapeDtypeStruct((S, D), BF16),
                   jax.ShapeDtypeStruct((S, 2 * D), BF16)],
        in_specs=[pl.BlockSpec((tm, D), lambda i: (i, 3)), pl.BlockSpec((tm, D), lambda i: (i, 4)),
                  _row_spec(tm, D), _row_spec(tm, D), _row_spec(tm, D)],
        out_specs=[_row_spec(tm, D), _row_spec(tm, D), pl.BlockSpec((tm, 2 * D), lambda i: (i, 0))],
        compiler_params=_params(("parallel",)),
    )(cg, cg, yc, ya, dm)


def _t5_bucket(dist):
    exact = NUM_BUCKETS // 2
    d = np.maximum(dist, 1).astype(np.float32)
    large = exact + (np.log(d / exact) / np.log(MAX_DISTANCE / exact) * (NUM_BUCKETS - exact)).astype(np.int32)
    large = np.minimum(large, NUM_BUCKETS - 1)
    return np.where(dist < exact, dist, large).astype(np.int32)


def _bucket_tables():
    i = np.arange(BLOCK)[:, None]
    j = np.arange(2 * BLOCK)[None, :]
    rel = i - j + BLOCK
    return np.stack([_t5_bucket(np.maximum(rel, 0) * d) for _, d in DILATION_GROUPS]).astype(np.int32)


def _band_masks():
    i = lax.broadcasted_iota(jnp.int32, (BLOCK, 2 * BLOCK), 0)
    j = lax.broadcasted_iota(jnp.int32, (BLOCK, 2 * BLOCK), 1)
    rel = i - j + BLOCK
    band = (rel >= 0) & (rel <= BLOCK)
    return band, band & (j >= BLOCK)


def _bias_build(rel_bias, buckets, name):
    def kern(rb_ref, bk_ref, o_ref):
        g = pl.program_id(0)
        bk = bk_ref[0]
        band, first = _band_masks()
        for h in range(HEADS_PER_GROUP):
            acc = jnp.zeros((BLOCK, 2 * BLOCK), F32)
            for b in range(NUM_BUCKETS):
                acc = jnp.where(bk == b, rb_ref[b, g * HEADS_PER_GROUP + h], acc)
            o_ref[0, 0, h] = jnp.where(first, acc, NEG_INF)
            o_ref[0, 1, h] = jnp.where(band, acc, NEG_INF)

    return pl.pallas_call(
        kern, name=name, grid=(N_GROUPS,),
        out_shape=jax.ShapeDtypeStruct((N_GROUPS, 2, HEADS_PER_GROUP, BLOCK, 2 * BLOCK), F32),
        in_specs=[pl.BlockSpec(memory_space=pltpu.SMEM),
                  pl.BlockSpec((1, BLOCK, 2 * BLOCK), lambda g: (g, 0, 0))],
        out_specs=pl.BlockSpec((1, 2, HEADS_PER_GROUP, BLOCK, 2 * BLOCK), lambda g: (g, 0, 0, 0, 0)),
        compiler_params=_params(("parallel",)),
    )(rel_bias, buckets)


def _bias_bwd(dlog, buckets, name):
    def kern(dl_ref, bk_ref, o_ref):
        g = pl.program_id(0)
        bk = bk_ref[0]
        rowi = lax.broadcasted_iota(jnp.int32, (NUM_BUCKETS, 128), 0)
        coli = lax.broadcasted_iota(jnp.int32, (NUM_BUCKETS, 128), 1)

        @pl.when(g == 0)
        def _():
            o_ref[...] = jnp.zeros_like(o_ref)

        acc = jnp.zeros((NUM_BUCKETS, 128), F32)
        for h in range(HEADS_PER_GROUP):
            dv = dl_ref[0, h]
            for b in range(NUM_BUCKETS):
                t = jnp.sum(jnp.where(bk == b, dv, 0.0), axis=0, keepdims=True)
                t = jnp.sum(t, axis=1, keepdims=True)
                acc = acc + jnp.where((rowi == b) & (coli == g * HEADS_PER_GROUP + h), t, 0.0)
        o_ref[...] += acc

    return pl.pallas_call(
        kern, name=name, grid=(N_GROUPS,),
        out_shape=jax.ShapeDtypeStruct((NUM_BUCKETS, 128), F32),
        in_specs=[pl.BlockSpec((1, HEADS_PER_GROUP, BLOCK, 2 * BLOCK), lambda g: (g, 0, 0, 0)),
                  pl.BlockSpec((1, BLOCK, 2 * BLOCK), lambda g: (g, 0, 0))],
        out_specs=pl.BlockSpec((NUM_BUCKETS, 128), lambda g: (0, 0)),
        compiler_params=_params(("arbitrary",)),
    )(dlog, buckets)


def _head_masks():
    lane = lax.broadcasted_iota(jnp.int32, (BLOCK, 128), 1)
    lo = lane < HEAD_DIM
    return lo, jnp.logical_not(lo)


def _attn_fwd(qkv, bias, d, name):
    S = qkv.shape[0]
    nb = S // d // BLOCK

    def kern(q_ref, kp_ref, kc_ref, vp_ref, vc_ref, b_ref, o_ref, lse_ref):
        lo, hi = _head_masks()
        for p in range(HEADS_PER_GROUP // 2):
            sl = slice(128 * p, 128 * (p + 1))
            q = q_ref[:, sl]
            k = jnp.concatenate([kp_ref[:, sl], kc_ref[:, sl]], axis=0)
            v = jnp.concatenate([vp_ref[:, sl], vc_ref[:, sl]], axis=0)
            zero = jnp.zeros_like(q)
            q2 = jnp.concatenate([jnp.where(lo, q, zero), jnp.where(hi, q, zero)], axis=0)
            b2 = jnp.concatenate([b_ref[0, 2 * p], b_ref[0, 2 * p + 1]], axis=0)
            s = lax.dot_general(q2, k, NT, preferred_element_type=F32) * SCALE + b2
            m = jnp.max(s, axis=1, keepdims=True)
            e = jnp.exp(s - m)
            l = jnp.sum(e, axis=1, keepdims=True)
            o2 = lax.dot_general(e.astype(BF16), v, NN, preferred_element_type=F32) / l
            l2 = jnp.broadcast_to(m + jnp.log(l), (2 * BLOCK, 128))
            o_ref[:, sl] = jnp.where(lo, o2[0:BLOCK], o2[BLOCK:2 * BLOCK])
            lse_ref[:, sl] = jnp.where(lo, l2[0:BLOCK], l2[BLOCK:2 * BLOCK])

    def blk(col, prev):
        if prev:
            return pl.BlockSpec((BLOCK, ATTN_OUT), lambda r, n: (r * nb + jnp.maximum(n - 1, 0), col))
        return pl.BlockSpec((BLOCK, ATTN_OUT), lambda r, n: (r * nb + n, col))

    o_spec = pl.BlockSpec((BLOCK, ATTN_OUT), lambda r, n: (r * nb + n, 0))
    return pl.pallas_call(
        kern, name=name, grid=(d, nb),
        out_shape=[jax.ShapeDtypeStruct((S, ATTN_OUT), F32)] * 2,
        in_specs=[blk(0, False), blk(1, True), blk(1, False), blk(2, True), blk(2, False),
                  pl.BlockSpec((1, HEADS_PER_GROUP, BLOCK, 2 * BLOCK), lambda r, n: (jnp.minimum(n, 1), 0, 0, 0))],
        out_specs=[o_spec, o_spec],
        compiler_params=_params(("parallel", "arbitrary")),
    )(qkv, qkv, qkv, qkv, qkv, bias)


def _attn_bwd(qkv, do, lse, delta, bias, d, name):
    S = qkv.shape[0]
    nb = S // d // BLOCK
    low = -3.0e38

    def kern(q_ref, kp_ref, kc_ref, vp_ref, vc_ref, do_ref, lse_ref, dl_ref, b_ref,
             dq_ref, dkv_ref, db_ref, ck_ref, cv_ref):
        r, n = pl.program_id(0), pl.program_id(1)

        @pl.when((r == 0) & (n == 0))
        def _():
            db_ref[...] = jnp.zeros_like(db_ref)

        @pl.when(n == 0)
        def _():
            ck_ref[...] = jnp.zeros_like(ck_ref)
            cv_ref[...] = jnp.zeros_like(cv_ref)

        @pl.when(n < nb)
        def _():
            lo, hi = _head_masks()
            for p in range(HEADS_PER_GROUP // 2):
                sl = slice(128 * p, 128 * (p + 1))
                sv = slice(ATTN_OUT + 128 * p, ATTN_OUT + 128 * (p + 1))
                q = q_ref[:, sl]
                k = jnp.concatenate([kp_ref[:, sl], kc_ref[:, sl]], axis=0)
                v = jnp.concatenate([vp_ref[:, sl], vc_ref[:, sl]], axis=0)
                dov = do_ref[:, sl]
                lse_b = lse_ref[:, sl]
                del_b = dl_ref[:, sl]
                zero = jnp.zeros_like(q)
                q2 = jnp.concatenate([jnp.where(lo, q, zero), jnp.where(hi, q, zero)], axis=0)
                do2 = jnp.concatenate([jnp.where(lo, dov, zero), jnp.where(hi, dov, zero)], axis=0)
                lse2 = jnp.concatenate([jnp.max(jnp.where(msk, lse_b, low), axis=1, keepdims=True) for msk in (lo, hi)], axis=0)
                del2 = jnp.concatenate([jnp.max(jnp.where(msk, del_b, low), axis=1, keepdims=True) for msk in (lo, hi)], axis=0)
                b2 = jnp.concatenate([b_ref[0, 2 * p], b_ref[0, 2 * p + 1]], axis=0)
                s = lax.dot_general(q2, k, NT, preferred_element_type=F32) * SCALE + b2
                pr = jnp.exp(s - lse2)
                dp = lax.dot_general(do2, v, NT, preferred_element_type=F32)
                ds = pr * (dp - del2)
                db_ref[2 * p] += ds[0:BLOCK]
                db_ref[2 * p + 1] += ds[BLOCK:2 * BLOCK]
                dsb = (ds * SCALE).astype(BF16)
                dq2 = lax.dot_general(dsb, k, NN, preferred_element_type=F32)
                dk_acc = lax.dot_general(dsb, q2, TN, preferred_element_type=F32)
                dv_acc = lax.dot_general(pr.astype(BF16), do2, TN, preferred_element_type=F32)
                dq_ref[:, sl] = jnp.where(lo, dq2[0:BLOCK], dq2[BLOCK:2 * BLOCK]).astype(BF16)
                dkv_ref[:, sl] = (ck_ref[:, sl] + dk_acc[0:BLOCK]).astype(BF16)
                dkv_ref[:, sv] = (cv_ref[:, sl] + dv_acc[0:BLOCK]).astype(BF16)
                ck_ref[:, sl] = dk_acc[BLOCK:2 * BLOCK]
                cv_ref[:, sl] = dv_acc[BLOCK:2 * BLOCK]

        @pl.when(n == nb)
        def _():
            dkv_ref[:, 0:ATTN_OUT] = ck_ref[...].astype(BF16)
            dkv_ref[:, ATTN_OUT:2 * ATTN_OUT] = cv_ref[...].astype(BF16)

    def cur(n):
        return jnp.minimum(n, nb - 1)

    def blk(col, prev):
        if prev:
            return pl.BlockSpec((BLOCK, ATTN_OUT), lambda r, n: (r * nb + jnp.maximum(cur(n) - 1, 0), col))
        return pl.BlockSpec((BLOCK, ATTN_OUT), lambda r, n: (r * nb + cur(n), col))

    q_like = pl.BlockSpec((BLOCK, ATTN_OUT), lambda r, n: (r * nb + cur(n), 0))
    return pl.pallas_call(
        kern, name=name, grid=(d, nb + 1),
        out_shape=[jax.ShapeDtypeStruct((S, ATTN_OUT), BF16), jax.ShapeDtypeStruct((S, 2 * ATTN_OUT), BF16),
                   jax.ShapeDtypeStruct((HEADS_PER_GROUP, BLOCK, 2 * BLOCK), F32)],
        in_specs=[blk(0, False), blk(1, True), blk(1, False), blk(2, True), blk(2, False),
                  q_like, q_like, q_like,
                  pl.BlockSpec((1, HEADS_PER_GROUP, BLOCK, 2 * BLOCK),
                               lambda r, n: (jnp.minimum(cur(n), 1), 0, 0, 0))],
        out_specs=[q_like,
                   pl.BlockSpec((BLOCK, 2 * ATTN_OUT), lambda r, n: (r * nb + jnp.maximum(n - 1, 0), 0)),
                   pl.BlockSpec((HEADS_PER_GROUP, BLOCK, 2 * BLOCK), lambda r, n: (0, 0, 0))],
        scratch_shapes=[pltpu.VMEM((BLOCK, ATTN_OUT), F32), pltpu.VMEM((BLOCK, ATTN_OUT), F32)],
        compiler_params=_params(("arbitrary", "arbitrary")),
    )(qkv, qkv, qkv, qkv, qkv, do, lse, delta, bias)


def _by_residue(a, dil):
    return a if dil == 1 else a.reshape(dil, a.shape[0] // dil, a.shape[1])


def _flat(a):
    return a if a.ndim == 2 else a.reshape(a.shape[0] * a.shape[1], a.shape[2])


def _combine_fwd(os_, lses, name):
    S, W = os_[0].shape
    tm = _pick(S, [512, 256, 128])
    perm = [dil for dil in DILS if dil > 1]

    def kern(*refs):
        o_in, l_in = refs[0:N_GROUPS], refs[N_GROUPS:2 * N_GROUPS]
        of_ref, ob_ref, lse_ref = refs[2 * N_GROUPS:2 * N_GROUPS + 3]
        lse_p = refs[2 * N_GROUPS + 3:2 * N_GROUPS + 3 + len(perm)]
        scr = refs[2 * N_GROUPS + 3 + len(perm):]
        ov, lv = [], []
        si = 0
        for g, dil in enumerate(DILS):
            if dil == 1:
                ov.append(o_in[g][...])
                lv.append(l_in[g][...])
            else:
                so, sl = scr[si], scr[si + 1]
                si += 2
                for res in range(dil):
                    _put_residue(so, res, dil, o_in[g][res])
                    _put_residue(sl, res, dil, l_in[g][res])
                ov.append(_unstage(so))
                lv.append(_unstage(sl))
        m = jnp.maximum(jnp.maximum(lv[0], lv[1]), lv[2])
        e = [jnp.exp(t - m) for t in lv]
        tot = e[0] + e[1] + e[2]
        o = (e[0] * ov[0] + e[1] * ov[1] + e[2] * ov[2]) / tot
        lse = m + jnp.log(tot)
        of_ref[...] = o
        ob_ref[...] = o.astype(BF16)
        lse_ref[...] = lse
        sl = scr[1]
        _stage(sl, lse)
        for dil, p_ref in zip(perm, lse_p):
            for res in range(dil):
                p_ref[res] = _get_residue(sl, res, dil)

    def in_spec(dil):
        return _row_spec(tm, W) if dil == 1 else _perm_spec(dil, tm, W)

    ins = [_by_residue(a, dil) for a, dil in zip(os_, DILS)] + [_by_residue(a, dil) for a, dil in zip(lses, DILS)]
    return pl.pallas_call(
        kern, name=name, grid=(S // tm,),
        out_shape=[jax.ShapeDtypeStruct((S, W), F32), jax.ShapeDtypeStruct((S, W), BF16),
                   jax.ShapeDtypeStruct((S, W), F32)]
        + [jax.ShapeDtypeStruct((dil, S // dil, W), F32) for dil in perm],
        in_specs=[in_spec(dil) for dil in DILS] * 2,
        out_specs=[_row_spec(tm, W)] * 3 + [_perm_spec(dil, tm, W) for dil in perm],
        scratch_shapes=[_stage_shape(tm, W) for _ in range(2 * len(perm))],
        compiler_params=_params(("parallel",)),
    )(*ins)


def _delta(do, o, name):
    S, W = o.shape
    tm = _pick(S, [512, 256, 128])
    perm = [dil for dil in DILS if dil > 1]

    def kern(do_ref, o_ref, dob_ref, d_ref, *rest):
        scr, scr_do = rest[2 * len(perm)], rest[2 * len(perm) + 1]
        prod = do_ref[...] * o_ref[...]
        ri = jnp.right_shift(lax.broadcasted_iota(jnp.int32, (W, W), 0), HEAD_SHIFT)
        ci = jnp.right_shift(lax.broadcasted_iota(jnp.int32, (W, W), 1), HEAD_SHIFT)
        same = jnp.where(ri == ci, 1.0, 0.0).astype(BF16)
        hi_p = prod.astype(BF16)
        lo_p = (prod - hi_p.astype(F32)).astype(BF16)
        dl = (lax.dot_general(hi_p, same, NN, preferred_element_type=F32)
              + lax.dot_general(lo_p, same, NN, preferred_element_type=F32))
        d_ref[...] = dl
        dob_ref[...] = do_ref[...].astype(BF16)
        _stage(scr, dl)
        _stage(scr_do, do_ref[...])
        for j, dil in enumerate(perm):
            for res in range(dil):
                rest[2 * j][res] = _get_residue(scr_do, res, dil).astype(BF16)
                rest[2 * j + 1][res] = _get_residue(scr, res, dil)

    out_shape = [jax.ShapeDtypeStruct((S, W), BF16), jax.ShapeDtypeStruct((S, W), F32)]
    out_specs = [_row_spec(tm, W), _row_spec(tm, W)]
    for dil in perm:
        out_shape += [jax.ShapeDtypeStruct((dil, S // dil, W), BF16), jax.ShapeDtypeStruct((dil, S // dil, W), F32)]
        out_specs += [_perm_spec(dil, tm, W), _perm_spec(dil, tm, W)]
    return pl.pallas_call(
        kern, name=name, grid=(S // tm,),
        out_shape=out_shape,
        in_specs=[_row_spec(tm, W), _row_spec(tm, W)], out_specs=out_specs,
        scratch_shapes=[_stage_shape(tm, W), _stage_shape(tm, W)],
        compiler_params=_params(("parallel",)),
    )(do, o)


def _ada_fwd(c16, ada_w, name):
    depth, D, n = ada_w.shape
    rows = 2 * N_DEV

    def kern(c_ref, w_ref, o_ref, cs_ref):
        cv = c_ref[...]
        cs = cv * jax.nn.sigmoid(cv)
        cs_ref[...] = cs
        o_ref[0] = _dot3(cs, w_ref[0], NN)

    return pl.pallas_call(
        kern, name=name, grid=(depth,),
        out_shape=[jax.ShapeDtypeStruct((depth, rows, n), F32), jax.ShapeDtypeStruct((rows, D), F32)],
        in_specs=[pl.BlockSpec((rows, D), lambda l: (0, 0)), pl.BlockSpec((1, D, n), lambda l: (l, 0, 0))],
        out_specs=[pl.BlockSpec((1, rows, n), lambda l: (l, 0, 0)), pl.BlockSpec((rows, D), lambda l: (0, 0))],
        compiler_params=_params(("arbitrary",)),
    )(c16, ada_w)


def _ada_bwd(cs16, dm16, name):
    depth, _, n = dm16.shape
    D = cs16.shape[1]

    def kern(cs_ref, dm_ref, o_ref):
        o_ref[0] = _dot3(cs_ref[...], dm_ref[0], TN)

    return pl.pallas_call(
        kern, name=name, grid=(depth,),
        out_shape=jax.ShapeDtypeStruct((depth, D, n), F32),
        in_specs=[pl.BlockSpec((2 * N_DEV, D), lambda l: (0, 0)), pl.BlockSpec((1, 2 * N_DEV, n), lambda l: (l, 0, 0))],
        out_specs=pl.BlockSpec((1, D, n), lambda l: (l, 0, 0)),
        compiler_params=_params(("parallel",)),
    )(cs16, dm16)


def _sum_rows8(parts, name):
    _, r, n = parts.shape

    def kern(p_ref, o_ref):
        acc = p_ref[0]
        for k in range(1, N_DEV):
            acc = acc + p_ref[k]
        o_ref[...] = acc

    return pl.pallas_call(
        kern, name=name, out_shape=jax.ShapeDtypeStruct((r, n), F32),
        in_specs=[pl.BlockSpec(memory_space=pltpu.VMEM)], out_specs=pl.BlockSpec(memory_space=pltpu.VMEM),
    )(parts)


def _adamw(w, g, m, v, name):
    shape = w.shape
    c = shape[-1]
    r = int(np.prod(shape[:-1])) if len(shape) > 1 else 1
    w2, g2, m2, v2 = (t.reshape(r, c) for t in (w, g, m, v))
    tr = r
    for cand in (2048, 1024, 512, 256, 128, 64, 32, 16, 8):
        if r % cand == 0 and cand * c * 4 <= (1 << 20):
            tr = cand
            break
    c1 = 1.0 - ADAM_B1 ** ADAM_STEP
    c2 = 1.0 - ADAM_B2 ** ADAM_STEP

    def kern(w_ref, g_ref, m_ref, v_ref, d_ref, nm_ref, nv_ref):
        gv = g_ref[...]
        nm = ADAM_B1 * m_ref[...] + (1.0 - ADAM_B1) * gv
        nv = ADAM_B2 * v_ref[...] + (1.0 - ADAM_B2) * (gv * gv)
        nm_ref[...] = nm
        nv_ref[...] = nv
        d_ref[...] = -ADAM_LR * ((nm / c1) / (jnp.sqrt(nv / c2) + ADAM_EPS) + ADAM_WD * w_ref[...])

    spec = pl.BlockSpec((tr, c), lambda i: (i, 0))
    outs = pl.pallas_call(
        kern, name=name, grid=(r // tr,),
        out_shape=[jax.ShapeDtypeStruct((r, c), F32)] * 3,
        in_specs=[spec] * 4, out_specs=[spec] * 3,
        compiler_params=_params(("parallel",)),
    )(w2, g2, m2, v2)
    return tuple(o.reshape(shape) for o in outs)


def _adamw_slab(w3, g, m3, v3, idx, prev, name):
    ns, r, c = w3.shape
    tr = r
    for k in range(1, r + 1):
        if r % k == 0 and (r // k) % 8 == 0 and (r // k) * c * 4 <= 3 * (1 << 19):
            tr = r // k
            break
    c1 = 1.0 - ADAM_B1 ** ADAM_STEP
    c2 = 1.0 - ADAM_B2 ** ADAM_STEP

    def kern(w_ref, g_ref, m_ref, v_ref, p0, p1, p2, p3, go_ref, d_ref, nm_ref, nv_ref):
        gv = g_ref[...]
        nm = ADAM_B1 * m_ref[0] + (1.0 - ADAM_B1) * gv
        nv = ADAM_B2 * v_ref[0] + (1.0 - ADAM_B2) * (gv * gv)
        go_ref[0] = gv
        nm_ref[0] = nm
        nv_ref[0] = nv
        d_ref[0] = -ADAM_LR * ((nm / c1) / (jnp.sqrt(nv / c2) + ADAM_EPS) + ADAM_WD * w_ref[0])

    if prev is None:
        prev = [lax.empty((ns, r, c), F32) for _ in range(4)]
    slab = pl.BlockSpec((1, tr, c), lambda i: (idx, i, 0))
    return pl.pallas_call(
        kern, name=name, grid=(r // tr,),
        out_shape=[jax.ShapeDtypeStruct((ns, r, c), F32)] * 4,
        in_specs=[slab, pl.BlockSpec((tr, c), lambda i: (i, 0)), slab, slab] + [pl.BlockSpec(memory_space=pl.ANY)] * 4,
        out_specs=[slab] * 4,
        input_output_aliases={4: 0, 5: 1, 6: 2, 7: 3},
        compiler_params=_params(("parallel",)),
    )(w3, g, m3, v3, *prev)


def kernel(x, c, ada_w, ada_b, norm_g, ffn_w_gate, ffn_w_up, ffn_w_down, w_in, conv_w, w_conv_out, w_attn_out, w_o, rel_bias, final_g, loss_target, m_ada_w, m_ada_b, m_norm_g, m_ffn_w_gate, m_ffn_w_up, m_ffn_w_down, m_w_in, m_conv_w, m_w_conv_out, m_w_attn_out, m_w_o, m_rel_bias, m_final_g, v_ada_w, v_ada_b, v_norm_g, v_ffn_w_gate, v_ffn_w_up, v_ffn_w_down, v_w_in, v_conv_w, v_w_conv_out, v_w_attn_out, v_w_o, v_rel_bias, v_final_g):
    depth = ada_w.shape[0]
    S, D = x.shape[1], x.shape[2]
    me = 4 * lax.axis_index("x") + 2 * lax.axis_index("y") + lax.axis_index("c")
    x0 = x.reshape(S, D)
    target = loss_target.reshape(S, D)
    fsh = ffn_w_down.shape[2]
    insh = w_in.shape[2]
    dsh = D // N_DEV
    ao_rows = dsh * ATTN_OUT // D

    piece_rows = [fsh] * 6 + [insh, dsh, dsh, ao_rows]

    FIRST, MIXER, SECOND = [0, 2, 4], [6, 7, 8, 9], [1, 3, 5]

    def rows_of(idx):
        return [piece_rows[p] for p in idx]

    def pack(l, idx=None):
        def t(a):
            return jnp.transpose(a).astype(BF16)
        ps = [t(ffn_w_gate[l, 0]), t(ffn_w_gate[l, 1]), t(ffn_w_up[l, 0]), t(ffn_w_up[l, 1]),
              ffn_w_down[l, 0].astype(BF16), ffn_w_down[l, 1].astype(BF16), t(w_in[l]),
              w_conv_out[l].astype(BF16), w_o[l].astype(BF16), t(w_attn_out[l]).reshape(ao_rows, D)]
        return jnp.concatenate(ps if idx is None else [ps[p] for p in idx], axis=0)

    def mixer_weights(full):
        in_t = full[6]
        qkv_t = [jnp.concatenate([in_t[t * QKV_W + g * ATTN_OUT: t * QKV_W + (g + 1) * ATTN_OUT] for t in range(3)])
                 for g in range(N_GROUPS)]
        ao_t = full[9].reshape(N_DEV, dsh, ATTN_OUT).reshape(D, ATTN_OUT)
        return dict(qkv_t=qkv_t, cg_t=in_t[3 * QKV_W:], co=full[7], wo=full[8], ao_t=ao_t)

    def behind(v, token):
        return v + token[0, 0].astype(v.dtype)

    c_all = _all_gather(c.reshape(D // 128, 128), "c_all_gather").reshape(N_DEV, D)
    c16 = jnp.concatenate([c_all, jnp.zeros_like(c_all)], axis=0)
    mod_part, cs16 = _ada_fwd(c16, ada_w, "ada_fwd")
    mod_part = mod_part[:, :N_DEV]
    n_ada = ada_w.shape[2]
    mod_all = _all_gather(mod_part.reshape(depth * N_DEV * n_ada // 128, 128), "mod_all_gather")
    mod_all = mod_all.reshape(N_DEV, depth, N_DEV, n_ada)
    mod_mine = lax.dynamic_index_in_dim(mod_all, me, axis=2, keepdims=False)
    mod = jnp.transpose(mod_mine, (1, 0, 2)).reshape(depth, N_DEV * n_ada) + ada_b
    mod = mod.reshape(depth, 3, 3, 1, D)

    small = jnp.concatenate([norm_g.reshape(-1), conv_w.reshape(-1)]).reshape(-1, 128)
    small_all = _all_gather(small, "small_all_gather").reshape(N_DEV, -1)
    n_ng = norm_g.size
    norm_g_full = jnp.transpose(small_all[:, :n_ng].reshape(N_DEV, depth, 3, dsh), (1, 2, 0, 3)).reshape(depth, 3, 1, D)
    conv_w_full = jnp.transpose(small_all[:, n_ng:].reshape(N_DEV, depth, CONV_K, dsh), (1, 2, 0, 3)).reshape(depth, CONV_K, D)

    buckets = jnp.asarray(_bucket_tables())
    bias = _bias_build(rel_bias, buckets, "bias_build")
    perm_dils = tuple(dil for dil in DILS if dil > 1)

    chain_done = mod.reshape(-1)[:128] + small_all.reshape(-1)[:128]
    part0 = [(FIRST, "first"), (MIXER, "mixer"), (SECOND, "second")]
    started, after0 = [], chain_done
    for idx, tag in part0:
        started.append(_gather_start(pack(0, idx), rows_of(idx), after0, f"weights_gather_start_l0_{tag}"))
        after0 = started[-1][3]
    full0 = [None] * len(piece_rows)

    def arrive0(k, after_arr):
        idx, tag = part0[k]
        st = started[k]
        fw = _gather_forward(st[0], st[1], st[2], rows_of(idx), after_arr, f"weights_gather_forward_l0_{tag}")
        for p, z in zip(idx, _gather_finish(fw[0], fw[1], fw[2], after_arr, f"weights_gather_finish_l0_{tag}")):
            full0[p] = z
        return fw[3]

    arrive0(0, bias)
    W = [dict(g_t=[full0[0]], u_t=[full0[2]], down=[full0[4]])] + [None] * (depth - 1)

    saved = []
    xc = x0
    for l in range(depth):
        sv = {}
        gather, tie_sub, pin = None, 0, None
        if 0 < l < depth - 1:
            gather = _gather_start(pack(l + 1), piece_rows, W[l]["wo"], f"weights_gather_start_l{l + 1}")
        for sub in (0, 1, 2):
            if l == 0 and sub == 1:
                arrive0(1, xc)
                W[0].update(mixer_weights(full0))
                if depth > 1:
                    gather, tie_sub = _gather_start(pack(1), piece_rows, W[0]["wo"], "weights_gather_start_l1"), 1
            if l == 0 and sub == 2:
                arrive0(2, xc)
                W[0].update(g_t=full0[0:2], u_t=full0[2:4], down=full0[4:6])
            g, sh, sc, gt = norm_g_full[l, sub], mod[l, sub, 0], mod[l, sub, 1], mod[l, sub, 2]
            if sub == tie_sub and gather is not None:
                g = behind(g, gather[3])
            if l == 0 and sub == 0:
                g = behind(g, started[-1][3])
            if sub == 2 and pin is not None:
                g = behind(g, pin)
            rec = dict(x=xc)
            if sub != 1:
                i = 0 if sub == 0 else 1
                h = _norm_mod_fwd(xc, g, sc, sh, "norm_mod_fwd")[0]
                a, u, z = _ffn_up(h, W[l]["g_t"][i], W[l]["u_t"][i], "ffn_up")
                xc, f = _matmul(z, W[l]["down"][i], "nn", BF16, "ffn_down", tm=512, resid=(xc, gt, 0.5))
                rec.update(h=h, a=a, u=u, z=z, f=f)
            else:
                hs = _norm_mod_fwd(xc, g, sc, sh, "norm_mod_fwd_mixer", dils=perm_dils)
                h = hs[0]
                h_res = [h] + [_flat(t) for t in hs[1:]]
                cg = _matmul(h, W[l]["cg_t"], "nt", BF16, "mixer_cg")
                qkvs, os_, lses = [], [], []
                for gi, dil in enumerate(DILS):
                    qkv = _matmul(h_res[gi], W[l]["qkv_t"][gi], "nt", BF16, "mixer_qkv", tn=3 * ATTN_OUT)
                    o_g, lse_g = _attn_fwd(qkv, bias[gi], dil, f"attn_fwd_g{gi}")
                    qkvs.append(qkv)
                    os_.append(o_g)
                    lses.append(lse_g)
                comb = _combine_fwd(os_, lses, "combine_fwd")
                o_f, o_b, lse = comb[0:3]
                lse_res = [lse] + [_flat(t) for t in comb[3:]]
                yc_in = _conv_fwd(cg, conv_w_full[l], "conv_fwd")
                yc = _matmul(yc_in, W[l]["co"], "nn", BF16, "conv_out")
                ya = _matmul(o_b, W[l]["ao_t"], "nt", BF16, "attn_out")
                merged = _merge_fwd(cg, yc, ya, "merge_fwd")
                xc, f = _matmul(merged, W[l]["wo"], "nn", BF16, "mixer_out", resid=(xc, gt, 1.0))
                rec.update(h=h, h_res=h_res, qkvs=qkvs, cg=cg, o_f=o_f, o_b=o_b, lse_res=lse_res, yc_in=yc_in,
                           yc=yc, ya=ya, merged=merged, f=f)
                if gather is not None:
                    fwd = _gather_forward(gather[0], gather[1], gather[2], piece_rows, xc,
                                          f"weights_gather_forward_l{l + 1}")
                    pin = fwd[3]
            sv[sub] = rec
        if gather is not None:
            full = _gather_finish(fwd[0], fwd[1], fwd[2], xc, f"weights_gather_finish_l{l + 1}")
            W[l + 1] = dict(g_t=full[0:2], u_t=full[2:4], down=full[4:6], **mixer_weights(full))
        saved.append(sv)

    def gate_of(l, sub):
        return saved[l][sub]["f"], mod[l, sub, 2], (1.0 if sub == 1 else 0.5)

    def below(l, sub):
        if sub > 0:
            return gate_of(l, sub - 1)
        return gate_of(l - 1, 2) if l > 0 else None

    dx, head, df = _loss_head(xc, final_g.reshape(1, D), target, gate_of(depth - 1, 2), "loss_head")
    d_final_g = head[0]
    loss_part = head[2, 0]
    d_gate = head[4]

    d_mod = [[None] * 3 for _ in range(depth)]
    d_norm = [[None] * 3 for _ in range(depth)]
    d_conv = [None] * depth
    dlog = jnp.zeros((N_GROUPS, HEADS_PER_GROUP, BLOCK, 2 * BLOCK), F32)
    n_pieces = len(piece_rows)
    LATE = (0, 2, 4)
    EARLY = tuple(p for p in range(n_pieces) if p not in LATE)
    g_piece = [[None] * n_pieces for _ in range(depth)]

    piece_keys = (("g_t", 0), ("g_t", 1), ("u_t", 0), ("u_t", 1), ("down", 0), ("down", 1), "in_t", "co", "wo", "ao_t")

    def pieces_of(dW, idx):
        return [dW[piece_keys[p]].reshape(N_DEV * piece_rows[p], D) for p in idx]

    def finish_scatter(sc, idx, after_arr, layer, part=""):
        recv = _scatter_finish(sc[0], sc[1], sc[2], after_arr, f"grads_scatter_finish_l{layer}{part}")
        tot = _sum_sources(recv, "grads_sum")
        o = 0
        for p in idx:
            g_piece[layer][p] = tot[o:o + piece_rows[p]]
            o += piece_rows[p]
        return recv

    scatter = None
    early = None
    for l in reversed(range(depth)):
        dW = {}
        for sub in (2, 1, 0):
            rec = saved[l][sub]
            g, sc = norm_g_full[l, sub], mod[l, sub, 1]
            if sub == 2 and scatter is not None:
                df = behind(df, scatter[3])
            if sub == 0 and early is not None:
                df = behind(df, early[3])
            nxt = below(l, sub)
            if sub != 1:
                i = 0 if sub == 0 else 1
                dz = _matmul(df, W[l]["down"][i], "nt", BF16, "ffn_down_dx")
                dW["down", i] = _matmul(rec["z"], df, "tn", BF16, "ffn_down_dw")
                da, du, dh = _ffn_up_bwd(dz, rec["a"], rec["u"], W[l]["g_t"][i], W[l]["u_t"][i], "ffn_up_bwd")
                dW["g_t", i] = _matmul(da, rec["h"], "tn", BF16, "ffn_gate_dw")
                dW["u_t", i] = _matmul(du, rec["h"], "tn", BF16, "ffn_up_dw")
                res = _norm_mod_bwd(rec["x"], [dh], [], dx, g, sc, "norm_mod_bwd", nxt=nxt)
            else:
                dout = df
                dm = _matmul(dout, W[l]["wo"], "nt", BF16, "mixer_out_dx")
                dW["wo"] = _matmul(rec["merged"], dout, "tn", BF16, "mixer_out_dw")
                dyc, dya, dgg = _merge_bwd(rec["cg"], rec["yc"], rec["ya"], dm, "merge_bwd")
                dyc_in = _matmul(dyc, W[l]["co"], "nt", BF16, "conv_out_dx")
                dW["co"] = _matmul(rec["yc_in"], dyc, "tn", BF16, "conv_out_dw")
                do = _matmul(dya, W[l]["ao_t"], "nn", F32, "attn_out_dx")
                dW["ao_t"] = _matmul(dya, rec["o_b"], "tn", BF16, "attn_out_dw")
                dl = _delta(do, rec["o_f"], "attn_delta")
                do_res = [dl[0]] + [_flat(t) for t in dl[2::2]]
                del_res = [dl[1]] + [_flat(t) for t in dl[3::2]]
                dh_attn, dw_q, dw_kv, dlog_l = [], [], [], []
                for gi, dil in enumerate(DILS):
                    dq, dkv, dlg = _attn_bwd(rec["qkvs"][gi], do_res[gi], rec["lse_res"][gi], del_res[gi],
                                             bias[gi], dil, f"attn_bwd_g{gi}")
                    dlog_l.append(dlg)
                    dh_attn.append(_attn_dh(dq, dkv, W[l]["qkv_t"][gi], "attn_dh"))
                    dw_q.append(_matmul(dq, rec["h_res"][gi], "tn", BF16, "mixer_q_dw"))
                    dw_kv.append(_matmul(dkv, rec["h_res"][gi], "tn", BF16, "mixer_kv_dw"))
                dlog = dlog + jnp.stack(dlog_l)
                dcg, conv_sum = _conv_bwd(rec["cg"], dyc_in, dgg, conv_w_full[l], "conv_bwd")
                d_conv[l] = conv_sum[0:CONV_K]
                dh_cg = _matmul(dcg, W[l]["cg_t"], "nn", BF16, "mixer_cg_dx")
                dw_cg = _matmul(dcg, rec["h"], "tn", BF16, "mixer_cg_dw")
                dW["in_t"] = jnp.concatenate(
                    dw_q + [t[:ATTN_OUT] for t in dw_kv] + [t[ATTN_OUT:] for t in dw_kv] + [dw_cg], axis=0)
                perm_parts = [(dil, _by_residue(dh_attn[gi], dil)) for gi, dil in enumerate(DILS) if dil > 1]
                res = _norm_mod_bwd(rec["x"], [dh_cg, dh_attn[0]], perm_parts, dx, g, sc, "norm_mod_bwd_mixer", nxt=nxt)
            dx, sums = res[0], res[1]
            d_mod[l][sub] = jnp.stack([sums[0], sums[2], d_gate])
            d_norm[l][sub] = sums[3]
            if nxt is not None:
                df, d_gate = res[2], sums[4]
            if l == 0 and sub == 1:
                after = dx
                if scatter is not None:
                    after = finish_scatter(scatter, range(n_pieces), dx, l + 1)
                    scatter = None
                early = _scatter_start(pieces_of(dW, EARLY), after, "grads_scatter_start_l0_early")
        if l > 0:
            after = dx
            if scatter is not None:
                after = finish_scatter(scatter, range(n_pieces), dx, l + 1)
            scatter = _scatter_start(pieces_of(dW, range(n_pieces)), after, f"grads_scatter_start_l{l}")
        else:
            late = _scatter_start(pieces_of(dW, LATE), dx, "grads_scatter_start_l0_late")
    grad_x = dx.reshape(1, S, D)
    d_rel = _bias_bwd(dlog, buckets, "bias_bwd")[:, :rel_bias.shape[1]]

    big = dict(
        ffn_w_gate=(ffn_w_gate, m_ffn_w_gate, v_ffn_w_gate, (0, 1), "t_shard"),
        ffn_w_up=(ffn_w_up, m_ffn_w_up, v_ffn_w_up, (2, 3), "t_shard"),
        ffn_w_down=(ffn_w_down, m_ffn_w_down, v_ffn_w_down, (4, 5), "rows"),
        w_in=(w_in, m_w_in, v_w_in, (6,), "t_shard"),
        w_conv_out=(w_conv_out, m_w_conv_out, v_w_conv_out, (7,), "rows"),
        w_o=(w_o, m_w_o, v_w_o, (8,), "rows"),
        w_attn_out=(w_attn_out, m_w_attn_out, v_w_attn_out, (9,), "t_grad"))
    big_out = {name: None for name in big}

    def adam_layer(l, token=None):
        for name, (w_, m_, v_, plist, how) in big.items():
            if how == "t_shard":
                w_, m_, v_ = (jnp.swapaxes(t, -1, -2) for t in (w_, m_, v_))
            ns, r, cc = depth * len(plist), w_.shape[-2], w_.shape[-1]
            w3, m3, v3 = (t.reshape(ns, r, cc) for t in (w_, m_, v_))
            for j, p in enumerate(plist):
                gl = g_piece[l][p]
                if how == "t_grad":
                    gl = jnp.transpose(gl.reshape(cc, r))
                if token is not None:
                    gl = behind(gl, token)
                big_out[name] = _adamw_slab(w3, gl, m3, v3, l * len(plist) + j, big_out[name], "adamw_" + name)

    for l in range(depth - 1, 0, -1):
        adam_layer(l, late[3])
    done = [late[3]] + [st[1] for st in big_out.values() if st is not None]
    finish_scatter(early, EARLY, done, 0, "_early")
    finish_scatter(late, LATE, dx, 0, "_late")
    adam_layer(0)

    d_mod_flat = jnp.stack([jnp.stack(d_mod[l]) for l in range(depth)]).reshape(-1)
    d_norm_flat = jnp.stack([jnp.stack(d_norm[l]) for l in range(depth)]).reshape(-1)
    d_conv_flat = jnp.stack(d_conv).reshape(-1)
    vec = jnp.concatenate([d_mod_flat, d_norm_flat, d_conv_flat, d_rel.reshape(-1), d_final_g,
                           jnp.broadcast_to(loss_part, (128,))])
    pad = (-vec.size) % 1024
    vec = jnp.concatenate([vec, jnp.zeros((pad,), F32)]).reshape(-1, 128)
    parts = _all_gather(vec, "small_grads_all_gather").reshape(N_DEV, vec.shape[0], 128)
    tot = _sum_rows8(parts, "small_grads_sum").reshape(-1)
    o0 = 0
    g_ada_b = tot[o0:o0 + d_mod_flat.size].reshape(ada_b.shape)
    o0 += d_mod_flat.size
    g_norm_full = tot[o0:o0 + d_norm_flat.size].reshape(depth, 3, D)
    o0 += d_norm_flat.size
    g_conv_full = tot[o0:o0 + d_conv_flat.size].reshape(depth, CONV_K, D)
    o0 += d_conv_flat.size
    g_rel = tot[o0:o0 + rel_bias.size].reshape(rel_bias.shape)
    o0 += rel_bias.size
    g_final = tot[o0:o0 + D]
    o0 += D
    loss = tot[o0]
    g_norm = lax.dynamic_slice_in_dim(g_norm_full, me * dsh, dsh, axis=2)
    g_conv = lax.dynamic_slice_in_dim(g_conv_full, me * dsh, dsh, axis=2)

    dm_all = parts.reshape(N_DEV, -1)[:, :d_mod_flat.size].reshape(N_DEV, depth, N_DEV * n_ada)
    dm_cols = lax.dynamic_slice_in_dim(dm_all, me * n_ada, n_ada, axis=2)
    dm16 = jnp.concatenate([jnp.transpose(dm_cols, (1, 0, 2)), jnp.zeros((depth, N_DEV, n_ada), F32)], axis=1)
    g_ada_w = _ada_bwd(cs16, dm16, "ada_bwd")

    small = dict(ada_w=(ada_w, g_ada_w, m_ada_w, v_ada_w), ada_b=(ada_b, g_ada_b, m_ada_b, v_ada_b),
                 norm_g=(norm_g, g_norm, m_norm_g, v_norm_g), conv_w=(conv_w, g_conv, m_conv_w, v_conv_w),
                 rel_bias=(rel_bias, g_rel, m_rel_bias, v_rel_bias), final_g=(final_g, g_final, m_final_g, v_final_g))
    order = ("ada_w", "ada_b", "norm_g", "ffn_w_gate", "ffn_w_up", "ffn_w_down", "w_in", "conv_w", "w_conv_out",
             "w_attn_out", "w_o", "rel_bias", "final_g")
    res = {}
    for name in order:
        if name in big:
            shape = big[name][0].shape
            if big[name][4] == "t_shard":
                t_shape = shape[:-2] + (shape[-1], shape[-2])
                res[name] = tuple(jnp.swapaxes(t.reshape(t_shape), -1, -2) for t in big_out[name])
            else:
                res[name] = tuple(t.reshape(shape) for t in big_out[name])
        else:
            w_, g_, m_, v_ = small[name]
            res[name] = (g_,) + _adamw(w_, g_, m_, v_, "adamw_" + name)
    return (loss, grad_x, *[res[n][0] for n in order], *[res[n][1] for n in order],
            *[res[n][2] for n in order], *[res[n][3] for n in order])
```
